```python
import jax, jax.numpy as jnp
from jax import lax
import numpy as np

D_MODEL = 1024
BATCH = 8
SEQ = 4096
DEPTH = 1

CHUNK = 64
PLE_DIM = 256
MIX_WIDTH = D_MODEL
S5_WIDTH = MIX_WIDTH // 2
RWKV_WIDTH = MIX_WIDTH - S5_WIDTH
S5_GROUP = 16
S5_GROUPS = S5_WIDTH // S5_GROUP
S5_STATE = 64
RWKV_HEAD = 64
RWKV_HEADS = RWKV_WIDTH // RWKV_HEAD
DECAY_LORA = 64
AAA_LORA = 64
GATE_LORA = 128
SHIFT_COLS = 3 * RWKV_WIDTH + DECAY_LORA + AAA_LORA + GATE_LORA
IN_COLS = S5_WIDTH + SHIFT_COLS
FFN_HIDDEN = ((8 * D_MODEL + 3 * 256 - 1) // (3 * 256)) * 256
RMS_EPS = 1e-6
GN_EPS = 64e-5
L2_EPS = 1e-12

kernel_name = "hymba_s5_rwkv7_streaming_block"


def rms_norm(x, g):
    xf = x.astype(jnp.float32)
    y = xf * lax.rsqrt(jnp.mean(xf * xf, axis=-1, keepdims=True) + RMS_EPS)
    return (y * g.astype(jnp.float32)).astype(x.dtype)


def _linear_binop(e1, e2):
    a1, b1 = e1
    a2, b2 = e2
    return a2 * a1, a2 * b1 + b2


def s5_mixer(u, lam_re, lam_im, log_step, b_re, b_im, c_re, c_im, d_skip, glu_w, glu_b):
    f32 = jnp.float32
    bsz, seq, _ = u.shape
    n_chunks = seq // CHUNK
    lam = lax.complex(lam_re.astype(f32), lam_im.astype(f32))
    lam_dt = lam * jnp.exp(log_step.astype(f32))[:, None]
    lam_bar = jnp.exp(lam_dt)
    b_mat = lax.complex(b_re.astype(f32), b_im.astype(f32))
    b_bar = ((lam_bar - 1.0) / lam)[..., None] * b_mat
    c_mat = lax.complex(c_re.astype(f32), c_im.astype(f32))
    d_g = d_skip.astype(f32).reshape(S5_GROUPS, S5_GROUP)
    lam_pow = jnp.exp(lam_dt[None] * jnp.arange(1, CHUNK + 1, dtype=f32)[:, None, None])
    ug = u.astype(f32).reshape(bsz, n_chunks, CHUNK, S5_GROUPS, S5_GROUP)
    ug = jnp.moveaxis(ug, 1, 0)

    def chunk_step(carry, u_c):
        bu = jnp.einsum('gpc,btgc->btgp', b_bar, u_c.astype(jnp.complex64))
        a = jnp.broadcast_to(lam_bar, bu.shape)
        _, local = lax.associative_scan(_linear_binop, (a, bu), axis=1)
        states = local + lam_pow[None] * carry[:, None]
        y = jnp.einsum('gcp,btgp->btgc', c_mat, states).real + d_g * u_c
        return states[:, -1], y

    carry0 = jnp.zeros((bsz, S5_GROUPS, S5_STATE), jnp.complex64)
    _, ys = lax.scan(chunk_step, carry0, ug)
    y = jnp.moveaxis(ys, 0, 1).reshape(bsz, seq, S5_WIDTH)
    z = jax.nn.gelu(y)
    return z * jax.nn.sigmoid(z @ glu_w.astype(f32) + glu_b.astype(f32))


def wkv7_scan(r, w, k, v, a, b):
    bsz, seq, nh, n = r.shape
    n_chunks = seq // CHUNK

    def to_chunks(t):
        return t.reshape(bsz, n_chunks, CHUNK, nh, n).transpose(1, 2, 0, 3, 4)

    def step(s, inp):
        r_t, w_t, k_t, v_t, a_t, b_t = inp
        sa = jnp.einsum('bhvk,bhk->bhv', s, a_t)
        s = s * w_t[:, :, None, :] + sa[..., None] * b_t[:, :, None, :] + v_t[..., None] * k_t[:, :, None, :]
        return s, jnp.einsum('bhvk,bhk->bhv', s, r_t)

    def chunk_step(s, inp_c):
        return lax.scan(step, s, inp_c)

    s0 = jnp.zeros((bsz, nh, n, n), jnp.float32)
    _, ys = lax.scan(chunk_step, s0, tuple(to_chunks(t) for t in (r, w, k, v, a, b)))
    return ys.transpose(2, 0, 1, 3, 4).reshape(bsz, seq, nh, n)


def rwkv7_mixer(z, shift_mu, w0, w2, a0, a2, g2, k_k, k_a, r_k, ln_w, ln_b):
    f32 = jnp.float32
    bsz, seq, _ = z.shape
    z = z.astype(f32)
    prev = jnp.pad(z[:, :-1], ((0, 0), (1, 0), (0, 0)))
    zs = z + (prev - z) * shift_mu.astype(f32)
    rw = RWKV_WIDTH
    splits = [rw, 2 * rw, 3 * rw, 3 * rw + DECAY_LORA, 3 * rw + DECAY_LORA + AAA_LORA]
    r, k, v, wl, al, gl = jnp.split(zs, splits, axis=-1)
    w = -jax.nn.softplus(-(w0.astype(f32) + jnp.tanh(wl) @ w2.astype(f32))) - 0.5
    decay = jnp.exp(-jnp.exp(w))
    a = jax.nn.sigmoid(a0.astype(f32) + al @ a2.astype(f32))
    g = jax.nn.sigmoid(gl) @ g2.astype(f32)

    def heads(t):
        return t.reshape(bsz, seq, RWKV_HEADS, RWKV_HEAD)

    kk = heads(k * k_k.astype(f32))
    kk = kk / jnp.maximum(jnp.sqrt(jnp.sum(kk * kk, axis=-1, keepdims=True)), L2_EPS)
    k = k * (1.0 + (a - 1.0) * k_a.astype(f32))
    r_h, k_h, v_h, a_h = heads(r), heads(k), heads(v), heads(a)
    y = wkv7_scan(r_h, heads(decay), k_h, v_h, -kk, kk * a_h)
    mu = jnp.mean(y, axis=-1, keepdims=True)
    yc = y - mu
    yn = yc * lax.rsqrt(jnp.mean(yc * yc, axis=-1, keepdims=True) + GN_EPS)
    yn = yn.reshape(bsz, seq, rw) * ln_w.astype(f32) + ln_b.astype(f32)
    bonus = jnp.sum(r_h * k_h * r_k.astype(f32), axis=-1, keepdims=True) * v_h
    return (yn + bonus.reshape(bsz, seq, rw)) * g


def _fwd_setup_inputs(seed: int = 0) -> dict:
    key = jax.random.key(seed)
    ks = jax.random.split(key, 40)
    f32 = jnp.float32
    nrm = lambda k, s, sc: jax.random.normal(k, s, f32) * sc
    D, L = DEPTH, D_MODEL
    n_idx = jnp.arange(RWKV_WIDTH, dtype=f32) / (RWKV_WIDTH - 1)
    w0_base = -7.0 + 5.0 * n_idx ** 0.85 + 0.5
    lam_im_base = jnp.pi * jnp.arange(S5_STATE, dtype=f32)
    return {
        "x": nrm(ks[0], (BATCH, SEQ, D_MODEL), 1.0),
        "p": nrm(ks[1], (DEPTH, BATCH, SEQ, PLE_DIM), 1.0),
        "norm_mix": 1.0 + nrm(ks[2], (D, L), 0.02),
        "w_in": nrm(ks[3], (D, L, IN_COLS), L ** -0.5),
        "s5_lam_re": -0.5 + nrm(ks[4], (D, S5_GROUPS, S5_STATE), 0.01),
        "s5_lam_im": lam_im_base + nrm(ks[5], (D, S5_GROUPS, S5_STATE), 0.01),
        "s5_log_step": jax.random.uniform(ks[6], (D, S5_GROUPS), f32, np.log(1e-3), np.log(1e-1)),
        "s5_b_re": nrm(ks[7], (D, S5_GROUPS, S5_STATE, S5_GROUP), (2 * S5_GROUP) ** -0.5),
        "s5_b_im": nrm(ks[8], (D, S5_GROUPS, S5_STATE, S5_GROUP), (2 * S5_GROUP) ** -0.5),
        "s5_c_re": nrm(ks[9], (D, S5_GROUPS, S5_GROUP, S5_STATE), S5_STATE ** -0.5),
        "s5_c_im": nrm(ks[10], (D, S5_GROUPS, S5_GROUP, S5_STATE), S5_STATE ** -0.5),
        "s5_d": nrm(ks[11], (D, S5_WIDTH), 1.0),
        "s5_glu_w": nrm(ks[12], (D, S5_WIDTH, S5_WIDTH), S5_WIDTH ** -0.5),
        "s5_glu_b": nrm(ks[13], (D, S5_WIDTH), 0.01),
        "rw_shift_mu": jax.random.uniform(ks[14], (D, SHIFT_COLS), f32, 0.1, 0.9),
        "rw_w0": w0_base + nrm(ks[15], (D, RWKV_WIDTH), 0.1),
        "rw_w2": nrm(ks[16], (D, DECAY_LORA, RWKV_WIDTH), 0.1),
        "rw_a0": nrm(ks[17], (D, RWKV_WIDTH), 0.1),
        "rw_a2": nrm(ks[18], (D, AAA_LORA, RWKV_WIDTH), 0.1),
        "rw_g2": nrm(ks[19], (D, GATE_LORA, RWKV_WIDTH), GATE_LORA ** -0.5),
        "rw_k_k": 0.85 + nrm(ks[20], (D, RWKV_WIDTH), 0.02),
        "rw_k_a": 1.0 + nrm(ks[21], (D, RWKV_WIDTH), 0.02),
        "rw_r_k": -0.04 + nrm(ks[22], (D, RWKV_HEADS, RWKV_HEAD), 0.02),
        "rw_ln_w": 1.0 + nrm(ks[23], (D, RWKV_WIDTH), 0.02),
        "rw_ln_b": nrm(ks[24], (D, RWKV_WIDTH), 0.01),
        "w_out": nrm(ks[25], (D, MIX_WIDTH, L), MIX_WIDTH ** -0.5),
        "norm_ffn": 1.0 + nrm(ks[26], (D, L), 0.02),
        "ffn_w1": nrm(ks[27], (D, L, FFN_HIDDEN), L ** -0.5),
        "ffn_w3": nrm(ks[28], (D, L, FFN_HIDDEN), L ** -0.5),
        "ffn_w2": nrm(ks[29], (D, FFN_HIDDEN, L), FFN_HIDDEN ** -0.5),
        "norm_ple": 1.0 + nrm(ks[30], (D, L), 0.02),
        "ple_gate_w": nrm(ks[31], (D, L, L), L ** -0.5),
        "ple_up_w": nrm(ks[32], (D, PLE_DIM, L), PLE_DIM ** -0.5),
        "final_norm": 1.0 + nrm(ks[33], (L,), 0.02),
    }


def _fwd_reference(x, p, norm_mix, w_in, s5_lam_re, s5_lam_im, s5_log_step, s5_b_re, s5_b_im,
              s5_c_re, s5_c_im, s5_d, s5_glu_w, s5_glu_b, rw_shift_mu, rw_w0, rw_w2, rw_a0,
              rw_a2, rw_g2, rw_k_k, rw_k_a, rw_r_k, rw_ln_w, rw_ln_b, w_out, norm_ffn,
              ffn_w1, ffn_w3, ffn_w2, norm_ple, ple_gate_w, ple_up_w, final_norm):
    h = x
    for i in range(DEPTH):
        xn = rms_norm(h, norm_mix[i])
        proj = xn @ w_in[i]
        s5_out = s5_mixer(proj[..., :S5_WIDTH], s5_lam_re[i], s5_lam_im[i], s5_log_step[i],
                          s5_b_re[i], s5_b_im[i], s5_c_re[i], s5_c_im[i], s5_d[i],
                          s5_glu_w[i], s5_glu_b[i])
        rw_out = rwkv7_mixer(proj[..., S5_WIDTH:], rw_shift_mu[i], rw_w0[i], rw_w2[i], rw_a0[i],
                             rw_a2[i], rw_g2[i], rw_k_k[i], rw_k_a[i], rw_r_k[i],
                             rw_ln_w[i], rw_ln_b[i])
        mixed = jnp.concatenate([s5_out, rw_out], axis=-1).astype(h.dtype) @ w_out[i]
        h = h + mixed
        hn = rms_norm(h, norm_ffn[i])
        h = h + (jax.nn.silu(hn @ ffn_w1[i]) * (hn @ ffn_w3[i])) @ ffn_w2[i]
        gate = jax.nn.sigmoid(rms_norm(h, norm_ple[i]) @ ple_gate_w[i])
        h = h + gate * (p[i] @ ple_up_w[i])
    return rms_norm(h, final_norm)


import jax as _jax
import jax.numpy as _jnp

TWIN_FORMAT = 'train_step'
FWD_PARAMS = ['x', 'p', 'norm_mix', 'w_in', 's5_lam_re', 's5_lam_im', 's5_log_step', 's5_b_re', 's5_b_im', 's5_c_re', 's5_c_im', 's5_d', 's5_glu_w', 's5_glu_b', 'rw_shift_mu', 'rw_w0', 'rw_w2', 'rw_a0', 'rw_a2', 'rw_g2', 'rw_k_k', 'rw_k_a', 'rw_r_k', 'rw_ln_w', 'rw_ln_b', 'w_out', 'norm_ffn', 'ffn_w1', 'ffn_w3', 'ffn_w2', 'norm_ple', 'ple_gate_w', 'ple_up_w', 'final_norm']
TWIN_WEIGHTS = ['norm_mix', 'w_in', 's5_lam_re', 's5_lam_im', 's5_log_step', 's5_b_re', 's5_b_im', 's5_c_re', 's5_c_im', 's5_d', 's5_glu_w', 's5_glu_b', 'rw_shift_mu', 'rw_w0', 'rw_w2', 'rw_a0', 'rw_a2', 'rw_g2', 'rw_k_k', 'rw_k_a', 'rw_r_k', 'rw_ln_w', 'rw_ln_b', 'w_out', 'norm_ffn', 'ffn_w1', 'ffn_w3', 'ffn_w2', 'norm_ple', 'ple_gate_w', 'ple_up_w', 'final_norm']
TWIN_DIFF_INPUT = 'x'
TWIN_INPUTS = ['x', 'p', 'norm_mix', 'w_in', 's5_lam_re', 's5_lam_im', 's5_log_step', 's5_b_re', 's5_b_im', 's5_c_re', 's5_c_im', 's5_d', 's5_glu_w', 's5_glu_b', 'rw_shift_mu', 'rw_w0', 'rw_w2', 'rw_a0', 'rw_a2', 'rw_g2', 'rw_k_k', 'rw_k_a', 'rw_r_k', 'rw_ln_w', 'rw_ln_b', 'w_out', 'norm_ffn', 'ffn_w1', 'ffn_w3', 'ffn_w2', 'norm_ple', 'ple_gate_w', 'ple_up_w', 'final_norm', 'loss_target', 'm_norm_mix', 'm_w_in', 'm_s5_lam_re', 'm_s5_lam_im', 'm_s5_log_step', 'm_s5_b_re', 'm_s5_b_im', 'm_s5_c_re', 'm_s5_c_im', 'm_s5_d', 'm_s5_glu_w', 'm_s5_glu_b', 'm_rw_shift_mu', 'm_rw_w0', 'm_rw_w2', 'm_rw_a0', 'm_rw_a2', 'm_rw_g2', 'm_rw_k_k', 'm_rw_k_a', 'm_rw_r_k', 'm_rw_ln_w', 'm_rw_ln_b', 'm_w_out', 'm_norm_ffn', 'm_ffn_w1', 'm_ffn_w3', 'm_ffn_w2', 'm_norm_ple', 'm_ple_gate_w', 'm_ple_up_w', 'm_final_norm', 'v_norm_mix', 'v_w_in', 'v_s5_lam_re', 'v_s5_lam_im', 'v_s5_log_step', 'v_s5_b_re', 'v_s5_b_im', 'v_s5_c_re', 'v_s5_c_im', 'v_s5_d', 'v_s5_glu_w', 'v_s5_glu_b', 'v_rw_shift_mu', 'v_rw_w0', 'v_rw_w2', 'v_rw_a0', 'v_rw_a2', 'v_rw_g2', 'v_rw_k_k', 'v_rw_k_a', 'v_rw_r_k', 'v_rw_ln_w', 'v_rw_ln_b', 'v_w_out', 'v_norm_ffn', 'v_ffn_w1', 'v_ffn_w3', 'v_ffn_w2', 'v_norm_ple', 'v_ple_gate_w', 'v_ple_up_w', 'v_final_norm']
TWIN_OUTPUTS = ['loss', 'grad_x', 'grad_norm_mix', 'grad_w_in', 'grad_s5_lam_re', 'grad_s5_lam_im', 'grad_s5_log_step', 'grad_s5_b_re', 'grad_s5_b_im', 'grad_s5_c_re', 'grad_s5_c_im', 'grad_s5_d', 'grad_s5_glu_w', 'grad_s5_glu_b', 'grad_rw_shift_mu', 'grad_rw_w0', 'grad_rw_w2', 'grad_rw_a0', 'grad_rw_a2', 'grad_rw_g2', 'grad_rw_k_k', 'grad_rw_k_a', 'grad_rw_r_k', 'grad_rw_ln_w', 'grad_rw_ln_b', 'grad_w_out', 'grad_norm_ffn', 'grad_ffn_w1', 'grad_ffn_w3', 'grad_ffn_w2', 'grad_norm_ple', 'grad_ple_gate_w', 'grad_ple_up_w', 'grad_final_norm', 'delta_norm_mix', 'delta_w_in', 'delta_s5_lam_re', 'delta_s5_lam_im', 'delta_s5_log_step', 'delta_s5_b_re', 'delta_s5_b_im', 'delta_s5_c_re', 'delta_s5_c_im', 'delta_s5_d', 'delta_s5_glu_w', 'delta_s5_glu_b', 'delta_rw_shift_mu', 'delta_rw_w0', 'delta_rw_w2', 'delta_rw_a0', 'delta_rw_a2', 'delta_rw_g2', 'delta_rw_k_k', 'delta_rw_k_a', 'delta_rw_r_k', 'delta_rw_ln_w', 'delta_rw_ln_b', 'delta_w_out', 'delta_norm_ffn', 'delta_ffn_w1', 'delta_ffn_w3', 'delta_ffn_w2', 'delta_norm_ple', 'delta_ple_gate_w', 'delta_ple_up_w', 'delta_final_norm', 'new_m_norm_mix', 'new_m_w_in', 'new_m_s5_lam_re', 'new_m_s5_lam_im', 'new_m_s5_log_step', 'new_m_s5_b_re', 'new_m_s5_b_im', 'new_m_s5_c_re', 'new_m_s5_c_im', 'new_m_s5_d', 'new_m_s5_glu_w', 'new_m_s5_glu_b', 'new_m_rw_shift_mu', 'new_m_rw_w0', 'new_m_rw_w2', 'new_m_rw_a0', 'new_m_rw_a2', 'new_m_rw_g2', 'new_m_rw_k_k', 'new_m_rw_k_a', 'new_m_rw_r_k', 'new_m_rw_ln_w', 'new_m_rw_ln_b', 'new_m_w_out', 'new_m_norm_ffn', 'new_m_ffn_w1', 'new_m_ffn_w3', 'new_m_ffn_w2', 'new_m_norm_ple', 'new_m_ple_gate_w', 'new_m_ple_up_w', 'new_m_final_norm', 'new_v_norm_mix', 'new_v_w_in', 'new_v_s5_lam_re', 'new_v_s5_lam_im', 'new_v_s5_log_step', 'new_v_s5_b_re', 'new_v_s5_b_im', 'new_v_s5_c_re', 'new_v_s5_c_im', 'new_v_s5_d', 'new_v_s5_glu_w', 'new_v_s5_glu_b', 'new_v_rw_shift_mu', 'new_v_rw_w0', 'new_v_rw_w2', 'new_v_rw_a0', 'new_v_rw_a2', 'new_v_rw_g2', 'new_v_rw_k_k', 'new_v_rw_k_a', 'new_v_rw_r_k', 'new_v_rw_ln_w', 'new_v_rw_ln_b', 'new_v_w_out', 'new_v_norm_ffn', 'new_v_ffn_w1', 'new_v_ffn_w3', 'new_v_ffn_w2', 'new_v_norm_ple', 'new_v_ple_gate_w', 'new_v_ple_up_w', 'new_v_final_norm']
TWIN_LEAF_KINDS = {'loss': 'loss', 'grad_x': 'grad_x', 'grad_norm_mix': 'grad_w', 'grad_w_in': 'grad_w', 'grad_s5_lam_re': 'grad_w', 'grad_s5_lam_im': 'grad_w', 'grad_s5_log_step': 'grad_w', 'grad_s5_b_re': 'grad_w', 'grad_s5_b_im': 'grad_w', 'grad_s5_c_re': 'grad_w', 'grad_s5_c_im': 'grad_w', 'grad_s5_d': 'grad_w', 'grad_s5_glu_w': 'grad_w', 'grad_s5_glu_b': 'grad_w', 'grad_rw_shift_mu': 'grad_w', 'grad_rw_w0': 'grad_w', 'grad_rw_w2': 'grad_w', 'grad_rw_a0': 'grad_w', 'grad_rw_a2': 'grad_w', 'grad_rw_g2': 'grad_w', 'grad_rw_k_k': 'grad_w', 'grad_rw_k_a': 'grad_w', 'grad_rw_r_k': 'grad_w', 'grad_rw_ln_w': 'grad_w', 'grad_rw_ln_b': 'grad_w', 'grad_w_out': 'grad_w', 'grad_norm_ffn': 'grad_w', 'grad_ffn_w1': 'grad_w', 'grad_ffn_w3': 'grad_w', 'grad_ffn_w2': 'grad_w', 'grad_norm_ple': 'grad_w', 'grad_ple_gate_w': 'grad_w', 'grad_ple_up_w': 'grad_w', 'grad_final_norm': 'grad_w', 'delta_norm_mix': 'delta_w', 'delta_w_in': 'delta_w', 'delta_s5_lam_re': 'delta_w', 'delta_s5_lam_im': 'delta_w', 'delta_s5_log_step': 'delta_w', 'delta_s5_b_re': 'delta_w', 'delta_s5_b_im': 'delta_w', 'delta_s5_c_re': 'delta_w', 'delta_s5_c_im': 'delta_w', 'delta_s5_d': 'delta_w', 'delta_s5_glu_w': 'delta_w', 'delta_s5_glu_b': 'delta_w', 'delta_rw_shift_mu': 'delta_w', 'delta_rw_w0': 'delta_w', 'delta_rw_w2': 'delta_w', 'delta_rw_a0': 'delta_w', 'delta_rw_a2': 'delta_w', 'delta_rw_g2': 'delta_w', 'delta_rw_k_k': 'delta_w', 'delta_rw_k_a': 'delta_w', 'delta_rw_r_k': 'delta_w', 'delta_rw_ln_w': 'delta_w', 'delta_rw_ln_b': 'delta_w', 'delta_w_out': 'delta_w', 'delta_norm_ffn': 'delta_w', 'delta_ffn_w1': 'delta_w', 'delta_ffn_w3': 'delta_w', 'delta_ffn_w2': 'delta_w', 'delta_norm_ple': 'delta_w', 'delta_ple_gate_w': 'delta_w', 'delta_ple_up_w': 'delta_w', 'delta_final_norm': 'delta_w', 'new_m_norm_mix': 'new_m', 'new_m_w_in': 'new_m', 'new_m_s5_lam_re': 'new_m', 'new_m_s5_lam_im': 'new_m', 'new_m_s5_log_step': 'new_m', 'new_m_s5_b_re': 'new_m', 'new_m_s5_b_im': 'new_m', 'new_m_s5_c_re': 'new_m', 'new_m_s5_c_im': 'new_m', 'new_m_s5_d': 'new_m', 'new_m_s5_glu_w': 'new_m', 'new_m_s5_glu_b': 'new_m', 'new_m_rw_shift_mu': 'new_m', 'new_m_rw_w0': 'new_m', 'new_m_rw_w2': 'new_m', 'new_m_rw_a0': 'new_m', 'new_m_rw_a2': 'new_m', 'new_m_rw_g2': 'new_m', 'new_m_rw_k_k': 'new_m', 'new_m_rw_k_a': 'new_m', 'new_m_rw_r_k': 'new_m', 'new_m_rw_ln_w': 'new_m', 'new_m_rw_ln_b': 'new_m', 'new_m_w_out': 'new_m', 'new_m_norm_ffn': 'new_m', 'new_m_ffn_w1': 'new_m', 'new_m_ffn_w3': 'new_m', 'new_m_ffn_w2': 'new_m', 'new_m_norm_ple': 'new_m', 'new_m_ple_gate_w': 'new_m', 'new_m_ple_up_w': 'new_m', 'new_m_final_norm': 'new_m', 'new_v_norm_mix': 'new_v', 'new_v_w_in': 'new_v', 'new_v_s5_lam_re': 'new_v', 'new_v_s5_lam_im': 'new_v', 'new_v_s5_log_step': 'new_v', 'new_v_s5_b_re': 'new_v', 'new_v_s5_b_im': 'new_v', 'new_v_s5_c_re': 'new_v', 'new_v_s5_c_im': 'new_v', 'new_v_s5_d': 'new_v', 'new_v_s5_glu_w': 'new_v', 'new_v_s5_glu_b': 'new_v', 'new_v_rw_shift_mu': 'new_v', 'new_v_rw_w0': 'new_v', 'new_v_rw_w2': 'new_v', 'new_v_rw_a0': 'new_v', 'new_v_rw_a2': 'new_v', 'new_v_rw_g2': 'new_v', 'new_v_rw_k_k': 'new_v', 'new_v_rw_k_a': 'new_v', 'new_v_rw_r_k': 'new_v', 'new_v_rw_ln_w': 'new_v', 'new_v_rw_ln_b': 'new_v', 'new_v_w_out': 'new_v', 'new_v_norm_ffn': 'new_v', 'new_v_ffn_w1': 'new_v', 'new_v_ffn_w3': 'new_v', 'new_v_ffn_w2': 'new_v', 'new_v_norm_ple': 'new_v', 'new_v_ple_gate_w': 'new_v', 'new_v_ple_up_w': 'new_v', 'new_v_final_norm': 'new_v'}


def _forward(args):
    return _fwd_reference(*[args[k] for k in FWD_PARAMS])


def _output_shape():
    def fwd():
        inp = _fwd_setup_inputs(0)
        return _fwd_reference(*[inp[k] for k in FWD_PARAMS])
    out = _jax.eval_shape(fwd)
    return out.shape, out.dtype

N_MICROBATCH = 1
ADAM_LR = 0.001
ADAM_B1 = 0.9
ADAM_B2 = 0.999
ADAM_EPS = 1e-08
ADAM_WD = 0.01
ADAM_STEP = 10
PER_EXAMPLE_BATCH_AXIS = {'x': 0, 'p': 1, 'loss_target': 0}
SHARED_INPUTS = []
_WEIGHT_DTYPES = {'norm_mix': _jnp.float32, 'w_in': _jnp.float32, 's5_lam_re': _jnp.float32, 's5_lam_im': _jnp.float32, 's5_log_step': _jnp.float32, 's5_b_re': _jnp.float32, 's5_b_im': _jnp.float32, 's5_c_re': _jnp.float32, 's5_c_im': _jnp.float32, 's5_d': _jnp.float32, 's5_glu_w': _jnp.float32, 's5_glu_b': _jnp.float32, 'rw_shift_mu': _jnp.float32, 'rw_w0': _jnp.float32, 'rw_w2': _jnp.float32, 'rw_a0': _jnp.float32, 'rw_a2': _jnp.float32, 'rw_g2': _jnp.float32, 'rw_k_k': _jnp.float32, 'rw_k_a': _jnp.float32, 'rw_r_k': _jnp.float32, 'rw_ln_w': _jnp.float32, 'rw_ln_b': _jnp.float32, 'w_out': _jnp.float32, 'norm_ffn': _jnp.float32, 'ffn_w1': _jnp.float32, 'ffn_w3': _jnp.float32, 'ffn_w2': _jnp.float32, 'norm_ple': _jnp.float32, 'ple_gate_w': _jnp.float32, 'ple_up_w': _jnp.float32, 'final_norm': _jnp.float32}
MOMENT_SCALE = {'norm_mix': 1.269039e-01, 'w_in': 8.335880e-02, 's5_lam_re': 4.444665e-03, 's5_lam_im': 4.839484e-03, 's5_log_step': 3.504952e+00, 's5_b_re': 2.971124e-03, 's5_b_im': 2.986339e-03, 's5_c_re': 4.128574e-03, 's5_c_im': 4.169783e-03, 's5_d': 6.968526e-02, 's5_glu_w': 1.824750e-02, 's5_glu_b': 2.764561e-02, 'rw_shift_mu': 1.483781e-01, 'rw_w0': 4.500601e-02, 'rw_w2': 6.348312e-03, 'rw_a0': 3.749734e-02, 'rw_a2': 3.122522e-02, 'rw_g2': 8.992165e-02, 'rw_k_k': 8.465353e-02, 'rw_k_a': 1.005463e-01, 'rw_r_k': 1.698697e-01, 'rw_ln_w': 9.568733e-02, 'rw_ln_b': 8.988083e-02, 'w_out': 7.485096e-02, 'norm_ffn': 1.134903e-01, 'ffn_w1': 4.916741e-02, 'ffn_w3': 4.748856e-02, 'ffn_w2': 7.888661e-02, 'norm_ple': 2.906914e-02, 'ple_gate_w': 2.791297e-02, 'ple_up_w': 7.096902e-02, 'final_norm': 3.201700e+01}


def _to_microbatches(a, axis):
    t = _jnp.moveaxis(a, axis, 0)
    t = t.reshape((N_MICROBATCH, t.shape[0] // N_MICROBATCH) + t.shape[1:])
    return _jnp.moveaxis(t, 1, axis + 1)


def setup_inputs(seed: int = 0) -> dict:
    inp = _fwd_setup_inputs(seed)
    key = _jax.random.fold_in(_jax.random.key(seed), 7919)
    shape, _ = _output_shape()
    out = dict(inp)
    out["loss_target"] = _jax.random.normal(_jax.random.fold_in(key, 0), shape, _jnp.float32)
    for i, name in enumerate(TWIN_WEIGHTS):
        w = inp[name].astype(_jnp.float32)
        if MOMENT_SCALE is None:
            s = _jnp.sqrt(_jnp.mean(_jnp.square(w)) + 1e-30)
        else:
            s = MOMENT_SCALE[name]
        km, kv = _jax.random.split(_jax.random.fold_in(key, i + 1))
        out[name] = w
        out["m_" + name] = s * _jax.random.normal(km, w.shape, _jnp.float32)
        out["v_" + name] = (s * s) * _jax.random.uniform(kv, w.shape, _jnp.float32, 0.5, 1.5)
    if N_MICROBATCH > 1:
        for name, axis in PER_EXAMPLE_BATCH_AXIS.items():
            out[name] = _to_microbatches(out[name], axis)
    return {'x': out['x'], 'p': out['p'], 'norm_mix': out['norm_mix'], 'w_in': out['w_in'], 's5_lam_re': out['s5_lam_re'], 's5_lam_im': out['s5_lam_im'], 's5_log_step': out['s5_log_step'], 's5_b_re': out['s5_b_re'], 's5_b_im': out['s5_b_im'], 's5_c_re': out['s5_c_re'], 's5_c_im': out['s5_c_im'], 's5_d': out['s5_d'], 's5_glu_w': out['s5_glu_w'], 's5_glu_b': out['s5_glu_b'], 'rw_shift_mu': out['rw_shift_mu'], 'rw_w0': out['rw_w0'], 'rw_w2': out['rw_w2'], 'rw_a0': out['rw_a0'], 'rw_a2': out['rw_a2'], 'rw_g2': out['rw_g2'], 'rw_k_k': out['rw_k_k'], 'rw_k_a': out['rw_k_a'], 'rw_r_k': out['rw_r_k'], 'rw_ln_w': out['rw_ln_w'], 'rw_ln_b': out['rw_ln_b'], 'w_out': out['w_out'], 'norm_ffn': out['norm_ffn'], 'ffn_w1': out['ffn_w1'], 'ffn_w3': out['ffn_w3'], 'ffn_w2': out['ffn_w2'], 'norm_ple': out['norm_ple'], 'ple_gate_w': out['ple_gate_w'], 'ple_up_w': out['ple_up_w'], 'final_norm': out['final_norm'], 'loss_target': out['loss_target'], 'm_norm_mix': out['m_norm_mix'], 'm_w_in': out['m_w_in'], 'm_s5_lam_re': out['m_s5_lam_re'], 'm_s5_lam_im': out['m_s5_lam_im'], 'm_s5_log_step': out['m_s5_log_step'], 'm_s5_b_re': out['m_s5_b_re'], 'm_s5_b_im': out['m_s5_b_im'], 'm_s5_c_re': out['m_s5_c_re'], 'm_s5_c_im': out['m_s5_c_im'], 'm_s5_d': out['m_s5_d'], 'm_s5_glu_w': out['m_s5_glu_w'], 'm_s5_glu_b': out['m_s5_glu_b'], 'm_rw_shift_mu': out['m_rw_shift_mu'], 'm_rw_w0': out['m_rw_w0'], 'm_rw_w2': out['m_rw_w2'], 'm_rw_a0': out['m_rw_a0'], 'm_rw_a2': out['m_rw_a2'], 'm_rw_g2': out['m_rw_g2'], 'm_rw_k_k': out['m_rw_k_k'], 'm_rw_k_a': out['m_rw_k_a'], 'm_rw_r_k': out['m_rw_r_k'], 'm_rw_ln_w': out['m_rw_ln_w'], 'm_rw_ln_b': out['m_rw_ln_b'], 'm_w_out': out['m_w_out'], 'm_norm_ffn': out['m_norm_ffn'], 'm_ffn_w1': out['m_ffn_w1'], 'm_ffn_w3': out['m_ffn_w3'], 'm_ffn_w2': out['m_ffn_w2'], 'm_norm_ple': out['m_norm_ple'], 'm_ple_gate_w': out['m_ple_gate_w'], 'm_ple_up_w': out['m_ple_up_w'], 'm_final_norm': out['m_final_norm'], 'v_norm_mix': out['v_norm_mix'], 'v_w_in': out['v_w_in'], 'v_s5_lam_re': out['v_s5_lam_re'], 'v_s5_lam_im': out['v_s5_lam_im'], 'v_s5_log_step': out['v_s5_log_step'], 'v_s5_b_re': out['v_s5_b_re'], 'v_s5_b_im': out['v_s5_b_im'], 'v_s5_c_re': out['v_s5_c_re'], 'v_s5_c_im': out['v_s5_c_im'], 'v_s5_d': out['v_s5_d'], 'v_s5_glu_w': out['v_s5_glu_w'], 'v_s5_glu_b': out['v_s5_glu_b'], 'v_rw_shift_mu': out['v_rw_shift_mu'], 'v_rw_w0': out['v_rw_w0'], 'v_rw_w2': out['v_rw_w2'], 'v_rw_a0': out['v_rw_a0'], 'v_rw_a2': out['v_rw_a2'], 'v_rw_g2': out['v_rw_g2'], 'v_rw_k_k': out['v_rw_k_k'], 'v_rw_k_a': out['v_rw_k_a'], 'v_rw_r_k': out['v_rw_r_k'], 'v_rw_ln_w': out['v_rw_ln_w'], 'v_rw_ln_b': out['v_rw_ln_b'], 'v_w_out': out['v_w_out'], 'v_norm_ffn': out['v_norm_ffn'], 'v_ffn_w1': out['v_ffn_w1'], 'v_ffn_w3': out['v_ffn_w3'], 'v_ffn_w2': out['v_ffn_w2'], 'v_norm_ple': out['v_norm_ple'], 'v_ple_gate_w': out['v_ple_gate_w'], 'v_ple_up_w': out['v_ple_up_w'], 'v_final_norm': out['v_final_norm']}


def _loss(weights, diff, rest, loss_target):
    with _jax.named_scope("forward"):
        args = {**rest, TWIN_DIFF_INPUT: diff, **{k: w.astype(_WEIGHT_DTYPES[k]) for k, w in weights.items()}}
        y = _forward(args)
    with _jax.named_scope("loss_head"):
        err = _jnp.square(y.astype(_jnp.float32) - loss_target)
        return 0.5 * _jnp.sum(_jnp.mean(err, axis=-1)) if err.ndim else 0.5 * err


def _adamw(w, g, m, v):
    m = ADAM_B1 * m + (1.0 - ADAM_B1) * g
    v = ADAM_B2 * v + (1.0 - ADAM_B2) * _jnp.square(g)
    m_hat = m / (1.0 - ADAM_B1 ** ADAM_STEP)
    v_hat = v / (1.0 - ADAM_B2 ** ADAM_STEP)
    delta = -ADAM_LR * (m_hat / (_jnp.sqrt(v_hat) + ADAM_EPS) + ADAM_WD * w)
    return delta, m, v


def reference(x, p, norm_mix, w_in, s5_lam_re, s5_lam_im, s5_log_step, s5_b_re, s5_b_im, s5_c_re, s5_c_im, s5_d, s5_glu_w, s5_glu_b, rw_shift_mu, rw_w0, rw_w2, rw_a0, rw_a2, rw_g2, rw_k_k, rw_k_a, rw_r_k, rw_ln_w, rw_ln_b, w_out, norm_ffn, ffn_w1, ffn_w3, ffn_w2, norm_ple, ple_gate_w, ple_up_w, final_norm, loss_target, m_norm_mix, m_w_in, m_s5_lam_re, m_s5_lam_im, m_s5_log_step, m_s5_b_re, m_s5_b_im, m_s5_c_re, m_s5_c_im, m_s5_d, m_s5_glu_w, m_s5_glu_b, m_rw_shift_mu, m_rw_w0, m_rw_w2, m_rw_a0, m_rw_a2, m_rw_g2, m_rw_k_k, m_rw_k_a, m_rw_r_k, m_rw_ln_w, m_rw_ln_b, m_w_out, m_norm_ffn, m_ffn_w1, m_ffn_w3, m_ffn_w2, m_norm_ple, m_ple_gate_w, m_ple_up_w, m_final_norm, v_norm_mix, v_w_in, v_s5_lam_re, v_s5_lam_im, v_s5_log_step, v_s5_b_re, v_s5_b_im, v_s5_c_re, v_s5_c_im, v_s5_d, v_s5_glu_w, v_s5_glu_b, v_rw_shift_mu, v_rw_w0, v_rw_w2, v_rw_a0, v_rw_a2, v_rw_g2, v_rw_k_k, v_rw_k_a, v_rw_r_k, v_rw_ln_w, v_rw_ln_b, v_w_out, v_norm_ffn, v_ffn_w1, v_ffn_w3, v_ffn_w2, v_norm_ple, v_ple_gate_w, v_ple_up_w, v_final_norm):
    given = dict(x=x, p=p, norm_mix=norm_mix, w_in=w_in, s5_lam_re=s5_lam_re, s5_lam_im=s5_lam_im, s5_log_step=s5_log_step, s5_b_re=s5_b_re, s5_b_im=s5_b_im, s5_c_re=s5_c_re, s5_c_im=s5_c_im, s5_d=s5_d, s5_glu_w=s5_glu_w, s5_glu_b=s5_glu_b, rw_shift_mu=rw_shift_mu, rw_w0=rw_w0, rw_w2=rw_w2, rw_a0=rw_a0, rw_a2=rw_a2, rw_g2=rw_g2, rw_k_k=rw_k_k, rw_k_a=rw_k_a, rw_r_k=rw_r_k, rw_ln_w=rw_ln_w, rw_ln_b=rw_ln_b, w_out=w_out, norm_ffn=norm_ffn, ffn_w1=ffn_w1, ffn_w3=ffn_w3, ffn_w2=ffn_w2, norm_ple=norm_ple, ple_gate_w=ple_gate_w, ple_up_w=ple_up_w, final_norm=final_norm, loss_target=loss_target, m_norm_mix=m_norm_mix, m_w_in=m_w_in, m_s5_lam_re=m_s5_lam_re, m_s5_lam_im=m_s5_lam_im, m_s5_log_step=m_s5_log_step, m_s5_b_re=m_s5_b_re, m_s5_b_im=m_s5_b_im, m_s5_c_re=m_s5_c_re, m_s5_c_im=m_s5_c_im, m_s5_d=m_s5_d, m_s5_glu_w=m_s5_glu_w, m_s5_glu_b=m_s5_glu_b, m_rw_shift_mu=m_rw_shift_mu, m_rw_w0=m_rw_w0, m_rw_w2=m_rw_w2, m_rw_a0=m_rw_a0, m_rw_a2=m_rw_a2, m_rw_g2=m_rw_g2, m_rw_k_k=m_rw_k_k, m_rw_k_a=m_rw_k_a, m_rw_r_k=m_rw_r_k, m_rw_ln_w=m_rw_ln_w, m_rw_ln_b=m_rw_ln_b, m_w_out=m_w_out, m_norm_ffn=m_norm_ffn, m_ffn_w1=m_ffn_w1, m_ffn_w3=m_ffn_w3, m_ffn_w2=m_ffn_w2, m_norm_ple=m_norm_ple, m_ple_gate_w=m_ple_gate_w, m_ple_up_w=m_ple_up_w, m_final_norm=m_final_norm, v_norm_mix=v_norm_mix, v_w_in=v_w_in, v_s5_lam_re=v_s5_lam_re, v_s5_lam_im=v_s5_lam_im, v_s5_log_step=v_s5_log_step, v_s5_b_re=v_s5_b_re, v_s5_b_im=v_s5_b_im, v_s5_c_re=v_s5_c_re, v_s5_c_im=v_s5_c_im, v_s5_d=v_s5_d, v_s5_glu_w=v_s5_glu_w, v_s5_glu_b=v_s5_glu_b, v_rw_shift_mu=v_rw_shift_mu, v_rw_w0=v_rw_w0, v_rw_w2=v_rw_w2, v_rw_a0=v_rw_a0, v_rw_a2=v_rw_a2, v_rw_g2=v_rw_g2, v_rw_k_k=v_rw_k_k, v_rw_k_a=v_rw_k_a, v_rw_r_k=v_rw_r_k, v_rw_ln_w=v_rw_ln_w, v_rw_ln_b=v_rw_ln_b, v_w_out=v_w_out, v_norm_ffn=v_norm_ffn, v_ffn_w1=v_ffn_w1, v_ffn_w3=v_ffn_w3, v_ffn_w2=v_ffn_w2, v_norm_ple=v_norm_ple, v_ple_gate_w=v_ple_gate_w, v_ple_up_w=v_ple_up_w, v_final_norm=v_final_norm)
    weights = {n: given[n] for n in TWIN_WEIGHTS}
    shared = {n: given[n] for n in SHARED_INPUTS}
    per_example = {n: given[n] for n in ['x', 'p']}
    grad_fn = _jax.value_and_grad(_loss, argnums=(0, 1))

    def one_microbatch(ex, loss_target):
        ex = dict(ex)
        diff = ex.pop(TWIN_DIFF_INPUT)
        return grad_fn(weights, diff, {**shared, **ex}, loss_target)

    if N_MICROBATCH == 1:
        loss, (grad_w, grad_x) = one_microbatch(per_example, given["loss_target"])
    else:
        def body(carry, xs):
            loss_sum, grad_sum = carry
            l_k, (gw_k, gx_k) = one_microbatch(xs[0], xs[1])
            with _jax.named_scope("update"):
                return (loss_sum + l_k, _jax.tree.map(_jnp.add, grad_sum, gw_k)), gx_k

        init = (_jnp.zeros((), _jnp.float32), _jax.tree.map(_jnp.zeros_like, weights))
        (loss, grad_w), grad_x = _jax.lax.scan(body, init, (per_example, given["loss_target"]))
    with _jax.named_scope("update"):
        delta_w, new_m, new_v = {}, {}, {}
        for n in TWIN_WEIGHTS:
            delta_w[n], new_m[n], new_v[n] = _adamw(weights[n], grad_w[n], given["m_" + n], given["v_" + n])
    return (loss, grad_x, *[grad_w[n] for n in TWIN_WEIGHTS], *[delta_w[n] for n in TWIN_WEIGHTS],
            *[new_m[n] for n in TWIN_WEIGHTS], *[new_v[n] for n in TWIN_WEIGHTS])
```

```python
import functools
import math

import jax
import jax.numpy as jnp
from jax import lax
from jax.experimental import pallas as pl
from jax.experimental.pallas import tpu as pltpu

F32 = jnp.float32
BF16 = jnp.bfloat16
HIGHEST = lax.Precision.HIGHEST
MESH = pl.DeviceIdType.MESH

S5_WIDTH = 512
RWKV_WIDTH = 512
S5_GROUP = 16
S5_GROUPS = 32
S5_STATE = 64
S5_LANES = S5_GROUPS * S5_STATE
HEAD = 64
HEADS = 8
DECAY_LORA = 64
AAA_LORA = 64
GATE_LORA = 128
FFN_HIDDEN = 2816
RMS_EPS = 1e-6
GN_EPS = 64e-5
L2_EPS = 1e-12
ADAM_LR = 0.001
ADAM_B1 = 0.9
ADAM_B2 = 0.999
ADAM_EPS = 1e-08
ADAM_WD = 0.01
ADAM_STEP = 10

WKV_CHUNK = 64
VMEM_LIMIT_BYTES = 48 * 1024 * 1024
LANE = 128
PACK_COLS = 1024
SMALL_ROWS = 112

WEIGHT_NAMES = ['norm_mix', 'w_in', 's5_lam_re', 's5_lam_im', 's5_log_step', 's5_b_re', 's5_b_im', 's5_c_re',
                's5_c_im', 's5_d', 's5_glu_w', 's5_glu_b', 'rw_shift_mu', 'rw_w0', 'rw_w2', 'rw_a0', 'rw_a2',
                'rw_g2', 'rw_k_k', 'rw_k_a', 'rw_r_k', 'rw_ln_w', 'rw_ln_b', 'w_out', 'norm_ffn', 'ffn_w1',
                'ffn_w3', 'ffn_w2', 'norm_ple', 'ple_gate_w', 'ple_up_w', 'final_norm']
SHARDED = {'w_in': 1, 's5_glu_w': 0, 'rw_w2': 1, 'rw_a2': 1, 'rw_g2': 1, 'w_out': 0, 'ffn_w1': 1, 'ffn_w3': 1,
           'ffn_w2': 0, 'ple_gate_w': 0, 'ple_up_w': 1}
SHARDED_NAMES = [n for n in WEIGHT_NAMES if n in SHARDED]
SMALL_NAMES = [n for n in WEIGHT_NAMES if n not in SHARDED]


def _params(sem=None):
    return pltpu.CompilerParams(dimension_semantics=sem, vmem_limit_bytes=VMEM_LIMIT_BYTES)


def _tile(n, target):
    best = None
    for d in range(LANE, min(n, target) + 1, LANE):
        if n % d == 0:
            best = d
    return n if best is None else best


_NN = (((1,), (0,)), ((), ()))
_NT = (((1,), (1,)), ((), ()))
_TN = (((0,), (0,)), ((), ()))


def _dg(a, b, dims, precise):
    if precise:
        return lax.dot_general(a.astype(F32), b.astype(F32), dims, precision=HIGHEST, preferred_element_type=F32)
    return lax.dot_general(a.astype(BF16), b.astype(BF16), dims, preferred_element_type=F32)


@jax.custom_vjp
def _bdot(x, w):
    return _dg(x, w, _NN, False)


def _bdot_fwd(x, w):
    return _dg(x, w, _NN, False), (x, w)


def _bdot_bwd(res, g):
    x, w = res
    return _dg(g, w, _NT, False), _dg(x, g, _TN, False)


_bdot.defvjp(_bdot_fwd, _bdot_bwd)


def _fdot(a, b, dims=_NN):
    return _dg(a, b, dims, True)


@jax.custom_vjp
def _shift_down(z):
    return pltpu.roll(z, 1, 0)


def _shift_down_fwd(z):
    return pltpu.roll(z, 1, 0), None


def _shift_down_bwd(_, g):
    return (pltpu.roll(g, g.shape[0] - 1, 0),)


_shift_down.defvjp(_shift_down_fwd, _shift_down_bwd)


def _head_ones():
    r = lax.broadcasted_iota(jnp.int32, (RWKV_WIDTH, RWKV_WIDTH), 0) // HEAD
    c = lax.broadcasted_iota(jnp.int32, (RWKV_WIDTH, RWKV_WIDTH), 1) // HEAD
    return (r == c).astype(F32)


def _mm(name, a, b, mode, out_dtype=F32, precise=False, tm=512, tn=512, tk=1024):
    if mode == 'nn':
        (m, k), (_, n) = a.shape, b.shape
    elif mode == 'nt':
        (m, k), (n, _) = a.shape, b.shape
    else:
        (k, m), (_, n) = a.shape, b.shape
    tm, tn, tk = _tile(m, tm), _tile(n, tn), _tile(k, tk)
    nk = k // tk
    dims = {'nn': _NN, 'nt': _NT, 'tn': _TN}[mode]

    def body(a_ref, b_ref, o_ref, acc_ref):
        kk = pl.program_id(2)

        @pl.when(kk == 0)
        def _():
            acc_ref[...] = jnp.zeros_like(acc_ref)

        acc_ref[...] += _dg(a_ref[...], b_ref[...], dims, precise)

        @pl.when(kk == nk - 1)
        def _():
            o_ref[...] = acc_ref[...].astype(o_ref.dtype)

    if mode == 'tn':
        a_spec = pl.BlockSpec((tk, tm), lambda i, j, l: (l, i))
    else:
        a_spec = pl.BlockSpec((tm, tk), lambda i, j, l: (i, l))
    if mode == 'nt':
        b_spec = pl.BlockSpec((tn, tk), lambda i, j, l: (j, l))
    else:
        b_spec = pl.BlockSpec((tk, tn), lambda i, j, l: (l, j))
    return pl.pallas_call(
        body, name=name,
        out_shape=jax.ShapeDtypeStruct((m, n), out_dtype),
        grid=(m // tm, n // tn, nk),
        in_specs=[a_spec, b_spec],
        out_specs=pl.BlockSpec((tm, tn), lambda i, j, l: (i, j)),
        scratch_shapes=[pltpu.VMEM((tm, tn), F32)],
        compiler_params=_params(("parallel", "parallel", "arbitrary")),
    )(a, b)


def _full_spec(p):
    nd = p.ndim
    return pl.BlockSpec(p.shape, lambda i, nd=nd: (0,) * nd)


def _tok_fwd(name, fn, toks, params, outs, tm):
    t = toks[0].shape[0]
    nt, npar = len(toks), len(params)

    def body(*refs):
        tv = [r[...].astype(F32) for r in refs[:nt]]
        pv = [r[...].astype(F32) for r in refs[nt:nt + npar]]
        res = fn(*tv, *pv)
        for r, v in zip(refs[nt + npar:], res):
            r[...] = v.astype(r.dtype)

    return pl.pallas_call(
        body, name=name,
        out_shape=[jax.ShapeDtypeStruct((t, w), d) for w, d in outs],
        grid=(t // tm,),
        in_specs=[pl.BlockSpec((tm, a.shape[1]), lambda i: (i, 0)) for a in toks] + [_full_spec(p) for p in params],
        out_specs=[pl.BlockSpec((tm, w), lambda i: (i, 0)) for w, _ in outs],
        compiler_params=_params(("parallel",)),
    )(*toks, *params)


def _tok_bwd(name, fn, toks, params, cots, dtok, dpar, tm, acc_out=None, add_to=None):
    t = toks[0].shape[0]
    nt, npar = len(toks), len(params)
    cot_arrays = [c for c in cots if c is not None]
    ncot = len(cot_arrays)
    extra = [] if add_to is None else [add_to[1]]
    dtok_idx = [i for i, d in enumerate(dtok) if d is not None]
    dpar_idx = [i for i, d in enumerate(dpar) if d]

    def body(*refs):
        pos = 0
        tin = refs[pos:pos + nt]; pos += nt
        pin = refs[pos:pos + npar]; pos += npar
        cin = refs[pos:pos + ncot]; pos += ncot
        ein = refs[pos:pos + len(extra)]; pos += len(extra)
        dto = refs[pos:pos + len(dtok_idx)]; pos += len(dtok_idx)
        dpo = refs[pos:pos + len(dpar_idx)]; pos += len(dpar_idx)
        acc = refs[pos] if acc_out is not None else None
        first = pl.program_id(0) == 0

        tv = [r[...].astype(F32) for r in tin]
        pv = [r[...].astype(F32) for r in pin]
        res, vjp = jax.vjp(fn, *tv, *pv)
        cit = iter(cin)
        cs = tuple(jnp.ones_like(o) if c is None else next(cit)[...].astype(F32) for c, o in zip(cots, res))
        g = vjp(cs)
        for r, i in zip(dto, dtok_idx):
            v = g[i]
            if add_to is not None and add_to[0] == i:
                v = v + ein[0][...].astype(F32)
            r[...] = v.astype(r.dtype)

        @pl.when(first)
        def _():
            for r in dpo:
                r[...] = jnp.zeros_like(r)
            if acc is not None:
                acc[...] = jnp.zeros_like(acc)

        for r, i in zip(dpo, dpar_idx):
            r[...] += g[nt + i]
        if acc is not None:
            acc[...] += res[acc_out]

    out_shape = [jax.ShapeDtypeStruct(toks[i].shape, dtok[i]) for i in dtok_idx]
    out_shape += [jax.ShapeDtypeStruct(params[i].shape, F32) for i in dpar_idx]
    out_specs = [pl.BlockSpec((tm, toks[i].shape[1]), lambda i_: (i_, 0)) for i in dtok_idx]
    out_specs += [_full_spec(params[i]) for i in dpar_idx]
    if acc_out is not None:
        out_shape.append(jax.ShapeDtypeStruct((1, 1), F32))
        out_specs.append(pl.BlockSpec((1, 1), lambda i_: (0, 0)))
    tok_spec = lambda a: pl.BlockSpec((tm, a.shape[1]), lambda i_: (i_, 0))
    return pl.pallas_call(
        body, name=name,
        out_shape=out_shape,
        grid=(t // tm,),
        in_specs=[tok_spec(a) for a in toks] + [_full_spec(p) for p in params]
        + [tok_spec(c) for c in cot_arrays] + [tok_spec(e) for e in extra],
        out_specs=out_specs,
        compiler_params=_params(("arbitrary",)),
    )(*toks, *params, *cot_arrays, *extra)


def _small_fwd(name, fn, ins, outs):
    n = len(ins)

    def body(*refs):
        res = fn(*[r[...] for r in refs[:n]])
        for r, v in zip(refs[n:], res):
            r[...] = v.astype(r.dtype)

    return pl.pallas_call(
        body, name=name,
        out_shape=[jax.ShapeDtypeStruct(s, d) for s, d in outs],
        compiler_params=_params(),
    )(*ins)


def _small_bwd(name, fn, ins, cots):
    n = len(ins)

    def body(*refs):
        _, vjp = jax.vjp(fn, *[r[...] for r in refs[:n]])
        g = vjp(tuple(r[...] for r in refs[n:n + len(cots)]))
        for r, v in zip(refs[n + len(cots):], g):
            r[...] = v

    return pl.pallas_call(
        body, name=name,
        out_shape=[jax.ShapeDtypeStruct(a.shape, F32) for a in ins],
        compiler_params=_params(),
    )(*ins, *cots)


def _rms(x, g):
    return x * lax.rsqrt(jnp.mean(x * x, axis=-1, keepdims=True) + RMS_EPS) * g


def _f_norm_in(x, g):
    return (_rms(x, g),)


def _f_mix_res(x, mixed, g):
    h1 = x + mixed
    return h1, _rms(h1, g)


def _f_ffn_act(a13):
    a1, a3 = a13[:, :FFN_HIDDEN], a13[:, FFN_HIDDEN:]
    return (jax.nn.silu(a1) * a3,)


def _f_ffn_res(h1, ffo, g):
    h2 = h1 + ffo
    return h2, _rms(h2, g)


def _f_loss(h2, gpre, pu, target, g):
    h3 = h2 + jax.nn.sigmoid(gpre) * pu
    y = _rms(h3, g)
    err = jnp.square(y - target)
    return (0.5 * jnp.sum(jnp.mean(err, axis=-1, keepdims=True), axis=0, keepdims=True),)


def _f_s5_post(ypre, u, d, glu_w, glu_b):
    z = jax.nn.gelu(ypre + u * d)
    return (z * jax.nn.sigmoid(_bdot(z, glu_w) + glu_b),)


def _softplus(x):
    return jnp.maximum(x, 0.0) + jnp.log(1.0 + jnp.exp(-jnp.abs(x)))


def _f_rw_pre(z, carry, shift_mu, w0, w2, a0, a2, g2, k_k, k_a):
    rw = RWKV_WIDTH
    first_row = lax.broadcasted_iota(jnp.int32, z.shape, 0) == 0
    prev = jnp.where(first_row, carry, _shift_down(z))
    zs = z + (prev - z) * shift_mu
    o1, o2 = 3 * rw + DECAY_LORA, 3 * rw + DECAY_LORA + AAA_LORA
    r, k, v = zs[:, :rw], zs[:, rw:2 * rw], zs[:, 2 * rw:3 * rw]
    wl, al, gl = zs[:, 3 * rw:o1], zs[:, o1:o2], zs[:, o2:]
    w = -_softplus(-(w0 + _bdot(jnp.tanh(wl), w2))) - 0.5
    log_decay = -jnp.exp(w)
    a = jax.nn.sigmoid(a0 + _bdot(al, a2))
    g = _bdot(jax.nn.sigmoid(gl), g2)
    kk = k * k_k
    norm = jnp.sqrt(_fdot(kk * kk, _head_ones()))
    kk = kk / jnp.maximum(norm, L2_EPS)
    kp = k * (1.0 + (a - 1.0) * k_a)
    return r, log_decay, kp, v, -kk, kk * a, g


def _f_rw_post(y, r, kp, v, g, ln_w, ln_b, r_k):
    ones = _head_ones()
    yc = y - _fdot(y, ones) * (1.0 / HEAD)
    var = _fdot(yc * yc, ones) * (1.0 / HEAD)
    yn = yc * lax.rsqrt(var + GN_EPS) * ln_w + ln_b
    bonus = _fdot(r * kp * r_k, ones) * v
    return ((yn + bonus) * g,)


def _f_s5_lam(lam_re, lam_im, log_step):
    step = jnp.exp(log_step)
    dr, di = lam_re * step, lam_im * step
    e = jnp.exp(dr)
    lbr, lbi = e * jnp.cos(di), e * jnp.sin(di)
    nr, ni = lbr - 1.0, lbi
    den = lam_re * lam_re + lam_im * lam_im
    return lbr, lbi, (nr * lam_re + ni * lam_im) / den, (ni * lam_re - nr * lam_im) / den


def _f_s5_build(coef_r, coef_i, btr, bti, ctr, cti):
    bbr = coef_r * btr - coef_i * bti
    bbi = coef_r * bti + coef_i * btr
    rows = lax.broadcasted_iota(jnp.int32, (S5_WIDTH, S5_LANES), 0) // S5_GROUP
    cols = lax.broadcasted_iota(jnp.int32, (S5_WIDTH, S5_LANES), 1) // S5_STATE
    mask = (rows == cols).astype(F32)
    rep = lambda m: jnp.concatenate([m] * S5_GROUPS, axis=0) * mask
    return (jnp.concatenate([rep(bbr), rep(bbi)], axis=1), jnp.concatenate([rep(ctr), -rep(cti)], axis=1))


HALO = 8


def _rw_pre_specs(z, params, tm, order):
    halo_blocks = tm // HALO
    return ([pl.BlockSpec((tm, z.shape[1]), lambda i: (order(i), 0)),
             pl.BlockSpec((HALO, z.shape[1]), lambda i: (jnp.maximum(order(i) * halo_blocks - 1, 0), 0))]
            + [_full_spec(p) for p in params])


def _rw_pre_fwd(z, params, tm):
    t = z.shape[0]
    npar = len(params)

    def body(z_ref, halo_ref, *refs):
        carry = jnp.where(pl.program_id(0) == 0, 0.0, halo_ref[pl.ds(HALO - 1, 1), :])
        res = _f_rw_pre(z_ref[...], carry, *[r[...].astype(F32) for r in refs[:npar]])
        for r, v in zip(refs[npar:], res):
            r[...] = v

    return pl.pallas_call(
        body, name="rw_pre",
        out_shape=[jax.ShapeDtypeStruct((t, RWKV_WIDTH), F32)] * 7,
        grid=(t // tm,),
        in_specs=_rw_pre_specs(z, params, tm, lambda i: i),
        out_specs=[pl.BlockSpec((tm, RWKV_WIDTH), lambda i: (i, 0))] * 7,
        compiler_params=_params(("parallel",)),
    )(z, z, *params)


def _rw_pre_bwd(z, params, cots, tm):
    t = z.shape[0]
    nt = t // tm
    npar = len(params)
    order = lambda i: nt - 1 - i

    def body(z_ref, halo_ref, *refs):
        pin, cin = refs[:npar], refs[npar:npar + 7]
        dz_ref = refs[npar + 7]
        dpo = refs[npar + 8:npar + 8 + npar]
        dcarry_ref = refs[npar + 8 + npar]
        i = pl.program_id(0)

        @pl.when(i == 0)
        def _():
            dcarry_ref[...] = jnp.zeros_like(dcarry_ref)
            for r in dpo:
                r[...] = jnp.zeros_like(r)

        carry = jnp.where(i == nt - 1, 0.0, halo_ref[pl.ds(HALO - 1, 1), :])
        _, vjp = jax.vjp(_f_rw_pre, z_ref[...], carry, *[r[...].astype(F32) for r in pin])
        g = vjp(tuple(c[...] for c in cin))
        last_row = lax.broadcasted_iota(jnp.int32, z_ref.shape, 0) == tm - 1
        dz_ref[...] = g[0] + jnp.where(last_row, dcarry_ref[...], 0.0)
        dcarry_ref[...] = g[1]
        for r, v in zip(dpo, g[2:]):
            r[...] += v

    tok = lambda w: pl.BlockSpec((tm, w), lambda i: (order(i), 0))
    return pl.pallas_call(
        body, name="rw_pre_bwd",
        out_shape=[jax.ShapeDtypeStruct(z.shape, F32)] + [jax.ShapeDtypeStruct(p.shape, F32) for p in params],
        grid=(nt,),
        in_specs=_rw_pre_specs(z, params, tm, order) + [tok(RWKV_WIDTH)] * 7,
        out_specs=[tok(z.shape[1])] + [_full_spec(p) for p in params],
        scratch_shapes=[pltpu.VMEM((1, z.shape[1]), F32)],
        compiler_params=_params(("arbitrary",)),
    )(z, z, *params, *cots)


def _s5_scan(bu, lam, tm):
    t, w = bu.shape
    h = w // 2

    def body(bu_ref, lam_ref, x_ref, xp_ref, carry_ref):
        @pl.when(pl.program_id(0) == 0)
        def _():
            carry_ref[...] = jnp.zeros_like(carry_ref)

        lr, li = lam_ref[:, :h], lam_ref[:, h:]

        def step(s, c):
            cr, ci = c
            row = pl.ds(s, 1)
            xp_ref[row, :h] = cr
            xp_ref[row, h:] = ci
            nr = lr * cr - li * ci + bu_ref[row, :h]
            ni = lr * ci + li * cr + bu_ref[row, h:]
            x_ref[row, :h] = nr
            x_ref[row, h:] = ni
            return nr, ni

        cr, ci = lax.fori_loop(0, tm, step, (carry_ref[:, :h], carry_ref[:, h:]))
        carry_ref[:, :h] = cr
        carry_ref[:, h:] = ci

    spec = pl.BlockSpec((tm, w), lambda i: (i, 0))
    return pl.pallas_call(
        body, name="s5_scan",
        out_shape=[jax.ShapeDtypeStruct((t, w), F32)] * 2,
        grid=(t // tm,),
        in_specs=[spec, pl.BlockSpec((1, w), lambda i: (0, 0))],
        out_specs=[spec, spec],
        scratch_shapes=[pltpu.VMEM((1, w), F32)],
        compiler_params=_params(("arbitrary",)),
    )(bu, lam)


def _s5_scan_bwd(dx, xp, lam, tm):
    t, w = dx.shape
    h = w // 2
    nt = t // tm

    def body(dx_ref, xp_ref, lam_ref, dbu_ref, dlam_ref, carry_ref):
        @pl.when(pl.program_id(0) == 0)
        def _():
            carry_ref[...] = jnp.zeros_like(carry_ref)
            dlam_ref[...] = jnp.zeros_like(dlam_ref)

        lr, li = lam_ref[:, :h], lam_ref[:, h:]

        def step(s, c):
            cr, ci = c
            row = pl.ds(tm - 1 - s, 1)
            nr = lr * cr + li * ci + dx_ref[row, :h]
            ni = lr * ci - li * cr + dx_ref[row, h:]
            dbu_ref[row, :h] = nr
            dbu_ref[row, h:] = ni
            return nr, ni

        cr, ci = lax.fori_loop(0, tm, step, (carry_ref[:, :h], carry_ref[:, h:]))
        carry_ref[:, :h] = cr
        carry_ref[:, h:] = ci
        gr, gi = dbu_ref[:, :h], dbu_ref[:, h:]
        pr, pi_ = xp_ref[:, :h], xp_ref[:, h:]
        dlam_ref[:, :h] += jnp.sum(gr * pr + gi * pi_, axis=0, keepdims=True)
        dlam_ref[:, h:] += jnp.sum(gi * pr - gr * pi_, axis=0, keepdims=True)

    spec = pl.BlockSpec((tm, w), lambda i: (nt - 1 - i, 0))
    row_spec = pl.BlockSpec((1, w), lambda i: (0, 0))
    return pl.pallas_call(
        body, name="s5_scan_bwd",
        out_shape=[jax.ShapeDtypeStruct((t, w), F32), jax.ShapeDtypeStruct((1, w), F32)],
        grid=(nt,),
        in_specs=[spec, spec, row_spec],
        out_specs=[spec, row_spec],
        scratch_shapes=[pltpu.VMEM((1, w), F32)],
        compiler_params=_params(("arbitrary",)),
    )(dx, xp, lam)


def _wkv_chunk(s0, r, lw, k, v, a, b):
    c = r.shape[0]
    row = lax.broadcasted_iota(jnp.int32, (c, c), 0)
    col = lax.broadcasted_iota(jnp.int32, (c, c), 1)
    incl, strict = col <= row, col < row
    lc = _fdot(incl.astype(F32), lw)
    e_in, e_ex, e_neg = jnp.exp(lc), jnp.exp(lc - lw), jnp.exp(-lc)
    rt, at, kt, bt = r * e_in, a * e_ex, k * e_neg, b * e_neg
    mab = jnp.where(strict, _fdot(at, bt, _NT), 0.0)
    mak = jnp.where(strict, _fdot(at, kt, _NT), 0.0)
    mrb = jnp.where(incl, _fdot(rt, bt, _NT), 0.0)
    mrk = jnp.where(incl, _fdot(rt, kt, _NT), 0.0)
    inv = (row == col).astype(F32) + mab
    pw = mab
    for _ in range(int(math.log2(c)) - 1):
        pw = _fdot(pw, pw)
        inv = inv + _fdot(inv, pw)
    u = _fdot(inv, _fdot(at, s0, _NT) + _fdot(mak, v))
    y = _fdot(rt, s0, _NT) + _fdot(mrb, u) + _fdot(mrk, v)
    e_tot = jnp.exp(jnp.sum(lw, axis=0, keepdims=True))
    s1 = (s0 + _fdot(u, bt, _TN) + _fdot(v, kt, _TN)) * e_tot
    return y, s1


def _wkv_fwd(seqs):
    hh, t, n = seqs[0].shape
    c = WKV_CHUNK
    nc = t // c

    def body(*refs):
        ins, y_ref, ck_ref, s_ref = refs[:6], refs[6], refs[7], refs[8]

        @pl.when(pl.program_id(0) == 0)
        def _():
            s_ref[...] = jnp.zeros_like(s_ref)

        def head(h, carry):
            s0 = s_ref[h]
            ck_ref[0, h] = s0
            y, s1 = _wkv_chunk(s0, *[r[h] for r in ins])
            y_ref[h] = y
            s_ref[h] = s1
            return carry

        lax.fori_loop(0, hh, head, 0)

    spec = pl.BlockSpec((hh, c, n), lambda i: (0, i, 0))
    return pl.pallas_call(
        body, name="wkv_fwd",
        out_shape=[jax.ShapeDtypeStruct((hh, t, n), F32), jax.ShapeDtypeStruct((nc, hh, n, n), F32)],
        grid=(nc,),
        in_specs=[spec] * 6,
        out_specs=[spec, pl.BlockSpec((1, hh, n, n), lambda i: (i, 0, 0, 0))],
        scratch_shapes=[pltpu.VMEM((hh, n, n), F32)],
        compiler_params=_params(("arbitrary",)),
    )(*seqs)


def _wkv_bwd(seqs, ck, dy):
    hh, t, n = seqs[0].shape
    c = WKV_CHUNK
    nc = t // c

    def body(*refs):
        ins, ck_ref, dy_ref, outs, ds_ref = refs[:6], refs[6], refs[7], refs[8:14], refs[14]

        @pl.when(pl.program_id(0) == 0)
        def _():
            ds_ref[...] = jnp.zeros_like(ds_ref)

        def head(h, carry):
            _, vjp = jax.vjp(_wkv_chunk, ck_ref[0, h], *[r[h] for r in ins])
            g = vjp((dy_ref[h], ds_ref[h]))
            ds_ref[h] = g[0]
            for o, v in zip(outs, g[1:]):
                o[h] = v
            return carry

        lax.fori_loop(0, hh, head, 0)

    spec = pl.BlockSpec((hh, c, n), lambda i: (0, nc - 1 - i, 0))
    return pl.pallas_call(
        body, name="wkv_bwd",
        out_shape=[jax.ShapeDtypeStruct((hh, t, n), F32)] * 6,
        grid=(nc,),
        in_specs=[spec] * 6 + [pl.BlockSpec((1, hh, n, n), lambda i: (nc - 1 - i, 0, 0, 0)), spec],
        out_specs=[spec] * 6,
        scratch_shapes=[pltpu.VMEM((hh, n, n), F32)],
        compiler_params=_params(("arbitrary",)),
    )(*seqs, ck, dy)


def _coords():
    return lax.axis_index("x"), lax.axis_index("y"), lax.axis_index("c")


def _flip(v, f):
    return 1 - v if f else v


_CHIP_FLIPS = [(1, 0), (0, 1), (1, 1)]
_DEV_FLIPS = [(fx, fy, fc) for fx in (0, 1) for fy in (0, 1) for fc in (0, 1) if (fx, fy, fc) != (0, 0, 0)]
HBM_SPEC = pl.BlockSpec(memory_space=pl.ANY)


def _gather_shards(packed):
    def body(src, out, send_sems, recv_sems, local_sem):
        x, y, c = _coords()
        mine = pltpu.make_async_copy(src, out.at[2 * x + y], local_sem)
        mine.start()

        def copy(k):
            fx, fy = _CHIP_FLIPS[k]
            return pltpu.make_async_remote_copy(
                src_ref=src, dst_ref=out.at[2 * x + y], send_sem=send_sems.at[k], recv_sem=recv_sems.at[k],
                device_id=(_flip(x, fx), _flip(y, fy), c), device_id_type=MESH)

        def arrival(k):
            fx, fy = _CHIP_FLIPS[k]
            return pltpu.make_async_remote_copy(
                src_ref=src, dst_ref=out.at[2 * _flip(x, fx) + _flip(y, fy)], send_sem=send_sems.at[k],
                recv_sem=recv_sems.at[k], device_id=(_flip(x, fx), _flip(y, fy), c), device_id_type=MESH)

        sends = [copy(k) for k in range(3)]
        for cp in sends:
            cp.start()
        for k in range(3):
            arrival(k).wait_recv()
        for cp in sends:
            cp.wait_send()
        mine.wait()

    return pl.pallas_call(
        body, name="gather_weights",
        out_shape=jax.ShapeDtypeStruct((4,) + packed.shape, packed.dtype),
        in_specs=[HBM_SPEC], out_specs=HBM_SPEC,
        scratch_shapes=[pltpu.SemaphoreType.DMA((3,)), pltpu.SemaphoreType.DMA((3,)), pltpu.SemaphoreType.DMA],
    )(packed)


def _exchange_pieces(pieces):
    def body(src, out, send_sems, recv_sems, local_sem):
        x, y, c = _coords()
        me = 4 * x + 2 * y + c
        mine = pltpu.make_async_copy(src.at[me], out.at[me], local_sem)
        mine.start()

        def peer(k):
            fx, fy, fc = _DEV_FLIPS[k]
            return _flip(x, fx), _flip(y, fy), _flip(c, fc)

        def copy(k):
            px, py, pc = peer(k)
            return pltpu.make_async_remote_copy(
                src_ref=src.at[4 * px + 2 * py + pc], dst_ref=out.at[me], send_sem=send_sems.at[k],
                recv_sem=recv_sems.at[k], device_id=(px, py, pc), device_id_type=MESH)

        def arrival(k):
            px, py, pc = peer(k)
            return pltpu.make_async_remote_copy(
                src_ref=src.at[me], dst_ref=out.at[4 * px + 2 * py + pc], send_sem=send_sems.at[k],
                recv_sem=recv_sems.at[k], device_id=(px, py, pc), device_id_type=MESH)

        sends = [copy(k) for k in range(7)]
        for cp in sends:
            cp.start()
        for k in range(7):
            arrival(k).wait_recv()
        for cp in sends:
            cp.wait_send()
        mine.wait()

    return pl.pallas_call(
        body, name="exchange_grads",
        out_shape=jax.ShapeDtypeStruct(pieces.shape, pieces.dtype),
        in_specs=[HBM_SPEC], out_specs=HBM_SPEC,
        scratch_shapes=[pltpu.SemaphoreType.DMA((7,)), pltpu.SemaphoreType.DMA((7,)), pltpu.SemaphoreType.DMA],
    )(pieces)


def _share_results(res, big_rows):
    _, rr, cc = res.shape
    small_rows = rr - big_rows

    def body(src, big, small, send_sems, recv_sems, local_sems):
        x, y, c = _coords()
        me = 4 * x + 2 * y + c
        src_big = src.at[:, pl.ds(0, big_rows), :]
        src_small = src.at[:, pl.ds(big_rows, small_rows), :]
        own = [pltpu.make_async_copy(src_big, big.at[c], local_sems.at[0]),
               pltpu.make_async_copy(src_small, small.at[me], local_sems.at[1])]
        for cp in own:
            cp.start()

        def peer(k):
            fx, fy, fc = _DEV_FLIPS[k]
            return _flip(x, fx), _flip(y, fy), _flip(c, fc)

        def small_copy(k, sender):
            px, py, pc = peer(k)
            slot = me if sender else 4 * px + 2 * py + pc
            return pltpu.make_async_remote_copy(
                src_ref=src_small, dst_ref=small.at[slot], send_sem=send_sems.at[k], recv_sem=recv_sems.at[k],
                device_id=(px, py, pc), device_id_type=MESH)

        def big_copy(sender):
            return pltpu.make_async_remote_copy(
                src_ref=src_big, dst_ref=big.at[c if sender else 1 - c], send_sem=send_sems.at[7],
                recv_sem=recv_sems.at[7], device_id=(x, y, 1 - c), device_id_type=MESH)

        sends = [small_copy(k, True) for k in range(7)] + [big_copy(True)]
        for cp in sends:
            cp.start()
        for k in range(7):
            small_copy(k, False).wait_recv()
        big_copy(False).wait_recv()
        for cp in sends:
            cp.wait_send()
        for cp in own:
            cp.wait()

    return pl.pallas_call(
        body, name="share_updates",
        out_shape=[jax.ShapeDtypeStruct((2, 4, big_rows, cc), res.dtype),
                   jax.ShapeDtypeStruct((8, 4, small_rows, cc), res.dtype)],
        in_specs=[HBM_SPEC], out_specs=[HBM_SPEC, HBM_SPEC],
        scratch_shapes=[pltpu.SemaphoreType.DMA((8,)), pltpu.SemaphoreType.DMA((8,)), pltpu.SemaphoreType.DMA((2,))],
    )(res)


def _reduce_adamw(recv, w, m, v, tr=128):
    _, rr, cc = recv.shape
    bc1 = 1.0 - ADAM_B1 ** ADAM_STEP
    bc2 = 1.0 - ADAM_B2 ** ADAM_STEP

    def body(g_ref, w_ref, m_ref, v_ref, o_ref):
        g = g_ref[0]
        for d in range(1, 8):
            g = g + g_ref[d]
        m_new = ADAM_B1 * m_ref[...] + (1.0 - ADAM_B1) * g
        v_new = ADAM_B2 * v_ref[...] + (1.0 - ADAM_B2) * jnp.square(g)
        o_ref[0] = g
        o_ref[1] = -ADAM_LR * ((m_new / bc1) / (jnp.sqrt(v_new / bc2) + ADAM_EPS) + ADAM_WD * w_ref[...])
        o_ref[2] = m_new
        o_ref[3] = v_new

    spec = pl.BlockSpec((tr, cc), lambda i: (i, 0))
    return pl.pallas_call(
        body, name="reduce_adamw",
        out_shape=jax.ShapeDtypeStruct((4, rr, cc), F32),
        grid=(rr // tr,),
        in_specs=[pl.BlockSpec((8, tr, cc), lambda i: (0, i, 0)), spec, spec, spec],
        out_specs=pl.BlockSpec((4, tr, cc), lambda i: (0, i, 0)),
        compiler_params=_params(("parallel",)),
    )(recv, w, m, v)


def _mat(a):
    return a.reshape(a.shape[-2:])


def _pack_rows(flat, rows):
    return jnp.pad(flat, (0, rows * PACK_COLS - flat.shape[0])).reshape(rows, PACK_COLS)


def _pack_shards(shards):
    flat = jnp.concatenate([shards[n].reshape(-1) for n in SHARDED_NAMES])
    assert flat.shape[0] % (2 * 8 * PACK_COLS) == 0
    return flat.reshape(-1, PACK_COLS)


def _unpack_shards(packed, shapes):
    flat = packed.reshape(-1)
    out, off = {}, 0
    for n in SHARDED_NAMES:
        size = shapes[n][0] * shapes[n][1]
        out[n] = flat[off:off + size].reshape(shapes[n])
        off += size
    return out


def _pack_small(arrays):
    flat = jnp.concatenate([arrays[n].reshape(-1) for n in SMALL_NAMES])
    return _pack_rows(flat, 8 * SMALL_ROWS).reshape(8, SMALL_ROWS, PACK_COLS)


def _unpack_small(packed, shapes):
    flat = packed.reshape(-1)
    out, off = {}, 0
    for n in SMALL_NAMES:
        size = math.prod(shapes[n])
        out[n] = flat[off:off + size].reshape(shapes[n])
        off += size
    return out


def _to_heads(a):
    t = a.shape[0]
    return a.reshape(t, HEADS, HEAD).transpose(1, 0, 2)


def _from_heads(a):
    return a.transpose(1, 0, 2).reshape(a.shape[1], HEADS * HEAD)


def _row(a):
    return a.reshape(1, -1)


def _local_step(x, p, target, wf, ws):
    t = x.shape[0]
    tm = min(256, t)
    g = {}

    lam_re, lam_im = ws['s5_lam_re'].reshape(S5_GROUPS, S5_STATE), ws['s5_lam_im'].reshape(S5_GROUPS, S5_STATE)
    log_step = ws['s5_log_step'].reshape(S5_GROUPS, 1)
    gp = (S5_GROUPS, S5_STATE)
    lam_ins = (lam_re, lam_im, log_step)
    lbr, lbi, cfr, cfi = _small_fwd("s5_lam", _f_s5_lam, lam_ins, [(gp, F32)] * 4)
    lam_row = jnp.concatenate([_row(lbr), _row(lbi)], axis=1)
    to_t = lambda a, perm: a.reshape((S5_GROUPS,) + a.shape[-2:]).transpose(perm).reshape(S5_GROUP, S5_LANES)
    build_ins = (_row(cfr), _row(cfi), to_t(ws['s5_b_re'], (2, 0, 1)), to_t(ws['s5_b_im'], (2, 0, 1)),
                 to_t(ws['s5_c_re'], (1, 0, 2)), to_t(ws['s5_c_im'], (1, 0, 2)))
    stack_shape = (S5_WIDTH, 2 * S5_LANES)
    bstack, cstack_t = _small_fwd("s5_build", _f_s5_build, build_ins, [(stack_shape, F32)] * 2)

    norm_mix, norm_ffn, norm_ple = _row(ws['norm_mix']), _row(ws['norm_ffn']), _row(ws['norm_ple'])
    final_norm = _row(ws['final_norm'])
    (xn,) = _tok_fwd("norm_in", _f_norm_in, [x], [norm_mix], [(x.shape[1], BF16)], tm)
    proj = _mm("proj", xn, wf['w_in'], 'nn')
    u, z = proj[:, :S5_WIDTH], proj[:, S5_WIDTH:]

    bu = _mm("s5_bu", u, bstack, 'nn', precise=True)
    xs, xs_prev = _s5_scan(bu, lam_row, tm)
    ypre = _mm("s5_y", xs, cstack_t, 'nt', precise=True)
    s5_par = [_row(ws['s5_d']), wf['s5_glu_w'], _row(ws['s5_glu_b'])]
    (s5_out,) = _tok_fwd("s5_post", _f_s5_post, [ypre, u], s5_par, [(S5_WIDTH, BF16)], tm)

    pre_par = [_row(ws['rw_shift_mu']), _row(ws['rw_w0']), wf['rw_w2'], _row(ws['rw_a0']), wf['rw_a2'],
               wf['rw_g2'], _row(ws['rw_k_k']), _row(ws['rw_k_a'])]
    r, lw, kp, v, an, bn, gate = _rw_pre_fwd(z, pre_par, tm)
    seqs = [_to_heads(s) for s in (r, lw, kp, v, an, bn)]
    y_heads, ck = _wkv_fwd(seqs)
    y_wkv = _from_heads(y_heads)
    post_par = [_row(ws['rw_ln_w']), _row(ws['rw_ln_b']), _row(ws['rw_r_k'])]
    post_toks = [y_wkv, r, kp, v, gate]
    (rw_out,) = _tok_fwd("rw_post", _f_rw_post, post_toks, post_par, [(RWKV_WIDTH, BF16)], tm)

    mixcat = jnp.concatenate([s5_out, rw_out], axis=1)
    mixed = _mm("mix_out", mixcat, wf['w_out'], 'nn')
    h1, hn = _tok_fwd("mix_res", _f_mix_res, [x, mixed], [norm_ffn], [(x.shape[1], F32), (x.shape[1], BF16)], tm)
    w13 = jnp.concatenate([wf['ffn_w1'], wf['ffn_w3']], axis=1)
    a13 = _mm("ffn_up", hn, w13, 'nn')
    (f,) = _tok_fwd("ffn_act", _f_ffn_act, [a13], [], [(FFN_HIDDEN, BF16)], tm)
    ffo = _mm("ffn_down", f, wf['ffn_w2'], 'nn')
    h2, hp = _tok_fwd("ffn_res", _f_ffn_res, [h1, ffo], [norm_ple], [(x.shape[1], F32), (x.shape[1], BF16)], tm)
    gpre = _mm("ple_gate", hp, wf['ple_gate_w'], 'nn')
    pu = _mm("ple_up", p, wf['ple_up_w'], 'nn')

    dh2, dgpre, dpu, g['final_norm'], loss = _tok_bwd(
        "loss", _f_loss, [h2, gpre, pu, target], [final_norm], [None],
        [F32, BF16, BF16, None], [True], tm, acc_out=0)
    g['ple_gate_w'] = _mm("d_ple_gate_w", hp, dgpre, 'tn')
    g['ple_up_w'] = _mm("d_ple_up_w", p, dpu, 'tn')
    dhp = _mm("d_hp", dgpre, wf['ple_gate_w'], 'nt')
    dh1, dffo, g['norm_ple'] = _tok_bwd("ffn_res_bwd", _f_ffn_res, [h1, ffo], [norm_ple], [dh2, dhp],
                                        [F32, BF16], [True], tm)
    g['ffn_w2'] = _mm("d_ffn_w2", f, dffo, 'tn')
    df = _mm("d_f", dffo, wf['ffn_w2'], 'nt')
    (da13,) = _tok_bwd("ffn_act_bwd", _f_ffn_act, [a13], [], [df], [BF16], [], tm)
    dw13 = _mm("d_ffn_w13", hn, da13, 'tn')
    g['ffn_w1'], g['ffn_w3'] = dw13[:, :FFN_HIDDEN], dw13[:, FFN_HIDDEN:]
    dhn = _mm("d_hn", da13, w13, 'nt')
    dx_a, dmixed, g['norm_ffn'] = _tok_bwd("mix_res_bwd", _f_mix_res, [x, mixed], [norm_ffn], [dh1, dhn],
                                           [F32, BF16], [True], tm)
    g['w_out'] = _mm("d_w_out", mixcat, dmixed, 'tn')
    dmixcat = _mm("d_mixcat", dmixed, wf['w_out'], 'nt')
    ds5_out, drw_out = dmixcat[:, :S5_WIDTH], dmixcat[:, S5_WIDTH:]

    dy_wkv, dr_b, dkp_b, dv_b, dgate, g['rw_ln_w'], g['rw_ln_b'], g['rw_r_k'] = _tok_bwd(
        "rw_post_bwd", _f_rw_post, post_toks, post_par, [drw_out], [F32] * 5, [True] * 3, tm)
    dseqs = [_from_heads(d) for d in _wkv_bwd(seqs, ck, _to_heads(dy_wkv))]
    pre_cots = [dseqs[0] + dr_b, dseqs[1], dseqs[2] + dkp_b, dseqs[3] + dv_b, dseqs[4], dseqs[5], dgate]
    dz, *dpre = _rw_pre_bwd(z, pre_par, pre_cots, tm)
    for n, d in zip(['rw_shift_mu', 'rw_w0', 'rw_w2', 'rw_a0', 'rw_a2', 'rw_g2', 'rw_k_k', 'rw_k_a'], dpre):
        g[n] = d

    dypre, du_a, g['s5_d'], g['s5_glu_w'], g['s5_glu_b'] = _tok_bwd(
        "s5_post_bwd", _f_s5_post, [ypre, u], s5_par, [ds5_out], [F32, F32], [True] * 3, tm)
    dxs = _mm("d_s5_x", dypre, cstack_t, 'nn', precise=True)
    dcstack_t = _mm("d_s5_c", dypre, xs, 'tn', precise=True)
    dbu, dlam_row = _s5_scan_bwd(dxs, xs_prev, lam_row, tm)
    du_b = _mm("d_s5_u", dbu, bstack, 'nt', precise=True)
    dbstack = _mm("d_s5_b", u, dbu, 'tn', precise=True)
    dbuild = _small_bwd("s5_build_bwd", _f_s5_build, build_ins, (dbstack, dcstack_t))
    lam_cots = (dlam_row[:, :S5_LANES].reshape(gp), dlam_row[:, S5_LANES:].reshape(gp),
                dbuild[0].reshape(gp), dbuild[1].reshape(gp))
    g['s5_lam_re'], g['s5_lam_im'], g['s5_log_step'] = _small_bwd("s5_lam_bwd", _f_s5_lam, lam_ins, lam_cots)
    from_t = lambda a, perm: a.reshape(S5_GROUP, S5_GROUPS, S5_STATE).transpose(perm)
    g['s5_b_re'], g['s5_b_im'] = from_t(dbuild[2], (1, 2, 0)), from_t(dbuild[3], (1, 2, 0))
    g['s5_c_re'], g['s5_c_im'] = from_t(dbuild[4], (1, 0, 2)), from_t(dbuild[5], (1, 0, 2))

    dproj = jnp.concatenate([(du_a + du_b).astype(BF16), dz.astype(BF16)], axis=1)
    g['w_in'] = _mm("d_w_in", xn, dproj, 'tn')
    dxn = _mm("d_xn", dproj, wf['w_in'], 'nt')
    grad_x, g['norm_mix'] = _tok_bwd("norm_in_bwd", _f_norm_in, [x], [norm_mix], [dxn], [F32], [True], tm,
                                     add_to=(0, dx_a))
    return loss[0, 0], grad_x, g


def _split_shards(full, axis):
    return jnp.split(full, 4, axis=axis)


def _step(x, p, target, w, m, v):
    xc, yc, cc = _coords()
    chip, me = 2 * xc + yc, 4 * xc + 2 * yc + cc

    shard_shapes = {n: _mat(w[n]).shape for n in SHARDED_NAMES}
    packed_w = _pack_shards({n: _mat(w[n]) for n in SHARDED_NAMES})
    gathered = _gather_shards(packed_w.astype(BF16))
    per_chip = [_unpack_shards(gathered[s], shard_shapes) for s in range(4)]
    wf = {n: jnp.concatenate([per_chip[s][n] for s in range(4)], axis=SHARDED[n]) for n in SHARDED_NAMES}
    ws = {n: w[n] for n in SMALL_NAMES}

    loss, grad_x, g = _local_step(x[0], p[0, 0], target[0], wf, ws)

    big_rows = packed_w.shape[0] // 2
    per_chip_g = [_pack_shards({n: _split_shards(g[n], SHARDED[n])[s] for n in SHARDED_NAMES}) for s in range(4)]
    big = jnp.stack(per_chip_g).reshape(8, big_rows, PACK_COLS)
    small = _pack_small({n: g[n] for n in SMALL_NAMES})
    pieces = jnp.concatenate([big, small], axis=1)
    recv = _exchange_pieces(pieces)

    def my_piece(d):
        big_d = lax.dynamic_slice_in_dim(_pack_shards({n: _mat(d[n]) for n in SHARDED_NAMES}), cc * big_rows,
                                         big_rows, axis=0)
        small_d = lax.dynamic_index_in_dim(_pack_small({n: d[n] for n in SMALL_NAMES}), me, axis=0, keepdims=False)
        return jnp.concatenate([big_d, small_d], axis=0)

    res = _reduce_adamw(recv, my_piece(w), my_piece(m), my_piece(v))
    out_big, out_small = _share_results(res, big_rows)

    small_shapes = {n: w[n].shape for n in SMALL_NAMES}
    kinds = []
    for j in range(4):
        shards = _unpack_shards(out_big[:, j].reshape(2 * big_rows, PACK_COLS), shard_shapes)
        smalls = _unpack_small(out_small[:, j], small_shapes)
        kinds.append([shards[n].reshape(w[n].shape) if n in SHARDED else smalls[n] for n in WEIGHT_NAMES])
    total = lax.psum(loss, ("x", "y", "c"))
    return (total, grad_x[None], *kinds[0], *kinds[1], *kinds[2], *kinds[3])


def kernel(x, p, norm_mix, w_in, s5_lam_re, s5_lam_im, s5_log_step, s5_b_re, s5_b_im, s5_c_re, s5_c_im, s5_d, s5_glu_w, s5_glu_b, rw_shift_mu, rw_w0, rw_w2, rw_a0, rw_a2, rw_g2, rw_k_k, rw_k_a, rw_r_k, rw_ln_w, rw_ln_b, w_out, norm_ffn, ffn_w1, ffn_w3, ffn_w2, norm_ple, ple_gate_w, ple_up_w, final_norm, loss_target, m_norm_mix, m_w_in, m_s5_lam_re, m_s5_lam_im, m_s5_log_step, m_s5_b_re, m_s5_b_im, m_s5_c_re, m_s5_c_im, m_s5_d, m_s5_glu_w, m_s5_glu_b, m_rw_shift_mu, m_rw_w0, m_rw_w2, m_rw_a0, m_rw_a2, m_rw_g2, m_rw_k_k, m_rw_k_a, m_rw_r_k, m_rw_ln_w, m_rw_ln_b, m_w_out, m_norm_ffn, m_ffn_w1, m_ffn_w3, m_ffn_w2, m_norm_ple, m_ple_gate_w, m_ple_up_w, m_final_norm, v_norm_mix, v_w_in, v_s5_lam_re, v_s5_lam_im, v_s5_log_step, v_s5_b_re, v_s5_b_im, v_s5_c_re, v_s5_c_im, v_s5_d, v_s5_glu_w, v_s5_glu_b, v_rw_shift_mu, v_rw_w0, v_rw_w2, v_rw_a0, v_rw_a2, v_rw_g2, v_rw_k_k, v_rw_k_a, v_rw_r_k, v_rw_ln_w, v_rw_ln_b, v_w_out, v_norm_ffn, v_ffn_w1, v_ffn_w3, v_ffn_w2, v_norm_ple, v_ple_gate_w, v_ple_up_w, v_final_norm):
    args = dict(locals())
    w = {n: args[n] for n in WEIGHT_NAMES}
    m = {n: args["m_" + n] for n in WEIGHT_NAMES}
    v = {n: args["v_" + n] for n in WEIGHT_NAMES}
    return _step(x, p, loss_target, w, m, v)
```

```python
import functools
import math

import jax
import jax.numpy as jnp
from jax import lax
from jax.experimental import pallas as pl
from jax.experimental.pallas import tpu as pltpu

F32 = jnp.float32
BF16 = jnp.bfloat16
MESH = pl.DeviceIdType.MESH

S5_WIDTH = 512
RWKV_WIDTH = 512
S5_GROUP = 16
S5_GROUPS = 32
S5_STATE = 64
S5_LANES = S5_GROUPS * S5_STATE
HEAD = 64
HEADS = 8
DECAY_LORA = 64
AAA_LORA = 64
GATE_LORA = 128
FFN_HIDDEN = 2816
RMS_EPS = 1e-6
GN_EPS = 64e-5
L2_EPS = 1e-12
ADAM_LR = 0.001
ADAM_B1 = 0.9
ADAM_B2 = 0.999
ADAM_EPS = 1e-08
ADAM_WD = 0.01
ADAM_STEP = 10

WKV_CHUNK = 64
WKV_INTERLEAVE = 8
VMEM_LIMIT_BYTES = 48 * 1024 * 1024
LANE = 128
PACK_COLS = 1024
SMALL_ROWS = 24

WEIGHT_NAMES = ['norm_mix', 'w_in', 's5_lam_re', 's5_lam_im', 's5_log_step', 's5_b_re', 's5_b_im', 's5_c_re',
                's5_c_im', 's5_d', 's5_glu_w', 's5_glu_b', 'rw_shift_mu', 'rw_w0', 'rw_w2', 'rw_a0', 'rw_a2',
                'rw_g2', 'rw_k_k', 'rw_k_a', 'rw_r_k', 'rw_ln_w', 'rw_ln_b', 'w_out', 'norm_ffn', 'ffn_w1',
                'ffn_w3', 'ffn_w2', 'norm_ple', 'ple_gate_w', 'ple_up_w', 'final_norm']
SHARDED = {'w_in': 1, 's5_glu_w': 0, 'rw_w2': 1, 'rw_a2': 1, 'rw_g2': 1, 'w_out': 0, 'ffn_w1': 1, 'ffn_w3': 1,
           'ffn_w2': 0, 'ple_gate_w': 0, 'ple_up_w': 1}
SHARDED_NAMES = [n for n in WEIGHT_NAMES if n in SHARDED]
SMALL_NAMES = [n for n in WEIGHT_NAMES if n not in SHARDED]


def _params(sem=None):
    return pltpu.CompilerParams(dimension_semantics=sem, vmem_limit_bytes=VMEM_LIMIT_BYTES)


def _tile(n, target):
    best = None
    for d in range(LANE, min(n, target) + 1, LANE):
        if n % d == 0:
            best = d
    return n if best is None else best


_NN = (((1,), (0,)), ((), ()))
_NT = (((1,), (1,)), ((), ()))
_TN = (((0,), (0,)), ((), ()))


def _split(a):
    a = a.astype(F32)
    hi = a.astype(BF16)
    return hi, (a - hi.astype(F32)).astype(BF16)


def _dg(a, b, dims, precise):
    if precise:
        (ah, al), (bh, bl) = _split(a), _split(b)
        dg = lambda p, q: lax.dot_general(p, q, dims, preferred_element_type=F32)
        return dg(ah, bh) + (dg(ah, bl) + dg(al, bh))
    return lax.dot_general(a.astype(BF16), b.astype(BF16), dims, preferred_element_type=F32)


@jax.custom_vjp
def _bdot(x, w):
    return _dg(x, w, _NN, False)


def _bdot_fwd(x, w):
    return _dg(x, w, _NN, False), (x, w)


def _bdot_bwd(res, g):
    x, w = res
    return _dg(g, w, _NT, False), _dg(x, g, _TN, False)


_bdot.defvjp(_bdot_fwd, _bdot_bwd)


_FDOT_BWD = {_NN: (("g", "b", _NT), ("a", "g", _TN)),
             _NT: (("g", "b", _NN), ("g", "a", _TN)),
             _TN: (("b", "g", _NT), ("a", "g", _NN))}


@functools.partial(jax.custom_vjp, nondiff_argnums=(2,))
def _fdot(a, b, dims=_NN):
    return _dg(a, b, dims, True)


def _fdot_fwd(a, b, dims):
    return _dg(a, b, dims, True), (a, b)


def _fdot_bwd(dims, res, g):
    env = {"a": res[0], "b": res[1], "g": g}
    return tuple(_dg(env[p], env[q], d, True) for p, q, d in _FDOT_BWD[dims])


_fdot.defvjp(_fdot_fwd, _fdot_bwd)


@jax.custom_vjp
def _shift_down(z):
    return pltpu.roll(z, 1, 0)


def _shift_down_fwd(z):
    return pltpu.roll(z, 1, 0), None


def _shift_down_bwd(_, g):
    return (pltpu.roll(g, g.shape[0] - 1, 0),)


_shift_down.defvjp(_shift_down_fwd, _shift_down_bwd)


def _head_ones():
    r = lax.broadcasted_iota(jnp.int32, (RWKV_WIDTH, RWKV_WIDTH), 0) // HEAD
    c = lax.broadcasted_iota(jnp.int32, (RWKV_WIDTH, RWKV_WIDTH), 1) // HEAD
    return (r == c).astype(F32)


def _mm(name, a, b, mode, out_dtype=F32, precise=False, tm=512, tn=512, tk=1024):
    if mode == 'nn':
        (m, k), (_, n) = a.shape, b.shape
    elif mode == 'nt':
        (m, k), (n, _) = a.shape, b.shape
    else:
        (k, m), (_, n) = a.shape, b.shape
    tm, tn, tk = _tile(m, tm), _tile(n, tn), _tile(k, tk)
    nk = k // tk
    dims = {'nn': _NN, 'nt': _NT, 'tn': _TN}[mode]

    def body(a_ref, b_ref, o_ref, acc_ref):
        kk = pl.program_id(2)

        @pl.when(kk == 0)
        def _():
            acc_ref[...] = jnp.zeros_like(acc_ref)

        acc_ref[...] += _dg(a_ref[...], b_ref[...], dims, precise)

        @pl.when(kk == nk - 1)
        def _():
            o_ref[...] = acc_ref[...].astype(o_ref.dtype)

    if mode == 'tn':
        a_spec = pl.BlockSpec((tk, tm), lambda i, j, l: (l, i))
    else:
        a_spec = pl.BlockSpec((tm, tk), lambda i, j, l: (i, l))
    if mode == 'nt':
        b_spec = pl.BlockSpec((tn, tk), lambda i, j, l: (j, l))
    else:
        b_spec = pl.BlockSpec((tk, tn), lambda i, j, l: (l, j))
    return pl.pallas_call(
        body, name=name,
        out_shape=jax.ShapeDtypeStruct((m, n), out_dtype),
        grid=(m // tm, n // tn, nk),
        in_specs=[a_spec, b_spec],
        out_specs=pl.BlockSpec((tm, tn), lambda i, j, l: (i, j)),
        scratch_shapes=[pltpu.VMEM((tm, tn), F32)],
        compiler_params=_params(("parallel", "parallel", "arbitrary")),
    )(a, b)


def _full_spec(p):
    nd = p.ndim
    return pl.BlockSpec(p.shape, lambda i, nd=nd: (0,) * nd)


def _tok_fwd(name, fn, toks, params, outs, tm):
    t = toks[0].shape[0]
    nt, npar = len(toks), len(params)

    def body(*refs):
        tv = [r[...].astype(F32) for r in refs[:nt]]
        pv = [r[...].astype(F32) for r in refs[nt:nt + npar]]
        res = fn(*tv, *pv)
        for r, v in zip(refs[nt + npar:], res):
            r[...] = v.astype(r.dtype)

    return pl.pallas_call(
        body, name=name,
        out_shape=[jax.ShapeDtypeStruct((t, w), d) for w, d in outs],
        grid=(t // tm,),
        in_specs=[pl.BlockSpec((tm, a.shape[1]), lambda i: (i, 0)) for a in toks] + [_full_spec(p) for p in params],
        out_specs=[pl.BlockSpec((tm, w), lambda i: (i, 0)) for w, _ in outs],
        compiler_params=_params(("parallel",)),
    )(*toks, *params)


def _tok_bwd(name, fn, toks, params, cots, dtok, dpar, tm, acc_out=None, add_to=None):
    t = toks[0].shape[0]
    nt, npar = len(toks), len(params)
    cot_arrays = [c for c in cots if c is not None]
    ncot = len(cot_arrays)
    extra = [] if add_to is None else [add_to[1]]
    dtok_idx = [i for i, d in enumerate(dtok) if d is not None]
    dpar_idx = [i for i, d in enumerate(dpar) if d]

    def body(*refs):
        pos = 0
        tin = refs[pos:pos + nt]; pos += nt
        pin = refs[pos:pos + npar]; pos += npar
        cin = refs[pos:pos + ncot]; pos += ncot
        ein = refs[pos:pos + len(extra)]; pos += len(extra)
        dto = refs[pos:pos + len(dtok_idx)]; pos += len(dtok_idx)
        dpo = refs[pos:pos + len(dpar_idx)]; pos += len(dpar_idx)
        acc = refs[pos] if acc_out is not None else None
        first = pl.program_id(0) == 0

        tv = [r[...].astype(F32) for r in tin]
        pv = [r[...].astype(F32) for r in pin]
        res, vjp = jax.vjp(fn, *tv, *pv)
        cit = iter(cin)
        cs = tuple(jnp.ones_like(o) if c is None else next(cit)[...].astype(F32) for c, o in zip(cots, res))
        g = vjp(cs)
        for r, i in zip(dto, dtok_idx):
            v = g[i]
            if add_to is not None and add_to[0] == i:
                v = v + ein[0][...].astype(F32)
            r[...] = v.astype(r.dtype)

        @pl.when(first)
        def _():
            for r in dpo:
                r[...] = jnp.zeros_like(r)
            if acc is not None:
                acc[...] = jnp.zeros_like(acc)

        for r, i in zip(dpo, dpar_idx):
            r[...] += g[nt + i]
        if acc is not None:
            acc[...] += res[acc_out]

    out_shape = [jax.ShapeDtypeStruct(toks[i].shape, dtok[i]) for i in dtok_idx]
    out_shape += [jax.ShapeDtypeStruct(params[i].shape, F32) for i in dpar_idx]
    out_specs = [pl.BlockSpec((tm, toks[i].shape[1]), lambda i_: (i_, 0)) for i in dtok_idx]
    out_specs += [_full_spec(params[i]) for i in dpar_idx]
    if acc_out is not None:
        out_shape.append(jax.ShapeDtypeStruct((1, 1), F32))
        out_specs.append(pl.BlockSpec((1, 1), lambda i_: (0, 0)))
    tok_spec = lambda a: pl.BlockSpec((tm, a.shape[1]), lambda i_: (i_, 0))
    return pl.pallas_call(
        body, name=name,
        out_shape=out_shape,
        grid=(t // tm,),
        in_specs=[tok_spec(a) for a in toks] + [_full_spec(p) for p in params]
        + [tok_spec(c) for c in cot_arrays] + [tok_spec(e) for e in extra],
        out_specs=out_specs,
        compiler_params=_params(("arbitrary",)),
    )(*toks, *params, *cot_arrays, *extra)


def _small_fwd(name, fn, ins, outs):
    n = len(ins)

    def body(*refs):
        res = fn(*[r[...] for r in refs[:n]])
        for r, v in zip(refs[n:], res):
            r[...] = v.astype(r.dtype)

    return pl.pallas_call(
        body, name=name,
        out_shape=[jax.ShapeDtypeStruct(s, d) for s, d in outs],
        compiler_params=_params(),
    )(*ins)


def _small_bwd(name, fn, ins, cots):
    n = len(ins)

    def body(*refs):
        _, vjp = jax.vjp(fn, *[r[...] for r in refs[:n]])
        g = vjp(tuple(r[...] for r in refs[n:n + len(cots)]))
        for r, v in zip(refs[n + len(cots):], g):
            r[...] = v

    return pl.pallas_call(
        body, name=name,
        out_shape=[jax.ShapeDtypeStruct(a.shape, F32) for a in ins],
        compiler_params=_params(),
    )(*ins, *cots)


def _rms(x, g):
    return x * lax.rsqrt(jnp.mean(x * x, axis=-1, keepdims=True) + RMS_EPS) * g


def _f_norm_in(x, g):
    return (_rms(x, g),)


def _f_mix_res(x, mixed, g):
    h1 = x + mixed
    return h1, _rms(h1, g)


def _f_ffn_act(a13):
    a1, a3 = a13[:, :FFN_HIDDEN], a13[:, FFN_HIDDEN:]
    return (jax.nn.silu(a1) * a3,)


def _f_ffn_res(h1, ffo, g):
    h2 = h1 + ffo
    return h2, _rms(h2, g)


def _f_loss(h2, gpre, pu, target, g):
    h3 = h2 + jax.nn.sigmoid(gpre) * pu
    y = _rms(h3, g)
    err = jnp.square(y - target)
    return (0.5 * jnp.sum(jnp.mean(err, axis=-1, keepdims=True), axis=0, keepdims=True),)


def _f_s5_post(ypre, u, d, glu_w, glu_b):
    z = jax.nn.gelu(ypre + u * d)
    return (z * jax.nn.sigmoid(_bdot(z, glu_w) + glu_b),)


def _softplus(x):
    return jnp.maximum(x, 0.0) + jnp.log(1.0 + jnp.exp(-jnp.abs(x)))


def _f_rw_pre(z, carry, shift_mu, w0, w2, a0, a2, g2, k_k, k_a):
    rw = RWKV_WIDTH
    first_row = lax.broadcasted_iota(jnp.int32, z.shape, 0) == 0
    prev = jnp.where(first_row, carry, _shift_down(z))
    zs = z + (prev - z) * shift_mu
    o1, o2 = 3 * rw + DECAY_LORA, 3 * rw + DECAY_LORA + AAA_LORA
    r, k, v = zs[:, :rw], zs[:, rw:2 * rw], zs[:, 2 * rw:3 * rw]
    wl, al, gl = zs[:, 3 * rw:o1], zs[:, o1:o2], zs[:, o2:]
    w = -_softplus(-(w0 + _bdot(jnp.tanh(wl), w2))) - 0.5
    log_decay = -jnp.exp(w)
    a = jax.nn.sigmoid(a0 + _bdot(al, a2))
    g = _bdot(jax.nn.sigmoid(gl), g2)
    kk = k * k_k
    norm = jnp.sqrt(_fdot(kk * kk, _head_ones()))
    kk = kk / jnp.maximum(norm, L2_EPS)
    kp = k * (1.0 + (a - 1.0) * k_a)
    return r, log_decay, kp, v, -kk, kk * a, g


def _f_rw_post(y, r, kp, v, g, ln_w, ln_b, r_k):
    ones = _head_ones()
    yc = y - _fdot(y, ones) * (1.0 / HEAD)
    var = _fdot(yc * yc, ones) * (1.0 / HEAD)
    yn = yc * lax.rsqrt(var + GN_EPS) * ln_w + ln_b
    bonus = _fdot(r * kp * r_k, ones) * v
    return ((yn + bonus) * g,)


def _f_s5_lam(lam_re, lam_im, log_step):
    step = jnp.exp(log_step)
    dr, di = lam_re * step, lam_im * step
    e = jnp.exp(dr)
    lbr, lbi = e * jnp.cos(di), e * jnp.sin(di)
    nr, ni = lbr - 1.0, lbi
    den = lam_re * lam_re + lam_im * lam_im
    return lbr, lbi, (nr * lam_re + ni * lam_im) / den, (ni * lam_re - nr * lam_im) / den


def _f_s5_build(coef_r, coef_i, btr, bti, ctr, cti):
    bbr = coef_r * btr - coef_i * bti
    bbi = coef_r * bti + coef_i * btr
    rows = lax.broadcasted_iota(jnp.int32, (S5_WIDTH, S5_LANES), 0) // S5_GROUP
    cols = lax.broadcasted_iota(jnp.int32, (S5_WIDTH, S5_LANES), 1) // S5_STATE
    mask = (rows == cols).astype(F32)
    rep = lambda m: jnp.concatenate([m] * S5_GROUPS, axis=0) * mask
    return (jnp.concatenate([rep(bbr), rep(bbi)], axis=1), jnp.concatenate([rep(ctr), -rep(cti)], axis=1))


HALO = 8


def _rw_pre_specs(z, params, tm, order):
    halo_blocks = tm // HALO
    return ([pl.BlockSpec((tm, z.shape[1]), lambda i: (order(i), 0)),
             pl.BlockSpec((HALO, z.shape[1]), lambda i: (jnp.maximum(order(i) * halo_blocks - 1, 0), 0))]
            + [_full_spec(p) for p in params])


def _rw_pre_fwd(z, params, tm):
    t = z.shape[0]
    npar = len(params)

    def body(z_ref, halo_ref, *refs):
        carry = jnp.where(pl.program_id(0) == 0, 0.0, halo_ref[pl.ds(HALO - 1, 1), :])
        res = _f_rw_pre(z_ref[...], carry, *[r[...].astype(F32) for r in refs[:npar]])
        for r, v in zip(refs[npar:], res):
            r[...] = v

    return pl.pallas_call(
        body, name="rw_pre",
        out_shape=[jax.ShapeDtypeStruct((t, RWKV_WIDTH), F32)] * 7,
        grid=(t // tm,),
        in_specs=_rw_pre_specs(z, params, tm, lambda i: i),
        out_specs=[pl.BlockSpec((tm, RWKV_WIDTH), lambda i: (i, 0))] * 7,
        compiler_params=_params(("parallel",)),
    )(z, z, *params)


def _rw_pre_bwd(z, params, cots, tm):
    t = z.shape[0]
    nt = t // tm
    npar = len(params)
    order = lambda i: nt - 1 - i

    def body(z_ref, halo_ref, *refs):
        pin, cin = refs[:npar], refs[npar:npar + 7]
        dz_ref = refs[npar + 7]
        dpo = refs[npar + 8:npar + 8 + npar]
        dcarry_ref = refs[npar + 8 + npar]
        i = pl.program_id(0)

        @pl.when(i == 0)
        def _():
            dcarry_ref[...] = jnp.zeros_like(dcarry_ref)
            for r in dpo:
                r[...] = jnp.zeros_like(r)

        carry = jnp.where(i == nt - 1, 0.0, halo_ref[pl.ds(HALO - 1, 1), :])
        _, vjp = jax.vjp(_f_rw_pre, z_ref[...], carry, *[r[...].astype(F32) for r in pin])
        g = vjp(tuple(c[...] for c in cin))
        last_row = lax.broadcasted_iota(jnp.int32, z_ref.shape, 0) == tm - 1
        dz_ref[...] = g[0] + jnp.where(last_row, dcarry_ref[...], 0.0)
        dcarry_ref[...] = g[1]
        for r, v in zip(dpo, g[2:]):
            r[...] += v

    tok = lambda w: pl.BlockSpec((tm, w), lambda i: (order(i), 0))
    return pl.pallas_call(
        body, name="rw_pre_bwd",
        out_shape=[jax.ShapeDtypeStruct(z.shape, F32)] + [jax.ShapeDtypeStruct(p.shape, F32) for p in params],
        grid=(nt,),
        in_specs=_rw_pre_specs(z, params, tm, order) + [tok(RWKV_WIDTH)] * 7,
        out_specs=[tok(z.shape[1])] + [_full_spec(p) for p in params],
        scratch_shapes=[pltpu.VMEM((1, z.shape[1]), F32)],
        compiler_params=_params(("arbitrary",)),
    )(z, z, *params, *cots)


def _s5_scan(bu, lam, tm):
    t, w = bu.shape
    h = w // 2

    def body(bu_ref, lam_ref, x_ref, xp_ref, carry_ref):
        @pl.when(pl.program_id(0) == 0)
        def _():
            carry_ref[...] = jnp.zeros_like(carry_ref)

        lr, li = lam_ref[:, :h], lam_ref[:, h:]

        def step(s, c):
            cr, ci = c
            row = pl.ds(s, 1)
            xp_ref[row, :h] = cr
            xp_ref[row, h:] = ci
            nr = lr * cr - li * ci + bu_ref[row, :h]
            ni = lr * ci + li * cr + bu_ref[row, h:]
            x_ref[row, :h] = nr
            x_ref[row, h:] = ni
            return nr, ni

        cr, ci = lax.fori_loop(0, tm, step, (carry_ref[:, :h], carry_ref[:, h:]))
        carry_ref[:, :h] = cr
        carry_ref[:, h:] = ci

    spec = pl.BlockSpec((tm, w), lambda i: (i, 0))
    return pl.pallas_call(
        body, name="s5_scan",
        out_shape=[jax.ShapeDtypeStruct((t, w), F32)] * 2,
        grid=(t // tm,),
        in_specs=[spec, pl.BlockSpec((1, w), lambda i: (0, 0))],
        out_specs=[spec, spec],
        scratch_shapes=[pltpu.VMEM((1, w), F32)],
        compiler_params=_params(("arbitrary",)),
    )(bu, lam)


def _s5_scan_bwd(dx, xp, lam, tm):
    t, w = dx.shape
    h = w // 2
    nt = t // tm

    def body(dx_ref, xp_ref, lam_ref, dbu_ref, dlam_ref, carry_ref):
        @pl.when(pl.program_id(0) == 0)
        def _():
            carry_ref[...] = jnp.zeros_like(carry_ref)
            dlam_ref[...] = jnp.zeros_like(dlam_ref)

        lr, li = lam_ref[:, :h], lam_ref[:, h:]

        def step(s, c):
            cr, ci = c
            row = pl.ds(tm - 1 - s, 1)
            nr = lr * cr + li * ci + dx_ref[row, :h]
            ni = lr * ci - li * cr + dx_ref[row, h:]
            dbu_ref[row, :h] = nr
            dbu_ref[row, h:] = ni
            return nr, ni

        cr, ci = lax.fori_loop(0, tm, step, (carry_ref[:, :h], carry_ref[:, h:]))
        carry_ref[:, :h] = cr
        carry_ref[:, h:] = ci
        gr, gi = dbu_ref[:, :h], dbu_ref[:, h:]
        pr, pi_ = xp_ref[:, :h], xp_ref[:, h:]
        dlam_ref[:, :h] += jnp.sum(gr * pr + gi * pi_, axis=0, keepdims=True)
        dlam_ref[:, h:] += jnp.sum(gi * pr - gr * pi_, axis=0, keepdims=True)

    spec = pl.BlockSpec((tm, w), lambda i: (nt - 1 - i, 0))
    row_spec = pl.BlockSpec((1, w), lambda i: (0, 0))
    return pl.pallas_call(
        body, name="s5_scan_bwd",
        out_shape=[jax.ShapeDtypeStruct((t, w), F32), jax.ShapeDtypeStruct((1, w), F32)],
        grid=(nt,),
        in_specs=[spec, spec, row_spec],
        out_specs=[spec, row_spec],
        scratch_shapes=[pltpu.VMEM((1, w), F32)],
        compiler_params=_params(("arbitrary",)),
    )(dx, xp, lam)


def _wkv_chunk(s0, r, lw, k, v, a, b):
    c = r.shape[0]
    row = lax.broadcasted_iota(jnp.int32, (c, c), 0)
    col = lax.broadcasted_iota(jnp.int32, (c, c), 1)
    incl, strict = col <= row, col < row
    lc = _fdot(incl.astype(F32), lw)
    e_in, e_ex, e_neg = jnp.exp(lc), jnp.exp(lc - lw), jnp.exp(-lc)
    rt, at, kt, bt = r * e_in, a * e_ex, k * e_neg, b * e_neg
    mab = jnp.where(strict, _fdot(at, bt, _NT), 0.0)
    mak = jnp.where(strict, _fdot(at, kt, _NT), 0.0)
    mrb = jnp.where(incl, _fdot(rt, bt, _NT), 0.0)
    mrk = jnp.where(incl, _fdot(rt, kt, _NT), 0.0)
    inv = (row == col).astype(F32) + mab
    pw = mab
    for _ in range(int(math.log2(c)) - 1):
        pw = _fdot(pw, pw)
        inv = inv + _fdot(inv, pw)
    u = _fdot(inv, _fdot(at, s0, _NT) + _fdot(mak, v))
    y = _fdot(rt, s0, _NT) + _fdot(mrb, u) + _fdot(mrk, v)
    e_tot = jnp.exp(jnp.sum(lw, axis=0, keepdims=True))
    s1 = (s0 + _fdot(u, bt, _TN) + _fdot(v, kt, _TN)) * e_tot
    return y, s1


def _wkv_fwd(seqs):
    hh, t, n = seqs[0].shape
    c = WKV_CHUNK
    nc = t // c

    def body(*refs):
        ins, y_ref, ck_ref, s_ref = refs[:6], refs[6], refs[7], refs[8]

        @pl.when(pl.program_id(0) == 0)
        def _():
            s_ref[...] = jnp.zeros_like(s_ref)

        def heads(j, carry):
            hs = [j * WKV_INTERLEAVE + q for q in range(WKV_INTERLEAVE)]
            loaded = [(s_ref[h], [r[h] for r in ins]) for h in hs]
            res = [_wkv_chunk(s0, *vals) for s0, vals in loaded]
            for h, (s0, _), (y, s1) in zip(hs, loaded, res):
                ck_ref[0, h] = s0
                y_ref[h] = y
                s_ref[h] = s1
            return carry

        lax.fori_loop(0, hh // WKV_INTERLEAVE, heads, 0)

    spec = pl.BlockSpec((hh, c, n), lambda i: (0, i, 0))
    return pl.pallas_call(
        body, name="wkv_fwd",
        out_shape=[jax.ShapeDtypeStruct((hh, t, n), F32), jax.ShapeDtypeStruct((nc, hh, n, n), F32)],
        grid=(nc,),
        in_specs=[spec] * 6,
        out_specs=[spec, pl.BlockSpec((1, hh, n, n), lambda i: (i, 0, 0, 0))],
        scratch_shapes=[pltpu.VMEM((hh, n, n), F32)],
        compiler_params=_params(("arbitrary",)),
    )(*seqs)


def _wkv_bwd(seqs, ck, dy):
    hh, t, n = seqs[0].shape
    c = WKV_CHUNK
    nc = t // c

    def body(*refs):
        ins, ck_ref, dy_ref, outs, ds_ref = refs[:6], refs[6], refs[7], refs[8:14], refs[14]

        @pl.when(pl.program_id(0) == 0)
        def _():
            ds_ref[...] = jnp.zeros_like(ds_ref)

        def heads(j, carry):
            hs = [j * WKV_INTERLEAVE + q for q in range(WKV_INTERLEAVE)]
            loaded = [([ck_ref[0, h]] + [r[h] for r in ins], (dy_ref[h], ds_ref[h])) for h in hs]
            grads = [jax.vjp(_wkv_chunk, *vals)[1](cot) for vals, cot in loaded]
            for h, g in zip(hs, grads):
                ds_ref[h] = g[0]
                for o, v in zip(outs, g[1:]):
                    o[h] = v
            return carry

        lax.fori_loop(0, hh // WKV_INTERLEAVE, heads, 0)

    spec = pl.BlockSpec((hh, c, n), lambda i: (0, nc - 1 - i, 0))
    return pl.pallas_call(
        body, name="wkv_bwd",
        out_shape=[jax.ShapeDtypeStruct((hh, t, n), F32)] * 6,
        grid=(nc,),
        in_specs=[spec] * 6 + [pl.BlockSpec((1, hh, n, n), lambda i: (nc - 1 - i, 0, 0, 0)), spec],
        out_specs=[spec] * 6,
        scratch_shapes=[pltpu.VMEM((hh, n, n), F32)],
        compiler_params=_params(("arbitrary",)),
    )(*seqs, ck, dy)


def _coords():
    return lax.axis_index("x"), lax.axis_index("y"), lax.axis_index("c")


def _flip(v, f):
    return 1 - v if f else v


_CHIP_FLIPS = [(1, 0), (0, 1), (1, 1)]
_DEV_FLIPS = [(fx, fy, fc) for fx in (0, 1) for fy in (0, 1) for fc in (0, 1) if (fx, fy, fc) != (0, 0, 0)]
HBM_SPEC = pl.BlockSpec(memory_space=pl.ANY)


def _chip_peer(k, x, y):
    fx, fy = _CHIP_FLIPS[k]
    return _flip(x, fx), _flip(y, fy)


def _dev_peer(k, x, y, c):
    fx, fy, fc = _DEV_FLIPS[k]
    return _flip(x, fx), _flip(y, fy), _flip(c, fc)


def _rows_of_core(ref, core):
    h = ref.shape[-2] // 2
    rows = pl.ds(pl.multiple_of(core * h, 8), h)
    return ref.at[rows, :] if len(ref.shape) == 2 else ref.at[:, rows, :]


def _gather_weights(shards):
    n = len(shards)

    def body(*refs):
        srcs, outs = refs[:n], refs[n:2 * n]
        send_sems, recv_sems, local_sems = refs[2 * n:]
        x, y, c = _coords()
        me = 2 * x + y

        def ici(i, k, arriving):
            px, py = _chip_peer(k, x, y)
            chip = 2 * px + py if arriving else me
            return pltpu.make_async_remote_copy(
                src_ref=_rows_of_core(srcs[i], c), dst_ref=_rows_of_core(outs[i].at[chip], c),
                send_sem=send_sems.at[i, k], recv_sem=recv_sems.at[i, k],
                device_id=(px, py, c), device_id_type=MESH)

        def d2d(i, k, arriving):
            px, py = _chip_peer(k, x, y)
            blk = _rows_of_core(outs[i].at[2 * px + py], 1 - c if arriving else c)
            return pltpu.make_async_remote_copy(
                src_ref=blk, dst_ref=blk, send_sem=send_sems.at[i, 3 + k], recv_sem=recv_sems.at[i, 3 + k],
                device_id=(x, y, 1 - c), device_id_type=MESH)

        own = [pltpu.make_async_copy(srcs[i], outs[i].at[me], local_sems.at[i]) for i in range(n)]
        first = [ici(i, k, False) for k in range(3) for i in range(n)]
        for cp in own + first:
            cp.start()
        passed = []
        for k in range(3):
            for i in range(n):
                ici(i, k, True).wait_recv()
                passed.append(d2d(i, k, False))
                passed[-1].start()
        for k in range(3):
            for i in range(n):
                d2d(i, k, True).wait_recv()
        for cp in first + passed:
            cp.wait_send()
        for cp in own:
            cp.wait()

    return pl.pallas_call(
        body, name="gather_weights",
        out_shape=[jax.ShapeDtypeStruct((4,) + s.shape, s.dtype) for s in shards],
        in_specs=[HBM_SPEC] * n, out_specs=[HBM_SPEC] * n,
        scratch_shapes=[pltpu.SemaphoreType.DMA((n, 6)), pltpu.SemaphoreType.DMA((n, 6)),
                        pltpu.SemaphoreType.DMA((n,))],
    )(*shards)


def _reduce_cores(gs):
    n = len(gs)

    def body(*refs):
        srcs, own, other = refs[:n], refs[n:2 * n], refs[2 * n:3 * n]
        send_sems, recv_sems, local_sems = refs[3 * n:]
        x, y, c = _coords()

        def swap(i, arriving):
            return pltpu.make_async_remote_copy(
                src_ref=_rows_of_core(srcs[i], c if arriving else 1 - c), dst_ref=other[i],
                send_sem=send_sems.at[i], recv_sem=recv_sems.at[i], device_id=(x, y, 1 - c), device_id_type=MESH)

        local = [pltpu.make_async_copy(_rows_of_core(srcs[i], c), own[i], local_sems.at[i]) for i in range(n)]
        sends = [swap(i, False) for i in range(n)]
        for cp in local + sends:
            cp.start()
        for i in range(n):
            swap(i, True).wait_recv()
        for cp in sends:
            cp.wait_send()
        for cp in local:
            cp.wait()

    half = [jax.ShapeDtypeStruct((4, g.shape[1] // 2, g.shape[2]), g.dtype) for g in gs]
    res = pl.pallas_call(
        body, name="reduce_cores",
        out_shape=half + half,
        in_specs=[HBM_SPEC] * n, out_specs=[HBM_SPEC] * (2 * n),
        scratch_shapes=[pltpu.SemaphoreType.DMA((n,)), pltpu.SemaphoreType.DMA((n,)), pltpu.SemaphoreType.DMA((n,))],
    )(*gs)
    return res[:n], res[n:]


def _exchange_chips(ps, small):
    n = len(ps)

    def body(*refs):
        srcs, small_src, outs, small_out = refs[:n], refs[n], refs[n + 1:2 * n + 1], refs[2 * n + 1]
        send_sems, recv_sems, ssend, srecv, local_sems = refs[2 * n + 2:]
        x, y, c = _coords()
        chip, me = 2 * x + y, 4 * x + 2 * y + c

        def big(i, k, arriving):
            px, py = _chip_peer(k, x, y)
            peer = 2 * px + py
            return pltpu.make_async_remote_copy(
                src_ref=srcs[i].at[peer], dst_ref=outs[i].at[peer if arriving else chip],
                send_sem=send_sems.at[i, k], recv_sem=recv_sems.at[i, k],
                device_id=(px, py, c), device_id_type=MESH)

        def tiny(k, arriving):
            px, py, pc = _dev_peer(k, x, y, c)
            peer = 4 * px + 2 * py + pc
            return pltpu.make_async_remote_copy(
                src_ref=small_src.at[peer], dst_ref=small_out.at[peer if arriving else me],
                send_sem=ssend.at[k], recv_sem=srecv.at[k], device_id=(px, py, pc), device_id_type=MESH)

        local = [pltpu.make_async_copy(srcs[i].at[chip], outs[i].at[chip], local_sems.at[i]) for i in range(n)]
        local.append(pltpu.make_async_copy(small_src.at[me], small_out.at[me], local_sems.at[n]))
        sends = [big(i, k, False) for k in range(3) for i in range(n)] + [tiny(k, False) for k in range(7)]
        for cp in local + sends:
            cp.start()
        for k in range(3):
            for i in range(n):
                big(i, k, True).wait_recv()
        for k in range(7):
            tiny(k, True).wait_recv()
        for cp in sends:
            cp.wait_send()
        for cp in local:
            cp.wait()

    res = pl.pallas_call(
        body, name="exchange_chips",
        out_shape=[jax.ShapeDtypeStruct(p.shape, p.dtype) for p in ps] + [jax.ShapeDtypeStruct(small.shape, small.dtype)],
        in_specs=[HBM_SPEC] * (n + 1), out_specs=[HBM_SPEC] * (n + 1),
        scratch_shapes=[pltpu.SemaphoreType.DMA((n, 3)), pltpu.SemaphoreType.DMA((n, 3)),
                        pltpu.SemaphoreType.DMA((7,)), pltpu.SemaphoreType.DMA((7,)),
                        pltpu.SemaphoreType.DMA((n + 1,))],
    )(*ps, small)
    return res[:n], res[n]


def _share_cores(halves, small):
    n = len(halves)

    def body(*refs):
        srcs, small_src, outs, small_out = refs[:n], refs[n], refs[n + 1:2 * n + 1], refs[2 * n + 1]
        send_sems, recv_sems, ssend, srecv, local_sems = refs[2 * n + 2:]
        x, y, c = _coords()
        me = 4 * x + 2 * y + c

        def big(i, arriving):
            return pltpu.make_async_remote_copy(
                src_ref=srcs[i], dst_ref=outs[i].at[1 - c if arriving else c],
                send_sem=send_sems.at[i], recv_sem=recv_sems.at[i], device_id=(x, y, 1 - c), device_id_type=MESH)

        def tiny(k, arriving):
            px, py, pc = _dev_peer(k, x, y, c)
            return pltpu.make_async_remote_copy(
                src_ref=small_src, dst_ref=small_out.at[4 * px + 2 * py + pc if arriving else me],
                send_sem=ssend.at[k], recv_sem=srecv.at[k], device_id=(px, py, pc), device_id_type=MESH)

        local = [pltpu.make_async_copy(srcs[i], outs[i].at[c], local_sems.at[i]) for i in range(n)]
        local.append(pltpu.make_async_copy(small_src, small_out.at[me], local_sems.at[n]))
        sends = [big(i, False) for i in range(n)] + [tiny(k, False) for k in range(7)]
        for cp in local + sends:
            cp.start()
        for i in range(n):
            big(i, True).wait_recv()
        for k in range(7):
            tiny(k, True).wait_recv()
        for cp in sends:
            cp.wait_send()
        for cp in local:
            cp.wait()

    res = pl.pallas_call(
        body, name="share_cores",
        out_shape=[jax.ShapeDtypeStruct((2,) + s.shape, s.dtype) for s in halves]
        + [jax.ShapeDtypeStruct((8,) + small.shape, small.dtype)],
        in_specs=[HBM_SPEC] * (n + 1), out_specs=[HBM_SPEC] * (n + 1),
        scratch_shapes=[pltpu.SemaphoreType.DMA((n,)), pltpu.SemaphoreType.DMA((n,)),
                        pltpu.SemaphoreType.DMA((7,)), pltpu.SemaphoreType.DMA((7,)),
                        pltpu.SemaphoreType.DMA((n + 1,))],
    )(*halves, small)
    return res[:n], res[n]


def _row_tile(n, target):
    return max(d for d in range(8, min(n, target) + 1, 8) if n % d == 0)


def _ew(name, fn, ins, outs, block_bytes=1 << 20):
    rows, cols = ins[0].shape[-2:]
    lead = max(math.prod(a.shape[:-2]) for a in ins)
    tr = _row_tile(rows, max(8, block_bytes // (4 * cols * lead)))
    n = len(ins)

    def spec(shape):
        if len(shape) == 2:
            return pl.BlockSpec((tr, cols), lambda i: (i, 0))
        return pl.BlockSpec((shape[0], tr, cols), lambda i: (0, i, 0))

    def body(*refs):
        res = fn(*[r[...] for r in refs[:n]])
        for r, v in zip(refs[n:], res):
            r[...] = v

    return pl.pallas_call(
        body, name=name,
        out_shape=[jax.ShapeDtypeStruct(s, F32) for s in outs],
        grid=(rows // tr,),
        in_specs=[spec(a.shape) for a in ins],
        out_specs=[spec(s) for s in outs],
        compiler_params=_params(("parallel",)),
    )(*ins)


def _sum_slots(a):
    total = a[0]
    for s in range(1, a.shape[0]):
        total = total + a[s]
    return (total,)


def _adamw(g, w, m, v):
    bc1 = 1.0 - ADAM_B1 ** ADAM_STEP
    bc2 = 1.0 - ADAM_B2 ** ADAM_STEP
    m_new = ADAM_B1 * m + (1.0 - ADAM_B1) * g
    v_new = ADAM_B2 * v + (1.0 - ADAM_B2) * jnp.square(g)
    delta = -ADAM_LR * ((m_new / bc1) / (jnp.sqrt(v_new / bc2) + ADAM_EPS) + ADAM_WD * w)
    return delta, m_new, v_new


def _mat(a):
    return a.reshape(a.shape[-2:])


def _to_shard_major(full, axis):
    rows, cols = full.shape
    if axis == 0:
        return full.reshape(4, rows // 4, cols)
    return full.reshape(rows, 4, cols // 4).transpose(1, 0, 2)


def _from_shard_major(a, axis):
    _, r, cs = a.shape
    if axis == 0:
        return a.reshape(4 * r, cs)
    return a.transpose(1, 0, 2).reshape(r, 4 * cs)


def _pack_small(arrays):
    flat = jnp.concatenate([arrays[n].reshape(-1) for n in SMALL_NAMES])
    flat = jnp.pad(flat, (0, 8 * SMALL_ROWS * PACK_COLS - flat.shape[0]))
    return flat.reshape(8, SMALL_ROWS, PACK_COLS)


def _unpack_small(packed, shapes):
    flat = packed.reshape(-1)
    out, off = {}, 0
    for n in SMALL_NAMES:
        size = math.prod(shapes[n])
        out[n] = flat[off:off + size].reshape(shapes[n])
        off += size
    return out


def _to_heads(a):
    t = a.shape[0]
    return a.reshape(t, HEADS, HEAD).transpose(1, 0, 2)


def _from_heads(a):
    return a.transpose(1, 0, 2).reshape(a.shape[1], HEADS * HEAD)


def _row(a):
    return a.reshape(1, -1)


def _local_step(x, p, target, wf, ws):
    t = x.shape[0]
    tm = min(256, t)
    g = {}

    lam_re, lam_im = ws['s5_lam_re'].reshape(S5_GROUPS, S5_STATE), ws['s5_lam_im'].reshape(S5_GROUPS, S5_STATE)
    log_step = ws['s5_log_step'].reshape(S5_GROUPS, 1)
    gp = (S5_GROUPS, S5_STATE)
    lam_ins = (lam_re, lam_im, log_step)
    lbr, lbi, cfr, cfi = _small_fwd("s5_lam", _f_s5_lam, lam_ins, [(gp, F32)] * 4)
    lam_row = jnp.concatenate([_row(lbr), _row(lbi)], axis=1)
    to_t = lambda a, perm: a.reshape((S5_GROUPS,) + a.shape[-2:]).transpose(perm).reshape(S5_GROUP, S5_LANES)
    build_ins = (_row(cfr), _row(cfi), to_t(ws['s5_b_re'], (2, 0, 1)), to_t(ws['s5_b_im'], (2, 0, 1)),
                 to_t(ws['s5_c_re'], (1, 0, 2)), to_t(ws['s5_c_im'], (1, 0, 2)))
    stack_shape = (S5_WIDTH, 2 * S5_LANES)
    bstack, cstack_t = _small_fwd("s5_build", _f_s5_build, build_ins, [(stack_shape, F32)] * 2)

    norm_mix, norm_ffn, norm_ple = _row(ws['norm_mix']), _row(ws['norm_ffn']), _row(ws['norm_ple'])
    final_norm = _row(ws['final_norm'])
    (xn,) = _tok_fwd("norm_in", _f_norm_in, [x], [norm_mix], [(x.shape[1], BF16)], tm)
    proj = _mm("proj", xn, wf['w_in'], 'nn')
    u, z = proj[:, :S5_WIDTH], proj[:, S5_WIDTH:]

    bu = _mm("s5_bu", u, bstack, 'nn', precise=True)
    xs, xs_prev = _s5_scan(bu, lam_row, tm)
    ypre = _mm("s5_y", xs, cstack_t, 'nt', precise=True)
    s5_par = [_row(ws['s5_d']), wf['s5_glu_w'], _row(ws['s5_glu_b'])]
    (s5_out,) = _tok_fwd("s5_post", _f_s5_post, [ypre, u], s5_par, [(S5_WIDTH, BF16)], tm)

    pre_par = [_row(ws['rw_shift_mu']), _row(ws['rw_w0']), wf['rw_w2'], _row(ws['rw_a0']), wf['rw_a2'],
               wf['rw_g2'], _row(ws['rw_k_k']), _row(ws['rw_k_a'])]
    r, lw, kp, v, an, bn, gate = _rw_pre_fwd(z, pre_par, tm)
    seqs = [_to_heads(s) for s in (r, lw, kp, v, an, bn)]
    y_heads, ck = _wkv_fwd(seqs)
    y_wkv = _from_heads(y_heads)
    post_par = [_row(ws['rw_ln_w']), _row(ws['rw_ln_b']), _row(ws['rw_r_k'])]
    post_toks = [y_wkv, r, kp, v, gate]
    (rw_out,) = _tok_fwd("rw_post", _f_rw_post, post_toks, post_par, [(RWKV_WIDTH, BF16)], tm)

    mixcat = jnp.concatenate([s5_out, rw_out], axis=1)
    mixed = _mm("mix_out", mixcat, wf['w_out'], 'nn')
    h1, hn = _tok_fwd("mix_res", _f_mix_res, [x, mixed], [norm_ffn], [(x.shape[1], F32), (x.shape[1], BF16)], tm)
    w13 = jnp.concatenate([wf['ffn_w1'], wf['ffn_w3']], axis=1)
    a13 = _mm("ffn_up", hn, w13, 'nn')
    (f,) = _tok_fwd("ffn_act", _f_ffn_act, [a13], [], [(FFN_HIDDEN, BF16)], tm)
    ffo = _mm("ffn_down", f, wf['ffn_w2'], 'nn')
    h2, hp = _tok_fwd("ffn_res", _f_ffn_res, [h1, ffo], [norm_ple], [(x.shape[1], F32), (x.shape[1], BF16)], tm)
    gpre = _mm("ple_gate", hp, wf['ple_gate_w'], 'nn')
    pu = _mm("ple_up", p, wf['ple_up_w'], 'nn')

    dh2, dgpre, dpu, g['final_norm'], loss = _tok_bwd(
        "loss", _f_loss, [h2, gpre, pu, target], [final_norm], [None],
        [F32, BF16, BF16, None], [True], tm, acc_out=0)
    g['ple_gate_w'] = _mm("d_ple_gate_w", hp, dgpre, 'tn')
    g['ple_up_w'] = _mm("d_ple_up_w", p, dpu, 'tn')
    dhp = _mm("d_hp", dgpre, wf['ple_gate_w'], 'nt')
    dh1, dffo, g['norm_ple'] = _tok_bwd("ffn_res_bwd", _f_ffn_res, [h1, ffo], [norm_ple], [dh2, dhp],
                                        [F32, BF16], [True], tm)
    g['ffn_w2'] = _mm("d_ffn_w2", f, dffo, 'tn')
    df = _mm("d_f", dffo, wf['ffn_w2'], 'nt')
    (da13,) = _tok_bwd("ffn_act_bwd", _f_ffn_act, [a13], [], [df], [BF16], [], tm)
    dw13 = _mm("d_ffn_w13", hn, da13, 'tn')
    g['ffn_w1'], g['ffn_w3'] = dw13[:, :FFN_HIDDEN], dw13[:, FFN_HIDDEN:]
    dhn = _mm("d_hn", da13, w13, 'nt')
    dx_a, dmixed, g['norm_ffn'] = _tok_bwd("mix_res_bwd", _f_mix_res, [x, mixed], [norm_ffn], [dh1, dhn],
                                           [F32, BF16], [True], tm)
    g['w_out'] = _mm("d_w_out", mixcat, dmixed, 'tn')
    dmixcat = _mm("d_mixcat", dmixed, wf['w_out'], 'nt')
    ds5_out, drw_out = dmixcat[:, :S5_WIDTH], dmixcat[:, S5_WIDTH:]

    dy_wkv, dr_b, dkp_b, dv_b, dgate, g['rw_ln_w'], g['rw_ln_b'], g['rw_r_k'] = _tok_bwd(
        "rw_post_bwd", _f_rw_post, post_toks, post_par, [drw_out], [F32] * 5, [True] * 3, tm)
    dseqs = [_from_heads(d) for d in _wkv_bwd(seqs, ck, _to_heads(dy_wkv))]
    pre_cots = [dseqs[0] + dr_b, dseqs[1], dseqs[2] + dkp_b, dseqs[3] + dv_b, dseqs[4], dseqs[5], dgate]
    dz, *dpre = _rw_pre_bwd(z, pre_par, pre_cots, tm)
    for n, d in zip(['rw_shift_mu', 'rw_w0', 'rw_w2', 'rw_a0', 'rw_a2', 'rw_g2', 'rw_k_k', 'rw_k_a'], dpre):
        g[n] = d

    dypre, du_a, g['s5_d'], g['s5_glu_w'], g['s5_glu_b'] = _tok_bwd(
        "s5_post_bwd", _f_s5_post, [ypre, u], s5_par, [ds5_out], [F32, F32], [True] * 3, tm)
    dxs = _mm("d_s5_x", dypre, cstack_t, 'nn', precise=True)
    dcstack_t = _mm("d_s5_c", dypre, xs, 'tn', precise=True)
    dbu, dlam_row = _s5_scan_bwd(dxs, xs_prev, lam_row, tm)
    du_b = _mm("d_s5_u", dbu, bstack, 'nt', precise=True)
    dbstack = _mm("d_s5_b", u, dbu, 'tn', precise=True)
    dbuild = _small_bwd("s5_build_bwd", _f_s5_build, build_ins, (dbstack, dcstack_t))
    lam_cots = (dlam_row[:, :S5_LANES].reshape(gp), dlam_row[:, S5_LANES:].reshape(gp),
                dbuild[0].reshape(gp), dbuild[1].reshape(gp))
    g['s5_lam_re'], g['s5_lam_im'], g['s5_log_step'] = _small_bwd("s5_lam_bwd", _f_s5_lam, lam_ins, lam_cots)
    from_t = lambda a, perm: a.reshape(S5_GROUP, S5_GROUPS, S5_STATE).transpose(perm)
    g['s5_b_re'], g['s5_b_im'] = from_t(dbuild[2], (1, 2, 0)), from_t(dbuild[3], (1, 2, 0))
    g['s5_c_re'], g['s5_c_im'] = from_t(dbuild[4], (1, 0, 2)), from_t(dbuild[5], (1, 0, 2))

    dproj = jnp.concatenate([(du_a + du_b).astype(BF16), dz.astype(BF16)], axis=1)
    g['w_in'] = _mm("d_w_in", xn, dproj, 'tn')
    dxn = _mm("d_xn", dproj, wf['w_in'], 'nt')
    grad_x, g['norm_mix'] = _tok_bwd("norm_in_bwd", _f_norm_in, [x], [norm_mix], [dxn], [F32], [True], tm,
                                     add_to=(0, dx_a))
    return loss[0, 0], grad_x, g


def _step(x, p, target, w, m, v):
    gathered = _gather_weights([_mat(w[n]).astype(BF16) for n in SHARDED_NAMES])
    wf = {n: _from_shard_major(a, SHARDED[n]) for n, a in zip(SHARDED_NAMES, gathered)}
    ws = {n: w[n] for n in SMALL_NAMES}

    loss, grad_x, g = _local_step(x[0], p[0, 0], target[0], wf, ws)

    own, other = _reduce_cores([_to_shard_major(g[n], SHARDED[n]) for n in SHARDED_NAMES])
    add = lambda a, b: (a + b,)
    ps = [_ew("add_cores_" + n, add, [a, b], [a.shape])[0] for n, a, b in zip(SHARDED_NAMES, own, other)]
    by_chip, by_dev = _exchange_chips(ps, _pack_small({n: g[n] for n in SMALL_NAMES}))
    halves = [_ew("add_chips_" + n, _sum_slots, [a], [a.shape[1:]])[0] for n, a in zip(SHARDED_NAMES, by_chip)]
    (small_piece,) = _ew("add_devices", _sum_slots, [by_dev], [by_dev.shape[1:]])
    both, small_g = _share_cores(halves, small_piece)

    kinds = [{}, {}, {}, {}]
    for n, gn in zip(SHARDED_NAMES, both):
        shard = _mat(w[n]).shape
        res = _ew("adamw_" + n, _adamw, [gn.reshape(shard), _mat(w[n]), _mat(m[n]), _mat(v[n])], [shard] * 3)
        for kind, a in zip(kinds, [gn] + list(res)):
            kind[n] = a.reshape(w[n].shape)
    flat = (8 * SMALL_ROWS, PACK_COLS)
    packed = [_pack_small({n: d[n] for n in SMALL_NAMES}).reshape(flat) for d in (w, m, v)]
    small_res = _ew("adamw_small", _adamw, [small_g.reshape(flat)] + packed, [flat] * 3)
    small_shapes = {n: w[n].shape for n in SMALL_NAMES}
    for kind, a in zip(kinds, [small_g] + list(small_res)):
        kind.update(_unpack_small(a, small_shapes))
    total = lax.psum(loss, ("x", "y", "c"))
    return (total, grad_x[None], *[kind[n] for kind in kinds for n in WEIGHT_NAMES])


def kernel(x, p, norm_mix, w_in, s5_lam_re, s5_lam_im, s5_log_step, s5_b_re, s5_b_im, s5_c_re, s5_c_im, s5_d, s5_glu_w, s5_glu_b, rw_shift_mu, rw_w0, rw_w2, rw_a0, rw_a2, rw_g2, rw_k_k, rw_k_a, rw_r_k, rw_ln_w, rw_ln_b, w_out, norm_ffn, ffn_w1, ffn_w3, ffn_w2, norm_ple, ple_gate_w, ple_up_w, final_norm, loss_target, m_norm_mix, m_w_in, m_s5_lam_re, m_s5_lam_im, m_s5_log_step, m_s5_b_re, m_s5_b_im, m_s5_c_re, m_s5_c_im, m_s5_d, m_s5_glu_w, m_s5_glu_b, m_rw_shift_mu, m_rw_w0, m_rw_w2, m_rw_a0, m_rw_a2, m_rw_g2, m_rw_k_k, m_rw_k_a, m_rw_r_k, m_rw_ln_w, m_rw_ln_b, m_w_out, m_norm_ffn, m_ffn_w1, m_ffn_w3, m_ffn_w2, m_norm_ple, m_ple_gate_w, m_ple_up_w, m_final_norm, v_norm_mix, v_w_in, v_s5_lam_re, v_s5_lam_im, v_s5_log_step, v_s5_b_re, v_s5_b_im, v_s5_c_re, v_s5_c_im, v_s5_d, v_s5_glu_w, v_s5_glu_b, v_rw_shift_mu, v_rw_w0, v_rw_w2, v_rw_a0, v_rw_a2, v_rw_g2, v_rw_k_k, v_rw_k_a, v_rw_r_k, v_rw_ln_w, v_rw_ln_b, v_w_out, v_norm_ffn, v_ffn_w1, v_ffn_w3, v_ffn_w2, v_norm_ple, v_ple_gate_w, v_ple_up_w, v_final_norm):
    args = dict(locals())
    w = {n: args[n] for n in WEIGHT_NAMES}
    m = {n: args["m_" + n] for n in WEIGHT_NAMES}
    v = {n: args["v_" + n] for n in WEIGHT_NAMES}
    return _step(x, p, loss_target, w, m, v)
```

```python
import functools
import math
from typing import Any, Callable, NamedTuple, Sequence

import jax
import jax.numpy as jnp
from jax import lax
from jax.experimental import pallas as pl
from jax.experimental.pallas import tpu as pltpu

F32 = jnp.float32
BF16 = jnp.bfloat16
MESH = pl.DeviceIdType.MESH

S5_WIDTH = 512
RWKV_WIDTH = 512
S5_GROUP = 16
S5_GROUPS = 32
S5_STATE = 64
S5_LANES = S5_GROUPS * S5_STATE
HEAD = 64
HEADS = 8
DECAY_LORA = 64
AAA_LORA = 64
GATE_LORA = 128
FFN_HIDDEN = 2816
RMS_EPS = 1e-6
GN_EPS = 64e-5
L2_EPS = 1e-12
ADAM_LR = 0.001
ADAM_B1 = 0.9
ADAM_B2 = 0.999
ADAM_EPS = 1e-08
ADAM_WD = 0.01
ADAM_STEP = 10

WKV_CHUNK = 64
WKV_INTERLEAVE = 8
VMEM_LIMIT_BYTES = 48 * 1024 * 1024
LANE = 128
PACK_COLS = 1024
SMALL_ROWS = 24

WEIGHT_NAMES = ['norm_mix', 'w_in', 's5_lam_re', 's5_lam_im', 's5_log_step', 's5_b_re', 's5_b_im', 's5_c_re',
                's5_c_im', 's5_d', 's5_glu_w', 's5_glu_b', 'rw_shift_mu', 'rw_w0', 'rw_w2', 'rw_a0', 'rw_a2',
                'rw_g2', 'rw_k_k', 'rw_k_a', 'rw_r_k', 'rw_ln_w', 'rw_ln_b', 'w_out', 'norm_ffn', 'ffn_w1',
                'ffn_w3', 'ffn_w2', 'norm_ple', 'ple_gate_w', 'ple_up_w', 'final_norm']
SHARDED = {'w_in': 1, 's5_glu_w': 0, 'rw_w2': 1, 'rw_a2': 1, 'rw_g2': 1, 'w_out': 0, 'ffn_w1': 1, 'ffn_w3': 1,
           'ffn_w2': 0, 'ple_gate_w': 0, 'ple_up_w': 1}
SHARDED_NAMES = [n for n in WEIGHT_NAMES if n in SHARDED]
LATE_NAMES = ['w_out', 'ffn_w1', 'ffn_w3', 'ffn_w2', 'ple_gate_w', 'ple_up_w']
EARLY_NAMES = [n for n in SHARDED_NAMES if n not in LATE_NAMES]
SMALL_NAMES = [n for n in WEIGHT_NAMES if n not in SHARDED]


def _params(sem=None):
    return pltpu.CompilerParams(dimension_semantics=sem, vmem_limit_bytes=VMEM_LIMIT_BYTES)


def _tile(n, target):
    best = None
    for d in range(LANE, min(n, target) + 1, LANE):
        if n % d == 0:
            best = d
    return n if best is None else best


_NN = (((1,), (0,)), ((), ()))
_NT = (((1,), (1,)), ((), ()))
_TN = (((0,), (0,)), ((), ()))


def _split(a):
    a = a.astype(F32)
    hi = a.astype(BF16)
    return hi, (a - hi.astype(F32)).astype(BF16)


def _dg(a, b, dims, precise):
    if precise:
        (ah, al), (bh, bl) = _split(a), _split(b)
        dg = lambda p, q: lax.dot_general(p, q, dims, preferred_element_type=F32)
        return dg(ah, bh) + (dg(ah, bl) + dg(al, bh))
    return lax.dot_general(a.astype(BF16), b.astype(BF16), dims, preferred_element_type=F32)


@jax.custom_vjp
def _bdot(x, w):
    return _dg(x, w, _NN, False)


def _bdot_fwd(x, w):
    return _dg(x, w, _NN, False), (x, w)


def _bdot_bwd(res, g):
    x, w = res
    return _dg(g, w, _NT, False), _dg(x, g, _TN, False)


_bdot.defvjp(_bdot_fwd, _bdot_bwd)


_FDOT_BWD = {_NN: (("g", "b", _NT), ("a", "g", _TN)),
             _NT: (("g", "b", _NN), ("g", "a", _TN)),
             _TN: (("b", "g", _NT), ("a", "g", _NN))}


@functools.partial(jax.custom_vjp, nondiff_argnums=(2,))
def _fdot(a, b, dims=_NN):
    return _dg(a, b, dims, True)


def _fdot_fwd(a, b, dims):
    return _dg(a, b, dims, True), (a, b)


def _fdot_bwd(dims, res, g):
    env = {"a": res[0], "b": res[1], "g": g}
    return tuple(_dg(env[p], env[q], d, True) for p, q, d in _FDOT_BWD[dims])


_fdot.defvjp(_fdot_fwd, _fdot_bwd)


@jax.custom_vjp
def _shift_down(z):
    return pltpu.roll(z, 1, 0)


def _shift_down_fwd(z):
    return pltpu.roll(z, 1, 0), None


def _shift_down_bwd(_, g):
    return (pltpu.roll(g, g.shape[0] - 1, 0),)


_shift_down.defvjp(_shift_down_fwd, _shift_down_bwd)


def _head_ones():
    r = lax.broadcasted_iota(jnp.int32, (RWKV_WIDTH, RWKV_WIDTH), 0) // HEAD
    c = lax.broadcasted_iota(jnp.int32, (RWKV_WIDTH, RWKV_WIDTH), 1) // HEAD
    return (r == c).astype(F32)


def _mm(name, a, b, mode, out_dtype=F32, precise=False, tm=1024, tn=1024, tk=1536):
    if mode == 'nn':
        (m, k), (_, n) = a.shape, b.shape
    elif mode == 'nt':
        (m, k), (n, _) = a.shape, b.shape
    else:
        (k, m), (_, n) = a.shape, b.shape
    tm, tn, tk = _tile(m, tm), _tile(n, tn), _tile(k, tk)
    nk = k // tk
    dims = {'nn': _NN, 'nt': _NT, 'tn': _TN}[mode]

    def body(a_ref, b_ref, o_ref, acc_ref):
        kk = pl.program_id(2)

        @pl.when(kk == 0)
        def _():
            acc_ref[...] = jnp.zeros_like(acc_ref)

        acc_ref[...] += _dg(a_ref[...], b_ref[...], dims, precise)

        @pl.when(kk == nk - 1)
        def _():
            o_ref[...] = acc_ref[...].astype(o_ref.dtype)

    if mode == 'tn':
        a_spec = pl.BlockSpec((tk, tm), lambda i, j, l: (l, i))
    else:
        a_spec = pl.BlockSpec((tm, tk), lambda i, j, l: (i, l))
    if mode == 'nt':
        b_spec = pl.BlockSpec((tn, tk), lambda i, j, l: (j, l))
    else:
        b_spec = pl.BlockSpec((tk, tn), lambda i, j, l: (l, j))
    return pl.pallas_call(
        body, name=name,
        out_shape=jax.ShapeDtypeStruct((m, n), out_dtype),
        grid=(m // tm, n // tn, nk),
        in_specs=[a_spec, b_spec],
        out_specs=pl.BlockSpec((tm, tn), lambda i, j, l: (i, j)),
        scratch_shapes=[pltpu.VMEM((tm, tn), F32)],
        compiler_params=_params(("parallel", "parallel", "arbitrary")),
    )(a, b)


def _full_spec(p):
    nd = p.ndim
    return pl.BlockSpec(p.shape, lambda i, nd=nd: (0,) * nd)


def _tok_fwd(name, fn, toks, params, outs, tm):
    t = toks[0].shape[0]
    nt, npar = len(toks), len(params)

    def body(*refs):
        tv = [r[...].astype(F32) for r in refs[:nt]]
        pv = [r[...].astype(F32) for r in refs[nt:nt + npar]]
        res = fn(*tv, *pv)
        for r, v in zip(refs[nt + npar:], res):
            r[...] = v.astype(r.dtype)

    return pl.pallas_call(
        body, name=name,
        out_shape=[jax.ShapeDtypeStruct((t, w), d) for w, d in outs],
        grid=(t // tm,),
        in_specs=[pl.BlockSpec((tm, a.shape[1]), lambda i: (i, 0)) for a in toks] + [_full_spec(p) for p in params],
        out_specs=[pl.BlockSpec((tm, w), lambda i: (i, 0)) for w, _ in outs],
        compiler_params=_params(("parallel",)),
    )(*toks, *params)


def _tok_bwd(name, fn, toks, params, cots, dtok, dpar, tm, acc_out=None, add_to=None):
    t = toks[0].shape[0]
    nt, npar = len(toks), len(params)
    cot_arrays = [c for c in cots if c is not None]
    ncot = len(cot_arrays)
    extra = [] if add_to is None else [add_to[1]]
    dtok_idx = [i for i, d in enumerate(dtok) if d is not None]
    dpar_idx = [i for i, d in enumerate(dpar) if d]

    def body(*refs):
        pos = 0
        tin = refs[pos:pos + nt]; pos += nt
        pin = refs[pos:pos + npar]; pos += npar
        cin = refs[pos:pos + ncot]; pos += ncot
        ein = refs[pos:pos + len(extra)]; pos += len(extra)
        dto = refs[pos:pos + len(dtok_idx)]; pos += len(dtok_idx)
        dpo = refs[pos:pos + len(dpar_idx)]; pos += len(dpar_idx)
        acc = refs[pos] if acc_out is not None else None
        first = pl.program_id(0) == 0

        tv = [r[...].astype(F32) for r in tin]
        pv = [r[...].astype(F32) for r in pin]
        res, vjp = jax.vjp(fn, *tv, *pv)
        cit = iter(cin)
        cs = tuple(jnp.ones_like(o) if c is None else next(cit)[...].astype(F32) for c, o in zip(cots, res))
        g = vjp(cs)
        for r, i in zip(dto, dtok_idx):
            v = g[i]
            if add_to is not None and add_to[0] == i:
                v = v + ein[0][...].astype(F32)
            r[...] = v.astype(r.dtype)

        @pl.when(first)
        def _():
            for r in dpo:
                r[...] = jnp.zeros_like(r)
            if acc is not None:
                acc[...] = jnp.zeros_like(acc)

        for r, i in zip(dpo, dpar_idx):
            r[...] += g[nt + i]
        if acc is not None:
            acc[...] += res[acc_out]

    out_shape = [jax.ShapeDtypeStruct(toks[i].shape, dtok[i]) for i in dtok_idx]
    out_shape += [jax.ShapeDtypeStruct(params[i].shape, F32) for i in dpar_idx]
    out_specs = [pl.BlockSpec((tm, toks[i].shape[1]), lambda i_: (i_, 0)) for i in dtok_idx]
    out_specs += [_full_spec(params[i]) for i in dpar_idx]
    if acc_out is not None:
        out_shape.append(jax.ShapeDtypeStruct((1, 1), F32))
        out_specs.append(pl.BlockSpec((1, 1), lambda i_: (0, 0)))
    tok_spec = lambda a: pl.BlockSpec((tm, a.shape[1]), lambda i_: (i_, 0))
    return pl.pallas_call(
        body, name=name,
        out_shape=out_shape,
        grid=(t // tm,),
        in_specs=[tok_spec(a) for a in toks] + [_full_spec(p) for p in params]
        + [tok_spec(c) for c in cot_arrays] + [tok_spec(e) for e in extra],
        out_specs=out_specs,
        compiler_params=_params(("arbitrary",)),
    )(*toks, *params, *cot_arrays, *extra)


def _small_fwd(name, fn, ins, outs):
    n = len(ins)

    def body(*refs):
        res = fn(*[r[...] for r in refs[:n]])
        for r, v in zip(refs[n:], res):
            r[...] = v.astype(r.dtype)

    return pl.pallas_call(
        body, name=name,
        out_shape=[jax.ShapeDtypeStruct(s, d) for s, d in outs],
        compiler_params=_params(),
    )(*ins)


def _small_bwd(name, fn, ins, cots):
    n = len(ins)

    def body(*refs):
        _, vjp = jax.vjp(fn, *[r[...] for r in refs[:n]])
        g = vjp(tuple(r[...] for r in refs[n:n + len(cots)]))
        for r, v in zip(refs[n + len(cots):], g):
            r[...] = v

    return pl.pallas_call(
        body, name=name,
        out_shape=[jax.ShapeDtypeStruct(a.shape, F32) for a in ins],
        compiler_params=_params(),
    )(*ins, *cots)


def _rms(x, g):
    return x * lax.rsqrt(jnp.mean(x * x, axis=-1, keepdims=True) + RMS_EPS) * g


def _f_norm_in(x, g):
    return (_rms(x, g),)


def _f_mix_res(x, mixed, g):
    h1 = x + mixed
    return h1, _rms(h1, g)


def _f_ffn_act(a13):
    a1, a3 = a13[:, :FFN_HIDDEN], a13[:, FFN_HIDDEN:]
    return (jax.nn.silu(a1) * a3,)


def _f_ffn_res(h1, ffo, g):
    h2 = h1 + ffo
    return h2, _rms(h2, g)


def _f_loss(h2, gpre, pu, target, g):
    h3 = h2 + jax.nn.sigmoid(gpre) * pu
    y = _rms(h3, g)
    err = jnp.square(y - target)
    return (0.5 * jnp.sum(jnp.mean(err, axis=-1, keepdims=True), axis=0, keepdims=True),)


def _f_s5_post(ypre, u, d, glu_w, glu_b):
    z = jax.nn.gelu(ypre + u * d)
    return (z * jax.nn.sigmoid(_bdot(z, glu_w) + glu_b),)


def _softplus(x):
    return jnp.maximum(x, 0.0) + jnp.log(1.0 + jnp.exp(-jnp.abs(x)))


def _f_rw_pre(z, carry, shift_mu, w0, w2, a0, a2, g2, k_k, k_a):
    rw = RWKV_WIDTH
    first_row = lax.broadcasted_iota(jnp.int32, z.shape, 0) == 0
    prev = jnp.where(first_row, carry, _shift_down(z))
    zs = z + (prev - z) * shift_mu
    o1, o2 = 3 * rw + DECAY_LORA, 3 * rw + DECAY_LORA + AAA_LORA
    r, k, v = zs[:, :rw], zs[:, rw:2 * rw], zs[:, 2 * rw:3 * rw]
    wl, al, gl = zs[:, 3 * rw:o1], zs[:, o1:o2], zs[:, o2:]
    w = -_softplus(-(w0 + _bdot(jnp.tanh(wl), w2))) - 0.5
    log_decay = -jnp.exp(w)
    a = jax.nn.sigmoid(a0 + _bdot(al, a2))
    g = _bdot(jax.nn.sigmoid(gl), g2)
    kk = k * k_k
    norm = jnp.sqrt(_fdot(kk * kk, _head_ones()))
    kk = kk / jnp.maximum(norm, L2_EPS)
    kp = k * (1.0 + (a - 1.0) * k_a)
    return r, log_decay, kp, v, -kk, kk * a, g


def _f_rw_post(y, r, kp, v, g, ln_w, ln_b, r_k):
    ones = _head_ones()
    yc = y - _fdot(y, ones) * (1.0 / HEAD)
    var = _fdot(yc * yc, ones) * (1.0 / HEAD)
    yn = yc * lax.rsqrt(var + GN_EPS) * ln_w + ln_b
    bonus = _fdot(r * kp * r_k, ones) * v
    return ((yn + bonus) * g,)


def _f_s5_lam(lam_re, lam_im, log_step):
    step = jnp.exp(log_step)
    dr, di = lam_re * step, lam_im * step
    e = jnp.exp(dr)
    lbr, lbi = e * jnp.cos(di), e * jnp.sin(di)
    nr, ni = lbr - 1.0, lbi
    den = lam_re * lam_re + lam_im * lam_im
    return lbr, lbi, (nr * lam_re + ni * lam_im) / den, (ni * lam_re - nr * lam_im) / den


def _f_s5_build(coef_r, coef_i, btr, bti, ctr, cti):
    bbr = coef_r * btr - coef_i * bti
    bbi = coef_r * bti + coef_i * btr
    rows = lax.broadcasted_iota(jnp.int32, (S5_WIDTH, S5_LANES), 0) // S5_GROUP
    cols = lax.broadcasted_iota(jnp.int32, (S5_WIDTH, S5_LANES), 1) // S5_STATE
    mask = (rows == cols).astype(F32)
    rep = lambda m: jnp.concatenate([m] * S5_GROUPS, axis=0) * mask
    return (jnp.concatenate([rep(bbr), rep(bbi)], axis=1), jnp.concatenate([rep(ctr), -rep(cti)], axis=1))


HALO = 8


def _rw_pre_specs(z, params, tm, order):
    halo_blocks = tm // HALO
    return ([pl.BlockSpec((tm, z.shape[1]), lambda i: (order(i), 0)),
             pl.BlockSpec((HALO, z.shape[1]), lambda i: (jnp.maximum(order(i) * halo_blocks - 1, 0), 0))]
            + [_full_spec(p) for p in params])


def _rw_pre_fwd(z, params, tm):
    t = z.shape[0]
    npar = len(params)

    def body(z_ref, halo_ref, *refs):
        carry = jnp.where(pl.program_id(0) == 0, 0.0, halo_ref[pl.ds(HALO - 1, 1), :])
        res = _f_rw_pre(z_ref[...], carry, *[r[...].astype(F32) for r in refs[:npar]])
        for r, v in zip(refs[npar:], res):
            r[...] = v

    return pl.pallas_call(
        body, name="rw_pre",
        out_shape=[jax.ShapeDtypeStruct((t, RWKV_WIDTH), F32)] * 7,
        grid=(t // tm,),
        in_specs=_rw_pre_specs(z, params, tm, lambda i: i),
        out_specs=[pl.BlockSpec((tm, RWKV_WIDTH), lambda i: (i, 0))] * 7,
        compiler_params=_params(("parallel",)),
    )(z, z, *params)


def _rw_pre_bwd(z, params, cots, tm):
    t = z.shape[0]
    nt = t // tm
    npar = len(params)
    order = lambda i: nt - 1 - i

    def body(z_ref, halo_ref, *refs):
        pin, cin = refs[:npar], refs[npar:npar + 7]
        dz_ref = refs[npar + 7]
        dpo = refs[npar + 8:npar + 8 + npar]
        dcarry_ref = refs[npar + 8 + npar]
        i = pl.program_id(0)

        @pl.when(i == 0)
        def _():
            dcarry_ref[...] = jnp.zeros_like(dcarry_ref)
            for r in dpo:
                r[...] = jnp.zeros_like(r)

        carry = jnp.where(i == nt - 1, 0.0, halo_ref[pl.ds(HALO - 1, 1), :])
        _, vjp = jax.vjp(_f_rw_pre, z_ref[...], carry, *[r[...].astype(F32) for r in pin])
        g = vjp(tuple(c[...] for c in cin))
        last_row = lax.broadcasted_iota(jnp.int32, z_ref.shape, 0) == tm - 1
        dz_ref[...] = g[0] + jnp.where(last_row, dcarry_ref[...], 0.0)
        dcarry_ref[...] = g[1]
        for r, v in zip(dpo, g[2:]):
            r[...] += v

    tok = lambda w: pl.BlockSpec((tm, w), lambda i: (order(i), 0))
    return pl.pallas_call(
        body, name="rw_pre_bwd",
        out_shape=[jax.ShapeDtypeStruct(z.shape, F32)] + [jax.ShapeDtypeStruct(p.shape, F32) for p in params],
        grid=(nt,),
        in_specs=_rw_pre_specs(z, params, tm, order) + [tok(RWKV_WIDTH)] * 7,
        out_specs=[tok(z.shape[1])] + [_full_spec(p) for p in params],
        scratch_shapes=[pltpu.VMEM((1, z.shape[1]), F32)],
        compiler_params=_params(("arbitrary",)),
    )(z, z, *params, *cots)


def _s5_scan(bu, lam, tm):
    t, w = bu.shape
    h = w // 2

    def body(bu_ref, lam_ref, x_ref, xp_ref, carry_ref):
        @pl.when(pl.program_id(0) == 0)
        def _():
            carry_ref[...] = jnp.zeros_like(carry_ref)

        lr, li = lam_ref[:, :h], lam_ref[:, h:]

        def step(s, c):
            cr, ci = c
            row = pl.ds(s, 1)
            xp_ref[row, :h] = cr
            xp_ref[row, h:] = ci
            nr = lr * cr - li * ci + bu_ref[row, :h]
            ni = lr * ci + li * cr + bu_ref[row, h:]
            x_ref[row, :h] = nr
            x_ref[row, h:] = ni
            return nr, ni

        cr, ci = lax.fori_loop(0, tm, step, (carry_ref[:, :h], carry_ref[:, h:]))
        carry_ref[:, :h] = cr
        carry_ref[:, h:] = ci

    spec = pl.BlockSpec((tm, w), lambda i: (i, 0))
    return pl.pallas_call(
        body, name="s5_scan",
        out_shape=[jax.ShapeDtypeStruct((t, w), F32)] * 2,
        grid=(t // tm,),
        in_specs=[spec, pl.BlockSpec((1, w), lambda i: (0, 0))],
        out_specs=[spec, spec],
        scratch_shapes=[pltpu.VMEM((1, w), F32)],
        compiler_params=_params(("arbitrary",)),
    )(bu, lam)


def _s5_scan_bwd(dx, xp, lam, tm):
    t, w = dx.shape
    h = w // 2
    nt = t // tm

    def body(dx_ref, xp_ref, lam_ref, dbu_ref, dlam_ref, carry_ref):
        @pl.when(pl.program_id(0) == 0)
        def _():
            carry_ref[...] = jnp.zeros_like(carry_ref)
            dlam_ref[...] = jnp.zeros_like(dlam_ref)

        lr, li = lam_ref[:, :h], lam_ref[:, h:]

        def step(s, c):
            cr, ci = c
            row = pl.ds(tm - 1 - s, 1)
            nr = lr * cr + li * ci + dx_ref[row, :h]
            ni = lr * ci - li * cr + dx_ref[row, h:]
            dbu_ref[row, :h] = nr
            dbu_ref[row, h:] = ni
            return nr, ni

        cr, ci = lax.fori_loop(0, tm, step, (carry_ref[:, :h], carry_ref[:, h:]))
        carry_ref[:, :h] = cr
        carry_ref[:, h:] = ci
        gr, gi = dbu_ref[:, :h], dbu_ref[:, h:]
        pr, pi_ = xp_ref[:, :h], xp_ref[:, h:]
        dlam_ref[:, :h] += jnp.sum(gr * pr + gi * pi_, axis=0, keepdims=True)
        dlam_ref[:, h:] += jnp.sum(gi * pr - gr * pi_, axis=0, keepdims=True)

    spec = pl.BlockSpec((tm, w), lambda i: (nt - 1 - i, 0))
    row_spec = pl.BlockSpec((1, w), lambda i: (0, 0))
    return pl.pallas_call(
        body, name="s5_scan_bwd",
        out_shape=[jax.ShapeDtypeStruct((t, w), F32), jax.ShapeDtypeStruct((1, w), F32)],
        grid=(nt,),
        in_specs=[spec, spec, row_spec],
        out_specs=[spec, row_spec],
        scratch_shapes=[pltpu.VMEM((1, w), F32)],
        compiler_params=_params(("arbitrary",)),
    )(dx, xp, lam)


def _wkv_chunks(s0, r, lw, k, v, a, b):
    c = r[0].shape[0]
    row = lax.broadcasted_iota(jnp.int32, (c, c), 0)
    col = lax.broadcasted_iota(jnp.int32, (c, c), 1)
    incl, strict = col <= row, col < row
    tri = incl.astype(F32)
    eye = (row == col).astype(F32)
    each = lambda f, *xs: [f(*t) for t in zip(*xs)]
    lc = each(lambda l: _fdot(tri, l), lw)
    e_neg = each(lambda l: jnp.exp(-l), lc)
    rt = each(lambda x, l: x * jnp.exp(l), r, lc)
    at = each(lambda x, l, w: x * jnp.exp(l - w), a, lc, lw)
    kt = each(lambda x, e: x * e, k, e_neg)
    bt = each(lambda x, e: x * e, b, e_neg)
    mab = each(lambda p, q: jnp.where(strict, _fdot(p, q, _NT), 0.0), at, bt)
    mak = each(lambda p, q: jnp.where(strict, _fdot(p, q, _NT), 0.0), at, kt)
    mrb = each(lambda p, q: jnp.where(incl, _fdot(p, q, _NT), 0.0), rt, bt)
    mrk = each(lambda p, q: jnp.where(incl, _fdot(p, q, _NT), 0.0), rt, kt)
    xs = each(lambda p, s, m, q: _fdot(p, s, _NT) + _fdot(m, q), at, s0, mak, v)
    ys = each(lambda p, s, m, q: _fdot(p, s, _NT) + _fdot(m, q), rt, s0, mrk, v)
    sk = each(lambda q, p: _fdot(q, p, _TN), v, kt)
    inv = each(lambda m: eye + m, mab)
    pw = mab
    for _ in range(int(math.log2(c)) - 1):
        pw = each(lambda m: _fdot(m, m), pw)
        inv = each(lambda i, m: i + _fdot(i, m), inv, pw)
    u = each(_fdot, inv, xs)
    y = each(lambda y0, m, q: y0 + _fdot(m, q), ys, mrb, u)
    e_tot = each(lambda l: jnp.exp(jnp.sum(l, axis=0, keepdims=True)), lw)
    s1 = each(lambda s, q, p, z, e: (s + _fdot(q, p, _TN) + z) * e, s0, u, bt, sk, e_tot)
    return y, s1


def _carry(plan, n_args, n_outs):
    n_in, n_out = len(plan.ins), len(plan.out_shape)

    def parts(refs):
        base = n_args + n_in + n_outs
        return refs[n_args:n_args + n_in], refs[base:base + n_out], refs[base + n_out + 1:]

    return parts, [HBM_SPEC] * n_in, list(plan.out_shape), [HBM_SPEC] * n_out, list(plan.sems)


def _wkv_fwd(seqs, plan):
    hh, t, n = seqs[0].shape
    c = WKV_CHUNK
    nc = t // c
    parts, plan_in_specs, plan_out_shape, plan_out_specs, plan_sems = _carry(plan, 6, 2)

    def body(*refs):
        ins, (y_ref, ck_ref) = refs[:6], refs[6 + len(plan.ins):8 + len(plan.ins)]
        s_ref = refs[8 + len(plan.ins) + len(plan.out_shape)]

        @pl.when(pl.program_id(0) == 0)
        def _():
            s_ref[...] = jnp.zeros_like(s_ref)
            plan.start(*parts(refs))

        def heads(j, carry):
            hs = [j * WKV_INTERLEAVE + q for q in range(WKV_INTERLEAVE)]
            s0 = tuple(s_ref[h] for h in hs)
            ys, s1 = _wkv_chunks(s0, *[tuple(r[h] for h in hs) for r in ins])
            for q, h in enumerate(hs):
                ck_ref[0, h] = s0[q]
                y_ref[h] = ys[q]
                s_ref[h] = s1[q]
            return carry

        lax.fori_loop(0, hh // WKV_INTERLEAVE, heads, 0)

        @pl.when(pl.program_id(0) == nc - 1)
        def _():
            plan.wait(*parts(refs))

    spec = pl.BlockSpec((hh, c, n), lambda i: (0, i, 0))
    res = pl.pallas_call(
        body, name="wkv_fwd",
        out_shape=[jax.ShapeDtypeStruct((hh, t, n), F32), jax.ShapeDtypeStruct((nc, hh, n, n), F32)] + plan_out_shape,
        grid=(nc,),
        in_specs=[spec] * 6 + plan_in_specs,
        out_specs=[spec, pl.BlockSpec((1, hh, n, n), lambda i: (i, 0, 0, 0))] + plan_out_specs,
        scratch_shapes=[pltpu.VMEM((hh, n, n), F32)] + plan_sems,
        compiler_params=_params(("arbitrary",)),
    )(*seqs, *plan.ins)
    return res[0], res[1], res[2:]


def _wkv_bwd(seqs, ck, dy, plan):
    hh, t, n = seqs[0].shape
    c = WKV_CHUNK
    nc = t // c
    parts, plan_in_specs, plan_out_shape, plan_out_specs, plan_sems = _carry(plan, 8, 6)

    def body(*refs):
        ins, ck_ref, dy_ref = refs[:6], refs[6], refs[7]
        outs = refs[8 + len(plan.ins):14 + len(plan.ins)]
        ds_ref = refs[14 + len(plan.ins) + len(plan.out_shape)]

        @pl.when(pl.program_id(0) == 0)
        def _():
            ds_ref[...] = jnp.zeros_like(ds_ref)
            plan.start(*parts(refs))

        def heads(j, carry):
            hs = [j * WKV_INTERLEAVE + q for q in range(WKV_INTERLEAVE)]
            s0 = tuple(ck_ref[0, h] for h in hs)
            _, vjp = jax.vjp(_wkv_chunks, s0, *[tuple(r[h] for h in hs) for r in ins])
            g = vjp(([dy_ref[h] for h in hs], [ds_ref[h] for h in hs]))
            for q, h in enumerate(hs):
                ds_ref[h] = g[0][q]
                for o, d in zip(outs, g[1:]):
                    o[h] = d[q]
            return carry

        lax.fori_loop(0, hh // WKV_INTERLEAVE, heads, 0)

        @pl.when(pl.program_id(0) == nc - 1)
        def _():
            plan.wait(*parts(refs))

    spec = pl.BlockSpec((hh, c, n), lambda i: (0, nc - 1 - i, 0))
    res = pl.pallas_call(
        body, name="wkv_bwd",
        out_shape=[jax.ShapeDtypeStruct((hh, t, n), F32)] * 6 + plan_out_shape,
        grid=(nc,),
        in_specs=[spec] * 6 + [pl.BlockSpec((1, hh, n, n), lambda i: (nc - 1 - i, 0, 0, 0)), spec] + plan_in_specs,
        out_specs=[spec] * 6 + plan_out_specs,
        scratch_shapes=[pltpu.VMEM((hh, n, n), F32)] + plan_sems,
        compiler_params=_params(("arbitrary",)),
    )(*seqs, ck, dy, *plan.ins)
    return res[:6], res[6:]


def _coords():
    return lax.axis_index("x"), lax.axis_index("y"), lax.axis_index("c")


def _flip(v, f):
    return 1 - v if f else v


_CHIP_FLIPS = [(1, 0), (0, 1), (1, 1)]
_DEV_FLIPS = [(fx, fy, fc) for fx in (0, 1) for fy in (0, 1) for fc in (0, 1) if (fx, fy, fc) != (0, 0, 0)]
HBM_SPEC = pl.BlockSpec(memory_space=pl.ANY)


def _chip_peer(k, x, y):
    fx, fy = _CHIP_FLIPS[k]
    return _flip(x, fx), _flip(y, fy)


def _dev_peer(k, x, y, c):
    fx, fy, fc = _DEV_FLIPS[k]
    return _flip(x, fx), _flip(y, fy), _flip(c, fc)


def _rows_of_core(ref, core):
    h = ref.shape[-2] // 2
    rows = pl.ds(pl.multiple_of(core * h, 8), h)
    return ref.at[rows, :] if len(ref.shape) == 2 else ref.at[:, rows, :]


class _Plan(NamedTuple):
    ins: Sequence[Any]
    out_shape: Sequence[Any]
    sems: Sequence[Any]
    start: Callable
    wait: Callable


def _run_plan(name, plan):
    n_in, n_out = len(plan.ins), len(plan.out_shape)

    def body(*refs):
        parts = refs[:n_in], refs[n_in:n_in + n_out], refs[n_in + n_out:]
        plan.start(*parts)
        plan.wait(*parts)

    return pl.pallas_call(
        body, name=name, out_shape=list(plan.out_shape),
        in_specs=[HBM_SPEC] * n_in, out_specs=[HBM_SPEC] * n_out, scratch_shapes=list(plan.sems),
    )(*plan.ins)


def _gather_plan(shards):
    n = len(shards)

    def copies(srcs, outs, sems):
        send_sems, recv_sems, local_sems = sems
        x, y, c = _coords()
        me = 2 * x + y

        def remote(i, k, arriving):
            px, py = _chip_peer(k, x, y)
            return pltpu.make_async_remote_copy(
                src_ref=srcs[i], dst_ref=outs[i].at[2 * px + py if arriving else me],
                send_sem=send_sems.at[i, k], recv_sem=recv_sems.at[i, k],
                device_id=(px, py, c), device_id_type=MESH)

        own = [pltpu.make_async_copy(srcs[i], outs[i].at[me], local_sems.at[i]) for i in range(n)]
        pairs = [(i, k) for k in range(3) for i in range(n)]
        return own, [remote(i, k, False) for i, k in pairs], [remote(i, k, True) for i, k in pairs]

    return _Plan(
        ins=shards, out_shape=[jax.ShapeDtypeStruct((4,) + s.shape, s.dtype) for s in shards],
        sems=[pltpu.SemaphoreType.DMA((n, 3)), pltpu.SemaphoreType.DMA((n, 3)), pltpu.SemaphoreType.DMA((n,))],
        start=functools.partial(_start_copies, copies), wait=functools.partial(_wait_copies, copies))


def _start_copies(copies, ins, outs, sems):
    own, sends, _ = copies(ins, outs, sems)
    for cp in own + sends:
        cp.start()


def _wait_copies(copies, ins, outs, sems):
    own, sends, arrivals = copies(ins, outs, sems)
    for cp in arrivals:
        cp.wait_recv()
    for cp in sends:
        cp.wait_send()
    for cp in own:
        cp.wait()


def _exchange_plan(gs, small=None):
    n = len(gs)
    arrays = list(gs) + ([] if small is None else [small])

    def copies(srcs, outs, sems):
        send_sems, recv_sems, local_sems = sems
        x, y, c = _coords()
        me = 4 * x + 2 * y + c

        def piece(i, px, py, pc):
            if i == n:
                return srcs[i].at[4 * px + 2 * py + pc]
            return _rows_of_core(srcs[i].at[2 * px + py], pc)

        def remote(i, k, arriving):
            px, py, pc = _dev_peer(k, x, y, c)
            return pltpu.make_async_remote_copy(
                src_ref=piece(i, px, py, pc), dst_ref=outs[i].at[4 * px + 2 * py + pc if arriving else me],
                send_sem=send_sems.at[i, k], recv_sem=recv_sems.at[i, k],
                device_id=(px, py, pc), device_id_type=MESH)

        own = [pltpu.make_async_copy(piece(i, x, y, c), outs[i].at[me], local_sems.at[i]) for i in range(len(arrays))]
        pairs = [(i, k) for k in range(7) for i in range(len(arrays))]
        return own, [remote(i, k, False) for i, k in pairs], [remote(i, k, True) for i, k in pairs]

    out_shape = [jax.ShapeDtypeStruct((8, g.shape[1] // 2, g.shape[2]), g.dtype) for g in gs]
    if small is not None:
        out_shape.append(jax.ShapeDtypeStruct(small.shape, small.dtype))
    m = len(arrays)
    return _Plan(
        ins=arrays, out_shape=out_shape,
        sems=[pltpu.SemaphoreType.DMA((m, 7)), pltpu.SemaphoreType.DMA((m, 7)), pltpu.SemaphoreType.DMA((m,))],
        start=functools.partial(_start_copies, copies), wait=functools.partial(_wait_copies, copies))


def _share_cores(halves, small):
    n = len(halves)

    def body(*refs):
        srcs, small_src, outs, small_out = refs[:n], refs[n], refs[n + 1:2 * n + 1], refs[2 * n + 1]
        send_sems, recv_sems, ssend, srecv, local_sems = refs[2 * n + 2:]
        x, y, c = _coords()
        me = 4 * x + 2 * y + c

        def big(i, arriving):
            return pltpu.make_async_remote_copy(
                src_ref=srcs[i], dst_ref=outs[i].at[1 - c if arriving else c],
                send_sem=send_sems.at[i], recv_sem=recv_sems.at[i], device_id=(x, y, 1 - c), device_id_type=MESH)

        def tiny(k, arriving):
            px, py, pc = _dev_peer(k, x, y, c)
            return pltpu.make_async_remote_copy(
                src_ref=small_src, dst_ref=small_out.at[4 * px + 2 * py + pc if arriving else me],
                send_sem=ssend.at[k], recv_sem=srecv.at[k], device_id=(px, py, pc), device_id_type=MESH)

        local = [pltpu.make_async_copy(srcs[i], outs[i].at[c], local_sems.at[i]) for i in range(n)]
        local.append(pltpu.make_async_copy(small_src, small_out.at[me], local_sems.at[n]))
        sends = [big(i, False) for i in range(n)] + [tiny(k, False) for k in range(7)]
        for cp in local + sends:
            cp.start()
        for i in range(n):
            big(i, True).wait_recv()
        for k in range(7):
            tiny(k, True).wait_recv()
        for cp in sends:
            cp.wait_send()
        for cp in local:
            cp.wait()

    res = pl.pallas_call(
        body, name="share_cores",
        out_shape=[jax.ShapeDtypeStruct((2,) + s.shape, s.dtype) for s in halves]
        + [jax.ShapeDtypeStruct((8,) + small.shape, small.dtype)],
        in_specs=[HBM_SPEC] * (n + 1), out_specs=[HBM_SPEC] * (n + 1),
        scratch_shapes=[pltpu.SemaphoreType.DMA((n,)), pltpu.SemaphoreType.DMA((n,)),
                        pltpu.SemaphoreType.DMA((7,)), pltpu.SemaphoreType.DMA((7,)),
                        pltpu.SemaphoreType.DMA((n + 1,))],
    )(*halves, small)
    return res[:n], res[n]


def _row_tile(n, target):
    return max(d for d in range(8, min(n, target) + 1, 8) if n % d == 0)


def _ew(name, fn, ins, outs, block_bytes=1 << 20):
    rows, cols = ins[0].shape[-2:]
    lead = max(math.prod(a.shape[:-2]) for a in ins)
    tr = _row_tile(rows, max(8, block_bytes // (4 * cols * lead)))
    n = len(ins)

    def spec(shape):
        if len(shape) == 2:
            return pl.BlockSpec((tr, cols), lambda i: (i, 0))
        return pl.BlockSpec((shape[0], tr, cols), lambda i: (0, i, 0))

    def body(*refs):
        res = fn(*[r[...] for r in refs[:n]])
        for r, v in zip(refs[n:], res):
            r[...] = v

    return pl.pallas_call(
        body, name=name,
        out_shape=[jax.ShapeDtypeStruct(s, F32) for s in outs],
        grid=(rows // tr,),
        in_specs=[spec(a.shape) for a in ins],
        out_specs=[spec(s) for s in outs],
        compiler_params=_params(("parallel",)),
    )(*ins)


def _sum_slots(a):
    total = a[0]
    for s in range(1, a.shape[0]):
        total = total + a[s]
    return (total,)


def _adamw(g, w, m, v):
    bc1 = 1.0 - ADAM_B1 ** ADAM_STEP
    bc2 = 1.0 - ADAM_B2 ** ADAM_STEP
    m_new = ADAM_B1 * m + (1.0 - ADAM_B1) * g
    v_new = ADAM_B2 * v + (1.0 - ADAM_B2) * jnp.square(g)
    delta = -ADAM_LR * ((m_new / bc1) / (jnp.sqrt(v_new / bc2) + ADAM_EPS) + ADAM_WD * w)
    return delta, m_new, v_new


def _mat(a):
    return a.reshape(a.shape[-2:])


def _to_shard_major(full, axis):
    rows, cols = full.shape
    if axis == 0:
        return full.reshape(4, rows // 4, cols)
    return full.reshape(rows, 4, cols // 4).transpose(1, 0, 2)


def _from_shard_major(a, axis):
    _, r, cs = a.shape
    if axis == 0:
        return a.reshape(4 * r, cs)
    return a.transpose(1, 0, 2).reshape(r, 4 * cs)


def _pack_small(arrays):
    flat = jnp.concatenate([arrays[n].reshape(-1) for n in SMALL_NAMES])
    flat = jnp.pad(flat, (0, 8 * SMALL_ROWS * PACK_COLS - flat.shape[0]))
    return flat.reshape(8, SMALL_ROWS, PACK_COLS)


def _unpack_small(packed, shapes):
    flat = packed.reshape(-1)
    out, off = {}, 0
    for n in SMALL_NAMES:
        size = math.prod(shapes[n])
        out[n] = flat[off:off + size].reshape(shapes[n])
        off += size
    return out


def _to_heads(a):
    t = a.shape[0]
    return a.reshape(t, HEADS, HEAD).transpose(1, 0, 2)


def _from_heads(a):
    return a.transpose(1, 0, 2).reshape(a.shape[1], HEADS * HEAD)


def _row(a):
    return a.reshape(1, -1)


def _local_step(x, p, target, wf, ws, late_gather):
    wf = dict(wf)
    t = x.shape[0]
    tm = min(256, t)
    g = {}

    lam_re, lam_im = ws['s5_lam_re'].reshape(S5_GROUPS, S5_STATE), ws['s5_lam_im'].reshape(S5_GROUPS, S5_STATE)
    log_step = ws['s5_log_step'].reshape(S5_GROUPS, 1)
    gp = (S5_GROUPS, S5_STATE)
    lam_ins = (lam_re, lam_im, log_step)
    lbr, lbi, cfr, cfi = _small_fwd("s5_lam", _f_s5_lam, lam_ins, [(gp, F32)] * 4)
    lam_row = jnp.concatenate([_row(lbr), _row(lbi)], axis=1)
    to_t = lambda a, perm: a.reshape((S5_GROUPS,) + a.shape[-2:]).transpose(perm).reshape(S5_GROUP, S5_LANES)
    build_ins = (_row(cfr), _row(cfi), to_t(ws['s5_b_re'], (2, 0, 1)), to_t(ws['s5_b_im'], (2, 0, 1)),
                 to_t(ws['s5_c_re'], (1, 0, 2)), to_t(ws['s5_c_im'], (1, 0, 2)))
    stack_shape = (S5_WIDTH, 2 * S5_LANES)
    bstack, cstack_t = _small_fwd("s5_build", _f_s5_build, build_ins, [(stack_shape, F32)] * 2)

    norm_mix, norm_ffn, norm_ple = _row(ws['norm_mix']), _row(ws['norm_ffn']), _row(ws['norm_ple'])
    final_norm = _row(ws['final_norm'])
    (xn,) = _tok_fwd("norm_in", _f_norm_in, [x], [norm_mix], [(x.shape[1], BF16)], tm)
    proj = _mm("proj", xn, wf['w_in'], 'nn')
    u, z = proj[:, :S5_WIDTH], proj[:, S5_WIDTH:]

    bu = _mm("s5_bu", u, bstack, 'nn')
    xs, xs_prev = _s5_scan(bu, lam_row, tm)
    ypre = _mm("s5_y", xs, cstack_t, 'nt')
    s5_par = [_row(ws['s5_d']), wf['s5_glu_w'], _row(ws['s5_glu_b'])]
    (s5_out,) = _tok_fwd("s5_post", _f_s5_post, [ypre, u], s5_par, [(S5_WIDTH, BF16)], tm)

    pre_par = [_row(ws['rw_shift_mu']), _row(ws['rw_w0']), wf['rw_w2'], _row(ws['rw_a0']), wf['rw_a2'],
               wf['rw_g2'], _row(ws['rw_k_k']), _row(ws['rw_k_a'])]
    r, lw, kp, v, an, bn, gate = _rw_pre_fwd(z, pre_par, tm)
    seqs = [_to_heads(s) for s in (r, lw, kp, v, an, bn)]
    y_heads, ck, late = _wkv_fwd(seqs, late_gather)
    wf.update({n: _from_shard_major(a, SHARDED[n]) for n, a in zip(LATE_NAMES, late)})
    y_wkv = _from_heads(y_heads)
    post_par = [_row(ws['rw_ln_w']), _row(ws['rw_ln_b']), _row(ws['rw_r_k'])]
    post_toks = [y_wkv, r, kp, v, gate]
    (rw_out,) = _tok_fwd("rw_post", _f_rw_post, post_toks, post_par, [(RWKV_WIDTH, BF16)], tm)

    mixcat = jnp.concatenate([s5_out, rw_out], axis=1)
    mixed = _mm("mix_out", mixcat, wf['w_out'], 'nn')
    h1, hn = _tok_fwd("mix_res", _f_mix_res, [x, mixed], [norm_ffn], [(x.shape[1], F32), (x.shape[1], BF16)], tm)
    w13 = jnp.concatenate([wf['ffn_w1'], wf['ffn_w3']], axis=1)
    a13 = _mm("ffn_up", hn, w13, 'nn')
    (f,) = _tok_fwd("ffn_act", _f_ffn_act, [a13], [], [(FFN_HIDDEN, BF16)], tm)
    ffo = _mm("ffn_down", f, wf['ffn_w2'], 'nn')
    h2, hp = _tok_fwd("ffn_res", _f_ffn_res, [h1, ffo], [norm_ple], [(x.shape[1], F32), (x.shape[1], BF16)], tm)
    gpre = _mm("ple_gate", hp, wf['ple_gate_w'], 'nn')
    pu = _mm("ple_up", p, wf['ple_up_w'], 'nn')

    dh2, dgpre, dpu, g['final_norm'], loss = _tok_bwd(
        "loss", _f_loss, [h2, gpre, pu, target], [final_norm], [None],
        [F32, BF16, BF16, None], [True], tm, acc_out=0)
    g['ple_gate_w'] = _mm("d_ple_gate_w", hp, dgpre, 'tn')
    g['ple_up_w'] = _mm("d_ple_up_w", p, dpu, 'tn')
    dhp = _mm("d_hp", dgpre, wf['ple_gate_w'], 'nt')
    dh1, dffo, g['norm_ple'] = _tok_bwd("ffn_res_bwd", _f_ffn_res, [h1, ffo], [norm_ple], [dh2, dhp],
                                        [F32, BF16], [True], tm)
    g['ffn_w2'] = _mm("d_ffn_w2", f, dffo, 'tn')
    df = _mm("d_f", dffo, wf['ffn_w2'], 'nt')
    (da13,) = _tok_bwd("ffn_act_bwd", _f_ffn_act, [a13], [], [df], [BF16], [], tm)
    dw13 = _mm("d_ffn_w13", hn, da13, 'tn')
    g['ffn_w1'], g['ffn_w3'] = dw13[:, :FFN_HIDDEN], dw13[:, FFN_HIDDEN:]
    dhn = _mm("d_hn", da13, w13, 'nt')
    dx_a, dmixed, g['norm_ffn'] = _tok_bwd("mix_res_bwd", _f_mix_res, [x, mixed], [norm_ffn], [dh1, dhn],
                                           [F32, BF16], [True], tm)
    g['w_out'] = _mm("d_w_out", mixcat, dmixed, 'tn')
    dmixcat = _mm("d_mixcat", dmixed, wf['w_out'], 'nt')
    ds5_out, drw_out = dmixcat[:, :S5_WIDTH], dmixcat[:, S5_WIDTH:]

    dy_wkv, dr_b, dkp_b, dv_b, dgate, g['rw_ln_w'], g['rw_ln_b'], g['rw_r_k'] = _tok_bwd(
        "rw_post_bwd", _f_rw_post, post_toks, post_par, [drw_out], [F32] * 5, [True] * 3, tm)
    late_exchange = _exchange_plan([_to_shard_major(g[n], SHARDED[n]) for n in LATE_NAMES])
    dheads, late_pieces = _wkv_bwd(seqs, ck, _to_heads(dy_wkv), late_exchange)
    dseqs = [_from_heads(d) for d in dheads]
    pre_cots = [dseqs[0] + dr_b, dseqs[1], dseqs[2] + dkp_b, dseqs[3] + dv_b, dseqs[4], dseqs[5], dgate]
    dz, *dpre = _rw_pre_bwd(z, pre_par, pre_cots, tm)
    for n, d in zip(['rw_shift_mu', 'rw_w0', 'rw_w2', 'rw_a0', 'rw_a2', 'rw_g2', 'rw_k_k', 'rw_k_a'], dpre):
        g[n] = d

    dypre, du_a, g['s5_d'], g['s5_glu_w'], g['s5_glu_b'] = _tok_bwd(
        "s5_post_bwd", _f_s5_post, [ypre, u], s5_par, [ds5_out], [F32, F32], [True] * 3, tm)
    dxs = _mm("d_s5_x", dypre, cstack_t, 'nn')
    dcstack_t = _mm("d_s5_c", dypre, xs, 'tn')
    dbu, dlam_row = _s5_scan_bwd(dxs, xs_prev, lam_row, tm)
    du_b = _mm("d_s5_u", dbu, bstack, 'nt')
    dbstack = _mm("d_s5_b", u, dbu, 'tn')
    dbuild = _small_bwd("s5_build_bwd", _f_s5_build, build_ins, (dbstack, dcstack_t))
    lam_cots = (dlam_row[:, :S5_LANES].reshape(gp), dlam_row[:, S5_LANES:].reshape(gp),
                dbuild[0].reshape(gp), dbuild[1].reshape(gp))
    g['s5_lam_re'], g['s5_lam_im'], g['s5_log_step'] = _small_bwd("s5_lam_bwd", _f_s5_lam, lam_ins, lam_cots)
    from_t = lambda a, perm: a.reshape(S5_GROUP, S5_GROUPS, S5_STATE).transpose(perm)
    g['s5_b_re'], g['s5_b_im'] = from_t(dbuild[2], (1, 2, 0)), from_t(dbuild[3], (1, 2, 0))
    g['s5_c_re'], g['s5_c_im'] = from_t(dbuild[4], (1, 0, 2)), from_t(dbuild[5], (1, 0, 2))

    dproj = jnp.concatenate([(du_a + du_b).astype(BF16), dz.astype(BF16)], axis=1)
    g['w_in'] = _mm("d_w_in", xn, dproj, 'tn')
    dxn = _mm("d_xn", dproj, wf['w_in'], 'nt')
    grad_x, g['norm_mix'] = _tok_bwd("norm_in_bwd", _f_norm_in, [x], [norm_mix], [dxn], [F32], [True], tm,
                                     add_to=(0, dx_a))
    return loss[0, 0], grad_x, g, late_pieces


def _step(x, p, target, w, m, v):
    shards = {n: _mat(w[n]).astype(BF16) for n in SHARDED_NAMES}
    early = _run_plan("gather_early", _gather_plan([shards[n] for n in EARLY_NAMES]))
    wf = {n: _from_shard_major(a, SHARDED[n]) for n, a in zip(EARLY_NAMES, early)}
    ws = {n: w[n] for n in SMALL_NAMES}

    late_gather = _gather_plan([shards[n] for n in LATE_NAMES])
    loss, grad_x, g, late_pieces = _local_step(x[0], p[0, 0], target[0], wf, ws, late_gather)

    early_plan = _exchange_plan([_to_shard_major(g[n], SHARDED[n]) for n in EARLY_NAMES],
                                _pack_small({n: g[n] for n in SMALL_NAMES}))
    *early_pieces, by_dev = _run_plan("exchange_early", early_plan)
    pieces = dict(zip(LATE_NAMES + EARLY_NAMES, list(late_pieces) + early_pieces))
    halves = [_ew("add_devices_" + n, _sum_slots, [pieces[n]], [pieces[n].shape[1:]])[0] for n in SHARDED_NAMES]
    (small_piece,) = _ew("add_devices_small", _sum_slots, [by_dev], [by_dev.shape[1:]])
    both, small_g = _share_cores(halves, small_piece)

    kinds = [{}, {}, {}, {}]
    for n, gn in zip(SHARDED_NAMES, both):
        shard = _mat(w[n]).shape
        res = _ew("adamw_" + n, _adamw, [gn.reshape(shard), _mat(w[n]), _mat(m[n]), _mat(v[n])], [shard] * 3)
        for kind, a in zip(kinds, [gn] + list(res)):
            kind[n] = a.reshape(w[n].shape)
    flat = (8 * SMALL_ROWS, PACK_COLS)
    packed = [_pack_small({n: d[n] for n in SMALL_NAMES}).reshape(flat) for d in (w, m, v)]
    small_res = _ew("adamw_small", _adamw, [small_g.reshape(flat)] + packed, [flat] * 3)
    small_shapes = {n: w[n].shape for n in SMALL_NAMES}
    for kind, a in zip(kinds, [small_g] + list(small_res)):
        kind.update(_unpack_small(a, small_shapes))
    total = lax.psum(loss, ("x", "y", "c"))
    return (total, grad_x[None], *[kind[n] for kind in kinds for n in WEIGHT_NAMES])


def kernel(x, p, norm_mix, w_in, s5_lam_re, s5_lam_im, s5_log_step, s5_b_re, s5_b_im, s5_c_re, s5_c_im, s5_d, s5_glu_w, s5_glu_b, rw_shift_mu, rw_w0, rw_w2, rw_a0, rw_a2, rw_g2, rw_k_k, rw_k_a, rw_r_k, rw_ln_w, rw_ln_b, w_out, norm_ffn, ffn_w1, ffn_w3, ffn_w2, norm_ple, ple_gate_w, ple_up_w, final_norm, loss_target, m_norm_mix, m_w_in, m_s5_lam_re, m_s5_lam_im, m_s5_log_step, m_s5_b_re, m_s5_b_im, m_s5_c_re, m_s5_c_im, m_s5_d, m_s5_glu_w, m_s5_glu_b, m_rw_shift_mu, m_rw_w0, m_rw_w2, m_rw_a0, m_rw_a2, m_rw_g2, m_rw_k_k, m_rw_k_a, m_rw_r_k, m_rw_ln_w, m_rw_ln_b, m_w_out, m_norm_ffn, m_ffn_w1, m_ffn_w3, m_ffn_w2, m_norm_ple, m_ple_gate_w, m_ple_up_w, m_final_norm, v_norm_mix, v_w_in, v_s5_lam_re, v_s5_lam_im, v_s5_log_step, v_s5_b_re, v_s5_b_im, v_s5_c_re, v_s5_c_im, v_s5_d, v_s5_glu_w, v_s5_glu_b, v_rw_shift_mu, v_rw_w0, v_rw_w2, v_rw_a0, v_rw_a2, v_rw_g2, v_rw_k_k, v_rw_k_a, v_rw_r_k, v_rw_ln_w, v_rw_ln_b, v_w_out, v_norm_ffn, v_ffn_w1, v_ffn_w3, v_ffn_w2, v_norm_ple, v_ple_gate_w, v_ple_up_w, v_final_norm):
    args = dict(locals())
    w = {n: args[n] for n in WEIGHT_NAMES}
    m = {n: args["m_" + n] for n in WEIGHT_NAMES}
    v = {n: args["v_" + n] for n in WEIGHT_NAMES}
    return _step(x, p, loss_target, w, m, v)
```

```python
import functools
import math
from typing import Any, Callable, NamedTuple, Sequence

import jax
import jax.numpy as jnp
from jax import lax
from jax.experimental import pallas as pl
from jax.experimental.pallas import tpu as pltpu

F32 = jnp.float32
BF16 = jnp.bfloat16
MESH = pl.DeviceIdType.MESH

S5_WIDTH = 512
RWKV_WIDTH = 512
S5_GROUP = 16
S5_GROUPS = 32
S5_STATE = 64
S5_LANES = S5_GROUPS * S5_STATE
HEAD = 64
HEADS = 8
DECAY_LORA = 64
AAA_LORA = 64
GATE_LORA = 128
FFN_HIDDEN = 2816
RMS_EPS = 1e-6
GN_EPS = 64e-5
L2_EPS = 1e-12
ADAM_LR = 0.001
ADAM_B1 = 0.9
ADAM_B2 = 0.999
ADAM_EPS = 1e-08
ADAM_WD = 0.01
ADAM_STEP = 10

WKV_CHUNK = 64
VMEM_LIMIT_BYTES = 48 * 1024 * 1024
LANE = 128
PACK_COLS = 1024
SMALL_ROWS = 24

WEIGHT_NAMES = ['norm_mix', 'w_in', 's5_lam_re', 's5_lam_im', 's5_log_step', 's5_b_re', 's5_b_im', 's5_c_re',
                's5_c_im', 's5_d', 's5_glu_w', 's5_glu_b', 'rw_shift_mu', 'rw_w0', 'rw_w2', 'rw_a0', 'rw_a2',
                'rw_g2', 'rw_k_k', 'rw_k_a', 'rw_r_k', 'rw_ln_w', 'rw_ln_b', 'w_out', 'norm_ffn', 'ffn_w1',
                'ffn_w3', 'ffn_w2', 'norm_ple', 'ple_gate_w', 'ple_up_w', 'final_norm']
SHARDED = {'w_in': 1, 's5_glu_w': 0, 'rw_w2': 1, 'rw_a2': 1, 'rw_g2': 1, 'w_out': 0, 'ffn_w1': 1, 'ffn_w3': 1,
           'ffn_w2': 0, 'ple_gate_w': 0, 'ple_up_w': 1}
SHARDED_NAMES = [n for n in WEIGHT_NAMES if n in SHARDED]
LATE_NAMES = ['w_out', 'ffn_w1', 'ffn_w3', 'ffn_w2', 'ple_gate_w', 'ple_up_w']
EARLY_NAMES = [n for n in SHARDED_NAMES if n not in LATE_NAMES]
SMALL_NAMES = [n for n in WEIGHT_NAMES if n not in SHARDED]


def _params(sem=None):
    return pltpu.CompilerParams(dimension_semantics=sem, vmem_limit_bytes=VMEM_LIMIT_BYTES)


def _tile(n, target):
    best = None
    for d in range(LANE, min(n, target) + 1, LANE):
        if n % d == 0:
            best = d
    return n if best is None else best


_NN = (((1,), (0,)), ((), ()))
_NT = (((1,), (1,)), ((), ()))
_TN = (((0,), (0,)), ((), ()))


def _split(a):
    a = a.astype(F32)
    hi = a.astype(BF16)
    return hi, (a - hi.astype(F32)).astype(BF16)


def _dg(a, b, dims, precise):
    if precise:
        (ah, al), (bh, bl) = _split(a), _split(b)
        dg = lambda p, q: lax.dot_general(p, q, dims, preferred_element_type=F32)
        return dg(ah, bh) + (dg(ah, bl) + dg(al, bh))
    return lax.dot_general(a.astype(BF16), b.astype(BF16), dims, preferred_element_type=F32)


@jax.custom_vjp
def _bdot(x, w):
    return _dg(x, w, _NN, False)


def _bdot_fwd(x, w):
    return _dg(x, w, _NN, False), (x, w)


def _bdot_bwd(res, g):
    x, w = res
    return _dg(g, w, _NT, False), _dg(x, g, _TN, False)


_bdot.defvjp(_bdot_fwd, _bdot_bwd)


_FDOT_BWD = {_NN: (("g", "b", _NT), ("a", "g", _TN)),
             _NT: (("g", "b", _NN), ("g", "a", _TN)),
             _TN: (("b", "g", _NT), ("a", "g", _NN))}


@functools.partial(jax.custom_vjp, nondiff_argnums=(2,))
def _fdot(a, b, dims=_NN):
    return _dg(a, b, dims, True)


def _fdot_fwd(a, b, dims):
    return _dg(a, b, dims, True), (a, b)


def _fdot_bwd(dims, res, g):
    env = {"a": res[0], "b": res[1], "g": g}
    return tuple(_dg(env[p], env[q], d, True) for p, q, d in _FDOT_BWD[dims])


_fdot.defvjp(_fdot_fwd, _fdot_bwd)


@jax.custom_vjp
def _shift_down(z):
    return pltpu.roll(z, 1, 0)


def _shift_down_fwd(z):
    return pltpu.roll(z, 1, 0), None


def _shift_down_bwd(_, g):
    return (pltpu.roll(g, g.shape[0] - 1, 0),)


_shift_down.defvjp(_shift_down_fwd, _shift_down_bwd)


def _head_ones():
    r = lax.broadcasted_iota(jnp.int32, (RWKV_WIDTH, RWKV_WIDTH), 0) // HEAD
    c = lax.broadcasted_iota(jnp.int32, (RWKV_WIDTH, RWKV_WIDTH), 1) // HEAD
    return (r == c).astype(F32)


def _mm(name, a, b, mode, out_dtype=F32, precise=False, tm=1024, tn=1024, tk=1536):
    if mode == 'nn':
        (m, k), (_, n) = a.shape, b.shape
    elif mode == 'nt':
        (m, k), (n, _) = a.shape, b.shape
    else:
        (k, m), (_, n) = a.shape, b.shape
    tm, tn, tk = _tile(m, tm), _tile(n, tn), _tile(k, tk)
    nk = k // tk
    dims = {'nn': _NN, 'nt': _NT, 'tn': _TN}[mode]

    def body(a_ref, b_ref, o_ref, acc_ref):
        kk = pl.program_id(2)

        @pl.when(kk == 0)
        def _():
            acc_ref[...] = jnp.zeros_like(acc_ref)

        acc_ref[...] += _dg(a_ref[...], b_ref[...], dims, precise)

        @pl.when(kk == nk - 1)
        def _():
            o_ref[...] = acc_ref[...].astype(o_ref.dtype)

    if mode == 'tn':
        a_spec = pl.BlockSpec((tk, tm), lambda i, j, l: (l, i))
    else:
        a_spec = pl.BlockSpec((tm, tk), lambda i, j, l: (i, l))
    if mode == 'nt':
        b_spec = pl.BlockSpec((tn, tk), lambda i, j, l: (j, l))
    else:
        b_spec = pl.BlockSpec((tk, tn), lambda i, j, l: (l, j))
    return pl.pallas_call(
        body, name=name,
        out_shape=jax.ShapeDtypeStruct((m, n), out_dtype),
        grid=(m // tm, n // tn, nk),
        in_specs=[a_spec, b_spec],
        out_specs=pl.BlockSpec((tm, tn), lambda i, j, l: (i, j)),
        scratch_shapes=[pltpu.VMEM((tm, tn), F32)],
        compiler_params=_params(("parallel", "parallel", "arbitrary")),
    )(a, b)


def _full_spec(p):
    nd = p.ndim
    return pl.BlockSpec(p.shape, lambda i, nd=nd: (0,) * nd)


def _tok_fwd(name, fn, toks, params, outs, tm):
    t = toks[0].shape[0]
    nt, npar = len(toks), len(params)

    def body(*refs):
        tv = [r[...].astype(F32) for r in refs[:nt]]
        pv = [r[...].astype(F32) for r in refs[nt:nt + npar]]
        res = fn(*tv, *pv)
        for r, v in zip(refs[nt + npar:], res):
            r[...] = v.astype(r.dtype)

    return pl.pallas_call(
        body, name=name,
        out_shape=[jax.ShapeDtypeStruct((t, w), d) for w, d in outs],
        grid=(t // tm,),
        in_specs=[pl.BlockSpec((tm, a.shape[1]), lambda i: (i, 0)) for a in toks] + [_full_spec(p) for p in params],
        out_specs=[pl.BlockSpec((tm, w), lambda i: (i, 0)) for w, _ in outs],
        compiler_params=_params(("parallel",)),
    )(*toks, *params)


def _tok_bwd(name, fn, toks, params, cots, dtok, dpar, tm, acc_out=None, add_to=None):
    t = toks[0].shape[0]
    nt, npar = len(toks), len(params)
    cot_arrays = [c for c in cots if c is not None]
    ncot = len(cot_arrays)
    extra = [] if add_to is None else [add_to[1]]
    dtok_idx = [i for i, d in enumerate(dtok) if d is not None]
    dpar_idx = [i for i, d in enumerate(dpar) if d]

    def body(*refs):
        pos = 0
        tin = refs[pos:pos + nt]; pos += nt
        pin = refs[pos:pos + npar]; pos += npar
        cin = refs[pos:pos + ncot]; pos += ncot
        ein = refs[pos:pos + len(extra)]; pos += len(extra)
        dto = refs[pos:pos + len(dtok_idx)]; pos += len(dtok_idx)
        dpo = refs[pos:pos + len(dpar_idx)]; pos += len(dpar_idx)
        acc = refs[pos] if acc_out is not None else None
        first = pl.program_id(0) == 0

        tv = [r[...].astype(F32) for r in tin]
        pv = [r[...].astype(F32) for r in pin]
        res, vjp = jax.vjp(fn, *tv, *pv)
        cit = iter(cin)
        cs = tuple(jnp.ones_like(o) if c is None else next(cit)[...].astype(F32) for c, o in zip(cots, res))
        g = vjp(cs)
        for r, i in zip(dto, dtok_idx):
            v = g[i]
            if add_to is not None and add_to[0] == i:
                v = v + ein[0][...].astype(F32)
            r[...] = v.astype(r.dtype)

        @pl.when(first)
        def _():
            for r in dpo:
                r[...] = jnp.zeros_like(r)
            if acc is not None:
                acc[...] = jnp.zeros_like(acc)

        for r, i in zip(dpo, dpar_idx):
            r[...] += g[nt + i]
        if acc is not None:
            acc[...] += res[acc_out]

    out_shape = [jax.ShapeDtypeStruct(toks[i].shape, dtok[i]) for i in dtok_idx]
    out_shape += [jax.ShapeDtypeStruct(params[i].shape, F32) for i in dpar_idx]
    out_specs = [pl.BlockSpec((tm, toks[i].shape[1]), lambda i_: (i_, 0)) for i in dtok_idx]
    out_specs += [_full_spec(params[i]) for i in dpar_idx]
    if acc_out is not None:
        out_shape.append(jax.ShapeDtypeStruct((1, 1), F32))
        out_specs.append(pl.BlockSpec((1, 1), lambda i_: (0, 0)))
    tok_spec = lambda a: pl.BlockSpec((tm, a.shape[1]), lambda i_: (i_, 0))
    return pl.pallas_call(
        body, name=name,
        out_shape=out_shape,
        grid=(t // tm,),
        in_specs=[tok_spec(a) for a in toks] + [_full_spec(p) for p in params]
        + [tok_spec(c) for c in cot_arrays] + [tok_spec(e) for e in extra],
        out_specs=out_specs,
        compiler_params=_params(("arbitrary",)),
    )(*toks, *params, *cot_arrays, *extra)


def _small_fwd(name, fn, ins, outs):
    n = len(ins)

    def body(*refs):
        res = fn(*[r[...] for r in refs[:n]])
        for r, v in zip(refs[n:], res):
            r[...] = v.astype(r.dtype)

    return pl.pallas_call(
        body, name=name,
        out_shape=[jax.ShapeDtypeStruct(s, d) for s, d in outs],
        compiler_params=_params(),
    )(*ins)


def _small_bwd(name, fn, ins, cots):
    n = len(ins)

    def body(*refs):
        _, vjp = jax.vjp(fn, *[r[...] for r in refs[:n]])
        g = vjp(tuple(r[...] for r in refs[n:n + len(cots)]))
        for r, v in zip(refs[n + len(cots):], g):
            r[...] = v

    return pl.pallas_call(
        body, name=name,
        out_shape=[jax.ShapeDtypeStruct(a.shape, F32) for a in ins],
        compiler_params=_params(),
    )(*ins, *cots)


def _rms(x, g):
    return x * lax.rsqrt(jnp.mean(x * x, axis=-1, keepdims=True) + RMS_EPS) * g


def _f_norm_in(x, g):
    return (_rms(x, g),)


def _f_mix_res(x, mixed, g):
    h1 = x + mixed
    return h1, _rms(h1, g)


def _f_ffn_act(a13):
    a1, a3 = a13[:, :FFN_HIDDEN], a13[:, FFN_HIDDEN:]
    return (jax.nn.silu(a1) * a3,)


def _f_ffn_res(h1, ffo, g):
    h2 = h1 + ffo
    return h2, _rms(h2, g)


def _f_loss(h2, gpre, pu, target, g):
    h3 = h2 + jax.nn.sigmoid(gpre) * pu
    y = _rms(h3, g)
    err = jnp.square(y - target)
    return (0.5 * jnp.sum(jnp.mean(err, axis=-1, keepdims=True), axis=0, keepdims=True),)


def _f_s5_post(ypre, u, d, glu_w, glu_b):
    z = jax.nn.gelu(ypre + u * d)
    return (z * jax.nn.sigmoid(_bdot(z, glu_w) + glu_b),)


def _softplus(x):
    return jnp.maximum(x, 0.0) + jnp.log(1.0 + jnp.exp(-jnp.abs(x)))


def _f_rw_pre(z, carry, shift_mu, w0, w2, a0, a2, g2, k_k, k_a):
    rw = RWKV_WIDTH
    first_row = lax.broadcasted_iota(jnp.int32, z.shape, 0) == 0
    prev = jnp.where(first_row, carry, _shift_down(z))
    zs = z + (prev - z) * shift_mu
    o1, o2 = 3 * rw + DECAY_LORA, 3 * rw + DECAY_LORA + AAA_LORA
    r, k, v = zs[:, :rw], zs[:, rw:2 * rw], zs[:, 2 * rw:3 * rw]
    wl, al, gl = zs[:, 3 * rw:o1], zs[:, o1:o2], zs[:, o2:]
    w = -_softplus(-(w0 + _bdot(jnp.tanh(wl), w2))) - 0.5
    log_decay = -jnp.exp(w)
    a = jax.nn.sigmoid(a0 + _bdot(al, a2))
    g = _bdot(jax.nn.sigmoid(gl), g2)
    kk = k * k_k
    norm = jnp.sqrt(_fdot(kk * kk, _head_ones()))
    kk = kk / jnp.maximum(norm, L2_EPS)
    kp = k * (1.0 + (a - 1.0) * k_a)
    return r, log_decay, kp, v, -kk, kk * a, g


def _f_rw_post(y, r, kp, v, g, ln_w, ln_b, r_k):
    ones = _head_ones()
    yc = y - _fdot(y, ones) * (1.0 / HEAD)
    var = _fdot(yc * yc, ones) * (1.0 / HEAD)
    yn = yc * lax.rsqrt(var + GN_EPS) * ln_w + ln_b
    bonus = _fdot(r * kp * r_k, ones) * v
    return ((yn + bonus) * g,)


def _f_s5_lam(lam_re, lam_im, log_step):
    step = jnp.exp(log_step)
    dr, di = lam_re * step, lam_im * step
    e = jnp.exp(dr)
    lbr, lbi = e * jnp.cos(di), e * jnp.sin(di)
    nr, ni = lbr - 1.0, lbi
    den = lam_re * lam_re + lam_im * lam_im
    return lbr, lbi, (nr * lam_re + ni * lam_im) / den, (ni * lam_re - nr * lam_im) / den


def _f_s5_build(coef_r, coef_i, btr, bti, ctr, cti):
    bbr = coef_r * btr - coef_i * bti
    bbi = coef_r * bti + coef_i * btr
    rows = lax.broadcasted_iota(jnp.int32, (S5_WIDTH, S5_LANES), 0) // S5_GROUP
    cols = lax.broadcasted_iota(jnp.int32, (S5_WIDTH, S5_LANES), 1) // S5_STATE
    mask = (rows == cols).astype(F32)
    rep = lambda m: jnp.concatenate([m] * S5_GROUPS, axis=0) * mask
    return (jnp.concatenate([rep(bbr), rep(bbi)], axis=1), jnp.concatenate([rep(ctr), -rep(cti)], axis=1))


HALO = 8


def _rw_pre_specs(z, params, tm, order):
    halo_blocks = tm // HALO
    return ([pl.BlockSpec((tm, z.shape[1]), lambda i: (order(i), 0)),
             pl.BlockSpec((HALO, z.shape[1]), lambda i: (jnp.maximum(order(i) * halo_blocks - 1, 0), 0))]
            + [_full_spec(p) for p in params])


def _rw_pre_fwd(z, params, tm):
    t = z.shape[0]
    npar = len(params)

    def body(z_ref, halo_ref, *refs):
        carry = jnp.where(pl.program_id(0) == 0, 0.0, halo_ref[pl.ds(HALO - 1, 1), :])
        res = _f_rw_pre(z_ref[...], carry, *[r[...].astype(F32) for r in refs[:npar]])
        for r, v in zip(refs[npar:], res):
            r[...] = v

    return pl.pallas_call(
        body, name="rw_pre",
        out_shape=[jax.ShapeDtypeStruct((t, RWKV_WIDTH), F32)] * 7,
        grid=(t // tm,),
        in_specs=_rw_pre_specs(z, params, tm, lambda i: i),
        out_specs=[pl.BlockSpec((tm, RWKV_WIDTH), lambda i: (i, 0))] * 7,
        compiler_params=_params(("parallel",)),
    )(z, z, *params)


def _rw_pre_bwd(z, params, cots, tm):
    t = z.shape[0]
    nt = t // tm
    npar = len(params)
    order = lambda i: nt - 1 - i
    flat_cots = [a for group in cots for a in group]
    ncot = len(flat_cots)

    def body(z_ref, halo_ref, *refs):
        pin, cin = refs[:npar], list(refs[npar:npar + ncot])
        dz_ref = refs[npar + ncot]
        dpo = refs[npar + ncot + 1:npar + ncot + 1 + npar]
        dcarry_ref = refs[npar + ncot + 1 + npar]
        i = pl.program_id(0)

        @pl.when(i == 0)
        def _():
            dcarry_ref[...] = jnp.zeros_like(dcarry_ref)
            for r in dpo:
                r[...] = jnp.zeros_like(r)

        carry = jnp.where(i == nt - 1, 0.0, halo_ref[pl.ds(HALO - 1, 1), :])
        _, vjp = jax.vjp(_f_rw_pre, z_ref[...], carry, *[r[...].astype(F32) for r in pin])
        g = vjp(tuple(sum(cin.pop(0)[...] for _ in group) for group in cots))
        last_row = lax.broadcasted_iota(jnp.int32, z_ref.shape, 0) == tm - 1
        dz_ref[...] = g[0] + jnp.where(last_row, dcarry_ref[...], 0.0)
        dcarry_ref[...] = g[1]
        for r, v in zip(dpo, g[2:]):
            r[...] += v

    tok = lambda w: pl.BlockSpec((tm, w), lambda i: (order(i), 0))
    return pl.pallas_call(
        body, name="rw_pre_bwd",
        out_shape=[jax.ShapeDtypeStruct(z.shape, F32)] + [jax.ShapeDtypeStruct(p.shape, F32) for p in params],
        grid=(nt,),
        in_specs=_rw_pre_specs(z, params, tm, order) + [tok(RWKV_WIDTH)] * ncot,
        out_specs=[tok(z.shape[1])] + [_full_spec(p) for p in params],
        scratch_shapes=[pltpu.VMEM((1, z.shape[1]), F32)],
        compiler_params=_params(("arbitrary",)),
    )(z, z, *params, *flat_cots)


def _s5_scan(bu, lam, tm):
    t, w = bu.shape
    h = w // 2

    def body(bu_ref, lam_ref, x_ref, xp_ref, carry_ref):
        @pl.when(pl.program_id(0) == 0)
        def _():
            carry_ref[...] = jnp.zeros_like(carry_ref)

        lr, li = lam_ref[:, :h], lam_ref[:, h:]

        def step(s, c):
            cr, ci = c
            row = pl.ds(s, 1)
            xp_ref[row, :h] = cr
            xp_ref[row, h:] = ci
            nr = lr * cr - li * ci + bu_ref[row, :h]
            ni = lr * ci + li * cr + bu_ref[row, h:]
            x_ref[row, :h] = nr
            x_ref[row, h:] = ni
            return nr, ni

        cr, ci = lax.fori_loop(0, tm, step, (carry_ref[:, :h], carry_ref[:, h:]))
        carry_ref[:, :h] = cr
        carry_ref[:, h:] = ci

    spec = pl.BlockSpec((tm, w), lambda i: (i, 0))
    return pl.pallas_call(
        body, name="s5_scan",
        out_shape=[jax.ShapeDtypeStruct((t, w), F32)] * 2,
        grid=(t // tm,),
        in_specs=[spec, pl.BlockSpec((1, w), lambda i: (0, 0))],
        out_specs=[spec, spec],
        scratch_shapes=[pltpu.VMEM((1, w), F32)],
        compiler_params=_params(("arbitrary",)),
    )(bu, lam)


def _s5_scan_bwd(dx, xp, lam, tm):
    t, w = dx.shape
    h = w // 2
    nt = t // tm

    def body(dx_ref, xp_ref, lam_ref, dbu_ref, dlam_ref, carry_ref):
        @pl.when(pl.program_id(0) == 0)
        def _():
            carry_ref[...] = jnp.zeros_like(carry_ref)
            dlam_ref[...] = jnp.zeros_like(dlam_ref)

        lr, li = lam_ref[:, :h], lam_ref[:, h:]

        def step(s, c):
            cr, ci = c
            row = pl.ds(tm - 1 - s, 1)
            nr = lr * cr + li * ci + dx_ref[row, :h]
            ni = lr * ci - li * cr + dx_ref[row, h:]
            dbu_ref[row, :h] = nr
            dbu_ref[row, h:] = ni
            return nr, ni

        cr, ci = lax.fori_loop(0, tm, step, (carry_ref[:, :h], carry_ref[:, h:]))
        carry_ref[:, :h] = cr
        carry_ref[:, h:] = ci
        gr, gi = dbu_ref[:, :h], dbu_ref[:, h:]
        pr, pi_ = xp_ref[:, :h], xp_ref[:, h:]
        dlam_ref[:, :h] += jnp.sum(gr * pr + gi * pi_, axis=0, keepdims=True)
        dlam_ref[:, h:] += jnp.sum(gi * pr - gr * pi_, axis=0, keepdims=True)

    spec = pl.BlockSpec((tm, w), lambda i: (nt - 1 - i, 0))
    row_spec = pl.BlockSpec((1, w), lambda i: (0, 0))
    return pl.pallas_call(
        body, name="s5_scan_bwd",
        out_shape=[jax.ShapeDtypeStruct((t, w), F32), jax.ShapeDtypeStruct((1, w), F32)],
        grid=(nt,),
        in_specs=[spec, spec, row_spec],
        out_specs=[spec, row_spec],
        scratch_shapes=[pltpu.VMEM((1, w), F32)],
        compiler_params=_params(("arbitrary",)),
    )(dx, xp, lam)


def _wkv_chunks(s0, r, lw, k, v, a, b):
    c = r[0].shape[0]
    row = lax.broadcasted_iota(jnp.int32, (c, c), 0)
    col = lax.broadcasted_iota(jnp.int32, (c, c), 1)
    incl, strict = col <= row, col < row
    tri = incl.astype(F32)
    eye = (row == col).astype(F32)
    each = lambda f, *xs: [f(*t) for t in zip(*xs)]
    lc = each(lambda l: _fdot(tri, l), lw)
    e_neg = each(lambda l: jnp.exp(-l), lc)
    rt = each(lambda x, l: x * jnp.exp(l), r, lc)
    at = each(lambda x, l, w: x * jnp.exp(l - w), a, lc, lw)
    kt = each(lambda x, e: x * e, k, e_neg)
    bt = each(lambda x, e: x * e, b, e_neg)
    mab = each(lambda p, q: jnp.where(strict, _fdot(p, q, _NT), 0.0), at, bt)
    mak = each(lambda p, q: jnp.where(strict, _fdot(p, q, _NT), 0.0), at, kt)
    mrb = each(lambda p, q: jnp.where(incl, _fdot(p, q, _NT), 0.0), rt, bt)
    mrk = each(lambda p, q: jnp.where(incl, _fdot(p, q, _NT), 0.0), rt, kt)
    xs = each(lambda p, s, m, q: _fdot(p, s, _NT) + _fdot(m, q), at, s0, mak, v)
    ys = each(lambda p, s, m, q: _fdot(p, s, _NT) + _fdot(m, q), rt, s0, mrk, v)
    sk = each(lambda q, p: _fdot(q, p, _TN), v, kt)
    inv = each(lambda m: eye + m, mab)
    pw = mab
    for _ in range(int(math.log2(c)) - 1):
        pw = each(lambda m: _fdot(m, m), pw)
        inv = each(lambda i, m: i + _fdot(i, m), inv, pw)
    u = each(_fdot, inv, xs)
    y = each(lambda y0, m, q: y0 + _fdot(m, q), ys, mrb, u)
    e_tot = each(lambda l: jnp.exp(jnp.sum(l, axis=0, keepdims=True)), lw)
    s1 = each(lambda s, q, p, z, e: (s + _fdot(q, p, _TN) + z) * e, s0, u, bt, sk, e_tot)
    return y, s1


def _carry(plan, n_args, n_outs):
    n_in, n_out = len(plan.ins), len(plan.out_shape)

    def parts(refs):
        base = n_args + n_in + n_outs
        return refs[n_args:n_args + n_in], refs[base:base + n_out], refs[base + n_out + 1:]

    return parts, [HBM_SPEC] * n_in, list(plan.out_shape), [HBM_SPEC] * n_out, list(plan.sems)


def _head_cols(ref):
    return tuple(ref[:, h * HEAD:(h + 1) * HEAD] for h in range(HEADS))


def _wkv_fwd(seqs, plan):
    t, w = seqs[0].shape
    c, n = WKV_CHUNK, HEAD
    nc = t // c
    parts, plan_in_specs, plan_out_shape, plan_out_specs, plan_sems = _carry(plan, 6, 2)

    def body(*refs):
        ins, (y_ref, ck_ref) = refs[:6], refs[6 + len(plan.ins):8 + len(plan.ins)]
        s_ref = refs[8 + len(plan.ins) + len(plan.out_shape)]

        @pl.when(pl.program_id(0) == 0)
        def _():
            s_ref[...] = jnp.zeros_like(s_ref)
            plan.start(*parts(refs))

        s0 = tuple(s_ref[h] for h in range(HEADS))
        ys, s1 = _wkv_chunks(s0, *[_head_cols(r) for r in ins])
        for h in range(HEADS):
            ck_ref[0, h] = s0[h]
            y_ref[:, h * n:(h + 1) * n] = ys[h]
            s_ref[h] = s1[h]

        @pl.when(pl.program_id(0) == nc - 1)
        def _():
            plan.wait(*parts(refs))

    spec = pl.BlockSpec((c, w), lambda i: (i, 0))
    res = pl.pallas_call(
        body, name="wkv_fwd",
        out_shape=[jax.ShapeDtypeStruct((t, w), F32), jax.ShapeDtypeStruct((nc, HEADS, n, n), F32)] + plan_out_shape,
        grid=(nc,),
        in_specs=[spec] * 6 + plan_in_specs,
        out_specs=[spec, pl.BlockSpec((1, HEADS, n, n), lambda i: (i, 0, 0, 0))] + plan_out_specs,
        scratch_shapes=[pltpu.VMEM((HEADS, n, n), F32)] + plan_sems,
        compiler_params=_params(("arbitrary",)),
    )(*seqs, *plan.ins)
    return res[0], res[1], res[2:]


def _wkv_bwd(seqs, ck, dy, plan):
    t, w = seqs[0].shape
    c, n = WKV_CHUNK, HEAD
    nc = t // c
    parts, plan_in_specs, plan_out_shape, plan_out_specs, plan_sems = _carry(plan, 8, 6)

    def body(*refs):
        ins, ck_ref, dy_ref = refs[:6], refs[6], refs[7]
        outs = refs[8 + len(plan.ins):14 + len(plan.ins)]
        ds_ref = refs[14 + len(plan.ins) + len(plan.out_shape)]

        @pl.when(pl.program_id(0) == 0)
        def _():
            ds_ref[...] = jnp.zeros_like(ds_ref)
            plan.start(*parts(refs))

        s0 = tuple(ck_ref[0, h] for h in range(HEADS))
        _, vjp = jax.vjp(_wkv_chunks, s0, *[_head_cols(r) for r in ins])
        g = vjp((list(_head_cols(dy_ref)), [ds_ref[h] for h in range(HEADS)]))
        for h in range(HEADS):
            ds_ref[h] = g[0][h]
            for o, d in zip(outs, g[1:]):
                o[:, h * n:(h + 1) * n] = d[h]

        @pl.when(pl.program_id(0) == nc - 1)
        def _():
            plan.wait(*parts(refs))

    spec = pl.BlockSpec((c, w), lambda i: (nc - 1 - i, 0))
    res = pl.pallas_call(
        body, name="wkv_bwd",
        out_shape=[jax.ShapeDtypeStruct((t, w), F32)] * 6 + plan_out_shape,
        grid=(nc,),
        in_specs=[spec] * 6 + [pl.BlockSpec((1, HEADS, n, n), lambda i: (nc - 1 - i, 0, 0, 0)), spec] + plan_in_specs,
        out_specs=[spec] * 6 + plan_out_specs,
        scratch_shapes=[pltpu.VMEM((HEADS, n, n), F32)] + plan_sems,
        compiler_params=_params(("arbitrary",)),
    )(*seqs, ck, dy, *plan.ins)
    return res[:6], res[6:]


def _coords():
    return lax.axis_index("x"), lax.axis_index("y"), lax.axis_index("c")


def _flip(v, f):
    return 1 - v if f else v


_CHIP_FLIPS = [(1, 0), (0, 1), (1, 1)]
_DEV_FLIPS = [(fx, fy, fc) for fx in (0, 1) for fy in (0, 1) for fc in (0, 1) if (fx, fy, fc) != (0, 0, 0)]
HBM_SPEC = pl.BlockSpec(memory_space=pl.ANY)


def _chip_peer(k, x, y):
    fx, fy = _CHIP_FLIPS[k]
    return _flip(x, fx), _flip(y, fy)


def _dev_peer(k, x, y, c):
    fx, fy, fc = _DEV_FLIPS[k]
    return _flip(x, fx), _flip(y, fy), _flip(c, fc)


def _rows_of_core(ref, core):
    h = ref.shape[-2] // 2
    rows = pl.ds(pl.multiple_of(core * h, 8), h)
    return ref.at[rows, :] if len(ref.shape) == 2 else ref.at[:, rows, :]


class _Plan(NamedTuple):
    ins: Sequence[Any]
    out_shape: Sequence[Any]
    sems: Sequence[Any]
    start: Callable
    wait: Callable


def _run_plan(name, plan):
    n_in, n_out = len(plan.ins), len(plan.out_shape)

    def body(*refs):
        parts = refs[:n_in], refs[n_in:n_in + n_out], refs[n_in + n_out:]
        plan.start(*parts)
        plan.wait(*parts)

    return pl.pallas_call(
        body, name=name, out_shape=list(plan.out_shape),
        in_specs=[HBM_SPEC] * n_in, out_specs=[HBM_SPEC] * n_out, scratch_shapes=list(plan.sems),
    )(*plan.ins)


def _gather_plan(shards):
    n = len(shards)

    def copies(srcs, outs, sems):
        send_sems, recv_sems, local_sems = sems
        x, y, c = _coords()
        me = 2 * x + y

        def remote(i, k, arriving):
            px, py = _chip_peer(k, x, y)
            return pltpu.make_async_remote_copy(
                src_ref=srcs[i], dst_ref=outs[i].at[2 * px + py if arriving else me],
                send_sem=send_sems.at[i, k], recv_sem=recv_sems.at[i, k],
                device_id=(px, py, c), device_id_type=MESH)

        own = [pltpu.make_async_copy(srcs[i], outs[i].at[me], local_sems.at[i]) for i in range(n)]
        pairs = [(i, k) for k in range(3) for i in range(n)]
        return own, [remote(i, k, False) for i, k in pairs], [remote(i, k, True) for i, k in pairs]

    return _Plan(
        ins=shards, out_shape=[jax.ShapeDtypeStruct((4,) + s.shape, s.dtype) for s in shards],
        sems=[pltpu.SemaphoreType.DMA((n, 3)), pltpu.SemaphoreType.DMA((n, 3)), pltpu.SemaphoreType.DMA((n,))],
        start=functools.partial(_start_copies, copies), wait=functools.partial(_wait_copies, copies))


def _start_copies(copies, ins, outs, sems):
    own, sends, _ = copies(ins, outs, sems)
    for cp in own + sends:
        cp.start()


def _wait_copies(copies, ins, outs, sems):
    own, sends, arrivals = copies(ins, outs, sems)
    for cp in arrivals:
        cp.wait_recv()
    for cp in sends:
        cp.wait_send()
    for cp in own:
        cp.wait()


def _exchange_plan(gs, small=None):
    n = len(gs)
    arrays = list(gs) + ([] if small is None else [small])

    def copies(srcs, outs, sems):
        send_sems, recv_sems, local_sems = sems
        x, y, c = _coords()
        me = 4 * x + 2 * y + c

        def piece(i, px, py, pc):
            if i == n:
                return srcs[i].at[4 * px + 2 * py + pc]
            return _rows_of_core(srcs[i].at[2 * px + py], pc)

        def remote(i, k, arriving):
            px, py, pc = _dev_peer(k, x, y, c)
            return pltpu.make_async_remote_copy(
                src_ref=piece(i, px, py, pc), dst_ref=outs[i].at[4 * px + 2 * py + pc if arriving else me],
                send_sem=send_sems.at[i, k], recv_sem=recv_sems.at[i, k],
                device_id=(px, py, pc), device_id_type=MESH)

        own = [pltpu.make_async_copy(piece(i, x, y, c), outs[i].at[me], local_sems.at[i]) for i in range(len(arrays))]
        pairs = [(i, k) for k in range(7) for i in range(len(arrays))]
        return own, [remote(i, k, False) for i, k in pairs], [remote(i, k, True) for i, k in pairs]

    out_shape = [jax.ShapeDtypeStruct((8, g.shape[1] // 2, g.shape[2]), g.dtype) for g in gs]
    if small is not None:
        out_shape.append(jax.ShapeDtypeStruct(small.shape, small.dtype))
    m = len(arrays)
    return _Plan(
        ins=arrays, out_shape=out_shape,
        sems=[pltpu.SemaphoreType.DMA((m, 7)), pltpu.SemaphoreType.DMA((m, 7)), pltpu.SemaphoreType.DMA((m,))],
        start=functools.partial(_start_copies, copies), wait=functools.partial(_wait_copies, copies))


def _share_cores(halves, small):
    n = len(halves)

    def body(*refs):
        srcs, small_src, outs, small_out = refs[:n], refs[n], refs[n + 1:2 * n + 1], refs[2 * n + 1]
        mine, theirs = refs[2 * n + 2:3 * n + 2], refs[3 * n + 2:4 * n + 2]
        send_sems, recv_sems, ssend, srecv, local_sems = refs[4 * n + 2:]
        x, y, c = _coords()
        me = 4 * x + 2 * y + c

        def big(i):
            return pltpu.make_async_remote_copy(
                src_ref=mine[i], dst_ref=theirs[i], send_sem=send_sems.at[i], recv_sem=recv_sems.at[i],
                device_id=(x, y, 1 - c), device_id_type=MESH)

        def tiny(k, arriving):
            px, py, pc = _dev_peer(k, x, y, c)
            return pltpu.make_async_remote_copy(
                src_ref=small_src, dst_ref=small_out.at[4 * px + 2 * py + pc if arriving else me],
                send_sem=ssend.at[k], recv_sem=srecv.at[k], device_id=(px, py, pc), device_id_type=MESH)

        small_sends = [tiny(k, False) for k in range(7)]
        own_small = pltpu.make_async_copy(small_src, small_out.at[me], local_sems.at[2 * n])
        stage = [pltpu.make_async_copy(srcs[i], mine[i], local_sems.at[i]) for i in range(n)]
        for cp in small_sends + [own_small] + stage:
            cp.start()
        sends = []
        for i in range(n):
            stage[i].wait()
            sends.append(big(i))
            sends[-1].start()
        store = [pltpu.make_async_copy(mine[i], outs[i].at[c], local_sems.at[i]) for i in range(n)]
        for cp in store:
            cp.start()
        for i in range(n):
            big(i).wait_recv()
            store.append(pltpu.make_async_copy(theirs[i], outs[i].at[1 - c], local_sems.at[n + i]))
            store[-1].start()
        for k in range(7):
            tiny(k, True).wait_recv()
        for cp in sends + small_sends:
            cp.wait_send()
        for cp in store + [own_small]:
            cp.wait()

    staged = [pltpu.VMEM(s.shape, s.dtype) for s in halves]
    res = pl.pallas_call(
        body, name="share_cores",
        out_shape=[jax.ShapeDtypeStruct((2,) + s.shape, s.dtype) for s in halves]
        + [jax.ShapeDtypeStruct((8,) + small.shape, small.dtype)],
        in_specs=[HBM_SPEC] * (n + 1), out_specs=[HBM_SPEC] * (n + 1),
        scratch_shapes=staged + staged + [
            pltpu.SemaphoreType.DMA((n,)), pltpu.SemaphoreType.DMA((n,)),
            pltpu.SemaphoreType.DMA((7,)), pltpu.SemaphoreType.DMA((7,)),
            pltpu.SemaphoreType.DMA((2 * n + 1,))],
        compiler_params=pltpu.CompilerParams(vmem_limit_bytes=VMEM_LIMIT_BYTES),
    )(*halves, small)
    return res[:n], res[n]


def _row_tile(n, target):
    return max(d for d in range(8, min(n, target) + 1, 8) if n % d == 0)


def _ew(name, fn, ins, outs, block_bytes=1 << 20):
    rows, cols = ins[0].shape[-2:]
    lead = max(math.prod(a.shape[:-2]) for a in ins)
    tr = _row_tile(rows, max(8, block_bytes // (4 * cols * lead)))
    n = len(ins)

    def spec(shape):
        if len(shape) == 2:
            return pl.BlockSpec((tr, cols), lambda i: (i, 0))
        return pl.BlockSpec((shape[0], tr, cols), lambda i: (0, i, 0))

    def body(*refs):
        res = fn(*[r[...] for r in refs[:n]])
        for r, v in zip(refs[n:], res):
            r[...] = v

    return pl.pallas_call(
        body, name=name,
        out_shape=[jax.ShapeDtypeStruct(s, F32) for s in outs],
        grid=(rows // tr,),
        in_specs=[spec(a.shape) for a in ins],
        out_specs=[spec(s) for s in outs],
        compiler_params=_params(("parallel",)),
    )(*ins)


def _sum_slots(a):
    total = a[0]
    for s in range(1, a.shape[0]):
        total = total + a[s]
    return (total,)


def _adamw(g, w, m, v):
    bc1 = 1.0 - ADAM_B1 ** ADAM_STEP
    bc2 = 1.0 - ADAM_B2 ** ADAM_STEP
    m_new = ADAM_B1 * m + (1.0 - ADAM_B1) * g
    v_new = ADAM_B2 * v + (1.0 - ADAM_B2) * jnp.square(g)
    delta = -ADAM_LR * ((m_new / bc1) / (jnp.sqrt(v_new / bc2) + ADAM_EPS) + ADAM_WD * w)
    return delta, m_new, v_new


def _mat(a):
    return a.reshape(a.shape[-2:])


def _to_shard_major(full, axis):
    rows, cols = full.shape
    if axis == 0:
        return full.reshape(4, rows // 4, cols)
    return full.reshape(rows, 4, cols // 4).transpose(1, 0, 2)


def _from_shard_major(a, axis):
    _, r, cs = a.shape
    if axis == 0:
        return a.reshape(4 * r, cs)
    return a.transpose(1, 0, 2).reshape(r, 4 * cs)


def _pack_small(arrays):
    flat = jnp.concatenate([arrays[n].reshape(-1) for n in SMALL_NAMES])
    flat = jnp.pad(flat, (0, 8 * SMALL_ROWS * PACK_COLS - flat.shape[0]))
    return flat.reshape(8, SMALL_ROWS, PACK_COLS)


def _unpack_small(packed, shapes):
    flat = packed.reshape(-1)
    out, off = {}, 0
    for n in SMALL_NAMES:
        size = math.prod(shapes[n])
        out[n] = flat[off:off + size].reshape(shapes[n])
        off += size
    return out


def _row(a):
    return a.reshape(1, -1)


def _local_step(x, p, target, wf, ws, late_gather):
    wf = dict(wf)
    t = x.shape[0]
    tm = min(256, t)
    g = {}

    lam_re, lam_im = ws['s5_lam_re'].reshape(S5_GROUPS, S5_STATE), ws['s5_lam_im'].reshape(S5_GROUPS, S5_STATE)
    log_step = ws['s5_log_step'].reshape(S5_GROUPS, 1)
    gp = (S5_GROUPS, S5_STATE)
    lam_ins = (lam_re, lam_im, log_step)
    lbr, lbi, cfr, cfi = _small_fwd("s5_lam", _f_s5_lam, lam_ins, [(gp, F32)] * 4)
    lam_row = jnp.concatenate([_row(lbr), _row(lbi)], axis=1)
    to_t = lambda a, perm: a.reshape((S5_GROUPS,) + a.shape[-2:]).transpose(perm).reshape(S5_GROUP, S5_LANES)
    build_ins = (_row(cfr), _row(cfi), to_t(ws['s5_b_re'], (2, 0, 1)), to_t(ws['s5_b_im'], (2, 0, 1)),
                 to_t(ws['s5_c_re'], (1, 0, 2)), to_t(ws['s5_c_im'], (1, 0, 2)))
    stack_shape = (S5_WIDTH, 2 * S5_LANES)
    bstack, cstack_t = _small_fwd("s5_build", _f_s5_build, build_ins, [(stack_shape, F32)] * 2)

    norm_mix, norm_ffn, norm_ple = _row(ws['norm_mix']), _row(ws['norm_ffn']), _row(ws['norm_ple'])
    final_norm = _row(ws['final_norm'])
    (xn,) = _tok_fwd("norm_in", _f_norm_in, [x], [norm_mix], [(x.shape[1], BF16)], tm)
    u = _mm("proj_s5", xn, wf['w_in'][:, :S5_WIDTH], 'nn')
    z = _mm("proj_rw", xn, wf['w_in'][:, S5_WIDTH:], 'nn')

    bu = _mm("s5_bu", u, bstack, 'nn')
    xs, xs_prev = _s5_scan(bu, lam_row, tm)
    ypre = _mm("s5_y", xs, cstack_t, 'nt')
    s5_par = [_row(ws['s5_d']), wf['s5_glu_w'], _row(ws['s5_glu_b'])]
    (s5_out,) = _tok_fwd("s5_post", _f_s5_post, [ypre, u], s5_par, [(S5_WIDTH, BF16)], tm)

    pre_par = [_row(ws['rw_shift_mu']), _row(ws['rw_w0']), wf['rw_w2'], _row(ws['rw_a0']), wf['rw_a2'],
               wf['rw_g2'], _row(ws['rw_k_k']), _row(ws['rw_k_a'])]
    r, lw, kp, v, an, bn, gate = _rw_pre_fwd(z, pre_par, tm)
    seqs = [r, lw, kp, v, an, bn]
    y_wkv, ck, late = _wkv_fwd(seqs, late_gather)
    wf.update({n: _from_shard_major(a, SHARDED[n]) for n, a in zip(LATE_NAMES, late)})
    post_par = [_row(ws['rw_ln_w']), _row(ws['rw_ln_b']), _row(ws['rw_r_k'])]
    post_toks = [y_wkv, r, kp, v, gate]
    (rw_out,) = _tok_fwd("rw_post", _f_rw_post, post_toks, post_par, [(RWKV_WIDTH, BF16)], tm)

    mixcat = jnp.concatenate([s5_out, rw_out], axis=1)
    mixed = _mm("mix_out", mixcat, wf['w_out'], 'nn')
    h1, hn = _tok_fwd("mix_res", _f_mix_res, [x, mixed], [norm_ffn], [(x.shape[1], F32), (x.shape[1], BF16)], tm)
    w13 = jnp.concatenate([wf['ffn_w1'], wf['ffn_w3']], axis=1)
    a13 = _mm("ffn_up", hn, w13, 'nn')
    (f,) = _tok_fwd("ffn_act", _f_ffn_act, [a13], [], [(FFN_HIDDEN, BF16)], tm)
    ffo = _mm("ffn_down", f, wf['ffn_w2'], 'nn')
    h2, hp = _tok_fwd("ffn_res", _f_ffn_res, [h1, ffo], [norm_ple], [(x.shape[1], F32), (x.shape[1], BF16)], tm)
    gpre = _mm("ple_gate", hp, wf['ple_gate_w'], 'nn')
    pu = _mm("ple_up", p, wf['ple_up_w'], 'nn')

    dh2, dgpre, dpu, g['final_norm'], loss = _tok_bwd(
        "loss", _f_loss, [h2, gpre, pu, target], [final_norm], [None],
        [F32, BF16, BF16, None], [True], tm, acc_out=0)
    g['ple_gate_w'] = _mm("d_ple_gate_w", hp, dgpre, 'tn')
    g['ple_up_w'] = _mm("d_ple_up_w", p, dpu, 'tn')
    dhp = _mm("d_hp", dgpre, wf['ple_gate_w'], 'nt')
    dh1, dffo, g['norm_ple'] = _tok_bwd("ffn_res_bwd", _f_ffn_res, [h1, ffo], [norm_ple], [dh2, dhp],
                                        [F32, BF16], [True], tm)
    g['ffn_w2'] = _mm("d_ffn_w2", f, dffo, 'tn')
    df = _mm("d_f", dffo, wf['ffn_w2'], 'nt')
    (da13,) = _tok_bwd("ffn_act_bwd", _f_ffn_act, [a13], [], [df], [BF16], [], tm)
    dw13 = _mm("d_ffn_w13", hn, da13, 'tn')
    g['ffn_w1'], g['ffn_w3'] = dw13[:, :FFN_HIDDEN], dw13[:, FFN_HIDDEN:]
    dhn = _mm("d_hn", da13, w13, 'nt')
    dx_a, dmixed, g['norm_ffn'] = _tok_bwd("mix_res_bwd", _f_mix_res, [x, mixed], [norm_ffn], [dh1, dhn],
                                           [F32, BF16], [True], tm)
    g['w_out'] = _mm("d_w_out", mixcat, dmixed, 'tn')
    dmixcat = _mm("d_mixcat", dmixed, wf['w_out'], 'nt')
    ds5_out, drw_out = dmixcat[:, :S5_WIDTH], dmixcat[:, S5_WIDTH:]

    dy_wkv, dr_b, dkp_b, dv_b, dgate, g['rw_ln_w'], g['rw_ln_b'], g['rw_r_k'] = _tok_bwd(
        "rw_post_bwd", _f_rw_post, post_toks, post_par, [drw_out], [F32] * 5, [True] * 3, tm)
    late_exchange = _exchange_plan([_to_shard_major(g[n], SHARDED[n]) for n in LATE_NAMES])
    dseqs, late_pieces = _wkv_bwd(seqs, ck, dy_wkv, late_exchange)
    pre_cots = [(dseqs[0], dr_b), (dseqs[1],), (dseqs[2], dkp_b), (dseqs[3], dv_b), (dseqs[4],), (dseqs[5],),
                (dgate,)]
    dz, *dpre = _rw_pre_bwd(z, pre_par, pre_cots, tm)
    for n, d in zip(['rw_shift_mu', 'rw_w0', 'rw_w2', 'rw_a0', 'rw_a2', 'rw_g2', 'rw_k_k', 'rw_k_a'], dpre):
        g[n] = d

    dypre, du_a, g['s5_d'], g['s5_glu_w'], g['s5_glu_b'] = _tok_bwd(
        "s5_post_bwd", _f_s5_post, [ypre, u], s5_par, [ds5_out], [F32, F32], [True] * 3, tm)
    dxs = _mm("d_s5_x", dypre, cstack_t, 'nn')
    dcstack_t = _mm("d_s5_c", dypre, xs, 'tn')
    dbu, dlam_row = _s5_scan_bwd(dxs, xs_prev, lam_row, tm)
    du_b = _mm("d_s5_u", dbu, bstack, 'nt')
    dbstack = _mm("d_s5_b", u, dbu, 'tn')
    dbuild = _small_bwd("s5_build_bwd", _f_s5_build, build_ins, (dbstack, dcstack_t))
    lam_cots = (dlam_row[:, :S5_LANES].reshape(gp), dlam_row[:, S5_LANES:].reshape(gp),
                dbuild[0].reshape(gp), dbuild[1].reshape(gp))
    g['s5_lam_re'], g['s5_lam_im'], g['s5_log_step'] = _small_bwd("s5_lam_bwd", _f_s5_lam, lam_ins, lam_cots)
    from_t = lambda a, perm: a.reshape(S5_GROUP, S5_GROUPS, S5_STATE).transpose(perm)
    g['s5_b_re'], g['s5_b_im'] = from_t(dbuild[2], (1, 2, 0)), from_t(dbuild[3], (1, 2, 0))
    g['s5_c_re'], g['s5_c_im'] = from_t(dbuild[4], (1, 0, 2)), from_t(dbuild[5], (1, 0, 2))

    dproj = jnp.concatenate([(du_a + du_b).astype(BF16), dz.astype(BF16)], axis=1)
    g['w_in'] = _mm("d_w_in", xn, dproj, 'tn')
    dxn = _mm("d_xn", dproj, wf['w_in'], 'nt')
    grad_x, g['norm_mix'] = _tok_bwd("norm_in_bwd", _f_norm_in, [x], [norm_mix], [dxn], [F32], [True], tm,
                                     add_to=(0, dx_a))
    return loss[0, 0], grad_x, g, late_pieces


def _step(x, p, target, w, m, v):
    shards = {n: _mat(w[n]).astype(BF16) for n in SHARDED_NAMES}
    early = _run_plan("gather_early", _gather_plan([shards[n] for n in EARLY_NAMES]))
    wf = {n: _from_shard_major(a, SHARDED[n]) for n, a in zip(EARLY_NAMES, early)}
    ws = {n: w[n] for n in SMALL_NAMES}

    late_gather = _gather_plan([shards[n] for n in LATE_NAMES])
    loss, grad_x, g, late_pieces = _local_step(x[0], p[0, 0], target[0], wf, ws, late_gather)

    early_plan = _exchange_plan([_to_shard_major(g[n], SHARDED[n]) for n in EARLY_NAMES],
                                _pack_small({n: g[n] for n in SMALL_NAMES}))
    *early_pieces, by_dev = _run_plan("exchange_early", early_plan)
    pieces = dict(zip(LATE_NAMES + EARLY_NAMES, list(late_pieces) + early_pieces))
    halves = [_ew("add_devices_" + n, _sum_slots, [pieces[n]], [pieces[n].shape[1:]])[0] for n in SHARDED_NAMES]
    (small_piece,) = _ew("add_devices_small", _sum_slots, [by_dev], [by_dev.shape[1:]])
    both, small_g = _share_cores(halves, small_piece)

    kinds = [{}, {}, {}, {}]
    for n, gn in zip(SHARDED_NAMES, both):
        shard = _mat(w[n]).shape
        res = _ew("adamw_" + n, _adamw, [gn.reshape(shard), _mat(w[n]), _mat(m[n]), _mat(v[n])], [shard] * 3)
        for kind, a in zip(kinds, [gn] + list(res)):
            kind[n] = a.reshape(w[n].shape)
    flat = (8 * SMALL_ROWS, PACK_COLS)
    packed = [_pack_small({n: d[n] for n in SMALL_NAMES}).reshape(flat) for d in (w, m, v)]
    small_res = _ew("adamw_small", _adamw, [small_g.reshape(flat)] + packed, [flat] * 3)
    small_shapes = {n: w[n].shape for n in SMALL_NAMES}
    for kind, a in zip(kinds, [small_g] + list(small_res)):
        kind.update(_unpack_small(a, small_shapes))
    total = lax.psum(loss, ("x", "y", "c"))
    return (total, grad_x[None], *[kind[n] for kind in kinds for n in WEIGHT_NAMES])


def kernel(x, p, norm_mix, w_in, s5_lam_re, s5_lam_im, s5_log_step, s5_b_re, s5_b_im, s5_c_re, s5_c_im, s5_d, s5_glu_w, s5_glu_b, rw_shift_mu, rw_w0, rw_w2, rw_a0, rw_a2, rw_g2, rw_k_k, rw_k_a, rw_r_k, rw_ln_w, rw_ln_b, w_out, norm_ffn, ffn_w1, ffn_w3, ffn_w2, norm_ple, ple_gate_w, ple_up_w, final_norm, loss_target, m_norm_mix, m_w_in, m_s5_lam_re, m_s5_lam_im, m_s5_log_step, m_s5_b_re, m_s5_b_im, m_s5_c_re, m_s5_c_im, m_s5_d, m_s5_glu_w, m_s5_glu_b, m_rw_shift_mu, m_rw_w0, m_rw_w2, m_rw_a0, m_rw_a2, m_rw_g2, m_rw_k_k, m_rw_k_a, m_rw_r_k, m_rw_ln_w, m_rw_ln_b, m_w_out, m_norm_ffn, m_ffn_w1, m_ffn_w3, m_ffn_w2, m_norm_ple, m_ple_gate_w, m_ple_up_w, m_final_norm, v_norm_mix, v_w_in, v_s5_lam_re, v_s5_lam_im, v_s5_log_step, v_s5_b_re, v_s5_b_im, v_s5_c_re, v_s5_c_im, v_s5_d, v_s5_glu_w, v_s5_glu_b, v_rw_shift_mu, v_rw_w0, v_rw_w2, v_rw_a0, v_rw_a2, v_rw_g2, v_rw_k_k, v_rw_k_a, v_rw_r_k, v_rw_ln_w, v_rw_ln_b, v_w_out, v_norm_ffn, v_ffn_w1, v_ffn_w3, v_ffn_w2, v_norm_ple, v_ple_gate_w, v_ple_up_w, v_final_norm):
    args = dict(locals())
    w = {n: args[n] for n in WEIGHT_NAMES}
    m = {n: args["m_" + n] for n in WEIGHT_NAMES}
    v = {n: args["v_" + n] for n in WEIGHT_NAMES}
    return _step(x, p, loss_target, w, m, v)
```

```python
import functools
import math
from typing import Any, Callable, NamedTuple, Sequence

import jax
import jax.numpy as jnp
from jax import lax
from jax.experimental import pallas as pl
from jax.experimental.pallas import tpu as pltpu

F32 = jnp.float32
BF16 = jnp.bfloat16
MESH = pl.DeviceIdType.MESH

S5_WIDTH = 512
RWKV_WIDTH = 512
S5_GROUP = 16
S5_GROUPS = 32
S5_STATE = 64
S5_LANES = S5_GROUPS * S5_STATE
HEAD = 64
HEADS = 8
DECAY_LORA = 64
AAA_LORA = 64
GATE_LORA = 128
FFN_HIDDEN = 2816
RMS_EPS = 1e-6
GN_EPS = 64e-5
L2_EPS = 1e-12
ADAM_LR = 0.001
ADAM_B1 = 0.9
ADAM_B2 = 0.999
ADAM_EPS = 1e-08
ADAM_WD = 0.01
ADAM_STEP = 10

WKV_CHUNK = 64
WKV_PASSES = 1
VMEM_LIMIT_BYTES = 48 * 1024 * 1024
LANE = 128
PACK_COLS = 1024
SMALL_ROWS = 24

WEIGHT_NAMES = ['norm_mix', 'w_in', 's5_lam_re', 's5_lam_im', 's5_log_step', 's5_b_re', 's5_b_im', 's5_c_re',
                's5_c_im', 's5_d', 's5_glu_w', 's5_glu_b', 'rw_shift_mu', 'rw_w0', 'rw_w2', 'rw_a0', 'rw_a2',
                'rw_g2', 'rw_k_k', 'rw_k_a', 'rw_r_k', 'rw_ln_w', 'rw_ln_b', 'w_out', 'norm_ffn', 'ffn_w1',
                'ffn_w3', 'ffn_w2', 'norm_ple', 'ple_gate_w', 'ple_up_w', 'final_norm']
SHARDED = {'w_in': 1, 's5_glu_w': 0, 'rw_w2': 1, 'rw_a2': 1, 'rw_g2': 1, 'w_out': 0, 'ffn_w1': 1, 'ffn_w3': 1,
           'ffn_w2': 0, 'ple_gate_w': 0, 'ple_up_w': 1}
SHARDED_NAMES = [n for n in WEIGHT_NAMES if n in SHARDED]
LATE_NAMES = ['w_out', 'ffn_w1', 'ffn_w3', 'ffn_w2', 'ple_gate_w', 'ple_up_w']
EARLY_NAMES = [n for n in SHARDED_NAMES if n not in LATE_NAMES]
SMALL_NAMES = [n for n in WEIGHT_NAMES if n not in SHARDED]


def _params(sem=None):
    return pltpu.CompilerParams(dimension_semantics=sem, vmem_limit_bytes=VMEM_LIMIT_BYTES)


def _tile(n, target):
    best = None
    for d in range(LANE, min(n, target) + 1, LANE):
        if n % d == 0:
            best = d
    return n if best is None else best


_NN = (((1,), (0,)), ((), ()))
_NT = (((1,), (1,)), ((), ()))
_TN = (((0,), (0,)), ((), ()))


def _split(a):
    a = a.astype(F32)
    hi = a.astype(BF16)
    return hi, (a - hi.astype(F32)).astype(BF16)


def _dg(a, b, dims, passes):
    dg = lambda p, q: lax.dot_general(p, q, dims, preferred_element_type=F32)
    if passes == 1:
        return dg(a.astype(BF16), b.astype(BF16))
    bh, bl = _split(b)
    if passes == 2:
        return dg(a.astype(BF16), bh) + dg(a.astype(BF16), bl)
    ah, al = _split(a)
    return dg(ah, bh) + (dg(ah, bl) + dg(al, bh))


_DOT_BWD = {_NN: (("g", "b", _NT), ("a", "g", _TN)),
            _NT: (("g", "b", _NN), ("g", "a", _TN)),
            _TN: (("b", "g", _NT), ("a", "g", _NN))}


@functools.partial(jax.custom_vjp, nondiff_argnums=(2, 3))
def _dot(a, b, dims, passes):
    return _dg(a, b, dims, passes)


def _dot_fwd(a, b, dims, passes):
    return _dg(a, b, dims, passes), (a, b)


def _dot_bwd(dims, passes, res, g):
    env = {"a": res[0], "b": res[1], "g": g}
    return tuple(_dg(env[p], env[q], d, passes) for p, q, d in _DOT_BWD[dims])


_dot.defvjp(_dot_fwd, _dot_bwd)


def _bdot(x, w):
    return _dot(x, w, _NN, 1)


def _fdot(a, b, dims=_NN):
    return _dot(a, b, dims, 3)


@jax.custom_vjp
def _shift_down(z):
    return pltpu.roll(z, 1, 0)


def _shift_down_fwd(z):
    return pltpu.roll(z, 1, 0), None


def _shift_down_bwd(_, g):
    return (pltpu.roll(g, g.shape[0] - 1, 0),)


_shift_down.defvjp(_shift_down_fwd, _shift_down_bwd)


def _head_ones():
    r = lax.broadcasted_iota(jnp.int32, (RWKV_WIDTH, RWKV_WIDTH), 0) // HEAD
    c = lax.broadcasted_iota(jnp.int32, (RWKV_WIDTH, RWKV_WIDTH), 1) // HEAD
    return (r == c).astype(F32)


def _mm(name, a, b, mode, out_dtype=F32, precise=False, tm=1024, tn=1024, tk=1536):
    if mode == 'nn':
        (m, k), (_, n) = a.shape, b.shape
    elif mode == 'nt':
        (m, k), (n, _) = a.shape, b.shape
    else:
        (k, m), (_, n) = a.shape, b.shape
    tm, tn, tk = _tile(m, tm), _tile(n, tn), _tile(k, tk)
    nk = k // tk
    dims = {'nn': _NN, 'nt': _NT, 'tn': _TN}[mode]

    def body(a_ref, b_ref, o_ref, acc_ref):
        kk = pl.program_id(2)

        @pl.when(kk == 0)
        def _():
            acc_ref[...] = jnp.zeros_like(acc_ref)

        acc_ref[...] += _dg(a_ref[...], b_ref[...], dims, 3 if precise else 1)

        @pl.when(kk == nk - 1)
        def _():
            o_ref[...] = acc_ref[...].astype(o_ref.dtype)

    if mode == 'tn':
        a_spec = pl.BlockSpec((tk, tm), lambda i, j, l: (l, i))
    else:
        a_spec = pl.BlockSpec((tm, tk), lambda i, j, l: (i, l))
    if mode == 'nt':
        b_spec = pl.BlockSpec((tn, tk), lambda i, j, l: (j, l))
    else:
        b_spec = pl.BlockSpec((tk, tn), lambda i, j, l: (l, j))
    return pl.pallas_call(
        body, name=name,
        out_shape=jax.ShapeDtypeStruct((m, n), out_dtype),
        grid=(m // tm, n // tn, nk),
        in_specs=[a_spec, b_spec],
        out_specs=pl.BlockSpec((tm, tn), lambda i, j, l: (i, j)),
        scratch_shapes=[pltpu.VMEM((tm, tn), F32)],
        compiler_params=_params(("parallel", "parallel", "arbitrary")),
    )(a, b)


def _full_spec(p):
    nd = p.ndim
    return pl.BlockSpec(p.shape, lambda i, nd=nd: (0,) * nd)


def _tok_fwd(name, fn, toks, params, outs, tm):
    t = toks[0].shape[0]
    nt, npar = len(toks), len(params)

    def body(*refs):
        tv = [r[...].astype(F32) for r in refs[:nt]]
        pv = [r[...].astype(F32) for r in refs[nt:nt + npar]]
        res = fn(*tv, *pv)
        for r, v in zip(refs[nt + npar:], res):
            r[...] = v.astype(r.dtype)

    return pl.pallas_call(
        body, name=name,
        out_shape=[jax.ShapeDtypeStruct((t, w), d) for w, d in outs],
        grid=(t // tm,),
        in_specs=[pl.BlockSpec((tm, a.shape[1]), lambda i: (i, 0)) for a in toks] + [_full_spec(p) for p in params],
        out_specs=[pl.BlockSpec((tm, w), lambda i: (i, 0)) for w, _ in outs],
        compiler_params=_params(("parallel",)),
    )(*toks, *params)


def _tok_bwd(name, fn, toks, params, cots, dtok, dpar, tm, acc_out=None, add_to=None):
    t = toks[0].shape[0]
    nt, npar = len(toks), len(params)
    cot_arrays = [c for c in cots if c is not None]
    ncot = len(cot_arrays)
    extra = [] if add_to is None else [add_to[1]]
    dtok_idx = [i for i, d in enumerate(dtok) if d is not None]
    dpar_idx = [i for i, d in enumerate(dpar) if d]

    def body(*refs):
        pos = 0
        tin = refs[pos:pos + nt]; pos += nt
        pin = refs[pos:pos + npar]; pos += npar
        cin = refs[pos:pos + ncot]; pos += ncot
        ein = refs[pos:pos + len(extra)]; pos += len(extra)
        dto = refs[pos:pos + len(dtok_idx)]; pos += len(dtok_idx)
        dpo = refs[pos:pos + len(dpar_idx)]; pos += len(dpar_idx)
        acc = refs[pos] if acc_out is not None else None
        first = pl.program_id(0) == 0

        tv = [r[...].astype(F32) for r in tin]
        pv = [r[...].astype(F32) for r in pin]
        res, vjp = jax.vjp(fn, *tv, *pv)
        cit = iter(cin)
        cs = tuple(jnp.ones_like(o) if c is None else next(cit)[...].astype(F32) for c, o in zip(cots, res))
        g = vjp(cs)
        for r, i in zip(dto, dtok_idx):
            v = g[i]
            if add_to is not None and add_to[0] == i:
                v = v + ein[0][...].astype(F32)
            r[...] = v.astype(r.dtype)

        @pl.when(first)
        def _():
            for r in dpo:
                r[...] = jnp.zeros_like(r)
            if acc is not None:
                acc[...] = jnp.zeros_like(acc)

        for r, i in zip(dpo, dpar_idx):
            r[...] += g[nt + i]
        if acc is not None:
            acc[...] += res[acc_out]

    out_shape = [jax.ShapeDtypeStruct(toks[i].shape, dtok[i]) for i in dtok_idx]
    out_shape += [jax.ShapeDtypeStruct(params[i].shape, F32) for i in dpar_idx]
    out_specs = [pl.BlockSpec((tm, toks[i].shape[1]), lambda i_: (i_, 0)) for i in dtok_idx]
    out_specs += [_full_spec(params[i]) for i in dpar_idx]
    if acc_out is not None:
        out_shape.append(jax.ShapeDtypeStruct((1, 1), F32))
        out_specs.append(pl.BlockSpec((1, 1), lambda i_: (0, 0)))
    tok_spec = lambda a: pl.BlockSpec((tm, a.shape[1]), lambda i_: (i_, 0))
    return pl.pallas_call(
        body, name=name,
        out_shape=out_shape,
        grid=(t // tm,),
        in_specs=[tok_spec(a) for a in toks] + [_full_spec(p) for p in params]
        + [tok_spec(c) for c in cot_arrays] + [tok_spec(e) for e in extra],
        out_specs=out_specs,
        compiler_params=_params(("arbitrary",)),
    )(*toks, *params, *cot_arrays, *extra)


def _small_fwd(name, fn, ins, outs):
    n = len(ins)

    def body(*refs):
        res = fn(*[r[...] for r in refs[:n]])
        for r, v in zip(refs[n:], res):
            r[...] = v.astype(r.dtype)

    return pl.pallas_call(
        body, name=name,
        out_shape=[jax.ShapeDtypeStruct(s, d) for s, d in outs],
        compiler_params=_params(),
    )(*ins)


def _small_bwd(name, fn, ins, cots):
    n = len(ins)

    def body(*refs):
        _, vjp = jax.vjp(fn, *[r[...] for r in refs[:n]])
        g = vjp(tuple(r[...] for r in refs[n:n + len(cots)]))
        for r, v in zip(refs[n + len(cots):], g):
            r[...] = v

    return pl.pallas_call(
        body, name=name,
        out_shape=[jax.ShapeDtypeStruct(a.shape, F32) for a in ins],
        compiler_params=_params(),
    )(*ins, *cots)


def _rms(x, g):
    return x * lax.rsqrt(jnp.mean(x * x, axis=-1, keepdims=True) + RMS_EPS) * g


def _f_norm_in(x, g):
    return (_rms(x, g),)


def _f_mix_res(x, mixed, g):
    h1 = x + mixed
    return h1, _rms(h1, g)


def _f_ffn_act(a13):
    a1, a3 = a13[:, :FFN_HIDDEN], a13[:, FFN_HIDDEN:]
    return (jax.nn.silu(a1) * a3,)


def _f_ffn_res(h1, ffo, g):
    h2 = h1 + ffo
    return h2, _rms(h2, g)


def _f_loss(h2, gpre, pu, target, g):
    h3 = h2 + jax.nn.sigmoid(gpre) * pu
    y = _rms(h3, g)
    err = jnp.square(y - target)
    return (0.5 * jnp.sum(jnp.mean(err, axis=-1, keepdims=True), axis=0, keepdims=True),)


def _f_s5_post(ypre, u, d, glu_w, glu_b):
    z = jax.nn.gelu(ypre + u * d)
    return (z * jax.nn.sigmoid(_bdot(z, glu_w) + glu_b),)


def _softplus(x):
    return jnp.maximum(x, 0.0) + jnp.log(1.0 + jnp.exp(-jnp.abs(x)))


def _f_rw_pre(z, carry, shift_mu, w0, w2, a0, a2, g2, k_k, k_a):
    rw = RWKV_WIDTH
    first_row = lax.broadcasted_iota(jnp.int32, z.shape, 0) == 0
    prev = jnp.where(first_row, carry, _shift_down(z))
    zs = z + (prev - z) * shift_mu
    o1, o2 = 3 * rw + DECAY_LORA, 3 * rw + DECAY_LORA + AAA_LORA
    r, k, v = zs[:, :rw], zs[:, rw:2 * rw], zs[:, 2 * rw:3 * rw]
    wl, al, gl = zs[:, 3 * rw:o1], zs[:, o1:o2], zs[:, o2:]
    w = -_softplus(-(w0 + _bdot(jnp.tanh(wl), w2))) - 0.5
    log_decay = -jnp.exp(w)
    a = jax.nn.sigmoid(a0 + _bdot(al, a2))
    g = _bdot(jax.nn.sigmoid(gl), g2)
    kk = k * k_k
    norm = jnp.sqrt(_fdot(kk * kk, _head_ones()))
    kk = kk / jnp.maximum(norm, L2_EPS)
    kp = k * (1.0 + (a - 1.0) * k_a)
    return r, log_decay, kp, v, -kk, kk * a, g


def _f_rw_post(y, r, kp, v, g, ln_w, ln_b, r_k):
    ones = _head_ones()
    yc = y - _fdot(y, ones) * (1.0 / HEAD)
    var = _fdot(yc * yc, ones) * (1.0 / HEAD)
    yn = yc * lax.rsqrt(var + GN_EPS) * ln_w + ln_b
    bonus = _fdot(r * kp * r_k, ones) * v
    return ((yn + bonus) * g,)


def _f_s5_lam(lam_re, lam_im, log_step):
    step = jnp.exp(log_step)
    dr, di = lam_re * step, lam_im * step
    e = jnp.exp(dr)
    lbr, lbi = e * jnp.cos(di), e * jnp.sin(di)
    nr, ni = lbr - 1.0, lbi
    den = lam_re * lam_re + lam_im * lam_im
    return lbr, lbi, (nr * lam_re + ni * lam_im) / den, (ni * lam_re - nr * lam_im) / den


def _f_s5_build(coef_r, coef_i, btr, bti, ctr, cti):
    bbr = coef_r * btr - coef_i * bti
    bbi = coef_r * bti + coef_i * btr
    rows = lax.broadcasted_iota(jnp.int32, (S5_WIDTH, S5_LANES), 0) // S5_GROUP
    cols = lax.broadcasted_iota(jnp.int32, (S5_WIDTH, S5_LANES), 1) // S5_STATE
    mask = (rows == cols).astype(F32)
    rep = lambda m: jnp.concatenate([m] * S5_GROUPS, axis=0) * mask
    return (jnp.concatenate([rep(bbr), rep(bbi)], axis=1), jnp.concatenate([rep(ctr), -rep(cti)], axis=1))


HALO = 8


def _rw_pre_specs(z, params, tm, order):
    halo_blocks = tm // HALO
    return ([pl.BlockSpec((tm, z.shape[1]), lambda i: (order(i), 0)),
             pl.BlockSpec((HALO, z.shape[1]), lambda i: (jnp.maximum(order(i) * halo_blocks - 1, 0), 0))]
            + [_full_spec(p) for p in params])


def _rw_pre_fwd(z, params, tm):
    t = z.shape[0]
    npar = len(params)

    def body(z_ref, halo_ref, *refs):
        carry = jnp.where(pl.program_id(0) == 0, 0.0, halo_ref[pl.ds(HALO - 1, 1), :])
        res = _f_rw_pre(z_ref[...], carry, *[r[...].astype(F32) for r in refs[:npar]])
        for r, v in zip(refs[npar:], res):
            r[...] = v

    return pl.pallas_call(
        body, name="rw_pre",
        out_shape=[jax.ShapeDtypeStruct((t, RWKV_WIDTH), F32)] * 7,
        grid=(t // tm,),
        in_specs=_rw_pre_specs(z, params, tm, lambda i: i),
        out_specs=[pl.BlockSpec((tm, RWKV_WIDTH), lambda i: (i, 0))] * 7,
        compiler_params=_params(("parallel",)),
    )(z, z, *params)


def _rw_pre_bwd(z, params, cots, tm):
    t = z.shape[0]
    nt = t // tm
    npar = len(params)
    order = lambda i: nt - 1 - i
    flat_cots = [a for group in cots for a in group]
    ncot = len(flat_cots)

    def body(z_ref, halo_ref, *refs):
        pin, cin = refs[:npar], list(refs[npar:npar + ncot])
        dz_ref = refs[npar + ncot]
        dpo = refs[npar + ncot + 1:npar + ncot + 1 + npar]
        dcarry_ref = refs[npar + ncot + 1 + npar]
        i = pl.program_id(0)

        @pl.when(i == 0)
        def _():
            dcarry_ref[...] = jnp.zeros_like(dcarry_ref)
            for r in dpo:
                r[...] = jnp.zeros_like(r)

        carry = jnp.where(i == nt - 1, 0.0, halo_ref[pl.ds(HALO - 1, 1), :])
        _, vjp = jax.vjp(_f_rw_pre, z_ref[...], carry, *[r[...].astype(F32) for r in pin])
        g = vjp(tuple(sum(cin.pop(0)[...] for _ in group) for group in cots))
        last_row = lax.broadcasted_iota(jnp.int32, z_ref.shape, 0) == tm - 1
        dz_ref[...] = g[0] + jnp.where(last_row, dcarry_ref[...], 0.0)
        dcarry_ref[...] = g[1]
        for r, v in zip(dpo, g[2:]):
            r[...] += v

    tok = lambda w: pl.BlockSpec((tm, w), lambda i: (order(i), 0))
    return pl.pallas_call(
        body, name="rw_pre_bwd",
        out_shape=[jax.ShapeDtypeStruct(z.shape, F32)] + [jax.ShapeDtypeStruct(p.shape, F32) for p in params],
        grid=(nt,),
        in_specs=_rw_pre_specs(z, params, tm, order) + [tok(RWKV_WIDTH)] * ncot,
        out_specs=[tok(z.shape[1])] + [_full_spec(p) for p in params],
        scratch_shapes=[pltpu.VMEM((1, z.shape[1]), F32)],
        compiler_params=_params(("arbitrary",)),
    )(z, z, *params, *flat_cots)


def _s5_scan(bu, lam, tm):
    t, w = bu.shape
    h = w // 2

    def body(bu_ref, lam_ref, x_ref, xp_ref, carry_ref):
        @pl.when(pl.program_id(0) == 0)
        def _():
            carry_ref[...] = jnp.zeros_like(carry_ref)

        lr, li = lam_ref[:, :h], lam_ref[:, h:]

        def step(s, c):
            cr, ci = c
            row = pl.ds(s, 1)
            xp_ref[row, :h] = cr
            xp_ref[row, h:] = ci
            nr = lr * cr - li * ci + bu_ref[row, :h]
            ni = lr * ci + li * cr + bu_ref[row, h:]
            x_ref[row, :h] = nr
            x_ref[row, h:] = ni
            return nr, ni

        cr, ci = lax.fori_loop(0, tm, step, (carry_ref[:, :h], carry_ref[:, h:]))
        carry_ref[:, :h] = cr
        carry_ref[:, h:] = ci

    spec = pl.BlockSpec((tm, w), lambda i: (i, 0))
    return pl.pallas_call(
        body, name="s5_scan",
        out_shape=[jax.ShapeDtypeStruct((t, w), F32)] * 2,
        grid=(t // tm,),
        in_specs=[spec, pl.BlockSpec((1, w), lambda i: (0, 0))],
        out_specs=[spec, spec],
        scratch_shapes=[pltpu.VMEM((1, w), F32)],
        compiler_params=_params(("arbitrary",)),
    )(bu, lam)


def _s5_scan_bwd(dx, xp, lam, tm):
    t, w = dx.shape
    h = w // 2
    nt = t // tm

    def body(dx_ref, xp_ref, lam_ref, dbu_ref, dlam_ref, carry_ref):
        @pl.when(pl.program_id(0) == 0)
        def _():
            carry_ref[...] = jnp.zeros_like(carry_ref)
            dlam_ref[...] = jnp.zeros_like(dlam_ref)

        lr, li = lam_ref[:, :h], lam_ref[:, h:]

        def step(s, c):
            cr, ci = c
            row = pl.ds(tm - 1 - s, 1)
            nr = lr * cr + li * ci + dx_ref[row, :h]
            ni = lr * ci - li * cr + dx_ref[row, h:]
            dbu_ref[row, :h] = nr
            dbu_ref[row, h:] = ni
            return nr, ni

        cr, ci = lax.fori_loop(0, tm, step, (carry_ref[:, :h], carry_ref[:, h:]))
        carry_ref[:, :h] = cr
        carry_ref[:, h:] = ci
        gr, gi = dbu_ref[:, :h], dbu_ref[:, h:]
        pr, pi_ = xp_ref[:, :h], xp_ref[:, h:]
        dlam_ref[:, :h] += jnp.sum(gr * pr + gi * pi_, axis=0, keepdims=True)
        dlam_ref[:, h:] += jnp.sum(gi * pr - gr * pi_, axis=0, keepdims=True)

    spec = pl.BlockSpec((tm, w), lambda i: (nt - 1 - i, 0))
    row_spec = pl.BlockSpec((1, w), lambda i: (0, 0))
    return pl.pallas_call(
        body, name="s5_scan_bwd",
        out_shape=[jax.ShapeDtypeStruct((t, w), F32), jax.ShapeDtypeStruct((1, w), F32)],
        grid=(nt,),
        in_specs=[spec, spec, row_spec],
        out_specs=[spec, row_spec],
        scratch_shapes=[pltpu.VMEM((1, w), F32)],
        compiler_params=_params(("arbitrary",)),
    )(dx, xp, lam)


def _wkv_chunks(s0, r, lw, k, v, a, b):
    c = r[0].shape[0]
    row = lax.broadcasted_iota(jnp.int32, (c, c), 0)
    col = lax.broadcasted_iota(jnp.int32, (c, c), 1)
    incl, strict = col <= row, col < row
    tri = incl.astype(F32)
    eye = (row == col).astype(F32)
    each = lambda f, *xs: [f(*t) for t in zip(*xs)]
    stack = lambda p, q: jnp.concatenate([p, q], axis=0)
    dot = lambda p, q, dims=_NN: _dot(p, q, dims, WKV_PASSES)
    lc = each(lambda l: _dot(tri, l, _NN, 2), lw)
    e_neg = each(lambda l: jnp.exp(-l), lc)
    ar = each(lambda x, z, l, w: stack(x * jnp.exp(l - w), z * jnp.exp(l)), a, r, lc, lw)
    bk = each(lambda x, z, e: stack(x * e, z * e), b, k, e_neg)
    m = each(lambda p, q: dot(p, q, _NT), ar, bk)
    mab = each(lambda q: jnp.where(strict, q[:c, :c], 0.0), m)
    mak_mrk = each(lambda q: stack(jnp.where(strict, q[:c, c:], 0.0), jnp.where(incl, q[c:, c:], 0.0)), m)
    mrb = each(lambda q: jnp.where(incl, q[c:, :c], 0.0), m)
    xy = each(lambda p, s, q, z: dot(p, s, _NT) + dot(q, z), ar, s0, mak_mrk, v)
    inv = each(lambda q: eye + q, mab)
    pw = each(lambda q: dot(q, q), mab)
    for _ in range(int(math.log2(c)) - 2):
        both = each(lambda i, q: dot(stack(i, q), q), inv, pw)
        inv = each(lambda i, q: i + q[:c], inv, both)
        pw = each(lambda q: q[c:], both)
    inv = each(lambda i, q: i + dot(i, q), inv, pw)
    u = each(lambda i, q: dot(i, q[:c]), inv, xy)
    y = each(lambda q, z, p: q[c:] + dot(z, p), xy, mrb, u)
    e_tot = each(lambda l: jnp.exp(jnp.sum(l, axis=0, keepdims=True)), lw)
    s1 = each(lambda s, p, z, q, e: (s + dot(stack(p, z), q, _TN)) * e, s0, u, v, bk, e_tot)
    return y, s1


def _carry(plan, n_args, n_outs):
    n_in, n_out = len(plan.ins), len(plan.out_shape)

    def parts(refs):
        base = n_args + n_in + n_outs
        return refs[n_args:n_args + n_in], refs[base:base + n_out], refs[base + n_out + 1:]

    return parts, [HBM_SPEC] * n_in, list(plan.out_shape), [HBM_SPEC] * n_out, list(plan.sems)


def _head_cols(ref):
    return tuple(ref[:, h * HEAD:(h + 1) * HEAD] for h in range(HEADS))


def _wkv_fwd(seqs, plan):
    t, w = seqs[0].shape
    c, n = WKV_CHUNK, HEAD
    nc = t // c
    parts, plan_in_specs, plan_out_shape, plan_out_specs, plan_sems = _carry(plan, 6, 2)

    def body(*refs):
        ins, (y_ref, ck_ref) = refs[:6], refs[6 + len(plan.ins):8 + len(plan.ins)]
        s_ref = refs[8 + len(plan.ins) + len(plan.out_shape)]

        @pl.when(pl.program_id(0) == 0)
        def _():
            s_ref[...] = jnp.zeros_like(s_ref)
            plan.start(*parts(refs))

        s0 = tuple(s_ref[h] for h in range(HEADS))
        ys, s1 = _wkv_chunks(s0, *[_head_cols(r) for r in ins])
        for h in range(HEADS):
            ck_ref[0, h] = s0[h]
            y_ref[:, h * n:(h + 1) * n] = ys[h]
            s_ref[h] = s1[h]

        @pl.when(pl.program_id(0) == nc - 1)
        def _():
            plan.wait(*parts(refs))

    spec = pl.BlockSpec((c, w), lambda i: (i, 0))
    res = pl.pallas_call(
        body, name="wkv_fwd",
        out_shape=[jax.ShapeDtypeStruct((t, w), F32), jax.ShapeDtypeStruct((nc, HEADS, n, n), F32)] + plan_out_shape,
        grid=(nc,),
        in_specs=[spec] * 6 + plan_in_specs,
        out_specs=[spec, pl.BlockSpec((1, HEADS, n, n), lambda i: (i, 0, 0, 0))] + plan_out_specs,
        scratch_shapes=[pltpu.VMEM((HEADS, n, n), F32)] + plan_sems,
        compiler_params=_params(("arbitrary",)),
    )(*seqs, *plan.ins)
    return res[0], res[1], res[2:]


def _wkv_bwd(seqs, ck, dy, plan):
    t, w = seqs[0].shape
    c, n = WKV_CHUNK, HEAD
    nc = t // c
    parts, plan_in_specs, plan_out_shape, plan_out_specs, plan_sems = _carry(plan, 8, 6)

    def body(*refs):
        ins, ck_ref, dy_ref = refs[:6], refs[6], refs[7]
        outs = refs[8 + len(plan.ins):14 + len(plan.ins)]
        ds_ref = refs[14 + len(plan.ins) + len(plan.out_shape)]

        @pl.when(pl.program_id(0) == 0)
        def _():
            ds_ref[...] = jnp.zeros_like(ds_ref)
            plan.start(*parts(refs))

        s0 = tuple(ck_ref[0, h] for h in range(HEADS))
        _, vjp = jax.vjp(_wkv_chunks, s0, *[_head_cols(r) for r in ins])
        g = vjp((list(_head_cols(dy_ref)), [ds_ref[h] for h in range(HEADS)]))
        for h in range(HEADS):
            ds_ref[h] = g[0][h]
            for o, d in zip(outs, g[1:]):
                o[:, h * n:(h + 1) * n] = d[h]

        @pl.when(pl.program_id(0) == nc - 1)
        def _():
            plan.wait(*parts(refs))

    spec = pl.BlockSpec((c, w), lambda i: (nc - 1 - i, 0))
    res = pl.pallas_call(
        body, name="wkv_bwd",
        out_shape=[jax.ShapeDtypeStruct((t, w), F32)] * 6 + plan_out_shape,
        grid=(nc,),
        in_specs=[spec] * 6 + [pl.BlockSpec((1, HEADS, n, n), lambda i: (nc - 1 - i, 0, 0, 0)), spec] + plan_in_specs,
        out_specs=[spec] * 6 + plan_out_specs,
        scratch_shapes=[pltpu.VMEM((HEADS, n, n), F32)] + plan_sems,
        compiler_params=_params(("arbitrary",)),
    )(*seqs, ck, dy, *plan.ins)
    return res[:6], res[6:]


def _coords():
    return lax.axis_index("x"), lax.axis_index("y"), lax.axis_index("c")


def _flip(v, f):
    return 1 - v if f else v


_CHIP_FLIPS = [(1, 0), (0, 1), (1, 1)]
_DEV_FLIPS = [(fx, fy, fc) for fx in (0, 1) for fy in (0, 1) for fc in (0, 1) if (fx, fy, fc) != (0, 0, 0)]
HBM_SPEC = pl.BlockSpec(memory_space=pl.ANY)


def _chip_peer(k, x, y):
    fx, fy = _CHIP_FLIPS[k]
    return _flip(x, fx), _flip(y, fy)


def _dev_peer(k, x, y, c):
    fx, fy, fc = _DEV_FLIPS[k]
    return _flip(x, fx), _flip(y, fy), _flip(c, fc)


def _rows_of_core(ref, core):
    h = ref.shape[-2] // 2
    rows = pl.ds(pl.multiple_of(core * h, 8), h)
    return ref.at[rows, :] if len(ref.shape) == 2 else ref.at[:, rows, :]


class _Plan(NamedTuple):
    ins: Sequence[Any]
    out_shape: Sequence[Any]
    sems: Sequence[Any]
    start: Callable
    wait: Callable


def _run_plan(name, plan):
    n_in, n_out = len(plan.ins), len(plan.out_shape)

    def body(*refs):
        parts = refs[:n_in], refs[n_in:n_in + n_out], refs[n_in + n_out:]
        plan.start(*parts)
        plan.wait(*parts)

    return pl.pallas_call(
        body, name=name, out_shape=list(plan.out_shape),
        in_specs=[HBM_SPEC] * n_in, out_specs=[HBM_SPEC] * n_out, scratch_shapes=list(plan.sems),
    )(*plan.ins)


def _gather_plan(shards):
    n = len(shards)

    def copies(srcs, outs, sems):
        send_sems, recv_sems, local_sems = sems
        x, y, c = _coords()
        me = 2 * x + y

        def remote(i, k, arriving):
            px, py = _chip_peer(k, x, y)
            return pltpu.make_async_remote_copy(
                src_ref=srcs[i], dst_ref=outs[i].at[2 * px + py if arriving else me],
                send_sem=send_sems.at[i, k], recv_sem=recv_sems.at[i, k],
                device_id=(px, py, c), device_id_type=MESH)

        own = [pltpu.make_async_copy(srcs[i], outs[i].at[me], local_sems.at[i]) for i in range(n)]
        pairs = [(i, k) for k in range(3) for i in range(n)]
        return own, [remote(i, k, False) for i, k in pairs], [remote(i, k, True) for i, k in pairs]

    return _Plan(
        ins=shards, out_shape=[jax.ShapeDtypeStruct((4,) + s.shape, s.dtype) for s in shards],
        sems=[pltpu.SemaphoreType.DMA((n, 3)), pltpu.SemaphoreType.DMA((n, 3)), pltpu.SemaphoreType.DMA((n,))],
        start=functools.partial(_start_copies, copies), wait=functools.partial(_wait_copies, copies))


def _start_copies(copies, ins, outs, sems):
    own, sends, _ = copies(ins, outs, sems)
    for cp in own + sends:
        cp.start()


def _wait_copies(copies, ins, outs, sems):
    own, sends, arrivals = copies(ins, outs, sems)
    for cp in arrivals:
        cp.wait_recv()
    for cp in sends:
        cp.wait_send()
    for cp in own:
        cp.wait()


def _exchange_plan(gs, small=None):
    n = len(gs)
    arrays = list(gs) + ([] if small is None else [small])

    def copies(srcs, outs, sems):
        send_sems, recv_sems, local_sems = sems
        x, y, c = _coords()
        me = 4 * x + 2 * y + c

        def piece(i, px, py, pc):
            if i == n:
                return srcs[i].at[4 * px + 2 * py + pc]
            return _rows_of_core(srcs[i].at[2 * px + py], pc)

        def remote(i, k, arriving):
            px, py, pc = _dev_peer(k, x, y, c)
            return pltpu.make_async_remote_copy(
                src_ref=piece(i, px, py, pc), dst_ref=outs[i].at[4 * px + 2 * py + pc if arriving else me],
                send_sem=send_sems.at[i, k], recv_sem=recv_sems.at[i, k],
                device_id=(px, py, pc), device_id_type=MESH)

        own = [pltpu.make_async_copy(piece(i, x, y, c), outs[i].at[me], local_sems.at[i]) for i in range(len(arrays))]
        pairs = [(i, k) for k in range(7) for i in range(len(arrays))]
        return own, [remote(i, k, False) for i, k in pairs], [remote(i, k, True) for i, k in pairs]

    out_shape = [jax.ShapeDtypeStruct((8, g.shape[1] // 2, g.shape[2]), g.dtype) for g in gs]
    if small is not None:
        out_shape.append(jax.ShapeDtypeStruct(small.shape, small.dtype))
    m = len(arrays)
    return _Plan(
        ins=arrays, out_shape=out_shape,
        sems=[pltpu.SemaphoreType.DMA((m, 7)), pltpu.SemaphoreType.DMA((m, 7)), pltpu.SemaphoreType.DMA((m,))],
        start=functools.partial(_start_copies, copies), wait=functools.partial(_wait_copies, copies))


def _share_cores(halves, small):
    n = len(halves)

    def body(*refs):
        srcs, small_src, outs, small_out = refs[:n], refs[n], refs[n + 1:2 * n + 1], refs[2 * n + 1]
        mine, theirs = refs[2 * n + 2:3 * n + 2], refs[3 * n + 2:4 * n + 2]
        send_sems, recv_sems, ssend, srecv, local_sems = refs[4 * n + 2:]
        x, y, c = _coords()
        me = 4 * x + 2 * y + c

        def big(i):
            return pltpu.make_async_remote_copy(
                src_ref=mine[i], dst_ref=theirs[i], send_sem=send_sems.at[i], recv_sem=recv_sems.at[i],
                device_id=(x, y, 1 - c), device_id_type=MESH)

        def tiny(k, arriving):
            px, py, pc = _dev_peer(k, x, y, c)
            return pltpu.make_async_remote_copy(
                src_ref=small_src, dst_ref=small_out.at[4 * px + 2 * py + pc if arriving else me],
                send_sem=ssend.at[k], recv_sem=srecv.at[k], device_id=(px, py, pc), device_id_type=MESH)

        small_sends = [tiny(k, False) for k in range(7)]
        own_small = pltpu.make_async_copy(small_src, small_out.at[me], local_sems.at[2 * n])
        stage = [pltpu.make_async_copy(srcs[i], mine[i], local_sems.at[i]) for i in range(n)]
        for cp in small_sends + [own_small] + stage:
            cp.start()
        sends = []
        for i in range(n):
            stage[i].wait()
            sends.append(big(i))
            sends[-1].start()
        store = [pltpu.make_async_copy(mine[i], outs[i].at[c], local_sems.at[i]) for i in range(n)]
        for cp in store:
            cp.start()
        for i in range(n):
            big(i).wait_recv()
            store.append(pltpu.make_async_copy(theirs[i], outs[i].at[1 - c], local_sems.at[n + i]))
            store[-1].start()
        for k in range(7):
            tiny(k, True).wait_recv()
        for cp in sends + small_sends:
            cp.wait_send()
        for cp in store + [own_small]:
            cp.wait()

    staged = [pltpu.VMEM(s.shape, s.dtype) for s in halves]
    res = pl.pallas_call(
        body, name="share_cores",
        out_shape=[jax.ShapeDtypeStruct((2,) + s.shape, s.dtype) for s in halves]
        + [jax.ShapeDtypeStruct((8,) + small.shape, small.dtype)],
        in_specs=[HBM_SPEC] * (n + 1), out_specs=[HBM_SPEC] * (n + 1),
        scratch_shapes=staged + staged + [
            pltpu.SemaphoreType.DMA((n,)), pltpu.SemaphoreType.DMA((n,)),
            pltpu.SemaphoreType.DMA((7,)), pltpu.SemaphoreType.DMA((7,)),
            pltpu.SemaphoreType.DMA((2 * n + 1,))],
        compiler_params=pltpu.CompilerParams(vmem_limit_bytes=VMEM_LIMIT_BYTES),
    )(*halves, small)
    return res[:n], res[n]


def _row_tile(n, target):
    return max(d for d in range(8, min(n, target) + 1, 8) if n % d == 0)


def _ew(name, fn, ins, outs, block_bytes=1 << 20):
    rows, cols = ins[0].shape[-2:]
    lead = max(math.prod(a.shape[:-2]) for a in ins)
    tr = _row_tile(rows, max(8, block_bytes // (4 * cols * lead)))
    n = len(ins)

    def spec(shape):
        if len(shape) == 2:
            return pl.BlockSpec((tr, cols), lambda i: (i, 0))
        return pl.BlockSpec((shape[0], tr, cols), lambda i: (0, i, 0))

    def body(*refs):
        res = fn(*[r[...] for r in refs[:n]])
        for r, v in zip(refs[n:], res):
            r[...] = v

    return pl.pallas_call(
        body, name=name,
        out_shape=[jax.ShapeDtypeStruct(s, F32) for s in outs],
        grid=(rows // tr,),
        in_specs=[spec(a.shape) for a in ins],
        out_specs=[spec(s) for s in outs],
        compiler_params=_params(("parallel",)),
    )(*ins)


def _sum_slots(a):
    total = a[0]
    for s in range(1, a.shape[0]):
        total = total + a[s]
    return (total,)


def _adamw(g, w, m, v):
    bc1 = 1.0 - ADAM_B1 ** ADAM_STEP
    bc2 = 1.0 - ADAM_B2 ** ADAM_STEP
    m_new = ADAM_B1 * m + (1.0 - ADAM_B1) * g
    v_new = ADAM_B2 * v + (1.0 - ADAM_B2) * jnp.square(g)
    delta = -ADAM_LR * ((m_new / bc1) / (jnp.sqrt(v_new / bc2) + ADAM_EPS) + ADAM_WD * w)
    return delta, m_new, v_new


def _mat(a):
    return a.reshape(a.shape[-2:])


def _to_shard_major(full, axis):
    rows, cols = full.shape
    if axis == 0:
        return full.reshape(4, rows // 4, cols)
    return full.reshape(rows, 4, cols // 4).transpose(1, 0, 2)


def _from_shard_major(a, axis):
    _, r, cs = a.shape
    if axis == 0:
        return a.reshape(4 * r, cs)
    return a.transpose(1, 0, 2).reshape(r, 4 * cs)


def _pack_small(arrays):
    flat = jnp.concatenate([arrays[n].reshape(-1) for n in SMALL_NAMES])
    flat = jnp.pad(flat, (0, 8 * SMALL_ROWS * PACK_COLS - flat.shape[0]))
    return flat.reshape(8, SMALL_ROWS, PACK_COLS)


def _unpack_small(packed, shapes):
    flat = packed.reshape(-1)
    out, off = {}, 0
    for n in SMALL_NAMES:
        size = math.prod(shapes[n])
        out[n] = flat[off:off + size].reshape(shapes[n])
        off += size
    return out


def _row(a):
    return a.reshape(1, -1)


def _local_step(x, p, target, wf, ws, late_gather):
    wf = dict(wf)
    t = x.shape[0]
    tm = min(256, t)
    g = {}

    lam_re, lam_im = ws['s5_lam_re'].reshape(S5_GROUPS, S5_STATE), ws['s5_lam_im'].reshape(S5_GROUPS, S5_STATE)
    log_step = ws['s5_log_step'].reshape(S5_GROUPS, 1)
    gp = (S5_GROUPS, S5_STATE)
    lam_ins = (lam_re, lam_im, log_step)
    lbr, lbi, cfr, cfi = _small_fwd("s5_lam", _f_s5_lam, lam_ins, [(gp, F32)] * 4)
    lam_row = jnp.concatenate([_row(lbr), _row(lbi)], axis=1)
    to_t = lambda a, perm: a.reshape((S5_GROUPS,) + a.shape[-2:]).transpose(perm).reshape(S5_GROUP, S5_LANES)
    build_ins = (_row(cfr), _row(cfi), to_t(ws['s5_b_re'], (2, 0, 1)), to_t(ws['s5_b_im'], (2, 0, 1)),
                 to_t(ws['s5_c_re'], (1, 0, 2)), to_t(ws['s5_c_im'], (1, 0, 2)))
    stack_shape = (S5_WIDTH, 2 * S5_LANES)
    bstack, cstack_t = _small_fwd("s5_build", _f_s5_build, build_ins, [(stack_shape, F32)] * 2)

    norm_mix, norm_ffn, norm_ple = _row(ws['norm_mix']), _row(ws['norm_ffn']), _row(ws['norm_ple'])
    final_norm = _row(ws['final_norm'])
    (xn,) = _tok_fwd("norm_in", _f_norm_in, [x], [norm_mix], [(x.shape[1], BF16)], tm)
    u = _mm("proj_s5", xn, wf['w_in'][:, :S5_WIDTH], 'nn')
    z = _mm("proj_rw", xn, wf['w_in'][:, S5_WIDTH:], 'nn')

    bu = _mm("s5_bu", u, bstack, 'nn')
    xs, xs_prev = _s5_scan(bu, lam_row, tm)
    ypre = _mm("s5_y", xs, cstack_t, 'nt')
    s5_par = [_row(ws['s5_d']), wf['s5_glu_w'], _row(ws['s5_glu_b'])]
    (s5_out,) = _tok_fwd("s5_post", _f_s5_post, [ypre, u], s5_par, [(S5_WIDTH, BF16)], tm)

    pre_par = [_row(ws['rw_shift_mu']), _row(ws['rw_w0']), wf['rw_w2'], _row(ws['rw_a0']), wf['rw_a2'],
               wf['rw_g2'], _row(ws['rw_k_k']), _row(ws['rw_k_a'])]
    r, lw, kp, v, an, bn, gate = _rw_pre_fwd(z, pre_par, tm)
    seqs = [r, lw, kp, v, an, bn]
    y_wkv, ck, late = _wkv_fwd(seqs, late_gather)
    wf.update({n: _from_shard_major(a, SHARDED[n]) for n, a in zip(LATE_NAMES, late)})
    post_par = [_row(ws['rw_ln_w']), _row(ws['rw_ln_b']), _row(ws['rw_r_k'])]
    post_toks = [y_wkv, r, kp, v, gate]
    (rw_out,) = _tok_fwd("rw_post", _f_rw_post, post_toks, post_par, [(RWKV_WIDTH, BF16)], tm)

    mixcat = jnp.concatenate([s5_out, rw_out], axis=1)
    mixed = _mm("mix_out", mixcat, wf['w_out'], 'nn')
    h1, hn = _tok_fwd("mix_res", _f_mix_res, [x, mixed], [norm_ffn], [(x.shape[1], F32), (x.shape[1], BF16)], tm)
    w13 = jnp.concatenate([wf['ffn_w1'], wf['ffn_w3']], axis=1)
    a13 = _mm("ffn_up", hn, w13, 'nn')
    (f,) = _tok_fwd("ffn_act", _f_ffn_act, [a13], [], [(FFN_HIDDEN, BF16)], tm)
    ffo = _mm("ffn_down", f, wf['ffn_w2'], 'nn')
    h2, hp = _tok_fwd("ffn_res", _f_ffn_res, [h1, ffo], [norm_ple], [(x.shape[1], F32), (x.shape[1], BF16)], tm)
    gpre = _mm("ple_gate", hp, wf['ple_gate_w'], 'nn')
    pu = _mm("ple_up", p, wf['ple_up_w'], 'nn')

    dh2, dgpre, dpu, g['final_norm'], loss = _tok_bwd(
        "loss", _f_loss, [h2, gpre, pu, target], [final_norm], [None],
        [F32, BF16, BF16, None], [True], tm, acc_out=0)
    g['ple_gate_w'] = _mm("d_ple_gate_w", hp, dgpre, 'tn')
    g['ple_up_w'] = _mm("d_ple_up_w", p, dpu, 'tn')
    dhp = _mm("d_hp", dgpre, wf['ple_gate_w'], 'nt')
    dh1, dffo, g['norm_ple'] = _tok_bwd("ffn_res_bwd", _f_ffn_res, [h1, ffo], [norm_ple], [dh2, dhp],
                                        [F32, BF16], [True], tm)
    g['ffn_w2'] = _mm("d_ffn_w2", f, dffo, 'tn')
    df = _mm("d_f", dffo, wf['ffn_w2'], 'nt')
    (da13,) = _tok_bwd("ffn_act_bwd", _f_ffn_act, [a13], [], [df], [BF16], [], tm)
    dw13 = _mm("d_ffn_w13", hn, da13, 'tn')
    g['ffn_w1'], g['ffn_w3'] = dw13[:, :FFN_HIDDEN], dw13[:, FFN_HIDDEN:]
    dhn = _mm("d_hn", da13, w13, 'nt')
    dx_a, dmixed, g['norm_ffn'] = _tok_bwd("mix_res_bwd", _f_mix_res, [x, mixed], [norm_ffn], [dh1, dhn],
                                           [F32, BF16], [True], tm)
    g['w_out'] = _mm("d_w_out", mixcat, dmixed, 'tn')
    dmixcat = _mm("d_mixcat", dmixed, wf['w_out'], 'nt')
    ds5_out, drw_out = dmixcat[:, :S5_WIDTH], dmixcat[:, S5_WIDTH:]

    dy_wkv, dr_b, dkp_b, dv_b, dgate, g['rw_ln_w'], g['rw_ln_b'], g['rw_r_k'] = _tok_bwd(
        "rw_post_bwd", _f_rw_post, post_toks, post_par, [drw_out], [F32] * 5, [True] * 3, tm)
    late_exchange = _exchange_plan([_to_shard_major(g[n], SHARDED[n]) for n in LATE_NAMES])
    dseqs, late_pieces = _wkv_bwd(seqs, ck, dy_wkv, late_exchange)
    pre_cots = [(dseqs[0], dr_b), (dseqs[1],), (dseqs[2], dkp_b), (dseqs[3], dv_b), (dseqs[4],), (dseqs[5],),
                (dgate,)]
    dz, *dpre = _rw_pre_bwd(z, pre_par, pre_cots, tm)
    for n, d in zip(['rw_shift_mu', 'rw_w0', 'rw_w2', 'rw_a0', 'rw_a2', 'rw_g2', 'rw_k_k', 'rw_k_a'], dpre):
        g[n] = d

    dypre, du_a, g['s5_d'], g['s5_glu_w'], g['s5_glu_b'] = _tok_bwd(
        "s5_post_bwd", _f_s5_post, [ypre, u], s5_par, [ds5_out], [F32, F32], [True] * 3, tm)
    dxs = _mm("d_s5_x", dypre, cstack_t, 'nn')
    dcstack_t = _mm("d_s5_c", dypre, xs, 'tn')
    dbu, dlam_row = _s5_scan_bwd(dxs, xs_prev, lam_row, tm)
    du_b = _mm("d_s5_u", dbu, bstack, 'nt')
    dbstack = _mm("d_s5_b", u, dbu, 'tn')
    dbuild = _small_bwd("s5_build_bwd", _f_s5_build, build_ins, (dbstack, dcstack_t))
    lam_cots = (dlam_row[:, :S5_LANES].reshape(gp), dlam_row[:, S5_LANES:].reshape(gp),
                dbuild[0].reshape(gp), dbuild[1].reshape(gp))
    g['s5_lam_re'], g['s5_lam_im'], g['s5_log_step'] = _small_bwd("s5_lam_bwd", _f_s5_lam, lam_ins, lam_cots)
    from_t = lambda a, perm: a.reshape(S5_GROUP, S5_GROUPS, S5_STATE).transpose(perm)
    g['s5_b_re'], g['s5_b_im'] = from_t(dbuild[2], (1, 2, 0)), from_t(dbuild[3], (1, 2, 0))
    g['s5_c_re'], g['s5_c_im'] = from_t(dbuild[4], (1, 0, 2)), from_t(dbuild[5], (1, 0, 2))

    dproj = jnp.concatenate([(du_a + du_b).astype(BF16), dz.astype(BF16)], axis=1)
    g['w_in'] = _mm("d_w_in", xn, dproj, 'tn')
    dxn = _mm("d_xn", dproj, wf['w_in'], 'nt')
    grad_x, g['norm_mix'] = _tok_bwd("norm_in_bwd", _f_norm_in, [x], [norm_mix], [dxn], [F32], [True], tm,
                                     add_to=(0, dx_a))
    return loss[0, 0], grad_x, g, late_pieces


def _step(x, p, target, w, m, v):
    shards = {n: _mat(w[n]).astype(BF16) for n in SHARDED_NAMES}
    early = _run_plan("gather_early", _gather_plan([shards[n] for n in EARLY_NAMES]))
    wf = {n: _from_shard_major(a, SHARDED[n]) for n, a in zip(EARLY_NAMES, early)}
    ws = {n: w[n] for n in SMALL_NAMES}

    late_gather = _gather_plan([shards[n] for n in LATE_NAMES])
    loss, grad_x, g, late_pieces = _local_step(x[0], p[0, 0], target[0], wf, ws, late_gather)

    early_plan = _exchange_plan([_to_shard_major(g[n], SHARDED[n]) for n in EARLY_NAMES],
                                _pack_small({n: g[n] for n in SMALL_NAMES}))
    *early_pieces, by_dev = _run_plan("exchange_early", early_plan)
    pieces = dict(zip(LATE_NAMES + EARLY_NAMES, list(late_pieces) + early_pieces))
    halves = [_ew("add_devices_" + n, _sum_slots, [pieces[n]], [pieces[n].shape[1:]])[0] for n in SHARDED_NAMES]
    (small_piece,) = _ew("add_devices_small", _sum_slots, [by_dev], [by_dev.shape[1:]])
    both, small_g = _share_cores(halves, small_piece)

    kinds = [{}, {}, {}, {}]
    for n, gn in zip(SHARDED_NAMES, both):
        shard = _mat(w[n]).shape
        res = _ew("adamw_" + n, _adamw, [gn.reshape(shard), _mat(w[n]), _mat(m[n]), _mat(v[n])], [shard] * 3)
        for kind, a in zip(kinds, [gn] + list(res)):
            kind[n] = a.reshape(w[n].shape)
    flat = (8 * SMALL_ROWS, PACK_COLS)
    packed = [_pack_small({n: d[n] for n in SMALL_NAMES}).reshape(flat) for d in (w, m, v)]
    small_res = _ew("adamw_small", _adamw, [small_g.reshape(flat)] + packed, [flat] * 3)
    small_shapes = {n: w[n].shape for n in SMALL_NAMES}
    for kind, a in zip(kinds, [small_g] + list(small_res)):
        kind.update(_unpack_small(a, small_shapes))
    total = lax.psum(loss, ("x", "y", "c"))
    return (total, grad_x[None], *[kind[n] for kind in kinds for n in WEIGHT_NAMES])


def kernel(x, p, norm_mix, w_in, s5_lam_re, s5_lam_im, s5_log_step, s5_b_re, s5_b_im, s5_c_re, s5_c_im, s5_d, s5_glu_w, s5_glu_b, rw_shift_mu, rw_w0, rw_w2, rw_a0, rw_a2, rw_g2, rw_k_k, rw_k_a, rw_r_k, rw_ln_w, rw_ln_b, w_out, norm_ffn, ffn_w1, ffn_w3, ffn_w2, norm_ple, ple_gate_w, ple_up_w, final_norm, loss_target, m_norm_mix, m_w_in, m_s5_lam_re, m_s5_lam_im, m_s5_log_step, m_s5_b_re, m_s5_b_im, m_s5_c_re, m_s5_c_im, m_s5_d, m_s5_glu_w, m_s5_glu_b, m_rw_shift_mu, m_rw_w0, m_rw_w2, m_rw_a0, m_rw_a2, m_rw_g2, m_rw_k_k, m_rw_k_a, m_rw_r_k, m_rw_ln_w, m_rw_ln_b, m_w_out, m_norm_ffn, m_ffn_w1, m_ffn_w3, m_ffn_w2, m_norm_ple, m_ple_gate_w, m_ple_up_w, m_final_norm, v_norm_mix, v_w_in, v_s5_lam_re, v_s5_lam_im, v_s5_log_step, v_s5_b_re, v_s5_b_im, v_s5_c_re, v_s5_c_im, v_s5_d, v_s5_glu_w, v_s5_glu_b, v_rw_shift_mu, v_rw_w0, v_rw_w2, v_rw_a0, v_rw_a2, v_rw_g2, v_rw_k_k, v_rw_k_a, v_rw_r_k, v_rw_ln_w, v_rw_ln_b, v_w_out, v_norm_ffn, v_ffn_w1, v_ffn_w3, v_ffn_w2, v_norm_ple, v_ple_gate_w, v_ple_up_w, v_final_norm):
    args = dict(locals())
    w = {n: args[n] for n in WEIGHT_NAMES}
    m = {n: args["m_" + n] for n in WEIGHT_NAMES}
    v = {n: args["v_" + n] for n in WEIGHT_NAMES}
    return _step(x, p, loss_target, w, m, v)
```

```python
import functools
import math
from typing import Any, Callable, NamedTuple, Sequence

import jax
import jax.numpy as jnp
from jax import lax
from jax.experimental import pallas as pl
from jax.experimental.pallas import tpu as pltpu

F32 = jnp.float32
BF16 = jnp.bfloat16
MESH = pl.DeviceIdType.MESH

S5_WIDTH = 512
RWKV_WIDTH = 512
S5_GROUP = 16
S5_GROUPS = 32
S5_STATE = 64
S5_LANES = S5_GROUPS * S5_STATE
HEAD = 64
HEADS = 8
DECAY_LORA = 64
AAA_LORA = 64
GATE_LORA = 128
FFN_HIDDEN = 2816
RMS_EPS = 1e-6
GN_EPS = 64e-5
L2_EPS = 1e-12
ADAM_LR = 0.001
ADAM_B1 = 0.9
ADAM_B2 = 0.999
ADAM_EPS = 1e-08
ADAM_WD = 0.01
ADAM_STEP = 10

WKV_CHUNK = 64
WIRE = jnp.bfloat16
WKV_PASSES = 1
VMEM_LIMIT_BYTES = 48 * 1024 * 1024
LANE = 128
PACK_COLS = 1024
SMALL_ROWS = 24

WEIGHT_NAMES = ['norm_mix', 'w_in', 's5_lam_re', 's5_lam_im', 's5_log_step', 's5_b_re', 's5_b_im', 's5_c_re',
                's5_c_im', 's5_d', 's5_glu_w', 's5_glu_b', 'rw_shift_mu', 'rw_w0', 'rw_w2', 'rw_a0', 'rw_a2',
                'rw_g2', 'rw_k_k', 'rw_k_a', 'rw_r_k', 'rw_ln_w', 'rw_ln_b', 'w_out', 'norm_ffn', 'ffn_w1',
                'ffn_w3', 'ffn_w2', 'norm_ple', 'ple_gate_w', 'ple_up_w', 'final_norm']
SHARDED = {'w_in': 1, 's5_glu_w': 0, 'rw_w2': 1, 'rw_a2': 1, 'rw_g2': 1, 'w_out': 0, 'ffn_w1': 1, 'ffn_w3': 1,
           'ffn_w2': 0, 'ple_gate_w': 0, 'ple_up_w': 1}
SHARDED_NAMES = [n for n in WEIGHT_NAMES if n in SHARDED]
LATE_NAMES = ['w_out', 'ffn_w1', 'ffn_w3', 'ffn_w2', 'ple_gate_w', 'ple_up_w']
EARLY_NAMES = [n for n in SHARDED_NAMES if n not in LATE_NAMES]
SMALL_NAMES = [n for n in WEIGHT_NAMES if n not in SHARDED]


def _params(sem=None):
    return pltpu.CompilerParams(dimension_semantics=sem, vmem_limit_bytes=VMEM_LIMIT_BYTES)


def _tile(n, target):
    best = None
    for d in range(LANE, min(n, target) + 1, LANE):
        if n % d == 0:
            best = d
    return n if best is None else best


_NN = (((1,), (0,)), ((), ()))
_NT = (((1,), (1,)), ((), ()))
_TN = (((0,), (0,)), ((), ()))


def _split(a):
    a = a.astype(F32)
    hi = a.astype(BF16)
    return hi, (a - hi.astype(F32)).astype(BF16)


def _dg(a, b, dims, passes):
    dg = lambda p, q: lax.dot_general(p, q, dims, preferred_element_type=F32)
    if passes == 1:
        return dg(a.astype(BF16), b.astype(BF16))
    bh, bl = _split(b)
    if passes == 2:
        return dg(a.astype(BF16), bh) + dg(a.astype(BF16), bl)
    ah, al = _split(a)
    return dg(ah, bh) + (dg(ah, bl) + dg(al, bh))


_DOT_BWD = {_NN: (("g", "b", _NT), ("a", "g", _TN)),
            _NT: (("g", "b", _NN), ("g", "a", _TN)),
            _TN: (("b", "g", _NT), ("a", "g", _NN))}


@functools.partial(jax.custom_vjp, nondiff_argnums=(2, 3))
def _dot(a, b, dims, passes):
    return _dg(a, b, dims, passes)


def _dot_fwd(a, b, dims, passes):
    return _dg(a, b, dims, passes), (a, b)


def _dot_bwd(dims, passes, res, g):
    env = {"a": res[0], "b": res[1], "g": g}
    return tuple(_dg(env[p], env[q], d, passes) for p, q, d in _DOT_BWD[dims])


_dot.defvjp(_dot_fwd, _dot_bwd)


def _bdot(x, w):
    return _dot(x, w, _NN, 1)


def _fdot(a, b, dims=_NN):
    return _dot(a, b, dims, 3)


@jax.custom_vjp
def _shift_down(z):
    return pltpu.roll(z, 1, 0)


def _shift_down_fwd(z):
    return pltpu.roll(z, 1, 0), None


def _shift_down_bwd(_, g):
    return (pltpu.roll(g, g.shape[0] - 1, 0),)


_shift_down.defvjp(_shift_down_fwd, _shift_down_bwd)


def _head_ones():
    r = lax.broadcasted_iota(jnp.int32, (RWKV_WIDTH, RWKV_WIDTH), 0) // HEAD
    c = lax.broadcasted_iota(jnp.int32, (RWKV_WIDTH, RWKV_WIDTH), 1) // HEAD
    return (r == c).astype(F32)


def _mm(name, a, b, mode, out_dtype=F32, precise=False, tm=1024, tn=1024, tk=1536):
    if mode == 'nn':
        (m, k), (_, n) = a.shape, b.shape
    elif mode == 'nt':
        (m, k), (n, _) = a.shape, b.shape
    else:
        (k, m), (_, n) = a.shape, b.shape
    tm, tn, tk = _tile(m, tm), _tile(n, tn), _tile(k, tk)
    nk = k // tk
    dims = {'nn': _NN, 'nt': _NT, 'tn': _TN}[mode]

    def body(a_ref, b_ref, o_ref, acc_ref):
        kk = pl.program_id(2)

        @pl.when(kk == 0)
        def _():
            acc_ref[...] = jnp.zeros_like(acc_ref)

        acc_ref[...] += _dg(a_ref[...], b_ref[...], dims, 3 if precise else 1)

        @pl.when(kk == nk - 1)
        def _():
            o_ref[...] = acc_ref[...].astype(o_ref.dtype)

    if mode == 'tn':
        a_spec = pl.BlockSpec((tk, tm), lambda i, j, l: (l, i))
    else:
        a_spec = pl.BlockSpec((tm, tk), lambda i, j, l: (i, l))
    if mode == 'nt':
        b_spec = pl.BlockSpec((tn, tk), lambda i, j, l: (j, l))
    else:
        b_spec = pl.BlockSpec((tk, tn), lambda i, j, l: (l, j))
    return pl.pallas_call(
        body, name=name,
        out_shape=jax.ShapeDtypeStruct((m, n), out_dtype),
        grid=(m // tm, n // tn, nk),
        in_specs=[a_spec, b_spec],
        out_specs=pl.BlockSpec((tm, tn), lambda i, j, l: (i, j)),
        scratch_shapes=[pltpu.VMEM((tm, tn), F32)],
        compiler_params=_params(("parallel", "parallel", "arbitrary")),
    )(a, b)


def _full_spec(p):
    nd = p.ndim
    return pl.BlockSpec(p.shape, lambda i, nd=nd: (0,) * nd)


def _tok_fwd(name, fn, toks, params, outs, tm):
    t = toks[0].shape[0]
    nt, npar = len(toks), len(params)

    def body(*refs):
        tv = [r[...].astype(F32) for r in refs[:nt]]
        pv = [r[...].astype(F32) for r in refs[nt:nt + npar]]
        res = fn(*tv, *pv)
        for r, v in zip(refs[nt + npar:], res):
            r[...] = v.astype(r.dtype)

    return pl.pallas_call(
        body, name=name,
        out_shape=[jax.ShapeDtypeStruct((t, w), d) for w, d in outs],
        grid=(t // tm,),
        in_specs=[pl.BlockSpec((tm, a.shape[1]), lambda i: (i, 0)) for a in toks] + [_full_spec(p) for p in params],
        out_specs=[pl.BlockSpec((tm, w), lambda i: (i, 0)) for w, _ in outs],
        compiler_params=_params(("parallel",)),
    )(*toks, *params)


def _tok_bwd(name, fn, toks, params, cots, dtok, dpar, tm, acc_out=None, add_to=None):
    t = toks[0].shape[0]
    nt, npar = len(toks), len(params)
    cot_arrays = [c for c in cots if c is not None]
    ncot = len(cot_arrays)
    extra = [] if add_to is None else [add_to[1]]
    dtok_idx = [i for i, d in enumerate(dtok) if d is not None]
    dpar_idx = [i for i, d in enumerate(dpar) if d]

    def body(*refs):
        pos = 0
        tin = refs[pos:pos + nt]; pos += nt
        pin = refs[pos:pos + npar]; pos += npar
        cin = refs[pos:pos + ncot]; pos += ncot
        ein = refs[pos:pos + len(extra)]; pos += len(extra)
        dto = refs[pos:pos + len(dtok_idx)]; pos += len(dtok_idx)
        dpo = refs[pos:pos + len(dpar_idx)]; pos += len(dpar_idx)
        acc = refs[pos] if acc_out is not None else None
        first = pl.program_id(0) == 0

        tv = [r[...].astype(F32) for r in tin]
        pv = [r[...].astype(F32) for r in pin]
        res, vjp = jax.vjp(fn, *tv, *pv)
        cit = iter(cin)
        cs = tuple(jnp.ones_like(o) if c is None else next(cit)[...].astype(F32) for c, o in zip(cots, res))
        g = vjp(cs)
        for r, i in zip(dto, dtok_idx):
            v = g[i]
            if add_to is not None and add_to[0] == i:
                v = v + ein[0][...].astype(F32)
            r[...] = v.astype(r.dtype)

        @pl.when(first)
        def _():
            for r in dpo:
                r[...] = jnp.zeros_like(r)
            if acc is not None:
                acc[...] = jnp.zeros_like(acc)

        for r, i in zip(dpo, dpar_idx):
            r[...] += g[nt + i]
        if acc is not None:
            acc[...] += res[acc_out]

    out_shape = [jax.ShapeDtypeStruct(toks[i].shape, dtok[i]) for i in dtok_idx]
    out_shape += [jax.ShapeDtypeStruct(params[i].shape, F32) for i in dpar_idx]
    out_specs = [pl.BlockSpec((tm, toks[i].shape[1]), lambda i_: (i_, 0)) for i in dtok_idx]
    out_specs += [_full_spec(params[i]) for i in dpar_idx]
    if acc_out is not None:
        out_shape.append(jax.ShapeDtypeStruct((1, 1), F32))
        out_specs.append(pl.BlockSpec((1, 1), lambda i_: (0, 0)))
    tok_spec = lambda a: pl.BlockSpec((tm, a.shape[1]), lambda i_: (i_, 0))
    return pl.pallas_call(
        body, name=name,
        out_shape=out_shape,
        grid=(t // tm,),
        in_specs=[tok_spec(a) for a in toks] + [_full_spec(p) for p in params]
        + [tok_spec(c) for c in cot_arrays] + [tok_spec(e) for e in extra],
        out_specs=out_specs,
        compiler_params=_params(("arbitrary",)),
    )(*toks, *params, *cot_arrays, *extra)


def _small_fwd(name, fn, ins, outs):
    n = len(ins)

    def body(*refs):
        res = fn(*[r[...] for r in refs[:n]])
        for r, v in zip(refs[n:], res):
            r[...] = v.astype(r.dtype)

    return pl.pallas_call(
        body, name=name,
        out_shape=[jax.ShapeDtypeStruct(s, d) for s, d in outs],
        compiler_params=_params(),
    )(*ins)


def _small_bwd(name, fn, ins, cots):
    n = len(ins)

    def body(*refs):
        _, vjp = jax.vjp(fn, *[r[...] for r in refs[:n]])
        g = vjp(tuple(r[...] for r in refs[n:n + len(cots)]))
        for r, v in zip(refs[n + len(cots):], g):
            r[...] = v

    return pl.pallas_call(
        body, name=name,
        out_shape=[jax.ShapeDtypeStruct(a.shape, F32) for a in ins],
        compiler_params=_params(),
    )(*ins, *cots)


def _rms(x, g):
    return x * lax.rsqrt(jnp.mean(x * x, axis=-1, keepdims=True) + RMS_EPS) * g


def _f_norm_in(x, g):
    return (_rms(x, g),)


def _f_mix_res(x, mixed, g):
    h1 = x + mixed
    return h1, _rms(h1, g)


def _f_ffn_act(a13):
    a1, a3 = a13[:, :FFN_HIDDEN], a13[:, FFN_HIDDEN:]
    return (jax.nn.silu(a1) * a3,)


def _f_ffn_res(h1, ffo, g):
    h2 = h1 + ffo
    return h2, _rms(h2, g)


def _f_loss(h2, gpre, pu, target, g):
    h3 = h2 + jax.nn.sigmoid(gpre) * pu
    y = _rms(h3, g)
    err = jnp.square(y - target)
    return (0.5 * jnp.sum(jnp.mean(err, axis=-1, keepdims=True), axis=0, keepdims=True),)


def _f_s5_post(ypre, u, d, glu_w, glu_b):
    z = jax.nn.gelu(ypre + u * d)
    return (z * jax.nn.sigmoid(_bdot(z, glu_w) + glu_b),)


def _softplus(x):
    return jnp.maximum(x, 0.0) + jnp.log(1.0 + jnp.exp(-jnp.abs(x)))


def _f_rw_pre(z, carry, shift_mu, w0, w2, a0, a2, g2, k_k, k_a):
    rw = RWKV_WIDTH
    first_row = lax.broadcasted_iota(jnp.int32, z.shape, 0) == 0
    prev = jnp.where(first_row, carry, _shift_down(z))
    zs = z + (prev - z) * shift_mu
    o1, o2 = 3 * rw + DECAY_LORA, 3 * rw + DECAY_LORA + AAA_LORA
    r, k, v = zs[:, :rw], zs[:, rw:2 * rw], zs[:, 2 * rw:3 * rw]
    wl, al, gl = zs[:, 3 * rw:o1], zs[:, o1:o2], zs[:, o2:]
    w = -_softplus(-(w0 + _bdot(jnp.tanh(wl), w2))) - 0.5
    log_decay = -jnp.exp(w)
    a = jax.nn.sigmoid(a0 + _bdot(al, a2))
    g = _bdot(jax.nn.sigmoid(gl), g2)
    kk = k * k_k
    norm = jnp.sqrt(_fdot(kk * kk, _head_ones()))
    kk = kk / jnp.maximum(norm, L2_EPS)
    kp = k * (1.0 + (a - 1.0) * k_a)
    return r, log_decay, kp, v, -kk, kk * a, g


def _f_rw_post(y, r, kp, v, g, ln_w, ln_b, r_k):
    ones = _head_ones()
    yc = y - _fdot(y, ones) * (1.0 / HEAD)
    var = _fdot(yc * yc, ones) * (1.0 / HEAD)
    yn = yc * lax.rsqrt(var + GN_EPS) * ln_w + ln_b
    bonus = _fdot(r * kp * r_k, ones) * v
    return ((yn + bonus) * g,)


def _f_s5_lam(lam_re, lam_im, log_step):
    step = jnp.exp(log_step)
    dr, di = lam_re * step, lam_im * step
    e = jnp.exp(dr)
    lbr, lbi = e * jnp.cos(di), e * jnp.sin(di)
    nr, ni = lbr - 1.0, lbi
    den = lam_re * lam_re + lam_im * lam_im
    return lbr, lbi, (nr * lam_re + ni * lam_im) / den, (ni * lam_re - nr * lam_im) / den


def _f_s5_build(coef_r, coef_i, btr, bti, ctr, cti):
    bbr = coef_r * btr - coef_i * bti
    bbi = coef_r * bti + coef_i * btr
    rows = lax.broadcasted_iota(jnp.int32, (S5_WIDTH, S5_LANES), 0) // S5_GROUP
    cols = lax.broadcasted_iota(jnp.int32, (S5_WIDTH, S5_LANES), 1) // S5_STATE
    mask = (rows == cols).astype(F32)
    rep = lambda m: jnp.concatenate([m] * S5_GROUPS, axis=0) * mask
    return (jnp.concatenate([rep(bbr), rep(bbi)], axis=1), jnp.concatenate([rep(ctr), -rep(cti)], axis=1))


HALO = 8


def _rw_pre_specs(z, params, tm, order):
    halo_blocks = tm // HALO
    return ([pl.BlockSpec((tm, z.shape[1]), lambda i: (order(i), 0)),
             pl.BlockSpec((HALO, z.shape[1]), lambda i: (jnp.maximum(order(i) * halo_blocks - 1, 0), 0))]
            + [_full_spec(p) for p in params])


def _rw_pre_fwd(z, params, tm):
    t = z.shape[0]
    npar = len(params)

    def body(z_ref, halo_ref, *refs):
        carry = jnp.where(pl.program_id(0) == 0, 0.0, halo_ref[pl.ds(HALO - 1, 1), :])
        res = _f_rw_pre(z_ref[...], carry, *[r[...].astype(F32) for r in refs[:npar]])
        for r, v in zip(refs[npar:], res):
            r[...] = v

    return pl.pallas_call(
        body, name="rw_pre",
        out_shape=[jax.ShapeDtypeStruct((t, RWKV_WIDTH), F32)] * 7,
        grid=(t // tm,),
        in_specs=_rw_pre_specs(z, params, tm, lambda i: i),
        out_specs=[pl.BlockSpec((tm, RWKV_WIDTH), lambda i: (i, 0))] * 7,
        compiler_params=_params(("parallel",)),
    )(z, z, *params)


def _rw_pre_bwd(z, params, cots, tm):
    t = z.shape[0]
    nt = t // tm
    npar = len(params)
    order = lambda i: nt - 1 - i
    flat_cots = [a for group in cots for a in group]
    ncot = len(flat_cots)

    def body(z_ref, halo_ref, *refs):
        pin, cin = refs[:npar], list(refs[npar:npar + ncot])
        dz_ref = refs[npar + ncot]
        dpo = refs[npar + ncot + 1:npar + ncot + 1 + npar]
        dcarry_ref = refs[npar + ncot + 1 + npar]
        i = pl.program_id(0)

        @pl.when(i == 0)
        def _():
            dcarry_ref[...] = jnp.zeros_like(dcarry_ref)
            for r in dpo:
                r[...] = jnp.zeros_like(r)

        carry = jnp.where(i == nt - 1, 0.0, halo_ref[pl.ds(HALO - 1, 1), :])
        _, vjp = jax.vjp(_f_rw_pre, z_ref[...], carry, *[r[...].astype(F32) for r in pin])
        g = vjp(tuple(sum(cin.pop(0)[...] for _ in group) for group in cots))
        last_row = lax.broadcasted_iota(jnp.int32, z_ref.shape, 0) == tm - 1
        dz_ref[...] = g[0] + jnp.where(last_row, dcarry_ref[...], 0.0)
        dcarry_ref[...] = g[1]
        for r, v in zip(dpo, g[2:]):
            r[...] += v

    tok = lambda w: pl.BlockSpec((tm, w), lambda i: (order(i), 0))
    return pl.pallas_call(
        body, name="rw_pre_bwd",
        out_shape=[jax.ShapeDtypeStruct(z.shape, F32)] + [jax.ShapeDtypeStruct(p.shape, F32) for p in params],
        grid=(nt,),
        in_specs=_rw_pre_specs(z, params, tm, order) + [tok(RWKV_WIDTH)] * ncot,
        out_specs=[tok(z.shape[1])] + [_full_spec(p) for p in params],
        scratch_shapes=[pltpu.VMEM((1, z.shape[1]), F32)],
        compiler_params=_params(("arbitrary",)),
    )(z, z, *params, *flat_cots)


def _s5_scan(bu, lam, tm):
    t, w = bu.shape
    h = w // 2

    def body(bu_ref, lam_ref, x_ref, xp_ref, carry_ref):
        @pl.when(pl.program_id(0) == 0)
        def _():
            carry_ref[...] = jnp.zeros_like(carry_ref)

        lr, li = lam_ref[:, :h], lam_ref[:, h:]

        def step(s, c):
            cr, ci = c
            row = pl.ds(s, 1)
            xp_ref[row, :h] = cr
            xp_ref[row, h:] = ci
            nr = lr * cr - li * ci + bu_ref[row, :h]
            ni = lr * ci + li * cr + bu_ref[row, h:]
            x_ref[row, :h] = nr
            x_ref[row, h:] = ni
            return nr, ni

        cr, ci = lax.fori_loop(0, tm, step, (carry_ref[:, :h], carry_ref[:, h:]))
        carry_ref[:, :h] = cr
        carry_ref[:, h:] = ci

    spec = pl.BlockSpec((tm, w), lambda i: (i, 0))
    return pl.pallas_call(
        body, name="s5_scan",
        out_shape=[jax.ShapeDtypeStruct((t, w), F32)] * 2,
        grid=(t // tm,),
        in_specs=[spec, pl.BlockSpec((1, w), lambda i: (0, 0))],
        out_specs=[spec, spec],
        scratch_shapes=[pltpu.VMEM((1, w), F32)],
        compiler_params=_params(("arbitrary",)),
    )(bu, lam)


def _s5_scan_bwd(dx, xp, lam, tm):
    t, w = dx.shape
    h = w // 2
    nt = t // tm

    def body(dx_ref, xp_ref, lam_ref, dbu_ref, dlam_ref, carry_ref):
        @pl.when(pl.program_id(0) == 0)
        def _():
            carry_ref[...] = jnp.zeros_like(carry_ref)
            dlam_ref[...] = jnp.zeros_like(dlam_ref)

        lr, li = lam_ref[:, :h], lam_ref[:, h:]

        def step(s, c):
            cr, ci = c
            row = pl.ds(tm - 1 - s, 1)
            nr = lr * cr + li * ci + dx_ref[row, :h]
            ni = lr * ci - li * cr + dx_ref[row, h:]
            dbu_ref[row, :h] = nr
            dbu_ref[row, h:] = ni
            return nr, ni

        cr, ci = lax.fori_loop(0, tm, step, (carry_ref[:, :h], carry_ref[:, h:]))
        carry_ref[:, :h] = cr
        carry_ref[:, h:] = ci
        gr, gi = dbu_ref[:, :h], dbu_ref[:, h:]
        pr, pi_ = xp_ref[:, :h], xp_ref[:, h:]
        dlam_ref[:, :h] += jnp.sum(gr * pr + gi * pi_, axis=0, keepdims=True)
        dlam_ref[:, h:] += jnp.sum(gi * pr - gr * pi_, axis=0, keepdims=True)

    spec = pl.BlockSpec((tm, w), lambda i: (nt - 1 - i, 0))
    row_spec = pl.BlockSpec((1, w), lambda i: (0, 0))
    return pl.pallas_call(
        body, name="s5_scan_bwd",
        out_shape=[jax.ShapeDtypeStruct((t, w), F32), jax.ShapeDtypeStruct((1, w), F32)],
        grid=(nt,),
        in_specs=[spec, spec, row_spec],
        out_specs=[spec, row_spec],
        scratch_shapes=[pltpu.VMEM((1, w), F32)],
        compiler_params=_params(("arbitrary",)),
    )(dx, xp, lam)


def _wkv_chunks(s0, r, lw, k, v, a, b):
    c = r[0].shape[0]
    row = lax.broadcasted_iota(jnp.int32, (c, c), 0)
    col = lax.broadcasted_iota(jnp.int32, (c, c), 1)
    incl, strict = col <= row, col < row
    tri = incl.astype(F32)
    eye = (row == col).astype(F32)
    each = lambda f, *xs: [f(*t) for t in zip(*xs)]
    stack = lambda p, q: jnp.concatenate([p, q], axis=0)
    dot = lambda p, q, dims=_NN: _dot(p, q, dims, WKV_PASSES)
    lc = each(lambda l: _dot(tri, l, _NN, 2), lw)
    e_neg = each(lambda l: jnp.exp(-l), lc)
    ar = each(lambda x, z, l, w: stack(x * jnp.exp(l - w), z * jnp.exp(l)), a, r, lc, lw)
    bk = each(lambda x, z, e: stack(x * e, z * e), b, k, e_neg)
    m = each(lambda p, q: dot(p, q, _NT), ar, bk)
    mab = each(lambda q: jnp.where(strict, q[:c, :c], 0.0), m)
    mak_mrk = each(lambda q: stack(jnp.where(strict, q[:c, c:], 0.0), jnp.where(incl, q[c:, c:], 0.0)), m)
    mrb = each(lambda q: jnp.where(incl, q[c:, :c], 0.0), m)
    xy = each(lambda p, s, q, z: dot(p, s, _NT) + dot(q, z), ar, s0, mak_mrk, v)
    inv = each(lambda q: eye + q, mab)
    pw = each(lambda q: dot(q, q), mab)
    for _ in range(int(math.log2(c)) - 2):
        both = each(lambda i, q: dot(stack(i, q), q), inv, pw)
        inv = each(lambda i, q: i + q[:c], inv, both)
        pw = each(lambda q: q[c:], both)
    inv = each(lambda i, q: i + dot(i, q), inv, pw)
    u = each(lambda i, q: dot(i, q[:c]), inv, xy)
    y = each(lambda q, z, p: q[c:] + dot(z, p), xy, mrb, u)
    e_tot = each(lambda l: jnp.exp(jnp.sum(l, axis=0, keepdims=True)), lw)
    s1 = each(lambda s, p, z, q, e: (s + dot(stack(p, z), q, _TN)) * e, s0, u, v, bk, e_tot)
    return y, s1


def _carry(plan, n_args, n_outs):
    n_in, n_out = len(plan.ins), len(plan.out_shape)

    def parts(refs):
        base = n_args + n_in + n_outs
        return refs[n_args:n_args + n_in], refs[base:base + n_out], refs[base + n_out + 1:]

    return parts, [HBM_SPEC] * n_in, list(plan.out_shape), [HBM_SPEC] * n_out, list(plan.sems)


def _head_cols(ref):
    return tuple(ref[:, h * HEAD:(h + 1) * HEAD] for h in range(HEADS))


def _wkv_fwd(seqs, plan):
    t, w = seqs[0].shape
    c, n = WKV_CHUNK, HEAD
    nc = t // c
    parts, plan_in_specs, plan_out_shape, plan_out_specs, plan_sems = _carry(plan, 6, 2)

    def body(*refs):
        ins, (y_ref, ck_ref) = refs[:6], refs[6 + len(plan.ins):8 + len(plan.ins)]
        s_ref = refs[8 + len(plan.ins) + len(plan.out_shape)]

        @pl.when(pl.program_id(0) == 0)
        def _():
            s_ref[...] = jnp.zeros_like(s_ref)
            plan.start(*parts(refs))

        s0 = tuple(s_ref[h] for h in range(HEADS))
        ys, s1 = _wkv_chunks(s0, *[_head_cols(r) for r in ins])
        for h in range(HEADS):
            ck_ref[0, h] = s0[h]
            y_ref[:, h * n:(h + 1) * n] = ys[h]
            s_ref[h] = s1[h]

        @pl.when(pl.program_id(0) == nc - 1)
        def _():
            plan.wait(*parts(refs))

    spec = pl.BlockSpec((c, w), lambda i: (i, 0))
    res = pl.pallas_call(
        body, name="wkv_fwd",
        out_shape=[jax.ShapeDtypeStruct((t, w), F32), jax.ShapeDtypeStruct((nc, HEADS, n, n), F32)] + plan_out_shape,
        grid=(nc,),
        in_specs=[spec] * 6 + plan_in_specs,
        out_specs=[spec, pl.BlockSpec((1, HEADS, n, n), lambda i: (i, 0, 0, 0))] + plan_out_specs,
        scratch_shapes=[pltpu.VMEM((HEADS, n, n), F32)] + plan_sems,
        compiler_params=_params(("arbitrary",)),
    )(*seqs, *plan.ins)
    return res[0], res[1], res[2:]


def _wkv_bwd(seqs, ck, dy, plan):
    t, w = seqs[0].shape
    c, n = WKV_CHUNK, HEAD
    nc = t // c
    parts, plan_in_specs, plan_out_shape, plan_out_specs, plan_sems = _carry(plan, 8, 6)

    def body(*refs):
        ins, ck_ref, dy_ref = refs[:6], refs[6], refs[7]
        outs = refs[8 + len(plan.ins):14 + len(plan.ins)]
        ds_ref = refs[14 + len(plan.ins) + len(plan.out_shape)]

        @pl.when(pl.program_id(0) == 0)
        def _():
            ds_ref[...] = jnp.zeros_like(ds_ref)
            plan.start(*parts(refs))

        s0 = tuple(ck_ref[0, h] for h in range(HEADS))
        _, vjp = jax.vjp(_wkv_chunks, s0, *[_head_cols(r) for r in ins])
        g = vjp((list(_head_cols(dy_ref)), [ds_ref[h] for h in range(HEADS)]))
        for h in range(HEADS):
            ds_ref[h] = g[0][h]
            for o, d in zip(outs, g[1:]):
                o[:, h * n:(h + 1) * n] = d[h]

        @pl.when(pl.program_id(0) == nc - 1)
        def _():
            plan.wait(*parts(refs))

    spec = pl.BlockSpec((c, w), lambda i: (nc - 1 - i, 0))
    res = pl.pallas_call(
        body, name="wkv_bwd",
        out_shape=[jax.ShapeDtypeStruct((t, w), F32)] * 6 + plan_out_shape,
        grid=(nc,),
        in_specs=[spec] * 6 + [pl.BlockSpec((1, HEADS, n, n), lambda i: (nc - 1 - i, 0, 0, 0)), spec] + plan_in_specs,
        out_specs=[spec] * 6 + plan_out_specs,
        scratch_shapes=[pltpu.VMEM((HEADS, n, n), F32)] + plan_sems,
        compiler_params=_params(("arbitrary",)),
    )(*seqs, ck, dy, *plan.ins)
    return res[:6], res[6:]


def _coords():
    return lax.axis_index("x"), lax.axis_index("y"), lax.axis_index("c")


def _flip(v, f):
    return 1 - v if f else v


_CHIP_FLIPS = [(1, 0), (0, 1), (1, 1)]
_DEV_FLIPS = [(fx, fy, fc) for fx in (0, 1) for fy in (0, 1) for fc in (0, 1) if (fx, fy, fc) != (0, 0, 0)]
HBM_SPEC = pl.BlockSpec(memory_space=pl.ANY)


def _chip_peer(k, x, y):
    fx, fy = _CHIP_FLIPS[k]
    return _flip(x, fx), _flip(y, fy)


def _dev_peer(k, x, y, c):
    fx, fy, fc = _DEV_FLIPS[k]
    return _flip(x, fx), _flip(y, fy), _flip(c, fc)


def _rows_of_core(ref, core):
    h = ref.shape[-2] // 2
    rows = pl.ds(pl.multiple_of(core * h, 8), h)
    return ref.at[rows, :] if len(ref.shape) == 2 else ref.at[:, rows, :]


class _Plan(NamedTuple):
    ins: Sequence[Any]
    out_shape: Sequence[Any]
    sems: Sequence[Any]
    start: Callable
    wait: Callable


def _run_plan(name, plan):
    n_in, n_out = len(plan.ins), len(plan.out_shape)

    def body(*refs):
        parts = refs[:n_in], refs[n_in:n_in + n_out], refs[n_in + n_out:]
        plan.start(*parts)
        plan.wait(*parts)

    return pl.pallas_call(
        body, name=name, out_shape=list(plan.out_shape),
        in_specs=[HBM_SPEC] * n_in, out_specs=[HBM_SPEC] * n_out, scratch_shapes=list(plan.sems),
    )(*plan.ins)


def _gather_plan(shards):
    n = len(shards)

    def copies(srcs, outs, sems):
        send_sems, recv_sems, local_sems = sems
        x, y, c = _coords()
        me = 2 * x + y

        def remote(i, k, arriving):
            px, py = _chip_peer(k, x, y)
            return pltpu.make_async_remote_copy(
                src_ref=srcs[i], dst_ref=outs[i].at[2 * px + py if arriving else me],
                send_sem=send_sems.at[i, k], recv_sem=recv_sems.at[i, k],
                device_id=(px, py, c), device_id_type=MESH)

        own = [pltpu.make_async_copy(srcs[i], outs[i].at[me], local_sems.at[i]) for i in range(n)]
        pairs = [(i, k) for k in range(3) for i in range(n)]
        return own, [remote(i, k, False) for i, k in pairs], [remote(i, k, True) for i, k in pairs]

    return _Plan(
        ins=shards, out_shape=[jax.ShapeDtypeStruct((4,) + s.shape, s.dtype) for s in shards],
        sems=[pltpu.SemaphoreType.DMA((n, 3)), pltpu.SemaphoreType.DMA((n, 3)), pltpu.SemaphoreType.DMA((n,))],
        start=functools.partial(_start_copies, copies), wait=functools.partial(_wait_copies, copies))


def _start_copies(copies, ins, outs, sems):
    own, sends, _ = copies(ins, outs, sems)
    for cp in own + sends:
        cp.start()


def _wait_copies(copies, ins, outs, sems):
    own, sends, arrivals = copies(ins, outs, sems)
    for cp in arrivals:
        cp.wait_recv()
    for cp in sends:
        cp.wait_send()
    for cp in own:
        cp.wait()


def _exchange_plan(gs, small=None):
    n = len(gs)
    arrays = list(gs) + ([] if small is None else [small])

    def copies(srcs, outs, sems):
        send_sems, recv_sems, local_sems = sems
        x, y, c = _coords()
        me = 4 * x + 2 * y + c

        def piece(i, px, py, pc):
            if i == n:
                return srcs[i].at[4 * px + 2 * py + pc]
            return _rows_of_core(srcs[i].at[2 * px + py], pc)

        def remote(i, k, arriving):
            px, py, pc = _dev_peer(k, x, y, c)
            return pltpu.make_async_remote_copy(
                src_ref=piece(i, px, py, pc), dst_ref=outs[i].at[4 * px + 2 * py + pc if arriving else me],
                send_sem=send_sems.at[i, k], recv_sem=recv_sems.at[i, k],
                device_id=(px, py, pc), device_id_type=MESH)

        own = [pltpu.make_async_copy(piece(i, x, y, c), outs[i].at[me], local_sems.at[i]) for i in range(len(arrays))]
        pairs = [(i, k) for k in range(7) for i in range(len(arrays))]
        return own, [remote(i, k, False) for i, k in pairs], [remote(i, k, True) for i, k in pairs]

    out_shape = [jax.ShapeDtypeStruct((8, g.shape[1] // 2, g.shape[2]), g.dtype) for g in gs]
    if small is not None:
        out_shape.append(jax.ShapeDtypeStruct(small.shape, small.dtype))
    m = len(arrays)
    return _Plan(
        ins=arrays, out_shape=out_shape,
        sems=[pltpu.SemaphoreType.DMA((m, 7)), pltpu.SemaphoreType.DMA((m, 7)), pltpu.SemaphoreType.DMA((m,))],
        start=functools.partial(_start_copies, copies), wait=functools.partial(_wait_copies, copies))


def _share_cores(halves, small):
    n = len(halves)

    def body(*refs):
        srcs, small_src, outs, small_out = refs[:n], refs[n], refs[n + 1:2 * n + 1], refs[2 * n + 1]
        mine, theirs = refs[2 * n + 2:3 * n + 2], refs[3 * n + 2:4 * n + 2]
        send_sems, recv_sems, ssend, srecv, local_sems = refs[4 * n + 2:]
        x, y, c = _coords()
        me = 4 * x + 2 * y + c

        def big(i):
            return pltpu.make_async_remote_copy(
                src_ref=mine[i], dst_ref=theirs[i], send_sem=send_sems.at[i], recv_sem=recv_sems.at[i],
                device_id=(x, y, 1 - c), device_id_type=MESH)

        def tiny(k, arriving):
            px, py, pc = _dev_peer(k, x, y, c)
            return pltpu.make_async_remote_copy(
                src_ref=small_src, dst_ref=small_out.at[4 * px + 2 * py + pc if arriving else me],
                send_sem=ssend.at[k], recv_sem=srecv.at[k], device_id=(px, py, pc), device_id_type=MESH)

        small_sends = [tiny(k, False) for k in range(7)]
        own_small = pltpu.make_async_copy(small_src, small_out.at[me], local_sems.at[2 * n])
        stage = [pltpu.make_async_copy(srcs[i], mine[i], local_sems.at[i]) for i in range(n)]
        for cp in small_sends + [own_small] + stage:
            cp.start()
        sends = []
        for i in range(n):
            stage[i].wait()
            sends.append(big(i))
            sends[-1].start()
        store = [pltpu.make_async_copy(mine[i], outs[i].at[c], local_sems.at[i]) for i in range(n)]
        for cp in store:
            cp.start()
        for i in range(n):
            big(i).wait_recv()
            store.append(pltpu.make_async_copy(theirs[i], outs[i].at[1 - c], local_sems.at[n + i]))
            store[-1].start()
        for k in range(7):
            tiny(k, True).wait_recv()
        for cp in sends + small_sends:
            cp.wait_send()
        for cp in store + [own_small]:
            cp.wait()

    staged = [pltpu.VMEM(s.shape, s.dtype) for s in halves]
    res = pl.pallas_call(
        body, name="share_cores",
        out_shape=[jax.ShapeDtypeStruct((2,) + s.shape, s.dtype) for s in halves]
        + [jax.ShapeDtypeStruct((8,) + small.shape, small.dtype)],
        in_specs=[HBM_SPEC] * (n + 1), out_specs=[HBM_SPEC] * (n + 1),
        scratch_shapes=staged + staged + [
            pltpu.SemaphoreType.DMA((n,)), pltpu.SemaphoreType.DMA((n,)),
            pltpu.SemaphoreType.DMA((7,)), pltpu.SemaphoreType.DMA((7,)),
            pltpu.SemaphoreType.DMA((2 * n + 1,))],
        compiler_params=pltpu.CompilerParams(vmem_limit_bytes=VMEM_LIMIT_BYTES),
    )(*halves, small)
    return res[:n], res[n]


BF16_SUBLANES = 16


def _row_tile(n, target, step=BF16_SUBLANES):
    return max([d for d in range(step, min(n, target) + 1, step) if n % d == 0] or [n])


def _ew(name, fn, ins, outs, block_bytes=1 << 20):
    rows, cols = ins[0].shape[-2:]
    lead = max(math.prod(a.shape[:-2]) for a in ins)
    tr = _row_tile(rows, max(8, block_bytes // (4 * cols * lead)))
    n = len(ins)

    def spec(shape):
        if len(shape) == 2:
            return pl.BlockSpec((tr, cols), lambda i: (i, 0))
        return pl.BlockSpec((shape[0], tr, cols), lambda i: (0, i, 0))

    def body(*refs):
        res = fn(*[r[...] for r in refs[:n]])
        for r, v in zip(refs[n:], res):
            r[...] = v

    return pl.pallas_call(
        body, name=name,
        out_shape=[jax.ShapeDtypeStruct(s, F32) for s in outs],
        grid=(rows // tr,),
        in_specs=[spec(a.shape) for a in ins],
        out_specs=[spec(s) for s in outs],
        compiler_params=_params(("parallel",)),
    )(*ins)


def _sum_slots(a):
    total = a[0].astype(F32)
    for s in range(1, a.shape[0]):
        total = total + a[s].astype(F32)
    return (total,)


def _adamw(g, w, m, v):
    bc1 = 1.0 - ADAM_B1 ** ADAM_STEP
    bc2 = 1.0 - ADAM_B2 ** ADAM_STEP
    m_new = ADAM_B1 * m + (1.0 - ADAM_B1) * g
    v_new = ADAM_B2 * v + (1.0 - ADAM_B2) * jnp.square(g)
    delta = -ADAM_LR * ((m_new / bc1) / (jnp.sqrt(v_new / bc2) + ADAM_EPS) + ADAM_WD * w)
    return delta, m_new, v_new


def _mat(a):
    return a.reshape(a.shape[-2:])


def _to_shard_major(full, axis):
    rows, cols = full.shape
    if axis == 0:
        return full.reshape(4, rows // 4, cols)
    return full.reshape(rows, 4, cols // 4).transpose(1, 0, 2)


def _from_shard_major(a, axis):
    _, r, cs = a.shape
    if axis == 0:
        return a.reshape(4 * r, cs)
    return a.transpose(1, 0, 2).reshape(r, 4 * cs)


def _pack_small(arrays):
    flat = jnp.concatenate([arrays[n].reshape(-1) for n in SMALL_NAMES])
    flat = jnp.pad(flat, (0, 8 * SMALL_ROWS * PACK_COLS - flat.shape[0]))
    return flat.reshape(8, SMALL_ROWS, PACK_COLS)


def _unpack_small(packed, shapes):
    flat = packed.reshape(-1)
    out, off = {}, 0
    for n in SMALL_NAMES:
        size = math.prod(shapes[n])
        out[n] = flat[off:off + size].reshape(shapes[n])
        off += size
    return out


def _row(a):
    return a.reshape(1, -1)


def _local_step(x, p, target, wf, ws, late_gather):
    wf = dict(wf)
    t = x.shape[0]
    tm = min(256, t)
    g = {}

    lam_re, lam_im = ws['s5_lam_re'].reshape(S5_GROUPS, S5_STATE), ws['s5_lam_im'].reshape(S5_GROUPS, S5_STATE)
    log_step = ws['s5_log_step'].reshape(S5_GROUPS, 1)
    gp = (S5_GROUPS, S5_STATE)
    lam_ins = (lam_re, lam_im, log_step)
    lbr, lbi, cfr, cfi = _small_fwd("s5_lam", _f_s5_lam, lam_ins, [(gp, F32)] * 4)
    lam_row = jnp.concatenate([_row(lbr), _row(lbi)], axis=1)
    to_t = lambda a, perm: a.reshape((S5_GROUPS,) + a.shape[-2:]).transpose(perm).reshape(S5_GROUP, S5_LANES)
    build_ins = (_row(cfr), _row(cfi), to_t(ws['s5_b_re'], (2, 0, 1)), to_t(ws['s5_b_im'], (2, 0, 1)),
                 to_t(ws['s5_c_re'], (1, 0, 2)), to_t(ws['s5_c_im'], (1, 0, 2)))
    stack_shape = (S5_WIDTH, 2 * S5_LANES)
    bstack, cstack_t = _small_fwd("s5_build", _f_s5_build, build_ins, [(stack_shape, F32)] * 2)

    norm_mix, norm_ffn, norm_ple = _row(ws['norm_mix']), _row(ws['norm_ffn']), _row(ws['norm_ple'])
    final_norm = _row(ws['final_norm'])
    (xn,) = _tok_fwd("norm_in", _f_norm_in, [x], [norm_mix], [(x.shape[1], BF16)], tm)
    u = _mm("proj_s5", xn, wf['w_in'][:, :S5_WIDTH], 'nn')
    z = _mm("proj_rw", xn, wf['w_in'][:, S5_WIDTH:], 'nn')

    bu = _mm("s5_bu", u, bstack, 'nn')
    xs, xs_prev = _s5_scan(bu, lam_row, tm)
    ypre = _mm("s5_y", xs, cstack_t, 'nt')
    s5_par = [_row(ws['s5_d']), wf['s5_glu_w'], _row(ws['s5_glu_b'])]
    (s5_out,) = _tok_fwd("s5_post", _f_s5_post, [ypre, u], s5_par, [(S5_WIDTH, BF16)], tm)

    pre_par = [_row(ws['rw_shift_mu']), _row(ws['rw_w0']), wf['rw_w2'], _row(ws['rw_a0']), wf['rw_a2'],
               wf['rw_g2'], _row(ws['rw_k_k']), _row(ws['rw_k_a'])]
    r, lw, kp, v, an, bn, gate = _rw_pre_fwd(z, pre_par, tm)
    seqs = [r, lw, kp, v, an, bn]
    y_wkv, ck, late = _wkv_fwd(seqs, late_gather)
    wf.update({n: _from_shard_major(a, SHARDED[n]) for n, a in zip(LATE_NAMES, late)})
    post_par = [_row(ws['rw_ln_w']), _row(ws['rw_ln_b']), _row(ws['rw_r_k'])]
    post_toks = [y_wkv, r, kp, v, gate]
    (rw_out,) = _tok_fwd("rw_post", _f_rw_post, post_toks, post_par, [(RWKV_WIDTH, BF16)], tm)

    mixcat = jnp.concatenate([s5_out, rw_out], axis=1)
    mixed = _mm("mix_out", mixcat, wf['w_out'], 'nn')
    h1, hn = _tok_fwd("mix_res", _f_mix_res, [x, mixed], [norm_ffn], [(x.shape[1], F32), (x.shape[1], BF16)], tm)
    w13 = jnp.concatenate([wf['ffn_w1'], wf['ffn_w3']], axis=1)
    a13 = _mm("ffn_up", hn, w13, 'nn')
    (f,) = _tok_fwd("ffn_act", _f_ffn_act, [a13], [], [(FFN_HIDDEN, BF16)], tm)
    ffo = _mm("ffn_down", f, wf['ffn_w2'], 'nn')
    h2, hp = _tok_fwd("ffn_res", _f_ffn_res, [h1, ffo], [norm_ple], [(x.shape[1], F32), (x.shape[1], BF16)], tm)
    gpre = _mm("ple_gate", hp, wf['ple_gate_w'], 'nn')
    pu = _mm("ple_up", p, wf['ple_up_w'], 'nn')

    dh2, dgpre, dpu, g['final_norm'], loss = _tok_bwd(
        "loss", _f_loss, [h2, gpre, pu, target], [final_norm], [None],
        [F32, BF16, BF16, None], [True], tm, acc_out=0)
    g['ple_gate_w'] = _mm("d_ple_gate_w", hp, dgpre, 'tn', out_dtype=WIRE)
    g['ple_up_w'] = _mm("d_ple_up_w", p, dpu, 'tn', out_dtype=WIRE)
    dhp = _mm("d_hp", dgpre, wf['ple_gate_w'], 'nt')
    dh1, dffo, g['norm_ple'] = _tok_bwd("ffn_res_bwd", _f_ffn_res, [h1, ffo], [norm_ple], [dh2, dhp],
                                        [F32, BF16], [True], tm)
    g['ffn_w2'] = _mm("d_ffn_w2", f, dffo, 'tn', out_dtype=WIRE)
    df = _mm("d_f", dffo, wf['ffn_w2'], 'nt')
    (da13,) = _tok_bwd("ffn_act_bwd", _f_ffn_act, [a13], [], [df], [BF16], [], tm)
    dw13 = _mm("d_ffn_w13", hn, da13, 'tn', out_dtype=WIRE)
    g['ffn_w1'], g['ffn_w3'] = dw13[:, :FFN_HIDDEN], dw13[:, FFN_HIDDEN:]
    dhn = _mm("d_hn", da13, w13, 'nt')
    dx_a, dmixed, g['norm_ffn'] = _tok_bwd("mix_res_bwd", _f_mix_res, [x, mixed], [norm_ffn], [dh1, dhn],
                                           [F32, BF16], [True], tm)
    g['w_out'] = _mm("d_w_out", mixcat, dmixed, 'tn', out_dtype=WIRE)
    dmixcat = _mm("d_mixcat", dmixed, wf['w_out'], 'nt')
    ds5_out, drw_out = dmixcat[:, :S5_WIDTH], dmixcat[:, S5_WIDTH:]

    dy_wkv, dr_b, dkp_b, dv_b, dgate, g['rw_ln_w'], g['rw_ln_b'], g['rw_r_k'] = _tok_bwd(
        "rw_post_bwd", _f_rw_post, post_toks, post_par, [drw_out], [F32] * 5, [True] * 3, tm)
    late_exchange = _exchange_plan([_to_shard_major(g[n], SHARDED[n]).astype(WIRE) for n in LATE_NAMES])
    dseqs, late_pieces = _wkv_bwd(seqs, ck, dy_wkv, late_exchange)
    pre_cots = [(dseqs[0], dr_b), (dseqs[1],), (dseqs[2], dkp_b), (dseqs[3], dv_b), (dseqs[4],), (dseqs[5],),
                (dgate,)]
    dz, *dpre = _rw_pre_bwd(z, pre_par, pre_cots, tm)
    for n, d in zip(['rw_shift_mu', 'rw_w0', 'rw_w2', 'rw_a0', 'rw_a2', 'rw_g2', 'rw_k_k', 'rw_k_a'], dpre):
        g[n] = d

    dypre, du_a, g['s5_d'], g['s5_glu_w'], g['s5_glu_b'] = _tok_bwd(
        "s5_post_bwd", _f_s5_post, [ypre, u], s5_par, [ds5_out], [F32, F32], [True] * 3, tm)
    dxs = _mm("d_s5_x", dypre, cstack_t, 'nn')
    dcstack_t = _mm("d_s5_c", dypre, xs, 'tn')
    dbu, dlam_row = _s5_scan_bwd(dxs, xs_prev, lam_row, tm)
    du_b = _mm("d_s5_u", dbu, bstack, 'nt')
    dbstack = _mm("d_s5_b", u, dbu, 'tn')
    dbuild = _small_bwd("s5_build_bwd", _f_s5_build, build_ins, (dbstack, dcstack_t))
    lam_cots = (dlam_row[:, :S5_LANES].reshape(gp), dlam_row[:, S5_LANES:].reshape(gp),
                dbuild[0].reshape(gp), dbuild[1].reshape(gp))
    g['s5_lam_re'], g['s5_lam_im'], g['s5_log_step'] = _small_bwd("s5_lam_bwd", _f_s5_lam, lam_ins, lam_cots)
    from_t = lambda a, perm: a.reshape(S5_GROUP, S5_GROUPS, S5_STATE).transpose(perm)
    g['s5_b_re'], g['s5_b_im'] = from_t(dbuild[2], (1, 2, 0)), from_t(dbuild[3], (1, 2, 0))
    g['s5_c_re'], g['s5_c_im'] = from_t(dbuild[4], (1, 0, 2)), from_t(dbuild[5], (1, 0, 2))

    dproj = jnp.concatenate([(du_a + du_b).astype(BF16), dz.astype(BF16)], axis=1)
    g['w_in'] = _mm("d_w_in", xn, dproj, 'tn', out_dtype=WIRE)
    dxn = _mm("d_xn", dproj, wf['w_in'], 'nt')
    grad_x, g['norm_mix'] = _tok_bwd("norm_in_bwd", _f_norm_in, [x], [norm_mix], [dxn], [F32], [True], tm,
                                     add_to=(0, dx_a))
    return loss[0, 0], grad_x, g, late_pieces


def _step(x, p, target, w, m, v):
    shards = {n: _mat(w[n]).astype(BF16) for n in SHARDED_NAMES}
    early = _run_plan("gather_early", _gather_plan([shards[n] for n in EARLY_NAMES]))
    wf = {n: _from_shard_major(a, SHARDED[n]) for n, a in zip(EARLY_NAMES, early)}
    ws = {n: w[n] for n in SMALL_NAMES}

    late_gather = _gather_plan([shards[n] for n in LATE_NAMES])
    loss, grad_x, g, late_pieces = _local_step(x[0], p[0, 0], target[0], wf, ws, late_gather)

    early_plan = _exchange_plan([_to_shard_major(g[n], SHARDED[n]).astype(WIRE) for n in EARLY_NAMES],
                                _pack_small({n: g[n] for n in SMALL_NAMES}))
    *early_pieces, by_dev = _run_plan("exchange_early", early_plan)
    pieces = dict(zip(LATE_NAMES + EARLY_NAMES, list(late_pieces) + early_pieces))
    halves = [_ew("add_devices_" + n, _sum_slots, [pieces[n]], [pieces[n].shape[1:]])[0] for n in SHARDED_NAMES]
    (small_piece,) = _ew("add_devices_small", _sum_slots, [by_dev], [by_dev.shape[1:]])
    both, small_g = _share_cores(halves, small_piece)

    kinds = [{}, {}, {}, {}]
    for n, gn in zip(SHARDED_NAMES, both):
        shard = _mat(w[n]).shape
        res = _ew("adamw_" + n, _adamw, [gn.reshape(shard), _mat(w[n]), _mat(m[n]), _mat(v[n])], [shard] * 3)
        for kind, a in zip(kinds, [gn] + list(res)):
            kind[n] = a.reshape(w[n].shape)
    flat = (8 * SMALL_ROWS, PACK_COLS)
    packed = [_pack_small({n: d[n] for n in SMALL_NAMES}).reshape(flat) for d in (w, m, v)]
    small_res = _ew("adamw_small", _adamw, [small_g.reshape(flat)] + packed, [flat] * 3)
    small_shapes = {n: w[n].shape for n in SMALL_NAMES}
    for kind, a in zip(kinds, [small_g] + list(small_res)):
        kind.update(_unpack_small(a, small_shapes))
    total = lax.psum(loss, ("x", "y", "c"))
    return (total, grad_x[None], *[kind[n] for kind in kinds for n in WEIGHT_NAMES])


def kernel(x, p, norm_mix, w_in, s5_lam_re, s5_lam_im, s5_log_step, s5_b_re, s5_b_im, s5_c_re, s5_c_im, s5_d, s5_glu_w, s5_glu_b, rw_shift_mu, rw_w0, rw_w2, rw_a0, rw_a2, rw_g2, rw_k_k, rw_k_a, rw_r_k, rw_ln_w, rw_ln_b, w_out, norm_ffn, ffn_w1, ffn_w3, ffn_w2, norm_ple, ple_gate_w, ple_up_w, final_norm, loss_target, m_norm_mix, m_w_in, m_s5_lam_re, m_s5_lam_im, m_s5_log_step, m_s5_b_re, m_s5_b_im, m_s5_c_re, m_s5_c_im, m_s5_d, m_s5_glu_w, m_s5_glu_b, m_rw_shift_mu, m_rw_w0, m_rw_w2, m_rw_a0, m_rw_a2, m_rw_g2, m_rw_k_k, m_rw_k_a, m_rw_r_k, m_rw_ln_w, m_rw_ln_b, m_w_out, m_norm_ffn, m_ffn_w1, m_ffn_w3, m_ffn_w2, m_norm_ple, m_ple_gate_w, m_ple_up_w, m_final_norm, v_norm_mix, v_w_in, v_s5_lam_re, v_s5_lam_im, v_s5_log_step, v_s5_b_re, v_s5_b_im, v_s5_c_re, v_s5_c_im, v_s5_d, v_s5_glu_w, v_s5_glu_b, v_rw_shift_mu, v_rw_w0, v_rw_w2, v_rw_a0, v_rw_a2, v_rw_g2, v_rw_k_k, v_rw_k_a, v_rw_r_k, v_rw_ln_w, v_rw_ln_b, v_w_out, v_norm_ffn, v_ffn_w1, v_ffn_w3, v_ffn_w2, v_norm_ple, v_ple_gate_w, v_ple_up_w, v_final_norm):
    args = dict(locals())
    w = {n: args[n] for n in WEIGHT_NAMES}
    m = {n: args["m_" + n] for n in WEIGHT_NAMES}
    v = {n: args["v_" + n] for n in WEIGHT_NAMES}
    return _step(x, p, loss_target, w, m, v)
```

```python
import functools
import math
from typing import Any, Callable, NamedTuple, Sequence

import jax
import jax.numpy as jnp
from jax import lax
from jax.experimental import pallas as pl
from jax.experimental.pallas import tpu as pltpu

F32 = jnp.float32
BF16 = jnp.bfloat16
MESH = pl.DeviceIdType.MESH

S5_WIDTH = 512
RWKV_WIDTH = 512
S5_GROUP = 16
S5_GROUPS = 32
S5_STATE = 64
S5_LANES = S5_GROUPS * S5_STATE
HEAD = 64
HEADS = 8
DECAY_LORA = 64
AAA_LORA = 64
GATE_LORA = 128
FFN_HIDDEN = 2816
RMS_EPS = 1e-6
GN_EPS = 64e-5
L2_EPS = 1e-12
ADAM_LR = 0.001
ADAM_B1 = 0.9
ADAM_B2 = 0.999
ADAM_EPS = 1e-08
ADAM_WD = 0.01
ADAM_STEP = 10

WKV_CHUNK = 64
WIRE = jnp.bfloat16
WKV_PASSES = 1
VMEM_LIMIT_BYTES = 48 * 1024 * 1024
LANE = 128
PACK_COLS = 1024
SMALL_ROWS = 24

WEIGHT_NAMES = ['norm_mix', 'w_in', 's5_lam_re', 's5_lam_im', 's5_log_step', 's5_b_re', 's5_b_im', 's5_c_re',
                's5_c_im', 's5_d', 's5_glu_w', 's5_glu_b', 'rw_shift_mu', 'rw_w0', 'rw_w2', 'rw_a0', 'rw_a2',
                'rw_g2', 'rw_k_k', 'rw_k_a', 'rw_r_k', 'rw_ln_w', 'rw_ln_b', 'w_out', 'norm_ffn', 'ffn_w1',
                'ffn_w3', 'ffn_w2', 'norm_ple', 'ple_gate_w', 'ple_up_w', 'final_norm']
SHARDED = {'w_in': 1, 's5_glu_w': 0, 'rw_w2': 1, 'rw_a2': 1, 'rw_g2': 1, 'w_out': 0, 'ffn_w1': 1, 'ffn_w3': 1,
           'ffn_w2': 0, 'ple_gate_w': 0, 'ple_up_w': 1}
SHARDED_NAMES = [n for n in WEIGHT_NAMES if n in SHARDED]
LATE_NAMES = ['w_out', 'ffn_w1', 'ffn_w3', 'ffn_w2', 'ple_gate_w', 'ple_up_w']
EARLY_NAMES = [n for n in SHARDED_NAMES if n not in LATE_NAMES]
LATE_GATHER = {'s5_scan': ['w_out', 'ple_gate_w', 'ple_up_w'], 'wkv_fwd': ['ffn_w1', 'ffn_w3'], 'ffn_up': ['ffn_w2']}
SMALL_NAMES = [n for n in WEIGHT_NAMES if n not in SHARDED]


def _params(sem=None):
    return pltpu.CompilerParams(dimension_semantics=sem, vmem_limit_bytes=VMEM_LIMIT_BYTES)


def _tile(n, target):
    best = None
    for d in range(LANE, min(n, target) + 1, LANE):
        if n % d == 0:
            best = d
    return n if best is None else best


_NN = (((1,), (0,)), ((), ()))
_NT = (((1,), (1,)), ((), ()))
_TN = (((0,), (0,)), ((), ()))


def _split(a):
    a = a.astype(F32)
    hi = a.astype(BF16)
    return hi, (a - hi.astype(F32)).astype(BF16)


def _dg(a, b, dims, passes):
    dg = lambda p, q: lax.dot_general(p, q, dims, preferred_element_type=F32)
    if passes == 1:
        return dg(a.astype(BF16), b.astype(BF16))
    bh, bl = _split(b)
    if passes == 2:
        return dg(a.astype(BF16), bh) + dg(a.astype(BF16), bl)
    ah, al = _split(a)
    return dg(ah, bh) + (dg(ah, bl) + dg(al, bh))


_DOT_BWD = {_NN: (("g", "b", _NT), ("a", "g", _TN)),
            _NT: (("g", "b", _NN), ("g", "a", _TN)),
            _TN: (("b", "g", _NT), ("a", "g", _NN))}


@functools.partial(jax.custom_vjp, nondiff_argnums=(2, 3))
def _dot(a, b, dims, passes):
    return _dg(a, b, dims, passes)


def _dot_fwd(a, b, dims, passes):
    return _dg(a, b, dims, passes), (a, b)


def _dot_bwd(dims, passes, res, g):
    env = {"a": res[0], "b": res[1], "g": g}
    return tuple(_dg(env[p], env[q], d, passes) for p, q, d in _DOT_BWD[dims])


_dot.defvjp(_dot_fwd, _dot_bwd)


def _bdot(x, w):
    return _dot(x, w, _NN, 1)


def _fdot(a, b, dims=_NN):
    return _dot(a, b, dims, 3)


@jax.custom_vjp
def _shift_down(z):
    return pltpu.roll(z, 1, 0)


def _shift_down_fwd(z):
    return pltpu.roll(z, 1, 0), None


def _shift_down_bwd(_, g):
    return (pltpu.roll(g, g.shape[0] - 1, 0),)


_shift_down.defvjp(_shift_down_fwd, _shift_down_bwd)


def _head_ones():
    r = lax.broadcasted_iota(jnp.int32, (RWKV_WIDTH, RWKV_WIDTH), 0) // HEAD
    c = lax.broadcasted_iota(jnp.int32, (RWKV_WIDTH, RWKV_WIDTH), 1) // HEAD
    return (r == c).astype(F32)


def _mm(name, a, b, mode, out_dtype=F32, precise=False, tm=1024, tn=1024, tk=1536, plan=None):
    if mode == 'nn':
        (m, k), (_, n) = a.shape, b.shape
    elif mode == 'nt':
        (m, k), (n, _) = a.shape, b.shape
    else:
        (k, m), (_, n) = a.shape, b.shape
    tm, tn, tk = _tile(m, tm), _tile(n, tn), _tile(k, tk)
    nm, nn, nk = m // tm, n // tn, k // tk
    dims = {'nn': _NN, 'nt': _NT, 'tn': _TN}[mode]
    plan = _NO_PLAN if plan is None else plan
    parts, plan_in_specs, plan_out_shape, plan_out_specs, plan_sems = _carry(plan, 2, 1)

    def body(*refs):
        a_ref, b_ref, o_ref = refs[0], refs[1], refs[2 + len(plan.ins)]
        acc_ref = refs[3 + len(plan.ins) + len(plan.out_shape)]
        i, j, kk = pl.program_id(0), pl.program_id(1), pl.program_id(2)

        if plan is not _NO_PLAN:
            pl.when((i == 0) & (j == 0) & (kk == 0))(lambda: plan.start(*parts(refs)))

        @pl.when(kk == 0)
        def _():
            acc_ref[...] = jnp.zeros_like(acc_ref)

        acc_ref[...] += _dg(a_ref[...], b_ref[...], dims, 3 if precise else 1)

        @pl.when(kk == nk - 1)
        def _():
            o_ref[...] = acc_ref[...].astype(o_ref.dtype)

        if plan is not _NO_PLAN:
            pl.when((i == nm - 1) & (j == nn - 1) & (kk == nk - 1))(lambda: plan.wait(*parts(refs)))

    if mode == 'tn':
        a_spec = pl.BlockSpec((tk, tm), lambda i, j, l: (l, i))
    else:
        a_spec = pl.BlockSpec((tm, tk), lambda i, j, l: (i, l))
    if mode == 'nt':
        b_spec = pl.BlockSpec((tn, tk), lambda i, j, l: (j, l))
    else:
        b_spec = pl.BlockSpec((tk, tn), lambda i, j, l: (l, j))
    res = pl.pallas_call(
        body, name=name,
        out_shape=[jax.ShapeDtypeStruct((m, n), out_dtype)] + plan_out_shape,
        grid=(nm, nn, nk),
        in_specs=[a_spec, b_spec] + plan_in_specs,
        out_specs=[pl.BlockSpec((tm, tn), lambda i, j, l: (i, j))] + plan_out_specs,
        scratch_shapes=[pltpu.VMEM((tm, tn), F32)] + plan_sems,
        compiler_params=_params(("parallel", "parallel", "arbitrary") if plan is _NO_PLAN else ("arbitrary",) * 3),
    )(a, b, *plan.ins)
    return res[0] if plan is _NO_PLAN else (res[0], res[1:])


def _full_spec(p):
    nd = p.ndim
    return pl.BlockSpec(p.shape, lambda i, nd=nd: (0,) * nd)


def _tok_fwd(name, fn, toks, params, outs, tm):
    t = toks[0].shape[0]
    nt, npar = len(toks), len(params)

    def body(*refs):
        tv = [r[...].astype(F32) for r in refs[:nt]]
        pv = [r[...].astype(F32) for r in refs[nt:nt + npar]]
        res = fn(*tv, *pv)
        for r, v in zip(refs[nt + npar:], res):
            r[...] = v.astype(r.dtype)

    return pl.pallas_call(
        body, name=name,
        out_shape=[jax.ShapeDtypeStruct((t, w), d) for w, d in outs],
        grid=(t // tm,),
        in_specs=[pl.BlockSpec((tm, a.shape[1]), lambda i: (i, 0)) for a in toks] + [_full_spec(p) for p in params],
        out_specs=[pl.BlockSpec((tm, w), lambda i: (i, 0)) for w, _ in outs],
        compiler_params=_params(("parallel",)),
    )(*toks, *params)


def _tok_bwd(name, fn, toks, params, cots, dtok, dpar, tm, acc_out=None, add_to=None):
    t = toks[0].shape[0]
    nt, npar = len(toks), len(params)
    cot_arrays = [c for c in cots if c is not None]
    ncot = len(cot_arrays)
    extra = [] if add_to is None else [add_to[1]]
    dtok_idx = [i for i, d in enumerate(dtok) if d is not None]
    dpar_idx = [i for i, d in enumerate(dpar) if d]

    def body(*refs):
        pos = 0
        tin = refs[pos:pos + nt]; pos += nt
        pin = refs[pos:pos + npar]; pos += npar
        cin = refs[pos:pos + ncot]; pos += ncot
        ein = refs[pos:pos + len(extra)]; pos += len(extra)
        dto = refs[pos:pos + len(dtok_idx)]; pos += len(dtok_idx)
        dpo = refs[pos:pos + len(dpar_idx)]; pos += len(dpar_idx)
        acc = refs[pos] if acc_out is not None else None
        first = pl.program_id(0) == 0

        tv = [r[...].astype(F32) for r in tin]
        pv = [r[...].astype(F32) for r in pin]
        res, vjp = jax.vjp(fn, *tv, *pv)
        cit = iter(cin)
        cs = tuple(jnp.ones_like(o) if c is None else next(cit)[...].astype(F32) for c, o in zip(cots, res))
        g = vjp(cs)
        for r, i in zip(dto, dtok_idx):
            v = g[i]
            if add_to is not None and add_to[0] == i:
                v = v + ein[0][...].astype(F32)
            r[...] = v.astype(r.dtype)

        @pl.when(first)
        def _():
            for r in dpo:
                r[...] = jnp.zeros_like(r)
            if acc is not None:
                acc[...] = jnp.zeros_like(acc)

        for r, i in zip(dpo, dpar_idx):
            r[...] += g[nt + i]
        if acc is not None:
            acc[...] += res[acc_out]

    out_shape = [jax.ShapeDtypeStruct(toks[i].shape, dtok[i]) for i in dtok_idx]
    out_shape += [jax.ShapeDtypeStruct(params[i].shape, F32) for i in dpar_idx]
    out_specs = [pl.BlockSpec((tm, toks[i].shape[1]), lambda i_: (i_, 0)) for i in dtok_idx]
    out_specs += [_full_spec(params[i]) for i in dpar_idx]
    if acc_out is not None:
        out_shape.append(jax.ShapeDtypeStruct((1, 1), F32))
        out_specs.append(pl.BlockSpec((1, 1), lambda i_: (0, 0)))
    tok_spec = lambda a: pl.BlockSpec((tm, a.shape[1]), lambda i_: (i_, 0))
    return pl.pallas_call(
        body, name=name,
        out_shape=out_shape,
        grid=(t // tm,),
        in_specs=[tok_spec(a) for a in toks] + [_full_spec(p) for p in params]
        + [tok_spec(c) for c in cot_arrays] + [tok_spec(e) for e in extra],
        out_specs=out_specs,
        compiler_params=_params(("arbitrary",)),
    )(*toks, *params, *cot_arrays, *extra)


def _small_fwd(name, fn, ins, outs):
    n = len(ins)

    def body(*refs):
        res = fn(*[r[...] for r in refs[:n]])
        for r, v in zip(refs[n:], res):
            r[...] = v.astype(r.dtype)

    return pl.pallas_call(
        body, name=name,
        out_shape=[jax.ShapeDtypeStruct(s, d) for s, d in outs],
        compiler_params=_params(),
    )(*ins)


def _small_bwd(name, fn, ins, cots):
    n = len(ins)

    def body(*refs):
        _, vjp = jax.vjp(fn, *[r[...] for r in refs[:n]])
        g = vjp(tuple(r[...] for r in refs[n:n + len(cots)]))
        for r, v in zip(refs[n + len(cots):], g):
            r[...] = v

    return pl.pallas_call(
        body, name=name,
        out_shape=[jax.ShapeDtypeStruct(a.shape, F32) for a in ins],
        compiler_params=_params(),
    )(*ins, *cots)


def _rms(x, g):
    return x * lax.rsqrt(jnp.mean(x * x, axis=-1, keepdims=True) + RMS_EPS) * g


def _f_norm_in(x, g):
    return (_rms(x, g),)


def _f_mix_res(x, mixed, g):
    h1 = x + mixed
    return h1, _rms(h1, g)


def _f_ffn_act(a13):
    a1, a3 = a13[:, :FFN_HIDDEN], a13[:, FFN_HIDDEN:]
    return (jax.nn.silu(a1) * a3,)


def _f_ffn_res(h1, ffo, g):
    h2 = h1 + ffo
    return h2, _rms(h2, g)


def _f_loss(h2, gpre, pu, target, g):
    h3 = h2 + jax.nn.sigmoid(gpre) * pu
    y = _rms(h3, g)
    err = jnp.square(y - target)
    return (0.5 * jnp.sum(jnp.mean(err, axis=-1, keepdims=True), axis=0, keepdims=True),)


def _f_s5_post(ypre, u, d, glu_w, glu_b):
    z = jax.nn.gelu(ypre + u * d)
    return (z * jax.nn.sigmoid(_bdot(z, glu_w) + glu_b),)


def _softplus(x):
    return jnp.maximum(x, 0.0) + jnp.log(1.0 + jnp.exp(-jnp.abs(x)))


def _f_rw_pre(z, carry, shift_mu, w0, w2, a0, a2, g2, k_k, k_a):
    rw = RWKV_WIDTH
    first_row = lax.broadcasted_iota(jnp.int32, z.shape, 0) == 0
    prev = jnp.where(first_row, carry, _shift_down(z))
    zs = z + (prev - z) * shift_mu
    o1, o2 = 3 * rw + DECAY_LORA, 3 * rw + DECAY_LORA + AAA_LORA
    r, k, v = zs[:, :rw], zs[:, rw:2 * rw], zs[:, 2 * rw:3 * rw]
    wl, al, gl = zs[:, 3 * rw:o1], zs[:, o1:o2], zs[:, o2:]
    w = -_softplus(-(w0 + _bdot(jnp.tanh(wl), w2))) - 0.5
    log_decay = -jnp.exp(w)
    a = jax.nn.sigmoid(a0 + _bdot(al, a2))
    g = _bdot(jax.nn.sigmoid(gl), g2)
    kk = k * k_k
    norm = jnp.sqrt(_fdot(kk * kk, _head_ones()))
    kk = kk / jnp.maximum(norm, L2_EPS)
    kp = k * (1.0 + (a - 1.0) * k_a)
    return r, log_decay, kp, v, -kk, kk * a, g


def _f_rw_post(y, r, kp, v, g, ln_w, ln_b, r_k):
    ones = _head_ones()
    yc = y - _fdot(y, ones) * (1.0 / HEAD)
    var = _fdot(yc * yc, ones) * (1.0 / HEAD)
    yn = yc * lax.rsqrt(var + GN_EPS) * ln_w + ln_b
    bonus = _fdot(r * kp * r_k, ones) * v
    return ((yn + bonus) * g,)


def _f_s5_lam(lam_re, lam_im, log_step):
    step = jnp.exp(log_step)
    dr, di = lam_re * step, lam_im * step
    e = jnp.exp(dr)
    lbr, lbi = e * jnp.cos(di), e * jnp.sin(di)
    nr, ni = lbr - 1.0, lbi
    den = lam_re * lam_re + lam_im * lam_im
    return lbr, lbi, (nr * lam_re + ni * lam_im) / den, (ni * lam_re - nr * lam_im) / den


def _f_s5_build(coef_r, coef_i, btr, bti, ctr, cti):
    bbr = coef_r * btr - coef_i * bti
    bbi = coef_r * bti + coef_i * btr
    rows = lax.broadcasted_iota(jnp.int32, (S5_WIDTH, S5_LANES), 0) // S5_GROUP
    cols = lax.broadcasted_iota(jnp.int32, (S5_WIDTH, S5_LANES), 1) // S5_STATE
    mask = (rows == cols).astype(F32)
    rep = lambda m: jnp.concatenate([m] * S5_GROUPS, axis=0) * mask
    return (jnp.concatenate([rep(bbr), rep(bbi)], axis=1), jnp.concatenate([rep(ctr), -rep(cti)], axis=1))


HALO = 8


def _rw_pre_specs(z, params, tm, order):
    halo_blocks = tm // HALO
    return ([pl.BlockSpec((tm, z.shape[1]), lambda i: (order(i), 0)),
             pl.BlockSpec((HALO, z.shape[1]), lambda i: (jnp.maximum(order(i) * halo_blocks - 1, 0), 0))]
            + [_full_spec(p) for p in params])


def _rw_pre_fwd(z, params, tm):
    t = z.shape[0]
    npar = len(params)

    def body(z_ref, halo_ref, *refs):
        carry = jnp.where(pl.program_id(0) == 0, 0.0, halo_ref[pl.ds(HALO - 1, 1), :])
        res = _f_rw_pre(z_ref[...], carry, *[r[...].astype(F32) for r in refs[:npar]])
        for r, v in zip(refs[npar:], res):
            r[...] = v

    return pl.pallas_call(
        body, name="rw_pre",
        out_shape=[jax.ShapeDtypeStruct((t, RWKV_WIDTH), F32)] * 7,
        grid=(t // tm,),
        in_specs=_rw_pre_specs(z, params, tm, lambda i: i),
        out_specs=[pl.BlockSpec((tm, RWKV_WIDTH), lambda i: (i, 0))] * 7,
        compiler_params=_params(("parallel",)),
    )(z, z, *params)


def _rw_pre_bwd(z, params, cots, tm):
    t = z.shape[0]
    nt = t // tm
    npar = len(params)
    order = lambda i: nt - 1 - i
    flat_cots = [a for group in cots for a in group]
    ncot = len(flat_cots)

    def body(z_ref, halo_ref, *refs):
        pin, cin = refs[:npar], list(refs[npar:npar + ncot])
        dz_ref = refs[npar + ncot]
        dpo = refs[npar + ncot + 1:npar + ncot + 1 + npar]
        dcarry_ref = refs[npar + ncot + 1 + npar]
        i = pl.program_id(0)

        @pl.when(i == 0)
        def _():
            dcarry_ref[...] = jnp.zeros_like(dcarry_ref)
            for r in dpo:
                r[...] = jnp.zeros_like(r)

        carry = jnp.where(i == nt - 1, 0.0, halo_ref[pl.ds(HALO - 1, 1), :])
        _, vjp = jax.vjp(_f_rw_pre, z_ref[...], carry, *[r[...].astype(F32) for r in pin])
        g = vjp(tuple(sum(cin.pop(0)[...] for _ in group) for group in cots))
        last_row = lax.broadcasted_iota(jnp.int32, z_ref.shape, 0) == tm - 1
        dz_ref[...] = g[0] + jnp.where(last_row, dcarry_ref[...], 0.0)
        dcarry_ref[...] = g[1]
        for r, v in zip(dpo, g[2:]):
            r[...] += v

    tok = lambda w: pl.BlockSpec((tm, w), lambda i: (order(i), 0))
    return pl.pallas_call(
        body, name="rw_pre_bwd",
        out_shape=[jax.ShapeDtypeStruct(z.shape, F32)] + [jax.ShapeDtypeStruct(p.shape, F32) for p in params],
        grid=(nt,),
        in_specs=_rw_pre_specs(z, params, tm, order) + [tok(RWKV_WIDTH)] * ncot,
        out_specs=[tok(z.shape[1])] + [_full_spec(p) for p in params],
        scratch_shapes=[pltpu.VMEM((1, z.shape[1]), F32)],
        compiler_params=_params(("arbitrary",)),
    )(z, z, *params, *flat_cots)


def _s5_scan(bu, lam, tm, plan):
    t, w = bu.shape
    h = w // 2
    nt = t // tm
    parts, plan_in_specs, plan_out_shape, plan_out_specs, plan_sems = _carry(plan, 2, 2)

    def body(*refs):
        bu_ref, lam_ref = refs[:2]
        x_ref, xp_ref = refs[2 + len(plan.ins):4 + len(plan.ins)]
        carry_ref = refs[4 + len(plan.ins) + len(plan.out_shape)]

        @pl.when(pl.program_id(0) == 0)
        def _():
            carry_ref[...] = jnp.zeros_like(carry_ref)
            plan.start(*parts(refs))

        lr, li = lam_ref[:, :h], lam_ref[:, h:]

        def step(s, c):
            cr, ci = c
            row = pl.ds(s, 1)
            xp_ref[row, :h] = cr
            xp_ref[row, h:] = ci
            nr = lr * cr - li * ci + bu_ref[row, :h]
            ni = lr * ci + li * cr + bu_ref[row, h:]
            x_ref[row, :h] = nr
            x_ref[row, h:] = ni
            return nr, ni

        cr, ci = lax.fori_loop(0, tm, step, (carry_ref[:, :h], carry_ref[:, h:]))
        carry_ref[:, :h] = cr
        carry_ref[:, h:] = ci

        @pl.when(pl.program_id(0) == nt - 1)
        def _():
            plan.wait(*parts(refs))

    spec = pl.BlockSpec((tm, w), lambda i: (i, 0))
    res = pl.pallas_call(
        body, name="s5_scan",
        out_shape=[jax.ShapeDtypeStruct((t, w), F32)] * 2 + plan_out_shape,
        grid=(nt,),
        in_specs=[spec, pl.BlockSpec((1, w), lambda i: (0, 0))] + plan_in_specs,
        out_specs=[spec, spec] + plan_out_specs,
        scratch_shapes=[pltpu.VMEM((1, w), F32)] + plan_sems,
        compiler_params=_params(("arbitrary",)),
    )(bu, lam, *plan.ins)
    return res[0], res[1], res[2:]


def _s5_scan_bwd(dx, xp, lam, tm):
    t, w = dx.shape
    h = w // 2
    nt = t // tm

    def body(dx_ref, xp_ref, lam_ref, dbu_ref, dlam_ref, carry_ref):
        @pl.when(pl.program_id(0) == 0)
        def _():
            carry_ref[...] = jnp.zeros_like(carry_ref)
            dlam_ref[...] = jnp.zeros_like(dlam_ref)

        lr, li = lam_ref[:, :h], lam_ref[:, h:]

        def step(s, c):
            cr, ci = c
            row = pl.ds(tm - 1 - s, 1)
            nr = lr * cr + li * ci + dx_ref[row, :h]
            ni = lr * ci - li * cr + dx_ref[row, h:]
            dbu_ref[row, :h] = nr
            dbu_ref[row, h:] = ni
            return nr, ni

        cr, ci = lax.fori_loop(0, tm, step, (carry_ref[:, :h], carry_ref[:, h:]))
        carry_ref[:, :h] = cr
        carry_ref[:, h:] = ci
        gr, gi = dbu_ref[:, :h], dbu_ref[:, h:]
        pr, pi_ = xp_ref[:, :h], xp_ref[:, h:]
        dlam_ref[:, :h] += jnp.sum(gr * pr + gi * pi_, axis=0, keepdims=True)
        dlam_ref[:, h:] += jnp.sum(gi * pr - gr * pi_, axis=0, keepdims=True)

    spec = pl.BlockSpec((tm, w), lambda i: (nt - 1 - i, 0))
    row_spec = pl.BlockSpec((1, w), lambda i: (0, 0))
    return pl.pallas_call(
        body, name="s5_scan_bwd",
        out_shape=[jax.ShapeDtypeStruct((t, w), F32), jax.ShapeDtypeStruct((1, w), F32)],
        grid=(nt,),
        in_specs=[spec, spec, row_spec],
        out_specs=[spec, row_spec],
        scratch_shapes=[pltpu.VMEM((1, w), F32)],
        compiler_params=_params(("arbitrary",)),
    )(dx, xp, lam)


def _wkv_chunks(s0, r, lw, k, v, a, b):
    c = r[0].shape[0]
    row = lax.broadcasted_iota(jnp.int32, (c, c), 0)
    col = lax.broadcasted_iota(jnp.int32, (c, c), 1)
    incl, strict = col <= row, col < row
    tri = incl.astype(F32)
    eye = (row == col).astype(F32)
    each = lambda f, *xs: [f(*t) for t in zip(*xs)]
    stack = lambda p, q: jnp.concatenate([p, q], axis=0)
    dot = lambda p, q, dims=_NN: _dot(p, q, dims, WKV_PASSES)
    lc = each(lambda l: _dot(tri, l, _NN, 2), lw)
    e_neg = each(lambda l: jnp.exp(-l), lc)
    ar = each(lambda x, z, l, w: stack(x * jnp.exp(l - w), z * jnp.exp(l)), a, r, lc, lw)
    bk = each(lambda x, z, e: stack(x * e, z * e), b, k, e_neg)
    m = each(lambda p, q: dot(p, q, _NT), ar, bk)
    mab = each(lambda q: jnp.where(strict, q[:c, :c], 0.0), m)
    mak_mrk = each(lambda q: stack(jnp.where(strict, q[:c, c:], 0.0), jnp.where(incl, q[c:, c:], 0.0)), m)
    mrb = each(lambda q: jnp.where(incl, q[c:, :c], 0.0), m)
    xy = each(lambda p, s, q, z: dot(p, s, _NT) + dot(q, z), ar, s0, mak_mrk, v)
    inv = each(lambda q: eye + q, mab)
    pw = each(lambda q: dot(q, q), mab)
    for _ in range(int(math.log2(c)) - 2):
        both = each(lambda i, q: dot(stack(i, q), q), inv, pw)
        inv = each(lambda i, q: i + q[:c], inv, both)
        pw = each(lambda q: q[c:], both)
    inv = each(lambda i, q: i + dot(i, q), inv, pw)
    u = each(lambda i, q: dot(i, q[:c]), inv, xy)
    y = each(lambda q, z, p: q[c:] + dot(z, p), xy, mrb, u)
    e_tot = each(lambda l: jnp.exp(jnp.sum(l, axis=0, keepdims=True)), lw)
    s1 = each(lambda s, p, z, q, e: (s + dot(stack(p, z), q, _TN)) * e, s0, u, v, bk, e_tot)
    return y, s1


def _carry(plan, n_args, n_outs):
    n_in, n_out = len(plan.ins), len(plan.out_shape)

    def parts(refs):
        base = n_args + n_in + n_outs
        return refs[n_args:n_args + n_in], refs[base:base + n_out], refs[base + n_out + 1:]

    return parts, [HBM_SPEC] * n_in, list(plan.out_shape), [HBM_SPEC] * n_out, list(plan.sems)


def _head_cols(ref):
    return tuple(ref[:, h * HEAD:(h + 1) * HEAD] for h in range(HEADS))


def _wkv_fwd(seqs, plan):
    t, w = seqs[0].shape
    c, n = WKV_CHUNK, HEAD
    nc = t // c
    parts, plan_in_specs, plan_out_shape, plan_out_specs, plan_sems = _carry(plan, 6, 2)

    def body(*refs):
        ins, (y_ref, ck_ref) = refs[:6], refs[6 + len(plan.ins):8 + len(plan.ins)]
        s_ref = refs[8 + len(plan.ins) + len(plan.out_shape)]

        @pl.when(pl.program_id(0) == 0)
        def _():
            s_ref[...] = jnp.zeros_like(s_ref)
            plan.start(*parts(refs))

        s0 = tuple(s_ref[h] for h in range(HEADS))
        ys, s1 = _wkv_chunks(s0, *[_head_cols(r) for r in ins])
        for h in range(HEADS):
            ck_ref[0, h] = s0[h]
            y_ref[:, h * n:(h + 1) * n] = ys[h]
            s_ref[h] = s1[h]

        @pl.when(pl.program_id(0) == nc - 1)
        def _():
            plan.wait(*parts(refs))

    spec = pl.BlockSpec((c, w), lambda i: (i, 0))
    res = pl.pallas_call(
        body, name="wkv_fwd",
        out_shape=[jax.ShapeDtypeStruct((t, w), F32), jax.ShapeDtypeStruct((nc, HEADS, n, n), F32)] + plan_out_shape,
        grid=(nc,),
        in_specs=[spec] * 6 + plan_in_specs,
        out_specs=[spec, pl.BlockSpec((1, HEADS, n, n), lambda i: (i, 0, 0, 0))] + plan_out_specs,
        scratch_shapes=[pltpu.VMEM((HEADS, n, n), F32)] + plan_sems,
        compiler_params=_params(("arbitrary",)),
    )(*seqs, *plan.ins)
    return res[0], res[1], res[2:]


def _wkv_bwd(seqs, ck, dy, plan):
    t, w = seqs[0].shape
    c, n = WKV_CHUNK, HEAD
    nc = t // c
    parts, plan_in_specs, plan_out_shape, plan_out_specs, plan_sems = _carry(plan, 8, 6)

    def body(*refs):
        ins, ck_ref, dy_ref = refs[:6], refs[6], refs[7]
        outs = refs[8 + len(plan.ins):14 + len(plan.ins)]
        ds_ref = refs[14 + len(plan.ins) + len(plan.out_shape)]

        @pl.when(pl.program_id(0) == 0)
        def _():
            ds_ref[...] = jnp.zeros_like(ds_ref)
            plan.start(*parts(refs))

        s0 = tuple(ck_ref[0, h] for h in range(HEADS))
        _, vjp = jax.vjp(_wkv_chunks, s0, *[_head_cols(r) for r in ins])
        g = vjp((list(_head_cols(dy_ref)), [ds_ref[h] for h in range(HEADS)]))
        for h in range(HEADS):
            ds_ref[h] = g[0][h]
            for o, d in zip(outs, g[1:]):
                o[:, h * n:(h + 1) * n] = d[h]

        @pl.when(pl.program_id(0) == nc - 1)
        def _():
            plan.wait(*parts(refs))

    spec = pl.BlockSpec((c, w), lambda i: (nc - 1 - i, 0))
    res = pl.pallas_call(
        body, name="wkv_bwd",
        out_shape=[jax.ShapeDtypeStruct((t, w), F32)] * 6 + plan_out_shape,
        grid=(nc,),
        in_specs=[spec] * 6 + [pl.BlockSpec((1, HEADS, n, n), lambda i: (nc - 1 - i, 0, 0, 0)), spec] + plan_in_specs,
        out_specs=[spec] * 6 + plan_out_specs,
        scratch_shapes=[pltpu.VMEM((HEADS, n, n), F32)] + plan_sems,
        compiler_params=_params(("arbitrary",)),
    )(*seqs, ck, dy, *plan.ins)
    return res[:6], res[6:]


def _coords():
    return lax.axis_index("x"), lax.axis_index("y"), lax.axis_index("c")


def _flip(v, f):
    return 1 - v if f else v


_CHIP_FLIPS = [(1, 0), (0, 1), (1, 1)]
_DEV_FLIPS = [(fx, fy, fc) for fx in (0, 1) for fy in (0, 1) for fc in (0, 1) if (fx, fy, fc) != (0, 0, 0)]
HBM_SPEC = pl.BlockSpec(memory_space=pl.ANY)


def _chip_peer(k, x, y):
    fx, fy = _CHIP_FLIPS[k]
    return _flip(x, fx), _flip(y, fy)


def _dev_peer(k, x, y, c):
    fx, fy, fc = _DEV_FLIPS[k]
    return _flip(x, fx), _flip(y, fy), _flip(c, fc)


def _rows_of_core(ref, core):
    h = ref.shape[-2] // 2
    rows = pl.ds(pl.multiple_of(core * h, 8), h)
    return ref.at[rows, :] if len(ref.shape) == 2 else ref.at[:, rows, :]


class _Plan(NamedTuple):
    ins: Sequence[Any]
    out_shape: Sequence[Any]
    sems: Sequence[Any]
    start: Callable
    wait: Callable


_NO_PLAN = _Plan([], [], [], lambda *_: None, lambda *_: None)


def _run_plan(name, plan):
    n_in, n_out = len(plan.ins), len(plan.out_shape)

    def body(*refs):
        parts = refs[:n_in], refs[n_in:n_in + n_out], refs[n_in + n_out:]
        plan.start(*parts)
        plan.wait(*parts)

    return pl.pallas_call(
        body, name=name, out_shape=list(plan.out_shape),
        in_specs=[HBM_SPEC] * n_in, out_specs=[HBM_SPEC] * n_out, scratch_shapes=list(plan.sems),
    )(*plan.ins)


def _gather_plan(shards):
    n = len(shards)

    def copies(srcs, outs, sems):
        send_sems, recv_sems, local_sems = sems
        x, y, c = _coords()
        me = 2 * x + y

        def remote(i, k, arriving):
            px, py = _chip_peer(k, x, y)
            return pltpu.make_async_remote_copy(
                src_ref=srcs[i], dst_ref=outs[i].at[2 * px + py if arriving else me],
                send_sem=send_sems.at[i, k], recv_sem=recv_sems.at[i, k],
                device_id=(px, py, c), device_id_type=MESH)

        own = [pltpu.make_async_copy(srcs[i], outs[i].at[me], local_sems.at[i]) for i in range(n)]
        pairs = [(i, k) for k in range(3) for i in range(n)]
        return own, [remote(i, k, False) for i, k in pairs], [remote(i, k, True) for i, k in pairs]

    return _Plan(
        ins=shards, out_shape=[jax.ShapeDtypeStruct((4,) + s.shape, s.dtype) for s in shards],
        sems=[pltpu.SemaphoreType.DMA((n, 3)), pltpu.SemaphoreType.DMA((n, 3)), pltpu.SemaphoreType.DMA((n,))],
        start=functools.partial(_start_copies, copies), wait=functools.partial(_wait_copies, copies))


def _start_copies(copies, ins, outs, sems):
    own, sends, _ = copies(ins, outs, sems)
    for cp in own + sends:
        cp.start()


def _wait_copies(copies, ins, outs, sems):
    own, sends, arrivals = copies(ins, outs, sems)
    for cp in arrivals:
        cp.wait_recv()
    for cp in sends:
        cp.wait_send()
    for cp in own:
        cp.wait()


def _exchange_plan(gs, small=None):
    n = len(gs)
    arrays = list(gs) + ([] if small is None else [small])

    def copies(srcs, outs, sems):
        send_sems, recv_sems, local_sems = sems
        x, y, c = _coords()
        me = 4 * x + 2 * y + c

        def piece(i, px, py, pc):
            if i == n:
                return srcs[i].at[4 * px + 2 * py + pc]
            return _rows_of_core(srcs[i].at[2 * px + py], pc)

        def remote(i, k, arriving):
            px, py, pc = _dev_peer(k, x, y, c)
            return pltpu.make_async_remote_copy(
                src_ref=piece(i, px, py, pc), dst_ref=outs[i].at[4 * px + 2 * py + pc if arriving else me],
                send_sem=send_sems.at[i, k], recv_sem=recv_sems.at[i, k],
                device_id=(px, py, pc), device_id_type=MESH)

        own = [pltpu.make_async_copy(piece(i, x, y, c), outs[i].at[me], local_sems.at[i]) for i in range(len(arrays))]
        pairs = [(i, k) for k in range(7) for i in range(len(arrays))]
        return own, [remote(i, k, False) for i, k in pairs], [remote(i, k, True) for i, k in pairs]

    out_shape = [jax.ShapeDtypeStruct((8, g.shape[1] // 2, g.shape[2]), g.dtype) for g in gs]
    if small is not None:
        out_shape.append(jax.ShapeDtypeStruct(small.shape, small.dtype))
    m = len(arrays)
    return _Plan(
        ins=arrays, out_shape=out_shape,
        sems=[pltpu.SemaphoreType.DMA((m, 7)), pltpu.SemaphoreType.DMA((m, 7)), pltpu.SemaphoreType.DMA((m,))],
        start=functools.partial(_start_copies, copies), wait=functools.partial(_wait_copies, copies))


def _share_cores(halves, small):
    n = len(halves)

    def body(*refs):
        srcs, small_src, outs, small_out = refs[:n], refs[n], refs[n + 1:2 * n + 1], refs[2 * n + 1]
        mine, theirs = refs[2 * n + 2:3 * n + 2], refs[3 * n + 2:4 * n + 2]
        send_sems, recv_sems, ssend, srecv, local_sems = refs[4 * n + 2:]
        x, y, c = _coords()
        me = 4 * x + 2 * y + c

        def big(i):
            return pltpu.make_async_remote_copy(
                src_ref=mine[i], dst_ref=theirs[i], send_sem=send_sems.at[i], recv_sem=recv_sems.at[i],
                device_id=(x, y, 1 - c), device_id_type=MESH)

        def tiny(k, arriving):
            px, py, pc = _dev_peer(k, x, y, c)
            return pltpu.make_async_remote_copy(
                src_ref=small_src, dst_ref=small_out.at[4 * px + 2 * py + pc if arriving else me],
                send_sem=ssend.at[k], recv_sem=srecv.at[k], device_id=(px, py, pc), device_id_type=MESH)

        small_sends = [tiny(k, False) for k in range(7)]
        own_small = pltpu.make_async_copy(small_src, small_out.at[me], local_sems.at[2 * n])
        stage = [pltpu.make_async_copy(srcs[i], mine[i], local_sems.at[i]) for i in range(n)]
        for cp in small_sends + [own_small] + stage:
            cp.start()
        sends = []
        for i in range(n):
            stage[i].wait()
            sends.append(big(i))
            sends[-1].start()
        store = [pltpu.make_async_copy(mine[i], outs[i].at[c], local_sems.at[i]) for i in range(n)]
        for cp in store:
            cp.start()
        for i in range(n):
            big(i).wait_recv()
            store.append(pltpu.make_async_copy(theirs[i], outs[i].at[1 - c], local_sems.at[n + i]))
            store[-1].start()
        for k in range(7):
            tiny(k, True).wait_recv()
        for cp in sends + small_sends:
            cp.wait_send()
        for cp in store + [own_small]:
            cp.wait()

    staged = [pltpu.VMEM(s.shape, s.dtype) for s in halves]
    res = pl.pallas_call(
        body, name="share_cores",
        out_shape=[jax.ShapeDtypeStruct((2,) + s.shape, s.dtype) for s in halves]
        + [jax.ShapeDtypeStruct((8,) + small.shape, small.dtype)],
        in_specs=[HBM_SPEC] * (n + 1), out_specs=[HBM_SPEC] * (n + 1),
        scratch_shapes=staged + staged + [
            pltpu.SemaphoreType.DMA((n,)), pltpu.SemaphoreType.DMA((n,)),
            pltpu.SemaphoreType.DMA((7,)), pltpu.SemaphoreType.DMA((7,)),
            pltpu.SemaphoreType.DMA((2 * n + 1,))],
        compiler_params=pltpu.CompilerParams(vmem_limit_bytes=VMEM_LIMIT_BYTES),
    )(*halves, small)
    return res[:n], res[n]


BF16_SUBLANES = 16


def _row_tile(n, target, step=BF16_SUBLANES):
    return max([d for d in range(step, min(n, target) + 1, step) if n % d == 0] or [n])


def _ew(name, fn, ins, outs, block_bytes=1 << 20):
    rows, cols = ins[0].shape[-2:]
    lead = max(math.prod(a.shape[:-2]) for a in ins)
    tr = _row_tile(rows, max(8, block_bytes // (4 * cols * lead)))
    n = len(ins)

    def spec(shape):
        if len(shape) == 2:
            return pl.BlockSpec((tr, cols), lambda i: (i, 0))
        return pl.BlockSpec((shape[0], tr, cols), lambda i: (0, i, 0))

    def body(*refs):
        res = fn(*[r[...] for r in refs[:n]])
        for r, v in zip(refs[n:], res):
            r[...] = v

    return pl.pallas_call(
        body, name=name,
        out_shape=[jax.ShapeDtypeStruct(s, F32) for s in outs],
        grid=(rows // tr,),
        in_specs=[spec(a.shape) for a in ins],
        out_specs=[spec(s) for s in outs],
        compiler_params=_params(("parallel",)),
    )(*ins)


def _sum_slots(a):
    total = a[0].astype(F32)
    for s in range(1, a.shape[0]):
        total = total + a[s].astype(F32)
    return (total,)


def _adamw(g, w, m, v):
    bc1 = 1.0 - ADAM_B1 ** ADAM_STEP
    bc2 = 1.0 - ADAM_B2 ** ADAM_STEP
    m_new = ADAM_B1 * m + (1.0 - ADAM_B1) * g
    v_new = ADAM_B2 * v + (1.0 - ADAM_B2) * jnp.square(g)
    delta = -ADAM_LR * ((m_new / bc1) / (jnp.sqrt(v_new / bc2) + ADAM_EPS) + ADAM_WD * w)
    return delta, m_new, v_new


def _mat(a):
    return a.reshape(a.shape[-2:])


def _to_shard_major(full, axis):
    rows, cols = full.shape
    if axis == 0:
        return full.reshape(4, rows // 4, cols)
    return full.reshape(rows, 4, cols // 4).transpose(1, 0, 2)


def _from_shard_major(a, axis):
    _, r, cs = a.shape
    if axis == 0:
        return a.reshape(4 * r, cs)
    return a.transpose(1, 0, 2).reshape(r, 4 * cs)


def _pack_small(arrays):
    flat = jnp.concatenate([arrays[n].reshape(-1) for n in SMALL_NAMES])
    flat = jnp.pad(flat, (0, 8 * SMALL_ROWS * PACK_COLS - flat.shape[0]))
    return flat.reshape(8, SMALL_ROWS, PACK_COLS)


def _unpack_small(packed, shapes):
    flat = packed.reshape(-1)
    out, off = {}, 0
    for n in SMALL_NAMES:
        size = math.prod(shapes[n])
        out[n] = flat[off:off + size].reshape(shapes[n])
        off += size
    return out


def _row(a):
    return a.reshape(1, -1)


def _local_step(x, p, target, wf, ws, late_shards):
    wf = dict(wf)
    t = x.shape[0]
    tm = min(256, t)
    g = {}

    lam_re, lam_im = ws['s5_lam_re'].reshape(S5_GROUPS, S5_STATE), ws['s5_lam_im'].reshape(S5_GROUPS, S5_STATE)
    log_step = ws['s5_log_step'].reshape(S5_GROUPS, 1)
    gp = (S5_GROUPS, S5_STATE)
    lam_ins = (lam_re, lam_im, log_step)
    lbr, lbi, cfr, cfi = _small_fwd("s5_lam", _f_s5_lam, lam_ins, [(gp, F32)] * 4)
    lam_row = jnp.concatenate([_row(lbr), _row(lbi)], axis=1)
    to_t = lambda a, perm: a.reshape((S5_GROUPS,) + a.shape[-2:]).transpose(perm).reshape(S5_GROUP, S5_LANES)
    build_ins = (_row(cfr), _row(cfi), to_t(ws['s5_b_re'], (2, 0, 1)), to_t(ws['s5_b_im'], (2, 0, 1)),
                 to_t(ws['s5_c_re'], (1, 0, 2)), to_t(ws['s5_c_im'], (1, 0, 2)))
    stack_shape = (S5_WIDTH, 2 * S5_LANES)
    bstack, cstack_t = _small_fwd("s5_build", _f_s5_build, build_ins, [(stack_shape, F32)] * 2)

    norm_mix, norm_ffn, norm_ple = _row(ws['norm_mix']), _row(ws['norm_ffn']), _row(ws['norm_ple'])
    final_norm = _row(ws['final_norm'])
    (xn,) = _tok_fwd("norm_in", _f_norm_in, [x], [norm_mix], [(x.shape[1], BF16)], tm)
    u = _mm("proj_s5", xn, wf['w_in'][:, :S5_WIDTH], 'nn')
    z = _mm("proj_rw", xn, wf['w_in'][:, S5_WIDTH:], 'nn')

    bu = _mm("s5_bu", u, bstack, 'nn')
    def late_plan(carrier):
        return _gather_plan([late_shards[n] for n in LATE_GATHER[carrier]])

    def arrived(carrier, got):
        wf.update({n: _from_shard_major(a, SHARDED[n]) for n, a in zip(LATE_GATHER[carrier], got)})

    xs, xs_prev, got = _s5_scan(bu, lam_row, tm, late_plan('s5_scan'))
    arrived('s5_scan', got)
    ypre = _mm("s5_y", xs, cstack_t, 'nt')
    s5_par = [_row(ws['s5_d']), wf['s5_glu_w'], _row(ws['s5_glu_b'])]
    (s5_out,) = _tok_fwd("s5_post", _f_s5_post, [ypre, u], s5_par, [(S5_WIDTH, BF16)], tm)

    pre_par = [_row(ws['rw_shift_mu']), _row(ws['rw_w0']), wf['rw_w2'], _row(ws['rw_a0']), wf['rw_a2'],
               wf['rw_g2'], _row(ws['rw_k_k']), _row(ws['rw_k_a'])]
    r, lw, kp, v, an, bn, gate = _rw_pre_fwd(z, pre_par, tm)
    seqs = [r, lw, kp, v, an, bn]
    y_wkv, ck, got = _wkv_fwd(seqs, late_plan('wkv_fwd'))
    arrived('wkv_fwd', got)
    post_par = [_row(ws['rw_ln_w']), _row(ws['rw_ln_b']), _row(ws['rw_r_k'])]
    post_toks = [y_wkv, r, kp, v, gate]
    (rw_out,) = _tok_fwd("rw_post", _f_rw_post, post_toks, post_par, [(RWKV_WIDTH, BF16)], tm)

    mixcat = jnp.concatenate([s5_out, rw_out], axis=1)
    mixed = _mm("mix_out", mixcat, wf['w_out'], 'nn')
    h1, hn = _tok_fwd("mix_res", _f_mix_res, [x, mixed], [norm_ffn], [(x.shape[1], F32), (x.shape[1], BF16)], tm)
    w13 = jnp.concatenate([wf['ffn_w1'], wf['ffn_w3']], axis=1)
    a13, got = _mm("ffn_up", hn, w13, 'nn', plan=late_plan('ffn_up'))
    arrived('ffn_up', got)
    (f,) = _tok_fwd("ffn_act", _f_ffn_act, [a13], [], [(FFN_HIDDEN, BF16)], tm)
    ffo = _mm("ffn_down", f, wf['ffn_w2'], 'nn')
    h2, hp = _tok_fwd("ffn_res", _f_ffn_res, [h1, ffo], [norm_ple], [(x.shape[1], F32), (x.shape[1], BF16)], tm)
    gpre = _mm("ple_gate", hp, wf['ple_gate_w'], 'nn')
    pu = _mm("ple_up", p, wf['ple_up_w'], 'nn')

    dh2, dgpre, dpu, g['final_norm'], loss = _tok_bwd(
        "loss", _f_loss, [h2, gpre, pu, target], [final_norm], [None],
        [F32, BF16, BF16, None], [True], tm, acc_out=0)
    g['ple_gate_w'] = _mm("d_ple_gate_w", hp, dgpre, 'tn', out_dtype=WIRE)
    g['ple_up_w'] = _mm("d_ple_up_w", p, dpu, 'tn', out_dtype=WIRE)
    dhp = _mm("d_hp", dgpre, wf['ple_gate_w'], 'nt')
    dh1, dffo, g['norm_ple'] = _tok_bwd("ffn_res_bwd", _f_ffn_res, [h1, ffo], [norm_ple], [dh2, dhp],
                                        [F32, BF16], [True], tm)
    g['ffn_w2'] = _mm("d_ffn_w2", f, dffo, 'tn', out_dtype=WIRE)
    df = _mm("d_f", dffo, wf['ffn_w2'], 'nt', out_dtype=BF16)
    (da13,) = _tok_bwd("ffn_act_bwd", _f_ffn_act, [a13], [], [df], [BF16], [], tm)
    dw13 = _mm("d_ffn_w13", hn, da13, 'tn', out_dtype=WIRE)
    g['ffn_w1'], g['ffn_w3'] = dw13[:, :FFN_HIDDEN], dw13[:, FFN_HIDDEN:]
    dhn = _mm("d_hn", da13, w13, 'nt')
    dx_a, dmixed, g['norm_ffn'] = _tok_bwd("mix_res_bwd", _f_mix_res, [x, mixed], [norm_ffn], [dh1, dhn],
                                           [F32, BF16], [True], tm)
    g['w_out'] = _mm("d_w_out", mixcat, dmixed, 'tn', out_dtype=WIRE)
    dmixcat = _mm("d_mixcat", dmixed, wf['w_out'], 'nt')
    ds5_out, drw_out = dmixcat[:, :S5_WIDTH], dmixcat[:, S5_WIDTH:]

    dy_wkv, dr_b, dkp_b, dv_b, dgate, g['rw_ln_w'], g['rw_ln_b'], g['rw_r_k'] = _tok_bwd(
        "rw_post_bwd", _f_rw_post, post_toks, post_par, [drw_out], [F32] * 5, [True] * 3, tm)
    late_exchange = _exchange_plan([_to_shard_major(g[n], SHARDED[n]).astype(WIRE) for n in LATE_NAMES])
    dseqs, late_pieces = _wkv_bwd(seqs, ck, dy_wkv, late_exchange)
    pre_cots = [(dseqs[0], dr_b), (dseqs[1],), (dseqs[2], dkp_b), (dseqs[3], dv_b), (dseqs[4],), (dseqs[5],),
                (dgate,)]
    dz, *dpre = _rw_pre_bwd(z, pre_par, pre_cots, tm)
    for n, d in zip(['rw_shift_mu', 'rw_w0', 'rw_w2', 'rw_a0', 'rw_a2', 'rw_g2', 'rw_k_k', 'rw_k_a'], dpre):
        g[n] = d

    dypre, du_a, g['s5_d'], g['s5_glu_w'], g['s5_glu_b'] = _tok_bwd(
        "s5_post_bwd", _f_s5_post, [ypre, u], s5_par, [ds5_out], [F32, F32], [True] * 3, tm)
    dxs = _mm("d_s5_x", dypre, cstack_t, 'nn')
    dcstack_t = _mm("d_s5_c", dypre, xs, 'tn')
    dbu, dlam_row = _s5_scan_bwd(dxs, xs_prev, lam_row, tm)
    du_b = _mm("d_s5_u", dbu, bstack, 'nt')
    dbstack = _mm("d_s5_b", u, dbu, 'tn')
    dbuild = _small_bwd("s5_build_bwd", _f_s5_build, build_ins, (dbstack, dcstack_t))
    lam_cots = (dlam_row[:, :S5_LANES].reshape(gp), dlam_row[:, S5_LANES:].reshape(gp),
                dbuild[0].reshape(gp), dbuild[1].reshape(gp))
    g['s5_lam_re'], g['s5_lam_im'], g['s5_log_step'] = _small_bwd("s5_lam_bwd", _f_s5_lam, lam_ins, lam_cots)
    from_t = lambda a, perm: a.reshape(S5_GROUP, S5_GROUPS, S5_STATE).transpose(perm)
    g['s5_b_re'], g['s5_b_im'] = from_t(dbuild[2], (1, 2, 0)), from_t(dbuild[3], (1, 2, 0))
    g['s5_c_re'], g['s5_c_im'] = from_t(dbuild[4], (1, 0, 2)), from_t(dbuild[5], (1, 0, 2))

    dproj = jnp.concatenate([(du_a + du_b).astype(BF16), dz.astype(BF16)], axis=1)
    g['w_in'] = _mm("d_w_in", xn, dproj, 'tn', out_dtype=WIRE)
    dxn = _mm("d_xn", dproj, wf['w_in'], 'nt')
    grad_x, g['norm_mix'] = _tok_bwd("norm_in_bwd", _f_norm_in, [x], [norm_mix], [dxn], [F32], [True], tm,
                                     add_to=(0, dx_a))
    return loss[0, 0], grad_x, g, late_pieces


def _step(x, p, target, w, m, v):
    shards = {n: _mat(w[n]).astype(BF16) for n in SHARDED_NAMES}
    early = _run_plan("gather_early", _gather_plan([shards[n] for n in EARLY_NAMES]))
    wf = {n: _from_shard_major(a, SHARDED[n]) for n, a in zip(EARLY_NAMES, early)}
    ws = {n: w[n] for n in SMALL_NAMES}

    loss, grad_x, g, late_pieces = _local_step(x[0], p[0, 0], target[0], wf, ws, shards)

    early_plan = _exchange_plan([_to_shard_major(g[n], SHARDED[n]).astype(WIRE) for n in EARLY_NAMES],
                                _pack_small({n: g[n] for n in SMALL_NAMES}))
    *early_pieces, by_dev = _run_plan("exchange_early", early_plan)
    pieces = dict(zip(LATE_NAMES + EARLY_NAMES, list(late_pieces) + early_pieces))
    halves = [_ew("add_devices_" + n, _sum_slots, [pieces[n]], [pieces[n].shape[1:]])[0] for n in SHARDED_NAMES]
    (small_piece,) = _ew("add_devices_small", _sum_slots, [by_dev], [by_dev.shape[1:]])
    both, small_g = _share_cores(halves, small_piece)

    kinds = [{}, {}, {}, {}]
    for n, gn in zip(SHARDED_NAMES, both):
        shard = _mat(w[n]).shape
        res = _ew("adamw_" + n, _adamw, [gn.reshape(shard), _mat(w[n]), _mat(m[n]), _mat(v[n])], [shard] * 3)
        for kind, a in zip(kinds, [gn] + list(res)):
            kind[n] = a.reshape(w[n].shape)
    flat = (8 * SMALL_ROWS, PACK_COLS)
    packed = [_pack_small({n: d[n] for n in SMALL_NAMES}).reshape(flat) for d in (w, m, v)]
    small_res = _ew("adamw_small", _adamw, [small_g.reshape(flat)] + packed, [flat] * 3)
    small_shapes = {n: w[n].shape for n in SMALL_NAMES}
    for kind, a in zip(kinds, [small_g] + list(small_res)):
        kind.update(_unpack_small(a, small_shapes))
    total = lax.psum(loss, ("x", "y", "c"))
    return (total, grad_x[None], *[kind[n] for kind in kinds for n in WEIGHT_NAMES])


def kernel(x, p, norm_mix, w_in, s5_lam_re, s5_lam_im, s5_log_step, s5_b_re, s5_b_im, s5_c_re, s5_c_im, s5_d, s5_glu_w, s5_glu_b, rw_shift_mu, rw_w0, rw_w2, rw_a0, rw_a2, rw_g2, rw_k_k, rw_k_a, rw_r_k, rw_ln_w, rw_ln_b, w_out, norm_ffn, ffn_w1, ffn_w3, ffn_w2, norm_ple, ple_gate_w, ple_up_w, final_norm, loss_target, m_norm_mix, m_w_in, m_s5_lam_re, m_s5_lam_im, m_s5_log_step, m_s5_b_re, m_s5_b_im, m_s5_c_re, m_s5_c_im, m_s5_d, m_s5_glu_w, m_s5_glu_b, m_rw_shift_mu, m_rw_w0, m_rw_w2, m_rw_a0, m_rw_a2, m_rw_g2, m_rw_k_k, m_rw_k_a, m_rw_r_k, m_rw_ln_w, m_rw_ln_b, m_w_out, m_norm_ffn, m_ffn_w1, m_ffn_w3, m_ffn_w2, m_norm_ple, m_ple_gate_w, m_ple_up_w, m_final_norm, v_norm_mix, v_w_in, v_s5_lam_re, v_s5_lam_im, v_s5_log_step, v_s5_b_re, v_s5_b_im, v_s5_c_re, v_s5_c_im, v_s5_d, v_s5_glu_w, v_s5_glu_b, v_rw_shift_mu, v_rw_w0, v_rw_w2, v_rw_a0, v_rw_a2, v_rw_g2, v_rw_k_k, v_rw_k_a, v_rw_r_k, v_rw_ln_w, v_rw_ln_b, v_w_out, v_norm_ffn, v_ffn_w1, v_ffn_w3, v_ffn_w2, v_norm_ple, v_ple_gate_w, v_ple_up_w, v_final_norm):
    args = dict(locals())
    w = {n: args[n] for n in WEIGHT_NAMES}
    m = {n: args["m_" + n] for n in WEIGHT_NAMES}
    v = {n: args["v_" + n] for n in WEIGHT_NAMES}
    return _step(x, p, loss_target, w, m, v)
```

```python
import functools
import math
from typing import Any, Callable, NamedTuple, Sequence

import jax
import jax.numpy as jnp
from jax import lax
from jax.experimental import pallas as pl
from jax.experimental.pallas import tpu as pltpu

F32 = jnp.float32
BF16 = jnp.bfloat16
MESH = pl.DeviceIdType.MESH

S5_WIDTH = 512
RWKV_WIDTH = 512
S5_GROUP = 16
S5_GROUPS = 32
S5_STATE = 64
S5_LANES = S5_GROUPS * S5_STATE
S5_TILE_GROUPS = 8
S5_TILES = S5_GROUPS // S5_TILE_GROUPS
S5_TILE_CH = S5_TILE_GROUPS * S5_GROUP
S5_TILE_LANES = S5_TILE_GROUPS * S5_STATE
HEAD = 64
HEADS = 8
DECAY_LORA = 64
AAA_LORA = 64
GATE_LORA = 128
FFN_HIDDEN = 2816
RMS_EPS = 1e-6
GN_EPS = 64e-5
L2_EPS = 1e-12
ADAM_LR = 0.001
ADAM_B1 = 0.9
ADAM_B2 = 0.999
ADAM_EPS = 1e-08
ADAM_WD = 0.01
ADAM_STEP = 10

WKV_CHUNK = 64
WIRE = jnp.bfloat16
WKV_PASSES = 1
VMEM_LIMIT_BYTES = 48 * 1024 * 1024
LANE = 128
PACK_COLS = 1024
SMALL_ROWS = 24

WEIGHT_NAMES = ['norm_mix', 'w_in', 's5_lam_re', 's5_lam_im', 's5_log_step', 's5_b_re', 's5_b_im', 's5_c_re',
                's5_c_im', 's5_d', 's5_glu_w', 's5_glu_b', 'rw_shift_mu', 'rw_w0', 'rw_w2', 'rw_a0', 'rw_a2',
                'rw_g2', 'rw_k_k', 'rw_k_a', 'rw_r_k', 'rw_ln_w', 'rw_ln_b', 'w_out', 'norm_ffn', 'ffn_w1',
                'ffn_w3', 'ffn_w2', 'norm_ple', 'ple_gate_w', 'ple_up_w', 'final_norm']
SHARDED = {'w_in': 1, 's5_glu_w': 0, 'rw_w2': 1, 'rw_a2': 1, 'rw_g2': 1, 'w_out': 0, 'ffn_w1': 1, 'ffn_w3': 1,
           'ffn_w2': 0, 'ple_gate_w': 0, 'ple_up_w': 1}
SHARDED_NAMES = [n for n in WEIGHT_NAMES if n in SHARDED]
LATE_NAMES = ['w_out', 'ffn_w1', 'ffn_w3', 'ffn_w2', 'ple_gate_w', 'ple_up_w']
EARLY_NAMES = [n for n in SHARDED_NAMES if n not in LATE_NAMES]
LATE_GATHER = {'s5_scan': ['w_out', 'ple_gate_w', 'ple_up_w'], 'wkv_fwd': ['ffn_w1', 'ffn_w3'], 'ffn_up': ['ffn_w2']}
SMALL_NAMES = [n for n in WEIGHT_NAMES if n not in SHARDED]


def _params(sem=None):
    return pltpu.CompilerParams(dimension_semantics=sem, vmem_limit_bytes=VMEM_LIMIT_BYTES)


def _tile(n, target):
    best = None
    for d in range(LANE, min(n, target) + 1, LANE):
        if n % d == 0:
            best = d
    return n if best is None else best


_NN = (((1,), (0,)), ((), ()))
_NT = (((1,), (1,)), ((), ()))
_TN = (((0,), (0,)), ((), ()))


def _split(a):
    a = a.astype(F32)
    hi = a.astype(BF16)
    return hi, (a - hi.astype(F32)).astype(BF16)


def _dg(a, b, dims, passes):
    dg = lambda p, q: lax.dot_general(p, q, dims, preferred_element_type=F32)
    if passes == 1:
        return dg(a.astype(BF16), b.astype(BF16))
    bh, bl = _split(b)
    if passes == 2:
        return dg(a.astype(BF16), bh) + dg(a.astype(BF16), bl)
    ah, al = _split(a)
    return dg(ah, bh) + (dg(ah, bl) + dg(al, bh))


_DOT_BWD = {_NN: (("g", "b", _NT), ("a", "g", _TN)),
            _NT: (("g", "b", _NN), ("g", "a", _TN)),
            _TN: (("b", "g", _NT), ("a", "g", _NN))}


@functools.partial(jax.custom_vjp, nondiff_argnums=(2, 3))
def _dot(a, b, dims, passes):
    return _dg(a, b, dims, passes)


def _dot_fwd(a, b, dims, passes):
    return _dg(a, b, dims, passes), (a, b)


def _dot_bwd(dims, passes, res, g):
    env = {"a": res[0], "b": res[1], "g": g}
    return tuple(_dg(env[p], env[q], d, passes) for p, q, d in _DOT_BWD[dims])


_dot.defvjp(_dot_fwd, _dot_bwd)


def _bdot(x, w):
    return _dot(x, w, _NN, 1)


def _fdot(a, b, dims=_NN):
    return _dot(a, b, dims, 3)


@jax.custom_vjp
def _shift_down(z):
    return pltpu.roll(z, 1, 0)


def _shift_down_fwd(z):
    return pltpu.roll(z, 1, 0), None


def _shift_down_bwd(_, g):
    return (pltpu.roll(g, g.shape[0] - 1, 0),)


_shift_down.defvjp(_shift_down_fwd, _shift_down_bwd)


def _head_ones():
    r = lax.broadcasted_iota(jnp.int32, (RWKV_WIDTH, RWKV_WIDTH), 0) // HEAD
    c = lax.broadcasted_iota(jnp.int32, (RWKV_WIDTH, RWKV_WIDTH), 1) // HEAD
    return (r == c).astype(F32)


def _mm(name, a, b, mode, out_dtype=F32, precise=False, tm=1024, tn=1024, tk=1536, plan=None):
    if mode == 'nn':
        (m, k), (_, n) = a.shape, b.shape
    elif mode == 'nt':
        (m, k), (n, _) = a.shape, b.shape
    else:
        (k, m), (_, n) = a.shape, b.shape
    tm, tn, tk = _tile(m, tm), _tile(n, tn), _tile(k, tk)
    nm, nn, nk = m // tm, n // tn, k // tk
    dims = {'nn': _NN, 'nt': _NT, 'tn': _TN}[mode]
    plan = _NO_PLAN if plan is None else plan
    parts, plan_in_specs, plan_out_shape, plan_out_specs, plan_sems = _carry(plan, 2, 1)

    def body(*refs):
        a_ref, b_ref, o_ref = refs[0], refs[1], refs[2 + len(plan.ins)]
        acc_ref = refs[3 + len(plan.ins) + len(plan.out_shape)]
        i, j, kk = pl.program_id(0), pl.program_id(1), pl.program_id(2)

        if plan is not _NO_PLAN:
            pl.when((i == 0) & (j == 0) & (kk == 0))(lambda: plan.start(*parts(refs)))

        @pl.when(kk == 0)
        def _():
            acc_ref[...] = jnp.zeros_like(acc_ref)

        acc_ref[...] += _dg(a_ref[...], b_ref[...], dims, 3 if precise else 1)

        @pl.when(kk == nk - 1)
        def _():
            o_ref[...] = acc_ref[...].astype(o_ref.dtype)

        if plan is not _NO_PLAN:
            pl.when((i == nm - 1) & (j == nn - 1) & (kk == nk - 1))(lambda: plan.wait(*parts(refs)))

    if mode == 'tn':
        a_spec = pl.BlockSpec((tk, tm), lambda i, j, l: (l, i))
    else:
        a_spec = pl.BlockSpec((tm, tk), lambda i, j, l: (i, l))
    if mode == 'nt':
        b_spec = pl.BlockSpec((tn, tk), lambda i, j, l: (j, l))
    else:
        b_spec = pl.BlockSpec((tk, tn), lambda i, j, l: (l, j))
    res = pl.pallas_call(
        body, name=name,
        out_shape=[jax.ShapeDtypeStruct((m, n), out_dtype)] + plan_out_shape,
        grid=(nm, nn, nk),
        in_specs=[a_spec, b_spec] + plan_in_specs,
        out_specs=[pl.BlockSpec((tm, tn), lambda i, j, l: (i, j))] + plan_out_specs,
        scratch_shapes=[pltpu.VMEM((tm, tn), F32)] + plan_sems,
        compiler_params=_params(("parallel", "parallel", "arbitrary") if plan is _NO_PLAN else ("arbitrary",) * 3),
    )(a, b, *plan.ins)
    return res[0] if plan is _NO_PLAN else (res[0], res[1:])


def _mm_tiles(name, a, b, mode, out_shape, grid, a_spec, b_spec, o_spec):
    dims = {'nn': _NN, 'nt': _NT, 'tn': _TN}[mode]
    nk = grid[2]

    def body(a_ref, b_ref, o_ref, acc_ref):
        kk = pl.program_id(2)

        @pl.when(kk == 0)
        def _():
            acc_ref[...] = jnp.zeros_like(acc_ref)

        acc_ref[...] += _dg(a_ref[...], b_ref[...], dims, 1)

        @pl.when(kk == nk - 1)
        def _():
            o_ref[...] = acc_ref[...]

    return pl.pallas_call(
        body, name=name,
        out_shape=jax.ShapeDtypeStruct(out_shape, F32),
        grid=grid, in_specs=[a_spec, b_spec], out_specs=o_spec,
        scratch_shapes=[pltpu.VMEM(o_spec.block_shape, F32)],
        compiler_params=_params(("parallel", "parallel", "arbitrary")),
    )(a, b)


def _s5_expand(name, u, blk, tm=1024):
    t = u.shape[0]
    tm = min(tm, t)
    ch, ln, nt = S5_TILE_CH, S5_TILE_LANES, S5_TILES
    return _mm_tiles(name, u, blk, 'nn', (t, 2 * S5_LANES), (t // tm, 2 * nt, 1),
                     pl.BlockSpec((tm, ch), lambda i, j, l: (i, j % nt)),
                     pl.BlockSpec((ch, ln), lambda i, j, l: (j % nt, j // nt)),
                     pl.BlockSpec((tm, ln), lambda i, j, l: (i, j)))


def _s5_contract(name, x, blk, tm=1024):
    t = x.shape[0]
    tm = min(tm, t)
    ch, ln, nt = S5_TILE_CH, S5_TILE_LANES, S5_TILES
    return _mm_tiles(name, x, blk, 'nt', (t, S5_WIDTH), (t // tm, nt, 2),
                     pl.BlockSpec((tm, ln), lambda i, j, l: (i, j + nt * l)),
                     pl.BlockSpec((ch, ln), lambda i, j, l: (j, l)),
                     pl.BlockSpec((tm, ch), lambda i, j, l: (i, j)))


def _s5_block_grad(name, u, x, tk=1024):
    t = u.shape[0]
    tk = min(tk, t)
    ch, ln, nt = S5_TILE_CH, S5_TILE_LANES, S5_TILES
    return _mm_tiles(name, u, x, 'tn', (S5_WIDTH, 2 * ln), (nt, 2, t // tk),
                     pl.BlockSpec((tk, ch), lambda i, j, l: (l, i)),
                     pl.BlockSpec((tk, ln), lambda i, j, l: (l, i + nt * j)),
                     pl.BlockSpec((ch, ln), lambda i, j, l: (i, j)))


def _full_spec(p):
    nd = p.ndim
    return pl.BlockSpec(p.shape, lambda i, nd=nd: (0,) * nd)


def _tok_fwd(name, fn, toks, params, outs, tm):
    t = toks[0].shape[0]
    nt, npar = len(toks), len(params)

    def body(*refs):
        tv = [r[...].astype(F32) for r in refs[:nt]]
        pv = [r[...].astype(F32) for r in refs[nt:nt + npar]]
        res = fn(*tv, *pv)
        for r, v in zip(refs[nt + npar:], res):
            r[...] = v.astype(r.dtype)

    return pl.pallas_call(
        body, name=name,
        out_shape=[jax.ShapeDtypeStruct((t, w), d) for w, d in outs],
        grid=(t // tm,),
        in_specs=[pl.BlockSpec((tm, a.shape[1]), lambda i: (i, 0)) for a in toks] + [_full_spec(p) for p in params],
        out_specs=[pl.BlockSpec((tm, w), lambda i: (i, 0)) for w, _ in outs],
        compiler_params=_params(("parallel",)),
    )(*toks, *params)


def _tok_bwd(name, fn, toks, params, cots, dtok, dpar, tm, acc_out=None, add_to=None):
    t = toks[0].shape[0]
    nt, npar = len(toks), len(params)
    cot_arrays = [c for c in cots if c is not None]
    ncot = len(cot_arrays)
    extra = [] if add_to is None else [add_to[1]]
    dtok_idx = [i for i, d in enumerate(dtok) if d is not None]
    dpar_idx = [i for i, d in enumerate(dpar) if d]

    def body(*refs):
        pos = 0
        tin = refs[pos:pos + nt]; pos += nt
        pin = refs[pos:pos + npar]; pos += npar
        cin = refs[pos:pos + ncot]; pos += ncot
        ein = refs[pos:pos + len(extra)]; pos += len(extra)
        dto = refs[pos:pos + len(dtok_idx)]; pos += len(dtok_idx)
        dpo = refs[pos:pos + len(dpar_idx)]; pos += len(dpar_idx)
        acc = refs[pos] if acc_out is not None else None
        first = pl.program_id(0) == 0

        tv = [r[...].astype(F32) for r in tin]
        pv = [r[...].astype(F32) for r in pin]
        res, vjp = jax.vjp(fn, *tv, *pv)
        cit = iter(cin)
        cs = tuple(jnp.ones_like(o) if c is None else next(cit)[...].astype(F32) for c, o in zip(cots, res))
        g = vjp(cs)
        for r, i in zip(dto, dtok_idx):
            v = g[i]
            if add_to is not None and add_to[0] == i:
                v = v + ein[0][...].astype(F32)
            r[...] = v.astype(r.dtype)

        @pl.when(first)
        def _():
            for r in dpo:
                r[...] = jnp.zeros_like(r)
            if acc is not None:
                acc[...] = jnp.zeros_like(acc)

        for r, i in zip(dpo, dpar_idx):
            r[...] += g[nt + i]
        if acc is not None:
            acc[...] += res[acc_out]

    out_shape = [jax.ShapeDtypeStruct(toks[i].shape, dtok[i]) for i in dtok_idx]
    out_shape += [jax.ShapeDtypeStruct(params[i].shape, F32) for i in dpar_idx]
    out_specs = [pl.BlockSpec((tm, toks[i].shape[1]), lambda i_: (i_, 0)) for i in dtok_idx]
    out_specs += [_full_spec(params[i]) for i in dpar_idx]
    if acc_out is not None:
        out_shape.append(jax.ShapeDtypeStruct((1, 1), F32))
        out_specs.append(pl.BlockSpec((1, 1), lambda i_: (0, 0)))
    tok_spec = lambda a: pl.BlockSpec((tm, a.shape[1]), lambda i_: (i_, 0))
    return pl.pallas_call(
        body, name=name,
        out_shape=out_shape,
        grid=(t // tm,),
        in_specs=[tok_spec(a) for a in toks] + [_full_spec(p) for p in params]
        + [tok_spec(c) for c in cot_arrays] + [tok_spec(e) for e in extra],
        out_specs=out_specs,
        compiler_params=_params(("arbitrary",)),
    )(*toks, *params, *cot_arrays, *extra)


def _small_fwd(name, fn, ins, outs):
    n = len(ins)

    def body(*refs):
        res = fn(*[r[...] for r in refs[:n]])
        for r, v in zip(refs[n:], res):
            r[...] = v.astype(r.dtype)

    return pl.pallas_call(
        body, name=name,
        out_shape=[jax.ShapeDtypeStruct(s, d) for s, d in outs],
        compiler_params=_params(),
    )(*ins)


def _small_bwd(name, fn, ins, cots):
    n = len(ins)

    def body(*refs):
        _, vjp = jax.vjp(fn, *[r[...] for r in refs[:n]])
        g = vjp(tuple(r[...] for r in refs[n:n + len(cots)]))
        for r, v in zip(refs[n + len(cots):], g):
            r[...] = v

    return pl.pallas_call(
        body, name=name,
        out_shape=[jax.ShapeDtypeStruct(a.shape, F32) for a in ins],
        compiler_params=_params(),
    )(*ins, *cots)


def _rms(x, g):
    return x * lax.rsqrt(jnp.mean(x * x, axis=-1, keepdims=True) + RMS_EPS) * g


def _f_norm_in(x, g):
    return (_rms(x, g),)


def _f_mix_res(x, mixed, g):
    h1 = x + mixed
    return h1, _rms(h1, g)


def _f_ffn_act(a13):
    a1, a3 = a13[:, :FFN_HIDDEN], a13[:, FFN_HIDDEN:]
    return (jax.nn.silu(a1) * a3,)


def _f_ffn_res(h1, ffo, g):
    h2 = h1 + ffo
    return h2, _rms(h2, g)


def _f_loss(h2, gpre, pu, target, g):
    h3 = h2 + jax.nn.sigmoid(gpre) * pu
    y = _rms(h3, g)
    err = jnp.square(y - target)
    return (0.5 * jnp.sum(jnp.mean(err, axis=-1, keepdims=True), axis=0, keepdims=True),)


def _f_s5_post(ypre, u, d, glu_w, glu_b):
    z = jax.nn.gelu(ypre + u * d)
    return (z * jax.nn.sigmoid(_bdot(z, glu_w) + glu_b),)


def _softplus(x):
    return jnp.maximum(x, 0.0) + jnp.log(1.0 + jnp.exp(-jnp.abs(x)))


def _f_rw_pre(z, carry, shift_mu, w0, w2, a0, a2, g2, k_k, k_a):
    rw = RWKV_WIDTH
    first_row = lax.broadcasted_iota(jnp.int32, z.shape, 0) == 0
    prev = jnp.where(first_row, carry, _shift_down(z))
    zs = z + (prev - z) * shift_mu
    o1, o2 = 3 * rw + DECAY_LORA, 3 * rw + DECAY_LORA + AAA_LORA
    r, k, v = zs[:, :rw], zs[:, rw:2 * rw], zs[:, 2 * rw:3 * rw]
    wl, al, gl = zs[:, 3 * rw:o1], zs[:, o1:o2], zs[:, o2:]
    w = -_softplus(-(w0 + _bdot(jnp.tanh(wl), w2))) - 0.5
    log_decay = -jnp.exp(w)
    a = jax.nn.sigmoid(a0 + _bdot(al, a2))
    g = _bdot(jax.nn.sigmoid(gl), g2)
    kk = k * k_k
    norm = jnp.sqrt(_fdot(kk * kk, _head_ones()))
    kk = kk / jnp.maximum(norm, L2_EPS)
    kp = k * (1.0 + (a - 1.0) * k_a)
    return r, log_decay, kp, v, -kk, kk * a, g


def _f_rw_post(y, r, kp, v, g, ln_w, ln_b, r_k):
    ones = _head_ones()
    yc = y - _fdot(y, ones) * (1.0 / HEAD)
    var = _fdot(yc * yc, ones) * (1.0 / HEAD)
    yn = yc * lax.rsqrt(var + GN_EPS) * ln_w + ln_b
    bonus = _fdot(r * kp * r_k, ones) * v
    return ((yn + bonus) * g,)


def _f_s5_lam(lam_re, lam_im, log_step):
    step = jnp.exp(log_step)
    dr, di = lam_re * step, lam_im * step
    e = jnp.exp(dr)
    lbr, lbi = e * jnp.cos(di), e * jnp.sin(di)
    nr, ni = lbr - 1.0, lbi
    den = lam_re * lam_re + lam_im * lam_im
    return lbr, lbi, (nr * lam_re + ni * lam_im) / den, (ni * lam_re - nr * lam_im) / den


def _f_s5_build(coef_r, coef_i, btr, bti, ctr, cti):
    bbr = coef_r * btr - coef_i * bti
    bbi = coef_r * bti + coef_i * btr
    shape = (S5_WIDTH, S5_TILE_LANES)
    rows = (lax.broadcasted_iota(jnp.int32, shape, 0) % S5_TILE_CH) // S5_GROUP
    cols = lax.broadcasted_iota(jnp.int32, shape, 1) // S5_STATE
    mask = (rows == cols).astype(F32)

    def blocks(m):
        per_tile = [m[:, S5_TILE_LANES * i:S5_TILE_LANES * (i + 1)] for i in range(S5_TILES)]
        return jnp.concatenate([t for t in per_tile for _ in range(S5_TILE_GROUPS)], axis=0) * mask

    return (jnp.concatenate([blocks(bbr), blocks(bbi)], axis=1),
            jnp.concatenate([blocks(ctr), -blocks(cti)], axis=1))


HALO = 8


def _rw_pre_specs(z, params, tm, order):
    halo_blocks = tm // HALO
    return ([pl.BlockSpec((tm, z.shape[1]), lambda i: (order(i), 0)),
             pl.BlockSpec((HALO, z.shape[1]), lambda i: (jnp.maximum(order(i) * halo_blocks - 1, 0), 0))]
            + [_full_spec(p) for p in params])


def _rw_pre_fwd(z, params, tm):
    t = z.shape[0]
    npar = len(params)

    def body(z_ref, halo_ref, *refs):
        carry = jnp.where(pl.program_id(0) == 0, 0.0, halo_ref[pl.ds(HALO - 1, 1), :])
        res = _f_rw_pre(z_ref[...], carry, *[r[...].astype(F32) for r in refs[:npar]])
        for r, v in zip(refs[npar:], res):
            r[...] = v

    return pl.pallas_call(
        body, name="rw_pre",
        out_shape=[jax.ShapeDtypeStruct((t, RWKV_WIDTH), F32)] * 7,
        grid=(t // tm,),
        in_specs=_rw_pre_specs(z, params, tm, lambda i: i),
        out_specs=[pl.BlockSpec((tm, RWKV_WIDTH), lambda i: (i, 0))] * 7,
        compiler_params=_params(("parallel",)),
    )(z, z, *params)


def _rw_pre_bwd(z, params, cots, tm):
    t = z.shape[0]
    nt = t // tm
    npar = len(params)
    order = lambda i: nt - 1 - i
    flat_cots = [a for group in cots for a in group]
    ncot = len(flat_cots)

    def body(z_ref, halo_ref, *refs):
        pin, cin = refs[:npar], list(refs[npar:npar + ncot])
        dz_ref = refs[npar + ncot]
        dpo = refs[npar + ncot + 1:npar + ncot + 1 + npar]
        dcarry_ref = refs[npar + ncot + 1 + npar]
        i = pl.program_id(0)

        @pl.when(i == 0)
        def _():
            dcarry_ref[...] = jnp.zeros_like(dcarry_ref)
            for r in dpo:
                r[...] = jnp.zeros_like(r)

        carry = jnp.where(i == nt - 1, 0.0, halo_ref[pl.ds(HALO - 1, 1), :])
        _, vjp = jax.vjp(_f_rw_pre, z_ref[...], carry, *[r[...].astype(F32) for r in pin])
        g = vjp(tuple(sum(cin.pop(0)[...] for _ in group) for group in cots))
        last_row = lax.broadcasted_iota(jnp.int32, z_ref.shape, 0) == tm - 1
        dz_ref[...] = g[0] + jnp.where(last_row, dcarry_ref[...], 0.0)
        dcarry_ref[...] = g[1]
        for r, v in zip(dpo, g[2:]):
            r[...] += v

    tok = lambda w: pl.BlockSpec((tm, w), lambda i: (order(i), 0))
    return pl.pallas_call(
        body, name="rw_pre_bwd",
        out_shape=[jax.ShapeDtypeStruct(z.shape, F32)] + [jax.ShapeDtypeStruct(p.shape, F32) for p in params],
        grid=(nt,),
        in_specs=_rw_pre_specs(z, params, tm, order) + [tok(RWKV_WIDTH)] * ncot,
        out_specs=[tok(z.shape[1])] + [_full_spec(p) for p in params],
        scratch_shapes=[pltpu.VMEM((1, z.shape[1]), F32)],
        compiler_params=_params(("arbitrary",)),
    )(z, z, *params, *flat_cots)


SUBLANES = 8


def _cmul(ar, ai, br, bi):
    return ar * br - ai * bi, ar * bi + ai * br


def _scan_rows(vr, vi, cr, ci, lr, li, reverse):
    n = SUBLANES
    row = lax.broadcasted_iota(jnp.int32, vr.shape, 0)
    dist = (n - 1 - row) if reverse else row
    xr, xi = vr, vi
    pr, pi_ = jnp.broadcast_to(lr, vr.shape), jnp.broadcast_to(li, vr.shape)
    sr, si = lr, li
    for s in (1, 2, 4):
        shift = (n - s) if reverse else s
        keep = dist >= s
        gr, gi = _cmul(sr, si, pltpu.roll(xr, shift, 0), pltpu.roll(xi, shift, 0))
        xr, xi = xr + jnp.where(keep, gr, 0.0), xi + jnp.where(keep, gi, 0.0)
        qr, qi = _cmul(pr, pi_, sr, si)
        bit = (dist & s) != 0
        pr, pi_ = jnp.where(bit, qr, pr), jnp.where(bit, qi, pi_)
        sr, si = _cmul(sr, si, sr, si)
    gr, gi = _cmul(pr, pi_, cr, ci)
    return xr + gr, xi + gi


def _s5_scan(bu, lam, tm, plan):
    t, w = bu.shape
    h = w // 2
    nt = t // tm
    parts, plan_in_specs, plan_out_shape, plan_out_specs, plan_sems = _carry(plan, 2, 2)

    def body(*refs):
        bu_ref, lam_ref = refs[:2]
        x_ref, xp_ref = refs[2 + len(plan.ins):4 + len(plan.ins)]
        carry_ref = refs[4 + len(plan.ins) + len(plan.out_shape)]

        @pl.when(pl.program_id(0) == 0)
        def _():
            carry_ref[...] = jnp.zeros_like(carry_ref)
            plan.start(*parts(refs))

        lr, li = lam_ref[:, :h], lam_ref[:, h:]

        first_row = lax.broadcasted_iota(jnp.int32, (SUBLANES, h), 0) == 0

        def step(s, c):
            cr, ci = c
            rows = pl.ds(pl.multiple_of(s * SUBLANES, SUBLANES), SUBLANES)
            xr, xi = _scan_rows(bu_ref[rows, :h], bu_ref[rows, h:], cr, ci, lr, li, False)
            x_ref[rows, :h] = xr
            x_ref[rows, h:] = xi
            xp_ref[rows, :h] = jnp.where(first_row, cr, pltpu.roll(xr, 1, 0))
            xp_ref[rows, h:] = jnp.where(first_row, ci, pltpu.roll(xi, 1, 0))
            last = pl.ds(s * SUBLANES + SUBLANES - 1, 1)
            return x_ref[last, :h], x_ref[last, h:]

        cr, ci = lax.fori_loop(0, tm // SUBLANES, step, (carry_ref[:, :h], carry_ref[:, h:]))
        carry_ref[:, :h] = cr
        carry_ref[:, h:] = ci

        @pl.when(pl.program_id(0) == nt - 1)
        def _():
            plan.wait(*parts(refs))

    spec = pl.BlockSpec((tm, w), lambda i: (i, 0))
    res = pl.pallas_call(
        body, name="s5_scan",
        out_shape=[jax.ShapeDtypeStruct((t, w), F32)] * 2 + plan_out_shape,
        grid=(nt,),
        in_specs=[spec, pl.BlockSpec((1, w), lambda i: (0, 0))] + plan_in_specs,
        out_specs=[spec, spec] + plan_out_specs,
        scratch_shapes=[pltpu.VMEM((1, w), F32)] + plan_sems,
        compiler_params=_params(("arbitrary",)),
    )(bu, lam, *plan.ins)
    return res[0], res[1], res[2:]


def _s5_scan_bwd(dx, xp, lam, tm):
    t, w = dx.shape
    h = w // 2
    nt = t // tm

    def body(dx_ref, xp_ref, lam_ref, dbu_ref, dlam_ref, carry_ref):
        @pl.when(pl.program_id(0) == 0)
        def _():
            carry_ref[...] = jnp.zeros_like(carry_ref)
            dlam_ref[...] = jnp.zeros_like(dlam_ref)

        lr, li = lam_ref[:, :h], lam_ref[:, h:]

        groups = tm // SUBLANES

        def step(s, c):
            cr, ci = c
            start = pl.multiple_of((groups - 1 - s) * SUBLANES, SUBLANES)
            rows = pl.ds(start, SUBLANES)
            gr, gi = _scan_rows(dx_ref[rows, :h], dx_ref[rows, h:], cr, ci, lr, -li, True)
            dbu_ref[rows, :h] = gr
            dbu_ref[rows, h:] = gi
            return dbu_ref[pl.ds(start, 1), :h], dbu_ref[pl.ds(start, 1), h:]

        cr, ci = lax.fori_loop(0, groups, step, (carry_ref[:, :h], carry_ref[:, h:]))
        carry_ref[:, :h] = cr
        carry_ref[:, h:] = ci
        gr, gi = dbu_ref[:, :h], dbu_ref[:, h:]
        pr, pi_ = xp_ref[:, :h], xp_ref[:, h:]
        dlam_ref[:, :h] += jnp.sum(gr * pr + gi * pi_, axis=0, keepdims=True)
        dlam_ref[:, h:] += jnp.sum(gi * pr - gr * pi_, axis=0, keepdims=True)

    spec = pl.BlockSpec((tm, w), lambda i: (nt - 1 - i, 0))
    row_spec = pl.BlockSpec((1, w), lambda i: (0, 0))
    return pl.pallas_call(
        body, name="s5_scan_bwd",
        out_shape=[jax.ShapeDtypeStruct((t, w), F32), jax.ShapeDtypeStruct((1, w), F32)],
        grid=(nt,),
        in_specs=[spec, spec, row_spec],
        out_specs=[spec, row_spec],
        scratch_shapes=[pltpu.VMEM((1, w), F32)],
        compiler_params=_params(("arbitrary",)),
    )(dx, xp, lam)


def _wkv_chunks(s0, r, lw, k, v, a, b):
    c = r[0].shape[0]
    row = lax.broadcasted_iota(jnp.int32, (c, c), 0)
    col = lax.broadcasted_iota(jnp.int32, (c, c), 1)
    incl, strict = col <= row, col < row
    tri = incl.astype(F32)
    eye = (row == col).astype(F32)
    each = lambda f, *xs: [f(*t) for t in zip(*xs)]
    stack = lambda p, q: jnp.concatenate([p, q], axis=0)
    dot = lambda p, q, dims=_NN: _dot(p, q, dims, WKV_PASSES)
    lc = each(lambda l: _dot(tri, l, _NN, 2), lw)
    e_neg = each(lambda l: jnp.exp(-l), lc)
    ar = each(lambda x, z, l, w: stack(x * jnp.exp(l - w), z * jnp.exp(l)), a, r, lc, lw)
    bk = each(lambda x, z, e: stack(x * e, z * e), b, k, e_neg)
    m = each(lambda p, q: dot(p, q, _NT), ar, bk)
    mab = each(lambda q: jnp.where(strict, q[:c, :c], 0.0), m)
    mak_mrk = each(lambda q: stack(jnp.where(strict, q[:c, c:], 0.0), jnp.where(incl, q[c:, c:], 0.0)), m)
    mrb = each(lambda q: jnp.where(incl, q[c:, :c], 0.0), m)
    xy = each(lambda p, s, q, z: dot(p, s, _NT) + dot(q, z), ar, s0, mak_mrk, v)
    inv = each(lambda q: eye + q, mab)
    pw = each(lambda q: dot(q, q), mab)
    for _ in range(int(math.log2(c)) - 2):
        both = each(lambda i, q: dot(stack(i, q), q), inv, pw)
        inv = each(lambda i, q: i + q[:c], inv, both)
        pw = each(lambda q: q[c:], both)
    inv = each(lambda i, q: i + dot(i, q), inv, pw)
    u = each(lambda i, q: dot(i, q[:c]), inv, xy)
    y = each(lambda q, z, p: q[c:] + dot(z, p), xy, mrb, u)
    e_tot = each(lambda l: jnp.exp(jnp.sum(l, axis=0, keepdims=True)), lw)
    s1 = each(lambda s, p, z, q, e: (s + dot(stack(p, z), q, _TN)) * e, s0, u, v, bk, e_tot)
    return y, s1


def _carry(plan, n_args, n_outs):
    n_in, n_out = len(plan.ins), len(plan.out_shape)

    def parts(refs):
        base = n_args + n_in + n_outs
        return refs[n_args:n_args + n_in], refs[base:base + n_out], refs[base + n_out + 1:]

    return parts, [HBM_SPEC] * n_in, list(plan.out_shape), [HBM_SPEC] * n_out, list(plan.sems)


def _head_cols(ref):
    return tuple(ref[:, h * HEAD:(h + 1) * HEAD] for h in range(HEADS))


def _wkv_fwd(seqs, plan):
    t, w = seqs[0].shape
    c, n = WKV_CHUNK, HEAD
    nc = t // c
    parts, plan_in_specs, plan_out_shape, plan_out_specs, plan_sems = _carry(plan, 6, 2)

    def body(*refs):
        ins, (y_ref, ck_ref) = refs[:6], refs[6 + len(plan.ins):8 + len(plan.ins)]
        s_ref = refs[8 + len(plan.ins) + len(plan.out_shape)]

        @pl.when(pl.program_id(0) == 0)
        def _():
            s_ref[...] = jnp.zeros_like(s_ref)
            plan.start(*parts(refs))

        s0 = tuple(s_ref[h] for h in range(HEADS))
        ys, s1 = _wkv_chunks(s0, *[_head_cols(r) for r in ins])
        for h in range(HEADS):
            ck_ref[0, h] = s0[h]
            y_ref[:, h * n:(h + 1) * n] = ys[h]
            s_ref[h] = s1[h]

        @pl.when(pl.program_id(0) == nc - 1)
        def _():
            plan.wait(*parts(refs))

    spec = pl.BlockSpec((c, w), lambda i: (i, 0))
    res = pl.pallas_call(
        body, name="wkv_fwd",
        out_shape=[jax.ShapeDtypeStruct((t, w), F32), jax.ShapeDtypeStruct((nc, HEADS, n, n), F32)] + plan_out_shape,
        grid=(nc,),
        in_specs=[spec] * 6 + plan_in_specs,
        out_specs=[spec, pl.BlockSpec((1, HEADS, n, n), lambda i: (i, 0, 0, 0))] + plan_out_specs,
        scratch_shapes=[pltpu.VMEM((HEADS, n, n), F32)] + plan_sems,
        compiler_params=_params(("arbitrary",)),
    )(*seqs, *plan.ins)
    return res[0], res[1], res[2:]


def _wkv_bwd(seqs, ck, dy, plan):
    t, w = seqs[0].shape
    c, n = WKV_CHUNK, HEAD
    nc = t // c
    parts, plan_in_specs, plan_out_shape, plan_out_specs, plan_sems = _carry(plan, 8, 6)

    def body(*refs):
        ins, ck_ref, dy_ref = refs[:6], refs[6], refs[7]
        outs = refs[8 + len(plan.ins):14 + len(plan.ins)]
        ds_ref = refs[14 + len(plan.ins) + len(plan.out_shape)]

        @pl.when(pl.program_id(0) == 0)
        def _():
            ds_ref[...] = jnp.zeros_like(ds_ref)
            plan.start(*parts(refs))

        s0 = tuple(ck_ref[0, h] for h in range(HEADS))
        _, vjp = jax.vjp(_wkv_chunks, s0, *[_head_cols(r) for r in ins])
        g = vjp((list(_head_cols(dy_ref)), [ds_ref[h] for h in range(HEADS)]))
        for h in range(HEADS):
            ds_ref[h] = g[0][h]
            for o, d in zip(outs, g[1:]):
                o[:, h * n:(h + 1) * n] = d[h]

        @pl.when(pl.program_id(0) == nc - 1)
        def _():
            plan.wait(*parts(refs))

    spec = pl.BlockSpec((c, w), lambda i: (nc - 1 - i, 0))
    res = pl.pallas_call(
        body, name="wkv_bwd",
        out_shape=[jax.ShapeDtypeStruct((t, w), F32)] * 6 + plan_out_shape,
        grid=(nc,),
        in_specs=[spec] * 6 + [pl.BlockSpec((1, HEADS, n, n), lambda i: (nc - 1 - i, 0, 0, 0)), spec] + plan_in_specs,
        out_specs=[spec] * 6 + plan_out_specs,
        scratch_shapes=[pltpu.VMEM((HEADS, n, n), F32)] + plan_sems,
        compiler_params=_params(("arbitrary",)),
    )(*seqs, ck, dy, *plan.ins)
    return res[:6], res[6:]


def _coords():
    return lax.axis_index("x"), lax.axis_index("y"), lax.axis_index("c")


def _flip(v, f):
    return 1 - v if f else v


_CHIP_FLIPS = [(1, 0), (0, 1), (1, 1)]
_DEV_FLIPS = [(fx, fy, fc) for fx in (0, 1) for fy in (0, 1) for fc in (0, 1) if (fx, fy, fc) != (0, 0, 0)]
HBM_SPEC = pl.BlockSpec(memory_space=pl.ANY)


def _chip_peer(k, x, y):
    fx, fy = _CHIP_FLIPS[k]
    return _flip(x, fx), _flip(y, fy)


def _dev_peer(k, x, y, c):
    fx, fy, fc = _DEV_FLIPS[k]
    return _flip(x, fx), _flip(y, fy), _flip(c, fc)


def _rows_of_core(ref, core):
    h = ref.shape[-2] // 2
    rows = pl.ds(pl.multiple_of(core * h, 8), h)
    return ref.at[rows, :] if len(ref.shape) == 2 else ref.at[:, rows, :]


class _Plan(NamedTuple):
    ins: Sequence[Any]
    out_shape: Sequence[Any]
    sems: Sequence[Any]
    start: Callable
    wait: Callable


_NO_PLAN = _Plan([], [], [], lambda *_: None, lambda *_: None)


def _run_plan(name, plan):
    n_in, n_out = len(plan.ins), len(plan.out_shape)

    def body(*refs):
        parts = refs[:n_in], refs[n_in:n_in + n_out], refs[n_in + n_out:]
        plan.start(*parts)
        plan.wait(*parts)

    return pl.pallas_call(
        body, name=name, out_shape=list(plan.out_shape),
        in_specs=[HBM_SPEC] * n_in, out_specs=[HBM_SPEC] * n_out, scratch_shapes=list(plan.sems),
    )(*plan.ins)


def _gather_plan(shards):
    n = len(shards)

    def copies(srcs, outs, sems):
        send_sems, recv_sems, local_sems = sems
        x, y, c = _coords()
        me = 2 * x + y

        def remote(i, k, arriving):
            px, py = _chip_peer(k, x, y)
            return pltpu.make_async_remote_copy(
                src_ref=srcs[i], dst_ref=outs[i].at[2 * px + py if arriving else me],
                send_sem=send_sems.at[i, k], recv_sem=recv_sems.at[i, k],
                device_id=(px, py, c), device_id_type=MESH)

        own = [pltpu.make_async_copy(srcs[i], outs[i].at[me], local_sems.at[i]) for i in range(n)]
        pairs = [(i, k) for k in range(3) for i in range(n)]
        return own, [remote(i, k, False) for i, k in pairs], [remote(i, k, True) for i, k in pairs]

    return _Plan(
        ins=shards, out_shape=[jax.ShapeDtypeStruct((4,) + s.shape, s.dtype) for s in shards],
        sems=[pltpu.SemaphoreType.DMA((n, 3)), pltpu.SemaphoreType.DMA((n, 3)), pltpu.SemaphoreType.DMA((n,))],
        start=functools.partial(_start_copies, copies), wait=functools.partial(_wait_copies, copies))


def _start_copies(copies, ins, outs, sems):
    own, sends, _ = copies(ins, outs, sems)
    for cp in own + sends:
        cp.start()


def _wait_copies(copies, ins, outs, sems):
    own, sends, arrivals = copies(ins, outs, sems)
    for cp in arrivals:
        cp.wait_recv()
    for cp in sends:
        cp.wait_send()
    for cp in own:
        cp.wait()


def _exchange_plan(gs, small=None):
    n = len(gs)
    arrays = list(gs) + ([] if small is None else [small])

    def copies(srcs, outs, sems):
        send_sems, recv_sems, local_sems = sems
        x, y, c = _coords()
        me = 4 * x + 2 * y + c

        def piece(i, px, py, pc):
            if i == n:
                return srcs[i].at[4 * px + 2 * py + pc]
            return _rows_of_core(srcs[i].at[2 * px + py], pc)

        def remote(i, k, arriving):
            px, py, pc = _dev_peer(k, x, y, c)
            return pltpu.make_async_remote_copy(
                src_ref=piece(i, px, py, pc), dst_ref=outs[i].at[4 * px + 2 * py + pc if arriving else me],
                send_sem=send_sems.at[i, k], recv_sem=recv_sems.at[i, k],
                device_id=(px, py, pc), device_id_type=MESH)

        own = [pltpu.make_async_copy(piece(i, x, y, c), outs[i].at[me], local_sems.at[i]) for i in range(len(arrays))]
        pairs = [(i, k) for k in range(7) for i in range(len(arrays))]
        return own, [remote(i, k, False) for i, k in pairs], [remote(i, k, True) for i, k in pairs]

    out_shape = [jax.ShapeDtypeStruct((8, g.shape[1] // 2, g.shape[2]), g.dtype) for g in gs]
    if small is not None:
        out_shape.append(jax.ShapeDtypeStruct(small.shape, small.dtype))
    m = len(arrays)
    return _Plan(
        ins=arrays, out_shape=out_shape,
        sems=[pltpu.SemaphoreType.DMA((m, 7)), pltpu.SemaphoreType.DMA((m, 7)), pltpu.SemaphoreType.DMA((m,))],
        start=functools.partial(_start_copies, copies), wait=functools.partial(_wait_copies, copies))


def _share_cores(halves, small):
    n = len(halves)

    def body(*refs):
        srcs, small_src, outs, small_out = refs[:n], refs[n], refs[n + 1:2 * n + 1], refs[2 * n + 1]
        mine, theirs = refs[2 * n + 2:3 * n + 2], refs[3 * n + 2:4 * n + 2]
        send_sems, recv_sems, ssend, srecv, local_sems = refs[4 * n + 2:]
        x, y, c = _coords()
        me = 4 * x + 2 * y + c

        def big(i):
            return pltpu.make_async_remote_copy(
                src_ref=mine[i], dst_ref=theirs[i], send_sem=send_sems.at[i], recv_sem=recv_sems.at[i],
                device_id=(x, y, 1 - c), device_id_type=MESH)

        def tiny(k, arriving):
            px, py, pc = _dev_peer(k, x, y, c)
            return pltpu.make_async_remote_copy(
                src_ref=small_src, dst_ref=small_out.at[4 * px + 2 * py + pc if arriving else me],
                send_sem=ssend.at[k], recv_sem=srecv.at[k], device_id=(px, py, pc), device_id_type=MESH)

        small_sends = [tiny(k, False) for k in range(7)]
        own_small = pltpu.make_async_copy(small_src, small_out.at[me], local_sems.at[2 * n])
        stage = [pltpu.make_async_copy(srcs[i], mine[i], local_sems.at[i]) for i in range(n)]
        for cp in small_sends + [own_small] + stage:
            cp.start()
        sends = []
        for i in range(n):
            stage[i].wait()
            sends.append(big(i))
            sends[-1].start()
        store = [pltpu.make_async_copy(mine[i], outs[i].at[c], local_sems.at[i]) for i in range(n)]
        for cp in store:
            cp.start()
        for i in range(n):
            big(i).wait_recv()
            store.append(pltpu.make_async_copy(theirs[i], outs[i].at[1 - c], local_sems.at[n + i]))
            store[-1].start()
        for k in range(7):
            tiny(k, True).wait_recv()
        for cp in sends + small_sends:
            cp.wait_send()
        for cp in store + [own_small]:
            cp.wait()

    staged = [pltpu.VMEM(s.shape, s.dtype) for s in halves]
    res = pl.pallas_call(
        body, name="share_cores",
        out_shape=[jax.ShapeDtypeStruct((2,) + s.shape, s.dtype) for s in halves]
        + [jax.ShapeDtypeStruct((8,) + small.shape, small.dtype)],
        in_specs=[HBM_SPEC] * (n + 1), out_specs=[HBM_SPEC] * (n + 1),
        scratch_shapes=staged + staged + [
            pltpu.SemaphoreType.DMA((n,)), pltpu.SemaphoreType.DMA((n,)),
            pltpu.SemaphoreType.DMA((7,)), pltpu.SemaphoreType.DMA((7,)),
            pltpu.SemaphoreType.DMA((2 * n + 1,))],
        compiler_params=pltpu.CompilerParams(vmem_limit_bytes=VMEM_LIMIT_BYTES),
    )(*halves, small)
    return res[:n], res[n]


BF16_SUBLANES = 16


def _row_tile(n, target, step=BF16_SUBLANES):
    return max([d for d in range(step, min(n, target) + 1, step) if n % d == 0] or [n])


def _ew(name, fn, ins, outs, block_bytes=1 << 20):
    rows, cols = ins[0].shape[-2:]
    lead = max(math.prod(a.shape[:-2]) for a in ins)
    tr = _row_tile(rows, max(8, block_bytes // (4 * cols * lead)))
    n = len(ins)

    def spec(shape):
        if len(shape) == 2:
            return pl.BlockSpec((tr, cols), lambda i: (i, 0))
        return pl.BlockSpec((shape[0], tr, cols), lambda i: (0, i, 0))

    def body(*refs):
        res = fn(*[r[...] for r in refs[:n]])
        for r, v in zip(refs[n:], res):
            r[...] = v

    return pl.pallas_call(
        body, name=name,
        out_shape=[jax.ShapeDtypeStruct(s, F32) for s in outs],
        grid=(rows // tr,),
        in_specs=[spec(a.shape) for a in ins],
        out_specs=[spec(s) for s in outs],
        compiler_params=_params(("parallel",)),
    )(*ins)


def _sum_slots(a):
    total = a[0].astype(F32)
    for s in range(1, a.shape[0]):
        total = total + a[s].astype(F32)
    return (total,)


def _adamw(g, w, m, v):
    bc1 = 1.0 - ADAM_B1 ** ADAM_STEP
    bc2 = 1.0 - ADAM_B2 ** ADAM_STEP
    m_new = ADAM_B1 * m + (1.0 - ADAM_B1) * g
    v_new = ADAM_B2 * v + (1.0 - ADAM_B2) * jnp.square(g)
    delta = -ADAM_LR * ((m_new / bc1) / (jnp.sqrt(v_new / bc2) + ADAM_EPS) + ADAM_WD * w)
    return delta, m_new, v_new


def _mat(a):
    return a.reshape(a.shape[-2:])


def _to_shard_major(full, axis):
    rows, cols = full.shape
    if axis == 0:
        return full.reshape(4, rows // 4, cols)
    return full.reshape(rows, 4, cols // 4).transpose(1, 0, 2)


def _from_shard_major(a, axis):
    _, r, cs = a.shape
    if axis == 0:
        return a.reshape(4 * r, cs)
    return a.transpose(1, 0, 2).reshape(r, 4 * cs)


def _pack_small(arrays):
    flat = jnp.concatenate([arrays[n].reshape(-1) for n in SMALL_NAMES])
    flat = jnp.pad(flat, (0, 8 * SMALL_ROWS * PACK_COLS - flat.shape[0]))
    return flat.reshape(8, SMALL_ROWS, PACK_COLS)


def _unpack_small(packed, shapes):
    flat = packed.reshape(-1)
    out, off = {}, 0
    for n in SMALL_NAMES:
        size = math.prod(shapes[n])
        out[n] = flat[off:off + size].reshape(shapes[n])
        off += size
    return out


def _row(a):
    return a.reshape(1, -1)


def _local_step(x, p, target, wf, ws, late_shards):
    wf = dict(wf)
    t = x.shape[0]
    tm = min(256, t)
    g = {}

    lam_re, lam_im = ws['s5_lam_re'].reshape(S5_GROUPS, S5_STATE), ws['s5_lam_im'].reshape(S5_GROUPS, S5_STATE)
    log_step = ws['s5_log_step'].reshape(S5_GROUPS, 1)
    gp = (S5_GROUPS, S5_STATE)
    lam_ins = (lam_re, lam_im, log_step)
    lbr, lbi, cfr, cfi = _small_fwd("s5_lam", _f_s5_lam, lam_ins, [(gp, F32)] * 4)
    lam_row = jnp.concatenate([_row(lbr), _row(lbi)], axis=1)
    to_t = lambda a, perm: a.reshape((S5_GROUPS,) + a.shape[-2:]).transpose(perm).reshape(S5_GROUP, S5_LANES)
    build_ins = (_row(cfr), _row(cfi), to_t(ws['s5_b_re'], (2, 0, 1)), to_t(ws['s5_b_im'], (2, 0, 1)),
                 to_t(ws['s5_c_re'], (1, 0, 2)), to_t(ws['s5_c_im'], (1, 0, 2)))
    block_shape = (S5_WIDTH, 2 * S5_TILE_LANES)
    b_blk, c_blk = _small_fwd("s5_build", _f_s5_build, build_ins, [(block_shape, F32)] * 2)

    norm_mix, norm_ffn, norm_ple = _row(ws['norm_mix']), _row(ws['norm_ffn']), _row(ws['norm_ple'])
    final_norm = _row(ws['final_norm'])
    (xn,) = _tok_fwd("norm_in", _f_norm_in, [x], [norm_mix], [(x.shape[1], BF16)], tm)
    u = _mm("proj_s5", xn, wf['w_in'][:, :S5_WIDTH], 'nn')
    z = _mm("proj_rw", xn, wf['w_in'][:, S5_WIDTH:], 'nn')

    bu = _s5_expand("s5_bu", u, b_blk)
    def late_plan(carrier):
        return _gather_plan([late_shards[n] for n in LATE_GATHER[carrier]])

    def arrived(carrier, got):
        wf.update({n: _from_shard_major(a, SHARDED[n]) for n, a in zip(LATE_GATHER[carrier], got)})

    xs, xs_prev, got = _s5_scan(bu, lam_row, tm, late_plan('s5_scan'))
    arrived('s5_scan', got)
    ypre = _s5_contract("s5_y", xs, c_blk)
    s5_par = [_row(ws['s5_d']), wf['s5_glu_w'], _row(ws['s5_glu_b'])]
    (s5_out,) = _tok_fwd("s5_post", _f_s5_post, [ypre, u], s5_par, [(S5_WIDTH, BF16)], tm)

    pre_par = [_row(ws['rw_shift_mu']), _row(ws['rw_w0']), wf['rw_w2'], _row(ws['rw_a0']), wf['rw_a2'],
               wf['rw_g2'], _row(ws['rw_k_k']), _row(ws['rw_k_a'])]
    r, lw, kp, v, an, bn, gate = _rw_pre_fwd(z, pre_par, tm)
    seqs = [r, lw, kp, v, an, bn]
    y_wkv, ck, got = _wkv_fwd(seqs, late_plan('wkv_fwd'))
    arrived('wkv_fwd', got)
    post_par = [_row(ws['rw_ln_w']), _row(ws['rw_ln_b']), _row(ws['rw_r_k'])]
    post_toks = [y_wkv, r, kp, v, gate]
    (rw_out,) = _tok_fwd("rw_post", _f_rw_post, post_toks, post_par, [(RWKV_WIDTH, BF16)], tm)

    mixcat = jnp.concatenate([s5_out, rw_out], axis=1)
    mixed = _mm("mix_out", mixcat, wf['w_out'], 'nn')
    h1, hn = _tok_fwd("mix_res", _f_mix_res, [x, mixed], [norm_ffn], [(x.shape[1], F32), (x.shape[1], BF16)], tm)
    w13 = jnp.concatenate([wf['ffn_w1'], wf['ffn_w3']], axis=1)
    a13, got = _mm("ffn_up", hn, w13, 'nn', plan=late_plan('ffn_up'))
    arrived('ffn_up', got)
    (f,) = _tok_fwd("ffn_act", _f_ffn_act, [a13], [], [(FFN_HIDDEN, BF16)], tm)
    ffo = _mm("ffn_down", f, wf['ffn_w2'], 'nn')
    h2, hp = _tok_fwd("ffn_res", _f_ffn_res, [h1, ffo], [norm_ple], [(x.shape[1], F32), (x.shape[1], BF16)], tm)
    gpre = _mm("ple_gate", hp, wf['ple_gate_w'], 'nn')
    pu = _mm("ple_up", p, wf['ple_up_w'], 'nn')

    dh2, dgpre, dpu, g['final_norm'], loss = _tok_bwd(
        "loss", _f_loss, [h2, gpre, pu, target], [final_norm], [None],
        [F32, BF16, BF16, None], [True], tm, acc_out=0)
    g['ple_gate_w'] = _mm("d_ple_gate_w", hp, dgpre, 'tn', out_dtype=WIRE)
    g['ple_up_w'] = _mm("d_ple_up_w", p, dpu, 'tn', out_dtype=WIRE)
    dhp = _mm("d_hp", dgpre, wf['ple_gate_w'], 'nt')
    dh1, dffo, g['norm_ple'] = _tok_bwd("ffn_res_bwd", _f_ffn_res, [h1, ffo], [norm_ple], [dh2, dhp],
                                        [F32, BF16], [True], tm)
    g['ffn_w2'] = _mm("d_ffn_w2", f, dffo, 'tn', out_dtype=WIRE)
    df = _mm("d_f", dffo, wf['ffn_w2'], 'nt', out_dtype=BF16)
    (da13,) = _tok_bwd("ffn_act_bwd", _f_ffn_act, [a13], [], [df], [BF16], [], tm)
    dw13 = _mm("d_ffn_w13", hn, da13, 'tn', out_dtype=WIRE)
    g['ffn_w1'], g['ffn_w3'] = dw13[:, :FFN_HIDDEN], dw13[:, FFN_HIDDEN:]
    dhn = _mm("d_hn", da13, w13, 'nt')
    dx_a, dmixed, g['norm_ffn'] = _tok_bwd("mix_res_bwd", _f_mix_res, [x, mixed], [norm_ffn], [dh1, dhn],
                                           [F32, BF16], [True], tm)
    g['w_out'] = _mm("d_w_out", mixcat, dmixed, 'tn', out_dtype=WIRE)
    dmixcat = _mm("d_mixcat", dmixed, wf['w_out'], 'nt')
    ds5_out, drw_out = dmixcat[:, :S5_WIDTH], dmixcat[:, S5_WIDTH:]

    dy_wkv, dr_b, dkp_b, dv_b, dgate, g['rw_ln_w'], g['rw_ln_b'], g['rw_r_k'] = _tok_bwd(
        "rw_post_bwd", _f_rw_post, post_toks, post_par, [drw_out], [F32] * 5, [True] * 3, tm)
    late_exchange = _exchange_plan([_to_shard_major(g[n], SHARDED[n]).astype(WIRE) for n in LATE_NAMES])
    dseqs, late_pieces = _wkv_bwd(seqs, ck, dy_wkv, late_exchange)
    pre_cots = [(dseqs[0], dr_b), (dseqs[1],), (dseqs[2], dkp_b), (dseqs[3], dv_b), (dseqs[4],), (dseqs[5],),
                (dgate,)]
    dz, *dpre = _rw_pre_bwd(z, pre_par, pre_cots, tm)
    for n, d in zip(['rw_shift_mu', 'rw_w0', 'rw_w2', 'rw_a0', 'rw_a2', 'rw_g2', 'rw_k_k', 'rw_k_a'], dpre):
        g[n] = d

    dypre, du_a, g['s5_d'], g['s5_glu_w'], g['s5_glu_b'] = _tok_bwd(
        "s5_post_bwd", _f_s5_post, [ypre, u], s5_par, [ds5_out], [F32, F32], [True] * 3, tm)
    dxs = _s5_expand("d_s5_x", dypre, c_blk)
    dc_blk = _s5_block_grad("d_s5_c", dypre, xs)
    dbu, dlam_row = _s5_scan_bwd(dxs, xs_prev, lam_row, tm)
    du_b = _s5_contract("d_s5_u", dbu, b_blk)
    db_blk = _s5_block_grad("d_s5_b", u, dbu)
    dbuild = _small_bwd("s5_build_bwd", _f_s5_build, build_ins, (db_blk, dc_blk))
    lam_cots = (dlam_row[:, :S5_LANES].reshape(gp), dlam_row[:, S5_LANES:].reshape(gp),
                dbuild[0].reshape(gp), dbuild[1].reshape(gp))
    g['s5_lam_re'], g['s5_lam_im'], g['s5_log_step'] = _small_bwd("s5_lam_bwd", _f_s5_lam, lam_ins, lam_cots)
    from_t = lambda a, perm: a.reshape(S5_GROUP, S5_GROUPS, S5_STATE).transpose(perm)
    g['s5_b_re'], g['s5_b_im'] = from_t(dbuild[2], (1, 2, 0)), from_t(dbuild[3], (1, 2, 0))
    g['s5_c_re'], g['s5_c_im'] = from_t(dbuild[4], (1, 0, 2)), from_t(dbuild[5], (1, 0, 2))

    dproj = jnp.concatenate([(du_a + du_b).astype(BF16), dz.astype(BF16)], axis=1)
    g['w_in'] = _mm("d_w_in", xn, dproj, 'tn', out_dtype=WIRE)
    dxn = _mm("d_xn", dproj, wf['w_in'], 'nt')
    grad_x, g['norm_mix'] = _tok_bwd("norm_in_bwd", _f_norm_in, [x], [norm_mix], [dxn], [F32], [True], tm,
                                     add_to=(0, dx_a))
    return loss[0, 0], grad_x, g, late_pieces


def _step(x, p, target, w, m, v):
    shards = {n: _mat(w[n]).astype(BF16) for n in SHARDED_NAMES}
    early = _run_plan("gather_early", _gather_plan([shards[n] for n in EARLY_NAMES]))
    wf = {n: _from_shard_major(a, SHARDED[n]) for n, a in zip(EARLY_NAMES, early)}
    ws = {n: w[n] for n in SMALL_NAMES}

    loss, grad_x, g, late_pieces = _local_step(x[0], p[0, 0], target[0], wf, ws, shards)

    early_plan = _exchange_plan([_to_shard_major(g[n], SHARDED[n]).astype(WIRE) for n in EARLY_NAMES],
                                _pack_small({n: g[n] for n in SMALL_NAMES}))
    *early_pieces, by_dev = _run_plan("exchange_early", early_plan)
    pieces = dict(zip(LATE_NAMES + EARLY_NAMES, list(late_pieces) + early_pieces))
    halves = [_ew("add_devices_" + n, _sum_slots, [pieces[n]], [pieces[n].shape[1:]])[0] for n in SHARDED_NAMES]
    (small_piece,) = _ew("add_devices_small", _sum_slots, [by_dev], [by_dev.shape[1:]])
    both, small_g = _share_cores(halves, small_piece)

    kinds = [{}, {}, {}, {}]
    for n, gn in zip(SHARDED_NAMES, both):
        shard = _mat(w[n]).shape
        res = _ew("adamw_" + n, _adamw, [gn.reshape(shard), _mat(w[n]), _mat(m[n]), _mat(v[n])], [shard] * 3)
        for kind, a in zip(kinds, [gn] + list(res)):
            kind[n] = a.reshape(w[n].shape)
    flat = (8 * SMALL_ROWS, PACK_COLS)
    packed = [_pack_small({n: d[n] for n in SMALL_NAMES}).reshape(flat) for d in (w, m, v)]
    small_res = _ew("adamw_small", _adamw, [small_g.reshape(flat)] + packed, [flat] * 3)
    small_shapes = {n: w[n].shape for n in SMALL_NAMES}
    for kind, a in zip(kinds, [small_g] + list(small_res)):
        kind.update(_unpack_small(a, small_shapes))
    total = lax.psum(loss, ("x", "y", "c"))
    return (total, grad_x[None], *[kind[n] for kind in kinds for n in WEIGHT_NAMES])


def kernel(x, p, norm_mix, w_in, s5_lam_re, s5_lam_im, s5_log_step, s5_b_re, s5_b_im, s5_c_re, s5_c_im, s5_d, s5_glu_w, s5_glu_b, rw_shift_mu, rw_w0, rw_w2, rw_a0, rw_a2, rw_g2, rw_k_k, rw_k_a, rw_r_k, rw_ln_w, rw_ln_b, w_out, norm_ffn, ffn_w1, ffn_w3, ffn_w2, norm_ple, ple_gate_w, ple_up_w, final_norm, loss_target, m_norm_mix, m_w_in, m_s5_lam_re, m_s5_lam_im, m_s5_log_step, m_s5_b_re, m_s5_b_im, m_s5_c_re, m_s5_c_im, m_s5_d, m_s5_glu_w, m_s5_glu_b, m_rw_shift_mu, m_rw_w0, m_rw_w2, m_rw_a0, m_rw_a2, m_rw_g2, m_rw_k_k, m_rw_k_a, m_rw_r_k, m_rw_ln_w, m_rw_ln_b, m_w_out, m_norm_ffn, m_ffn_w1, m_ffn_w3, m_ffn_w2, m_norm_ple, m_ple_gate_w, m_ple_up_w, m_final_norm, v_norm_mix, v_w_in, v_s5_lam_re, v_s5_lam_im, v_s5_log_step, v_s5_b_re, v_s5_b_im, v_s5_c_re, v_s5_c_im, v_s5_d, v_s5_glu_w, v_s5_glu_b, v_rw_shift_mu, v_rw_w0, v_rw_w2, v_rw_a0, v_rw_a2, v_rw_g2, v_rw_k_k, v_rw_k_a, v_rw_r_k, v_rw_ln_w, v_rw_ln_b, v_w_out, v_norm_ffn, v_ffn_w1, v_ffn_w3, v_ffn_w2, v_norm_ple, v_ple_gate_w, v_ple_up_w, v_final_norm):
    args = dict(locals())
    w = {n: args[n] for n in WEIGHT_NAMES}
    m = {n: args["m_" + n] for n in WEIGHT_NAMES}
    v = {n: args["v_" + n] for n in WEIGHT_NAMES}
    return _step(x, p, loss_target, w, m, v)
```

```python
import functools
import math
from typing import Any, Callable, NamedTuple, Sequence

import jax
import jax.numpy as jnp
from jax import lax
from jax.experimental import pallas as pl
from jax.experimental.pallas import tpu as pltpu

F32 = jnp.float32
BF16 = jnp.bfloat16
MESH = pl.DeviceIdType.MESH

S5_WIDTH = 512
RWKV_WIDTH = 512
S5_GROUP = 16
S5_GROUPS = 32
S5_STATE = 64
S5_LANES = S5_GROUPS * S5_STATE
S5_TILE_GROUPS = 8
S5_TILES = S5_GROUPS // S5_TILE_GROUPS
S5_TILE_CH = S5_TILE_GROUPS * S5_GROUP
S5_TILE_LANES = S5_TILE_GROUPS * S5_STATE
HEAD = 64
HEADS = 8
DECAY_LORA = 64
AAA_LORA = 64
GATE_LORA = 128
FFN_HIDDEN = 2816
RMS_EPS = 1e-6
GN_EPS = 64e-5
L2_EPS = 1e-12
ADAM_LR = 0.001
ADAM_B1 = 0.9
ADAM_B2 = 0.999
ADAM_EPS = 1e-08
ADAM_WD = 0.01
ADAM_STEP = 10

WKV_CHUNK = 64
WIRE = jnp.bfloat16
WKV_PASSES = 1
VMEM_LIMIT_BYTES = 48 * 1024 * 1024
LANE = 128
PACK_COLS = 1024
SMALL_ROWS = 24

WEIGHT_NAMES = ['norm_mix', 'w_in', 's5_lam_re', 's5_lam_im', 's5_log_step', 's5_b_re', 's5_b_im', 's5_c_re',
                's5_c_im', 's5_d', 's5_glu_w', 's5_glu_b', 'rw_shift_mu', 'rw_w0', 'rw_w2', 'rw_a0', 'rw_a2',
                'rw_g2', 'rw_k_k', 'rw_k_a', 'rw_r_k', 'rw_ln_w', 'rw_ln_b', 'w_out', 'norm_ffn', 'ffn_w1',
                'ffn_w3', 'ffn_w2', 'norm_ple', 'ple_gate_w', 'ple_up_w', 'final_norm']
SHARDED = {'w_in': 1, 's5_glu_w': 0, 'rw_w2': 1, 'rw_a2': 1, 'rw_g2': 1, 'w_out': 0, 'ffn_w1': 1, 'ffn_w3': 1,
           'ffn_w2': 0, 'ple_gate_w': 0, 'ple_up_w': 1}
SHARDED_NAMES = [n for n in WEIGHT_NAMES if n in SHARDED]
LATE_NAMES = ['w_out', 'ffn_w1', 'ffn_w3', 'ffn_w2', 'ple_gate_w', 'ple_up_w']
EARLY_NAMES = [n for n in SHARDED_NAMES if n not in LATE_NAMES]
LATE_GATHER = {'s5_scan': ['w_out', 'ple_gate_w', 'ple_up_w'], 'wkv_fwd': ['ffn_w1', 'ffn_w3'], 'ffn_up': ['ffn_w2']}
SMALL_NAMES = [n for n in WEIGHT_NAMES if n not in SHARDED]


def _params(sem=None):
    return pltpu.CompilerParams(dimension_semantics=sem, vmem_limit_bytes=VMEM_LIMIT_BYTES)


def _tile(n, target):
    best = None
    for d in range(LANE, min(n, target) + 1, LANE):
        if n % d == 0:
            best = d
    return n if best is None else best


_NN = (((1,), (0,)), ((), ()))
_NT = (((1,), (1,)), ((), ()))
_TN = (((0,), (0,)), ((), ()))


def _split(a):
    a = a.astype(F32)
    hi = a.astype(BF16)
    return hi, (a - hi.astype(F32)).astype(BF16)


def _dg(a, b, dims, passes):
    dg = lambda p, q: lax.dot_general(p, q, dims, preferred_element_type=F32)
    if passes == 1:
        return dg(a.astype(BF16), b.astype(BF16))
    bh, bl = _split(b)
    if passes == 2:
        return dg(a.astype(BF16), bh) + dg(a.astype(BF16), bl)
    ah, al = _split(a)
    return dg(ah, bh) + (dg(ah, bl) + dg(al, bh))


_DOT_BWD = {_NN: (("g", "b", _NT), ("a", "g", _TN)),
            _NT: (("g", "b", _NN), ("g", "a", _TN)),
            _TN: (("b", "g", _NT), ("a", "g", _NN))}


@functools.partial(jax.custom_vjp, nondiff_argnums=(2, 3))
def _dot(a, b, dims, passes):
    return _dg(a, b, dims, passes)


def _dot_fwd(a, b, dims, passes):
    return _dg(a, b, dims, passes), (a, b)


def _dot_bwd(dims, passes, res, g):
    env = {"a": res[0], "b": res[1], "g": g}
    return tuple(_dg(env[p], env[q], d, passes) for p, q, d in _DOT_BWD[dims])


_dot.defvjp(_dot_fwd, _dot_bwd)


def _bdot(x, w):
    return _dot(x, w, _NN, 1)


def _fdot(a, b, dims=_NN):
    return _dot(a, b, dims, 3)


@jax.custom_vjp
def _shift_down(z):
    return pltpu.roll(z, 1, 0)


def _shift_down_fwd(z):
    return pltpu.roll(z, 1, 0), None


def _shift_down_bwd(_, g):
    return (pltpu.roll(g, g.shape[0] - 1, 0),)


_shift_down.defvjp(_shift_down_fwd, _shift_down_bwd)


def _head_ones():
    r = lax.broadcasted_iota(jnp.int32, (RWKV_WIDTH, RWKV_WIDTH), 0) // HEAD
    c = lax.broadcasted_iota(jnp.int32, (RWKV_WIDTH, RWKV_WIDTH), 1) // HEAD
    return (r == c).astype(F32)


def _mm(name, a, b, mode, out_dtype=F32, precise=False, tm=1024, tn=1024, tk=1536, plan=None):
    if mode == 'nn':
        (m, k), (_, n) = a.shape, b.shape
    elif mode == 'nt':
        (m, k), (n, _) = a.shape, b.shape
    else:
        (k, m), (_, n) = a.shape, b.shape
    tm, tn, tk = _tile(m, tm), _tile(n, tn), _tile(k, tk)
    nm, nn, nk = m // tm, n // tn, k // tk
    dims = {'nn': _NN, 'nt': _NT, 'tn': _TN}[mode]
    plan = _NO_PLAN if plan is None else plan
    parts, plan_in_specs, plan_out_shape, plan_out_specs, plan_sems = _carry(plan, 2, 1)

    def body(*refs):
        a_ref, b_ref, o_ref = refs[0], refs[1], refs[2 + len(plan.ins)]
        acc_ref = refs[3 + len(plan.ins) + len(plan.out_shape)]
        i, j, kk = pl.program_id(0), pl.program_id(1), pl.program_id(2)

        if plan is not _NO_PLAN:
            pl.when((i == 0) & (j == 0) & (kk == 0))(lambda: plan.start(*parts(refs)))

        @pl.when(kk == 0)
        def _():
            acc_ref[...] = jnp.zeros_like(acc_ref)

        acc_ref[...] += _dg(a_ref[...], b_ref[...], dims, 3 if precise else 1)

        @pl.when(kk == nk - 1)
        def _():
            o_ref[...] = acc_ref[...].astype(o_ref.dtype)

        if plan is not _NO_PLAN:
            pl.when((i == nm - 1) & (j == nn - 1) & (kk == nk - 1))(lambda: plan.wait(*parts(refs)))

    if mode == 'tn':
        a_spec = pl.BlockSpec((tk, tm), lambda i, j, l: (l, i))
    else:
        a_spec = pl.BlockSpec((tm, tk), lambda i, j, l: (i, l))
    if mode == 'nt':
        b_spec = pl.BlockSpec((tn, tk), lambda i, j, l: (j, l))
    else:
        b_spec = pl.BlockSpec((tk, tn), lambda i, j, l: (l, j))
    res = pl.pallas_call(
        body, name=name,
        out_shape=[jax.ShapeDtypeStruct((m, n), out_dtype)] + plan_out_shape,
        grid=(nm, nn, nk),
        in_specs=[a_spec, b_spec] + plan_in_specs,
        out_specs=[pl.BlockSpec((tm, tn), lambda i, j, l: (i, j))] + plan_out_specs,
        scratch_shapes=[pltpu.VMEM((tm, tn), F32)] + plan_sems,
        compiler_params=_params(("parallel", "parallel", "arbitrary") if plan is _NO_PLAN else ("arbitrary",) * 3),
    )(a, b, *plan.ins)
    return res[0] if plan is _NO_PLAN else (res[0], res[1:])


def _mm_tiles(name, a, b, mode, out_shape, grid, a_spec, b_spec, o_spec):
    dims = {'nn': _NN, 'nt': _NT, 'tn': _TN}[mode]
    nk = grid[2]

    def body(a_ref, b_ref, o_ref, acc_ref):
        kk = pl.program_id(2)

        @pl.when(kk == 0)
        def _():
            acc_ref[...] = jnp.zeros_like(acc_ref)

        acc_ref[...] += _dg(a_ref[...], b_ref[...], dims, 1)

        @pl.when(kk == nk - 1)
        def _():
            o_ref[...] = acc_ref[...]

    return pl.pallas_call(
        body, name=name,
        out_shape=jax.ShapeDtypeStruct(out_shape, F32),
        grid=grid, in_specs=[a_spec, b_spec], out_specs=o_spec,
        scratch_shapes=[pltpu.VMEM(o_spec.block_shape, F32)],
        compiler_params=_params(("parallel", "parallel", "arbitrary")),
    )(a, b)


def _s5_expand(name, u, blk, tm=1024):
    t = u.shape[0]
    tm = min(tm, t)
    ch, ln, nt = S5_TILE_CH, S5_TILE_LANES, S5_TILES
    return _mm_tiles(name, u, blk, 'nn', (t, 2 * S5_LANES), (t // tm, 2 * nt, 1),
                     pl.BlockSpec((tm, ch), lambda i, j, l: (i, j % nt)),
                     pl.BlockSpec((ch, ln), lambda i, j, l: (j % nt, j // nt)),
                     pl.BlockSpec((tm, ln), lambda i, j, l: (i, j)))


def _s5_contract(name, x, blk, tm=1024):
    t = x.shape[0]
    tm = min(tm, t)
    ch, ln, nt = S5_TILE_CH, S5_TILE_LANES, S5_TILES
    return _mm_tiles(name, x, blk, 'nt', (t, S5_WIDTH), (t // tm, nt, 2),
                     pl.BlockSpec((tm, ln), lambda i, j, l: (i, j + nt * l)),
                     pl.BlockSpec((ch, ln), lambda i, j, l: (j, l)),
                     pl.BlockSpec((tm, ch), lambda i, j, l: (i, j)))


def _s5_block_grad(name, u, x, tk=1024):
    t = u.shape[0]
    tk = min(tk, t)
    ch, ln, nt = S5_TILE_CH, S5_TILE_LANES, S5_TILES
    return _mm_tiles(name, u, x, 'tn', (S5_WIDTH, 2 * ln), (nt, 2, t // tk),
                     pl.BlockSpec((tk, ch), lambda i, j, l: (l, i)),
                     pl.BlockSpec((tk, ln), lambda i, j, l: (l, i + nt * j)),
                     pl.BlockSpec((ch, ln), lambda i, j, l: (i, j)))


def _full_spec(p):
    nd = p.ndim
    return pl.BlockSpec(p.shape, lambda i, nd=nd: (0,) * nd)


def _tok_fwd(name, fn, toks, params, outs, tm):
    t = toks[0].shape[0]
    nt, npar = len(toks), len(params)

    def body(*refs):
        tv = [r[...].astype(F32) for r in refs[:nt]]
        pv = [r[...].astype(F32) for r in refs[nt:nt + npar]]
        res = fn(*tv, *pv)
        for r, v in zip(refs[nt + npar:], res):
            r[...] = v.astype(r.dtype)

    return pl.pallas_call(
        body, name=name,
        out_shape=[jax.ShapeDtypeStruct((t, w), d) for w, d in outs],
        grid=(t // tm,),
        in_specs=[pl.BlockSpec((tm, a.shape[1]), lambda i: (i, 0)) for a in toks] + [_full_spec(p) for p in params],
        out_specs=[pl.BlockSpec((tm, w), lambda i: (i, 0)) for w, _ in outs],
        compiler_params=_params(("parallel",)),
    )(*toks, *params)


def _tok_bwd(name, fn, toks, params, cots, dtok, dpar, tm, acc_out=None, add_to=None):
    t = toks[0].shape[0]
    nt, npar = len(toks), len(params)
    cot_arrays = [c for c in cots if c is not None]
    ncot = len(cot_arrays)
    extra = [] if add_to is None else [add_to[1]]
    dtok_idx = [i for i, d in enumerate(dtok) if d is not None]
    dpar_idx = [i for i, d in enumerate(dpar) if d]

    def body(*refs):
        pos = 0
        tin = refs[pos:pos + nt]; pos += nt
        pin = refs[pos:pos + npar]; pos += npar
        cin = refs[pos:pos + ncot]; pos += ncot
        ein = refs[pos:pos + len(extra)]; pos += len(extra)
        dto = refs[pos:pos + len(dtok_idx)]; pos += len(dtok_idx)
        dpo = refs[pos:pos + len(dpar_idx)]; pos += len(dpar_idx)
        acc = refs[pos] if acc_out is not None else None
        first = pl.program_id(0) == 0

        tv = [r[...].astype(F32) for r in tin]
        pv = [r[...].astype(F32) for r in pin]
        res, vjp = jax.vjp(fn, *tv, *pv)
        cit = iter(cin)
        cs = tuple(jnp.ones_like(o) if c is None else next(cit)[...].astype(F32) for c, o in zip(cots, res))
        g = vjp(cs)
        for r, i in zip(dto, dtok_idx):
            v = g[i]
            if add_to is not None and add_to[0] == i:
                v = v + ein[0][...].astype(F32)
            r[...] = v.astype(r.dtype)

        @pl.when(first)
        def _():
            for r in dpo:
                r[...] = jnp.zeros_like(r)
            if acc is not None:
                acc[...] = jnp.zeros_like(acc)

        for r, i in zip(dpo, dpar_idx):
            r[...] += g[nt + i]
        if acc is not None:
            acc[...] += res[acc_out]

    out_shape = [jax.ShapeDtypeStruct(toks[i].shape, dtok[i]) for i in dtok_idx]
    out_shape += [jax.ShapeDtypeStruct(params[i].shape, F32) for i in dpar_idx]
    out_specs = [pl.BlockSpec((tm, toks[i].shape[1]), lambda i_: (i_, 0)) for i in dtok_idx]
    out_specs += [_full_spec(params[i]) for i in dpar_idx]
    if acc_out is not None:
        out_shape.append(jax.ShapeDtypeStruct((1, 1), F32))
        out_specs.append(pl.BlockSpec((1, 1), lambda i_: (0, 0)))
    tok_spec = lambda a: pl.BlockSpec((tm, a.shape[1]), lambda i_: (i_, 0))
    return pl.pallas_call(
        body, name=name,
        out_shape=out_shape,
        grid=(t // tm,),
        in_specs=[tok_spec(a) for a in toks] + [_full_spec(p) for p in params]
        + [tok_spec(c) for c in cot_arrays] + [tok_spec(e) for e in extra],
        out_specs=out_specs,
        compiler_params=_params(("arbitrary",)),
    )(*toks, *params, *cot_arrays, *extra)


def _small_fwd(name, fn, ins, outs):
    n = len(ins)

    def body(*refs):
        res = fn(*[r[...] for r in refs[:n]])
        for r, v in zip(refs[n:], res):
            r[...] = v.astype(r.dtype)

    return pl.pallas_call(
        body, name=name,
        out_shape=[jax.ShapeDtypeStruct(s, d) for s, d in outs],
        compiler_params=_params(),
    )(*ins)


def _small_bwd(name, fn, ins, cots):
    n = len(ins)

    def body(*refs):
        _, vjp = jax.vjp(fn, *[r[...] for r in refs[:n]])
        g = vjp(tuple(r[...] for r in refs[n:n + len(cots)]))
        for r, v in zip(refs[n + len(cots):], g):
            r[...] = v

    return pl.pallas_call(
        body, name=name,
        out_shape=[jax.ShapeDtypeStruct(a.shape, F32) for a in ins],
        compiler_params=_params(),
    )(*ins, *cots)


def _rms(x, g):
    return x * lax.rsqrt(jnp.mean(x * x, axis=-1, keepdims=True) + RMS_EPS) * g


def _f_norm_in(x, g):
    return (_rms(x, g),)


def _f_mix_res(x, mixed, g):
    h1 = x + mixed
    return h1, _rms(h1, g)


def _f_ffn_act(a13):
    a1, a3 = a13[:, :FFN_HIDDEN], a13[:, FFN_HIDDEN:]
    return (jax.nn.silu(a1) * a3,)


def _f_ffn_res(h1, ffo, g):
    h2 = h1 + ffo
    return h2, _rms(h2, g)


def _f_loss(h2, gpre, pu, target, g):
    h3 = h2 + jax.nn.sigmoid(gpre) * pu
    y = _rms(h3, g)
    err = jnp.square(y - target)
    return (0.5 * jnp.sum(jnp.mean(err, axis=-1, keepdims=True), axis=0, keepdims=True),)


def _f_s5_post(ypre, u, d, glu_w, glu_b):
    z = jax.nn.gelu(ypre + u * d)
    return (z * jax.nn.sigmoid(_bdot(z, glu_w) + glu_b),)


def _softplus(x):
    return jnp.maximum(x, 0.0) + jnp.log(1.0 + jnp.exp(-jnp.abs(x)))


def _f_rw_pre(z, carry, shift_mu, w0, w2, a0, a2, g2, k_k, k_a):
    rw = RWKV_WIDTH
    first_row = lax.broadcasted_iota(jnp.int32, z.shape, 0) == 0
    prev = jnp.where(first_row, carry, _shift_down(z))
    zs = z + (prev - z) * shift_mu
    o1, o2 = 3 * rw + DECAY_LORA, 3 * rw + DECAY_LORA + AAA_LORA
    r, k, v = zs[:, :rw], zs[:, rw:2 * rw], zs[:, 2 * rw:3 * rw]
    wl, al, gl = zs[:, 3 * rw:o1], zs[:, o1:o2], zs[:, o2:]
    w = -_softplus(-(w0 + _bdot(jnp.tanh(wl), w2))) - 0.5
    log_decay = -jnp.exp(w)
    a = jax.nn.sigmoid(a0 + _bdot(al, a2))
    g = _bdot(jax.nn.sigmoid(gl), g2)
    kk = k * k_k
    norm = jnp.sqrt(_fdot(kk * kk, _head_ones()))
    kk = kk / jnp.maximum(norm, L2_EPS)
    kp = k * (1.0 + (a - 1.0) * k_a)
    return r, log_decay, kp, v, -kk, kk * a, g


def _f_rw_post(y, r, kp, v, g, ln_w, ln_b, r_k):
    ones = _head_ones()
    yc = y - _fdot(y, ones) * (1.0 / HEAD)
    var = _fdot(yc * yc, ones) * (1.0 / HEAD)
    yn = yc * lax.rsqrt(var + GN_EPS) * ln_w + ln_b
    bonus = _fdot(r * kp * r_k, ones) * v
    return ((yn + bonus) * g,)


def _f_s5_lam(lam_re, lam_im, log_step):
    step = jnp.exp(log_step)
    dr, di = lam_re * step, lam_im * step
    e = jnp.exp(dr)
    lbr, lbi = e * jnp.cos(di), e * jnp.sin(di)
    nr, ni = lbr - 1.0, lbi
    den = lam_re * lam_re + lam_im * lam_im
    return lbr, lbi, (nr * lam_re + ni * lam_im) / den, (ni * lam_re - nr * lam_im) / den


def _f_s5_build(coef_r, coef_i, btr, bti, ctr, cti):
    bbr = coef_r * btr - coef_i * bti
    bbi = coef_r * bti + coef_i * btr
    shape = (S5_WIDTH, S5_TILE_LANES)
    rows = (lax.broadcasted_iota(jnp.int32, shape, 0) % S5_TILE_CH) // S5_GROUP
    cols = lax.broadcasted_iota(jnp.int32, shape, 1) // S5_STATE
    mask = (rows == cols).astype(F32)

    def blocks(m):
        per_tile = [m[:, S5_TILE_LANES * i:S5_TILE_LANES * (i + 1)] for i in range(S5_TILES)]
        return jnp.concatenate([t for t in per_tile for _ in range(S5_TILE_GROUPS)], axis=0) * mask

    return (jnp.concatenate([blocks(bbr), blocks(bbi)], axis=1),
            jnp.concatenate([blocks(ctr), -blocks(cti)], axis=1))


HALO = 8


def _rw_pre_specs(z, params, tm, order):
    halo_blocks = tm // HALO
    return ([pl.BlockSpec((tm, z.shape[1]), lambda i: (order(i), 0)),
             pl.BlockSpec((HALO, z.shape[1]), lambda i: (jnp.maximum(order(i) * halo_blocks - 1, 0), 0))]
            + [_full_spec(p) for p in params])


def _rw_pre_fwd(z, params, tm):
    t = z.shape[0]
    npar = len(params)

    def body(z_ref, halo_ref, *refs):
        carry = jnp.where(pl.program_id(0) == 0, 0.0, halo_ref[pl.ds(HALO - 1, 1), :])
        res = _f_rw_pre(z_ref[...], carry, *[r[...].astype(F32) for r in refs[:npar]])
        for r, v in zip(refs[npar:], res):
            r[...] = v

    return pl.pallas_call(
        body, name="rw_pre",
        out_shape=[jax.ShapeDtypeStruct((t, RWKV_WIDTH), F32)] * 7,
        grid=(t // tm,),
        in_specs=_rw_pre_specs(z, params, tm, lambda i: i),
        out_specs=[pl.BlockSpec((tm, RWKV_WIDTH), lambda i: (i, 0))] * 7,
        compiler_params=_params(("parallel",)),
    )(z, z, *params)


def _rw_pre_bwd(z, params, cots, tm):
    t = z.shape[0]
    nt = t // tm
    npar = len(params)
    order = lambda i: nt - 1 - i
    flat_cots = [a for group in cots for a in group]
    ncot = len(flat_cots)

    def body(z_ref, halo_ref, *refs):
        pin, cin = refs[:npar], list(refs[npar:npar + ncot])
        dz_ref = refs[npar + ncot]
        dpo = refs[npar + ncot + 1:npar + ncot + 1 + npar]
        dcarry_ref = refs[npar + ncot + 1 + npar]
        i = pl.program_id(0)

        @pl.when(i == 0)
        def _():
            dcarry_ref[...] = jnp.zeros_like(dcarry_ref)
            for r in dpo:
                r[...] = jnp.zeros_like(r)

        carry = jnp.where(i == nt - 1, 0.0, halo_ref[pl.ds(HALO - 1, 1), :])
        _, vjp = jax.vjp(_f_rw_pre, z_ref[...], carry, *[r[...].astype(F32) for r in pin])
        g = vjp(tuple(sum(cin.pop(0)[...] for _ in group) for group in cots))
        last_row = lax.broadcasted_iota(jnp.int32, z_ref.shape, 0) == tm - 1
        dz_ref[...] = g[0] + jnp.where(last_row, dcarry_ref[...], 0.0)
        dcarry_ref[...] = g[1]
        for r, v in zip(dpo, g[2:]):
            r[...] += v

    tok = lambda w: pl.BlockSpec((tm, w), lambda i: (order(i), 0))
    return pl.pallas_call(
        body, name="rw_pre_bwd",
        out_shape=[jax.ShapeDtypeStruct(z.shape, F32)] + [jax.ShapeDtypeStruct(p.shape, F32) for p in params],
        grid=(nt,),
        in_specs=_rw_pre_specs(z, params, tm, order) + [tok(RWKV_WIDTH)] * ncot,
        out_specs=[tok(z.shape[1])] + [_full_spec(p) for p in params],
        scratch_shapes=[pltpu.VMEM((1, z.shape[1]), F32)],
        compiler_params=_params(("arbitrary",)),
    )(z, z, *params, *flat_cots)


def _s5_scan(bu, lam, tm, plan):
    t, w = bu.shape
    h = w // 2
    nt = t // tm
    parts, plan_in_specs, plan_out_shape, plan_out_specs, plan_sems = _carry(plan, 2, 2)

    def body(*refs):
        bu_ref, lam_ref = refs[:2]
        xb_ref, xp_ref = refs[2 + len(plan.ins):4 + len(plan.ins)]
        carry_ref = refs[4 + len(plan.ins) + len(plan.out_shape)]
        x_ref, refs = refs[-1], refs[:-1]

        @pl.when(pl.program_id(0) == 0)
        def _():
            carry_ref[...] = jnp.zeros_like(carry_ref)
            plan.start(*parts(refs))

        lr, li = lam_ref[:, :h], lam_ref[:, h:]

        def step(s, c):
            cr, ci = c
            row = pl.ds(s, 1)
            xp_ref[row, :h] = cr
            xp_ref[row, h:] = ci
            nr = lr * cr - li * ci + bu_ref[row, :h]
            ni = lr * ci + li * cr + bu_ref[row, h:]
            x_ref[row, :h] = nr
            x_ref[row, h:] = ni
            return nr, ni

        cr, ci = lax.fori_loop(0, tm, step, (carry_ref[:, :h], carry_ref[:, h:]))
        carry_ref[:, :h] = cr
        carry_ref[:, h:] = ci
        xb_ref[...] = x_ref[...].astype(BF16)

        @pl.when(pl.program_id(0) == nt - 1)
        def _():
            plan.wait(*parts(refs))

    spec = pl.BlockSpec((tm, w), lambda i: (i, 0))
    res = pl.pallas_call(
        body, name="s5_scan",
        out_shape=[jax.ShapeDtypeStruct((t, w), BF16), jax.ShapeDtypeStruct((t, w), F32)] + plan_out_shape,
        grid=(nt,),
        in_specs=[spec, pl.BlockSpec((1, w), lambda i: (0, 0))] + plan_in_specs,
        out_specs=[spec, spec] + plan_out_specs,
        scratch_shapes=[pltpu.VMEM((1, w), F32)] + plan_sems + [pltpu.VMEM((tm, w), F32)],
        compiler_params=_params(("arbitrary",)),
    )(bu, lam, *plan.ins)
    return res[0], res[1], res[2:]


def _s5_scan_bwd(dx, xp, lam, tm):
    t, w = dx.shape
    h = w // 2
    nt = t // tm

    def body(dx_ref, xp_ref, lam_ref, dbu_out_ref, dlam_ref, carry_ref, dbu_ref):
        @pl.when(pl.program_id(0) == 0)
        def _():
            carry_ref[...] = jnp.zeros_like(carry_ref)
            dlam_ref[...] = jnp.zeros_like(dlam_ref)

        lr, li = lam_ref[:, :h], lam_ref[:, h:]

        def step(s, c):
            cr, ci = c
            row = pl.ds(tm - 1 - s, 1)
            nr = lr * cr + li * ci + dx_ref[row, :h]
            ni = lr * ci - li * cr + dx_ref[row, h:]
            dbu_ref[row, :h] = nr
            dbu_ref[row, h:] = ni
            return nr, ni

        cr, ci = lax.fori_loop(0, tm, step, (carry_ref[:, :h], carry_ref[:, h:]))
        carry_ref[:, :h] = cr
        carry_ref[:, h:] = ci
        gr, gi = dbu_ref[:, :h], dbu_ref[:, h:]
        pr, pi_ = xp_ref[:, :h], xp_ref[:, h:]
        dlam_ref[:, :h] += jnp.sum(gr * pr + gi * pi_, axis=0, keepdims=True)
        dlam_ref[:, h:] += jnp.sum(gi * pr - gr * pi_, axis=0, keepdims=True)
        dbu_out_ref[...] = dbu_ref[...].astype(BF16)

    spec = pl.BlockSpec((tm, w), lambda i: (nt - 1 - i, 0))
    row_spec = pl.BlockSpec((1, w), lambda i: (0, 0))
    return pl.pallas_call(
        body, name="s5_scan_bwd",
        out_shape=[jax.ShapeDtypeStruct((t, w), BF16), jax.ShapeDtypeStruct((1, w), F32)],
        grid=(nt,),
        in_specs=[spec, spec, row_spec],
        out_specs=[spec, row_spec],
        scratch_shapes=[pltpu.VMEM((1, w), F32), pltpu.VMEM((tm, w), F32)],
        compiler_params=_params(("arbitrary",)),
    )(dx, xp, lam)


def _wkv_chunks(s0, r, lw, k, v, a, b):
    c = r[0].shape[0]
    row = lax.broadcasted_iota(jnp.int32, (c, c), 0)
    col = lax.broadcasted_iota(jnp.int32, (c, c), 1)
    incl, strict = col <= row, col < row
    tri = incl.astype(F32)
    eye = (row == col).astype(F32)
    each = lambda f, *xs: [f(*t) for t in zip(*xs)]
    stack = lambda p, q: jnp.concatenate([p, q], axis=0)
    dot = lambda p, q, dims=_NN: _dot(p, q, dims, WKV_PASSES)
    lc = each(lambda l: _dot(tri, l, _NN, 2), lw)
    e_neg = each(lambda l: jnp.exp(-l), lc)
    ar = each(lambda x, z, l, w: stack(x * jnp.exp(l - w), z * jnp.exp(l)), a, r, lc, lw)
    bk = each(lambda x, z, e: stack(x * e, z * e), b, k, e_neg)
    m = each(lambda p, q: dot(p, q, _NT), ar, bk)
    mab = each(lambda q: jnp.where(strict, q[:c, :c], 0.0), m)
    mak_mrk = each(lambda q: stack(jnp.where(strict, q[:c, c:], 0.0), jnp.where(incl, q[c:, c:], 0.0)), m)
    mrb = each(lambda q: jnp.where(incl, q[c:, :c], 0.0), m)
    xy = each(lambda p, s, q, z: dot(p, s, _NT) + dot(q, z), ar, s0, mak_mrk, v)
    inv = each(lambda q: eye + q, mab)
    pw = each(lambda q: dot(q, q), mab)
    for _ in range(int(math.log2(c)) - 2):
        both = each(lambda i, q: dot(stack(i, q), q), inv, pw)
        inv = each(lambda i, q: i + q[:c], inv, both)
        pw = each(lambda q: q[c:], both)
    inv = each(lambda i, q: i + dot(i, q), inv, pw)
    u = each(lambda i, q: dot(i, q[:c]), inv, xy)
    y = each(lambda q, z, p: q[c:] + dot(z, p), xy, mrb, u)
    e_tot = each(lambda l: jnp.exp(jnp.sum(l, axis=0, keepdims=True)), lw)
    s1 = each(lambda s, p, z, q, e: (s + dot(stack(p, z), q, _TN)) * e, s0, u, v, bk, e_tot)
    return y, s1


def _carry(plan, n_args, n_outs):
    n_in, n_out = len(plan.ins), len(plan.out_shape)

    def parts(refs):
        base = n_args + n_in + n_outs
        return refs[n_args:n_args + n_in], refs[base:base + n_out], refs[base + n_out + 1:]

    return parts, [HBM_SPEC] * n_in, list(plan.out_shape), [HBM_SPEC] * n_out, list(plan.sems)


def _head_cols(ref):
    return tuple(ref[:, h * HEAD:(h + 1) * HEAD] for h in range(HEADS))


def _wkv_fwd(seqs, plan):
    t, w = seqs[0].shape
    c, n = WKV_CHUNK, HEAD
    nc = t // c
    parts, plan_in_specs, plan_out_shape, plan_out_specs, plan_sems = _carry(plan, 6, 2)

    def body(*refs):
        ins, (y_ref, ck_ref) = refs[:6], refs[6 + len(plan.ins):8 + len(plan.ins)]
        s_ref = refs[8 + len(plan.ins) + len(plan.out_shape)]

        @pl.when(pl.program_id(0) == 0)
        def _():
            s_ref[...] = jnp.zeros_like(s_ref)
            plan.start(*parts(refs))

        s0 = tuple(s_ref[h] for h in range(HEADS))
        ys, s1 = _wkv_chunks(s0, *[_head_cols(r) for r in ins])
        for h in range(HEADS):
            ck_ref[0, h] = s0[h]
            y_ref[:, h * n:(h + 1) * n] = ys[h]
            s_ref[h] = s1[h]

        @pl.when(pl.program_id(0) == nc - 1)
        def _():
            plan.wait(*parts(refs))

    spec = pl.BlockSpec((c, w), lambda i: (i, 0))
    res = pl.pallas_call(
        body, name="wkv_fwd",
        out_shape=[jax.ShapeDtypeStruct((t, w), F32), jax.ShapeDtypeStruct((nc, HEADS, n, n), F32)] + plan_out_shape,
        grid=(nc,),
        in_specs=[spec] * 6 + plan_in_specs,
        out_specs=[spec, pl.BlockSpec((1, HEADS, n, n), lambda i: (i, 0, 0, 0))] + plan_out_specs,
        scratch_shapes=[pltpu.VMEM((HEADS, n, n), F32)] + plan_sems,
        compiler_params=_params(("arbitrary",)),
    )(*seqs, *plan.ins)
    return res[0], res[1], res[2:]


def _wkv_bwd(seqs, ck, dy, plan):
    t, w = seqs[0].shape
    c, n = WKV_CHUNK, HEAD
    nc = t // c
    parts, plan_in_specs, plan_out_shape, plan_out_specs, plan_sems = _carry(plan, 8, 6)

    def body(*refs):
        ins, ck_ref, dy_ref = refs[:6], refs[6], refs[7]
        outs = refs[8 + len(plan.ins):14 + len(plan.ins)]
        ds_ref = refs[14 + len(plan.ins) + len(plan.out_shape)]

        @pl.when(pl.program_id(0) == 0)
        def _():
            ds_ref[...] = jnp.zeros_like(ds_ref)
            plan.start(*parts(refs))

        s0 = tuple(ck_ref[0, h] for h in range(HEADS))
        _, vjp = jax.vjp(_wkv_chunks, s0, *[_head_cols(r) for r in ins])
        g = vjp((list(_head_cols(dy_ref)), [ds_ref[h] for h in range(HEADS)]))
        for h in range(HEADS):
            ds_ref[h] = g[0][h]
            for o, d in zip(outs, g[1:]):
                o[:, h * n:(h + 1) * n] = d[h]

        @pl.when(pl.program_id(0) == nc - 1)
        def _():
            plan.wait(*parts(refs))

    spec = pl.BlockSpec((c, w), lambda i: (nc - 1 - i, 0))
    res = pl.pallas_call(
        body, name="wkv_bwd",
        out_shape=[jax.ShapeDtypeStruct((t, w), F32)] * 6 + plan_out_shape,
        grid=(nc,),
        in_specs=[spec] * 6 + [pl.BlockSpec((1, HEADS, n, n), lambda i: (nc - 1 - i, 0, 0, 0)), spec] + plan_in_specs,
        out_specs=[spec] * 6 + plan_out_specs,
        scratch_shapes=[pltpu.VMEM((HEADS, n, n), F32)] + plan_sems,
        compiler_params=_params(("arbitrary",)),
    )(*seqs, ck, dy, *plan.ins)
    return res[:6], res[6:]


def _coords():
    return lax.axis_index("x"), lax.axis_index("y"), lax.axis_index("c")


def _flip(v, f):
    return 1 - v if f else v


_CHIP_FLIPS = [(1, 0), (0, 1), (1, 1)]
_DEV_FLIPS = [(fx, fy, fc) for fx in (0, 1) for fy in (0, 1) for fc in (0, 1) if (fx, fy, fc) != (0, 0, 0)]
HBM_SPEC = pl.BlockSpec(memory_space=pl.ANY)


def _chip_peer(k, x, y):
    fx, fy = _CHIP_FLIPS[k]
    return _flip(x, fx), _flip(y, fy)


def _dev_peer(k, x, y, c):
    fx, fy, fc = _DEV_FLIPS[k]
    return _flip(x, fx), _flip(y, fy), _flip(c, fc)


def _rows_of_core(ref, core):
    h = ref.shape[-2] // 2
    rows = pl.ds(pl.multiple_of(core * h, 8), h)
    return ref.at[rows, :] if len(ref.shape) == 2 else ref.at[:, rows, :]


class _Plan(NamedTuple):
    ins: Sequence[Any]
    out_shape: Sequence[Any]
    sems: Sequence[Any]
    start: Callable
    wait: Callable


_NO_PLAN = _Plan([], [], [], lambda *_: None, lambda *_: None)


def _run_plan(name, plan):
    n_in, n_out = len(plan.ins), len(plan.out_shape)

    def body(*refs):
        parts = refs[:n_in], refs[n_in:n_in + n_out], refs[n_in + n_out:]
        plan.start(*parts)
        plan.wait(*parts)

    return pl.pallas_call(
        body, name=name, out_shape=list(plan.out_shape),
        in_specs=[HBM_SPEC] * n_in, out_specs=[HBM_SPEC] * n_out, scratch_shapes=list(plan.sems),
    )(*plan.ins)


def _gather_plan(shards):
    n = len(shards)

    def copies(srcs, outs, sems):
        send_sems, recv_sems, local_sems = sems
        x, y, c = _coords()
        me = 2 * x + y

        def remote(i, k, arriving):
            px, py = _chip_peer(k, x, y)
            return pltpu.make_async_remote_copy(
                src_ref=srcs[i], dst_ref=outs[i].at[2 * px + py if arriving else me],
                send_sem=send_sems.at[i, k], recv_sem=recv_sems.at[i, k],
                device_id=(px, py, c), device_id_type=MESH)

        own = [pltpu.make_async_copy(srcs[i], outs[i].at[me], local_sems.at[i]) for i in range(n)]
        pairs = [(i, k) for k in range(3) for i in range(n)]
        return own, [remote(i, k, False) for i, k in pairs], [remote(i, k, True) for i, k in pairs]

    return _Plan(
        ins=shards, out_shape=[jax.ShapeDtypeStruct((4,) + s.shape, s.dtype) for s in shards],
        sems=[pltpu.SemaphoreType.DMA((n, 3)), pltpu.SemaphoreType.DMA((n, 3)), pltpu.SemaphoreType.DMA((n,))],
        start=functools.partial(_start_copies, copies), wait=functools.partial(_wait_copies, copies))


def _start_copies(copies, ins, outs, sems):
    own, sends, _ = copies(ins, outs, sems)
    for cp in own + sends:
        cp.start()


def _wait_copies(copies, ins, outs, sems):
    own, sends, arrivals = copies(ins, outs, sems)
    for cp in arrivals:
        cp.wait_recv()
    for cp in sends:
        cp.wait_send()
    for cp in own:
        cp.wait()


def _exchange_plan(gs, small=None):
    n = len(gs)
    arrays = list(gs) + ([] if small is None else [small])

    def copies(srcs, outs, sems):
        send_sems, recv_sems, local_sems = sems
        x, y, c = _coords()
        me = 4 * x + 2 * y + c

        def piece(i, px, py, pc):
            if i == n:
                return srcs[i].at[4 * px + 2 * py + pc]
            return _rows_of_core(srcs[i].at[2 * px + py], pc)

        def remote(i, k, arriving):
            px, py, pc = _dev_peer(k, x, y, c)
            return pltpu.make_async_remote_copy(
                src_ref=piece(i, px, py, pc), dst_ref=outs[i].at[4 * px + 2 * py + pc if arriving else me],
                send_sem=send_sems.at[i, k], recv_sem=recv_sems.at[i, k],
                device_id=(px, py, pc), device_id_type=MESH)

        own = [pltpu.make_async_copy(piece(i, x, y, c), outs[i].at[me], local_sems.at[i]) for i in range(len(arrays))]
        pairs = [(i, k) for k in range(7) for i in range(len(arrays))]
        return own, [remote(i, k, False) for i, k in pairs], [remote(i, k, True) for i, k in pairs]

    out_shape = [jax.ShapeDtypeStruct((8, g.shape[1] // 2, g.shape[2]), g.dtype) for g in gs]
    if small is not None:
        out_shape.append(jax.ShapeDtypeStruct(small.shape, small.dtype))
    m = len(arrays)
    return _Plan(
        ins=arrays, out_shape=out_shape,
        sems=[pltpu.SemaphoreType.DMA((m, 7)), pltpu.SemaphoreType.DMA((m, 7)), pltpu.SemaphoreType.DMA((m,))],
        start=functools.partial(_start_copies, copies), wait=functools.partial(_wait_copies, copies))


def _share_cores(halves, small):
    n = len(halves)

    def body(*refs):
        srcs, small_src, outs, small_out = refs[:n], refs[n], refs[n + 1:2 * n + 1], refs[2 * n + 1]
        mine, theirs = refs[2 * n + 2:3 * n + 2], refs[3 * n + 2:4 * n + 2]
        send_sems, recv_sems, ssend, srecv, local_sems = refs[4 * n + 2:]
        x, y, c = _coords()
        me = 4 * x + 2 * y + c

        def big(i):
            return pltpu.make_async_remote_copy(
                src_ref=mine[i], dst_ref=theirs[i], send_sem=send_sems.at[i], recv_sem=recv_sems.at[i],
                device_id=(x, y, 1 - c), device_id_type=MESH)

        def tiny(k, arriving):
            px, py, pc = _dev_peer(k, x, y, c)
            return pltpu.make_async_remote_copy(
                src_ref=small_src, dst_ref=small_out.at[4 * px + 2 * py + pc if arriving else me],
                send_sem=ssend.at[k], recv_sem=srecv.at[k], device_id=(px, py, pc), device_id_type=MESH)

        small_sends = [tiny(k, False) for k in range(7)]
        own_small = pltpu.make_async_copy(small_src, small_out.at[me], local_sems.at[2 * n])
        stage = [pltpu.make_async_copy(srcs[i], mine[i], local_sems.at[i]) for i in range(n)]
        for cp in small_sends + [own_small] + stage:
            cp.start()
        sends = []
        for i in range(n):
            stage[i].wait()
            sends.append(big(i))
            sends[-1].start()
        store = [pltpu.make_async_copy(mine[i], outs[i].at[c], local_sems.at[i]) for i in range(n)]
        for cp in store:
            cp.start()
        for i in range(n):
            big(i).wait_recv()
            store.append(pltpu.make_async_copy(theirs[i], outs[i].at[1 - c], local_sems.at[n + i]))
            store[-1].start()
        for k in range(7):
            tiny(k, True).wait_recv()
        for cp in sends + small_sends:
            cp.wait_send()
        for cp in store + [own_small]:
            cp.wait()

    staged = [pltpu.VMEM(s.shape, s.dtype) for s in halves]
    res = pl.pallas_call(
        body, name="share_cores",
        out_shape=[jax.ShapeDtypeStruct((2,) + s.shape, s.dtype) for s in halves]
        + [jax.ShapeDtypeStruct((8,) + small.shape, small.dtype)],
        in_specs=[HBM_SPEC] * (n + 1), out_specs=[HBM_SPEC] * (n + 1),
        scratch_shapes=staged + staged + [
            pltpu.SemaphoreType.DMA((n,)), pltpu.SemaphoreType.DMA((n,)),
            pltpu.SemaphoreType.DMA((7,)), pltpu.SemaphoreType.DMA((7,)),
            pltpu.SemaphoreType.DMA((2 * n + 1,))],
        compiler_params=pltpu.CompilerParams(vmem_limit_bytes=VMEM_LIMIT_BYTES),
    )(*halves, small)
    return res[:n], res[n]


BF16_SUBLANES = 16


def _row_tile(n, target, step=BF16_SUBLANES):
    return max([d for d in range(step, min(n, target) + 1, step) if n % d == 0] or [n])


def _ew(name, fn, ins, outs, block_bytes=1 << 20):
    rows, cols = ins[0].shape[-2:]
    lead = max(math.prod(a.shape[:-2]) for a in ins)
    tr = _row_tile(rows, max(8, block_bytes // (4 * cols * lead)))
    n = len(ins)

    def spec(shape):
        if len(shape) == 2:
            return pl.BlockSpec((tr, cols), lambda i: (i, 0))
        return pl.BlockSpec((shape[0], tr, cols), lambda i: (0, i, 0))

    def body(*refs):
        res = fn(*[r[...] for r in refs[:n]])
        for r, v in zip(refs[n:], res):
            r[...] = v

    return pl.pallas_call(
        body, name=name,
        out_shape=[jax.ShapeDtypeStruct(s, F32) for s in outs],
        grid=(rows // tr,),
        in_specs=[spec(a.shape) for a in ins],
        out_specs=[spec(s) for s in outs],
        compiler_params=_params(("parallel",)),
    )(*ins)


def _sum_slots(a):
    total = a[0].astype(F32)
    for s in range(1, a.shape[0]):
        total = total + a[s].astype(F32)
    return (total,)


def _adamw(g, w, m, v):
    bc1 = 1.0 - ADAM_B1 ** ADAM_STEP
    bc2 = 1.0 - ADAM_B2 ** ADAM_STEP
    m_new = ADAM_B1 * m + (1.0 - ADAM_B1) * g
    v_new = ADAM_B2 * v + (1.0 - ADAM_B2) * jnp.square(g)
    delta = -ADAM_LR * ((m_new / bc1) / (jnp.sqrt(v_new / bc2) + ADAM_EPS) + ADAM_WD * w)
    return delta, m_new, v_new


def _mat(a):
    return a.reshape(a.shape[-2:])


def _to_shard_major(full, axis):
    rows, cols = full.shape
    if axis == 0:
        return full.reshape(4, rows // 4, cols)
    return full.reshape(rows, 4, cols // 4).transpose(1, 0, 2)


def _from_shard_major(a, axis):
    _, r, cs = a.shape
    if axis == 0:
        return a.reshape(4 * r, cs)
    return a.transpose(1, 0, 2).reshape(r, 4 * cs)


def _pack_small(arrays):
    flat = jnp.concatenate([arrays[n].reshape(-1) for n in SMALL_NAMES])
    flat = jnp.pad(flat, (0, 8 * SMALL_ROWS * PACK_COLS - flat.shape[0]))
    return flat.reshape(8, SMALL_ROWS, PACK_COLS)


def _unpack_small(packed, shapes):
    flat = packed.reshape(-1)
    out, off = {}, 0
    for n in SMALL_NAMES:
        size = math.prod(shapes[n])
        out[n] = flat[off:off + size].reshape(shapes[n])
        off += size
    return out


def _row(a):
    return a.reshape(1, -1)


def _local_step(x, p, target, wf, ws, late_shards):
    wf = dict(wf)
    t = x.shape[0]
    tm = min(256, t)
    g = {}

    lam_re, lam_im = ws['s5_lam_re'].reshape(S5_GROUPS, S5_STATE), ws['s5_lam_im'].reshape(S5_GROUPS, S5_STATE)
    log_step = ws['s5_log_step'].reshape(S5_GROUPS, 1)
    gp = (S5_GROUPS, S5_STATE)
    lam_ins = (lam_re, lam_im, log_step)
    lbr, lbi, cfr, cfi = _small_fwd("s5_lam", _f_s5_lam, lam_ins, [(gp, F32)] * 4)
    lam_row = jnp.concatenate([_row(lbr), _row(lbi)], axis=1)
    to_t = lambda a, perm: a.reshape((S5_GROUPS,) + a.shape[-2:]).transpose(perm).reshape(S5_GROUP, S5_LANES)
    build_ins = (_row(cfr), _row(cfi), to_t(ws['s5_b_re'], (2, 0, 1)), to_t(ws['s5_b_im'], (2, 0, 1)),
                 to_t(ws['s5_c_re'], (1, 0, 2)), to_t(ws['s5_c_im'], (1, 0, 2)))
    block_shape = (S5_WIDTH, 2 * S5_TILE_LANES)
    b_blk, c_blk = _small_fwd("s5_build", _f_s5_build, build_ins, [(block_shape, F32)] * 2)

    norm_mix, norm_ffn, norm_ple = _row(ws['norm_mix']), _row(ws['norm_ffn']), _row(ws['norm_ple'])
    final_norm = _row(ws['final_norm'])
    (xn,) = _tok_fwd("norm_in", _f_norm_in, [x], [norm_mix], [(x.shape[1], BF16)], tm)
    u = _mm("proj_s5", xn, wf['w_in'][:, :S5_WIDTH], 'nn')
    z = _mm("proj_rw", xn, wf['w_in'][:, S5_WIDTH:], 'nn')

    bu = _s5_expand("s5_bu", u, b_blk)
    def late_plan(carrier):
        return _gather_plan([late_shards[n] for n in LATE_GATHER[carrier]])

    def arrived(carrier, got):
        wf.update({n: _from_shard_major(a, SHARDED[n]) for n, a in zip(LATE_GATHER[carrier], got)})

    xs, xs_prev, got = _s5_scan(bu, lam_row, tm, late_plan('s5_scan'))
    arrived('s5_scan', got)
    ypre = _s5_contract("s5_y", xs, c_blk)
    s5_par = [_row(ws['s5_d']), wf['s5_glu_w'], _row(ws['s5_glu_b'])]
    (s5_out,) = _tok_fwd("s5_post", _f_s5_post, [ypre, u], s5_par, [(S5_WIDTH, BF16)], tm)

    pre_par = [_row(ws['rw_shift_mu']), _row(ws['rw_w0']), wf['rw_w2'], _row(ws['rw_a0']), wf['rw_a2'],
               wf['rw_g2'], _row(ws['rw_k_k']), _row(ws['rw_k_a'])]
    r, lw, kp, v, an, bn, gate = _rw_pre_fwd(z, pre_par, tm)
    seqs = [r, lw, kp, v, an, bn]
    y_wkv, ck, got = _wkv_fwd(seqs, late_plan('wkv_fwd'))
    arrived('wkv_fwd', got)
    post_par = [_row(ws['rw_ln_w']), _row(ws['rw_ln_b']), _row(ws['rw_r_k'])]
    post_toks = [y_wkv, r, kp, v, gate]
    (rw_out,) = _tok_fwd("rw_post", _f_rw_post, post_toks, post_par, [(RWKV_WIDTH, BF16)], tm)

    mixcat = jnp.concatenate([s5_out, rw_out], axis=1)
    mixed = _mm("mix_out", mixcat, wf['w_out'], 'nn')
    h1, hn = _tok_fwd("mix_res", _f_mix_res, [x, mixed], [norm_ffn], [(x.shape[1], F32), (x.shape[1], BF16)], tm)
    w13 = jnp.concatenate([wf['ffn_w1'], wf['ffn_w3']], axis=1)
    a13, got = _mm("ffn_up", hn, w13, 'nn', out_dtype=BF16, plan=late_plan('ffn_up'))
    arrived('ffn_up', got)
    (f,) = _tok_fwd("ffn_act", _f_ffn_act, [a13], [], [(FFN_HIDDEN, BF16)], tm)
    ffo = _mm("ffn_down", f, wf['ffn_w2'], 'nn')
    h2, hp = _tok_fwd("ffn_res", _f_ffn_res, [h1, ffo], [norm_ple], [(x.shape[1], F32), (x.shape[1], BF16)], tm)
    gpre = _mm("ple_gate", hp, wf['ple_gate_w'], 'nn')
    pu = _mm("ple_up", p, wf['ple_up_w'], 'nn')

    dh2, dgpre, dpu, g['final_norm'], loss = _tok_bwd(
        "loss", _f_loss, [h2, gpre, pu, target], [final_norm], [None],
        [F32, BF16, BF16, None], [True], tm, acc_out=0)
    g['ple_gate_w'] = _mm("d_ple_gate_w", hp, dgpre, 'tn', out_dtype=WIRE)
    g['ple_up_w'] = _mm("d_ple_up_w", p, dpu, 'tn', out_dtype=WIRE)
    dhp = _mm("d_hp", dgpre, wf['ple_gate_w'], 'nt')
    dh1, dffo, g['norm_ple'] = _tok_bwd("ffn_res_bwd", _f_ffn_res, [h1, ffo], [norm_ple], [dh2, dhp],
                                        [F32, BF16], [True], tm)
    g['ffn_w2'] = _mm("d_ffn_w2", f, dffo, 'tn', out_dtype=WIRE)
    df = _mm("d_f", dffo, wf['ffn_w2'], 'nt', out_dtype=BF16)
    (da13,) = _tok_bwd("ffn_act_bwd", _f_ffn_act, [a13], [], [df], [BF16], [], tm)
    dw13 = _mm("d_ffn_w13", hn, da13, 'tn', out_dtype=WIRE)
    g['ffn_w1'], g['ffn_w3'] = dw13[:, :FFN_HIDDEN], dw13[:, FFN_HIDDEN:]
    dhn = _mm("d_hn", da13, w13, 'nt')
    dx_a, dmixed, g['norm_ffn'] = _tok_bwd("mix_res_bwd", _f_mix_res, [x, mixed], [norm_ffn], [dh1, dhn],
                                           [F32, BF16], [True], tm)
    g['w_out'] = _mm("d_w_out", mixcat, dmixed, 'tn', out_dtype=WIRE)
    dmixcat = _mm("d_mixcat", dmixed, wf['w_out'], 'nt')
    ds5_out, drw_out = dmixcat[:, :S5_WIDTH], dmixcat[:, S5_WIDTH:]

    dy_wkv, dr_b, dkp_b, dv_b, dgate, g['rw_ln_w'], g['rw_ln_b'], g['rw_r_k'] = _tok_bwd(
        "rw_post_bwd", _f_rw_post, post_toks, post_par, [drw_out], [F32] * 5, [True] * 3, tm)
    late_exchange = _exchange_plan([_to_shard_major(g[n], SHARDED[n]).astype(WIRE) for n in LATE_NAMES])
    dseqs, late_pieces = _wkv_bwd(seqs, ck, dy_wkv, late_exchange)
    pre_cots = [(dseqs[0], dr_b), (dseqs[1],), (dseqs[2], dkp_b), (dseqs[3], dv_b), (dseqs[4],), (dseqs[5],),
                (dgate,)]
    dz, *dpre = _rw_pre_bwd(z, pre_par, pre_cots, tm)
    for n, d in zip(['rw_shift_mu', 'rw_w0', 'rw_w2', 'rw_a0', 'rw_a2', 'rw_g2', 'rw_k_k', 'rw_k_a'], dpre):
        g[n] = d

    dypre, du_a, g['s5_d'], g['s5_glu_w'], g['s5_glu_b'] = _tok_bwd(
        "s5_post_bwd", _f_s5_post, [ypre, u], s5_par, [ds5_out], [F32, F32], [True] * 3, tm)
    dxs = _s5_expand("d_s5_x", dypre, c_blk)
    dc_blk = _s5_block_grad("d_s5_c", dypre, xs)
    dbu, dlam_row = _s5_scan_bwd(dxs, xs_prev, lam_row, tm)
    du_b = _s5_contract("d_s5_u", dbu, b_blk)
    db_blk = _s5_block_grad("d_s5_b", u, dbu)
    dbuild = _small_bwd("s5_build_bwd", _f_s5_build, build_ins, (db_blk, dc_blk))
    lam_cots = (dlam_row[:, :S5_LANES].reshape(gp), dlam_row[:, S5_LANES:].reshape(gp),
                dbuild[0].reshape(gp), dbuild[1].reshape(gp))
    g['s5_lam_re'], g['s5_lam_im'], g['s5_log_step'] = _small_bwd("s5_lam_bwd", _f_s5_lam, lam_ins, lam_cots)
    from_t = lambda a, perm: a.reshape(S5_GROUP, S5_GROUPS, S5_STATE).transpose(perm)
    g['s5_b_re'], g['s5_b_im'] = from_t(dbuild[2], (1, 2, 0)), from_t(dbuild[3], (1, 2, 0))
    g['s5_c_re'], g['s5_c_im'] = from_t(dbuild[4], (1, 0, 2)), from_t(dbuild[5], (1, 0, 2))

    dproj = jnp.concatenate([(du_a + du_b).astype(BF16), dz.astype(BF16)], axis=1)
    g['w_in'] = _mm("d_w_in", xn, dproj, 'tn', out_dtype=WIRE)
    dxn = _mm("d_xn", dproj, wf['w_in'], 'nt')
    grad_x, g['norm_mix'] = _tok_bwd("norm_in_bwd", _f_norm_in, [x], [norm_mix], [dxn], [F32], [True], tm,
                                     add_to=(0, dx_a))
    return loss[0, 0], grad_x, g, late_pieces


def _step(x, p, target, w, m, v):
    shards = {n: _mat(w[n]).astype(BF16) for n in SHARDED_NAMES}
    early = _run_plan("gather_early", _gather_plan([shards[n] for n in EARLY_NAMES]))
    wf = {n: _from_shard_major(a, SHARDED[n]) for n, a in zip(EARLY_NAMES, early)}
    ws = {n: w[n] for n in SMALL_NAMES}

    loss, grad_x, g, late_pieces = _local_step(x[0], p[0, 0], target[0], wf, ws, shards)

    early_plan = _exchange_plan([_to_shard_major(g[n], SHARDED[n]).astype(WIRE) for n in EARLY_NAMES],
                                _pack_small({n: g[n] for n in SMALL_NAMES}))
    *early_pieces, by_dev = _run_plan("exchange_early", early_plan)
    pieces = dict(zip(LATE_NAMES + EARLY_NAMES, list(late_pieces) + early_pieces))
    halves = [_ew("add_devices_" + n, _sum_slots, [pieces[n]], [pieces[n].shape[1:]])[0] for n in SHARDED_NAMES]
    (small_piece,) = _ew("add_devices_small", _sum_slots, [by_dev], [by_dev.shape[1:]])
    both, small_g = _share_cores(halves, small_piece)

    kinds = [{}, {}, {}, {}]
    for n, gn in zip(SHARDED_NAMES, both):
        shard = _mat(w[n]).shape
        res = _ew("adamw_" + n, _adamw, [gn.reshape(shard), _mat(w[n]), _mat(m[n]), _mat(v[n])], [shard] * 3)
        for kind, a in zip(kinds, [gn] + list(res)):
            kind[n] = a.reshape(w[n].shape)
    flat = (8 * SMALL_ROWS, PACK_COLS)
    packed = [_pack_small({n: d[n] for n in SMALL_NAMES}).reshape(flat) for d in (w, m, v)]
    small_res = _ew("adamw_small", _adamw, [small_g.reshape(flat)] + packed, [flat] * 3)
    small_shapes = {n: w[n].shape for n in SMALL_NAMES}
    for kind, a in zip(kinds, [small_g] + list(small_res)):
        kind.update(_unpack_small(a, small_shapes))
    total = lax.psum(loss, ("x", "y", "c"))
    return (total, grad_x[None], *[kind[n] for kind in kinds for n in WEIGHT_NAMES])


def kernel(x, p, norm_mix, w_in, s5_lam_re, s5_lam_im, s5_log_step, s5_b_re, s5_b_im, s5_c_re, s5_c_im, s5_d, s5_glu_w, s5_glu_b, rw_shift_mu, rw_w0, rw_w2, rw_a0, rw_a2, rw_g2, rw_k_k, rw_k_a, rw_r_k, rw_ln_w, rw_ln_b, w_out, norm_ffn, ffn_w1, ffn_w3, ffn_w2, norm_ple, ple_gate_w, ple_up_w, final_norm, loss_target, m_norm_mix, m_w_in, m_s5_lam_re, m_s5_lam_im, m_s5_log_step, m_s5_b_re, m_s5_b_im, m_s5_c_re, m_s5_c_im, m_s5_d, m_s5_glu_w, m_s5_glu_b, m_rw_shift_mu, m_rw_w0, m_rw_w2, m_rw_a0, m_rw_a2, m_rw_g2, m_rw_k_k, m_rw_k_a, m_rw_r_k, m_rw_ln_w, m_rw_ln_b, m_w_out, m_norm_ffn, m_ffn_w1, m_ffn_w3, m_ffn_w2, m_norm_ple, m_ple_gate_w, m_ple_up_w, m_final_norm, v_norm_mix, v_w_in, v_s5_lam_re, v_s5_lam_im, v_s5_log_step, v_s5_b_re, v_s5_b_im, v_s5_c_re, v_s5_c_im, v_s5_d, v_s5_glu_w, v_s5_glu_b, v_rw_shift_mu, v_rw_w0, v_rw_w2, v_rw_a0, v_rw_a2, v_rw_g2, v_rw_k_k, v_rw_k_a, v_rw_r_k, v_rw_ln_w, v_rw_ln_b, v_w_out, v_norm_ffn, v_ffn_w1, v_ffn_w3, v_ffn_w2, v_norm_ple, v_ple_gate_w, v_ple_up_w, v_final_norm):
    args = dict(locals())
    w = {n: args[n] for n in WEIGHT_NAMES}
    m = {n: args["m_" + n] for n in WEIGHT_NAMES}
    v = {n: args["v_" + n] for n in WEIGHT_NAMES}
    return _step(x, p, loss_target, w, m, v)
```

```python
import functools
import math
from typing import Any, Callable, NamedTuple, Sequence

import jax
import jax.numpy as jnp
from jax import lax
from jax.experimental import pallas as pl
from jax.experimental.pallas import tpu as pltpu

F32 = jnp.float32
BF16 = jnp.bfloat16
MESH = pl.DeviceIdType.MESH

S5_WIDTH = 512
RWKV_WIDTH = 512
S5_GROUP = 16
S5_GROUPS = 32
S5_STATE = 64
S5_LANES = S5_GROUPS * S5_STATE
S5_TILE_GROUPS = 8
S5_TILES = S5_GROUPS // S5_TILE_GROUPS
S5_TILE_CH = S5_TILE_GROUPS * S5_GROUP
S5_TILE_LANES = S5_TILE_GROUPS * S5_STATE
HEAD = 64
HEADS = 8
DECAY_LORA = 64
AAA_LORA = 64
GATE_LORA = 128
FFN_HIDDEN = 2816
RMS_EPS = 1e-6
GN_EPS = 64e-5
L2_EPS = 1e-12
ADAM_LR = 0.001
ADAM_B1 = 0.9
ADAM_B2 = 0.999
ADAM_EPS = 1e-08
ADAM_WD = 0.01
ADAM_STEP = 10

WKV_CHUNK = 64
SCAN_UNROLL = 4
WIRE = jnp.bfloat16
WKV_PASSES = 1
VMEM_LIMIT_BYTES = 48 * 1024 * 1024
LANE = 128
PACK_COLS = 1024
SMALL_ROWS = 24

WEIGHT_NAMES = ['norm_mix', 'w_in', 's5_lam_re', 's5_lam_im', 's5_log_step', 's5_b_re', 's5_b_im', 's5_c_re',
                's5_c_im', 's5_d', 's5_glu_w', 's5_glu_b', 'rw_shift_mu', 'rw_w0', 'rw_w2', 'rw_a0', 'rw_a2',
                'rw_g2', 'rw_k_k', 'rw_k_a', 'rw_r_k', 'rw_ln_w', 'rw_ln_b', 'w_out', 'norm_ffn', 'ffn_w1',
                'ffn_w3', 'ffn_w2', 'norm_ple', 'ple_gate_w', 'ple_up_w', 'final_norm']
SHARDED = {'w_in': 1, 's5_glu_w': 0, 'rw_w2': 1, 'rw_a2': 1, 'rw_g2': 1, 'w_out': 0, 'ffn_w1': 1, 'ffn_w3': 1,
           'ffn_w2': 0, 'ple_gate_w': 0, 'ple_up_w': 1}
SHARDED_NAMES = [n for n in WEIGHT_NAMES if n in SHARDED]
LATE_NAMES = ['w_out', 'ffn_w1', 'ffn_w3', 'ffn_w2', 'ple_gate_w', 'ple_up_w']
EARLY_NAMES = [n for n in SHARDED_NAMES if n not in LATE_NAMES]
LATE_GATHER = {'s5_scan': ['w_out', 'ple_gate_w', 'ple_up_w'], 'wkv_fwd': ['ffn_w1', 'ffn_w3'], 'ffn_up': ['ffn_w2']}
SMALL_NAMES = [n for n in WEIGHT_NAMES if n not in SHARDED]


def _params(sem=None):
    return pltpu.CompilerParams(dimension_semantics=sem, vmem_limit_bytes=VMEM_LIMIT_BYTES)


def _tile(n, target):
    best = None
    for d in range(LANE, min(n, target) + 1, LANE):
        if n % d == 0:
            best = d
    return n if best is None else best


_NN = (((1,), (0,)), ((), ()))
_NT = (((1,), (1,)), ((), ()))
_TN = (((0,), (0,)), ((), ()))


def _split(a):
    a = a.astype(F32)
    hi = a.astype(BF16)
    return hi, (a - hi.astype(F32)).astype(BF16)


def _dg(a, b, dims, passes):
    dg = lambda p, q: lax.dot_general(p, q, dims, preferred_element_type=F32)
    if passes == 1:
        return dg(a.astype(BF16), b.astype(BF16))
    bh, bl = _split(b)
    if passes == 2:
        return dg(a.astype(BF16), bh) + dg(a.astype(BF16), bl)
    ah, al = _split(a)
    return dg(ah, bh) + (dg(ah, bl) + dg(al, bh))


_DOT_BWD = {_NN: (("g", "b", _NT), ("a", "g", _TN)),
            _NT: (("g", "b", _NN), ("g", "a", _TN)),
            _TN: (("b", "g", _NT), ("a", "g", _NN))}


@functools.partial(jax.custom_vjp, nondiff_argnums=(2, 3))
def _dot(a, b, dims, passes):
    return _dg(a, b, dims, passes)


def _dot_fwd(a, b, dims, passes):
    return _dg(a, b, dims, passes), (a, b)


def _dot_bwd(dims, passes, res, g):
    env = {"a": res[0], "b": res[1], "g": g}
    return tuple(_dg(env[p], env[q], d, passes) for p, q, d in _DOT_BWD[dims])


_dot.defvjp(_dot_fwd, _dot_bwd)


def _bdot(x, w):
    return _dot(x, w, _NN, 1)


def _fdot(a, b, dims=_NN):
    return _dot(a, b, dims, 3)


@jax.custom_vjp
def _shift_down(z):
    return pltpu.roll(z, 1, 0)


def _shift_down_fwd(z):
    return pltpu.roll(z, 1, 0), None


def _shift_down_bwd(_, g):
    return (pltpu.roll(g, g.shape[0] - 1, 0),)


_shift_down.defvjp(_shift_down_fwd, _shift_down_bwd)


def _head_ones():
    r = lax.broadcasted_iota(jnp.int32, (RWKV_WIDTH, RWKV_WIDTH), 0) // HEAD
    c = lax.broadcasted_iota(jnp.int32, (RWKV_WIDTH, RWKV_WIDTH), 1) // HEAD
    return (r == c).astype(F32)


def _mm(name, a, b, mode, out_dtype=F32, precise=False, tm=1024, tn=1024, tk=1536, plan=None):
    if mode == 'nn':
        (m, k), (_, n) = a.shape, b.shape
    elif mode == 'nt':
        (m, k), (n, _) = a.shape, b.shape
    else:
        (k, m), (_, n) = a.shape, b.shape
    tm, tn, tk = _tile(m, tm), _tile(n, tn), _tile(k, tk)
    nm, nn, nk = m // tm, n // tn, k // tk
    dims = {'nn': _NN, 'nt': _NT, 'tn': _TN}[mode]
    plan = _NO_PLAN if plan is None else plan
    parts, plan_in_specs, plan_out_shape, plan_out_specs, plan_sems = _carry(plan, 2, 1)

    def body(*refs):
        a_ref, b_ref, o_ref = refs[0], refs[1], refs[2 + len(plan.ins)]
        acc_ref = refs[3 + len(plan.ins) + len(plan.out_shape)]
        i, j, kk = pl.program_id(0), pl.program_id(1), pl.program_id(2)

        if plan is not _NO_PLAN:
            pl.when((i == 0) & (j == 0) & (kk == 0))(lambda: plan.start(*parts(refs)))

        @pl.when(kk == 0)
        def _():
            acc_ref[...] = jnp.zeros_like(acc_ref)

        acc_ref[...] += _dg(a_ref[...], b_ref[...], dims, 3 if precise else 1)

        @pl.when(kk == nk - 1)
        def _():
            o_ref[...] = acc_ref[...].astype(o_ref.dtype)

        if plan is not _NO_PLAN:
            pl.when((i == nm - 1) & (j == nn - 1) & (kk == nk - 1))(lambda: plan.wait(*parts(refs)))

    if mode == 'tn':
        a_spec = pl.BlockSpec((tk, tm), lambda i, j, l: (l, i))
    else:
        a_spec = pl.BlockSpec((tm, tk), lambda i, j, l: (i, l))
    if mode == 'nt':
        b_spec = pl.BlockSpec((tn, tk), lambda i, j, l: (j, l))
    else:
        b_spec = pl.BlockSpec((tk, tn), lambda i, j, l: (l, j))
    res = pl.pallas_call(
        body, name=name,
        out_shape=[jax.ShapeDtypeStruct((m, n), out_dtype)] + plan_out_shape,
        grid=(nm, nn, nk),
        in_specs=[a_spec, b_spec] + plan_in_specs,
        out_specs=[pl.BlockSpec((tm, tn), lambda i, j, l: (i, j))] + plan_out_specs,
        scratch_shapes=[pltpu.VMEM((tm, tn), F32)] + plan_sems,
        compiler_params=_params(("parallel", "parallel", "arbitrary") if plan is _NO_PLAN else ("arbitrary",) * 3),
    )(a, b, *plan.ins)
    return res[0] if plan is _NO_PLAN else (res[0], res[1:])


def _mm_tiles(name, a, b, mode, out_shape, grid, a_spec, b_spec, o_spec):
    dims = {'nn': _NN, 'nt': _NT, 'tn': _TN}[mode]
    nk = grid[2]

    def body(a_ref, b_ref, o_ref, acc_ref):
        kk = pl.program_id(2)

        @pl.when(kk == 0)
        def _():
            acc_ref[...] = jnp.zeros_like(acc_ref)

        acc_ref[...] += _dg(a_ref[...], b_ref[...], dims, 1)

        @pl.when(kk == nk - 1)
        def _():
            o_ref[...] = acc_ref[...]

    return pl.pallas_call(
        body, name=name,
        out_shape=jax.ShapeDtypeStruct(out_shape, F32),
        grid=grid, in_specs=[a_spec, b_spec], out_specs=o_spec,
        scratch_shapes=[pltpu.VMEM(o_spec.block_shape, F32)],
        compiler_params=_params(("parallel", "parallel", "arbitrary")),
    )(a, b)


def _s5_expand(name, u, blk, tm=1024):
    t = u.shape[0]
    tm = min(tm, t)
    ch, ln, nt = S5_TILE_CH, S5_TILE_LANES, S5_TILES
    return _mm_tiles(name, u, blk, 'nn', (t, 2 * S5_LANES), (t // tm, 2 * nt, 1),
                     pl.BlockSpec((tm, ch), lambda i, j, l: (i, j % nt)),
                     pl.BlockSpec((ch, ln), lambda i, j, l: (j % nt, j // nt)),
                     pl.BlockSpec((tm, ln), lambda i, j, l: (i, j)))


def _s5_contract(name, x, blk, tm=1024):
    t = x.shape[0]
    tm = min(tm, t)
    ch, ln, nt = S5_TILE_CH, S5_TILE_LANES, S5_TILES
    return _mm_tiles(name, x, blk, 'nt', (t, S5_WIDTH), (t // tm, nt, 2),
                     pl.BlockSpec((tm, ln), lambda i, j, l: (i, j + nt * l)),
                     pl.BlockSpec((ch, ln), lambda i, j, l: (j, l)),
                     pl.BlockSpec((tm, ch), lambda i, j, l: (i, j)))


def _s5_block_grad(name, u, x, tk=1024):
    t = u.shape[0]
    tk = min(tk, t)
    ch, ln, nt = S5_TILE_CH, S5_TILE_LANES, S5_TILES
    return _mm_tiles(name, u, x, 'tn', (S5_WIDTH, 2 * ln), (nt, 2, t // tk),
                     pl.BlockSpec((tk, ch), lambda i, j, l: (l, i)),
                     pl.BlockSpec((tk, ln), lambda i, j, l: (l, i + nt * j)),
                     pl.BlockSpec((ch, ln), lambda i, j, l: (i, j)))


def _full_spec(p):
    nd = p.ndim
    return pl.BlockSpec(p.shape, lambda i, nd=nd: (0,) * nd)


def _tok_fwd(name, fn, toks, params, outs, tm):
    t = toks[0].shape[0]
    nt, npar = len(toks), len(params)

    def body(*refs):
        tv = [r[...].astype(F32) for r in refs[:nt]]
        pv = [r[...].astype(F32) for r in refs[nt:nt + npar]]
        res = fn(*tv, *pv)
        for r, v in zip(refs[nt + npar:], res):
            r[...] = v.astype(r.dtype)

    return pl.pallas_call(
        body, name=name,
        out_shape=[jax.ShapeDtypeStruct((t, w), d) for w, d in outs],
        grid=(t // tm,),
        in_specs=[pl.BlockSpec((tm, a.shape[1]), lambda i: (i, 0)) for a in toks] + [_full_spec(p) for p in params],
        out_specs=[pl.BlockSpec((tm, w), lambda i: (i, 0)) for w, _ in outs],
        compiler_params=_params(("parallel",)),
    )(*toks, *params)


def _tok_bwd(name, fn, toks, params, cots, dtok, dpar, tm, acc_out=None, add_to=None):
    t = toks[0].shape[0]
    nt, npar = len(toks), len(params)
    cot_arrays = [c for c in cots if c is not None]
    ncot = len(cot_arrays)
    extra = [] if add_to is None else [add_to[1]]
    dtok_idx = [i for i, d in enumerate(dtok) if d is not None]
    dpar_idx = [i for i, d in enumerate(dpar) if d]

    def body(*refs):
        pos = 0
        tin = refs[pos:pos + nt]; pos += nt
        pin = refs[pos:pos + npar]; pos += npar
        cin = refs[pos:pos + ncot]; pos += ncot
        ein = refs[pos:pos + len(extra)]; pos += len(extra)
        dto = refs[pos:pos + len(dtok_idx)]; pos += len(dtok_idx)
        dpo = refs[pos:pos + len(dpar_idx)]; pos += len(dpar_idx)
        acc = refs[pos] if acc_out is not None else None
        first = pl.program_id(0) == 0

        tv = [r[...].astype(F32) for r in tin]
        pv = [r[...].astype(F32) for r in pin]
        res, vjp = jax.vjp(fn, *tv, *pv)
        cit = iter(cin)
        cs = tuple(jnp.ones_like(o) if c is None else next(cit)[...].astype(F32) for c, o in zip(cots, res))
        g = vjp(cs)
        for r, i in zip(dto, dtok_idx):
            v = g[i]
            if add_to is not None and add_to[0] == i:
                v = v + ein[0][...].astype(F32)
            r[...] = v.astype(r.dtype)

        @pl.when(first)
        def _():
            for r in dpo:
                r[...] = jnp.zeros_like(r)
            if acc is not None:
                acc[...] = jnp.zeros_like(acc)

        for r, i in zip(dpo, dpar_idx):
            r[...] += g[nt + i]
        if acc is not None:
            acc[...] += res[acc_out]

    out_shape = [jax.ShapeDtypeStruct(toks[i].shape, dtok[i]) for i in dtok_idx]
    out_shape += [jax.ShapeDtypeStruct(params[i].shape, F32) for i in dpar_idx]
    out_specs = [pl.BlockSpec((tm, toks[i].shape[1]), lambda i_: (i_, 0)) for i in dtok_idx]
    out_specs += [_full_spec(params[i]) for i in dpar_idx]
    if acc_out is not None:
        out_shape.append(jax.ShapeDtypeStruct((1, 1), F32))
        out_specs.append(pl.BlockSpec((1, 1), lambda i_: (0, 0)))
    tok_spec = lambda a: pl.BlockSpec((tm, a.shape[1]), lambda i_: (i_, 0))
    return pl.pallas_call(
        body, name=name,
        out_shape=out_shape,
        grid=(t // tm,),
        in_specs=[tok_spec(a) for a in toks] + [_full_spec(p) for p in params]
        + [tok_spec(c) for c in cot_arrays] + [tok_spec(e) for e in extra],
        out_specs=out_specs,
        compiler_params=_params(("arbitrary",)),
    )(*toks, *params, *cot_arrays, *extra)


def _small_fwd(name, fn, ins, outs):
    n = len(ins)

    def body(*refs):
        res = fn(*[r[...] for r in refs[:n]])
        for r, v in zip(refs[n:], res):
            r[...] = v.astype(r.dtype)

    return pl.pallas_call(
        body, name=name,
        out_shape=[jax.ShapeDtypeStruct(s, d) for s, d in outs],
        compiler_params=_params(),
    )(*ins)


def _small_bwd(name, fn, ins, cots):
    n = len(ins)

    def body(*refs):
        _, vjp = jax.vjp(fn, *[r[...] for r in refs[:n]])
        g = vjp(tuple(r[...] for r in refs[n:n + len(cots)]))
        for r, v in zip(refs[n + len(cots):], g):
            r[...] = v

    return pl.pallas_call(
        body, name=name,
        out_shape=[jax.ShapeDtypeStruct(a.shape, F32) for a in ins],
        compiler_params=_params(),
    )(*ins, *cots)


def _rms(x, g):
    return x * lax.rsqrt(jnp.mean(x * x, axis=-1, keepdims=True) + RMS_EPS) * g


def _f_norm_in(x, g):
    return (_rms(x, g),)


def _f_mix_res(x, mixed, g):
    h1 = x + mixed
    return h1, _rms(h1, g)


def _f_ffn_act(a13):
    a1, a3 = a13[:, :FFN_HIDDEN], a13[:, FFN_HIDDEN:]
    return (jax.nn.silu(a1) * a3,)


def _f_ffn_res(h1, ffo, g):
    h2 = h1 + ffo
    return h2, _rms(h2, g)


def _f_loss(h2, gpre, pu, target, g):
    h3 = h2 + jax.nn.sigmoid(gpre) * pu
    y = _rms(h3, g)
    err = jnp.square(y - target)
    return (0.5 * jnp.sum(jnp.mean(err, axis=-1, keepdims=True), axis=0, keepdims=True),)


def _f_s5_post(ypre, u, d, glu_w, glu_b):
    z = jax.nn.gelu(ypre + u * d)
    return (z * jax.nn.sigmoid(_bdot(z, glu_w) + glu_b),)


def _softplus(x):
    return jnp.maximum(x, 0.0) + jnp.log(1.0 + jnp.exp(-jnp.abs(x)))


def _f_rw_pre(z, carry, shift_mu, w0, w2, a0, a2, g2, k_k, k_a):
    rw = RWKV_WIDTH
    first_row = lax.broadcasted_iota(jnp.int32, z.shape, 0) == 0
    prev = jnp.where(first_row, carry, _shift_down(z))
    zs = z + (prev - z) * shift_mu
    o1, o2 = 3 * rw + DECAY_LORA, 3 * rw + DECAY_LORA + AAA_LORA
    r, k, v = zs[:, :rw], zs[:, rw:2 * rw], zs[:, 2 * rw:3 * rw]
    wl, al, gl = zs[:, 3 * rw:o1], zs[:, o1:o2], zs[:, o2:]
    w = -_softplus(-(w0 + _bdot(jnp.tanh(wl), w2))) - 0.5
    log_decay = -jnp.exp(w)
    a = jax.nn.sigmoid(a0 + _bdot(al, a2))
    g = _bdot(jax.nn.sigmoid(gl), g2)
    kk = k * k_k
    norm = jnp.sqrt(_fdot(kk * kk, _head_ones()))
    kk = kk / jnp.maximum(norm, L2_EPS)
    kp = k * (1.0 + (a - 1.0) * k_a)
    return r, log_decay, kp, v, -kk, kk * a, g


def _f_rw_post(y, r, kp, v, g, ln_w, ln_b, r_k):
    ones = _head_ones()
    yc = y - _fdot(y, ones) * (1.0 / HEAD)
    var = _fdot(yc * yc, ones) * (1.0 / HEAD)
    yn = yc * lax.rsqrt(var + GN_EPS) * ln_w + ln_b
    bonus = _fdot(r * kp * r_k, ones) * v
    return ((yn + bonus) * g,)


def _f_s5_lam(lam_re, lam_im, log_step):
    step = jnp.exp(log_step)
    dr, di = lam_re * step, lam_im * step
    e = jnp.exp(dr)
    lbr, lbi = e * jnp.cos(di), e * jnp.sin(di)
    nr, ni = lbr - 1.0, lbi
    den = lam_re * lam_re + lam_im * lam_im
    return lbr, lbi, (nr * lam_re + ni * lam_im) / den, (ni * lam_re - nr * lam_im) / den


def _f_s5_build(coef_r, coef_i, btr, bti, ctr, cti):
    bbr = coef_r * btr - coef_i * bti
    bbi = coef_r * bti + coef_i * btr
    shape = (S5_WIDTH, S5_TILE_LANES)
    rows = (lax.broadcasted_iota(jnp.int32, shape, 0) % S5_TILE_CH) // S5_GROUP
    cols = lax.broadcasted_iota(jnp.int32, shape, 1) // S5_STATE
    mask = (rows == cols).astype(F32)

    def blocks(m):
        per_tile = [m[:, S5_TILE_LANES * i:S5_TILE_LANES * (i + 1)] for i in range(S5_TILES)]
        return jnp.concatenate([t for t in per_tile for _ in range(S5_TILE_GROUPS)], axis=0) * mask

    return (jnp.concatenate([blocks(bbr), blocks(bbi)], axis=1),
            jnp.concatenate([blocks(ctr), -blocks(cti)], axis=1))


HALO = 8


def _rw_pre_specs(z, params, tm, order):
    halo_blocks = tm // HALO
    return ([pl.BlockSpec((tm, z.shape[1]), lambda i: (order(i), 0)),
             pl.BlockSpec((HALO, z.shape[1]), lambda i: (jnp.maximum(order(i) * halo_blocks - 1, 0), 0))]
            + [_full_spec(p) for p in params])


def _rw_pre_fwd(z, params, tm):
    t = z.shape[0]
    npar = len(params)

    def body(z_ref, halo_ref, *refs):
        carry = jnp.where(pl.program_id(0) == 0, 0.0, halo_ref[pl.ds(HALO - 1, 1), :])
        res = _f_rw_pre(z_ref[...], carry, *[r[...].astype(F32) for r in refs[:npar]])
        for r, v in zip(refs[npar:], res):
            r[...] = v

    return pl.pallas_call(
        body, name="rw_pre",
        out_shape=[jax.ShapeDtypeStruct((t, RWKV_WIDTH), F32)] * 7,
        grid=(t // tm,),
        in_specs=_rw_pre_specs(z, params, tm, lambda i: i),
        out_specs=[pl.BlockSpec((tm, RWKV_WIDTH), lambda i: (i, 0))] * 7,
        compiler_params=_params(("parallel",)),
    )(z, z, *params)


def _rw_pre_bwd(z, params, cots, tm):
    t = z.shape[0]
    nt = t // tm
    npar = len(params)
    order = lambda i: nt - 1 - i
    flat_cots = [a for group in cots for a in group]
    ncot = len(flat_cots)

    def body(z_ref, halo_ref, *refs):
        pin, cin = refs[:npar], list(refs[npar:npar + ncot])
        dz_ref = refs[npar + ncot]
        dpo = refs[npar + ncot + 1:npar + ncot + 1 + npar]
        dcarry_ref = refs[npar + ncot + 1 + npar]
        i = pl.program_id(0)

        @pl.when(i == 0)
        def _():
            dcarry_ref[...] = jnp.zeros_like(dcarry_ref)
            for r in dpo:
                r[...] = jnp.zeros_like(r)

        carry = jnp.where(i == nt - 1, 0.0, halo_ref[pl.ds(HALO - 1, 1), :])
        _, vjp = jax.vjp(_f_rw_pre, z_ref[...], carry, *[r[...].astype(F32) for r in pin])
        g = vjp(tuple(sum(cin.pop(0)[...] for _ in group) for group in cots))
        last_row = lax.broadcasted_iota(jnp.int32, z_ref.shape, 0) == tm - 1
        dz_ref[...] = g[0] + jnp.where(last_row, dcarry_ref[...], 0.0)
        dcarry_ref[...] = g[1]
        for r, v in zip(dpo, g[2:]):
            r[...] += v

    tok = lambda w: pl.BlockSpec((tm, w), lambda i: (order(i), 0))
    return pl.pallas_call(
        body, name="rw_pre_bwd",
        out_shape=[jax.ShapeDtypeStruct(z.shape, F32)] + [jax.ShapeDtypeStruct(p.shape, F32) for p in params],
        grid=(nt,),
        in_specs=_rw_pre_specs(z, params, tm, order) + [tok(RWKV_WIDTH)] * ncot,
        out_specs=[tok(z.shape[1])] + [_full_spec(p) for p in params],
        scratch_shapes=[pltpu.VMEM((1, z.shape[1]), F32)],
        compiler_params=_params(("arbitrary",)),
    )(z, z, *params, *flat_cots)


def _s5_scan(bu, lam, tm, plan):
    t, w = bu.shape
    h = w // 2
    nt = t // tm
    parts, plan_in_specs, plan_out_shape, plan_out_specs, plan_sems = _carry(plan, 2, 1)

    def body(*refs):
        bu_ref, lam_ref = refs[:2]
        xb_ref = refs[2 + len(plan.ins)]
        carry_ref = refs[3 + len(plan.ins) + len(plan.out_shape)]
        x_ref, refs = refs[-1], refs[:-1]

        @pl.when(pl.program_id(0) == 0)
        def _():
            carry_ref[...] = jnp.zeros_like(carry_ref)
            plan.start(*parts(refs))

        lr, li = lam_ref[:, :h], lam_ref[:, h:]

        def step(s, c):
            cr, ci = c
            row = pl.ds(s, 1)
            nr = lr * cr - li * ci + bu_ref[row, :h]
            ni = lr * ci + li * cr + bu_ref[row, h:]
            x_ref[row, :h] = nr
            x_ref[row, h:] = ni
            return nr, ni

        cr, ci = lax.fori_loop(0, tm, step, (carry_ref[:, :h], carry_ref[:, h:]), unroll=SCAN_UNROLL)
        carry_ref[:, :h] = cr
        carry_ref[:, h:] = ci
        xb_ref[...] = x_ref[...].astype(BF16)

        @pl.when(pl.program_id(0) == nt - 1)
        def _():
            plan.wait(*parts(refs))

    spec = pl.BlockSpec((tm, w), lambda i: (i, 0))
    res = pl.pallas_call(
        body, name="s5_scan",
        out_shape=[jax.ShapeDtypeStruct((t, w), BF16)] + plan_out_shape,
        grid=(nt,),
        in_specs=[spec, pl.BlockSpec((1, w), lambda i: (0, 0))] + plan_in_specs,
        out_specs=[spec] + plan_out_specs,
        scratch_shapes=[pltpu.VMEM((1, w), F32)] + plan_sems + [pltpu.VMEM((tm, w), F32)],
        compiler_params=_params(("arbitrary",)),
    )(bu, lam, *plan.ins)
    return res[0], res[1:]


def _s5_scan_bwd(dx, xb, lam, tm):
    t, w = dx.shape
    h = w // 2
    nt = t // tm
    halo = BF16_SUBLANES

    def body(dx_ref, xb_ref, halo_ref, lam_ref, dbu_out_ref, dlam_ref, carry_ref, dbu_ref):
        @pl.when(pl.program_id(0) == 0)
        def _():
            carry_ref[...] = jnp.zeros_like(carry_ref)
            dlam_ref[...] = jnp.zeros_like(dlam_ref)

        lr, li = lam_ref[:, :h], lam_ref[:, h:]

        def step(s, c):
            cr, ci = c
            row = pl.ds(tm - 1 - s, 1)
            nr = lr * cr + li * ci + dx_ref[row, :h]
            ni = lr * ci - li * cr + dx_ref[row, h:]
            dbu_ref[row, :h] = nr
            dbu_ref[row, h:] = ni
            return nr, ni

        cr, ci = lax.fori_loop(0, tm, step, (carry_ref[:, :h], carry_ref[:, h:]), unroll=SCAN_UNROLL)
        carry_ref[:, :h] = cr
        carry_ref[:, h:] = ci
        halo_rows = lax.broadcasted_iota(jnp.int32, (halo, w), 0)
        before = jnp.sum(jnp.where(halo_rows == halo - 1, halo_ref[...].astype(F32), 0.0), axis=0, keepdims=True)
        before = jnp.where(pl.program_id(0) == nt - 1, 0.0, before)
        first_row = lax.broadcasted_iota(jnp.int32, (tm, w), 0) == 0
        xp = jnp.where(first_row, before, pltpu.roll(xb_ref[...].astype(F32), 1, 0))
        gr, gi = dbu_ref[:, :h], dbu_ref[:, h:]
        pr, pi_ = xp[:, :h], xp[:, h:]
        dlam_ref[:, :h] += jnp.sum(gr * pr + gi * pi_, axis=0, keepdims=True)
        dlam_ref[:, h:] += jnp.sum(gi * pr - gr * pi_, axis=0, keepdims=True)
        dbu_out_ref[...] = dbu_ref[...].astype(BF16)

    spec = pl.BlockSpec((tm, w), lambda i: (nt - 1 - i, 0))
    halo_spec = pl.BlockSpec((halo, w), lambda i: (jnp.maximum((nt - 1 - i) * (tm // halo) - 1, 0), 0))
    row_spec = pl.BlockSpec((1, w), lambda i: (0, 0))
    return pl.pallas_call(
        body, name="s5_scan_bwd",
        out_shape=[jax.ShapeDtypeStruct((t, w), BF16), jax.ShapeDtypeStruct((1, w), F32)],
        grid=(nt,),
        in_specs=[spec, spec, halo_spec, row_spec],
        out_specs=[spec, row_spec],
        scratch_shapes=[pltpu.VMEM((1, w), F32), pltpu.VMEM((tm, w), F32)],
        compiler_params=_params(("arbitrary",)),
    )(dx, xb, xb, lam)


def _wkv_chunks(s0, r, lw, k, v, a, b):
    c = r[0].shape[0]
    row = lax.broadcasted_iota(jnp.int32, (c, c), 0)
    col = lax.broadcasted_iota(jnp.int32, (c, c), 1)
    incl, strict = col <= row, col < row
    tri = incl.astype(F32)
    eye = (row == col).astype(F32)
    each = lambda f, *xs: [f(*t) for t in zip(*xs)]
    stack = lambda p, q: jnp.concatenate([p, q], axis=0)
    dot = lambda p, q, dims=_NN: _dot(p, q, dims, WKV_PASSES)
    lc = each(lambda l: _dot(tri, l, _NN, 2), lw)
    e_neg = each(lambda l: jnp.exp(-l), lc)
    ar = each(lambda x, z, l, w: stack(x * jnp.exp(l - w), z * jnp.exp(l)), a, r, lc, lw)
    bk = each(lambda x, z, e: stack(x * e, z * e), b, k, e_neg)
    m = each(lambda p, q: dot(p, q, _NT), ar, bk)
    mab = each(lambda q: jnp.where(strict, q[:c, :c], 0.0), m)
    mak_mrk = each(lambda q: stack(jnp.where(strict, q[:c, c:], 0.0), jnp.where(incl, q[c:, c:], 0.0)), m)
    mrb = each(lambda q: jnp.where(incl, q[c:, :c], 0.0), m)
    xy = each(lambda p, s, q, z: dot(p, s, _NT) + dot(q, z), ar, s0, mak_mrk, v)
    inv = each(lambda q: eye + q, mab)
    pw = each(lambda q: dot(q, q), mab)
    for _ in range(int(math.log2(c)) - 2):
        both = each(lambda i, q: dot(stack(i, q), q), inv, pw)
        inv = each(lambda i, q: i + q[:c], inv, both)
        pw = each(lambda q: q[c:], both)
    inv = each(lambda i, q: i + dot(i, q), inv, pw)
    u = each(lambda i, q: dot(i, q[:c]), inv, xy)
    y = each(lambda q, z, p: q[c:] + dot(z, p), xy, mrb, u)
    e_tot = each(lambda l: jnp.exp(jnp.sum(l, axis=0, keepdims=True)), lw)
    s1 = each(lambda s, p, z, q, e: (s + dot(stack(p, z), q, _TN)) * e, s0, u, v, bk, e_tot)
    return y, s1


def _carry(plan, n_args, n_outs):
    n_in, n_out = len(plan.ins), len(plan.out_shape)

    def parts(refs):
        base = n_args + n_in + n_outs
        return refs[n_args:n_args + n_in], refs[base:base + n_out], refs[base + n_out + 1:]

    return parts, [HBM_SPEC] * n_in, list(plan.out_shape), [HBM_SPEC] * n_out, list(plan.sems)


def _head_cols(ref):
    return tuple(ref[:, h * HEAD:(h + 1) * HEAD] for h in range(HEADS))


def _wkv_fwd(seqs, plan):
    t, w = seqs[0].shape
    c, n = WKV_CHUNK, HEAD
    nc = t // c
    parts, plan_in_specs, plan_out_shape, plan_out_specs, plan_sems = _carry(plan, 6, 2)

    def body(*refs):
        ins, (y_ref, ck_ref) = refs[:6], refs[6 + len(plan.ins):8 + len(plan.ins)]
        s_ref = refs[8 + len(plan.ins) + len(plan.out_shape)]

        @pl.when(pl.program_id(0) == 0)
        def _():
            s_ref[...] = jnp.zeros_like(s_ref)
            plan.start(*parts(refs))

        s0 = tuple(s_ref[h] for h in range(HEADS))
        ys, s1 = _wkv_chunks(s0, *[_head_cols(r) for r in ins])
        for h in range(HEADS):
            ck_ref[0, h] = s0[h]
            y_ref[:, h * n:(h + 1) * n] = ys[h]
            s_ref[h] = s1[h]

        @pl.when(pl.program_id(0) == nc - 1)
        def _():
            plan.wait(*parts(refs))

    spec = pl.BlockSpec((c, w), lambda i: (i, 0))
    res = pl.pallas_call(
        body, name="wkv_fwd",
        out_shape=[jax.ShapeDtypeStruct((t, w), F32), jax.ShapeDtypeStruct((nc, HEADS, n, n), F32)] + plan_out_shape,
        grid=(nc,),
        in_specs=[spec] * 6 + plan_in_specs,
        out_specs=[spec, pl.BlockSpec((1, HEADS, n, n), lambda i: (i, 0, 0, 0))] + plan_out_specs,
        scratch_shapes=[pltpu.VMEM((HEADS, n, n), F32)] + plan_sems,
        compiler_params=_params(("arbitrary",)),
    )(*seqs, *plan.ins)
    return res[0], res[1], res[2:]


def _wkv_bwd(seqs, ck, dy, plan):
    t, w = seqs[0].shape
    c, n = WKV_CHUNK, HEAD
    nc = t // c
    parts, plan_in_specs, plan_out_shape, plan_out_specs, plan_sems = _carry(plan, 8, 6)

    def body(*refs):
        ins, ck_ref, dy_ref = refs[:6], refs[6], refs[7]
        outs = refs[8 + len(plan.ins):14 + len(plan.ins)]
        ds_ref = refs[14 + len(plan.ins) + len(plan.out_shape)]

        @pl.when(pl.program_id(0) == 0)
        def _():
            ds_ref[...] = jnp.zeros_like(ds_ref)
            plan.start(*parts(refs))

        s0 = tuple(ck_ref[0, h] for h in range(HEADS))
        _, vjp = jax.vjp(_wkv_chunks, s0, *[_head_cols(r) for r in ins])
        g = vjp((list(_head_cols(dy_ref)), [ds_ref[h] for h in range(HEADS)]))
        for h in range(HEADS):
            ds_ref[h] = g[0][h]
            for o, d in zip(outs, g[1:]):
                o[:, h * n:(h + 1) * n] = d[h]

        @pl.when(pl.program_id(0) == nc - 1)
        def _():
            plan.wait(*parts(refs))

    spec = pl.BlockSpec((c, w), lambda i: (nc - 1 - i, 0))
    res = pl.pallas_call(
        body, name="wkv_bwd",
        out_shape=[jax.ShapeDtypeStruct((t, w), F32)] * 6 + plan_out_shape,
        grid=(nc,),
        in_specs=[spec] * 6 + [pl.BlockSpec((1, HEADS, n, n), lambda i: (nc - 1 - i, 0, 0, 0)), spec] + plan_in_specs,
        out_specs=[spec] * 6 + plan_out_specs,
        scratch_shapes=[pltpu.VMEM((HEADS, n, n), F32)] + plan_sems,
        compiler_params=_params(("arbitrary",)),
    )(*seqs, ck, dy, *plan.ins)
    return res[:6], res[6:]


def _coords():
    return lax.axis_index("x"), lax.axis_index("y"), lax.axis_index("c")


def _flip(v, f):
    return 1 - v if f else v


_CHIP_FLIPS = [(1, 0), (0, 1), (1, 1)]
_DEV_FLIPS = [(fx, fy, fc) for fx in (0, 1) for fy in (0, 1) for fc in (0, 1) if (fx, fy, fc) != (0, 0, 0)]
HBM_SPEC = pl.BlockSpec(memory_space=pl.ANY)


def _chip_peer(k, x, y):
    fx, fy = _CHIP_FLIPS[k]
    return _flip(x, fx), _flip(y, fy)


def _dev_peer(k, x, y, c):
    fx, fy, fc = _DEV_FLIPS[k]
    return _flip(x, fx), _flip(y, fy), _flip(c, fc)


def _rows_of_core(ref, core):
    h = ref.shape[-2] // 2
    rows = pl.ds(pl.multiple_of(core * h, 8), h)
    return ref.at[rows, :] if len(ref.shape) == 2 else ref.at[:, rows, :]


class _Plan(NamedTuple):
    ins: Sequence[Any]
    out_shape: Sequence[Any]
    sems: Sequence[Any]
    start: Callable
    wait: Callable


_NO_PLAN = _Plan([], [], [], lambda *_: None, lambda *_: None)


def _run_plan(name, plan):
    n_in, n_out = len(plan.ins), len(plan.out_shape)

    def body(*refs):
        parts = refs[:n_in], refs[n_in:n_in + n_out], refs[n_in + n_out:]
        plan.start(*parts)
        plan.wait(*parts)

    return pl.pallas_call(
        body, name=name, out_shape=list(plan.out_shape),
        in_specs=[HBM_SPEC] * n_in, out_specs=[HBM_SPEC] * n_out, scratch_shapes=list(plan.sems),
    )(*plan.ins)


def _gather_plan(shards):
    n = len(shards)

    def copies(srcs, outs, sems):
        send_sems, recv_sems, local_sems = sems
        x, y, c = _coords()
        me = 2 * x + y

        def remote(i, k, arriving):
            px, py = _chip_peer(k, x, y)
            return pltpu.make_async_remote_copy(
                src_ref=srcs[i], dst_ref=outs[i].at[2 * px + py if arriving else me],
                send_sem=send_sems.at[i, k], recv_sem=recv_sems.at[i, k],
                device_id=(px, py, c), device_id_type=MESH)

        own = [pltpu.make_async_copy(srcs[i], outs[i].at[me], local_sems.at[i]) for i in range(n)]
        pairs = [(i, k) for k in range(3) for i in range(n)]
        return own, [remote(i, k, False) for i, k in pairs], [remote(i, k, True) for i, k in pairs]

    return _Plan(
        ins=shards, out_shape=[jax.ShapeDtypeStruct((4,) + s.shape, s.dtype) for s in shards],
        sems=[pltpu.SemaphoreType.DMA((n, 3)), pltpu.SemaphoreType.DMA((n, 3)), pltpu.SemaphoreType.DMA((n,))],
        start=functools.partial(_start_copies, copies), wait=functools.partial(_wait_copies, copies))


def _start_copies(copies, ins, outs, sems):
    own, sends, _ = copies(ins, outs, sems)
    for cp in own + sends:
        cp.start()


def _wait_copies(copies, ins, outs, sems):
    own, sends, arrivals = copies(ins, outs, sems)
    for cp in arrivals:
        cp.wait_recv()
    for cp in sends:
        cp.wait_send()
    for cp in own:
        cp.wait()


def _exchange_plan(gs, small=None):
    n = len(gs)
    arrays = list(gs) + ([] if small is None else [small])

    def copies(srcs, outs, sems):
        send_sems, recv_sems, local_sems = sems
        x, y, c = _coords()
        me = 4 * x + 2 * y + c

        def piece(i, px, py, pc):
            if i == n:
                return srcs[i].at[4 * px + 2 * py + pc]
            return _rows_of_core(srcs[i].at[2 * px + py], pc)

        def remote(i, k, arriving):
            px, py, pc = _dev_peer(k, x, y, c)
            return pltpu.make_async_remote_copy(
                src_ref=piece(i, px, py, pc), dst_ref=outs[i].at[4 * px + 2 * py + pc if arriving else me],
                send_sem=send_sems.at[i, k], recv_sem=recv_sems.at[i, k],
                device_id=(px, py, pc), device_id_type=MESH)

        own = [pltpu.make_async_copy(piece(i, x, y, c), outs[i].at[me], local_sems.at[i]) for i in range(len(arrays))]
        pairs = [(i, k) for k in range(7) for i in range(len(arrays))]
        return own, [remote(i, k, False) for i, k in pairs], [remote(i, k, True) for i, k in pairs]

    out_shape = [jax.ShapeDtypeStruct((8, g.shape[1] // 2, g.shape[2]), g.dtype) for g in gs]
    if small is not None:
        out_shape.append(jax.ShapeDtypeStruct(small.shape, small.dtype))
    m = len(arrays)
    return _Plan(
        ins=arrays, out_shape=out_shape,
        sems=[pltpu.SemaphoreType.DMA((m, 7)), pltpu.SemaphoreType.DMA((m, 7)), pltpu.SemaphoreType.DMA((m,))],
        start=functools.partial(_start_copies, copies), wait=functools.partial(_wait_copies, copies))


def _share_cores(halves, small):
    n = len(halves)

    def body(*refs):
        srcs, small_src, outs, small_out = refs[:n], refs[n], refs[n + 1:2 * n + 1], refs[2 * n + 1]
        mine, theirs = refs[2 * n + 2:3 * n + 2], refs[3 * n + 2:4 * n + 2]
        send_sems, recv_sems, ssend, srecv, local_sems = refs[4 * n + 2:]
        x, y, c = _coords()
        me = 4 * x + 2 * y + c

        def big(i):
            return pltpu.make_async_remote_copy(
                src_ref=mine[i], dst_ref=theirs[i], send_sem=send_sems.at[i], recv_sem=recv_sems.at[i],
                device_id=(x, y, 1 - c), device_id_type=MESH)

        def tiny(k, arriving):
            px, py, pc = _dev_peer(k, x, y, c)
            return pltpu.make_async_remote_copy(
                src_ref=small_src, dst_ref=small_out.at[4 * px + 2 * py + pc if arriving else me],
                send_sem=ssend.at[k], recv_sem=srecv.at[k], device_id=(px, py, pc), device_id_type=MESH)

        small_sends = [tiny(k, False) for k in range(7)]
        own_small = pltpu.make_async_copy(small_src, small_out.at[me], local_sems.at[2 * n])
        stage = [pltpu.make_async_copy(srcs[i], mine[i], local_sems.at[i]) for i in range(n)]
        for cp in small_sends + [own_small] + stage:
            cp.start()
        sends = []
        for i in range(n):
            stage[i].wait()
            sends.append(big(i))
            sends[-1].start()
        store = [pltpu.make_async_copy(mine[i], outs[i].at[c], local_sems.at[i]) for i in range(n)]
        for cp in store:
            cp.start()
        for i in range(n):
            big(i).wait_recv()
            store.append(pltpu.make_async_copy(theirs[i], outs[i].at[1 - c], local_sems.at[n + i]))
            store[-1].start()
        for k in range(7):
            tiny(k, True).wait_recv()
        for cp in sends + small_sends:
            cp.wait_send()
        for cp in store + [own_small]:
            cp.wait()

    staged = [pltpu.VMEM(s.shape, s.dtype) for s in halves]
    res = pl.pallas_call(
        body, name="share_cores",
        out_shape=[jax.ShapeDtypeStruct((2,) + s.shape, s.dtype) for s in halves]
        + [jax.ShapeDtypeStruct((8,) + small.shape, small.dtype)],
        in_specs=[HBM_SPEC] * (n + 1), out_specs=[HBM_SPEC] * (n + 1),
        scratch_shapes=staged + staged + [
            pltpu.SemaphoreType.DMA((n,)), pltpu.SemaphoreType.DMA((n,)),
            pltpu.SemaphoreType.DMA((7,)), pltpu.SemaphoreType.DMA((7,)),
            pltpu.SemaphoreType.DMA((2 * n + 1,))],
        compiler_params=pltpu.CompilerParams(vmem_limit_bytes=VMEM_LIMIT_BYTES),
    )(*halves, small)
    return res[:n], res[n]


BF16_SUBLANES = 16


def _row_tile(n, target, step=BF16_SUBLANES):
    return max([d for d in range(step, min(n, target) + 1, step) if n % d == 0] or [n])


def _ew(name, fn, ins, outs, block_bytes=1 << 20):
    rows, cols = ins[0].shape[-2:]
    lead = max(math.prod(a.shape[:-2]) for a in ins)
    tr = _row_tile(rows, max(8, block_bytes // (4 * cols * lead)))
    n = len(ins)

    def spec(shape):
        if len(shape) == 2:
            return pl.BlockSpec((tr, cols), lambda i: (i, 0))
        return pl.BlockSpec((shape[0], tr, cols), lambda i: (0, i, 0))

    def body(*refs):
        res = fn(*[r[...] for r in refs[:n]])
        for r, v in zip(refs[n:], res):
            r[...] = v

    return pl.pallas_call(
        body, name=name,
        out_shape=[jax.ShapeDtypeStruct(s, F32) for s in outs],
        grid=(rows // tr,),
        in_specs=[spec(a.shape) for a in ins],
        out_specs=[spec(s) for s in outs],
        compiler_params=_params(("parallel",)),
    )(*ins)


def _sum_slots(a):
    total = a[0].astype(F32)
    for s in range(1, a.shape[0]):
        total = total + a[s].astype(F32)
    return (total,)


def _adamw(g, w, m, v):
    bc1 = 1.0 - ADAM_B1 ** ADAM_STEP
    bc2 = 1.0 - ADAM_B2 ** ADAM_STEP
    m_new = ADAM_B1 * m + (1.0 - ADAM_B1) * g
    v_new = ADAM_B2 * v + (1.0 - ADAM_B2) * jnp.square(g)
    delta = -ADAM_LR * ((m_new / bc1) / (jnp.sqrt(v_new / bc2) + ADAM_EPS) + ADAM_WD * w)
    return delta, m_new, v_new


def _mat(a):
    return a.reshape(a.shape[-2:])


def _to_shard_major(full, axis):
    rows, cols = full.shape
    if axis == 0:
        return full.reshape(4, rows // 4, cols)
    return full.reshape(rows, 4, cols // 4).transpose(1, 0, 2)


def _from_shard_major(a, axis):
    _, r, cs = a.shape
    if axis == 0:
        return a.reshape(4 * r, cs)
    return a.transpose(1, 0, 2).reshape(r, 4 * cs)


def _pack_small(arrays):
    flat = jnp.concatenate([arrays[n].reshape(-1) for n in SMALL_NAMES])
    flat = jnp.pad(flat, (0, 8 * SMALL_ROWS * PACK_COLS - flat.shape[0]))
    return flat.reshape(8, SMALL_ROWS, PACK_COLS)


def _unpack_small(packed, shapes):
    flat = packed.reshape(-1)
    out, off = {}, 0
    for n in SMALL_NAMES:
        size = math.prod(shapes[n])
        out[n] = flat[off:off + size].reshape(shapes[n])
        off += size
    return out


def _row(a):
    return a.reshape(1, -1)


def _local_step(x, p, target, wf, ws, late_shards):
    wf = dict(wf)
    t = x.shape[0]
    tm = min(256, t)
    g = {}

    lam_re, lam_im = ws['s5_lam_re'].reshape(S5_GROUPS, S5_STATE), ws['s5_lam_im'].reshape(S5_GROUPS, S5_STATE)
    log_step = ws['s5_log_step'].reshape(S5_GROUPS, 1)
    gp = (S5_GROUPS, S5_STATE)
    lam_ins = (lam_re, lam_im, log_step)
    lbr, lbi, cfr, cfi = _small_fwd("s5_lam", _f_s5_lam, lam_ins, [(gp, F32)] * 4)
    lam_row = jnp.concatenate([_row(lbr), _row(lbi)], axis=1)
    to_t = lambda a, perm: a.reshape((S5_GROUPS,) + a.shape[-2:]).transpose(perm).reshape(S5_GROUP, S5_LANES)
    build_ins = (_row(cfr), _row(cfi), to_t(ws['s5_b_re'], (2, 0, 1)), to_t(ws['s5_b_im'], (2, 0, 1)),
                 to_t(ws['s5_c_re'], (1, 0, 2)), to_t(ws['s5_c_im'], (1, 0, 2)))
    block_shape = (S5_WIDTH, 2 * S5_TILE_LANES)
    b_blk, c_blk = _small_fwd("s5_build", _f_s5_build, build_ins, [(block_shape, F32)] * 2)

    norm_mix, norm_ffn, norm_ple = _row(ws['norm_mix']), _row(ws['norm_ffn']), _row(ws['norm_ple'])
    final_norm = _row(ws['final_norm'])
    (xn,) = _tok_fwd("norm_in", _f_norm_in, [x], [norm_mix], [(x.shape[1], BF16)], tm)
    u = _mm("proj_s5", xn, wf['w_in'][:, :S5_WIDTH], 'nn')
    z = _mm("proj_rw", xn, wf['w_in'][:, S5_WIDTH:], 'nn')

    bu = _s5_expand("s5_bu", u, b_blk)
    def late_plan(carrier):
        return _gather_plan([late_shards[n] for n in LATE_GATHER[carrier]])

    def arrived(carrier, got):
        wf.update({n: _from_shard_major(a, SHARDED[n]) for n, a in zip(LATE_GATHER[carrier], got)})

    xs, got = _s5_scan(bu, lam_row, tm, late_plan('s5_scan'))
    arrived('s5_scan', got)
    ypre = _s5_contract("s5_y", xs, c_blk)
    s5_par = [_row(ws['s5_d']), wf['s5_glu_w'], _row(ws['s5_glu_b'])]
    (s5_out,) = _tok_fwd("s5_post", _f_s5_post, [ypre, u], s5_par, [(S5_WIDTH, BF16)], tm)

    pre_par = [_row(ws['rw_shift_mu']), _row(ws['rw_w0']), wf['rw_w2'], _row(ws['rw_a0']), wf['rw_a2'],
               wf['rw_g2'], _row(ws['rw_k_k']), _row(ws['rw_k_a'])]
    r, lw, kp, v, an, bn, gate = _rw_pre_fwd(z, pre_par, tm)
    seqs = [r, lw, kp, v, an, bn]
    y_wkv, ck, got = _wkv_fwd(seqs, late_plan('wkv_fwd'))
    arrived('wkv_fwd', got)
    post_par = [_row(ws['rw_ln_w']), _row(ws['rw_ln_b']), _row(ws['rw_r_k'])]
    post_toks = [y_wkv, r, kp, v, gate]
    (rw_out,) = _tok_fwd("rw_post", _f_rw_post, post_toks, post_par, [(RWKV_WIDTH, BF16)], tm)

    mixcat = jnp.concatenate([s5_out, rw_out], axis=1)
    mixed = _mm("mix_out", mixcat, wf['w_out'], 'nn')
    h1, hn = _tok_fwd("mix_res", _f_mix_res, [x, mixed], [norm_ffn], [(x.shape[1], F32), (x.shape[1], BF16)], tm)
    w13 = jnp.concatenate([wf['ffn_w1'], wf['ffn_w3']], axis=1)
    a13, got = _mm("ffn_up", hn, w13, 'nn', out_dtype=BF16, plan=late_plan('ffn_up'))
    arrived('ffn_up', got)
    (f,) = _tok_fwd("ffn_act", _f_ffn_act, [a13], [], [(FFN_HIDDEN, BF16)], tm)
    ffo = _mm("ffn_down", f, wf['ffn_w2'], 'nn')
    h2, hp = _tok_fwd("ffn_res", _f_ffn_res, [h1, ffo], [norm_ple], [(x.shape[1], F32), (x.shape[1], BF16)], tm)
    gpre = _mm("ple_gate", hp, wf['ple_gate_w'], 'nn')
    pu = _mm("ple_up", p, wf['ple_up_w'], 'nn')

    dh2, dgpre, dpu, g['final_norm'], loss = _tok_bwd(
        "loss", _f_loss, [h2, gpre, pu, target], [final_norm], [None],
        [F32, BF16, BF16, None], [True], tm, acc_out=0)
    g['ple_gate_w'] = _mm("d_ple_gate_w", hp, dgpre, 'tn', out_dtype=WIRE)
    g['ple_up_w'] = _mm("d_ple_up_w", p, dpu, 'tn', out_dtype=WIRE)
    dhp = _mm("d_hp", dgpre, wf['ple_gate_w'], 'nt')
    dh1, dffo, g['norm_ple'] = _tok_bwd("ffn_res_bwd", _f_ffn_res, [h1, ffo], [norm_ple], [dh2, dhp],
                                        [F32, BF16], [True], tm)
    g['ffn_w2'] = _mm("d_ffn_w2", f, dffo, 'tn', out_dtype=WIRE)
    df = _mm("d_f", dffo, wf['ffn_w2'], 'nt', out_dtype=BF16)
    (da13,) = _tok_bwd("ffn_act_bwd", _f_ffn_act, [a13], [], [df], [BF16], [], tm)
    dw13 = _mm("d_ffn_w13", hn, da13, 'tn', out_dtype=WIRE)
    g['ffn_w1'], g['ffn_w3'] = dw13[:, :FFN_HIDDEN], dw13[:, FFN_HIDDEN:]
    dhn = _mm("d_hn", da13, w13, 'nt')
    dx_a, dmixed, g['norm_ffn'] = _tok_bwd("mix_res_bwd", _f_mix_res, [x, mixed], [norm_ffn], [dh1, dhn],
                                           [F32, BF16], [True], tm)
    g['w_out'] = _mm("d_w_out", mixcat, dmixed, 'tn', out_dtype=WIRE)
    dmixcat = _mm("d_mixcat", dmixed, wf['w_out'], 'nt')
    ds5_out, drw_out = dmixcat[:, :S5_WIDTH], dmixcat[:, S5_WIDTH:]

    dy_wkv, dr_b, dkp_b, dv_b, dgate, g['rw_ln_w'], g['rw_ln_b'], g['rw_r_k'] = _tok_bwd(
        "rw_post_bwd", _f_rw_post, post_toks, post_par, [drw_out], [F32] * 5, [True] * 3, tm)
    late_exchange = _exchange_plan([_to_shard_major(g[n], SHARDED[n]).astype(WIRE) for n in LATE_NAMES])
    dseqs, late_pieces = _wkv_bwd(seqs, ck, dy_wkv, late_exchange)
    pre_cots = [(dseqs[0], dr_b), (dseqs[1],), (dseqs[2], dkp_b), (dseqs[3], dv_b), (dseqs[4],), (dseqs[5],),
                (dgate,)]
    dz, *dpre = _rw_pre_bwd(z, pre_par, pre_cots, tm)
    for n, d in zip(['rw_shift_mu', 'rw_w0', 'rw_w2', 'rw_a0', 'rw_a2', 'rw_g2', 'rw_k_k', 'rw_k_a'], dpre):
        g[n] = d

    dypre, du_a, g['s5_d'], g['s5_glu_w'], g['s5_glu_b'] = _tok_bwd(
        "s5_post_bwd", _f_s5_post, [ypre, u], s5_par, [ds5_out], [F32, F32], [True] * 3, tm)
    dxs = _s5_expand("d_s5_x", dypre, c_blk)
    dc_blk = _s5_block_grad("d_s5_c", dypre, xs)
    dbu, dlam_row = _s5_scan_bwd(dxs, xs, lam_row, tm)
    du_b = _s5_contract("d_s5_u", dbu, b_blk)
    db_blk = _s5_block_grad("d_s5_b", u, dbu)
    dbuild = _small_bwd("s5_build_bwd", _f_s5_build, build_ins, (db_blk, dc_blk))
    lam_cots = (dlam_row[:, :S5_LANES].reshape(gp), dlam_row[:, S5_LANES:].reshape(gp),
                dbuild[0].reshape(gp), dbuild[1].reshape(gp))
    g['s5_lam_re'], g['s5_lam_im'], g['s5_log_step'] = _small_bwd("s5_lam_bwd", _f_s5_lam, lam_ins, lam_cots)
    from_t = lambda a, perm: a.reshape(S5_GROUP, S5_GROUPS, S5_STATE).transpose(perm)
    g['s5_b_re'], g['s5_b_im'] = from_t(dbuild[2], (1, 2, 0)), from_t(dbuild[3], (1, 2, 0))
    g['s5_c_re'], g['s5_c_im'] = from_t(dbuild[4], (1, 0, 2)), from_t(dbuild[5], (1, 0, 2))

    dproj = jnp.concatenate([(du_a + du_b).astype(BF16), dz.astype(BF16)], axis=1)
    g['w_in'] = _mm("d_w_in", xn, dproj, 'tn', out_dtype=WIRE)
    dxn = _mm("d_xn", dproj, wf['w_in'], 'nt')
    grad_x, g['norm_mix'] = _tok_bwd("norm_in_bwd", _f_norm_in, [x], [norm_mix], [dxn], [F32], [True], tm,
                                     add_to=(0, dx_a))
    return loss[0, 0], grad_x, g, late_pieces


def _step(x, p, target, w, m, v):
    shards = {n: _mat(w[n]).astype(BF16) for n in SHARDED_NAMES}
    early = _run_plan("gather_early", _gather_plan([shards[n] for n in EARLY_NAMES]))
    wf = {n: _from_shard_major(a, SHARDED[n]) for n, a in zip(EARLY_NAMES, early)}
    ws = {n: w[n] for n in SMALL_NAMES}

    loss, grad_x, g, late_pieces = _local_step(x[0], p[0, 0], target[0], wf, ws, shards)

    early_plan = _exchange_plan([_to_shard_major(g[n], SHARDED[n]).astype(WIRE) for n in EARLY_NAMES],
                                _pack_small({n: g[n] for n in SMALL_NAMES}))
    *early_pieces, by_dev = _run_plan("exchange_early", early_plan)
    pieces = dict(zip(LATE_NAMES + EARLY_NAMES, list(late_pieces) + early_pieces))
    halves = [_ew("add_devices_" + n, _sum_slots, [pieces[n]], [pieces[n].shape[1:]])[0] for n in SHARDED_NAMES]
    (small_piece,) = _ew("add_devices_small", _sum_slots, [by_dev], [by_dev.shape[1:]])
    both, small_g = _share_cores(halves, small_piece)

    kinds = [{}, {}, {}, {}]
    for n, gn in zip(SHARDED_NAMES, both):
        shard = _mat(w[n]).shape
        res = _ew("adamw_" + n, _adamw, [gn.reshape(shard), _mat(w[n]), _mat(m[n]), _mat(v[n])], [shard] * 3)
        for kind, a in zip(kinds, [gn] + list(res)):
            kind[n] = a.reshape(w[n].shape)
    flat = (8 * SMALL_ROWS, PACK_COLS)
    packed = [_pack_small({n: d[n] for n in SMALL_NAMES}).reshape(flat) for d in (w, m, v)]
    small_res = _ew("adamw_small", _adamw, [small_g.reshape(flat)] + packed, [flat] * 3)
    small_shapes = {n: w[n].shape for n in SMALL_NAMES}
    for kind, a in zip(kinds, [small_g] + list(small_res)):
        kind.update(_unpack_small(a, small_shapes))
    total = lax.psum(loss, ("x", "y", "c"))
    return (total, grad_x[None], *[kind[n] for kind in kinds for n in WEIGHT_NAMES])


def kernel(x, p, norm_mix, w_in, s5_lam_re, s5_lam_im, s5_log_step, s5_b_re, s5_b_im, s5_c_re, s5_c_im, s5_d, s5_glu_w, s5_glu_b, rw_shift_mu, rw_w0, rw_w2, rw_a0, rw_a2, rw_g2, rw_k_k, rw_k_a, rw_r_k, rw_ln_w, rw_ln_b, w_out, norm_ffn, ffn_w1, ffn_w3, ffn_w2, norm_ple, ple_gate_w, ple_up_w, final_norm, loss_target, m_norm_mix, m_w_in, m_s5_lam_re, m_s5_lam_im, m_s5_log_step, m_s5_b_re, m_s5_b_im, m_s5_c_re, m_s5_c_im, m_s5_d, m_s5_glu_w, m_s5_glu_b, m_rw_shift_mu, m_rw_w0, m_rw_w2, m_rw_a0, m_rw_a2, m_rw_g2, m_rw_k_k, m_rw_k_a, m_rw_r_k, m_rw_ln_w, m_rw_ln_b, m_w_out, m_norm_ffn, m_ffn_w1, m_ffn_w3, m_ffn_w2, m_norm_ple, m_ple_gate_w, m_ple_up_w, m_final_norm, v_norm_mix, v_w_in, v_s5_lam_re, v_s5_lam_im, v_s5_log_step, v_s5_b_re, v_s5_b_im, v_s5_c_re, v_s5_c_im, v_s5_d, v_s5_glu_w, v_s5_glu_b, v_rw_shift_mu, v_rw_w0, v_rw_w2, v_rw_a0, v_rw_a2, v_rw_g2, v_rw_k_k, v_rw_k_a, v_rw_r_k, v_rw_ln_w, v_rw_ln_b, v_w_out, v_norm_ffn, v_ffn_w1, v_ffn_w3, v_ffn_w2, v_norm_ple, v_ple_gate_w, v_ple_up_w, v_final_norm):
    args = dict(locals())
    w = {n: args[n] for n in WEIGHT_NAMES}
    m = {n: args["m_" + n] for n in WEIGHT_NAMES}
    v = {n: args["v_" + n] for n in WEIGHT_NAMES}
    return _step(x, p, loss_target, w, m, v)
```

```python
import functools
import math
from typing import Any, Callable, NamedTuple, Sequence

import jax
import jax.numpy as jnp
from jax import lax
from jax.experimental import pallas as pl
from jax.experimental.pallas import tpu as pltpu

F32 = jnp.float32
BF16 = jnp.bfloat16
MESH = pl.DeviceIdType.MESH

S5_WIDTH = 512
RWKV_WIDTH = 512
S5_GROUP = 16
S5_GROUPS = 32
S5_STATE = 64
S5_LANES = S5_GROUPS * S5_STATE
S5_TILE_GROUPS = 8
S5_TILES = S5_GROUPS // S5_TILE_GROUPS
S5_TILE_CH = S5_TILE_GROUPS * S5_GROUP
S5_TILE_LANES = S5_TILE_GROUPS * S5_STATE
HEAD = 64
HEADS = 8
DECAY_LORA = 64
AAA_LORA = 64
GATE_LORA = 128
FFN_HIDDEN = 2816
RMS_EPS = 1e-6
GN_EPS = 64e-5
L2_EPS = 1e-12
ADAM_LR = 0.001
ADAM_B1 = 0.9
ADAM_B2 = 0.999
ADAM_EPS = 1e-08
ADAM_WD = 0.01
ADAM_STEP = 10

WKV_CHUNK = 64
SCAN_UNROLL = 4
WIRE = jnp.bfloat16
WKV_PASSES = 1
VMEM_LIMIT_BYTES = 48 * 1024 * 1024
LANE = 128
PACK_COLS = 1024
SMALL_ROWS = 24

WEIGHT_NAMES = ['norm_mix', 'w_in', 's5_lam_re', 's5_lam_im', 's5_log_step', 's5_b_re', 's5_b_im', 's5_c_re',
                's5_c_im', 's5_d', 's5_glu_w', 's5_glu_b', 'rw_shift_mu', 'rw_w0', 'rw_w2', 'rw_a0', 'rw_a2',
                'rw_g2', 'rw_k_k', 'rw_k_a', 'rw_r_k', 'rw_ln_w', 'rw_ln_b', 'w_out', 'norm_ffn', 'ffn_w1',
                'ffn_w3', 'ffn_w2', 'norm_ple', 'ple_gate_w', 'ple_up_w', 'final_norm']
SHARDED = {'w_in': 1, 's5_glu_w': 0, 'rw_w2': 1, 'rw_a2': 1, 'rw_g2': 1, 'w_out': 0, 'ffn_w1': 1, 'ffn_w3': 1,
           'ffn_w2': 0, 'ple_gate_w': 0, 'ple_up_w': 1}
SHARDED_NAMES = [n for n in WEIGHT_NAMES if n in SHARDED]
LATE_NAMES = ['w_out', 'ffn_w1', 'ffn_w3', 'ffn_w2', 'ple_gate_w', 'ple_up_w']
EARLY_NAMES = [n for n in SHARDED_NAMES if n not in LATE_NAMES]
LATE_GATHER = {'s5_scan': ['w_out', 'ple_gate_w', 'ple_up_w'], 'wkv_fwd': ['ffn_w1', 'ffn_w3'], 'ffn_up': ['ffn_w2']}
SMALL_NAMES = [n for n in WEIGHT_NAMES if n not in SHARDED]


def _params(sem=None):
    return pltpu.CompilerParams(dimension_semantics=sem, vmem_limit_bytes=VMEM_LIMIT_BYTES)


def _tile(n, target):
    best = None
    for d in range(LANE, min(n, target) + 1, LANE):
        if n % d == 0:
            best = d
    return n if best is None else best


_NN = (((1,), (0,)), ((), ()))
_NT = (((1,), (1,)), ((), ()))
_TN = (((0,), (0,)), ((), ()))


def _split(a):
    a = a.astype(F32)
    hi = a.astype(BF16)
    return hi, (a - hi.astype(F32)).astype(BF16)


def _dg(a, b, dims, passes):
    dg = lambda p, q: lax.dot_general(p, q, dims, preferred_element_type=F32)
    if passes == 1:
        return dg(a.astype(BF16), b.astype(BF16))
    bh, bl = _split(b)
    if passes == 2:
        return dg(a.astype(BF16), bh) + dg(a.astype(BF16), bl)
    ah, al = _split(a)
    return dg(ah, bh) + (dg(ah, bl) + dg(al, bh))


_DOT_BWD = {_NN: (("g", "b", _NT), ("a", "g", _TN)),
            _NT: (("g", "b", _NN), ("g", "a", _TN)),
            _TN: (("b", "g", _NT), ("a", "g", _NN))}


@functools.partial(jax.custom_vjp, nondiff_argnums=(2, 3))
def _dot(a, b, dims, passes):
    return _dg(a, b, dims, passes)


def _dot_fwd(a, b, dims, passes):
    return _dg(a, b, dims, passes), (a, b)


def _dot_bwd(dims, passes, res, g):
    env = {"a": res[0], "b": res[1], "g": g}
    return tuple(_dg(env[p], env[q], d, passes) for p, q, d in _DOT_BWD[dims])


_dot.defvjp(_dot_fwd, _dot_bwd)


def _bdot(x, w):
    return _dot(x, w, _NN, 1)


def _fdot(a, b, dims=_NN):
    return _dot(a, b, dims, 3)


@jax.custom_vjp
def _shift_down(z):
    return pltpu.roll(z, 1, 0)


def _shift_down_fwd(z):
    return pltpu.roll(z, 1, 0), None


def _shift_down_bwd(_, g):
    return (pltpu.roll(g, g.shape[0] - 1, 0),)


_shift_down.defvjp(_shift_down_fwd, _shift_down_bwd)


def _head_ones():
    r = lax.broadcasted_iota(jnp.int32, (RWKV_WIDTH, RWKV_WIDTH), 0) // HEAD
    c = lax.broadcasted_iota(jnp.int32, (RWKV_WIDTH, RWKV_WIDTH), 1) // HEAD
    return (r == c).astype(F32)


def _mm(name, a, b, mode, out_dtype=F32, precise=False, tm=1024, tn=1024, tk=1536, plan=None):
    if mode == 'nn':
        (m, k), (_, n) = a.shape, b.shape
    elif mode == 'nt':
        (m, k), (n, _) = a.shape, b.shape
    else:
        (k, m), (_, n) = a.shape, b.shape
    tm, tn, tk = _tile(m, tm), _tile(n, tn), _tile(k, tk)
    nm, nn, nk = m // tm, n // tn, k // tk
    dims = {'nn': _NN, 'nt': _NT, 'tn': _TN}[mode]
    plan = _NO_PLAN if plan is None else plan
    parts, plan_in_specs, plan_out_shape, plan_out_specs, plan_sems = _carry(plan, 2, 1)

    def body(*refs):
        a_ref, b_ref, o_ref = refs[0], refs[1], refs[2 + len(plan.ins)]
        acc_ref = refs[3 + len(plan.ins) + len(plan.out_shape)]
        i, j, kk = pl.program_id(0), pl.program_id(1), pl.program_id(2)

        if plan is not _NO_PLAN:
            pl.when((i == 0) & (j == 0) & (kk == 0))(lambda: plan.start(*parts(refs)))

        @pl.when(kk == 0)
        def _():
            acc_ref[...] = jnp.zeros_like(acc_ref)

        acc_ref[...] += _dg(a_ref[...], b_ref[...], dims, 3 if precise else 1)

        @pl.when(kk == nk - 1)
        def _():
            o_ref[...] = acc_ref[...].astype(o_ref.dtype)

        if plan is not _NO_PLAN:
            pl.when((i == nm - 1) & (j == nn - 1) & (kk == nk - 1))(lambda: plan.wait(*parts(refs)))

    if mode == 'tn':
        a_spec = pl.BlockSpec((tk, tm), lambda i, j, l: (l, i))
    else:
        a_spec = pl.BlockSpec((tm, tk), lambda i, j, l: (i, l))
    if mode == 'nt':
        b_spec = pl.BlockSpec((tn, tk), lambda i, j, l: (j, l))
    else:
        b_spec = pl.BlockSpec((tk, tn), lambda i, j, l: (l, j))
    res = pl.pallas_call(
        body, name=name,
        out_shape=[jax.ShapeDtypeStruct((m, n), out_dtype)] + plan_out_shape,
        grid=(nm, nn, nk),
        in_specs=[a_spec, b_spec] + plan_in_specs,
        out_specs=[pl.BlockSpec((tm, tn), lambda i, j, l: (i, j))] + plan_out_specs,
        scratch_shapes=[pltpu.VMEM((tm, tn), F32)] + plan_sems,
        compiler_params=_params(("parallel", "parallel", "arbitrary") if plan is _NO_PLAN else ("arbitrary",) * 3),
    )(a, b, *plan.ins)
    return res[0] if plan is _NO_PLAN else (res[0], res[1:])


def _mm_tiles(name, a, b, mode, out_shape, grid, a_spec, b_spec, o_spec):
    dims = {'nn': _NN, 'nt': _NT, 'tn': _TN}[mode]
    nk = grid[2]

    def body(a_ref, b_ref, o_ref, acc_ref):
        kk = pl.program_id(2)

        @pl.when(kk == 0)
        def _():
            acc_ref[...] = jnp.zeros_like(acc_ref)

        acc_ref[...] += _dg(a_ref[...], b_ref[...], dims, 1)

        @pl.when(kk == nk - 1)
        def _():
            o_ref[...] = acc_ref[...]

    return pl.pallas_call(
        body, name=name,
        out_shape=jax.ShapeDtypeStruct(out_shape, F32),
        grid=grid, in_specs=[a_spec, b_spec], out_specs=o_spec,
        scratch_shapes=[pltpu.VMEM(o_spec.block_shape, F32)],
        compiler_params=_params(("parallel", "parallel", "arbitrary")),
    )(a, b)


def _s5_expand(name, u, blk, tm=1024):
    t = u.shape[0]
    tm = min(tm, t)
    ch, ln, nt = S5_TILE_CH, S5_TILE_LANES, S5_TILES
    return _mm_tiles(name, u, blk, 'nn', (t, 2 * S5_LANES), (t // tm, 2 * nt, 1),
                     pl.BlockSpec((tm, ch), lambda i, j, l: (i, j % nt)),
                     pl.BlockSpec((ch, ln), lambda i, j, l: (j % nt, j // nt)),
                     pl.BlockSpec((tm, ln), lambda i, j, l: (i, j)))


def _s5_contract(name, x, blk, tm=1024):
    t = x.shape[0]
    tm = min(tm, t)
    ch, ln, nt = S5_TILE_CH, S5_TILE_LANES, S5_TILES
    return _mm_tiles(name, x, blk, 'nt', (t, S5_WIDTH), (t // tm, nt, 2),
                     pl.BlockSpec((tm, ln), lambda i, j, l: (i, j + nt * l)),
                     pl.BlockSpec((ch, ln), lambda i, j, l: (j, l)),
                     pl.BlockSpec((tm, ch), lambda i, j, l: (i, j)))


def _s5_block_grad(name, u, x, tk=1024):
    t = u.shape[0]
    tk = min(tk, t)
    ch, ln, nt = S5_TILE_CH, S5_TILE_LANES, S5_TILES
    return _mm_tiles(name, u, x, 'tn', (S5_WIDTH, 2 * ln), (nt, 2, t // tk),
                     pl.BlockSpec((tk, ch), lambda i, j, l: (l, i)),
                     pl.BlockSpec((tk, ln), lambda i, j, l: (l, i + nt * j)),
                     pl.BlockSpec((ch, ln), lambda i, j, l: (i, j)))


def _full_spec(p):
    nd = p.ndim
    return pl.BlockSpec(p.shape, lambda i, nd=nd: (0,) * nd)


def _tok_fwd(name, fn, toks, params, outs, tm):
    t = toks[0].shape[0]
    nt, npar = len(toks), len(params)

    def body(*refs):
        tv = [r[...].astype(F32) for r in refs[:nt]]
        pv = [r[...].astype(F32) for r in refs[nt:nt + npar]]
        res = fn(*tv, *pv)
        for r, v in zip(refs[nt + npar:], res):
            r[...] = v.astype(r.dtype)

    return pl.pallas_call(
        body, name=name,
        out_shape=[jax.ShapeDtypeStruct((t, w), d) for w, d in outs],
        grid=(t // tm,),
        in_specs=[pl.BlockSpec((tm, a.shape[1]), lambda i: (i, 0)) for a in toks] + [_full_spec(p) for p in params],
        out_specs=[pl.BlockSpec((tm, w), lambda i: (i, 0)) for w, _ in outs],
        compiler_params=_params(("parallel",)),
    )(*toks, *params)


def _tok_bwd(name, fn, toks, params, cots, dtok, dpar, tm, acc_out=None, add_to=None):
    t = toks[0].shape[0]
    nt, npar = len(toks), len(params)
    cot_arrays = [c for c in cots if c is not None]
    ncot = len(cot_arrays)
    extra = [] if add_to is None else [add_to[1]]
    dtok_idx = [i for i, d in enumerate(dtok) if d is not None]
    dpar_idx = [i for i, d in enumerate(dpar) if d]

    def body(*refs):
        pos = 0
        tin = refs[pos:pos + nt]; pos += nt
        pin = refs[pos:pos + npar]; pos += npar
        cin = refs[pos:pos + ncot]; pos += ncot
        ein = refs[pos:pos + len(extra)]; pos += len(extra)
        dto = refs[pos:pos + len(dtok_idx)]; pos += len(dtok_idx)
        dpo = refs[pos:pos + len(dpar_idx)]; pos += len(dpar_idx)
        acc = refs[pos] if acc_out is not None else None
        first = pl.program_id(0) == 0

        tv = [r[...].astype(F32) for r in tin]
        pv = [r[...].astype(F32) for r in pin]
        res, vjp = jax.vjp(fn, *tv, *pv)
        cit = iter(cin)
        cs = tuple(jnp.ones_like(o) if c is None else next(cit)[...].astype(F32) for c, o in zip(cots, res))
        g = vjp(cs)
        for r, i in zip(dto, dtok_idx):
            v = g[i]
            if add_to is not None and add_to[0] == i:
                v = v + ein[0][...].astype(F32)
            r[...] = v.astype(r.dtype)

        @pl.when(first)
        def _():
            for r in dpo:
                r[...] = jnp.zeros_like(r)
            if acc is not None:
                acc[...] = jnp.zeros_like(acc)

        for r, i in zip(dpo, dpar_idx):
            r[...] += g[nt + i]
        if acc is not None:
            acc[...] += res[acc_out]

    out_shape = [jax.ShapeDtypeStruct(toks[i].shape, dtok[i]) for i in dtok_idx]
    out_shape += [jax.ShapeDtypeStruct(params[i].shape, F32) for i in dpar_idx]
    out_specs = [pl.BlockSpec((tm, toks[i].shape[1]), lambda i_: (i_, 0)) for i in dtok_idx]
    out_specs += [_full_spec(params[i]) for i in dpar_idx]
    if acc_out is not None:
        out_shape.append(jax.ShapeDtypeStruct((1, 1), F32))
        out_specs.append(pl.BlockSpec((1, 1), lambda i_: (0, 0)))
    tok_spec = lambda a: pl.BlockSpec((tm, a.shape[1]), lambda i_: (i_, 0))
    return pl.pallas_call(
        body, name=name,
        out_shape=out_shape,
        grid=(t // tm,),
        in_specs=[tok_spec(a) for a in toks] + [_full_spec(p) for p in params]
        + [tok_spec(c) for c in cot_arrays] + [tok_spec(e) for e in extra],
        out_specs=out_specs,
        compiler_params=_params(("arbitrary",)),
    )(*toks, *params, *cot_arrays, *extra)


def _small_fwd(name, fn, ins, outs):
    n = len(ins)

    def body(*refs):
        res = fn(*[r[...] for r in refs[:n]])
        for r, v in zip(refs[n:], res):
            r[...] = v.astype(r.dtype)

    return pl.pallas_call(
        body, name=name,
        out_shape=[jax.ShapeDtypeStruct(s, d) for s, d in outs],
        compiler_params=_params(),
    )(*ins)


def _small_bwd(name, fn, ins, cots):
    n = len(ins)

    def body(*refs):
        _, vjp = jax.vjp(fn, *[r[...] for r in refs[:n]])
        g = vjp(tuple(r[...] for r in refs[n:n + len(cots)]))
        for r, v in zip(refs[n + len(cots):], g):
            r[...] = v

    return pl.pallas_call(
        body, name=name,
        out_shape=[jax.ShapeDtypeStruct(a.shape, F32) for a in ins],
        compiler_params=_params(),
    )(*ins, *cots)


def _rms(x, g):
    return x * lax.rsqrt(jnp.mean(x * x, axis=-1, keepdims=True) + RMS_EPS) * g


def _f_norm_in(x, g):
    return (_rms(x, g),)


def _f_mix_res(x, mixed, g):
    h1 = x + mixed
    return h1, _rms(h1, g)


def _f_ffn_act(a13):
    a1, a3 = a13[:, :FFN_HIDDEN], a13[:, FFN_HIDDEN:]
    return (jax.nn.silu(a1) * a3,)


def _f_ffn_res(h1, ffo, g):
    h2 = h1 + ffo
    return h2, _rms(h2, g)


def _f_loss(h2, gpre, pu, target, g):
    h3 = h2 + jax.nn.sigmoid(gpre) * pu
    y = _rms(h3, g)
    err = jnp.square(y - target)
    return (0.5 * jnp.sum(jnp.mean(err, axis=-1, keepdims=True), axis=0, keepdims=True),)


def _f_s5_post(ypre, u, d, glu_w, glu_b):
    z = jax.nn.gelu(ypre + u * d)
    return (z * jax.nn.sigmoid(_bdot(z, glu_w) + glu_b),)


def _softplus(x):
    return jnp.maximum(x, 0.0) + jnp.log(1.0 + jnp.exp(-jnp.abs(x)))


def _f_rw_pre(z, carry, shift_mu, w0, w2, a0, a2, g2, k_k, k_a):
    rw = RWKV_WIDTH
    first_row = lax.broadcasted_iota(jnp.int32, z.shape, 0) == 0
    prev = jnp.where(first_row, carry, _shift_down(z))
    zs = z + (prev - z) * shift_mu
    o1, o2 = 3 * rw + DECAY_LORA, 3 * rw + DECAY_LORA + AAA_LORA
    r, k, v = zs[:, :rw], zs[:, rw:2 * rw], zs[:, 2 * rw:3 * rw]
    wl, al, gl = zs[:, 3 * rw:o1], zs[:, o1:o2], zs[:, o2:]
    w = -_softplus(-(w0 + _bdot(jnp.tanh(wl), w2))) - 0.5
    log_decay = -jnp.exp(w)
    a = jax.nn.sigmoid(a0 + _bdot(al, a2))
    g = _bdot(jax.nn.sigmoid(gl), g2)
    kk = k * k_k
    norm = jnp.sqrt(_fdot(kk * kk, _head_ones()))
    kk = kk / jnp.maximum(norm, L2_EPS)
    kp = k * (1.0 + (a - 1.0) * k_a)
    return r, log_decay, kp, v, -kk, kk * a, g


def _f_rw_post(y, r, kp, v, g, ln_w, ln_b, r_k):
    ones = _head_ones()
    yc = y - _fdot(y, ones) * (1.0 / HEAD)
    var = _fdot(yc * yc, ones) * (1.0 / HEAD)
    yn = yc * lax.rsqrt(var + GN_EPS) * ln_w + ln_b
    bonus = _fdot(r * kp * r_k, ones) * v
    return ((yn + bonus) * g,)


def _f_s5_lam(lam_re, lam_im, log_step):
    step = jnp.exp(log_step)
    dr, di = lam_re * step, lam_im * step
    e = jnp.exp(dr)
    lbr, lbi = e * jnp.cos(di), e * jnp.sin(di)
    nr, ni = lbr - 1.0, lbi
    den = lam_re * lam_re + lam_im * lam_im
    return lbr, lbi, (nr * lam_re + ni * lam_im) / den, (ni * lam_re - nr * lam_im) / den


def _f_s5_build(coef_r, coef_i, btr, bti, ctr, cti):
    bbr = coef_r * btr - coef_i * bti
    bbi = coef_r * bti + coef_i * btr
    shape = (S5_WIDTH, S5_TILE_LANES)
    rows = (lax.broadcasted_iota(jnp.int32, shape, 0) % S5_TILE_CH) // S5_GROUP
    cols = lax.broadcasted_iota(jnp.int32, shape, 1) // S5_STATE
    mask = (rows == cols).astype(F32)

    def blocks(m):
        per_tile = [m[:, S5_TILE_LANES * i:S5_TILE_LANES * (i + 1)] for i in range(S5_TILES)]
        return jnp.concatenate([t for t in per_tile for _ in range(S5_TILE_GROUPS)], axis=0) * mask

    return (jnp.concatenate([blocks(bbr), blocks(bbi)], axis=1),
            jnp.concatenate([blocks(ctr), -blocks(cti)], axis=1))


HALO = 8


def _rw_pre_specs(z, params, tm, order):
    halo_blocks = tm // HALO
    return ([pl.BlockSpec((tm, z.shape[1]), lambda i: (order(i), 0)),
             pl.BlockSpec((HALO, z.shape[1]), lambda i: (jnp.maximum(order(i) * halo_blocks - 1, 0), 0))]
            + [_full_spec(p) for p in params])


def _rw_pre_fwd(z, params, tm):
    t = z.shape[0]
    npar = len(params)

    def body(z_ref, halo_ref, *refs):
        carry = jnp.where(pl.program_id(0) == 0, 0.0, halo_ref[pl.ds(HALO - 1, 1), :])
        res = _f_rw_pre(z_ref[...], carry, *[r[...].astype(F32) for r in refs[:npar]])
        for r, v in zip(refs[npar:], res):
            r[...] = v

    return pl.pallas_call(
        body, name="rw_pre",
        out_shape=[jax.ShapeDtypeStruct((t, RWKV_WIDTH), F32)] * 7,
        grid=(t // tm,),
        in_specs=_rw_pre_specs(z, params, tm, lambda i: i),
        out_specs=[pl.BlockSpec((tm, RWKV_WIDTH), lambda i: (i, 0))] * 7,
        compiler_params=_params(("parallel",)),
    )(z, z, *params)


def _rw_pre_bwd(z, params, cots, tm):
    t = z.shape[0]
    nt = t // tm
    npar = len(params)
    order = lambda i: nt - 1 - i
    flat_cots = [a for group in cots for a in group]
    ncot = len(flat_cots)

    def body(z_ref, halo_ref, *refs):
        pin, cin = refs[:npar], list(refs[npar:npar + ncot])
        dz_ref = refs[npar + ncot]
        dpo = refs[npar + ncot + 1:npar + ncot + 1 + npar]
        dcarry_ref = refs[npar + ncot + 1 + npar]
        i = pl.program_id(0)

        @pl.when(i == 0)
        def _():
            dcarry_ref[...] = jnp.zeros_like(dcarry_ref)
            for r in dpo:
                r[...] = jnp.zeros_like(r)

        carry = jnp.where(i == nt - 1, 0.0, halo_ref[pl.ds(HALO - 1, 1), :])
        _, vjp = jax.vjp(_f_rw_pre, z_ref[...], carry, *[r[...].astype(F32) for r in pin])
        g = vjp(tuple(sum(cin.pop(0)[...] for _ in group) for group in cots))
        last_row = lax.broadcasted_iota(jnp.int32, z_ref.shape, 0) == tm - 1
        dz_ref[...] = g[0] + jnp.where(last_row, dcarry_ref[...], 0.0)
        dcarry_ref[...] = g[1]
        for r, v in zip(dpo, g[2:]):
            r[...] += v

    tok = lambda w: pl.BlockSpec((tm, w), lambda i: (order(i), 0))
    return pl.pallas_call(
        body, name="rw_pre_bwd",
        out_shape=[jax.ShapeDtypeStruct(z.shape, F32)] + [jax.ShapeDtypeStruct(p.shape, F32) for p in params],
        grid=(nt,),
        in_specs=_rw_pre_specs(z, params, tm, order) + [tok(RWKV_WIDTH)] * ncot,
        out_specs=[tok(z.shape[1])] + [_full_spec(p) for p in params],
        scratch_shapes=[pltpu.VMEM((1, z.shape[1]), F32)],
        compiler_params=_params(("arbitrary",)),
    )(z, z, *params, *flat_cots)


def _s5_scan(bu, lam, tm, plan):
    t, w = bu.shape
    h = w // 2
    nt = t // tm
    parts, plan_in_specs, plan_out_shape, plan_out_specs, plan_sems = _carry(plan, 2, 1)

    def body(*refs):
        bu_ref, lam_ref = refs[:2]
        xb_ref = refs[2 + len(plan.ins)]
        carry_ref = refs[3 + len(plan.ins) + len(plan.out_shape)]
        x_ref, refs = refs[-1], refs[:-1]

        @pl.when(pl.program_id(0) == 0)
        def _():
            carry_ref[...] = jnp.zeros_like(carry_ref)
            plan.start(*parts(refs))

        lr, li = lam_ref[:, :h], lam_ref[:, h:]

        def step(s, c):
            cr, ci = c
            row = pl.ds(s, 1)
            nr = lr * cr - li * ci + bu_ref[row, :h]
            ni = lr * ci + li * cr + bu_ref[row, h:]
            x_ref[row, :h] = nr
            x_ref[row, h:] = ni
            return nr, ni

        cr, ci = lax.fori_loop(0, tm, step, (carry_ref[:, :h], carry_ref[:, h:]), unroll=SCAN_UNROLL)
        carry_ref[:, :h] = cr
        carry_ref[:, h:] = ci
        xb_ref[...] = x_ref[...].astype(BF16)

        @pl.when(pl.program_id(0) == nt - 1)
        def _():
            plan.wait(*parts(refs))

    spec = pl.BlockSpec((tm, w), lambda i: (i, 0))
    res = pl.pallas_call(
        body, name="s5_scan",
        out_shape=[jax.ShapeDtypeStruct((t, w), BF16)] + plan_out_shape,
        grid=(nt,),
        in_specs=[spec, pl.BlockSpec((1, w), lambda i: (0, 0))] + plan_in_specs,
        out_specs=[spec] + plan_out_specs,
        scratch_shapes=[pltpu.VMEM((1, w), F32)] + plan_sems + [pltpu.VMEM((tm, w), F32)],
        compiler_params=_params(("arbitrary",)),
    )(bu, lam, *plan.ins)
    return res[0], res[1:]


def _s5_scan_bwd(dx, xb, lam, tm):
    t, w = dx.shape
    h = w // 2
    nt = t // tm
    halo = BF16_SUBLANES

    def body(dx_ref, xb_ref, halo_ref, lam_ref, dbu_out_ref, dlam_ref, carry_ref, dbu_ref):
        @pl.when(pl.program_id(0) == 0)
        def _():
            carry_ref[...] = jnp.zeros_like(carry_ref)
            dlam_ref[...] = jnp.zeros_like(dlam_ref)

        lr, li = lam_ref[:, :h], lam_ref[:, h:]

        def step(s, c):
            cr, ci = c
            row = pl.ds(tm - 1 - s, 1)
            nr = lr * cr + li * ci + dx_ref[row, :h]
            ni = lr * ci - li * cr + dx_ref[row, h:]
            dbu_ref[row, :h] = nr
            dbu_ref[row, h:] = ni
            return nr, ni

        cr, ci = lax.fori_loop(0, tm, step, (carry_ref[:, :h], carry_ref[:, h:]), unroll=SCAN_UNROLL)
        carry_ref[:, :h] = cr
        carry_ref[:, h:] = ci
        halo_rows = lax.broadcasted_iota(jnp.int32, (halo, w), 0)
        before = jnp.sum(jnp.where(halo_rows == halo - 1, halo_ref[...].astype(F32), 0.0), axis=0, keepdims=True)
        before = jnp.where(pl.program_id(0) == nt - 1, 0.0, before)
        first_row = lax.broadcasted_iota(jnp.int32, (tm, w), 0) == 0
        xp = jnp.where(first_row, before, pltpu.roll(xb_ref[...].astype(F32), 1, 0))
        gr, gi = dbu_ref[:, :h], dbu_ref[:, h:]
        pr, pi_ = xp[:, :h], xp[:, h:]
        dlam_ref[:, :h] += jnp.sum(gr * pr + gi * pi_, axis=0, keepdims=True)
        dlam_ref[:, h:] += jnp.sum(gi * pr - gr * pi_, axis=0, keepdims=True)
        dbu_out_ref[...] = dbu_ref[...].astype(BF16)

    spec = pl.BlockSpec((tm, w), lambda i: (nt - 1 - i, 0))
    halo_spec = pl.BlockSpec((halo, w), lambda i: (jnp.maximum((nt - 1 - i) * (tm // halo) - 1, 0), 0))
    row_spec = pl.BlockSpec((1, w), lambda i: (0, 0))
    return pl.pallas_call(
        body, name="s5_scan_bwd",
        out_shape=[jax.ShapeDtypeStruct((t, w), BF16), jax.ShapeDtypeStruct((1, w), F32)],
        grid=(nt,),
        in_specs=[spec, spec, halo_spec, row_spec],
        out_specs=[spec, row_spec],
        scratch_shapes=[pltpu.VMEM((1, w), F32), pltpu.VMEM((tm, w), F32)],
        compiler_params=_params(("arbitrary",)),
    )(dx, xb, xb, lam)


def _wkv_chunks(s0, r, lw, k, v, a, b):
    c = r[0].shape[0]
    row = lax.broadcasted_iota(jnp.int32, (c, c), 0)
    col = lax.broadcasted_iota(jnp.int32, (c, c), 1)
    incl, strict = col <= row, col < row
    tri = incl.astype(F32)
    eye = (row == col).astype(F32)
    each = lambda f, *xs: [f(*t) for t in zip(*xs)]
    stack = lambda p, q: jnp.concatenate([p, q], axis=0)
    dot = lambda p, q, dims=_NN: _dot(p, q, dims, WKV_PASSES)
    lc = each(lambda l: _dot(tri, l, _NN, 2), lw)
    e_neg = each(lambda l: jnp.exp(-l), lc)
    ar = each(lambda x, z, l, w: stack(x * jnp.exp(l - w), z * jnp.exp(l)), a, r, lc, lw)
    bk = each(lambda x, z, e: stack(x * e, z * e), b, k, e_neg)
    m = each(lambda p, q: dot(p, q, _NT), ar, bk)
    mab = each(lambda q: jnp.where(strict, q[:c, :c], 0.0), m)
    mak_mrk = each(lambda q: stack(jnp.where(strict, q[:c, c:], 0.0), jnp.where(incl, q[c:, c:], 0.0)), m)
    mrb = each(lambda q: jnp.where(incl, q[c:, :c], 0.0), m)
    xy = each(lambda p, s, q, z: dot(p, s, _NT) + dot(q, z), ar, s0, mak_mrk, v)
    inv = each(lambda q: eye + q, mab)
    pw = each(lambda q: dot(q, q), mab)
    for _ in range(int(math.log2(c)) - 2):
        both = each(lambda i, q: dot(stack(i, q), q), inv, pw)
        inv = each(lambda i, q: i + q[:c], inv, both)
        pw = each(lambda q: q[c:], both)
    inv = each(lambda i, q: i + dot(i, q), inv, pw)
    u = each(lambda i, q: dot(i, q[:c]), inv, xy)
    y = each(lambda q, z, p: q[c:] + dot(z, p), xy, mrb, u)
    e_tot = each(lambda l: jnp.exp(jnp.sum(l, axis=0, keepdims=True)), lw)
    s1 = each(lambda s, p, z, q, e: (s + dot(stack(p, z), q, _TN)) * e, s0, u, v, bk, e_tot)
    return y, s1


def _carry(plan, n_args, n_outs):
    n_in, n_out = len(plan.ins), len(plan.out_shape)

    def parts(refs):
        base = n_args + n_in + n_outs
        return refs[n_args:n_args + n_in], refs[base:base + n_out], refs[base + n_out + 1:]

    return parts, [HBM_SPEC] * n_in, list(plan.out_shape), [HBM_SPEC] * n_out, list(plan.sems)


def _head_cols(ref):
    return tuple(ref[:, h * HEAD:(h + 1) * HEAD] for h in range(HEADS))


def _wkv_fwd(seqs, plan):
    t, w = seqs[0].shape
    c, n = WKV_CHUNK, HEAD
    nc = t // c
    parts, plan_in_specs, plan_out_shape, plan_out_specs, plan_sems = _carry(plan, 6, 2)

    def body(*refs):
        ins, (y_ref, ck_ref) = refs[:6], refs[6 + len(plan.ins):8 + len(plan.ins)]
        s_ref = refs[8 + len(plan.ins) + len(plan.out_shape)]

        @pl.when(pl.program_id(0) == 0)
        def _():
            s_ref[...] = jnp.zeros_like(s_ref)
            plan.start(*parts(refs))

        s0 = tuple(s_ref[h] for h in range(HEADS))
        ys, s1 = _wkv_chunks(s0, *[_head_cols(r) for r in ins])
        for h in range(HEADS):
            ck_ref[0, h] = s0[h]
            y_ref[:, h * n:(h + 1) * n] = ys[h]
            s_ref[h] = s1[h]

        @pl.when(pl.program_id(0) == nc - 1)
        def _():
            plan.wait(*parts(refs))

    spec = pl.BlockSpec((c, w), lambda i: (i, 0))
    res = pl.pallas_call(
        body, name="wkv_fwd",
        out_shape=[jax.ShapeDtypeStruct((t, w), F32), jax.ShapeDtypeStruct((nc, HEADS, n, n), F32)] + plan_out_shape,
        grid=(nc,),
        in_specs=[spec] * 6 + plan_in_specs,
        out_specs=[spec, pl.BlockSpec((1, HEADS, n, n), lambda i: (i, 0, 0, 0))] + plan_out_specs,
        scratch_shapes=[pltpu.VMEM((HEADS, n, n), F32)] + plan_sems,
        compiler_params=_params(("arbitrary",)),
    )(*seqs, *plan.ins)
    return res[0], res[1], res[2:]


def _wkv_bwd(seqs, ck, dy, plan):
    t, w = seqs[0].shape
    c, n = WKV_CHUNK, HEAD
    nc = t // c
    parts, plan_in_specs, plan_out_shape, plan_out_specs, plan_sems = _carry(plan, 8, 6)

    def body(*refs):
        ins, ck_ref, dy_ref = refs[:6], refs[6], refs[7]
        outs = refs[8 + len(plan.ins):14 + len(plan.ins)]
        ds_ref = refs[14 + len(plan.ins) + len(plan.out_shape)]

        @pl.when(pl.program_id(0) == 0)
        def _():
            ds_ref[...] = jnp.zeros_like(ds_ref)
            plan.start(*parts(refs))

        s0 = tuple(ck_ref[0, h] for h in range(HEADS))
        _, vjp = jax.vjp(_wkv_chunks, s0, *[_head_cols(r) for r in ins])
        g = vjp((list(_head_cols(dy_ref)), [ds_ref[h] for h in range(HEADS)]))
        for h in range(HEADS):
            ds_ref[h] = g[0][h]
            for o, d in zip(outs, g[1:]):
                o[:, h * n:(h + 1) * n] = d[h]

        @pl.when(pl.program_id(0) == nc - 1)
        def _():
            plan.wait(*parts(refs))

    spec = pl.BlockSpec((c, w), lambda i: (nc - 1 - i, 0))
    res = pl.pallas_call(
        body, name="wkv_bwd",
        out_shape=[jax.ShapeDtypeStruct((t, w), F32)] * 6 + plan_out_shape,
        grid=(nc,),
        in_specs=[spec] * 6 + [pl.BlockSpec((1, HEADS, n, n), lambda i: (nc - 1 - i, 0, 0, 0)), spec] + plan_in_specs,
        out_specs=[spec] * 6 + plan_out_specs,
        scratch_shapes=[pltpu.VMEM((HEADS, n, n), F32)] + plan_sems,
        compiler_params=_params(("arbitrary",)),
    )(*seqs, ck, dy, *plan.ins)
    return res[:6], res[6:]


def _coords():
    return lax.axis_index("x"), lax.axis_index("y"), lax.axis_index("c")


def _flip(v, f):
    return 1 - v if f else v


_CHIP_FLIPS = [(1, 0), (0, 1), (1, 1)]
_DEV_FLIPS = [(fx, fy, fc) for fx in (0, 1) for fy in (0, 1) for fc in (0, 1) if (fx, fy, fc) != (0, 0, 0)]
HBM_SPEC = pl.BlockSpec(memory_space=pl.ANY)


def _chip_peer(k, x, y):
    fx, fy = _CHIP_FLIPS[k]
    return _flip(x, fx), _flip(y, fy)


def _dev_peer(k, x, y, c):
    fx, fy, fc = _DEV_FLIPS[k]
    return _flip(x, fx), _flip(y, fy), _flip(c, fc)


def _rows_of_core(ref, core):
    h = ref.shape[-2] // 2
    rows = pl.ds(pl.multiple_of(core * h, 8), h)
    return ref.at[rows, :] if len(ref.shape) == 2 else ref.at[:, rows, :]


class _Plan(NamedTuple):
    ins: Sequence[Any]
    out_shape: Sequence[Any]
    sems: Sequence[Any]
    start: Callable
    wait: Callable


_NO_PLAN = _Plan([], [], [], lambda *_: None, lambda *_: None)


def _run_plan(name, plan):
    n_in, n_out = len(plan.ins), len(plan.out_shape)

    def body(*refs):
        parts = refs[:n_in], refs[n_in:n_in + n_out], refs[n_in + n_out:]
        plan.start(*parts)
        plan.wait(*parts)

    return pl.pallas_call(
        body, name=name, out_shape=list(plan.out_shape),
        in_specs=[HBM_SPEC] * n_in, out_specs=[HBM_SPEC] * n_out, scratch_shapes=list(plan.sems),
    )(*plan.ins)


def _gather_plan(shards):
    n = len(shards)

    def copies(srcs, outs, sems):
        send_sems, recv_sems, local_sems = sems
        x, y, c = _coords()
        me = 2 * x + y

        def remote(i, k, arriving):
            px, py = _chip_peer(k, x, y)
            return pltpu.make_async_remote_copy(
                src_ref=srcs[i], dst_ref=outs[i].at[2 * px + py if arriving else me],
                send_sem=send_sems.at[i, k], recv_sem=recv_sems.at[i, k],
                device_id=(px, py, c), device_id_type=MESH)

        own = [pltpu.make_async_copy(srcs[i], outs[i].at[me], local_sems.at[i]) for i in range(n)]
        pairs = [(i, k) for k in range(3) for i in range(n)]
        return own, [remote(i, k, False) for i, k in pairs], [remote(i, k, True) for i, k in pairs]

    return _Plan(
        ins=shards, out_shape=[jax.ShapeDtypeStruct((4,) + s.shape, s.dtype) for s in shards],
        sems=[pltpu.SemaphoreType.DMA((n, 3)), pltpu.SemaphoreType.DMA((n, 3)), pltpu.SemaphoreType.DMA((n,))],
        start=functools.partial(_start_copies, copies), wait=functools.partial(_wait_copies, copies))


def _start_copies(copies, ins, outs, sems):
    own, sends, _ = copies(ins, outs, sems)
    for cp in own + sends:
        cp.start()


def _wait_copies(copies, ins, outs, sems):
    own, sends, arrivals = copies(ins, outs, sems)
    for cp in arrivals:
        cp.wait_recv()
    for cp in sends:
        cp.wait_send()
    for cp in own:
        cp.wait()


def _exchange_plan(gs, small=None):
    n = len(gs)
    arrays = list(gs) + ([] if small is None else [small])

    def copies(srcs, outs, sems):
        send_sems, recv_sems, local_sems = sems
        x, y, c = _coords()
        me = 4 * x + 2 * y + c

        def piece(i, px, py, pc):
            if i == n:
                return srcs[i].at[4 * px + 2 * py + pc]
            return _rows_of_core(srcs[i].at[2 * px + py], pc)

        def remote(i, k, arriving):
            px, py, pc = _dev_peer(k, x, y, c)
            return pltpu.make_async_remote_copy(
                src_ref=piece(i, px, py, pc), dst_ref=outs[i].at[4 * px + 2 * py + pc if arriving else me],
                send_sem=send_sems.at[i, k], recv_sem=recv_sems.at[i, k],
                device_id=(px, py, pc), device_id_type=MESH)

        own = [pltpu.make_async_copy(piece(i, x, y, c), outs[i].at[me], local_sems.at[i]) for i in range(len(arrays))]
        pairs = [(i, k) for k in range(7) for i in range(len(arrays))]
        return own, [remote(i, k, False) for i, k in pairs], [remote(i, k, True) for i, k in pairs]

    out_shape = [jax.ShapeDtypeStruct((8, g.shape[1] // 2, g.shape[2]), g.dtype) for g in gs]
    if small is not None:
        out_shape.append(jax.ShapeDtypeStruct(small.shape, small.dtype))
    m = len(arrays)
    return _Plan(
        ins=arrays, out_shape=out_shape,
        sems=[pltpu.SemaphoreType.DMA((m, 7)), pltpu.SemaphoreType.DMA((m, 7)), pltpu.SemaphoreType.DMA((m,))],
        start=functools.partial(_start_copies, copies), wait=functools.partial(_wait_copies, copies))


def _share_cores(halves, small):
    n = len(halves)

    def body(*refs):
        srcs, small_src, outs, small_out = refs[:n], refs[n], refs[n + 1:2 * n + 1], refs[2 * n + 1]
        mine, theirs = refs[2 * n + 2:3 * n + 2], refs[3 * n + 2:4 * n + 2]
        send_sems, recv_sems, ssend, srecv, local_sems = refs[4 * n + 2:]
        x, y, c = _coords()
        me = 4 * x + 2 * y + c

        def big(i):
            return pltpu.make_async_remote_copy(
                src_ref=mine[i], dst_ref=theirs[i], send_sem=send_sems.at[i], recv_sem=recv_sems.at[i],
                device_id=(x, y, 1 - c), device_id_type=MESH)

        def tiny(k, arriving):
            px, py, pc = _dev_peer(k, x, y, c)
            return pltpu.make_async_remote_copy(
                src_ref=small_src, dst_ref=small_out.at[4 * px + 2 * py + pc if arriving else me],
                send_sem=ssend.at[k], recv_sem=srecv.at[k], device_id=(px, py, pc), device_id_type=MESH)

        small_sends = [tiny(k, False) for k in range(7)]
        own_small = pltpu.make_async_copy(small_src, small_out.at[me], local_sems.at[2 * n])
        stage = [pltpu.make_async_copy(srcs[i], mine[i], local_sems.at[i]) for i in range(n)]
        for cp in small_sends + [own_small] + stage:
            cp.start()
        sends = []
        for i in range(n):
            stage[i].wait()
            sends.append(big(i))
            sends[-1].start()
        store = [pltpu.make_async_copy(mine[i], outs[i].at[c], local_sems.at[i]) for i in range(n)]
        for cp in store:
            cp.start()
        for i in range(n):
            big(i).wait_recv()
            store.append(pltpu.make_async_copy(theirs[i], outs[i].at[1 - c], local_sems.at[n + i]))
            store[-1].start()
        for k in range(7):
            tiny(k, True).wait_recv()
        for cp in sends + small_sends:
            cp.wait_send()
        for cp in store + [own_small]:
            cp.wait()

    staged = [pltpu.VMEM(s.shape, s.dtype) for s in halves]
    res = pl.pallas_call(
        body, name="share_cores",
        out_shape=[jax.ShapeDtypeStruct((2,) + s.shape, s.dtype) for s in halves]
        + [jax.ShapeDtypeStruct((8,) + small.shape, small.dtype)],
        in_specs=[HBM_SPEC] * (n + 1), out_specs=[HBM_SPEC] * (n + 1),
        scratch_shapes=staged + staged + [
            pltpu.SemaphoreType.DMA((n,)), pltpu.SemaphoreType.DMA((n,)),
            pltpu.SemaphoreType.DMA((7,)), pltpu.SemaphoreType.DMA((7,)),
            pltpu.SemaphoreType.DMA((2 * n + 1,))],
        compiler_params=pltpu.CompilerParams(vmem_limit_bytes=VMEM_LIMIT_BYTES),
    )(*halves, small)
    return res[:n], res[n]


BF16_SUBLANES = 16


def _row_tile(n, target, step=BF16_SUBLANES):
    return max([d for d in range(step, min(n, target) + 1, step) if n % d == 0] or [n])


def _ew(name, fn, ins, outs, block_bytes=1 << 20):
    rows, cols = ins[0].shape[-2:]
    lead = max(math.prod(a.shape[:-2]) for a in ins)
    tr = _row_tile(rows, max(8, block_bytes // (4 * cols * lead)))
    n = len(ins)

    def spec(shape):
        if len(shape) == 2:
            return pl.BlockSpec((tr, cols), lambda i: (i, 0))
        return pl.BlockSpec((shape[0], tr, cols), lambda i: (0, i, 0))

    def body(*refs):
        res = fn(*[r[...] for r in refs[:n]])
        for r, v in zip(refs[n:], res):
            r[...] = v

    return pl.pallas_call(
        body, name=name,
        out_shape=[jax.ShapeDtypeStruct(s, F32) for s in outs],
        grid=(rows // tr,),
        in_specs=[spec(a.shape) for a in ins],
        out_specs=[spec(s) for s in outs],
        compiler_params=_params(("parallel",)),
    )(*ins)


def _sum_slots(a):
    total = a[0].astype(F32)
    for s in range(1, a.shape[0]):
        total = total + a[s].astype(F32)
    return (total,)


def _adamw(g, w, m, v):
    bc1 = 1.0 - ADAM_B1 ** ADAM_STEP
    bc2 = 1.0 - ADAM_B2 ** ADAM_STEP
    m_new = ADAM_B1 * m + (1.0 - ADAM_B1) * g
    v_new = ADAM_B2 * v + (1.0 - ADAM_B2) * jnp.square(g)
    delta = -ADAM_LR * ((m_new / bc1) / (jnp.sqrt(v_new / bc2) + ADAM_EPS) + ADAM_WD * w)
    return delta, m_new, v_new


def _mat(a):
    return a.reshape(a.shape[-2:])


def _to_shard_major(full, axis):
    rows, cols = full.shape
    if axis == 0:
        return full.reshape(4, rows // 4, cols)
    return full.reshape(rows, 4, cols // 4).transpose(1, 0, 2)


def _from_shard_major(a, axis):
    _, r, cs = a.shape
    if axis == 0:
        return a.reshape(4 * r, cs)
    return a.transpose(1, 0, 2).reshape(r, 4 * cs)


def _pack_small(arrays, tail=None):
    flat = [arrays[n].reshape(-1) for n in SMALL_NAMES] + ([] if tail is None else [tail.reshape(1)])
    used = sum(a.shape[0] for a in flat)
    flat.append(jnp.zeros((8 * SMALL_ROWS * PACK_COLS - used,), F32))
    return jnp.concatenate(flat).reshape(8, SMALL_ROWS, PACK_COLS)


def _unpack_small(packed, shapes):
    flat = packed.reshape(-1)
    out, off = {}, 0
    for n in SMALL_NAMES:
        size = math.prod(shapes[n])
        out[n] = flat[off:off + size].reshape(shapes[n])
        off += size
    return out


def _row(a):
    return a.reshape(1, -1)


def _local_step(x, p, target, wf, ws, late_shards):
    wf = dict(wf)
    t = x.shape[0]
    tm = min(256, t)
    g = {}

    lam_re, lam_im = ws['s5_lam_re'].reshape(S5_GROUPS, S5_STATE), ws['s5_lam_im'].reshape(S5_GROUPS, S5_STATE)
    log_step = ws['s5_log_step'].reshape(S5_GROUPS, 1)
    gp = (S5_GROUPS, S5_STATE)
    lam_ins = (lam_re, lam_im, log_step)
    lbr, lbi, cfr, cfi = _small_fwd("s5_lam", _f_s5_lam, lam_ins, [(gp, F32)] * 4)
    lam_row = jnp.concatenate([_row(lbr), _row(lbi)], axis=1)
    to_t = lambda a, perm: a.reshape((S5_GROUPS,) + a.shape[-2:]).transpose(perm).reshape(S5_GROUP, S5_LANES)
    build_ins = (_row(cfr), _row(cfi), to_t(ws['s5_b_re'], (2, 0, 1)), to_t(ws['s5_b_im'], (2, 0, 1)),
                 to_t(ws['s5_c_re'], (1, 0, 2)), to_t(ws['s5_c_im'], (1, 0, 2)))
    block_shape = (S5_WIDTH, 2 * S5_TILE_LANES)
    b_blk, c_blk = _small_fwd("s5_build", _f_s5_build, build_ins, [(block_shape, F32)] * 2)

    norm_mix, norm_ffn, norm_ple = _row(ws['norm_mix']), _row(ws['norm_ffn']), _row(ws['norm_ple'])
    final_norm = _row(ws['final_norm'])
    (xn,) = _tok_fwd("norm_in", _f_norm_in, [x], [norm_mix], [(x.shape[1], BF16)], tm)
    u = _mm("proj_s5", xn, wf['w_in'][:, :S5_WIDTH], 'nn')
    z = _mm("proj_rw", xn, wf['w_in'][:, S5_WIDTH:], 'nn')

    bu = _s5_expand("s5_bu", u, b_blk)
    def late_plan(carrier):
        return _gather_plan([late_shards[n] for n in LATE_GATHER[carrier]])

    def arrived(carrier, got):
        wf.update({n: _from_shard_major(a, SHARDED[n]) for n, a in zip(LATE_GATHER[carrier], got)})

    xs, got = _s5_scan(bu, lam_row, tm, late_plan('s5_scan'))
    arrived('s5_scan', got)
    ypre = _s5_contract("s5_y", xs, c_blk)
    s5_par = [_row(ws['s5_d']), wf['s5_glu_w'], _row(ws['s5_glu_b'])]
    (s5_out,) = _tok_fwd("s5_post", _f_s5_post, [ypre, u], s5_par, [(S5_WIDTH, BF16)], tm)

    pre_par = [_row(ws['rw_shift_mu']), _row(ws['rw_w0']), wf['rw_w2'], _row(ws['rw_a0']), wf['rw_a2'],
               wf['rw_g2'], _row(ws['rw_k_k']), _row(ws['rw_k_a'])]
    r, lw, kp, v, an, bn, gate = _rw_pre_fwd(z, pre_par, tm)
    seqs = [r, lw, kp, v, an, bn]
    y_wkv, ck, got = _wkv_fwd(seqs, late_plan('wkv_fwd'))
    arrived('wkv_fwd', got)
    post_par = [_row(ws['rw_ln_w']), _row(ws['rw_ln_b']), _row(ws['rw_r_k'])]
    post_toks = [y_wkv, r, kp, v, gate]
    (rw_out,) = _tok_fwd("rw_post", _f_rw_post, post_toks, post_par, [(RWKV_WIDTH, BF16)], tm)

    mixcat = jnp.concatenate([s5_out, rw_out], axis=1)
    mixed = _mm("mix_out", mixcat, wf['w_out'], 'nn')
    h1, hn = _tok_fwd("mix_res", _f_mix_res, [x, mixed], [norm_ffn], [(x.shape[1], F32), (x.shape[1], BF16)], tm)
    w13 = jnp.concatenate([wf['ffn_w1'], wf['ffn_w3']], axis=1)
    a13, got = _mm("ffn_up", hn, w13, 'nn', out_dtype=BF16, plan=late_plan('ffn_up'))
    arrived('ffn_up', got)
    (f,) = _tok_fwd("ffn_act", _f_ffn_act, [a13], [], [(FFN_HIDDEN, BF16)], tm)
    ffo = _mm("ffn_down", f, wf['ffn_w2'], 'nn')
    h2, hp = _tok_fwd("ffn_res", _f_ffn_res, [h1, ffo], [norm_ple], [(x.shape[1], F32), (x.shape[1], BF16)], tm)
    gpre = _mm("ple_gate", hp, wf['ple_gate_w'], 'nn')
    pu = _mm("ple_up", p, wf['ple_up_w'], 'nn')

    dh2, dgpre, dpu, g['final_norm'], loss = _tok_bwd(
        "loss", _f_loss, [h2, gpre, pu, target], [final_norm], [None],
        [F32, BF16, BF16, None], [True], tm, acc_out=0)
    g['ple_gate_w'] = _mm("d_ple_gate_w", hp, dgpre, 'tn', out_dtype=WIRE)
    g['ple_up_w'] = _mm("d_ple_up_w", p, dpu, 'tn', out_dtype=WIRE)
    dhp = _mm("d_hp", dgpre, wf['ple_gate_w'], 'nt')
    dh1, dffo, g['norm_ple'] = _tok_bwd("ffn_res_bwd", _f_ffn_res, [h1, ffo], [norm_ple], [dh2, dhp],
                                        [F32, BF16], [True], tm)
    g['ffn_w2'] = _mm("d_ffn_w2", f, dffo, 'tn', out_dtype=WIRE)
    df = _mm("d_f", dffo, wf['ffn_w2'], 'nt', out_dtype=BF16)
    (da13,) = _tok_bwd("ffn_act_bwd", _f_ffn_act, [a13], [], [df], [BF16], [], tm)
    dw13 = _mm("d_ffn_w13", hn, da13, 'tn', out_dtype=WIRE)
    dw13 = dw13.reshape(dw13.shape[0], 8, FFN_HIDDEN // 4).transpose(1, 0, 2)
    shard_major = {'ffn_w1': dw13[:4], 'ffn_w3': dw13[4:]}
    dhn = _mm("d_hn", da13, w13, 'nt')
    dx_a, dmixed, g['norm_ffn'] = _tok_bwd("mix_res_bwd", _f_mix_res, [x, mixed], [norm_ffn], [dh1, dhn],
                                           [F32, BF16], [True], tm)
    g['w_out'] = _mm("d_w_out", mixcat, dmixed, 'tn', out_dtype=WIRE)
    dmixcat = _mm("d_mixcat", dmixed, wf['w_out'], 'nt')
    ds5_out, drw_out = dmixcat[:, :S5_WIDTH], dmixcat[:, S5_WIDTH:]

    dy_wkv, dr_b, dkp_b, dv_b, dgate, g['rw_ln_w'], g['rw_ln_b'], g['rw_r_k'] = _tok_bwd(
        "rw_post_bwd", _f_rw_post, post_toks, post_par, [drw_out], [F32] * 5, [True] * 3, tm)
    late_exchange = _exchange_plan([shard_major[n] if n in shard_major else
                                    _to_shard_major(g[n], SHARDED[n]).astype(WIRE) for n in LATE_NAMES])
    dseqs, late_pieces = _wkv_bwd(seqs, ck, dy_wkv, late_exchange)
    pre_cots = [(dseqs[0], dr_b), (dseqs[1],), (dseqs[2], dkp_b), (dseqs[3], dv_b), (dseqs[4],), (dseqs[5],),
                (dgate,)]
    dz, *dpre = _rw_pre_bwd(z, pre_par, pre_cots, tm)
    for n, d in zip(['rw_shift_mu', 'rw_w0', 'rw_w2', 'rw_a0', 'rw_a2', 'rw_g2', 'rw_k_k', 'rw_k_a'], dpre):
        g[n] = d

    dypre, du_a, g['s5_d'], g['s5_glu_w'], g['s5_glu_b'] = _tok_bwd(
        "s5_post_bwd", _f_s5_post, [ypre, u], s5_par, [ds5_out], [F32, F32], [True] * 3, tm)
    dxs = _s5_expand("d_s5_x", dypre, c_blk)
    dc_blk = _s5_block_grad("d_s5_c", dypre, xs)
    dbu, dlam_row = _s5_scan_bwd(dxs, xs, lam_row, tm)
    du_b = _s5_contract("d_s5_u", dbu, b_blk)
    db_blk = _s5_block_grad("d_s5_b", u, dbu)
    dbuild = _small_bwd("s5_build_bwd", _f_s5_build, build_ins, (db_blk, dc_blk))
    lam_cots = (dlam_row[:, :S5_LANES].reshape(gp), dlam_row[:, S5_LANES:].reshape(gp),
                dbuild[0].reshape(gp), dbuild[1].reshape(gp))
    g['s5_lam_re'], g['s5_lam_im'], g['s5_log_step'] = _small_bwd("s5_lam_bwd", _f_s5_lam, lam_ins, lam_cots)
    from_t = lambda a, perm: a.reshape(S5_GROUP, S5_GROUPS, S5_STATE).transpose(perm)
    g['s5_b_re'], g['s5_b_im'] = from_t(dbuild[2], (1, 2, 0)), from_t(dbuild[3], (1, 2, 0))
    g['s5_c_re'], g['s5_c_im'] = from_t(dbuild[4], (1, 0, 2)), from_t(dbuild[5], (1, 0, 2))

    dproj = jnp.concatenate([(du_a + du_b).astype(BF16), dz.astype(BF16)], axis=1)
    g['w_in'] = _mm("d_w_in", xn, dproj, 'tn', out_dtype=WIRE)
    dxn = _mm("d_xn", dproj, wf['w_in'], 'nt')
    grad_x, g['norm_mix'] = _tok_bwd("norm_in_bwd", _f_norm_in, [x], [norm_mix], [dxn], [F32], [True], tm,
                                     add_to=(0, dx_a))
    return loss[0, 0], grad_x, g, late_pieces


def _step(x, p, target, w, m, v):
    shards = {n: _mat(w[n]).astype(BF16) for n in SHARDED_NAMES}
    early = _run_plan("gather_early", _gather_plan([shards[n] for n in EARLY_NAMES]))
    wf = {n: _from_shard_major(a, SHARDED[n]) for n, a in zip(EARLY_NAMES, early)}
    ws = {n: w[n] for n in SMALL_NAMES}

    loss, grad_x, g, late_pieces = _local_step(x[0], p[0, 0], target[0], wf, ws, shards)

    early_plan = _exchange_plan([_to_shard_major(g[n], SHARDED[n]).astype(WIRE) for n in EARLY_NAMES],
                                _pack_small({n: g[n] for n in SMALL_NAMES}, tail=loss))
    *early_pieces, by_dev = _run_plan("exchange_early", early_plan)
    pieces = dict(zip(LATE_NAMES + EARLY_NAMES, list(late_pieces) + early_pieces))
    halves = [_ew("add_devices_" + n, _sum_slots, [pieces[n]], [pieces[n].shape[1:]])[0] for n in SHARDED_NAMES]
    (small_piece,) = _ew("add_devices_small", _sum_slots, [by_dev], [by_dev.shape[1:]])
    both, small_g = _share_cores(halves, small_piece)

    kinds = [{}, {}, {}, {}]
    for n, gn in zip(SHARDED_NAMES, both):
        shard = _mat(w[n]).shape
        res = _ew("adamw_" + n, _adamw, [gn.reshape(shard), _mat(w[n]), _mat(m[n]), _mat(v[n])], [shard] * 3)
        for kind, a in zip(kinds, [gn] + list(res)):
            kind[n] = a.reshape(w[n].shape)
    flat = (8 * SMALL_ROWS, PACK_COLS)
    packed = [_pack_small({n: d[n] for n in SMALL_NAMES}).reshape(flat) for d in (w, m, v)]
    small_res = _ew("adamw_small", _adamw, [small_g.reshape(flat)] + packed, [flat] * 3)
    small_shapes = {n: w[n].shape for n in SMALL_NAMES}
    for kind, a in zip(kinds, [small_g] + list(small_res)):
        kind.update(_unpack_small(a, small_shapes))
    total = small_g.reshape(-1)[sum(math.prod(s) for s in small_shapes.values())]
    return (total, grad_x[None], *[kind[n] for kind in kinds for n in WEIGHT_NAMES])


def kernel(x, p, norm_mix, w_in, s5_lam_re, s5_lam_im, s5_log_step, s5_b_re, s5_b_im, s5_c_re, s5_c_im, s5_d, s5_glu_w, s5_glu_b, rw_shift_mu, rw_w0, rw_w2, rw_a0, rw_a2, rw_g2, rw_k_k, rw_k_a, rw_r_k, rw_ln_w, rw_ln_b, w_out, norm_ffn, ffn_w1, ffn_w3, ffn_w2, norm_ple, ple_gate_w, ple_up_w, final_norm, loss_target, m_norm_mix, m_w_in, m_s5_lam_re, m_s5_lam_im, m_s5_log_step, m_s5_b_re, m_s5_b_im, m_s5_c_re, m_s5_c_im, m_s5_d, m_s5_glu_w, m_s5_glu_b, m_rw_shift_mu, m_rw_w0, m_rw_w2, m_rw_a0, m_rw_a2, m_rw_g2, m_rw_k_k, m_rw_k_a, m_rw_r_k, m_rw_ln_w, m_rw_ln_b, m_w_out, m_norm_ffn, m_ffn_w1, m_ffn_w3, m_ffn_w2, m_norm_ple, m_ple_gate_w, m_ple_up_w, m_final_norm, v_norm_mix, v_w_in, v_s5_lam_re, v_s5_lam_im, v_s5_log_step, v_s5_b_re, v_s5_b_im, v_s5_c_re, v_s5_c_im, v_s5_d, v_s5_glu_w, v_s5_glu_b, v_rw_shift_mu, v_rw_w0, v_rw_w2, v_rw_a0, v_rw_a2, v_rw_g2, v_rw_k_k, v_rw_k_a, v_rw_r_k, v_rw_ln_w, v_rw_ln_b, v_w_out, v_norm_ffn, v_ffn_w1, v_ffn_w3, v_ffn_w2, v_norm_ple, v_ple_gate_w, v_ple_up_w, v_final_norm):
    args = dict(locals())
    w = {n: args[n] for n in WEIGHT_NAMES}
    m = {n: args["m_" + n] for n in WEIGHT_NAMES}
    v = {n: args["v_" + n] for n in WEIGHT_NAMES}
    return _step(x, p, loss_target, w, m, v)
```

```python
import functools
import math
from typing import Any, Callable, NamedTuple, Sequence

import jax
import jax.numpy as jnp
from jax import lax
from jax.experimental import pallas as pl
from jax.experimental.pallas import tpu as pltpu

F32 = jnp.float32
BF16 = jnp.bfloat16
MESH = pl.DeviceIdType.MESH

S5_WIDTH = 512
RWKV_WIDTH = 512
S5_GROUP = 16
S5_GROUPS = 32
S5_STATE = 64
S5_LANES = S5_GROUPS * S5_STATE
S5_TILE_GROUPS = 8
S5_TILES = S5_GROUPS // S5_TILE_GROUPS
S5_TILE_CH = S5_TILE_GROUPS * S5_GROUP
S5_TILE_LANES = S5_TILE_GROUPS * S5_STATE
HEAD = 64
HEADS = 8
DECAY_LORA = 64
AAA_LORA = 64
GATE_LORA = 128
FFN_HIDDEN = 2816
FFN_TILE = FFN_HIDDEN // 2
RMS_EPS = 1e-6
GN_EPS = 64e-5
L2_EPS = 1e-12
ADAM_LR = 0.001
ADAM_B1 = 0.9
ADAM_B2 = 0.999
ADAM_EPS = 1e-08
ADAM_WD = 0.01
ADAM_STEP = 10

WKV_CHUNK = 64
SCAN_UNROLL = 4
WIRE = jnp.bfloat16
WKV_PASSES = 1
VMEM_LIMIT_BYTES = 48 * 1024 * 1024
LANE = 128
PACK_COLS = 1024
SMALL_ROWS = 24

WEIGHT_NAMES = ['norm_mix', 'w_in', 's5_lam_re', 's5_lam_im', 's5_log_step', 's5_b_re', 's5_b_im', 's5_c_re',
                's5_c_im', 's5_d', 's5_glu_w', 's5_glu_b', 'rw_shift_mu', 'rw_w0', 'rw_w2', 'rw_a0', 'rw_a2',
                'rw_g2', 'rw_k_k', 'rw_k_a', 'rw_r_k', 'rw_ln_w', 'rw_ln_b', 'w_out', 'norm_ffn', 'ffn_w1',
                'ffn_w3', 'ffn_w2', 'norm_ple', 'ple_gate_w', 'ple_up_w', 'final_norm']
SHARDED = {'w_in': 1, 's5_glu_w': 0, 'rw_w2': 1, 'rw_a2': 1, 'rw_g2': 1, 'w_out': 0, 'ffn_w1': 1, 'ffn_w3': 1,
           'ffn_w2': 0, 'ple_gate_w': 0, 'ple_up_w': 1}
SHARDED_NAMES = [n for n in WEIGHT_NAMES if n in SHARDED]
LATE_NAMES = ['w_out', 'ffn_w1', 'ffn_w3', 'ffn_w2', 'ple_gate_w', 'ple_up_w']
EARLY_NAMES = [n for n in SHARDED_NAMES if n not in LATE_NAMES]
LATE_GATHER = {'s5_scan': ['w_out', 'ple_gate_w', 'ple_up_w'], 'wkv_fwd': ['ffn_w1', 'ffn_w3'], 'ffn_up': ['ffn_w2']}
SMALL_NAMES = [n for n in WEIGHT_NAMES if n not in SHARDED]


def _params(sem=None):
    return pltpu.CompilerParams(dimension_semantics=sem, vmem_limit_bytes=VMEM_LIMIT_BYTES)


def _tile(n, target):
    best = None
    for d in range(LANE, min(n, target) + 1, LANE):
        if n % d == 0:
            best = d
    return n if best is None else best


_NN = (((1,), (0,)), ((), ()))
_NT = (((1,), (1,)), ((), ()))
_TN = (((0,), (0,)), ((), ()))


def _split(a):
    a = a.astype(F32)
    hi = a.astype(BF16)
    return hi, (a - hi.astype(F32)).astype(BF16)


def _dg(a, b, dims, passes):
    dg = lambda p, q: lax.dot_general(p, q, dims, preferred_element_type=F32)
    if passes == 1:
        return dg(a.astype(BF16), b.astype(BF16))
    bh, bl = _split(b)
    if passes == 2:
        return dg(a.astype(BF16), bh) + dg(a.astype(BF16), bl)
    ah, al = _split(a)
    return dg(ah, bh) + (dg(ah, bl) + dg(al, bh))


_DOT_BWD = {_NN: (("g", "b", _NT), ("a", "g", _TN)),
            _NT: (("g", "b", _NN), ("g", "a", _TN)),
            _TN: (("b", "g", _NT), ("a", "g", _NN))}


@functools.partial(jax.custom_vjp, nondiff_argnums=(2, 3))
def _dot(a, b, dims, passes):
    return _dg(a, b, dims, passes)


def _dot_fwd(a, b, dims, passes):
    return _dg(a, b, dims, passes), (a, b)


def _dot_bwd(dims, passes, res, g):
    env = {"a": res[0], "b": res[1], "g": g}
    return tuple(_dg(env[p], env[q], d, passes) for p, q, d in _DOT_BWD[dims])


_dot.defvjp(_dot_fwd, _dot_bwd)


def _bdot(x, w):
    return _dot(x, w, _NN, 1)


@jax.custom_vjp
def _shift_down(z):
    return pltpu.roll(z, 1, 0)


def _shift_down_fwd(z):
    return pltpu.roll(z, 1, 0), None


def _shift_down_bwd(_, g):
    return (pltpu.roll(g, g.shape[0] - 1, 0),)


_shift_down.defvjp(_shift_down_fwd, _shift_down_bwd)


def _head_sum_impl(x):
    r = lax.broadcasted_iota(jnp.int32, (LANE, LANE), 0) // HEAD
    c = lax.broadcasted_iota(jnp.int32, (LANE, LANE), 1) // HEAD
    ones = (r == c).astype(BF16)
    hi, lo = _split(x)
    dg = lambda p: lax.dot_general(p, ones, _NN, preferred_element_type=F32)
    tiles = [slice(j, j + LANE) for j in range(0, x.shape[1], LANE)]
    return jnp.concatenate([dg(hi[:, s]) + dg(lo[:, s]) for s in tiles], axis=1)


@jax.custom_vjp
def _head_sum(x):
    return _head_sum_impl(x)


_head_sum.defvjp(lambda x: (_head_sum_impl(x), None), lambda _, g: (_head_sum_impl(g),))


def _mm(name, a, b, mode, out_dtype=F32, precise=False, tm=1024, tn=1024, tk=1536, plan=None):
    if mode == 'nn':
        (m, k), (_, n) = a.shape, b.shape
    elif mode == 'nt':
        (m, k), (n, _) = a.shape, b.shape
    else:
        (k, m), (_, n) = a.shape, b.shape
    tm, tn, tk = _tile(m, tm), _tile(n, tn), _tile(k, tk)
    nm, nn, nk = m // tm, n // tn, k // tk
    dims = {'nn': _NN, 'nt': _NT, 'tn': _TN}[mode]
    plan = _NO_PLAN if plan is None else plan
    parts, plan_in_specs, plan_out_shape, plan_out_specs, plan_sems = _carry(plan, 2, 1)

    def body(*refs):
        a_ref, b_ref, o_ref = refs[0], refs[1], refs[2 + len(plan.ins)]
        acc_ref = refs[3 + len(plan.ins) + len(plan.out_shape)]
        i, j, kk = pl.program_id(0), pl.program_id(1), pl.program_id(2)

        if plan is not _NO_PLAN:
            pl.when((i == 0) & (j == 0) & (kk == 0))(lambda: plan.start(*parts(refs)))

        @pl.when(kk == 0)
        def _():
            acc_ref[...] = jnp.zeros_like(acc_ref)

        acc_ref[...] += _dg(a_ref[...], b_ref[...], dims, 3 if precise else 1)

        @pl.when(kk == nk - 1)
        def _():
            o_ref[...] = acc_ref[...].astype(o_ref.dtype)

        if plan is not _NO_PLAN:
            pl.when((i == nm - 1) & (j == nn - 1) & (kk == nk - 1))(lambda: plan.wait(*parts(refs)))

    if mode == 'tn':
        a_spec = pl.BlockSpec((tk, tm), lambda i, j, l: (l, i))
    else:
        a_spec = pl.BlockSpec((tm, tk), lambda i, j, l: (i, l))
    if mode == 'nt':
        b_spec = pl.BlockSpec((tn, tk), lambda i, j, l: (j, l))
    else:
        b_spec = pl.BlockSpec((tk, tn), lambda i, j, l: (l, j))
    res = pl.pallas_call(
        body, name=name,
        out_shape=[jax.ShapeDtypeStruct((m, n), out_dtype)] + plan_out_shape,
        grid=(nm, nn, nk),
        in_specs=[a_spec, b_spec] + plan_in_specs,
        out_specs=[pl.BlockSpec((tm, tn), lambda i, j, l: (i, j))] + plan_out_specs,
        scratch_shapes=[pltpu.VMEM((tm, tn), F32)] + plan_sems,
        compiler_params=_params(("parallel", "parallel", "arbitrary") if plan is _NO_PLAN else ("arbitrary",) * 3),
    )(a, b, *plan.ins)
    return res[0] if plan is _NO_PLAN else (res[0], res[1:])


def _mm_tiles(name, a, b, mode, out_shape, grid, a_spec, b_spec, o_spec):
    dims = {'nn': _NN, 'nt': _NT, 'tn': _TN}[mode]
    nk = grid[2]

    def body(a_ref, b_ref, o_ref, acc_ref):
        kk = pl.program_id(2)

        @pl.when(kk == 0)
        def _():
            acc_ref[...] = jnp.zeros_like(acc_ref)

        acc_ref[...] += _dg(a_ref[...], b_ref[...], dims, 1)

        @pl.when(kk == nk - 1)
        def _():
            o_ref[...] = acc_ref[...]

    return pl.pallas_call(
        body, name=name,
        out_shape=jax.ShapeDtypeStruct(out_shape, F32),
        grid=grid, in_specs=[a_spec, b_spec], out_specs=o_spec,
        scratch_shapes=[pltpu.VMEM(o_spec.block_shape, F32)],
        compiler_params=_params(("parallel", "parallel", "arbitrary")),
    )(a, b)


def _s5_expand(name, u, blk, tm=1024):
    t = u.shape[0]
    tm = min(tm, t)
    ch, ln, nt = S5_TILE_CH, S5_TILE_LANES, S5_TILES
    return _mm_tiles(name, u, blk, 'nn', (t, 2 * S5_LANES), (t // tm, 2 * nt, 1),
                     pl.BlockSpec((tm, ch), lambda i, j, l: (i, j % nt)),
                     pl.BlockSpec((ch, ln), lambda i, j, l: (j % nt, j // nt)),
                     pl.BlockSpec((tm, ln), lambda i, j, l: (i, j)))


def _s5_contract(name, x, blk, tm=1024):
    t = x.shape[0]
    tm = min(tm, t)
    ch, ln, nt = S5_TILE_CH, S5_TILE_LANES, S5_TILES
    return _mm_tiles(name, x, blk, 'nt', (t, S5_WIDTH), (t // tm, nt, 2),
                     pl.BlockSpec((tm, ln), lambda i, j, l: (i, j + nt * l)),
                     pl.BlockSpec((ch, ln), lambda i, j, l: (j, l)),
                     pl.BlockSpec((tm, ch), lambda i, j, l: (i, j)))


def _s5_block_grad(name, u, x, tk=1024):
    t = u.shape[0]
    tk = min(tk, t)
    ch, ln, nt = S5_TILE_CH, S5_TILE_LANES, S5_TILES
    return _mm_tiles(name, u, x, 'tn', (S5_WIDTH, 2 * ln), (nt, 2, t // tk),
                     pl.BlockSpec((tk, ch), lambda i, j, l: (l, i)),
                     pl.BlockSpec((tk, ln), lambda i, j, l: (l, i + nt * j)),
                     pl.BlockSpec((ch, ln), lambda i, j, l: (i, j)))


def _full_spec(p):
    nd = p.ndim
    return pl.BlockSpec(p.shape, lambda i, nd=nd: (0,) * nd)


def _tok_fwd(name, fn, toks, params, outs, tm):
    t = toks[0].shape[0]
    nt, npar = len(toks), len(params)

    def body(*refs):
        tv = [r[...].astype(F32) for r in refs[:nt]]
        pv = [r[...].astype(F32) for r in refs[nt:nt + npar]]
        res = fn(*tv, *pv)
        for r, v in zip(refs[nt + npar:], res):
            r[...] = v.astype(r.dtype)

    return pl.pallas_call(
        body, name=name,
        out_shape=[jax.ShapeDtypeStruct((t, w), d) for w, d in outs],
        grid=(t // tm,),
        in_specs=[pl.BlockSpec((tm, a.shape[1]), lambda i: (i, 0)) for a in toks] + [_full_spec(p) for p in params],
        out_specs=[pl.BlockSpec((tm, w), lambda i: (i, 0)) for w, _ in outs],
        compiler_params=_params(("parallel",)),
    )(*toks, *params)


def _tok_bwd(name, fn, toks, params, cots, dtok, dpar, tm, acc_out=None, add_to=None):
    t = toks[0].shape[0]
    nt, npar = len(toks), len(params)
    cot_arrays = [c for c in cots if c is not None]
    ncot = len(cot_arrays)
    extra = [] if add_to is None else [add_to[1]]
    dtok_idx = [i for i, d in enumerate(dtok) if d is not None]
    dpar_idx = [i for i, d in enumerate(dpar) if d]

    def body(*refs):
        pos = 0
        tin = refs[pos:pos + nt]; pos += nt
        pin = refs[pos:pos + npar]; pos += npar
        cin = refs[pos:pos + ncot]; pos += ncot
        ein = refs[pos:pos + len(extra)]; pos += len(extra)
        dto = refs[pos:pos + len(dtok_idx)]; pos += len(dtok_idx)
        dpo = refs[pos:pos + len(dpar_idx)]; pos += len(dpar_idx)
        acc = refs[pos] if acc_out is not None else None
        first = pl.program_id(0) == 0

        tv = [r[...].astype(F32) for r in tin]
        pv = [r[...].astype(F32) for r in pin]
        res, vjp = jax.vjp(fn, *tv, *pv)
        cit = iter(cin)
        cs = tuple(jnp.ones_like(o) if c is None else next(cit)[...].astype(F32) for c, o in zip(cots, res))
        g = vjp(cs)
        for r, i in zip(dto, dtok_idx):
            v = g[i]
            if add_to is not None and add_to[0] == i:
                v = v + ein[0][...].astype(F32)
            r[...] = v.astype(r.dtype)

        @pl.when(first)
        def _():
            for r in dpo:
                r[...] = jnp.zeros_like(r)
            if acc is not None:
                acc[...] = jnp.zeros_like(acc)

        for r, i in zip(dpo, dpar_idx):
            r[...] += g[nt + i]
        if acc is not None:
            acc[...] += res[acc_out]

    out_shape = [jax.ShapeDtypeStruct(toks[i].shape, dtok[i]) for i in dtok_idx]
    out_shape += [jax.ShapeDtypeStruct(params[i].shape, F32) for i in dpar_idx]
    out_specs = [pl.BlockSpec((tm, toks[i].shape[1]), lambda i_: (i_, 0)) for i in dtok_idx]
    out_specs += [_full_spec(params[i]) for i in dpar_idx]
    if acc_out is not None:
        out_shape.append(jax.ShapeDtypeStruct((1, 1), F32))
        out_specs.append(pl.BlockSpec((1, 1), lambda i_: (0, 0)))
    tok_spec = lambda a: pl.BlockSpec((tm, a.shape[1]), lambda i_: (i_, 0))
    return pl.pallas_call(
        body, name=name,
        out_shape=out_shape,
        grid=(t // tm,),
        in_specs=[tok_spec(a) for a in toks] + [_full_spec(p) for p in params]
        + [tok_spec(c) for c in cot_arrays] + [tok_spec(e) for e in extra],
        out_specs=out_specs,
        compiler_params=_params(("arbitrary",)),
    )(*toks, *params, *cot_arrays, *extra)


def _small_fwd(name, fn, ins, outs):
    n = len(ins)

    def body(*refs):
        res = fn(*[r[...] for r in refs[:n]])
        for r, v in zip(refs[n:], res):
            r[...] = v.astype(r.dtype)

    return pl.pallas_call(
        body, name=name,
        out_shape=[jax.ShapeDtypeStruct(s, d) for s, d in outs],
        compiler_params=_params(),
    )(*ins)


def _small_bwd(name, fn, ins, cots):
    n = len(ins)

    def body(*refs):
        _, vjp = jax.vjp(fn, *[r[...] for r in refs[:n]])
        g = vjp(tuple(r[...] for r in refs[n:n + len(cots)]))
        for r, v in zip(refs[n + len(cots):], g):
            r[...] = v

    return pl.pallas_call(
        body, name=name,
        out_shape=[jax.ShapeDtypeStruct(a.shape, F32) for a in ins],
        compiler_params=_params(),
    )(*ins, *cots)


def _rms(x, g):
    return x * lax.rsqrt(jnp.mean(x * x, axis=-1, keepdims=True) + RMS_EPS) * g


def _f_norm_in(x, g):
    return (_rms(x, g),)


def _f_mix_res(x, mixed, g):
    h1 = x + mixed
    return h1, _rms(h1, g)


def _f_ffn_act(a13):
    a1, a3 = a13[:, :FFN_HIDDEN], a13[:, FFN_HIDDEN:]
    return (jax.nn.silu(a1) * a3,)


def _f_ffn_res(h1, ffo, g):
    h2 = h1 + ffo
    return h2, _rms(h2, g)


def _f_loss(h2, gpre, pu, target, g):
    h3 = h2 + jax.nn.sigmoid(gpre) * pu
    y = _rms(h3, g)
    err = jnp.square(y - target)
    return (0.5 * jnp.sum(jnp.mean(err, axis=-1, keepdims=True), axis=0, keepdims=True),)


def _f_s5_post(ypre, u, d, glu_w, glu_b):
    z = jax.nn.gelu(ypre + u * d)
    return (z * jax.nn.sigmoid(_bdot(z, glu_w) + glu_b),)


def _softplus(x):
    return jnp.maximum(x, 0.0) + jnp.log(1.0 + jnp.exp(-jnp.abs(x)))


def _f_rw_pre(z, carry, shift_mu, w0, w2, a0, a2, g2, k_k, k_a):
    rw = RWKV_WIDTH
    first_row = lax.broadcasted_iota(jnp.int32, z.shape, 0) == 0
    prev = jnp.where(first_row, carry, _shift_down(z))
    zs = z + (prev - z) * shift_mu
    o1, o2 = 3 * rw + DECAY_LORA, 3 * rw + DECAY_LORA + AAA_LORA
    r, k, v = zs[:, :rw], zs[:, rw:2 * rw], zs[:, 2 * rw:3 * rw]
    wl, al, gl = zs[:, 3 * rw:o1], zs[:, o1:o2], zs[:, o2:]
    w = -_softplus(-(w0 + _bdot(jnp.tanh(wl), w2))) - 0.5
    log_decay = -jnp.exp(w)
    a = jax.nn.sigmoid(a0 + _bdot(al, a2))
    g = _bdot(jax.nn.sigmoid(gl), g2)
    kk = k * k_k
    norm = jnp.sqrt(_head_sum(kk * kk))
    kk = kk / jnp.maximum(norm, L2_EPS)
    kp = k * (1.0 + (a - 1.0) * k_a)
    return r, log_decay, kp, v, -kk, kk * a, g


def _f_rw_post(y, r, kp, v, g, ln_w, ln_b, r_k):
    yc = y - _head_sum(y) * (1.0 / HEAD)
    var = _head_sum(yc * yc) * (1.0 / HEAD)
    yn = yc * lax.rsqrt(var + GN_EPS) * ln_w + ln_b
    bonus = _head_sum(r * kp * r_k) * v
    return ((yn + bonus) * g,)


def _f_s5_lam(lam_re, lam_im, log_step):
    step = jnp.exp(log_step)
    dr, di = lam_re * step, lam_im * step
    e = jnp.exp(dr)
    lbr, lbi = e * jnp.cos(di), e * jnp.sin(di)
    nr, ni = lbr - 1.0, lbi
    den = lam_re * lam_re + lam_im * lam_im
    return lbr, lbi, (nr * lam_re + ni * lam_im) / den, (ni * lam_re - nr * lam_im) / den


def _f_s5_build(coef_r, coef_i, btr, bti, ctr, cti):
    bbr = coef_r * btr - coef_i * bti
    bbi = coef_r * bti + coef_i * btr
    shape = (S5_WIDTH, S5_TILE_LANES)
    rows = (lax.broadcasted_iota(jnp.int32, shape, 0) % S5_TILE_CH) // S5_GROUP
    cols = lax.broadcasted_iota(jnp.int32, shape, 1) // S5_STATE
    mask = (rows == cols).astype(F32)

    def blocks(m):
        per_tile = [m[:, S5_TILE_LANES * i:S5_TILE_LANES * (i + 1)] for i in range(S5_TILES)]
        return jnp.concatenate([t for t in per_tile for _ in range(S5_TILE_GROUPS)], axis=0) * mask

    return (jnp.concatenate([blocks(bbr), blocks(bbi)], axis=1),
            jnp.concatenate([blocks(ctr), -blocks(cti)], axis=1))


HALO = 8


def _rw_pre_specs(z, params, tm, order):
    halo_blocks = tm // HALO
    return ([pl.BlockSpec((tm, z.shape[1]), lambda i: (order(i), 0)),
             pl.BlockSpec((HALO, z.shape[1]), lambda i: (jnp.maximum(order(i) * halo_blocks - 1, 0), 0))]
            + [_full_spec(p) for p in params])


def _rw_pre_fwd(z, params, tm):
    t = z.shape[0]
    npar = len(params)

    def body(z_ref, halo_ref, *refs):
        carry = jnp.where(pl.program_id(0) == 0, 0.0, halo_ref[pl.ds(HALO - 1, 1), :])
        res = _f_rw_pre(z_ref[...], carry, *[r[...].astype(F32) for r in refs[:npar]])
        for r, v in zip(refs[npar:], res):
            r[...] = v

    return pl.pallas_call(
        body, name="rw_pre",
        out_shape=[jax.ShapeDtypeStruct((t, RWKV_WIDTH), F32)] * 7,
        grid=(t // tm,),
        in_specs=_rw_pre_specs(z, params, tm, lambda i: i),
        out_specs=[pl.BlockSpec((tm, RWKV_WIDTH), lambda i: (i, 0))] * 7,
        compiler_params=_params(("parallel",)),
    )(z, z, *params)


def _rw_pre_bwd(z, params, cots, tm):
    t = z.shape[0]
    nt = t // tm
    npar = len(params)
    order = lambda i: nt - 1 - i
    flat_cots = [a for group in cots for a in group]
    ncot = len(flat_cots)

    def body(z_ref, halo_ref, *refs):
        pin, cin = refs[:npar], list(refs[npar:npar + ncot])
        dz_ref = refs[npar + ncot]
        dpo = refs[npar + ncot + 1:npar + ncot + 1 + npar]
        dcarry_ref = refs[npar + ncot + 1 + npar]
        i = pl.program_id(0)

        @pl.when(i == 0)
        def _():
            dcarry_ref[...] = jnp.zeros_like(dcarry_ref)
            for r in dpo:
                r[...] = jnp.zeros_like(r)

        carry = jnp.where(i == nt - 1, 0.0, halo_ref[pl.ds(HALO - 1, 1), :])
        _, vjp = jax.vjp(_f_rw_pre, z_ref[...], carry, *[r[...].astype(F32) for r in pin])
        g = vjp(tuple(sum(cin.pop(0)[...] for _ in group) for group in cots))
        last_row = lax.broadcasted_iota(jnp.int32, z_ref.shape, 0) == tm - 1
        dz_ref[...] = g[0] + jnp.where(last_row, dcarry_ref[...], 0.0)
        dcarry_ref[...] = g[1]
        for r, v in zip(dpo, g[2:]):
            r[...] += v

    tok = lambda w: pl.BlockSpec((tm, w), lambda i: (order(i), 0))
    return pl.pallas_call(
        body, name="rw_pre_bwd",
        out_shape=[jax.ShapeDtypeStruct(z.shape, F32)] + [jax.ShapeDtypeStruct(p.shape, F32) for p in params],
        grid=(nt,),
        in_specs=_rw_pre_specs(z, params, tm, order) + [tok(RWKV_WIDTH)] * ncot,
        out_specs=[tok(z.shape[1])] + [_full_spec(p) for p in params],
        scratch_shapes=[pltpu.VMEM((1, z.shape[1]), F32)],
        compiler_params=_params(("arbitrary",)),
    )(z, z, *params, *flat_cots)


def _s5_scan(bu, lam, tm, plan):
    t, w = bu.shape
    h = w // 2
    nt = t // tm
    parts, plan_in_specs, plan_out_shape, plan_out_specs, plan_sems = _carry(plan, 2, 1)

    def body(*refs):
        bu_ref, lam_ref = refs[:2]
        xb_ref = refs[2 + len(plan.ins)]
        carry_ref = refs[3 + len(plan.ins) + len(plan.out_shape)]
        x_ref, refs = refs[-1], refs[:-1]

        @pl.when(pl.program_id(0) == 0)
        def _():
            carry_ref[...] = jnp.zeros_like(carry_ref)
            plan.start(*parts(refs))

        lr, li = lam_ref[:, :h], lam_ref[:, h:]

        def step(s, c):
            cr, ci = c
            row = pl.ds(s, 1)
            nr = lr * cr - li * ci + bu_ref[row, :h]
            ni = lr * ci + li * cr + bu_ref[row, h:]
            x_ref[row, :h] = nr
            x_ref[row, h:] = ni
            return nr, ni

        cr, ci = lax.fori_loop(0, tm, step, (carry_ref[:, :h], carry_ref[:, h:]), unroll=SCAN_UNROLL)
        carry_ref[:, :h] = cr
        carry_ref[:, h:] = ci
        xb_ref[...] = x_ref[...].astype(BF16)

        @pl.when(pl.program_id(0) == nt - 1)
        def _():
            plan.wait(*parts(refs))

    spec = pl.BlockSpec((tm, w), lambda i: (i, 0))
    res = pl.pallas_call(
        body, name="s5_scan",
        out_shape=[jax.ShapeDtypeStruct((t, w), BF16)] + plan_out_shape,
        grid=(nt,),
        in_specs=[spec, pl.BlockSpec((1, w), lambda i: (0, 0))] + plan_in_specs,
        out_specs=[spec] + plan_out_specs,
        scratch_shapes=[pltpu.VMEM((1, w), F32)] + plan_sems + [pltpu.VMEM((tm, w), F32)],
        compiler_params=_params(("arbitrary",)),
    )(bu, lam, *plan.ins)
    return res[0], res[1:]


def _s5_scan_bwd(dx, xb, lam, tm):
    t, w = dx.shape
    h = w // 2
    nt = t // tm
    halo = BF16_SUBLANES

    def body(dx_ref, xb_ref, halo_ref, lam_ref, dbu_out_ref, dlam_ref, carry_ref, dbu_ref):
        @pl.when(pl.program_id(0) == 0)
        def _():
            carry_ref[...] = jnp.zeros_like(carry_ref)
            dlam_ref[...] = jnp.zeros_like(dlam_ref)

        lr, li = lam_ref[:, :h], lam_ref[:, h:]

        def step(s, c):
            cr, ci = c
            row = pl.ds(tm - 1 - s, 1)
            nr = lr * cr + li * ci + dx_ref[row, :h]
            ni = lr * ci - li * cr + dx_ref[row, h:]
            dbu_ref[row, :h] = nr
            dbu_ref[row, h:] = ni
            return nr, ni

        cr, ci = lax.fori_loop(0, tm, step, (carry_ref[:, :h], carry_ref[:, h:]), unroll=SCAN_UNROLL)
        carry_ref[:, :h] = cr
        carry_ref[:, h:] = ci
        halo_rows = lax.broadcasted_iota(jnp.int32, (halo, w), 0)
        before = jnp.sum(jnp.where(halo_rows == halo - 1, halo_ref[...].astype(F32), 0.0), axis=0, keepdims=True)
        before = jnp.where(pl.program_id(0) == nt - 1, 0.0, before)
        first_row = lax.broadcasted_iota(jnp.int32, (tm, w), 0) == 0
        xp = jnp.where(first_row, before, pltpu.roll(xb_ref[...].astype(F32), 1, 0))
        gr, gi = dbu_ref[:, :h], dbu_ref[:, h:]
        pr, pi_ = xp[:, :h], xp[:, h:]
        dlam_ref[:, :h] += jnp.sum(gr * pr + gi * pi_, axis=0, keepdims=True)
        dlam_ref[:, h:] += jnp.sum(gi * pr - gr * pi_, axis=0, keepdims=True)
        dbu_out_ref[...] = dbu_ref[...].astype(BF16)

    spec = pl.BlockSpec((tm, w), lambda i: (nt - 1 - i, 0))
    halo_spec = pl.BlockSpec((halo, w), lambda i: (jnp.maximum((nt - 1 - i) * (tm // halo) - 1, 0), 0))
    row_spec = pl.BlockSpec((1, w), lambda i: (0, 0))
    return pl.pallas_call(
        body, name="s5_scan_bwd",
        out_shape=[jax.ShapeDtypeStruct((t, w), BF16), jax.ShapeDtypeStruct((1, w), F32)],
        grid=(nt,),
        in_specs=[spec, spec, halo_spec, row_spec],
        out_specs=[spec, row_spec],
        scratch_shapes=[pltpu.VMEM((1, w), F32), pltpu.VMEM((tm, w), F32)],
        compiler_params=_params(("arbitrary",)),
    )(dx, xb, xb, lam)


def _wkv_chunks(s0, r, lw, k, v, a, b):
    c = r[0].shape[0]
    row = lax.broadcasted_iota(jnp.int32, (c, c), 0)
    col = lax.broadcasted_iota(jnp.int32, (c, c), 1)
    incl, strict = col <= row, col < row
    tri = incl.astype(F32)
    eye = (row == col).astype(F32)
    each = lambda f, *xs: [f(*t) for t in zip(*xs)]
    stack = lambda p, q: jnp.concatenate([p, q], axis=0)
    dot = lambda p, q, dims=_NN: _dot(p, q, dims, WKV_PASSES)
    lc = each(lambda l: _dot(tri, l, _NN, 2), lw)
    e_neg = each(lambda l: jnp.exp(-l), lc)
    ar = each(lambda x, z, l, w: stack(x * jnp.exp(l - w), z * jnp.exp(l)), a, r, lc, lw)
    bk = each(lambda x, z, e: stack(x * e, z * e), b, k, e_neg)
    m = each(lambda p, q: dot(p, q, _NT), ar, bk)
    mab = each(lambda q: jnp.where(strict, q[:c, :c], 0.0), m)
    mak_mrk = each(lambda q: stack(jnp.where(strict, q[:c, c:], 0.0), jnp.where(incl, q[c:, c:], 0.0)), m)
    mrb = each(lambda q: jnp.where(incl, q[c:, :c], 0.0), m)
    xy = each(lambda p, s, q, z: dot(p, s, _NT) + dot(q, z), ar, s0, mak_mrk, v)
    inv = each(lambda q: eye + q, mab)
    pw = each(lambda q: dot(q, q), mab)
    for _ in range(int(math.log2(c)) - 2):
        both = each(lambda i, q: dot(stack(i, q), q), inv, pw)
        inv = each(lambda i, q: i + q[:c], inv, both)
        pw = each(lambda q: q[c:], both)
    inv = each(lambda i, q: i + dot(i, q), inv, pw)
    u = each(lambda i, q: dot(i, q[:c]), inv, xy)
    y = each(lambda q, z, p: q[c:] + dot(z, p), xy, mrb, u)
    e_tot = each(lambda l: jnp.exp(jnp.sum(l, axis=0, keepdims=True)), lw)
    s1 = each(lambda s, p, z, q, e: (s + dot(stack(p, z), q, _TN)) * e, s0, u, v, bk, e_tot)
    return y, s1


def _carry(plan, n_args, n_outs):
    n_in, n_out = len(plan.ins), len(plan.out_shape)

    def parts(refs):
        base = n_args + n_in + n_outs
        return refs[n_args:n_args + n_in], refs[base:base + n_out], refs[base + n_out + 1:]

    return parts, [HBM_SPEC] * n_in, list(plan.out_shape), [HBM_SPEC] * n_out, list(plan.sems)


def _head_cols(ref):
    return tuple(ref[:, h * HEAD:(h + 1) * HEAD] for h in range(HEADS))


def _wkv_fwd(seqs, plan):
    t, w = seqs[0].shape
    c, n = WKV_CHUNK, HEAD
    nc = t // c
    parts, plan_in_specs, plan_out_shape, plan_out_specs, plan_sems = _carry(plan, 6, 2)

    def body(*refs):
        ins, (y_ref, ck_ref) = refs[:6], refs[6 + len(plan.ins):8 + len(plan.ins)]
        s_ref = refs[8 + len(plan.ins) + len(plan.out_shape)]

        @pl.when(pl.program_id(0) == 0)
        def _():
            s_ref[...] = jnp.zeros_like(s_ref)
            plan.start(*parts(refs))

        s0 = tuple(s_ref[h] for h in range(HEADS))
        ys, s1 = _wkv_chunks(s0, *[_head_cols(r) for r in ins])
        for h in range(HEADS):
            ck_ref[0, h] = s0[h]
            y_ref[:, h * n:(h + 1) * n] = ys[h]
            s_ref[h] = s1[h]

        @pl.when(pl.program_id(0) == nc - 1)
        def _():
            plan.wait(*parts(refs))

    spec = pl.BlockSpec((c, w), lambda i: (i, 0))
    res = pl.pallas_call(
        body, name="wkv_fwd",
        out_shape=[jax.ShapeDtypeStruct((t, w), F32), jax.ShapeDtypeStruct((nc, HEADS, n, n), F32)] + plan_out_shape,
        grid=(nc,),
        in_specs=[spec] * 6 + plan_in_specs,
        out_specs=[spec, pl.BlockSpec((1, HEADS, n, n), lambda i: (i, 0, 0, 0))] + plan_out_specs,
        scratch_shapes=[pltpu.VMEM((HEADS, n, n), F32)] + plan_sems,
        compiler_params=_params(("arbitrary",)),
    )(*seqs, *plan.ins)
    return res[0], res[1], res[2:]


def _wkv_bwd(seqs, ck, dy, plan):
    t, w = seqs[0].shape
    c, n = WKV_CHUNK, HEAD
    nc = t // c
    parts, plan_in_specs, plan_out_shape, plan_out_specs, plan_sems = _carry(plan, 8, 6)

    def body(*refs):
        ins, ck_ref, dy_ref = refs[:6], refs[6], refs[7]
        outs = refs[8 + len(plan.ins):14 + len(plan.ins)]
        ds_ref = refs[14 + len(plan.ins) + len(plan.out_shape)]

        @pl.when(pl.program_id(0) == 0)
        def _():
            ds_ref[...] = jnp.zeros_like(ds_ref)
            plan.start(*parts(refs))

        s0 = tuple(ck_ref[0, h] for h in range(HEADS))
        _, vjp = jax.vjp(_wkv_chunks, s0, *[_head_cols(r) for r in ins])
        g = vjp((list(_head_cols(dy_ref)), [ds_ref[h] for h in range(HEADS)]))
        for h in range(HEADS):
            ds_ref[h] = g[0][h]
            for o, d in zip(outs, g[1:]):
                o[:, h * n:(h + 1) * n] = d[h]

        @pl.when(pl.program_id(0) == nc - 1)
        def _():
            plan.wait(*parts(refs))

    spec = pl.BlockSpec((c, w), lambda i: (nc - 1 - i, 0))
    res = pl.pallas_call(
        body, name="wkv_bwd",
        out_shape=[jax.ShapeDtypeStruct((t, w), F32)] * 6 + plan_out_shape,
        grid=(nc,),
        in_specs=[spec] * 6 + [pl.BlockSpec((1, HEADS, n, n), lambda i: (nc - 1 - i, 0, 0, 0)), spec] + plan_in_specs,
        out_specs=[spec] * 6 + plan_out_specs,
        scratch_shapes=[pltpu.VMEM((HEADS, n, n), F32)] + plan_sems,
        compiler_params=_params(("arbitrary",)),
    )(*seqs, ck, dy, *plan.ins)
    return res[:6], res[6:]


def _coords():
    return lax.axis_index("x"), lax.axis_index("y"), lax.axis_index("c")


def _flip(v, f):
    return 1 - v if f else v


_CHIP_FLIPS = [(1, 0), (0, 1), (1, 1)]
_DEV_FLIPS = [(fx, fy, fc) for fx in (0, 1) for fy in (0, 1) for fc in (0, 1) if (fx, fy, fc) != (0, 0, 0)]
HBM_SPEC = pl.BlockSpec(memory_space=pl.ANY)


def _chip_peer(k, x, y):
    fx, fy = _CHIP_FLIPS[k]
    return _flip(x, fx), _flip(y, fy)


def _dev_peer(k, x, y, c):
    fx, fy, fc = _DEV_FLIPS[k]
    return _flip(x, fx), _flip(y, fy), _flip(c, fc)


def _rows_of_core(ref, core):
    h = ref.shape[-2] // 2
    rows = pl.ds(pl.multiple_of(core * h, 8), h)
    return ref.at[rows, :] if len(ref.shape) == 2 else ref.at[:, rows, :]


class _Plan(NamedTuple):
    ins: Sequence[Any]
    out_shape: Sequence[Any]
    sems: Sequence[Any]
    start: Callable
    wait: Callable


_NO_PLAN = _Plan([], [], [], lambda *_: None, lambda *_: None)


def _run_plan(name, plan):
    n_in, n_out = len(plan.ins), len(plan.out_shape)

    def body(*refs):
        parts = refs[:n_in], refs[n_in:n_in + n_out], refs[n_in + n_out:]
        plan.start(*parts)
        plan.wait(*parts)

    return pl.pallas_call(
        body, name=name, out_shape=list(plan.out_shape),
        in_specs=[HBM_SPEC] * n_in, out_specs=[HBM_SPEC] * n_out, scratch_shapes=list(plan.sems),
    )(*plan.ins)


def _gather_plan(shards):
    n = len(shards)

    def copies(srcs, outs, sems):
        send_sems, recv_sems, local_sems = sems
        x, y, c = _coords()
        me = 2 * x + y

        def remote(i, k, arriving):
            px, py = _chip_peer(k, x, y)
            return pltpu.make_async_remote_copy(
                src_ref=srcs[i], dst_ref=outs[i].at[2 * px + py if arriving else me],
                send_sem=send_sems.at[i, k], recv_sem=recv_sems.at[i, k],
                device_id=(px, py, c), device_id_type=MESH)

        own = [pltpu.make_async_copy(srcs[i], outs[i].at[me], local_sems.at[i]) for i in range(n)]
        pairs = [(i, k) for k in range(3) for i in range(n)]
        return own, [remote(i, k, False) for i, k in pairs], [remote(i, k, True) for i, k in pairs]

    return _Plan(
        ins=shards, out_shape=[jax.ShapeDtypeStruct((4,) + s.shape, s.dtype) for s in shards],
        sems=[pltpu.SemaphoreType.DMA((n, 3)), pltpu.SemaphoreType.DMA((n, 3)), pltpu.SemaphoreType.DMA((n,))],
        start=functools.partial(_start_copies, copies), wait=functools.partial(_wait_copies, copies))


def _start_copies(copies, ins, outs, sems):
    own, sends, _ = copies(ins, outs, sems)
    for cp in own + sends:
        cp.start()


def _wait_copies(copies, ins, outs, sems):
    own, sends, arrivals = copies(ins, outs, sems)
    for cp in arrivals:
        cp.wait_recv()
    for cp in sends:
        cp.wait_send()
    for cp in own:
        cp.wait()


def _exchange_plan(gs, small=None):
    n = len(gs)
    arrays = list(gs) + ([] if small is None else [small])

    def copies(srcs, outs, sems):
        send_sems, recv_sems, local_sems = sems
        x, y, c = _coords()
        me = 4 * x + 2 * y + c

        def piece(i, px, py, pc):
            if i == n:
                return srcs[i].at[4 * px + 2 * py + pc]
            return _rows_of_core(srcs[i].at[2 * px + py], pc)

        def remote(i, k, arriving):
            px, py, pc = _dev_peer(k, x, y, c)
            return pltpu.make_async_remote_copy(
                src_ref=piece(i, px, py, pc), dst_ref=outs[i].at[4 * px + 2 * py + pc if arriving else me],
                send_sem=send_sems.at[i, k], recv_sem=recv_sems.at[i, k],
                device_id=(px, py, pc), device_id_type=MESH)

        own = [pltpu.make_async_copy(piece(i, x, y, c), outs[i].at[me], local_sems.at[i]) for i in range(len(arrays))]
        pairs = [(i, k) for k in range(7) for i in range(len(arrays))]
        return own, [remote(i, k, False) for i, k in pairs], [remote(i, k, True) for i, k in pairs]

    out_shape = [jax.ShapeDtypeStruct((8, g.shape[1] // 2, g.shape[2]), g.dtype) for g in gs]
    if small is not None:
        out_shape.append(jax.ShapeDtypeStruct(small.shape, small.dtype))
    m = len(arrays)
    return _Plan(
        ins=arrays, out_shape=out_shape,
        sems=[pltpu.SemaphoreType.DMA((m, 7)), pltpu.SemaphoreType.DMA((m, 7)), pltpu.SemaphoreType.DMA((m,))],
        start=functools.partial(_start_copies, copies), wait=functools.partial(_wait_copies, copies))


def _share_cores(halves, small):
    n = len(halves)

    def body(*refs):
        srcs, small_src, outs, small_out = refs[:n], refs[n], refs[n + 1:2 * n + 1], refs[2 * n + 1]
        mine, theirs = refs[2 * n + 2:3 * n + 2], refs[3 * n + 2:4 * n + 2]
        send_sems, recv_sems, ssend, srecv, local_sems = refs[4 * n + 2:]
        x, y, c = _coords()
        me = 4 * x + 2 * y + c

        def big(i):
            return pltpu.make_async_remote_copy(
                src_ref=mine[i], dst_ref=theirs[i], send_sem=send_sems.at[i], recv_sem=recv_sems.at[i],
                device_id=(x, y, 1 - c), device_id_type=MESH)

        def tiny(k, arriving):
            px, py, pc = _dev_peer(k, x, y, c)
            return pltpu.make_async_remote_copy(
                src_ref=small_src, dst_ref=small_out.at[4 * px + 2 * py + pc if arriving else me],
                send_sem=ssend.at[k], recv_sem=srecv.at[k], device_id=(px, py, pc), device_id_type=MESH)

        small_sends = [tiny(k, False) for k in range(7)]
        own_small = pltpu.make_async_copy(small_src, small_out.at[me], local_sems.at[2 * n])
        stage = [pltpu.make_async_copy(srcs[i], mine[i], local_sems.at[i]) for i in range(n)]
        for cp in small_sends + [own_small] + stage:
            cp.start()
        sends = []
        for i in range(n):
            stage[i].wait()
            sends.append(big(i))
            sends[-1].start()
        store = [pltpu.make_async_copy(mine[i], outs[i].at[c], local_sems.at[i]) for i in range(n)]
        for cp in store:
            cp.start()
        for i in range(n):
            big(i).wait_recv()
            store.append(pltpu.make_async_copy(theirs[i], outs[i].at[1 - c], local_sems.at[n + i]))
            store[-1].start()
        for k in range(7):
            tiny(k, True).wait_recv()
        for cp in sends + small_sends:
            cp.wait_send()
        for cp in store + [own_small]:
            cp.wait()

    staged = [pltpu.VMEM(s.shape, s.dtype) for s in halves]
    res = pl.pallas_call(
        body, name="share_cores",
        out_shape=[jax.ShapeDtypeStruct((2,) + s.shape, s.dtype) for s in halves]
        + [jax.ShapeDtypeStruct((8,) + small.shape, small.dtype)],
        in_specs=[HBM_SPEC] * (n + 1), out_specs=[HBM_SPEC] * (n + 1),
        scratch_shapes=staged + staged + [
            pltpu.SemaphoreType.DMA((n,)), pltpu.SemaphoreType.DMA((n,)),
            pltpu.SemaphoreType.DMA((7,)), pltpu.SemaphoreType.DMA((7,)),
            pltpu.SemaphoreType.DMA((2 * n + 1,))],
        compiler_params=pltpu.CompilerParams(vmem_limit_bytes=VMEM_LIMIT_BYTES),
    )(*halves, small)
    return res[:n], res[n]


BF16_SUBLANES = 16


def _row_tile(n, target, step=BF16_SUBLANES):
    return max([d for d in range(step, min(n, target) + 1, step) if n % d == 0] or [n])


def _ew(name, fn, ins, outs, block_bytes=1 << 20):
    rows, cols = ins[0].shape[-2:]
    lead = max(math.prod(a.shape[:-2]) for a in ins)
    tr = _row_tile(rows, max(8, block_bytes // (4 * cols * lead)))
    n = len(ins)

    def spec(shape):
        if len(shape) == 2:
            return pl.BlockSpec((tr, cols), lambda i: (i, 0))
        return pl.BlockSpec((shape[0], tr, cols), lambda i: (0, i, 0))

    def body(*refs):
        res = fn(*[r[...] for r in refs[:n]])
        for r, v in zip(refs[n:], res):
            r[...] = v

    return pl.pallas_call(
        body, name=name,
        out_shape=[jax.ShapeDtypeStruct(s, F32) for s in outs],
        grid=(rows // tr,),
        in_specs=[spec(a.shape) for a in ins],
        out_specs=[spec(s) for s in outs],
        compiler_params=_params(("parallel",)),
    )(*ins)


def _sum_slots(a):
    total = a[0].astype(F32)
    for s in range(1, a.shape[0]):
        total = total + a[s].astype(F32)
    return (total,)


def _adamw(g, w, m, v):
    bc1 = 1.0 - ADAM_B1 ** ADAM_STEP
    bc2 = 1.0 - ADAM_B2 ** ADAM_STEP
    m_new = ADAM_B1 * m + (1.0 - ADAM_B1) * g
    v_new = ADAM_B2 * v + (1.0 - ADAM_B2) * jnp.square(g)
    delta = -ADAM_LR * ((m_new / bc1) / (jnp.sqrt(v_new / bc2) + ADAM_EPS) + ADAM_WD * w)
    return delta, m_new, v_new


def _mat(a):
    return a.reshape(a.shape[-2:])


def _to_shard_major(full, axis):
    rows, cols = full.shape
    if axis == 0:
        return full.reshape(4, rows // 4, cols)
    return full.reshape(rows, 4, cols // 4).transpose(1, 0, 2)


def _from_shard_major(a, axis):
    _, r, cs = a.shape
    if axis == 0:
        return a.reshape(4 * r, cs)
    return a.transpose(1, 0, 2).reshape(r, 4 * cs)


def _pack_small(arrays, tail=None):
    flat = [arrays[n].reshape(-1) for n in SMALL_NAMES] + ([] if tail is None else [tail.reshape(1)])
    used = sum(a.shape[0] for a in flat)
    flat.append(jnp.zeros((8 * SMALL_ROWS * PACK_COLS - used,), F32))
    return jnp.concatenate(flat).reshape(8, SMALL_ROWS, PACK_COLS)


def _unpack_small(packed, shapes):
    flat = packed.reshape(-1)
    out, off = {}, 0
    for n in SMALL_NAMES:
        size = math.prod(shapes[n])
        out[n] = flat[off:off + size].reshape(shapes[n])
        off += size
    return out


def _row(a):
    return a.reshape(1, -1)


def _local_step(x, p, target, wf, ws, late_shards):
    wf = dict(wf)
    t = x.shape[0]
    tm = min(256, t)
    g = {}

    lam_re, lam_im = ws['s5_lam_re'].reshape(S5_GROUPS, S5_STATE), ws['s5_lam_im'].reshape(S5_GROUPS, S5_STATE)
    log_step = ws['s5_log_step'].reshape(S5_GROUPS, 1)
    gp = (S5_GROUPS, S5_STATE)
    lam_ins = (lam_re, lam_im, log_step)
    lbr, lbi, cfr, cfi = _small_fwd("s5_lam", _f_s5_lam, lam_ins, [(gp, F32)] * 4)
    lam_row = jnp.concatenate([_row(lbr), _row(lbi)], axis=1)
    to_t = lambda a, perm: a.reshape((S5_GROUPS,) + a.shape[-2:]).transpose(perm).reshape(S5_GROUP, S5_LANES)
    build_ins = (_row(cfr), _row(cfi), to_t(ws['s5_b_re'], (2, 0, 1)), to_t(ws['s5_b_im'], (2, 0, 1)),
                 to_t(ws['s5_c_re'], (1, 0, 2)), to_t(ws['s5_c_im'], (1, 0, 2)))
    block_shape = (S5_WIDTH, 2 * S5_TILE_LANES)
    b_blk, c_blk = _small_fwd("s5_build", _f_s5_build, build_ins, [(block_shape, F32)] * 2)

    norm_mix, norm_ffn, norm_ple = _row(ws['norm_mix']), _row(ws['norm_ffn']), _row(ws['norm_ple'])
    final_norm = _row(ws['final_norm'])
    (xn,) = _tok_fwd("norm_in", _f_norm_in, [x], [norm_mix], [(x.shape[1], BF16)], tm)
    u = _mm("proj_s5", xn, wf['w_in'][:, :S5_WIDTH], 'nn')
    z = _mm("proj_rw", xn, wf['w_in'][:, S5_WIDTH:], 'nn')

    bu = _s5_expand("s5_bu", u, b_blk)
    def late_plan(carrier):
        return _gather_plan([late_shards[n] for n in LATE_GATHER[carrier]])

    def arrived(carrier, got):
        wf.update({n: _from_shard_major(a, SHARDED[n]) for n, a in zip(LATE_GATHER[carrier], got)})

    xs, got = _s5_scan(bu, lam_row, tm, late_plan('s5_scan'))
    arrived('s5_scan', got)
    ypre = _s5_contract("s5_y", xs, c_blk)
    s5_par = [_row(ws['s5_d']), wf['s5_glu_w'], _row(ws['s5_glu_b'])]
    (s5_out,) = _tok_fwd("s5_post", _f_s5_post, [ypre, u], s5_par, [(S5_WIDTH, BF16)], tm)

    pre_par = [_row(ws['rw_shift_mu']), _row(ws['rw_w0']), wf['rw_w2'], _row(ws['rw_a0']), wf['rw_a2'],
               wf['rw_g2'], _row(ws['rw_k_k']), _row(ws['rw_k_a'])]
    r, lw, kp, v, an, bn, gate = _rw_pre_fwd(z, pre_par, tm)
    seqs = [r, lw, kp, v, an, bn]
    y_wkv, ck, got = _wkv_fwd(seqs, late_plan('wkv_fwd'))
    arrived('wkv_fwd', got)
    post_par = [_row(ws['rw_ln_w']), _row(ws['rw_ln_b']), _row(ws['rw_r_k'])]
    post_toks = [y_wkv, r, kp, v, gate]
    (rw_out,) = _tok_fwd("rw_post", _f_rw_post, post_toks, post_par, [(RWKV_WIDTH, BF16)], tm)

    mixcat = jnp.concatenate([s5_out, rw_out], axis=1)
    mixed = _mm("mix_out", mixcat, wf['w_out'], 'nn')
    h1, hn = _tok_fwd("mix_res", _f_mix_res, [x, mixed], [norm_ffn], [(x.shape[1], F32), (x.shape[1], BF16)], tm)
    w13 = jnp.concatenate([wf['ffn_w1'], wf['ffn_w3']], axis=1)
    a13, got = _mm("ffn_up", hn, w13, 'nn', out_dtype=BF16, tn=FFN_TILE, plan=late_plan('ffn_up'))
    arrived('ffn_up', got)
    (f,) = _tok_fwd("ffn_act", _f_ffn_act, [a13], [], [(FFN_HIDDEN, BF16)], tm)
    ffo = _mm("ffn_down", f, wf['ffn_w2'], 'nn')
    h2, hp = _tok_fwd("ffn_res", _f_ffn_res, [h1, ffo], [norm_ple], [(x.shape[1], F32), (x.shape[1], BF16)], tm)
    gpre = _mm("ple_gate", hp, wf['ple_gate_w'], 'nn')
    pu = _mm("ple_up", p, wf['ple_up_w'], 'nn')

    dh2, dgpre, dpu, g['final_norm'], loss = _tok_bwd(
        "loss", _f_loss, [h2, gpre, pu, target], [final_norm], [None],
        [F32, BF16, BF16, None], [True], tm, acc_out=0)
    g['ple_gate_w'] = _mm("d_ple_gate_w", hp, dgpre, 'tn', out_dtype=WIRE)
    g['ple_up_w'] = _mm("d_ple_up_w", p, dpu, 'tn', out_dtype=WIRE)
    dhp = _mm("d_hp", dgpre, wf['ple_gate_w'], 'nt')
    dh1, dffo, g['norm_ple'] = _tok_bwd("ffn_res_bwd", _f_ffn_res, [h1, ffo], [norm_ple], [dh2, dhp],
                                        [F32, BF16], [True], tm)
    g['ffn_w2'] = _mm("d_ffn_w2", f, dffo, 'tn', out_dtype=WIRE, tm=FFN_TILE)
    df = _mm("d_f", dffo, wf['ffn_w2'], 'nt', out_dtype=BF16, tn=FFN_TILE)
    (da13,) = _tok_bwd("ffn_act_bwd", _f_ffn_act, [a13], [], [df], [BF16], [], tm)
    dw13 = _mm("d_ffn_w13", hn, da13, 'tn', out_dtype=WIRE, tn=FFN_TILE)
    dw13 = dw13.reshape(dw13.shape[0], 8, FFN_HIDDEN // 4).transpose(1, 0, 2)
    shard_major = {'ffn_w1': dw13[:4], 'ffn_w3': dw13[4:]}
    dhn = _mm("d_hn", da13, w13, 'nt')
    dx_a, dmixed, g['norm_ffn'] = _tok_bwd("mix_res_bwd", _f_mix_res, [x, mixed], [norm_ffn], [dh1, dhn],
                                           [F32, BF16], [True], tm)
    g['w_out'] = _mm("d_w_out", mixcat, dmixed, 'tn', out_dtype=WIRE)
    dmixcat = _mm("d_mixcat", dmixed, wf['w_out'], 'nt')
    ds5_out, drw_out = dmixcat[:, :S5_WIDTH], dmixcat[:, S5_WIDTH:]

    dy_wkv, dr_b, dkp_b, dv_b, dgate, g['rw_ln_w'], g['rw_ln_b'], g['rw_r_k'] = _tok_bwd(
        "rw_post_bwd", _f_rw_post, post_toks, post_par, [drw_out], [F32] * 5, [True] * 3, tm)
    late_exchange = _exchange_plan([shard_major[n] if n in shard_major else
                                    _to_shard_major(g[n], SHARDED[n]).astype(WIRE) for n in LATE_NAMES])
    dseqs, late_pieces = _wkv_bwd(seqs, ck, dy_wkv, late_exchange)
    pre_cots = [(dseqs[0], dr_b), (dseqs[1],), (dseqs[2], dkp_b), (dseqs[3], dv_b), (dseqs[4],), (dseqs[5],),
                (dgate,)]
    dz, *dpre = _rw_pre_bwd(z, pre_par, pre_cots, tm)
    for n, d in zip(['rw_shift_mu', 'rw_w0', 'rw_w2', 'rw_a0', 'rw_a2', 'rw_g2', 'rw_k_k', 'rw_k_a'], dpre):
        g[n] = d

    dypre, du_a, g['s5_d'], g['s5_glu_w'], g['s5_glu_b'] = _tok_bwd(
        "s5_post_bwd", _f_s5_post, [ypre, u], s5_par, [ds5_out], [F32, F32], [True] * 3, tm)
    dxs = _s5_expand("d_s5_x", dypre, c_blk)
    dc_blk = _s5_block_grad("d_s5_c", dypre, xs)
    dbu, dlam_row = _s5_scan_bwd(dxs, xs, lam_row, tm)
    du_b = _s5_contract("d_s5_u", dbu, b_blk)
    db_blk = _s5_block_grad("d_s5_b", u, dbu)
    dbuild = _small_bwd("s5_build_bwd", _f_s5_build, build_ins, (db_blk, dc_blk))
    lam_cots = (dlam_row[:, :S5_LANES].reshape(gp), dlam_row[:, S5_LANES:].reshape(gp),
                dbuild[0].reshape(gp), dbuild[1].reshape(gp))
    g['s5_lam_re'], g['s5_lam_im'], g['s5_log_step'] = _small_bwd("s5_lam_bwd", _f_s5_lam, lam_ins, lam_cots)
    from_t = lambda a, perm: a.reshape(S5_GROUP, S5_GROUPS, S5_STATE).transpose(perm)
    g['s5_b_re'], g['s5_b_im'] = from_t(dbuild[2], (1, 2, 0)), from_t(dbuild[3], (1, 2, 0))
    g['s5_c_re'], g['s5_c_im'] = from_t(dbuild[4], (1, 0, 2)), from_t(dbuild[5], (1, 0, 2))

    dproj = jnp.concatenate([(du_a + du_b).astype(BF16), dz.astype(BF16)], axis=1)
    g['w_in'] = _mm("d_w_in", xn, dproj, 'tn', out_dtype=WIRE)
    dxn = _mm("d_xn", dproj, wf['w_in'], 'nt')
    grad_x, g['norm_mix'] = _tok_bwd("norm_in_bwd", _f_norm_in, [x], [norm_mix], [dxn], [F32], [True], tm,
                                     add_to=(0, dx_a))
    return loss[0, 0], grad_x, g, late_pieces


def _step(x, p, target, w, m, v):
    shards = {n: _mat(w[n]).astype(BF16) for n in SHARDED_NAMES}
    early = _run_plan("gather_early", _gather_plan([shards[n] for n in EARLY_NAMES]))
    wf = {n: _from_shard_major(a, SHARDED[n]) for n, a in zip(EARLY_NAMES, early)}
    ws = {n: w[n] for n in SMALL_NAMES}

    loss, grad_x, g, late_pieces = _local_step(x[0], p[0, 0], target[0], wf, ws, shards)

    early_plan = _exchange_plan([_to_shard_major(g[n], SHARDED[n]).astype(WIRE) for n in EARLY_NAMES],
                                _pack_small({n: g[n] for n in SMALL_NAMES}, tail=loss))
    *early_pieces, by_dev = _run_plan("exchange_early", early_plan)
    pieces = dict(zip(LATE_NAMES + EARLY_NAMES, list(late_pieces) + early_pieces))
    halves = [_ew("add_devices_" + n, _sum_slots, [pieces[n]], [pieces[n].shape[1:]])[0] for n in SHARDED_NAMES]
    (small_piece,) = _ew("add_devices_small", _sum_slots, [by_dev], [by_dev.shape[1:]])
    both, small_g = _share_cores(halves, small_piece)

    kinds = [{}, {}, {}, {}]
    for n, gn in zip(SHARDED_NAMES, both):
        shard = _mat(w[n]).shape
        res = _ew("adamw_" + n, _adamw, [gn.reshape(shard), _mat(w[n]), _mat(m[n]), _mat(v[n])], [shard] * 3)
        for kind, a in zip(kinds, [gn] + list(res)):
            kind[n] = a.reshape(w[n].shape)
    flat = (8 * SMALL_ROWS, PACK_COLS)
    packed = [_pack_small({n: d[n] for n in SMALL_NAMES}).reshape(flat) for d in (w, m, v)]
    small_res = _ew("adamw_small", _adamw, [small_g.reshape(flat)] + packed, [flat] * 3)
    small_shapes = {n: w[n].shape for n in SMALL_NAMES}
    for kind, a in zip(kinds, [small_g] + list(small_res)):
        kind.update(_unpack_small(a, small_shapes))
    total = small_g.reshape(-1)[sum(math.prod(s) for s in small_shapes.values())]
    return (total, grad_x[None], *[kind[n] for kind in kinds for n in WEIGHT_NAMES])


def kernel(x, p, norm_mix, w_in, s5_lam_re, s5_lam_im, s5_log_step, s5_b_re, s5_b_im, s5_c_re, s5_c_im, s5_d, s5_glu_w, s5_glu_b, rw_shift_mu, rw_w0, rw_w2, rw_a0, rw_a2, rw_g2, rw_k_k, rw_k_a, rw_r_k, rw_ln_w, rw_ln_b, w_out, norm_ffn, ffn_w1, ffn_w3, ffn_w2, norm_ple, ple_gate_w, ple_up_w, final_norm, loss_target, m_norm_mix, m_w_in, m_s5_lam_re, m_s5_lam_im, m_s5_log_step, m_s5_b_re, m_s5_b_im, m_s5_c_re, m_s5_c_im, m_s5_d, m_s5_glu_w, m_s5_glu_b, m_rw_shift_mu, m_rw_w0, m_rw_w2, m_rw_a0, m_rw_a2, m_rw_g2, m_rw_k_k, m_rw_k_a, m_rw_r_k, m_rw_ln_w, m_rw_ln_b, m_w_out, m_norm_ffn, m_ffn_w1, m_ffn_w3, m_ffn_w2, m_norm_ple, m_ple_gate_w, m_ple_up_w, m_final_norm, v_norm_mix, v_w_in, v_s5_lam_re, v_s5_lam_im, v_s5_log_step, v_s5_b_re, v_s5_b_im, v_s5_c_re, v_s5_c_im, v_s5_d, v_s5_glu_w, v_s5_glu_b, v_rw_shift_mu, v_rw_w0, v_rw_w2, v_rw_a0, v_rw_a2, v_rw_g2, v_rw_k_k, v_rw_k_a, v_rw_r_k, v_rw_ln_w, v_rw_ln_b, v_w_out, v_norm_ffn, v_ffn_w1, v_ffn_w3, v_ffn_w2, v_norm_ple, v_ple_gate_w, v_ple_up_w, v_final_norm):
    args = dict(locals())
    w = {n: args[n] for n in WEIGHT_NAMES}
    m = {n: args["m_" + n] for n in WEIGHT_NAMES}
    v = {n: args["v_" + n] for n in WEIGHT_NAMES}
    return _step(x, p, loss_target, w, m, v)
```

```python
import functools
import math
from typing import Any, Callable, NamedTuple, Sequence

import jax
import jax.numpy as jnp
from jax import lax
from jax.experimental import pallas as pl
from jax.experimental.pallas import tpu as pltpu

F32 = jnp.float32
BF16 = jnp.bfloat16
MESH = pl.DeviceIdType.MESH

S5_WIDTH = 512
RWKV_WIDTH = 512
S5_GROUP = 16
S5_GROUPS = 32
S5_STATE = 64
S5_LANES = S5_GROUPS * S5_STATE
S5_TILE_GROUPS = 8
S5_TILES = S5_GROUPS // S5_TILE_GROUPS
S5_TILE_CH = S5_TILE_GROUPS * S5_GROUP
S5_TILE_LANES = S5_TILE_GROUPS * S5_STATE
HEAD = 64
HEADS = 8
DECAY_LORA = 64
AAA_LORA = 64
GATE_LORA = 128
FFN_HIDDEN = 2816
FFN_TILE = FFN_HIDDEN // 2
RMS_EPS = 1e-6
GN_EPS = 64e-5
L2_EPS = 1e-12
ADAM_LR = 0.001
ADAM_B1 = 0.9
ADAM_B2 = 0.999
ADAM_EPS = 1e-08
ADAM_WD = 0.01
ADAM_STEP = 10

WKV_CHUNK = 64
SCAN_UNROLL = 4
WIRE = jnp.bfloat16
WKV_PASSES = 1
VMEM_LIMIT_BYTES = 48 * 1024 * 1024
LANE = 128
PACK_COLS = 1024
SMALL_ROWS = 24

WEIGHT_NAMES = ['norm_mix', 'w_in', 's5_lam_re', 's5_lam_im', 's5_log_step', 's5_b_re', 's5_b_im', 's5_c_re',
                's5_c_im', 's5_d', 's5_glu_w', 's5_glu_b', 'rw_shift_mu', 'rw_w0', 'rw_w2', 'rw_a0', 'rw_a2',
                'rw_g2', 'rw_k_k', 'rw_k_a', 'rw_r_k', 'rw_ln_w', 'rw_ln_b', 'w_out', 'norm_ffn', 'ffn_w1',
                'ffn_w3', 'ffn_w2', 'norm_ple', 'ple_gate_w', 'ple_up_w', 'final_norm']
SHARDED = {'w_in': 1, 's5_glu_w': 0, 'rw_w2': 1, 'rw_a2': 1, 'rw_g2': 1, 'w_out': 0, 'ffn_w1': 1, 'ffn_w3': 1,
           'ffn_w2': 0, 'ple_gate_w': 0, 'ple_up_w': 1}
SHARDED_NAMES = [n for n in WEIGHT_NAMES if n in SHARDED]
LATE_NAMES = ['w_out', 'ffn_w1', 'ffn_w3', 'ffn_w2', 'ple_gate_w', 'ple_up_w']
EARLY_NAMES = [n for n in SHARDED_NAMES if n not in LATE_NAMES]
LATE_GATHER = {'s5_scan': ['w_out', 'ple_gate_w', 'ple_up_w'], 'wkv_fwd': ['ffn_w1', 'ffn_w3'], 'ffn_up': ['ffn_w2']}
SMALL_NAMES = [n for n in WEIGHT_NAMES if n not in SHARDED]


def _params(sem=None):
    return pltpu.CompilerParams(dimension_semantics=sem, vmem_limit_bytes=VMEM_LIMIT_BYTES)


def _tile(n, target):
    best = None
    for d in range(LANE, min(n, target) + 1, LANE):
        if n % d == 0:
            best = d
    return n if best is None else best


_NN = (((1,), (0,)), ((), ()))
_NT = (((1,), (1,)), ((), ()))
_TN = (((0,), (0,)), ((), ()))


def _split(a):
    a = a.astype(F32)
    hi = a.astype(BF16)
    return hi, (a - hi.astype(F32)).astype(BF16)


def _dg(a, b, dims, passes):
    dg = lambda p, q: lax.dot_general(p, q, dims, preferred_element_type=F32)
    if passes == 1:
        return dg(a.astype(BF16), b.astype(BF16))
    bh, bl = _split(b)
    if passes == 2:
        return dg(a.astype(BF16), bh) + dg(a.astype(BF16), bl)
    ah, al = _split(a)
    return dg(ah, bh) + (dg(ah, bl) + dg(al, bh))


_DOT_BWD = {_NN: (("g", "b", _NT), ("a", "g", _TN)),
            _NT: (("g", "b", _NN), ("g", "a", _TN)),
            _TN: (("b", "g", _NT), ("a", "g", _NN))}


@functools.partial(jax.custom_vjp, nondiff_argnums=(2, 3))
def _dot(a, b, dims, passes):
    return _dg(a, b, dims, passes)


def _dot_fwd(a, b, dims, passes):
    return _dg(a, b, dims, passes), (a, b)


def _dot_bwd(dims, passes, res, g):
    env = {"a": res[0], "b": res[1], "g": g}
    return tuple(_dg(env[p], env[q], d, passes) for p, q, d in _DOT_BWD[dims])


_dot.defvjp(_dot_fwd, _dot_bwd)


def _bdot(x, w):
    return _dot(x, w, _NN, 1)


@jax.custom_vjp
def _shift_down(z):
    return pltpu.roll(z, 1, 0)


def _shift_down_fwd(z):
    return pltpu.roll(z, 1, 0), None


def _shift_down_bwd(_, g):
    return (pltpu.roll(g, g.shape[0] - 1, 0),)


_shift_down.defvjp(_shift_down_fwd, _shift_down_bwd)


def _head_sum_impl(x):
    r = lax.broadcasted_iota(jnp.int32, (LANE, LANE), 0) // HEAD
    c = lax.broadcasted_iota(jnp.int32, (LANE, LANE), 1) // HEAD
    ones = (r == c).astype(BF16)
    hi, lo = _split(x)
    dg = lambda p: lax.dot_general(p, ones, _NN, preferred_element_type=F32)
    tiles = [slice(j, j + LANE) for j in range(0, x.shape[1], LANE)]
    return jnp.concatenate([dg(hi[:, s]) + dg(lo[:, s]) for s in tiles], axis=1)


@jax.custom_vjp
def _head_sum(x):
    return _head_sum_impl(x)


_head_sum.defvjp(lambda x: (_head_sum_impl(x), None), lambda _, g: (_head_sum_impl(g),))


def _mm(name, a, b, mode, out_dtype=F32, precise=False, tm=1024, tn=1024, tk=1536, plan=None):
    if mode == 'nn':
        (m, k), (_, n) = a.shape, b.shape
    elif mode == 'nt':
        (m, k), (n, _) = a.shape, b.shape
    else:
        (k, m), (_, n) = a.shape, b.shape
    tm, tn, tk = _tile(m, tm), _tile(n, tn), _tile(k, tk)
    nm, nn, nk = m // tm, n // tn, k // tk
    dims = {'nn': _NN, 'nt': _NT, 'tn': _TN}[mode]
    plan = _NO_PLAN if plan is None else plan
    parts, plan_in_specs, plan_out_shape, plan_out_specs, plan_sems = _carry(plan, 2, 1)

    def body(*refs):
        a_ref, b_ref, o_ref = refs[0], refs[1], refs[2 + len(plan.ins)]
        acc_ref = refs[3 + len(plan.ins) + len(plan.out_shape)]
        i, j, kk = pl.program_id(0), pl.program_id(1), pl.program_id(2)

        if plan is not _NO_PLAN:
            pl.when((i == 0) & (j == 0) & (kk == 0))(lambda: plan.start(*parts(refs)))

        @pl.when(kk == 0)
        def _():
            acc_ref[...] = jnp.zeros_like(acc_ref)

        acc_ref[...] += _dg(a_ref[...], b_ref[...], dims, 3 if precise else 1)

        @pl.when(kk == nk - 1)
        def _():
            o_ref[...] = acc_ref[...].astype(o_ref.dtype)

        if plan is not _NO_PLAN:
            pl.when((i == nm - 1) & (j == nn - 1) & (kk == nk - 1))(lambda: plan.wait(*parts(refs)))

    if mode == 'tn':
        a_spec = pl.BlockSpec((tk, tm), lambda i, j, l: (l, i))
    else:
        a_spec = pl.BlockSpec((tm, tk), lambda i, j, l: (i, l))
    if mode == 'nt':
        b_spec = pl.BlockSpec((tn, tk), lambda i, j, l: (j, l))
    else:
        b_spec = pl.BlockSpec((tk, tn), lambda i, j, l: (l, j))
    res = pl.pallas_call(
        body, name=name,
        out_shape=[jax.ShapeDtypeStruct((m, n), out_dtype)] + plan_out_shape,
        grid=(nm, nn, nk),
        in_specs=[a_spec, b_spec] + plan_in_specs,
        out_specs=[pl.BlockSpec((tm, tn), lambda i, j, l: (i, j))] + plan_out_specs,
        scratch_shapes=[pltpu.VMEM((tm, tn), F32)] + plan_sems,
        compiler_params=_params(("parallel", "parallel", "arbitrary") if plan is _NO_PLAN else ("arbitrary",) * 3),
    )(a, b, *plan.ins)
    return res[0] if plan is _NO_PLAN else (res[0], res[1:])


def _mm_tiles(name, a, b, mode, out_shape, grid, a_spec, b_spec, o_spec):
    dims = {'nn': _NN, 'nt': _NT, 'tn': _TN}[mode]
    nk = grid[2]

    def body(a_ref, b_ref, o_ref, acc_ref):
        kk = pl.program_id(2)

        @pl.when(kk == 0)
        def _():
            acc_ref[...] = jnp.zeros_like(acc_ref)

        acc_ref[...] += _dg(a_ref[...], b_ref[...], dims, 1)

        @pl.when(kk == nk - 1)
        def _():
            o_ref[...] = acc_ref[...]

    return pl.pallas_call(
        body, name=name,
        out_shape=jax.ShapeDtypeStruct(out_shape, F32),
        grid=grid, in_specs=[a_spec, b_spec], out_specs=o_spec,
        scratch_shapes=[pltpu.VMEM(o_spec.block_shape, F32)],
        compiler_params=_params(("parallel", "parallel", "arbitrary")),
    )(a, b)


def _s5_expand(name, u, blk, tm=1024):
    t = u.shape[0]
    tm = min(tm, t)
    ch, ln, nt = S5_TILE_CH, S5_TILE_LANES, S5_TILES
    return _mm_tiles(name, u, blk, 'nn', (t, 2 * S5_LANES), (t // tm, 2 * nt, 1),
                     pl.BlockSpec((tm, ch), lambda i, j, l: (i, j % nt)),
                     pl.BlockSpec((ch, ln), lambda i, j, l: (j % nt, j // nt)),
                     pl.BlockSpec((tm, ln), lambda i, j, l: (i, j)))


def _s5_contract(name, x, blk, tm=1024):
    t = x.shape[0]
    tm = min(tm, t)
    ch, ln, nt = S5_TILE_CH, S5_TILE_LANES, S5_TILES
    return _mm_tiles(name, x, blk, 'nt', (t, S5_WIDTH), (t // tm, nt, 2),
                     pl.BlockSpec((tm, ln), lambda i, j, l: (i, j + nt * l)),
                     pl.BlockSpec((ch, ln), lambda i, j, l: (j, l)),
                     pl.BlockSpec((tm, ch), lambda i, j, l: (i, j)))


def _s5_block_grad(name, u, x, tk=1024):
    t = u.shape[0]
    tk = min(tk, t)
    ch, ln, nt = S5_TILE_CH, S5_TILE_LANES, S5_TILES
    return _mm_tiles(name, u, x, 'tn', (S5_WIDTH, 2 * ln), (nt, 2, t // tk),
                     pl.BlockSpec((tk, ch), lambda i, j, l: (l, i)),
                     pl.BlockSpec((tk, ln), lambda i, j, l: (l, i + nt * j)),
                     pl.BlockSpec((ch, ln), lambda i, j, l: (i, j)))


def _full_spec(p):
    nd = p.ndim
    return pl.BlockSpec(p.shape, lambda i, nd=nd: (0,) * nd)


def _tok_fwd(name, fn, toks, params, outs, tm):
    t = toks[0].shape[0]
    nt, npar = len(toks), len(params)

    def body(*refs):
        tv = [r[...].astype(F32) for r in refs[:nt]]
        pv = [r[...].astype(F32) for r in refs[nt:nt + npar]]
        res = fn(*tv, *pv)
        for r, v in zip(refs[nt + npar:], res):
            r[...] = v.astype(r.dtype)

    return pl.pallas_call(
        body, name=name,
        out_shape=[jax.ShapeDtypeStruct((t, w), d) for w, d in outs],
        grid=(t // tm,),
        in_specs=[pl.BlockSpec((tm, a.shape[1]), lambda i: (i, 0)) for a in toks] + [_full_spec(p) for p in params],
        out_specs=[pl.BlockSpec((tm, w), lambda i: (i, 0)) for w, _ in outs],
        compiler_params=_params(("parallel",)),
    )(*toks, *params)


def _tok_bwd(name, fn, toks, params, cots, dtok, dpar, tm, acc_out=None, add_to=None):
    t = toks[0].shape[0]
    nt, npar = len(toks), len(params)
    cot_arrays = [c for c in cots if c is not None]
    ncot = len(cot_arrays)
    extra = [] if add_to is None else [add_to[1]]
    dtok_idx = [i for i, d in enumerate(dtok) if d is not None]
    dpar_idx = [i for i, d in enumerate(dpar) if d]

    def body(*refs):
        pos = 0
        tin = refs[pos:pos + nt]; pos += nt
        pin = refs[pos:pos + npar]; pos += npar
        cin = refs[pos:pos + ncot]; pos += ncot
        ein = refs[pos:pos + len(extra)]; pos += len(extra)
        dto = refs[pos:pos + len(dtok_idx)]; pos += len(dtok_idx)
        dpo = refs[pos:pos + len(dpar_idx)]; pos += len(dpar_idx)
        acc = refs[pos] if acc_out is not None else None
        first = pl.program_id(0) == 0

        tv = [r[...].astype(F32) for r in tin]
        pv = [r[...].astype(F32) for r in pin]
        res, vjp = jax.vjp(fn, *tv, *pv)
        cit = iter(cin)
        cs = tuple(jnp.ones_like(o) if c is None else next(cit)[...].astype(F32) for c, o in zip(cots, res))
        g = vjp(cs)
        for r, i in zip(dto, dtok_idx):
            v = g[i]
            if add_to is not None and add_to[0] == i:
                v = v + ein[0][...].astype(F32)
            r[...] = v.astype(r.dtype)

        @pl.when(first)
        def _():
            for r in dpo:
                r[...] = jnp.zeros_like(r)
            if acc is not None:
                acc[...] = jnp.zeros_like(acc)

        for r, i in zip(dpo, dpar_idx):
            r[...] += g[nt + i]
        if acc is not None:
            acc[...] += res[acc_out]

    out_shape = [jax.ShapeDtypeStruct(toks[i].shape, dtok[i]) for i in dtok_idx]
    out_shape += [jax.ShapeDtypeStruct(params[i].shape, F32) for i in dpar_idx]
    out_specs = [pl.BlockSpec((tm, toks[i].shape[1]), lambda i_: (i_, 0)) for i in dtok_idx]
    out_specs += [_full_spec(params[i]) for i in dpar_idx]
    if acc_out is not None:
        out_shape.append(jax.ShapeDtypeStruct((1, 1), F32))
        out_specs.append(pl.BlockSpec((1, 1), lambda i_: (0, 0)))
    tok_spec = lambda a: pl.BlockSpec((tm, a.shape[1]), lambda i_: (i_, 0))
    return pl.pallas_call(
        body, name=name,
        out_shape=out_shape,
        grid=(t // tm,),
        in_specs=[tok_spec(a) for a in toks] + [_full_spec(p) for p in params]
        + [tok_spec(c) for c in cot_arrays] + [tok_spec(e) for e in extra],
        out_specs=out_specs,
        compiler_params=_params(("arbitrary",)),
    )(*toks, *params, *cot_arrays, *extra)


def _small_fwd(name, fn, ins, outs):
    n = len(ins)

    def body(*refs):
        res = fn(*[r[...] for r in refs[:n]])
        for r, v in zip(refs[n:], res):
            r[...] = v.astype(r.dtype)

    return pl.pallas_call(
        body, name=name,
        out_shape=[jax.ShapeDtypeStruct(s, d) for s, d in outs],
        compiler_params=_params(),
    )(*ins)


def _small_bwd(name, fn, ins, cots):
    n = len(ins)

    def body(*refs):
        _, vjp = jax.vjp(fn, *[r[...] for r in refs[:n]])
        g = vjp(tuple(r[...] for r in refs[n:n + len(cots)]))
        for r, v in zip(refs[n + len(cots):], g):
            r[...] = v

    return pl.pallas_call(
        body, name=name,
        out_shape=[jax.ShapeDtypeStruct(a.shape, F32) for a in ins],
        compiler_params=_params(),
    )(*ins, *cots)


def _rms(x, g):
    return x * lax.rsqrt(jnp.mean(x * x, axis=-1, keepdims=True) + RMS_EPS) * g


def _f_norm_in(x, g):
    return (_rms(x, g),)


def _f_mix_res(x, mixed, g):
    h1 = x + mixed
    return h1, _rms(h1, g)


def _f_ffn_act(a13):
    a1, a3 = a13[:, :FFN_HIDDEN], a13[:, FFN_HIDDEN:]
    return (jax.nn.silu(a1) * a3,)


def _f_ffn_res(h1, ffo, g):
    h2 = h1 + ffo
    return h2, _rms(h2, g)


def _f_loss(h2, gpre, pu, target, g):
    h3 = h2 + jax.nn.sigmoid(gpre) * pu
    y = _rms(h3, g)
    err = jnp.square(y - target)
    return (0.5 * jnp.sum(jnp.mean(err, axis=-1, keepdims=True), axis=0, keepdims=True),)


def _f_s5_post(ypre, u, d, glu_w, glu_b):
    z = jax.nn.gelu(ypre + u * d)
    return (z * jax.nn.sigmoid(_bdot(z, glu_w) + glu_b),)


def _softplus(x):
    return jnp.maximum(x, 0.0) + jnp.log(1.0 + jnp.exp(-jnp.abs(x)))


def _f_rw_pre(z, carry, shift_mu, w0, w2, a0, a2, g2, k_k, k_a):
    rw = RWKV_WIDTH
    first_row = lax.broadcasted_iota(jnp.int32, z.shape, 0) == 0
    prev = jnp.where(first_row, carry, _shift_down(z))
    zs = z + (prev - z) * shift_mu
    o1, o2 = 3 * rw + DECAY_LORA, 3 * rw + DECAY_LORA + AAA_LORA
    r, k, v = zs[:, :rw], zs[:, rw:2 * rw], zs[:, 2 * rw:3 * rw]
    wl, al, gl = zs[:, 3 * rw:o1], zs[:, o1:o2], zs[:, o2:]
    w = -_softplus(-(w0 + _bdot(jnp.tanh(wl), w2))) - 0.5
    log_decay = -jnp.exp(w)
    a = jax.nn.sigmoid(a0 + _bdot(al, a2))
    g = _bdot(jax.nn.sigmoid(gl), g2)
    kk = k * k_k
    norm = jnp.sqrt(_head_sum(kk * kk))
    kk = kk / jnp.maximum(norm, L2_EPS)
    kp = k * (1.0 + (a - 1.0) * k_a)
    return r, log_decay, kp, v, -kk, kk * a, g


def _f_rw_post(y, r, kp, v, g, ln_w, ln_b, r_k):
    yc = y - _head_sum(y) * (1.0 / HEAD)
    var = _head_sum(yc * yc) * (1.0 / HEAD)
    yn = yc * lax.rsqrt(var + GN_EPS) * ln_w + ln_b
    bonus = _head_sum(r * kp * r_k) * v
    return ((yn + bonus) * g,)


def _f_s5_lam(lam_re, lam_im, log_step):
    step = jnp.exp(log_step)
    dr, di = lam_re * step, lam_im * step
    e = jnp.exp(dr)
    lbr, lbi = e * jnp.cos(di), e * jnp.sin(di)
    nr, ni = lbr - 1.0, lbi
    den = lam_re * lam_re + lam_im * lam_im
    return lbr, lbi, (nr * lam_re + ni * lam_im) / den, (ni * lam_re - nr * lam_im) / den


def _f_s5_build(coef_r, coef_i, btr, bti, ctr, cti):
    bbr = coef_r * btr - coef_i * bti
    bbi = coef_r * bti + coef_i * btr
    shape = (S5_WIDTH, S5_TILE_LANES)
    rows = (lax.broadcasted_iota(jnp.int32, shape, 0) % S5_TILE_CH) // S5_GROUP
    cols = lax.broadcasted_iota(jnp.int32, shape, 1) // S5_STATE
    mask = (rows == cols).astype(F32)

    def blocks(m):
        per_tile = [m[:, S5_TILE_LANES * i:S5_TILE_LANES * (i + 1)] for i in range(S5_TILES)]
        return jnp.concatenate([t for t in per_tile for _ in range(S5_TILE_GROUPS)], axis=0) * mask

    return (jnp.concatenate([blocks(bbr), blocks(bbi)], axis=1),
            jnp.concatenate([blocks(ctr), -blocks(cti)], axis=1))


HALO = 8


def _rw_pre_specs(z, params, tm, order):
    halo_blocks = tm // HALO
    return ([pl.BlockSpec((tm, z.shape[1]), lambda i: (order(i), 0)),
             pl.BlockSpec((HALO, z.shape[1]), lambda i: (jnp.maximum(order(i) * halo_blocks - 1, 0), 0))]
            + [_full_spec(p) for p in params])


def _rw_pre_fwd(z, params, tm):
    t = z.shape[0]
    npar = len(params)

    def body(z_ref, halo_ref, *refs):
        carry = jnp.where(pl.program_id(0) == 0, 0.0, halo_ref[pl.ds(HALO - 1, 1), :])
        res = _f_rw_pre(z_ref[...], carry, *[r[...].astype(F32) for r in refs[:npar]])
        for r, v in zip(refs[npar:], res):
            r[...] = v

    return pl.pallas_call(
        body, name="rw_pre",
        out_shape=[jax.ShapeDtypeStruct((t, RWKV_WIDTH), F32)] * 7,
        grid=(t // tm,),
        in_specs=_rw_pre_specs(z, params, tm, lambda i: i),
        out_specs=[pl.BlockSpec((tm, RWKV_WIDTH), lambda i: (i, 0))] * 7,
        compiler_params=_params(("parallel",)),
    )(z, z, *params)


def _rw_pre_bwd(z, params, cots, tm):
    t = z.shape[0]
    nt = t // tm
    npar = len(params)
    order = lambda i: nt - 1 - i
    flat_cots = [a for group in cots for a in group]
    ncot = len(flat_cots)

    def body(z_ref, halo_ref, *refs):
        pin, cin = refs[:npar], list(refs[npar:npar + ncot])
        dz_ref = refs[npar + ncot]
        dpo = refs[npar + ncot + 1:npar + ncot + 1 + npar]
        dcarry_ref = refs[npar + ncot + 1 + npar]
        i = pl.program_id(0)

        @pl.when(i == 0)
        def _():
            dcarry_ref[...] = jnp.zeros_like(dcarry_ref)
            for r in dpo:
                r[...] = jnp.zeros_like(r)

        carry = jnp.where(i == nt - 1, 0.0, halo_ref[pl.ds(HALO - 1, 1), :])
        _, vjp = jax.vjp(_f_rw_pre, z_ref[...], carry, *[r[...].astype(F32) for r in pin])
        g = vjp(tuple(sum(cin.pop(0)[...] for _ in group) for group in cots))
        last_row = lax.broadcasted_iota(jnp.int32, z_ref.shape, 0) == tm - 1
        dz_ref[...] = g[0] + jnp.where(last_row, dcarry_ref[...], 0.0)
        dcarry_ref[...] = g[1]
        for r, v in zip(dpo, g[2:]):
            r[...] += v

    tok = lambda w: pl.BlockSpec((tm, w), lambda i: (order(i), 0))
    return pl.pallas_call(
        body, name="rw_pre_bwd",
        out_shape=[jax.ShapeDtypeStruct(z.shape, F32)] + [jax.ShapeDtypeStruct(p.shape, F32) for p in params],
        grid=(nt,),
        in_specs=_rw_pre_specs(z, params, tm, order) + [tok(RWKV_WIDTH)] * ncot,
        out_specs=[tok(z.shape[1])] + [_full_spec(p) for p in params],
        scratch_shapes=[pltpu.VMEM((1, z.shape[1]), F32)],
        compiler_params=_params(("arbitrary",)),
    )(z, z, *params, *flat_cots)


def _s5_scan(bu, lam, tm, plan):
    t, w = bu.shape
    h = w // 2
    nt = t // tm
    parts, plan_in_specs, plan_out_shape, plan_out_specs, plan_sems = _carry(plan, 2, 1)

    def body(*refs):
        bu_ref, lam_ref = refs[:2]
        xb_ref = refs[2 + len(plan.ins)]
        carry_ref = refs[3 + len(plan.ins) + len(plan.out_shape)]
        x_ref, refs = refs[-1], refs[:-1]

        @pl.when(pl.program_id(0) == 0)
        def _():
            carry_ref[...] = jnp.zeros_like(carry_ref)
            plan.start(*parts(refs))

        lr, li = lam_ref[:, :h], lam_ref[:, h:]

        def step(s, c):
            cr, ci = c
            row = pl.ds(s, 1)
            nr = lr * cr - li * ci + bu_ref[row, :h]
            ni = lr * ci + li * cr + bu_ref[row, h:]
            x_ref[row, :h] = nr
            x_ref[row, h:] = ni
            return nr, ni

        cr, ci = lax.fori_loop(0, tm, step, (carry_ref[:, :h], carry_ref[:, h:]), unroll=SCAN_UNROLL)
        carry_ref[:, :h] = cr
        carry_ref[:, h:] = ci
        xb_ref[...] = x_ref[...].astype(BF16)

        @pl.when(pl.program_id(0) == nt - 1)
        def _():
            plan.wait(*parts(refs))

    spec = pl.BlockSpec((tm, w), lambda i: (i, 0))
    res = pl.pallas_call(
        body, name="s5_scan",
        out_shape=[jax.ShapeDtypeStruct((t, w), BF16)] + plan_out_shape,
        grid=(nt,),
        in_specs=[spec, pl.BlockSpec((1, w), lambda i: (0, 0))] + plan_in_specs,
        out_specs=[spec] + plan_out_specs,
        scratch_shapes=[pltpu.VMEM((1, w), F32)] + plan_sems + [pltpu.VMEM((tm, w), F32)],
        compiler_params=_params(("arbitrary",)),
    )(bu, lam, *plan.ins)
    return res[0], res[1:]


def _s5_scan_bwd(dx, xb, lam, tm):
    t, w = dx.shape
    h = w // 2
    nt = t // tm
    halo = BF16_SUBLANES

    def body(dx_ref, xb_ref, halo_ref, lam_ref, dbu_out_ref, dlam_ref, carry_ref, dbu_ref):
        @pl.when(pl.program_id(0) == 0)
        def _():
            carry_ref[...] = jnp.zeros_like(carry_ref)
            dlam_ref[...] = jnp.zeros_like(dlam_ref)

        lr, li = lam_ref[:, :h], lam_ref[:, h:]

        def step(s, c):
            cr, ci = c
            row = pl.ds(tm - 1 - s, 1)
            nr = lr * cr + li * ci + dx_ref[row, :h]
            ni = lr * ci - li * cr + dx_ref[row, h:]
            dbu_ref[row, :h] = nr
            dbu_ref[row, h:] = ni
            return nr, ni

        cr, ci = lax.fori_loop(0, tm, step, (carry_ref[:, :h], carry_ref[:, h:]), unroll=SCAN_UNROLL)
        carry_ref[:, :h] = cr
        carry_ref[:, h:] = ci
        halo_rows = lax.broadcasted_iota(jnp.int32, (halo, w), 0)
        before = jnp.sum(jnp.where(halo_rows == halo - 1, halo_ref[...].astype(F32), 0.0), axis=0, keepdims=True)
        before = jnp.where(pl.program_id(0) == nt - 1, 0.0, before)
        first_row = lax.broadcasted_iota(jnp.int32, (tm, w), 0) == 0
        xp = jnp.where(first_row, before, pltpu.roll(xb_ref[...].astype(F32), 1, 0))
        gr, gi = dbu_ref[:, :h], dbu_ref[:, h:]
        pr, pi_ = xp[:, :h], xp[:, h:]
        dlam_ref[:, :h] += jnp.sum(gr * pr + gi * pi_, axis=0, keepdims=True)
        dlam_ref[:, h:] += jnp.sum(gi * pr - gr * pi_, axis=0, keepdims=True)
        dbu_out_ref[...] = dbu_ref[...].astype(BF16)

    spec = pl.BlockSpec((tm, w), lambda i: (nt - 1 - i, 0))
    halo_spec = pl.BlockSpec((halo, w), lambda i: (jnp.maximum((nt - 1 - i) * (tm // halo) - 1, 0), 0))
    row_spec = pl.BlockSpec((1, w), lambda i: (0, 0))
    return pl.pallas_call(
        body, name="s5_scan_bwd",
        out_shape=[jax.ShapeDtypeStruct((t, w), BF16), jax.ShapeDtypeStruct((1, w), F32)],
        grid=(nt,),
        in_specs=[spec, spec, halo_spec, row_spec],
        out_specs=[spec, row_spec],
        scratch_shapes=[pltpu.VMEM((1, w), F32), pltpu.VMEM((tm, w), F32)],
        compiler_params=_params(("arbitrary",)),
    )(dx, xb, xb, lam)


def _unit_lower_inverses_impl(ns):
    c = ns[0].shape[0]
    eye = (lax.broadcasted_iota(jnp.int32, (c, c), 0) == lax.broadcasted_iota(jnp.int32, (c, c), 1)).astype(F32)
    inv = [eye + n for n in ns]
    pw = [_dg(n, n, _NN, WKV_PASSES) for n in ns]
    for _ in range(int(math.log2(c)) - 2):
        both = [_dg(jnp.concatenate([i, q], axis=0), q, _NN, WKV_PASSES) for i, q in zip(inv, pw)]
        inv = [i + q[:c] for i, q in zip(inv, both)]
        pw = [q[c:] for q in both]
    return tuple(i + _dg(i, q, _NN, WKV_PASSES) for i, q in zip(inv, pw))


@jax.custom_vjp
def _unit_lower_inverses(ns):
    return _unit_lower_inverses_impl(ns)


def _unit_lower_inverses_fwd(ns):
    inv = _unit_lower_inverses_impl(ns)
    return inv, inv


def _unit_lower_inverses_bwd(inv, g):
    left = [_dg(i, gi, _TN, WKV_PASSES) for i, gi in zip(inv, g)]
    return (tuple(_dg(q, i, _NT, WKV_PASSES) for q, i in zip(left, inv)),)


_unit_lower_inverses.defvjp(_unit_lower_inverses_fwd, _unit_lower_inverses_bwd)


def _wkv_chunks(s0, r, lw, k, v, a, b):
    c = r[0].shape[0]
    row = lax.broadcasted_iota(jnp.int32, (c, c), 0)
    col = lax.broadcasted_iota(jnp.int32, (c, c), 1)
    incl, strict = col <= row, col < row
    tri = incl.astype(F32)
    each = lambda f, *xs: [f(*t) for t in zip(*xs)]
    stack = lambda p, q: jnp.concatenate([p, q], axis=0)
    dot = lambda p, q, dims=_NN: _dot(p, q, dims, WKV_PASSES)
    lc = each(lambda l: _dot(tri, l, _NN, 2), lw)
    e_neg = each(lambda l: jnp.exp(-l), lc)
    ar = each(lambda x, z, l, w: stack(x * jnp.exp(l - w), z * jnp.exp(l)), a, r, lc, lw)
    bk = each(lambda x, z, e: stack(x * e, z * e), b, k, e_neg)
    m = each(lambda p, q: dot(p, q, _NT), ar, bk)
    mab = each(lambda q: jnp.where(strict, q[:c, :c], 0.0), m)
    mak_mrk = each(lambda q: stack(jnp.where(strict, q[:c, c:], 0.0), jnp.where(incl, q[c:, c:], 0.0)), m)
    mrb = each(lambda q: jnp.where(incl, q[c:, :c], 0.0), m)
    xy = each(lambda p, s, q, z: dot(p, s, _NT) + dot(q, z), ar, s0, mak_mrk, v)
    inv = _unit_lower_inverses(tuple(mab))
    u = each(lambda i, q: dot(i, q[:c]), inv, xy)
    y = each(lambda q, z, p: q[c:] + dot(z, p), xy, mrb, u)
    e_tot = each(lambda l: jnp.exp(jnp.sum(l, axis=0, keepdims=True)), lw)
    s1 = each(lambda s, p, z, q, e: (s + dot(stack(p, z), q, _TN)) * e, s0, u, v, bk, e_tot)
    return y, s1


def _carry(plan, n_args, n_outs):
    n_in, n_out = len(plan.ins), len(plan.out_shape)

    def parts(refs):
        base = n_args + n_in + n_outs
        return refs[n_args:n_args + n_in], refs[base:base + n_out], refs[base + n_out + 1:]

    return parts, [HBM_SPEC] * n_in, list(plan.out_shape), [HBM_SPEC] * n_out, list(plan.sems)


def _head_cols(ref):
    return tuple(ref[:, h * HEAD:(h + 1) * HEAD] for h in range(HEADS))


def _wkv_fwd(seqs, plan):
    t, w = seqs[0].shape
    c, n = WKV_CHUNK, HEAD
    nc = t // c
    parts, plan_in_specs, plan_out_shape, plan_out_specs, plan_sems = _carry(plan, 6, 2)

    def body(*refs):
        ins, (y_ref, ck_ref) = refs[:6], refs[6 + len(plan.ins):8 + len(plan.ins)]
        s_ref = refs[8 + len(plan.ins) + len(plan.out_shape)]

        @pl.when(pl.program_id(0) == 0)
        def _():
            s_ref[...] = jnp.zeros_like(s_ref)
            plan.start(*parts(refs))

        s0 = tuple(s_ref[h] for h in range(HEADS))
        ys, s1 = _wkv_chunks(s0, *[_head_cols(r) for r in ins])
        for h in range(HEADS):
            ck_ref[0, h] = s0[h]
            y_ref[:, h * n:(h + 1) * n] = ys[h]
            s_ref[h] = s1[h]

        @pl.when(pl.program_id(0) == nc - 1)
        def _():
            plan.wait(*parts(refs))

    spec = pl.BlockSpec((c, w), lambda i: (i, 0))
    res = pl.pallas_call(
        body, name="wkv_fwd",
        out_shape=[jax.ShapeDtypeStruct((t, w), F32), jax.ShapeDtypeStruct((nc, HEADS, n, n), F32)] + plan_out_shape,
        grid=(nc,),
        in_specs=[spec] * 6 + plan_in_specs,
        out_specs=[spec, pl.BlockSpec((1, HEADS, n, n), lambda i: (i, 0, 0, 0))] + plan_out_specs,
        scratch_shapes=[pltpu.VMEM((HEADS, n, n), F32)] + plan_sems,
        compiler_params=_params(("arbitrary",)),
    )(*seqs, *plan.ins)
    return res[0], res[1], res[2:]


def _wkv_bwd(seqs, ck, dy, plan):
    t, w = seqs[0].shape
    c, n = WKV_CHUNK, HEAD
    nc = t // c
    parts, plan_in_specs, plan_out_shape, plan_out_specs, plan_sems = _carry(plan, 8, 6)

    def body(*refs):
        ins, ck_ref, dy_ref = refs[:6], refs[6], refs[7]
        outs = refs[8 + len(plan.ins):14 + len(plan.ins)]
        ds_ref = refs[14 + len(plan.ins) + len(plan.out_shape)]

        @pl.when(pl.program_id(0) == 0)
        def _():
            ds_ref[...] = jnp.zeros_like(ds_ref)
            plan.start(*parts(refs))

        s0 = tuple(ck_ref[0, h] for h in range(HEADS))
        _, vjp = jax.vjp(_wkv_chunks, s0, *[_head_cols(r) for r in ins])
        g = vjp((list(_head_cols(dy_ref)), [ds_ref[h] for h in range(HEADS)]))
        for h in range(HEADS):
            ds_ref[h] = g[0][h]
            for o, d in zip(outs, g[1:]):
                o[:, h * n:(h + 1) * n] = d[h]

        @pl.when(pl.program_id(0) == nc - 1)
        def _():
            plan.wait(*parts(refs))

    spec = pl.BlockSpec((c, w), lambda i: (nc - 1 - i, 0))
    res = pl.pallas_call(
        body, name="wkv_bwd",
        out_shape=[jax.ShapeDtypeStruct((t, w), F32)] * 6 + plan_out_shape,
        grid=(nc,),
        in_specs=[spec] * 6 + [pl.BlockSpec((1, HEADS, n, n), lambda i: (nc - 1 - i, 0, 0, 0)), spec] + plan_in_specs,
        out_specs=[spec] * 6 + plan_out_specs,
        scratch_shapes=[pltpu.VMEM((HEADS, n, n), F32)] + plan_sems,
        compiler_params=_params(("arbitrary",)),
    )(*seqs, ck, dy, *plan.ins)
    return res[:6], res[6:]


def _coords():
    return lax.axis_index("x"), lax.axis_index("y"), lax.axis_index("c")


def _flip(v, f):
    return 1 - v if f else v


_CHIP_FLIPS = [(1, 0), (0, 1), (1, 1)]
_DEV_FLIPS = [(fx, fy, fc) for fx in (0, 1) for fy in (0, 1) for fc in (0, 1) if (fx, fy, fc) != (0, 0, 0)]
HBM_SPEC = pl.BlockSpec(memory_space=pl.ANY)


def _chip_peer(k, x, y):
    fx, fy = _CHIP_FLIPS[k]
    return _flip(x, fx), _flip(y, fy)


def _dev_peer(k, x, y, c):
    fx, fy, fc = _DEV_FLIPS[k]
    return _flip(x, fx), _flip(y, fy), _flip(c, fc)


def _rows_of_core(ref, core):
    h = ref.shape[-2] // 2
    rows = pl.ds(pl.multiple_of(core * h, 8), h)
    return ref.at[rows, :] if len(ref.shape) == 2 else ref.at[:, rows, :]


class _Plan(NamedTuple):
    ins: Sequence[Any]
    out_shape: Sequence[Any]
    sems: Sequence[Any]
    start: Callable
    wait: Callable


_NO_PLAN = _Plan([], [], [], lambda *_: None, lambda *_: None)


def _run_plan(name, plan):
    n_in, n_out = len(plan.ins), len(plan.out_shape)

    def body(*refs):
        parts = refs[:n_in], refs[n_in:n_in + n_out], refs[n_in + n_out:]
        plan.start(*parts)
        plan.wait(*parts)

    return pl.pallas_call(
        body, name=name, out_shape=list(plan.out_shape),
        in_specs=[HBM_SPEC] * n_in, out_specs=[HBM_SPEC] * n_out, scratch_shapes=list(plan.sems),
    )(*plan.ins)


def _gather_plan(shards):
    n = len(shards)

    def copies(srcs, outs, sems):
        send_sems, recv_sems, local_sems = sems
        x, y, c = _coords()
        me = 2 * x + y

        def remote(i, k, arriving):
            px, py = _chip_peer(k, x, y)
            return pltpu.make_async_remote_copy(
                src_ref=srcs[i], dst_ref=outs[i].at[2 * px + py if arriving else me],
                send_sem=send_sems.at[i, k], recv_sem=recv_sems.at[i, k],
                device_id=(px, py, c), device_id_type=MESH)

        own = [pltpu.make_async_copy(srcs[i], outs[i].at[me], local_sems.at[i]) for i in range(n)]
        pairs = [(i, k) for k in range(3) for i in range(n)]
        return own, [remote(i, k, False) for i, k in pairs], [remote(i, k, True) for i, k in pairs]

    return _Plan(
        ins=shards, out_shape=[jax.ShapeDtypeStruct((4,) + s.shape, s.dtype) for s in shards],
        sems=[pltpu.SemaphoreType.DMA((n, 3)), pltpu.SemaphoreType.DMA((n, 3)), pltpu.SemaphoreType.DMA((n,))],
        start=functools.partial(_start_copies, copies), wait=functools.partial(_wait_copies, copies))


def _start_copies(copies, ins, outs, sems):
    own, sends, _ = copies(ins, outs, sems)
    for cp in own + sends:
        cp.start()


def _wait_copies(copies, ins, outs, sems):
    own, sends, arrivals = copies(ins, outs, sems)
    for cp in arrivals:
        cp.wait_recv()
    for cp in sends:
        cp.wait_send()
    for cp in own:
        cp.wait()


def _exchange_plan(gs, small=None):
    n = len(gs)
    arrays = list(gs) + ([] if small is None else [small])

    def copies(srcs, outs, sems):
        send_sems, recv_sems, local_sems = sems
        x, y, c = _coords()
        me = 4 * x + 2 * y + c

        def piece(i, px, py, pc):
            if i == n:
                return srcs[i].at[4 * px + 2 * py + pc]
            return _rows_of_core(srcs[i].at[2 * px + py], pc)

        def remote(i, k, arriving):
            px, py, pc = _dev_peer(k, x, y, c)
            return pltpu.make_async_remote_copy(
                src_ref=piece(i, px, py, pc), dst_ref=outs[i].at[4 * px + 2 * py + pc if arriving else me],
                send_sem=send_sems.at[i, k], recv_sem=recv_sems.at[i, k],
                device_id=(px, py, pc), device_id_type=MESH)

        own = [pltpu.make_async_copy(piece(i, x, y, c), outs[i].at[me], local_sems.at[i]) for i in range(len(arrays))]
        pairs = [(i, k) for k in range(7) for i in range(len(arrays))]
        return own, [remote(i, k, False) for i, k in pairs], [remote(i, k, True) for i, k in pairs]

    out_shape = [jax.ShapeDtypeStruct((8, g.shape[1] // 2, g.shape[2]), g.dtype) for g in gs]
    if small is not None:
        out_shape.append(jax.ShapeDtypeStruct(small.shape, small.dtype))
    m = len(arrays)
    return _Plan(
        ins=arrays, out_shape=out_shape,
        sems=[pltpu.SemaphoreType.DMA((m, 7)), pltpu.SemaphoreType.DMA((m, 7)), pltpu.SemaphoreType.DMA((m,))],
        start=functools.partial(_start_copies, copies), wait=functools.partial(_wait_copies, copies))


def _share_cores(halves, small):
    n = len(halves)

    def body(*refs):
        srcs, small_src, outs, small_out = refs[:n], refs[n], refs[n + 1:2 * n + 1], refs[2 * n + 1]
        mine, theirs = refs[2 * n + 2:3 * n + 2], refs[3 * n + 2:4 * n + 2]
        send_sems, recv_sems, ssend, srecv, local_sems = refs[4 * n + 2:]
        x, y, c = _coords()
        me = 4 * x + 2 * y + c

        def big(i):
            return pltpu.make_async_remote_copy(
                src_ref=mine[i], dst_ref=theirs[i], send_sem=send_sems.at[i], recv_sem=recv_sems.at[i],
                device_id=(x, y, 1 - c), device_id_type=MESH)

        def tiny(k, arriving):
            px, py, pc = _dev_peer(k, x, y, c)
            return pltpu.make_async_remote_copy(
                src_ref=small_src, dst_ref=small_out.at[4 * px + 2 * py + pc if arriving else me],
                send_sem=ssend.at[k], recv_sem=srecv.at[k], device_id=(px, py, pc), device_id_type=MESH)

        small_sends = [tiny(k, False) for k in range(7)]
        own_small = pltpu.make_async_copy(small_src, small_out.at[me], local_sems.at[2 * n])
        stage = [pltpu.make_async_copy(srcs[i], mine[i], local_sems.at[i]) for i in range(n)]
        for cp in small_sends + [own_small] + stage:
            cp.start()
        sends = []
        for i in range(n):
            stage[i].wait()
            sends.append(big(i))
            sends[-1].start()
        store = [pltpu.make_async_copy(mine[i], outs[i].at[c], local_sems.at[i]) for i in range(n)]
        for cp in store:
            cp.start()
        for i in range(n):
            big(i).wait_recv()
            store.append(pltpu.make_async_copy(theirs[i], outs[i].at[1 - c], local_sems.at[n + i]))
            store[-1].start()
        for k in range(7):
            tiny(k, True).wait_recv()
        for cp in sends + small_sends:
            cp.wait_send()
        for cp in store + [own_small]:
            cp.wait()

    staged = [pltpu.VMEM(s.shape, s.dtype) for s in halves]
    res = pl.pallas_call(
        body, name="share_cores",
        out_shape=[jax.ShapeDtypeStruct((2,) + s.shape, s.dtype) for s in halves]
        + [jax.ShapeDtypeStruct((8,) + small.shape, small.dtype)],
        in_specs=[HBM_SPEC] * (n + 1), out_specs=[HBM_SPEC] * (n + 1),
        scratch_shapes=staged + staged + [
            pltpu.SemaphoreType.DMA((n,)), pltpu.SemaphoreType.DMA((n,)),
            pltpu.SemaphoreType.DMA((7,)), pltpu.SemaphoreType.DMA((7,)),
            pltpu.SemaphoreType.DMA((2 * n + 1,))],
        compiler_params=pltpu.CompilerParams(vmem_limit_bytes=VMEM_LIMIT_BYTES),
    )(*halves, small)
    return res[:n], res[n]


BF16_SUBLANES = 16


def _row_tile(n, target, step=BF16_SUBLANES):
    return max([d for d in range(step, min(n, target) + 1, step) if n % d == 0] or [n])


def _ew(name, fn, ins, outs, block_bytes=1 << 20):
    rows, cols = ins[0].shape[-2:]
    lead = max(math.prod(a.shape[:-2]) for a in ins)
    tr = _row_tile(rows, max(8, block_bytes // (4 * cols * lead)))
    n = len(ins)

    def spec(shape):
        if len(shape) == 2:
            return pl.BlockSpec((tr, cols), lambda i: (i, 0))
        return pl.BlockSpec((shape[0], tr, cols), lambda i: (0, i, 0))

    def body(*refs):
        res = fn(*[r[...] for r in refs[:n]])
        for r, v in zip(refs[n:], res):
            r[...] = v

    return pl.pallas_call(
        body, name=name,
        out_shape=[jax.ShapeDtypeStruct(s, F32) for s in outs],
        grid=(rows // tr,),
        in_specs=[spec(a.shape) for a in ins],
        out_specs=[spec(s) for s in outs],
        compiler_params=_params(("parallel",)),
    )(*ins)


def _sum_slots(a):
    total = a[0].astype(F32)
    for s in range(1, a.shape[0]):
        total = total + a[s].astype(F32)
    return (total,)


def _adamw(g, w, m, v):
    bc1 = 1.0 - ADAM_B1 ** ADAM_STEP
    bc2 = 1.0 - ADAM_B2 ** ADAM_STEP
    m_new = ADAM_B1 * m + (1.0 - ADAM_B1) * g
    v_new = ADAM_B2 * v + (1.0 - ADAM_B2) * jnp.square(g)
    delta = -ADAM_LR * ((m_new / bc1) / (jnp.sqrt(v_new / bc2) + ADAM_EPS) + ADAM_WD * w)
    return delta, m_new, v_new


def _mat(a):
    return a.reshape(a.shape[-2:])


def _to_shard_major(full, axis):
    rows, cols = full.shape
    if axis == 0:
        return full.reshape(4, rows // 4, cols)
    return full.reshape(rows, 4, cols // 4).transpose(1, 0, 2)


def _from_shard_major(a, axis):
    _, r, cs = a.shape
    if axis == 0:
        return a.reshape(4 * r, cs)
    return a.transpose(1, 0, 2).reshape(r, 4 * cs)


def _pack_small(arrays, tail=None):
    flat = [arrays[n].reshape(-1) for n in SMALL_NAMES] + ([] if tail is None else [tail.reshape(1)])
    used = sum(a.shape[0] for a in flat)
    flat.append(jnp.zeros((8 * SMALL_ROWS * PACK_COLS - used,), F32))
    return jnp.concatenate(flat).reshape(8, SMALL_ROWS, PACK_COLS)


def _unpack_small(packed, shapes):
    flat = packed.reshape(-1)
    out, off = {}, 0
    for n in SMALL_NAMES:
        size = math.prod(shapes[n])
        out[n] = flat[off:off + size].reshape(shapes[n])
        off += size
    return out


def _row(a):
    return a.reshape(1, -1)


def _local_step(x, p, target, wf, ws, late_shards):
    wf = dict(wf)
    t = x.shape[0]
    tm = min(256, t)
    g = {}

    lam_re, lam_im = ws['s5_lam_re'].reshape(S5_GROUPS, S5_STATE), ws['s5_lam_im'].reshape(S5_GROUPS, S5_STATE)
    log_step = ws['s5_log_step'].reshape(S5_GROUPS, 1)
    gp = (S5_GROUPS, S5_STATE)
    lam_ins = (lam_re, lam_im, log_step)
    lbr, lbi, cfr, cfi = _small_fwd("s5_lam", _f_s5_lam, lam_ins, [(gp, F32)] * 4)
    lam_row = jnp.concatenate([_row(lbr), _row(lbi)], axis=1)
    to_t = lambda a, perm: a.reshape((S5_GROUPS,) + a.shape[-2:]).transpose(perm).reshape(S5_GROUP, S5_LANES)
    build_ins = (_row(cfr), _row(cfi), to_t(ws['s5_b_re'], (2, 0, 1)), to_t(ws['s5_b_im'], (2, 0, 1)),
                 to_t(ws['s5_c_re'], (1, 0, 2)), to_t(ws['s5_c_im'], (1, 0, 2)))
    block_shape = (S5_WIDTH, 2 * S5_TILE_LANES)
    b_blk, c_blk = _small_fwd("s5_build", _f_s5_build, build_ins, [(block_shape, F32)] * 2)

    norm_mix, norm_ffn, norm_ple = _row(ws['norm_mix']), _row(ws['norm_ffn']), _row(ws['norm_ple'])
    final_norm = _row(ws['final_norm'])
    (xn,) = _tok_fwd("norm_in", _f_norm_in, [x], [norm_mix], [(x.shape[1], BF16)], tm)
    u = _mm("proj_s5", xn, wf['w_in'][:, :S5_WIDTH], 'nn')
    z = _mm("proj_rw", xn, wf['w_in'][:, S5_WIDTH:], 'nn')

    bu = _s5_expand("s5_bu", u, b_blk)
    def late_plan(carrier):
        return _gather_plan([late_shards[n] for n in LATE_GATHER[carrier]])

    def arrived(carrier, got):
        wf.update({n: _from_shard_major(a, SHARDED[n]) for n, a in zip(LATE_GATHER[carrier], got)})

    xs, got = _s5_scan(bu, lam_row, tm, late_plan('s5_scan'))
    arrived('s5_scan', got)
    ypre = _s5_contract("s5_y", xs, c_blk)
    s5_par = [_row(ws['s5_d']), wf['s5_glu_w'], _row(ws['s5_glu_b'])]
    (s5_out,) = _tok_fwd("s5_post", _f_s5_post, [ypre, u], s5_par, [(S5_WIDTH, BF16)], tm)

    pre_par = [_row(ws['rw_shift_mu']), _row(ws['rw_w0']), wf['rw_w2'], _row(ws['rw_a0']), wf['rw_a2'],
               wf['rw_g2'], _row(ws['rw_k_k']), _row(ws['rw_k_a'])]
    r, lw, kp, v, an, bn, gate = _rw_pre_fwd(z, pre_par, tm)
    seqs = [r, lw, kp, v, an, bn]
    y_wkv, ck, got = _wkv_fwd(seqs, late_plan('wkv_fwd'))
    arrived('wkv_fwd', got)
    post_par = [_row(ws['rw_ln_w']), _row(ws['rw_ln_b']), _row(ws['rw_r_k'])]
    post_toks = [y_wkv, r, kp, v, gate]
    (rw_out,) = _tok_fwd("rw_post", _f_rw_post, post_toks, post_par, [(RWKV_WIDTH, BF16)], tm)

    mixcat = jnp.concatenate([s5_out, rw_out], axis=1)
    mixed = _mm("mix_out", mixcat, wf['w_out'], 'nn')
    h1, hn = _tok_fwd("mix_res", _f_mix_res, [x, mixed], [norm_ffn], [(x.shape[1], F32), (x.shape[1], BF16)], tm)
    w13 = jnp.concatenate([wf['ffn_w1'], wf['ffn_w3']], axis=1)
    a13, got = _mm("ffn_up", hn, w13, 'nn', out_dtype=BF16, tn=FFN_TILE, plan=late_plan('ffn_up'))
    arrived('ffn_up', got)
    (f,) = _tok_fwd("ffn_act", _f_ffn_act, [a13], [], [(FFN_HIDDEN, BF16)], tm)
    ffo = _mm("ffn_down", f, wf['ffn_w2'], 'nn')
    h2, hp = _tok_fwd("ffn_res", _f_ffn_res, [h1, ffo], [norm_ple], [(x.shape[1], F32), (x.shape[1], BF16)], tm)
    gpre = _mm("ple_gate", hp, wf['ple_gate_w'], 'nn')
    pu = _mm("ple_up", p, wf['ple_up_w'], 'nn')

    dh2, dgpre, dpu, g['final_norm'], loss = _tok_bwd(
        "loss", _f_loss, [h2, gpre, pu, target], [final_norm], [None],
        [F32, BF16, BF16, None], [True], tm, acc_out=0)
    g['ple_gate_w'] = _mm("d_ple_gate_w", hp, dgpre, 'tn', out_dtype=WIRE)
    g['ple_up_w'] = _mm("d_ple_up_w", p, dpu, 'tn', out_dtype=WIRE)
    dhp = _mm("d_hp", dgpre, wf['ple_gate_w'], 'nt')
    dh1, dffo, g['norm_ple'] = _tok_bwd("ffn_res_bwd", _f_ffn_res, [h1, ffo], [norm_ple], [dh2, dhp],
                                        [F32, BF16], [True], tm)
    g['ffn_w2'] = _mm("d_ffn_w2", f, dffo, 'tn', out_dtype=WIRE, tm=FFN_TILE)
    df = _mm("d_f", dffo, wf['ffn_w2'], 'nt', out_dtype=BF16, tn=FFN_TILE)
    (da13,) = _tok_bwd("ffn_act_bwd", _f_ffn_act, [a13], [], [df], [BF16], [], tm)
    dw13 = _mm("d_ffn_w13", hn, da13, 'tn', out_dtype=WIRE, tn=FFN_TILE)
    dw13 = dw13.reshape(dw13.shape[0], 8, FFN_HIDDEN // 4).transpose(1, 0, 2)
    shard_major = {'ffn_w1': dw13[:4], 'ffn_w3': dw13[4:]}
    dhn = _mm("d_hn", da13, w13, 'nt')
    dx_a, dmixed, g['norm_ffn'] = _tok_bwd("mix_res_bwd", _f_mix_res, [x, mixed], [norm_ffn], [dh1, dhn],
                                           [F32, BF16], [True], tm)
    g['w_out'] = _mm("d_w_out", mixcat, dmixed, 'tn', out_dtype=WIRE)
    dmixcat = _mm("d_mixcat", dmixed, wf['w_out'], 'nt')
    ds5_out, drw_out = dmixcat[:, :S5_WIDTH], dmixcat[:, S5_WIDTH:]

    dy_wkv, dr_b, dkp_b, dv_b, dgate, g['rw_ln_w'], g['rw_ln_b'], g['rw_r_k'] = _tok_bwd(
        "rw_post_bwd", _f_rw_post, post_toks, post_par, [drw_out], [F32] * 5, [True] * 3, tm)
    late_exchange = _exchange_plan([shard_major[n] if n in shard_major else
                                    _to_shard_major(g[n], SHARDED[n]).astype(WIRE) for n in LATE_NAMES])
    dseqs, late_pieces = _wkv_bwd(seqs, ck, dy_wkv, late_exchange)
    pre_cots = [(dseqs[0], dr_b), (dseqs[1],), (dseqs[2], dkp_b), (dseqs[3], dv_b), (dseqs[4],), (dseqs[5],),
                (dgate,)]
    dz, *dpre = _rw_pre_bwd(z, pre_par, pre_cots, tm)
    for n, d in zip(['rw_shift_mu', 'rw_w0', 'rw_w2', 'rw_a0', 'rw_a2', 'rw_g2', 'rw_k_k', 'rw_k_a'], dpre):
        g[n] = d

    dypre, du_a, g['s5_d'], g['s5_glu_w'], g['s5_glu_b'] = _tok_bwd(
        "s5_post_bwd", _f_s5_post, [ypre, u], s5_par, [ds5_out], [F32, F32], [True] * 3, tm)
    dxs = _s5_expand("d_s5_x", dypre, c_blk)
    dc_blk = _s5_block_grad("d_s5_c", dypre, xs)
    dbu, dlam_row = _s5_scan_bwd(dxs, xs, lam_row, tm)
    du_b = _s5_contract("d_s5_u", dbu, b_blk)
    db_blk = _s5_block_grad("d_s5_b", u, dbu)
    dbuild = _small_bwd("s5_build_bwd", _f_s5_build, build_ins, (db_blk, dc_blk))
    lam_cots = (dlam_row[:, :S5_LANES].reshape(gp), dlam_row[:, S5_LANES:].reshape(gp),
                dbuild[0].reshape(gp), dbuild[1].reshape(gp))
    g['s5_lam_re'], g['s5_lam_im'], g['s5_log_step'] = _small_bwd("s5_lam_bwd", _f_s5_lam, lam_ins, lam_cots)
    from_t = lambda a, perm: a.reshape(S5_GROUP, S5_GROUPS, S5_STATE).transpose(perm)
    g['s5_b_re'], g['s5_b_im'] = from_t(dbuild[2], (1, 2, 0)), from_t(dbuild[3], (1, 2, 0))
    g['s5_c_re'], g['s5_c_im'] = from_t(dbuild[4], (1, 0, 2)), from_t(dbuild[5], (1, 0, 2))

    dproj = jnp.concatenate([(du_a + du_b).astype(BF16), dz.astype(BF16)], axis=1)
    g['w_in'] = _mm("d_w_in", xn, dproj, 'tn', out_dtype=WIRE)
    dxn = _mm("d_xn", dproj, wf['w_in'], 'nt')
    grad_x, g['norm_mix'] = _tok_bwd("norm_in_bwd", _f_norm_in, [x], [norm_mix], [dxn], [F32], [True], tm,
                                     add_to=(0, dx_a))
    return loss[0, 0], grad_x, g, late_pieces


def _step(x, p, target, w, m, v):
    shards = {n: _mat(w[n]).astype(BF16) for n in SHARDED_NAMES}
    early = _run_plan("gather_early", _gather_plan([shards[n] for n in EARLY_NAMES]))
    wf = {n: _from_shard_major(a, SHARDED[n]) for n, a in zip(EARLY_NAMES, early)}
    ws = {n: w[n] for n in SMALL_NAMES}

    loss, grad_x, g, late_pieces = _local_step(x[0], p[0, 0], target[0], wf, ws, shards)

    early_plan = _exchange_plan([_to_shard_major(g[n], SHARDED[n]).astype(WIRE) for n in EARLY_NAMES],
                                _pack_small({n: g[n] for n in SMALL_NAMES}, tail=loss))
    *early_pieces, by_dev = _run_plan("exchange_early", early_plan)
    pieces = dict(zip(LATE_NAMES + EARLY_NAMES, list(late_pieces) + early_pieces))
    halves = [_ew("add_devices_" + n, _sum_slots, [pieces[n]], [pieces[n].shape[1:]])[0] for n in SHARDED_NAMES]
    (small_piece,) = _ew("add_devices_small", _sum_slots, [by_dev], [by_dev.shape[1:]])
    both, small_g = _share_cores(halves, small_piece)

    kinds = [{}, {}, {}, {}]
    for n, gn in zip(SHARDED_NAMES, both):
        shard = _mat(w[n]).shape
        res = _ew("adamw_" + n, _adamw, [gn.reshape(shard), _mat(w[n]), _mat(m[n]), _mat(v[n])], [shard] * 3)
        for kind, a in zip(kinds, [gn] + list(res)):
            kind[n] = a.reshape(w[n].shape)
    flat = (8 * SMALL_ROWS, PACK_COLS)
    packed = [_pack_small({n: d[n] for n in SMALL_NAMES}).reshape(flat) for d in (w, m, v)]
    small_res = _ew("adamw_small", _adamw, [small_g.reshape(flat)] + packed, [flat] * 3)
    small_shapes = {n: w[n].shape for n in SMALL_NAMES}
    for kind, a in zip(kinds, [small_g] + list(small_res)):
        kind.update(_unpack_small(a, small_shapes))
    total = small_g.reshape(-1)[sum(math.prod(s) for s in small_shapes.values())]
    return (total, grad_x[None], *[kind[n] for kind in kinds for n in WEIGHT_NAMES])


def kernel(x, p, norm_mix, w_in, s5_lam_re, s5_lam_im, s5_log_step, s5_b_re, s5_b_im, s5_c_re, s5_c_im, s5_d, s5_glu_w, s5_glu_b, rw_shift_mu, rw_w0, rw_w2, rw_a0, rw_a2, rw_g2, rw_k_k, rw_k_a, rw_r_k, rw_ln_w, rw_ln_b, w_out, norm_ffn, ffn_w1, ffn_w3, ffn_w2, norm_ple, ple_gate_w, ple_up_w, final_norm, loss_target, m_norm_mix, m_w_in, m_s5_lam_re, m_s5_lam_im, m_s5_log_step, m_s5_b_re, m_s5_b_im, m_s5_c_re, m_s5_c_im, m_s5_d, m_s5_glu_w, m_s5_glu_b, m_rw_shift_mu, m_rw_w0, m_rw_w2, m_rw_a0, m_rw_a2, m_rw_g2, m_rw_k_k, m_rw_k_a, m_rw_r_k, m_rw_ln_w, m_rw_ln_b, m_w_out, m_norm_ffn, m_ffn_w1, m_ffn_w3, m_ffn_w2, m_norm_ple, m_ple_gate_w, m_ple_up_w, m_final_norm, v_norm_mix, v_w_in, v_s5_lam_re, v_s5_lam_im, v_s5_log_step, v_s5_b_re, v_s5_b_im, v_s5_c_re, v_s5_c_im, v_s5_d, v_s5_glu_w, v_s5_glu_b, v_rw_shift_mu, v_rw_w0, v_rw_w2, v_rw_a0, v_rw_a2, v_rw_g2, v_rw_k_k, v_rw_k_a, v_rw_r_k, v_rw_ln_w, v_rw_ln_b, v_w_out, v_norm_ffn, v_ffn_w1, v_ffn_w3, v_ffn_w2, v_norm_ple, v_ple_gate_w, v_ple_up_w, v_final_norm):
    args = dict(locals())
    w = {n: args[n] for n in WEIGHT_NAMES}
    m = {n: args["m_" + n] for n in WEIGHT_NAMES}
    v = {n: args["v_" + n] for n in WEIGHT_NAMES}
    return _step(x, p, loss_target, w, m, v)
```

```python
import functools
import math
from typing import Any, Callable, NamedTuple, Sequence

import jax
import jax.numpy as jnp
from jax import lax
from jax.experimental import pallas as pl
from jax.experimental.pallas import tpu as pltpu

F32 = jnp.float32
BF16 = jnp.bfloat16
MESH = pl.DeviceIdType.MESH

S5_WIDTH = 512
RWKV_WIDTH = 512
S5_GROUP = 16
S5_GROUPS = 32
S5_STATE = 64
S5_LANES = S5_GROUPS * S5_STATE
S5_TILE_GROUPS = 8
S5_TILES = S5_GROUPS // S5_TILE_GROUPS
S5_TILE_CH = S5_TILE_GROUPS * S5_GROUP
S5_TILE_LANES = S5_TILE_GROUPS * S5_STATE
HEAD = 64
HEADS = 8
DECAY_LORA = 64
AAA_LORA = 64
GATE_LORA = 128
FFN_HIDDEN = 2816
FFN_TILE = FFN_HIDDEN // 2
RMS_EPS = 1e-6
GN_EPS = 64e-5
L2_EPS = 1e-12
ADAM_LR = 0.001
ADAM_B1 = 0.9
ADAM_B2 = 0.999
ADAM_EPS = 1e-08
ADAM_WD = 0.01
ADAM_STEP = 10

WKV_CHUNK = 64
SCAN_UNROLL = 4
WIRE = jnp.bfloat16
WKV_PASSES = 1
VMEM_LIMIT_BYTES = 48 * 1024 * 1024
LANE = 128
PACK_COLS = 1024
SMALL_ROWS = 24

WEIGHT_NAMES = ['norm_mix', 'w_in', 's5_lam_re', 's5_lam_im', 's5_log_step', 's5_b_re', 's5_b_im', 's5_c_re',
                's5_c_im', 's5_d', 's5_glu_w', 's5_glu_b', 'rw_shift_mu', 'rw_w0', 'rw_w2', 'rw_a0', 'rw_a2',
                'rw_g2', 'rw_k_k', 'rw_k_a', 'rw_r_k', 'rw_ln_w', 'rw_ln_b', 'w_out', 'norm_ffn', 'ffn_w1',
                'ffn_w3', 'ffn_w2', 'norm_ple', 'ple_gate_w', 'ple_up_w', 'final_norm']
SHARDED = {'w_in': 1, 's5_glu_w': 0, 'rw_w2': 1, 'rw_a2': 1, 'rw_g2': 1, 'w_out': 0, 'ffn_w1': 1, 'ffn_w3': 1,
           'ffn_w2': 0, 'ple_gate_w': 0, 'ple_up_w': 1}
SHARDED_NAMES = [n for n in WEIGHT_NAMES if n in SHARDED]
LATE_NAMES = ['w_out', 'ffn_w1', 'ffn_w3', 'ffn_w2', 'ple_gate_w', 'ple_up_w']
EARLY_NAMES = [n for n in SHARDED_NAMES if n not in LATE_NAMES]
LATE_GATHER = {'s5_scan': ['w_out', 'ple_gate_w', 'ple_up_w'], 'wkv_fwd': ['ffn_w1', 'ffn_w3'], 'ffn_up': ['ffn_w2']}
SMALL_NAMES = [n for n in WEIGHT_NAMES if n not in SHARDED]


def _params(sem=None):
    return pltpu.CompilerParams(dimension_semantics=sem, vmem_limit_bytes=VMEM_LIMIT_BYTES)


def _tile(n, target):
    best = None
    for d in range(LANE, min(n, target) + 1, LANE):
        if n % d == 0:
            best = d
    return n if best is None else best


_NN = (((1,), (0,)), ((), ()))
_NT = (((1,), (1,)), ((), ()))
_TN = (((0,), (0,)), ((), ()))


def _split(a):
    a = a.astype(F32)
    hi = a.astype(BF16)
    return hi, (a - hi.astype(F32)).astype(BF16)


def _dg(a, b, dims, passes):
    dg = lambda p, q: lax.dot_general(p, q, dims, preferred_element_type=F32)
    if passes == 1:
        return dg(a.astype(BF16), b.astype(BF16))
    bh, bl = _split(b)
    if passes == 2:
        return dg(a.astype(BF16), bh) + dg(a.astype(BF16), bl)
    ah, al = _split(a)
    return dg(ah, bh) + (dg(ah, bl) + dg(al, bh))


_DOT_BWD = {_NN: (("g", "b", _NT), ("a", "g", _TN)),
            _NT: (("g", "b", _NN), ("g", "a", _TN)),
            _TN: (("b", "g", _NT), ("a", "g", _NN))}


@functools.partial(jax.custom_vjp, nondiff_argnums=(2, 3))
def _dot(a, b, dims, passes):
    return _dg(a, b, dims, passes)


def _dot_fwd(a, b, dims, passes):
    return _dg(a, b, dims, passes), (a, b)


def _dot_bwd(dims, passes, res, g):
    env = {"a": res[0], "b": res[1], "g": g}
    return tuple(_dg(env[p], env[q], d, passes) for p, q, d in _DOT_BWD[dims])


_dot.defvjp(_dot_fwd, _dot_bwd)


def _bdot(x, w):
    return _dot(x, w, _NN, 1)


@jax.custom_vjp
def _shift_down(z):
    return pltpu.roll(z, 1, 0)


def _shift_down_fwd(z):
    return pltpu.roll(z, 1, 0), None


def _shift_down_bwd(_, g):
    return (pltpu.roll(g, g.shape[0] - 1, 0),)


_shift_down.defvjp(_shift_down_fwd, _shift_down_bwd)


def _head_sum_impl(x):
    r = lax.broadcasted_iota(jnp.int32, (LANE, LANE), 0) // HEAD
    c = lax.broadcasted_iota(jnp.int32, (LANE, LANE), 1) // HEAD
    ones = (r == c).astype(BF16)
    hi, lo = _split(x)
    dg = lambda p: lax.dot_general(p, ones, _NN, preferred_element_type=F32)
    tiles = [slice(j, j + LANE) for j in range(0, x.shape[1], LANE)]
    return jnp.concatenate([dg(hi[:, s]) + dg(lo[:, s]) for s in tiles], axis=1)


@jax.custom_vjp
def _head_sum(x):
    return _head_sum_impl(x)


_head_sum.defvjp(lambda x: (_head_sum_impl(x), None), lambda _, g: (_head_sum_impl(g),))


def _mm(name, a, b, mode, out_dtype=F32, precise=False, tm=1024, tn=1024, tk=1536, plan=None):
    if mode == 'nn':
        (m, k), (_, n) = a.shape, b.shape
    elif mode == 'nt':
        (m, k), (n, _) = a.shape, b.shape
    else:
        (k, m), (_, n) = a.shape, b.shape
    tm, tn, tk = _tile(m, tm), _tile(n, tn), _tile(k, tk)
    nm, nn, nk = m // tm, n // tn, k // tk
    dims = {'nn': _NN, 'nt': _NT, 'tn': _TN}[mode]
    plan = _NO_PLAN if plan is None else plan
    parts, plan_in_specs, plan_out_shape, plan_out_specs, plan_sems = _carry(plan, 2, 1)

    def body(*refs):
        a_ref, b_ref, o_ref = refs[0], refs[1], refs[2 + len(plan.ins)]
        acc_ref = refs[3 + len(plan.ins) + len(plan.out_shape)]
        i, j, kk = pl.program_id(0), pl.program_id(1), pl.program_id(2)

        if plan is not _NO_PLAN:
            pl.when((i == 0) & (j == 0) & (kk == 0))(lambda: plan.start(*parts(refs)))

        @pl.when(kk == 0)
        def _():
            acc_ref[...] = jnp.zeros_like(acc_ref)

        acc_ref[...] += _dg(a_ref[...], b_ref[...], dims, 3 if precise else 1)

        @pl.when(kk == nk - 1)
        def _():
            o_ref[...] = acc_ref[...].astype(o_ref.dtype)

        if plan is not _NO_PLAN:
            pl.when((i == nm - 1) & (j == nn - 1) & (kk == nk - 1))(lambda: plan.wait(*parts(refs)))

    if mode == 'tn':
        a_spec = pl.BlockSpec((tk, tm), lambda i, j, l: (l, i))
    else:
        a_spec = pl.BlockSpec((tm, tk), lambda i, j, l: (i, l))
    if mode == 'nt':
        b_spec = pl.BlockSpec((tn, tk), lambda i, j, l: (j, l))
    else:
        b_spec = pl.BlockSpec((tk, tn), lambda i, j, l: (l, j))
    res = pl.pallas_call(
        body, name=name,
        out_shape=[jax.ShapeDtypeStruct((m, n), out_dtype)] + plan_out_shape,
        grid=(nm, nn, nk),
        in_specs=[a_spec, b_spec] + plan_in_specs,
        out_specs=[pl.BlockSpec((tm, tn), lambda i, j, l: (i, j))] + plan_out_specs,
        scratch_shapes=[pltpu.VMEM((tm, tn), F32)] + plan_sems,
        compiler_params=_params(("parallel", "parallel", "arbitrary") if plan is _NO_PLAN else ("arbitrary",) * 3),
    )(a, b, *plan.ins)
    return res[0] if plan is _NO_PLAN else (res[0], res[1:])


def _mm_tiles(name, a, b, mode, out_shape, grid, a_spec, b_spec, o_spec):
    dims = {'nn': _NN, 'nt': _NT, 'tn': _TN}[mode]
    nk = grid[2]

    def body(a_ref, b_ref, o_ref, acc_ref):
        kk = pl.program_id(2)

        @pl.when(kk == 0)
        def _():
            acc_ref[...] = jnp.zeros_like(acc_ref)

        acc_ref[...] += _dg(a_ref[...], b_ref[...], dims, 1)

        @pl.when(kk == nk - 1)
        def _():
            o_ref[...] = acc_ref[...]

    return pl.pallas_call(
        body, name=name,
        out_shape=jax.ShapeDtypeStruct(out_shape, F32),
        grid=grid, in_specs=[a_spec, b_spec], out_specs=o_spec,
        scratch_shapes=[pltpu.VMEM(o_spec.block_shape, F32)],
        compiler_params=_params(("parallel", "parallel", "arbitrary")),
    )(a, b)


def _s5_expand(name, u, blk, tm=2048):
    t = u.shape[0]
    tm = min(tm, t)
    ch, ln, nt = S5_TILE_CH, S5_TILE_LANES, S5_TILES
    return _mm_tiles(name, u, blk, 'nn', (t, 2 * S5_LANES), (t // tm, 2 * nt, 1),
                     pl.BlockSpec((tm, ch), lambda i, j, l: (i, j % nt)),
                     pl.BlockSpec((ch, ln), lambda i, j, l: (j % nt, j // nt)),
                     pl.BlockSpec((tm, ln), lambda i, j, l: (i, j)))


def _s5_contract(name, x, blk, tm=2048):
    t = x.shape[0]
    tm = min(tm, t)
    ch, ln, nt = S5_TILE_CH, S5_TILE_LANES, S5_TILES
    return _mm_tiles(name, x, blk, 'nt', (t, S5_WIDTH), (t // tm, nt, 2),
                     pl.BlockSpec((tm, ln), lambda i, j, l: (i, j + nt * l)),
                     pl.BlockSpec((ch, ln), lambda i, j, l: (j, l)),
                     pl.BlockSpec((tm, ch), lambda i, j, l: (i, j)))


def _s5_block_grad(name, u, x, tk=2048):
    t = u.shape[0]
    tk = min(tk, t)
    ch, ln, nt = S5_TILE_CH, S5_TILE_LANES, S5_TILES
    return _mm_tiles(name, u, x, 'tn', (S5_WIDTH, 2 * ln), (nt, 2, t // tk),
                     pl.BlockSpec((tk, ch), lambda i, j, l: (l, i)),
                     pl.BlockSpec((tk, ln), lambda i, j, l: (l, i + nt * j)),
                     pl.BlockSpec((ch, ln), lambda i, j, l: (i, j)))


def _full_spec(p):
    nd = p.ndim
    return pl.BlockSpec(p.shape, lambda i, nd=nd: (0,) * nd)


def _tok_fwd(name, fn, toks, params, outs, tm):
    t = toks[0].shape[0]
    nt, npar = len(toks), len(params)

    def body(*refs):
        tv = [r[...].astype(F32) for r in refs[:nt]]
        pv = [r[...].astype(F32) for r in refs[nt:nt + npar]]
        res = fn(*tv, *pv)
        for r, v in zip(refs[nt + npar:], res):
            r[...] = v.astype(r.dtype)

    return pl.pallas_call(
        body, name=name,
        out_shape=[jax.ShapeDtypeStruct((t, w), d) for w, d in outs],
        grid=(t // tm,),
        in_specs=[pl.BlockSpec((tm, a.shape[1]), lambda i: (i, 0)) for a in toks] + [_full_spec(p) for p in params],
        out_specs=[pl.BlockSpec((tm, w), lambda i: (i, 0)) for w, _ in outs],
        compiler_params=_params(("parallel",)),
    )(*toks, *params)


def _tok_bwd(name, fn, toks, params, cots, dtok, dpar, tm, acc_out=None, add_to=None):
    t = toks[0].shape[0]
    nt, npar = len(toks), len(params)
    cot_arrays = [c for c in cots if c is not None]
    ncot = len(cot_arrays)
    extra = [] if add_to is None else [add_to[1]]
    dtok_idx = [i for i, d in enumerate(dtok) if d is not None]
    dpar_idx = [i for i, d in enumerate(dpar) if d]

    def body(*refs):
        pos = 0
        tin = refs[pos:pos + nt]; pos += nt
        pin = refs[pos:pos + npar]; pos += npar
        cin = refs[pos:pos + ncot]; pos += ncot
        ein = refs[pos:pos + len(extra)]; pos += len(extra)
        dto = refs[pos:pos + len(dtok_idx)]; pos += len(dtok_idx)
        dpo = refs[pos:pos + len(dpar_idx)]; pos += len(dpar_idx)
        acc = refs[pos] if acc_out is not None else None
        first = pl.program_id(0) == 0

        tv = [r[...].astype(F32) for r in tin]
        pv = [r[...].astype(F32) for r in pin]
        res, vjp = jax.vjp(fn, *tv, *pv)
        cit = iter(cin)
        cs = tuple(jnp.ones_like(o) if c is None else next(cit)[...].astype(F32) for c, o in zip(cots, res))
        g = vjp(cs)
        for r, i in zip(dto, dtok_idx):
            v = g[i]
            if add_to is not None and add_to[0] == i:
                v = v + ein[0][...].astype(F32)
            r[...] = v.astype(r.dtype)

        @pl.when(first)
        def _():
            for r in dpo:
                r[...] = jnp.zeros_like(r)
            if acc is not None:
                acc[...] = jnp.zeros_like(acc)

        for r, i in zip(dpo, dpar_idx):
            r[...] += g[nt + i]
        if acc is not None:
            acc[...] += res[acc_out]

    out_shape = [jax.ShapeDtypeStruct(toks[i].shape, dtok[i]) for i in dtok_idx]
    out_shape += [jax.ShapeDtypeStruct(params[i].shape, F32) for i in dpar_idx]
    out_specs = [pl.BlockSpec((tm, toks[i].shape[1]), lambda i_: (i_, 0)) for i in dtok_idx]
    out_specs += [_full_spec(params[i]) for i in dpar_idx]
    if acc_out is not None:
        out_shape.append(jax.ShapeDtypeStruct((1, 1), F32))
        out_specs.append(pl.BlockSpec((1, 1), lambda i_: (0, 0)))
    tok_spec = lambda a: pl.BlockSpec((tm, a.shape[1]), lambda i_: (i_, 0))
    return pl.pallas_call(
        body, name=name,
        out_shape=out_shape,
        grid=(t // tm,),
        in_specs=[tok_spec(a) for a in toks] + [_full_spec(p) for p in params]
        + [tok_spec(c) for c in cot_arrays] + [tok_spec(e) for e in extra],
        out_specs=out_specs,
        compiler_params=_params(("arbitrary",)),
    )(*toks, *params, *cot_arrays, *extra)


def _small_fwd(name, fn, ins, outs):
    n = len(ins)

    def body(*refs):
        res = fn(*[r[...] for r in refs[:n]])
        for r, v in zip(refs[n:], res):
            r[...] = v.astype(r.dtype)

    return pl.pallas_call(
        body, name=name,
        out_shape=[jax.ShapeDtypeStruct(s, d) for s, d in outs],
        compiler_params=_params(),
    )(*ins)


def _small_bwd(name, fn, ins, cots):
    n = len(ins)

    def body(*refs):
        _, vjp = jax.vjp(fn, *[r[...] for r in refs[:n]])
        g = vjp(tuple(r[...] for r in refs[n:n + len(cots)]))
        for r, v in zip(refs[n + len(cots):], g):
            r[...] = v

    return pl.pallas_call(
        body, name=name,
        out_shape=[jax.ShapeDtypeStruct(a.shape, F32) for a in ins],
        compiler_params=_params(),
    )(*ins, *cots)


def _rms(x, g):
    return x * lax.rsqrt(jnp.mean(x * x, axis=-1, keepdims=True) + RMS_EPS) * g


def _f_norm_in(x, g):
    return (_rms(x, g),)


def _f_mix_res(x, mixed, g):
    h1 = x + mixed
    return h1, _rms(h1, g)


def _f_ffn_act(a13):
    a1, a3 = a13[:, :FFN_HIDDEN], a13[:, FFN_HIDDEN:]
    return (jax.nn.silu(a1) * a3,)


def _f_ffn_res(h1, ffo, g):
    h2 = h1 + ffo
    return h2, _rms(h2, g)


def _f_loss(h2, gpre, pu, target, g):
    h3 = h2 + jax.nn.sigmoid(gpre) * pu
    y = _rms(h3, g)
    err = jnp.square(y - target)
    return (0.5 * jnp.sum(jnp.mean(err, axis=-1, keepdims=True), axis=0, keepdims=True),)


def _f_s5_post(ypre, u, d, glu_w, glu_b):
    z = jax.nn.gelu(ypre + u * d)
    return (z * jax.nn.sigmoid(_bdot(z, glu_w) + glu_b),)


def _softplus(x):
    return jnp.maximum(x, 0.0) + jnp.log(1.0 + jnp.exp(-jnp.abs(x)))


def _f_rw_pre(z, carry, shift_mu, w0, w2, a0, a2, g2, k_k, k_a):
    rw = RWKV_WIDTH
    first_row = lax.broadcasted_iota(jnp.int32, z.shape, 0) == 0
    prev = jnp.where(first_row, carry, _shift_down(z))
    zs = z + (prev - z) * shift_mu
    o1, o2 = 3 * rw + DECAY_LORA, 3 * rw + DECAY_LORA + AAA_LORA
    r, k, v = zs[:, :rw], zs[:, rw:2 * rw], zs[:, 2 * rw:3 * rw]
    wl, al, gl = zs[:, 3 * rw:o1], zs[:, o1:o2], zs[:, o2:]
    w = -_softplus(-(w0 + _bdot(jnp.tanh(wl), w2))) - 0.5
    log_decay = -jnp.exp(w)
    a = jax.nn.sigmoid(a0 + _bdot(al, a2))
    g = _bdot(jax.nn.sigmoid(gl), g2)
    kk = k * k_k
    norm = jnp.sqrt(_head_sum(kk * kk))
    kk = kk / jnp.maximum(norm, L2_EPS)
    kp = k * (1.0 + (a - 1.0) * k_a)
    return r, log_decay, kp, v, -kk, kk * a, g


def _f_rw_post(y, r, kp, v, g, ln_w, ln_b, r_k):
    yc = y - _head_sum(y) * (1.0 / HEAD)
    var = _head_sum(yc * yc) * (1.0 / HEAD)
    yn = yc * lax.rsqrt(var + GN_EPS) * ln_w + ln_b
    bonus = _head_sum(r * kp * r_k) * v
    return ((yn + bonus) * g,)


def _f_s5_lam(lam_re, lam_im, log_step):
    step = jnp.exp(log_step)
    dr, di = lam_re * step, lam_im * step
    e = jnp.exp(dr)
    lbr, lbi = e * jnp.cos(di), e * jnp.sin(di)
    nr, ni = lbr - 1.0, lbi
    den = lam_re * lam_re + lam_im * lam_im
    return lbr, lbi, (nr * lam_re + ni * lam_im) / den, (ni * lam_re - nr * lam_im) / den


def _f_s5_build(coef_r, coef_i, btr, bti, ctr, cti):
    bbr = coef_r * btr - coef_i * bti
    bbi = coef_r * bti + coef_i * btr
    shape = (S5_WIDTH, S5_TILE_LANES)
    rows = (lax.broadcasted_iota(jnp.int32, shape, 0) % S5_TILE_CH) // S5_GROUP
    cols = lax.broadcasted_iota(jnp.int32, shape, 1) // S5_STATE
    mask = (rows == cols).astype(F32)

    def blocks(m):
        per_tile = [m[:, S5_TILE_LANES * i:S5_TILE_LANES * (i + 1)] for i in range(S5_TILES)]
        return jnp.concatenate([t for t in per_tile for _ in range(S5_TILE_GROUPS)], axis=0) * mask

    return (jnp.concatenate([blocks(bbr), blocks(bbi)], axis=1),
            jnp.concatenate([blocks(ctr), -blocks(cti)], axis=1))


HALO = 8


def _rw_pre_specs(z, params, tm, order):
    halo_blocks = tm // HALO
    return ([pl.BlockSpec((tm, z.shape[1]), lambda i: (order(i), 0)),
             pl.BlockSpec((HALO, z.shape[1]), lambda i: (jnp.maximum(order(i) * halo_blocks - 1, 0), 0))]
            + [_full_spec(p) for p in params])


def _rw_pre_fwd(z, params, tm):
    t = z.shape[0]
    npar = len(params)

    def body(z_ref, halo_ref, *refs):
        carry = jnp.where(pl.program_id(0) == 0, 0.0, halo_ref[pl.ds(HALO - 1, 1), :])
        res = _f_rw_pre(z_ref[...], carry, *[r[...].astype(F32) for r in refs[:npar]])
        for r, v in zip(refs[npar:], res):
            r[...] = v

    return pl.pallas_call(
        body, name="rw_pre",
        out_shape=[jax.ShapeDtypeStruct((t, RWKV_WIDTH), F32)] * 7,
        grid=(t // tm,),
        in_specs=_rw_pre_specs(z, params, tm, lambda i: i),
        out_specs=[pl.BlockSpec((tm, RWKV_WIDTH), lambda i: (i, 0))] * 7,
        compiler_params=_params(("parallel",)),
    )(z, z, *params)


def _rw_pre_bwd(z, params, cots, tm):
    t = z.shape[0]
    nt = t // tm
    npar = len(params)
    order = lambda i: nt - 1 - i
    flat_cots = [a for group in cots for a in group]
    ncot = len(flat_cots)

    def body(z_ref, halo_ref, *refs):
        pin, cin = refs[:npar], list(refs[npar:npar + ncot])
        dz_ref = refs[npar + ncot]
        dpo = refs[npar + ncot + 1:npar + ncot + 1 + npar]
        dcarry_ref = refs[npar + ncot + 1 + npar]
        i = pl.program_id(0)

        @pl.when(i == 0)
        def _():
            dcarry_ref[...] = jnp.zeros_like(dcarry_ref)
            for r in dpo:
                r[...] = jnp.zeros_like(r)

        carry = jnp.where(i == nt - 1, 0.0, halo_ref[pl.ds(HALO - 1, 1), :])
        _, vjp = jax.vjp(_f_rw_pre, z_ref[...], carry, *[r[...].astype(F32) for r in pin])
        g = vjp(tuple(sum(cin.pop(0)[...] for _ in group) for group in cots))
        last_row = lax.broadcasted_iota(jnp.int32, z_ref.shape, 0) == tm - 1
        dz_ref[...] = g[0] + jnp.where(last_row, dcarry_ref[...], 0.0)
        dcarry_ref[...] = g[1]
        for r, v in zip(dpo, g[2:]):
            r[...] += v

    tok = lambda w: pl.BlockSpec((tm, w), lambda i: (order(i), 0))
    return pl.pallas_call(
        body, name="rw_pre_bwd",
        out_shape=[jax.ShapeDtypeStruct(z.shape, F32)] + [jax.ShapeDtypeStruct(p.shape, F32) for p in params],
        grid=(nt,),
        in_specs=_rw_pre_specs(z, params, tm, order) + [tok(RWKV_WIDTH)] * ncot,
        out_specs=[tok(z.shape[1])] + [_full_spec(p) for p in params],
        scratch_shapes=[pltpu.VMEM((1, z.shape[1]), F32)],
        compiler_params=_params(("arbitrary",)),
    )(z, z, *params, *flat_cots)


def _s5_scan(bu, lam, tm, plan):
    t, w = bu.shape
    h = w // 2
    nt = t // tm
    parts, plan_in_specs, plan_out_shape, plan_out_specs, plan_sems = _carry(plan, 2, 1)

    def body(*refs):
        bu_ref, lam_ref = refs[:2]
        xb_ref = refs[2 + len(plan.ins)]
        carry_ref = refs[3 + len(plan.ins) + len(plan.out_shape)]
        x_ref, refs = refs[-1], refs[:-1]

        @pl.when(pl.program_id(0) == 0)
        def _():
            carry_ref[...] = jnp.zeros_like(carry_ref)
            plan.start(*parts(refs))

        lr, li = lam_ref[:, :h], lam_ref[:, h:]

        def step(s, c):
            cr, ci = c
            row = pl.ds(s, 1)
            nr = lr * cr - li * ci + bu_ref[row, :h]
            ni = lr * ci + li * cr + bu_ref[row, h:]
            x_ref[row, :h] = nr
            x_ref[row, h:] = ni
            return nr, ni

        cr, ci = lax.fori_loop(0, tm, step, (carry_ref[:, :h], carry_ref[:, h:]), unroll=SCAN_UNROLL)
        carry_ref[:, :h] = cr
        carry_ref[:, h:] = ci
        xb_ref[...] = x_ref[...].astype(BF16)

        @pl.when(pl.program_id(0) == nt - 1)
        def _():
            plan.wait(*parts(refs))

    spec = pl.BlockSpec((tm, w), lambda i: (i, 0))
    res = pl.pallas_call(
        body, name="s5_scan",
        out_shape=[jax.ShapeDtypeStruct((t, w), BF16)] + plan_out_shape,
        grid=(nt,),
        in_specs=[spec, pl.BlockSpec((1, w), lambda i: (0, 0))] + plan_in_specs,
        out_specs=[spec] + plan_out_specs,
        scratch_shapes=[pltpu.VMEM((1, w), F32)] + plan_sems + [pltpu.VMEM((tm, w), F32)],
        compiler_params=_params(("arbitrary",)),
    )(bu, lam, *plan.ins)
    return res[0], res[1:]


def _s5_scan_bwd(dx, xb, lam, tm):
    t, w = dx.shape
    h = w // 2
    nt = t // tm
    halo = BF16_SUBLANES

    def body(dx_ref, xb_ref, halo_ref, lam_ref, dbu_out_ref, dlam_ref, carry_ref, dbu_ref):
        @pl.when(pl.program_id(0) == 0)
        def _():
            carry_ref[...] = jnp.zeros_like(carry_ref)
            dlam_ref[...] = jnp.zeros_like(dlam_ref)

        lr, li = lam_ref[:, :h], lam_ref[:, h:]

        def step(s, c):
            cr, ci = c
            row = pl.ds(tm - 1 - s, 1)
            nr = lr * cr + li * ci + dx_ref[row, :h]
            ni = lr * ci - li * cr + dx_ref[row, h:]
            dbu_ref[row, :h] = nr
            dbu_ref[row, h:] = ni
            return nr, ni

        cr, ci = lax.fori_loop(0, tm, step, (carry_ref[:, :h], carry_ref[:, h:]), unroll=SCAN_UNROLL)
        carry_ref[:, :h] = cr
        carry_ref[:, h:] = ci
        halo_rows = lax.broadcasted_iota(jnp.int32, (halo, w), 0)
        before = jnp.sum(jnp.where(halo_rows == halo - 1, halo_ref[...].astype(F32), 0.0), axis=0, keepdims=True)
        before = jnp.where(pl.program_id(0) == nt - 1, 0.0, before)
        first_row = lax.broadcasted_iota(jnp.int32, (tm, w), 0) == 0
        xp = jnp.where(first_row, before, pltpu.roll(xb_ref[...].astype(F32), 1, 0))
        gr, gi = dbu_ref[:, :h], dbu_ref[:, h:]
        pr, pi_ = xp[:, :h], xp[:, h:]
        dlam_ref[:, :h] += jnp.sum(gr * pr + gi * pi_, axis=0, keepdims=True)
        dlam_ref[:, h:] += jnp.sum(gi * pr - gr * pi_, axis=0, keepdims=True)
        dbu_out_ref[...] = dbu_ref[...].astype(BF16)

    spec = pl.BlockSpec((tm, w), lambda i: (nt - 1 - i, 0))
    halo_spec = pl.BlockSpec((halo, w), lambda i: (jnp.maximum((nt - 1 - i) * (tm // halo) - 1, 0), 0))
    row_spec = pl.BlockSpec((1, w), lambda i: (0, 0))
    return pl.pallas_call(
        body, name="s5_scan_bwd",
        out_shape=[jax.ShapeDtypeStruct((t, w), BF16), jax.ShapeDtypeStruct((1, w), F32)],
        grid=(nt,),
        in_specs=[spec, spec, halo_spec, row_spec],
        out_specs=[spec, row_spec],
        scratch_shapes=[pltpu.VMEM((1, w), F32), pltpu.VMEM((tm, w), F32)],
        compiler_params=_params(("arbitrary",)),
    )(dx, xb, xb, lam)


def _unit_lower_inverses_impl(ns):
    c = ns[0].shape[0]
    eye = (lax.broadcasted_iota(jnp.int32, (c, c), 0) == lax.broadcasted_iota(jnp.int32, (c, c), 1)).astype(F32)
    inv = [eye + n for n in ns]
    pw = [_dg(n, n, _NN, WKV_PASSES) for n in ns]
    for _ in range(int(math.log2(c)) - 2):
        both = [_dg(jnp.concatenate([i, q], axis=0), q, _NN, WKV_PASSES) for i, q in zip(inv, pw)]
        inv = [i + q[:c] for i, q in zip(inv, both)]
        pw = [q[c:] for q in both]
    return tuple(i + _dg(i, q, _NN, WKV_PASSES) for i, q in zip(inv, pw))


@jax.custom_vjp
def _unit_lower_inverses(ns):
    return _unit_lower_inverses_impl(ns)


def _unit_lower_inverses_fwd(ns):
    inv = _unit_lower_inverses_impl(ns)
    return inv, inv


def _unit_lower_inverses_bwd(inv, g):
    left = [_dg(i, gi, _TN, WKV_PASSES) for i, gi in zip(inv, g)]
    return (tuple(_dg(q, i, _NT, WKV_PASSES) for q, i in zip(left, inv)),)


_unit_lower_inverses.defvjp(_unit_lower_inverses_fwd, _unit_lower_inverses_bwd)


def _wkv_chunks(s0, r, lw, k, v, a, b):
    c = r[0].shape[0]
    row = lax.broadcasted_iota(jnp.int32, (c, c), 0)
    col = lax.broadcasted_iota(jnp.int32, (c, c), 1)
    incl, strict = col <= row, col < row
    tri = incl.astype(F32)
    each = lambda f, *xs: [f(*t) for t in zip(*xs)]
    stack = lambda p, q: jnp.concatenate([p, q], axis=0)
    dot = lambda p, q, dims=_NN: _dot(p, q, dims, WKV_PASSES)
    lc = each(lambda l: _dot(tri, l, _NN, 2), lw)
    e_neg = each(lambda l: jnp.exp(-l), lc)
    ar = each(lambda x, z, l, w: stack(x * jnp.exp(l - w), z * jnp.exp(l)), a, r, lc, lw)
    bk = each(lambda x, z, e: stack(x * e, z * e), b, k, e_neg)
    m = each(lambda p, q: dot(p, q, _NT), ar, bk)
    mab = each(lambda q: jnp.where(strict, q[:c, :c], 0.0), m)
    mak_mrk = each(lambda q: stack(jnp.where(strict, q[:c, c:], 0.0), jnp.where(incl, q[c:, c:], 0.0)), m)
    mrb = each(lambda q: jnp.where(incl, q[c:, :c], 0.0), m)
    xy = each(lambda p, s, q, z: dot(p, s, _NT) + dot(q, z), ar, s0, mak_mrk, v)
    inv = _unit_lower_inverses(tuple(mab))
    u = each(lambda i, q: dot(i, q[:c]), inv, xy)
    y = each(lambda q, z, p: q[c:] + dot(z, p), xy, mrb, u)
    e_tot = each(lambda l: jnp.exp(jnp.sum(l, axis=0, keepdims=True)), lw)
    s1 = each(lambda s, p, z, q, e: (s + dot(stack(p, z), q, _TN)) * e, s0, u, v, bk, e_tot)
    return y, s1


def _carry(plan, n_args, n_outs):
    n_in, n_out = len(plan.ins), len(plan.out_shape)

    def parts(refs):
        base = n_args + n_in + n_outs
        return refs[n_args:n_args + n_in], refs[base:base + n_out], refs[base + n_out + 1:]

    return parts, [HBM_SPEC] * n_in, list(plan.out_shape), [HBM_SPEC] * n_out, list(plan.sems)


def _head_cols(ref):
    return tuple(ref[:, h * HEAD:(h + 1) * HEAD] for h in range(HEADS))


def _wkv_fwd(seqs, plan):
    t, w = seqs[0].shape
    c, n = WKV_CHUNK, HEAD
    nc = t // c
    parts, plan_in_specs, plan_out_shape, plan_out_specs, plan_sems = _carry(plan, 6, 2)

    def body(*refs):
        ins, (y_ref, ck_ref) = refs[:6], refs[6 + len(plan.ins):8 + len(plan.ins)]
        s_ref = refs[8 + len(plan.ins) + len(plan.out_shape)]

        @pl.when(pl.program_id(0) == 0)
        def _():
            s_ref[...] = jnp.zeros_like(s_ref)
            plan.start(*parts(refs))

        s0 = tuple(s_ref[h] for h in range(HEADS))
        ys, s1 = _wkv_chunks(s0, *[_head_cols(r) for r in ins])
        for h in range(HEADS):
            ck_ref[0, h] = s0[h]
            y_ref[:, h * n:(h + 1) * n] = ys[h]
            s_ref[h] = s1[h]

        @pl.when(pl.program_id(0) == nc - 1)
        def _():
            plan.wait(*parts(refs))

    spec = pl.BlockSpec((c, w), lambda i: (i, 0))
    res = pl.pallas_call(
        body, name="wkv_fwd",
        out_shape=[jax.ShapeDtypeStruct((t, w), F32), jax.ShapeDtypeStruct((nc, HEADS, n, n), F32)] + plan_out_shape,
        grid=(nc,),
        in_specs=[spec] * 6 + plan_in_specs,
        out_specs=[spec, pl.BlockSpec((1, HEADS, n, n), lambda i: (i, 0, 0, 0))] + plan_out_specs,
        scratch_shapes=[pltpu.VMEM((HEADS, n, n), F32)] + plan_sems,
        compiler_params=_params(("arbitrary",)),
    )(*seqs, *plan.ins)
    return res[0], res[1], res[2:]


def _wkv_bwd(seqs, ck, dy, plan):
    t, w = seqs[0].shape
    c, n = WKV_CHUNK, HEAD
    nc = t // c
    parts, plan_in_specs, plan_out_shape, plan_out_specs, plan_sems = _carry(plan, 8, 6)

    def body(*refs):
        ins, ck_ref, dy_ref = refs[:6], refs[6], refs[7]
        outs = refs[8 + len(plan.ins):14 + len(plan.ins)]
        ds_ref = refs[14 + len(plan.ins) + len(plan.out_shape)]

        @pl.when(pl.program_id(0) == 0)
        def _():
            ds_ref[...] = jnp.zeros_like(ds_ref)
            plan.start(*parts(refs))

        s0 = tuple(ck_ref[0, h] for h in range(HEADS))
        _, vjp = jax.vjp(_wkv_chunks, s0, *[_head_cols(r) for r in ins])
        g = vjp((list(_head_cols(dy_ref)), [ds_ref[h] for h in range(HEADS)]))
        for h in range(HEADS):
            ds_ref[h] = g[0][h]
            for o, d in zip(outs, g[1:]):
                o[:, h * n:(h + 1) * n] = d[h]

        @pl.when(pl.program_id(0) == nc - 1)
        def _():
            plan.wait(*parts(refs))

    spec = pl.BlockSpec((c, w), lambda i: (nc - 1 - i, 0))
    res = pl.pallas_call(
        body, name="wkv_bwd",
        out_shape=[jax.ShapeDtypeStruct((t, w), F32)] * 6 + plan_out_shape,
        grid=(nc,),
        in_specs=[spec] * 6 + [pl.BlockSpec((1, HEADS, n, n), lambda i: (nc - 1 - i, 0, 0, 0)), spec] + plan_in_specs,
        out_specs=[spec] * 6 + plan_out_specs,
        scratch_shapes=[pltpu.VMEM((HEADS, n, n), F32)] + plan_sems,
        compiler_params=_params(("arbitrary",)),
    )(*seqs, ck, dy, *plan.ins)
    return res[:6], res[6:]


def _coords():
    return lax.axis_index("x"), lax.axis_index("y"), lax.axis_index("c")


def _flip(v, f):
    return 1 - v if f else v


_CHIP_FLIPS = [(1, 0), (0, 1), (1, 1)]
_DEV_FLIPS = [(fx, fy, fc) for fx in (0, 1) for fy in (0, 1) for fc in (0, 1) if (fx, fy, fc) != (0, 0, 0)]
HBM_SPEC = pl.BlockSpec(memory_space=pl.ANY)


def _chip_peer(k, x, y):
    fx, fy = _CHIP_FLIPS[k]
    return _flip(x, fx), _flip(y, fy)


def _dev_peer(k, x, y, c):
    fx, fy, fc = _DEV_FLIPS[k]
    return _flip(x, fx), _flip(y, fy), _flip(c, fc)


def _rows_of_core(ref, core):
    h = ref.shape[-2] // 2
    rows = pl.ds(pl.multiple_of(core * h, 8), h)
    return ref.at[rows, :] if len(ref.shape) == 2 else ref.at[:, rows, :]


class _Plan(NamedTuple):
    ins: Sequence[Any]
    out_shape: Sequence[Any]
    sems: Sequence[Any]
    start: Callable
    wait: Callable


_NO_PLAN = _Plan([], [], [], lambda *_: None, lambda *_: None)


def _run_plan(name, plan):
    n_in, n_out = len(plan.ins), len(plan.out_shape)

    def body(*refs):
        parts = refs[:n_in], refs[n_in:n_in + n_out], refs[n_in + n_out:]
        plan.start(*parts)
        plan.wait(*parts)

    return pl.pallas_call(
        body, name=name, out_shape=list(plan.out_shape),
        in_specs=[HBM_SPEC] * n_in, out_specs=[HBM_SPEC] * n_out, scratch_shapes=list(plan.sems),
    )(*plan.ins)


def _gather_plan(shards):
    n = len(shards)

    def copies(srcs, outs, sems):
        send_sems, recv_sems, local_sems = sems
        x, y, c = _coords()
        me = 2 * x + y

        def remote(i, k, arriving):
            px, py = _chip_peer(k, x, y)
            return pltpu.make_async_remote_copy(
                src_ref=srcs[i], dst_ref=outs[i].at[2 * px + py if arriving else me],
                send_sem=send_sems.at[i, k], recv_sem=recv_sems.at[i, k],
                device_id=(px, py, c), device_id_type=MESH)

        own = [pltpu.make_async_copy(srcs[i], outs[i].at[me], local_sems.at[i]) for i in range(n)]
        pairs = [(i, k) for k in range(3) for i in range(n)]
        return own, [remote(i, k, False) for i, k in pairs], [remote(i, k, True) for i, k in pairs]

    return _Plan(
        ins=shards, out_shape=[jax.ShapeDtypeStruct((4,) + s.shape, s.dtype) for s in shards],
        sems=[pltpu.SemaphoreType.DMA((n, 3)), pltpu.SemaphoreType.DMA((n, 3)), pltpu.SemaphoreType.DMA((n,))],
        start=functools.partial(_start_copies, copies), wait=functools.partial(_wait_copies, copies))


def _start_copies(copies, ins, outs, sems):
    own, sends, _ = copies(ins, outs, sems)
    for cp in own + sends:
        cp.start()


def _wait_copies(copies, ins, outs, sems):
    own, sends, arrivals = copies(ins, outs, sems)
    for cp in arrivals:
        cp.wait_recv()
    for cp in sends:
        cp.wait_send()
    for cp in own:
        cp.wait()


def _exchange_plan(gs, small=None):
    n = len(gs)
    arrays = list(gs) + ([] if small is None else [small])

    def copies(srcs, outs, sems):
        send_sems, recv_sems, local_sems = sems
        x, y, c = _coords()
        me = 4 * x + 2 * y + c

        def piece(i, px, py, pc):
            if i == n:
                return srcs[i].at[4 * px + 2 * py + pc]
            return _rows_of_core(srcs[i].at[2 * px + py], pc)

        def remote(i, k, arriving):
            px, py, pc = _dev_peer(k, x, y, c)
            return pltpu.make_async_remote_copy(
                src_ref=piece(i, px, py, pc), dst_ref=outs[i].at[4 * px + 2 * py + pc if arriving else me],
                send_sem=send_sems.at[i, k], recv_sem=recv_sems.at[i, k],
                device_id=(px, py, pc), device_id_type=MESH)

        own = [pltpu.make_async_copy(piece(i, x, y, c), outs[i].at[me], local_sems.at[i]) for i in range(len(arrays))]
        pairs = [(i, k) for k in range(7) for i in range(len(arrays))]
        return own, [remote(i, k, False) for i, k in pairs], [remote(i, k, True) for i, k in pairs]

    out_shape = [jax.ShapeDtypeStruct((8, g.shape[1] // 2, g.shape[2]), g.dtype) for g in gs]
    if small is not None:
        out_shape.append(jax.ShapeDtypeStruct(small.shape, small.dtype))
    m = len(arrays)
    return _Plan(
        ins=arrays, out_shape=out_shape,
        sems=[pltpu.SemaphoreType.DMA((m, 7)), pltpu.SemaphoreType.DMA((m, 7)), pltpu.SemaphoreType.DMA((m,))],
        start=functools.partial(_start_copies, copies), wait=functools.partial(_wait_copies, copies))


def _share_cores(halves, small):
    n = len(halves)

    def body(*refs):
        srcs, small_src, outs, small_out = refs[:n], refs[n], refs[n + 1:2 * n + 1], refs[2 * n + 1]
        mine, theirs = refs[2 * n + 2:3 * n + 2], refs[3 * n + 2:4 * n + 2]
        send_sems, recv_sems, ssend, srecv, local_sems = refs[4 * n + 2:]
        x, y, c = _coords()
        me = 4 * x + 2 * y + c

        def big(i):
            return pltpu.make_async_remote_copy(
                src_ref=mine[i], dst_ref=theirs[i], send_sem=send_sems.at[i], recv_sem=recv_sems.at[i],
                device_id=(x, y, 1 - c), device_id_type=MESH)

        def tiny(k, arriving):
            px, py, pc = _dev_peer(k, x, y, c)
            return pltpu.make_async_remote_copy(
                src_ref=small_src, dst_ref=small_out.at[4 * px + 2 * py + pc if arriving else me],
                send_sem=ssend.at[k], recv_sem=srecv.at[k], device_id=(px, py, pc), device_id_type=MESH)

        small_sends = [tiny(k, False) for k in range(7)]
        own_small = pltpu.make_async_copy(small_src, small_out.at[me], local_sems.at[2 * n])
        stage = [pltpu.make_async_copy(srcs[i], mine[i], local_sems.at[i]) for i in range(n)]
        for cp in small_sends + [own_small] + stage:
            cp.start()
        sends = []
        for i in range(n):
            stage[i].wait()
            sends.append(big(i))
            sends[-1].start()
        store = [pltpu.make_async_copy(mine[i], outs[i].at[c], local_sems.at[i]) for i in range(n)]
        for cp in store:
            cp.start()
        for i in range(n):
            big(i).wait_recv()
            store.append(pltpu.make_async_copy(theirs[i], outs[i].at[1 - c], local_sems.at[n + i]))
            store[-1].start()
        for k in range(7):
            tiny(k, True).wait_recv()
        for cp in sends + small_sends:
            cp.wait_send()
        for cp in store + [own_small]:
            cp.wait()

    staged = [pltpu.VMEM(s.shape, s.dtype) for s in halves]
    res = pl.pallas_call(
        body, name="share_cores",
        out_shape=[jax.ShapeDtypeStruct((2,) + s.shape, s.dtype) for s in halves]
        + [jax.ShapeDtypeStruct((8,) + small.shape, small.dtype)],
        in_specs=[HBM_SPEC] * (n + 1), out_specs=[HBM_SPEC] * (n + 1),
        scratch_shapes=staged + staged + [
            pltpu.SemaphoreType.DMA((n,)), pltpu.SemaphoreType.DMA((n,)),
            pltpu.SemaphoreType.DMA((7,)), pltpu.SemaphoreType.DMA((7,)),
            pltpu.SemaphoreType.DMA((2 * n + 1,))],
        compiler_params=pltpu.CompilerParams(vmem_limit_bytes=VMEM_LIMIT_BYTES),
    )(*halves, small)
    return res[:n], res[n]


BF16_SUBLANES = 16


def _row_tile(n, target, step=BF16_SUBLANES):
    return max([d for d in range(step, min(n, target) + 1, step) if n % d == 0] or [n])


def _ew(name, fn, ins, outs, block_bytes=2 << 20):
    rows, cols = ins[0].shape[-2:]
    lead = max(math.prod(a.shape[:-2]) for a in ins)
    tr = _row_tile(rows, max(8, block_bytes // (4 * cols * lead)))
    n = len(ins)

    def spec(shape):
        if len(shape) == 2:
            return pl.BlockSpec((tr, cols), lambda i: (i, 0))
        return pl.BlockSpec((shape[0], tr, cols), lambda i: (0, i, 0))

    def body(*refs):
        res = fn(*[r[...] for r in refs[:n]])
        for r, v in zip(refs[n:], res):
            r[...] = v

    return pl.pallas_call(
        body, name=name,
        out_shape=[jax.ShapeDtypeStruct(s, F32) for s in outs],
        grid=(rows // tr,),
        in_specs=[spec(a.shape) for a in ins],
        out_specs=[spec(s) for s in outs],
        compiler_params=_params(("parallel",)),
    )(*ins)


def _sum_slots(a):
    total = a[0].astype(F32)
    for s in range(1, a.shape[0]):
        total = total + a[s].astype(F32)
    return (total,)


def _adamw(g, w, m, v):
    bc1 = 1.0 - ADAM_B1 ** ADAM_STEP
    bc2 = 1.0 - ADAM_B2 ** ADAM_STEP
    m_new = ADAM_B1 * m + (1.0 - ADAM_B1) * g
    v_new = ADAM_B2 * v + (1.0 - ADAM_B2) * jnp.square(g)
    delta = -ADAM_LR * ((m_new / bc1) / (jnp.sqrt(v_new / bc2) + ADAM_EPS) + ADAM_WD * w)
    return delta, m_new, v_new


def _mat(a):
    return a.reshape(a.shape[-2:])


def _to_shard_major(full, axis):
    rows, cols = full.shape
    if axis == 0:
        return full.reshape(4, rows // 4, cols)
    return full.reshape(rows, 4, cols // 4).transpose(1, 0, 2)


def _from_shard_major(a, axis):
    _, r, cs = a.shape
    if axis == 0:
        return a.reshape(4 * r, cs)
    return a.transpose(1, 0, 2).reshape(r, 4 * cs)


def _pack_small(arrays, tail=None):
    flat = [arrays[n].reshape(-1) for n in SMALL_NAMES] + ([] if tail is None else [tail.reshape(1)])
    used = sum(a.shape[0] for a in flat)
    flat.append(jnp.zeros((8 * SMALL_ROWS * PACK_COLS - used,), F32))
    return jnp.concatenate(flat).reshape(8, SMALL_ROWS, PACK_COLS)


def _unpack_small(packed, shapes):
    flat = packed.reshape(-1)
    out, off = {}, 0
    for n in SMALL_NAMES:
        size = math.prod(shapes[n])
        out[n] = flat[off:off + size].reshape(shapes[n])
        off += size
    return out


def _row(a):
    return a.reshape(1, -1)


def _local_step(x, p, target, wf, ws, late_shards):
    wf = dict(wf)
    t = x.shape[0]
    tm = min(256, t)
    tw = min(512, t)
    g = {}

    lam_re, lam_im = ws['s5_lam_re'].reshape(S5_GROUPS, S5_STATE), ws['s5_lam_im'].reshape(S5_GROUPS, S5_STATE)
    log_step = ws['s5_log_step'].reshape(S5_GROUPS, 1)
    gp = (S5_GROUPS, S5_STATE)
    lam_ins = (lam_re, lam_im, log_step)
    lbr, lbi, cfr, cfi = _small_fwd("s5_lam", _f_s5_lam, lam_ins, [(gp, F32)] * 4)
    lam_row = jnp.concatenate([_row(lbr), _row(lbi)], axis=1)
    to_t = lambda a, perm: a.reshape((S5_GROUPS,) + a.shape[-2:]).transpose(perm).reshape(S5_GROUP, S5_LANES)
    build_ins = (_row(cfr), _row(cfi), to_t(ws['s5_b_re'], (2, 0, 1)), to_t(ws['s5_b_im'], (2, 0, 1)),
                 to_t(ws['s5_c_re'], (1, 0, 2)), to_t(ws['s5_c_im'], (1, 0, 2)))
    block_shape = (S5_WIDTH, 2 * S5_TILE_LANES)
    b_blk, c_blk = _small_fwd("s5_build", _f_s5_build, build_ins, [(block_shape, F32)] * 2)

    norm_mix, norm_ffn, norm_ple = _row(ws['norm_mix']), _row(ws['norm_ffn']), _row(ws['norm_ple'])
    final_norm = _row(ws['final_norm'])
    (xn,) = _tok_fwd("norm_in", _f_norm_in, [x], [norm_mix], [(x.shape[1], BF16)], tw)
    u = _mm("proj_s5", xn, wf['w_in'][:, :S5_WIDTH], 'nn')
    z = _mm("proj_rw", xn, wf['w_in'][:, S5_WIDTH:], 'nn')

    bu = _s5_expand("s5_bu", u, b_blk)
    def late_plan(carrier):
        return _gather_plan([late_shards[n] for n in LATE_GATHER[carrier]])

    def arrived(carrier, got):
        wf.update({n: _from_shard_major(a, SHARDED[n]) for n, a in zip(LATE_GATHER[carrier], got)})

    xs, got = _s5_scan(bu, lam_row, tm, late_plan('s5_scan'))
    arrived('s5_scan', got)
    ypre = _s5_contract("s5_y", xs, c_blk)
    s5_par = [_row(ws['s5_d']), wf['s5_glu_w'], _row(ws['s5_glu_b'])]
    (s5_out,) = _tok_fwd("s5_post", _f_s5_post, [ypre, u], s5_par, [(S5_WIDTH, BF16)], tw)

    pre_par = [_row(ws['rw_shift_mu']), _row(ws['rw_w0']), wf['rw_w2'], _row(ws['rw_a0']), wf['rw_a2'],
               wf['rw_g2'], _row(ws['rw_k_k']), _row(ws['rw_k_a'])]
    r, lw, kp, v, an, bn, gate = _rw_pre_fwd(z, pre_par, tm)
    seqs = [r, lw, kp, v, an, bn]
    y_wkv, ck, got = _wkv_fwd(seqs, late_plan('wkv_fwd'))
    arrived('wkv_fwd', got)
    post_par = [_row(ws['rw_ln_w']), _row(ws['rw_ln_b']), _row(ws['rw_r_k'])]
    post_toks = [y_wkv, r, kp, v, gate]
    (rw_out,) = _tok_fwd("rw_post", _f_rw_post, post_toks, post_par, [(RWKV_WIDTH, BF16)], tw)

    mixcat = jnp.concatenate([s5_out, rw_out], axis=1)
    mixed = _mm("mix_out", mixcat, wf['w_out'], 'nn')
    h1, hn = _tok_fwd("mix_res", _f_mix_res, [x, mixed], [norm_ffn], [(x.shape[1], F32), (x.shape[1], BF16)], tw)
    w13 = jnp.concatenate([wf['ffn_w1'], wf['ffn_w3']], axis=1)
    a13, got = _mm("ffn_up", hn, w13, 'nn', out_dtype=BF16, tn=FFN_TILE, plan=late_plan('ffn_up'))
    arrived('ffn_up', got)
    (f,) = _tok_fwd("ffn_act", _f_ffn_act, [a13], [], [(FFN_HIDDEN, BF16)], tm)
    ffo = _mm("ffn_down", f, wf['ffn_w2'], 'nn')
    h2, hp = _tok_fwd("ffn_res", _f_ffn_res, [h1, ffo], [norm_ple], [(x.shape[1], F32), (x.shape[1], BF16)], tw)
    gpre = _mm("ple_gate", hp, wf['ple_gate_w'], 'nn')
    pu = _mm("ple_up", p, wf['ple_up_w'], 'nn')

    dh2, dgpre, dpu, g['final_norm'], loss = _tok_bwd(
        "loss", _f_loss, [h2, gpre, pu, target], [final_norm], [None],
        [F32, BF16, BF16, None], [True], tw, acc_out=0)
    g['ple_gate_w'] = _mm("d_ple_gate_w", hp, dgpre, 'tn', out_dtype=WIRE)
    g['ple_up_w'] = _mm("d_ple_up_w", p, dpu, 'tn', out_dtype=WIRE)
    dhp = _mm("d_hp", dgpre, wf['ple_gate_w'], 'nt')
    dh1, dffo, g['norm_ple'] = _tok_bwd("ffn_res_bwd", _f_ffn_res, [h1, ffo], [norm_ple], [dh2, dhp],
                                        [F32, BF16], [True], tw)
    g['ffn_w2'] = _mm("d_ffn_w2", f, dffo, 'tn', out_dtype=WIRE, tm=FFN_TILE)
    df = _mm("d_f", dffo, wf['ffn_w2'], 'nt', out_dtype=BF16, tn=FFN_TILE)
    (da13,) = _tok_bwd("ffn_act_bwd", _f_ffn_act, [a13], [], [df], [BF16], [], tm)
    dw13 = _mm("d_ffn_w13", hn, da13, 'tn', out_dtype=WIRE, tn=FFN_TILE)
    dw13 = dw13.reshape(dw13.shape[0], 8, FFN_HIDDEN // 4).transpose(1, 0, 2)
    shard_major = {'ffn_w1': dw13[:4], 'ffn_w3': dw13[4:]}
    dhn = _mm("d_hn", da13, w13, 'nt')
    dx_a, dmixed, g['norm_ffn'] = _tok_bwd("mix_res_bwd", _f_mix_res, [x, mixed], [norm_ffn], [dh1, dhn],
                                           [F32, BF16], [True], tw)
    g['w_out'] = _mm("d_w_out", mixcat, dmixed, 'tn', out_dtype=WIRE)
    dmixcat = _mm("d_mixcat", dmixed, wf['w_out'], 'nt')
    ds5_out, drw_out = dmixcat[:, :S5_WIDTH], dmixcat[:, S5_WIDTH:]

    dy_wkv, dr_b, dkp_b, dv_b, dgate, g['rw_ln_w'], g['rw_ln_b'], g['rw_r_k'] = _tok_bwd(
        "rw_post_bwd", _f_rw_post, post_toks, post_par, [drw_out], [F32] * 5, [True] * 3, tw)
    late_exchange = _exchange_plan([shard_major[n] if n in shard_major else
                                    _to_shard_major(g[n], SHARDED[n]).astype(WIRE) for n in LATE_NAMES])
    dseqs, late_pieces = _wkv_bwd(seqs, ck, dy_wkv, late_exchange)
    pre_cots = [(dseqs[0], dr_b), (dseqs[1],), (dseqs[2], dkp_b), (dseqs[3], dv_b), (dseqs[4],), (dseqs[5],),
                (dgate,)]
    dz, *dpre = _rw_pre_bwd(z, pre_par, pre_cots, tm)
    for n, d in zip(['rw_shift_mu', 'rw_w0', 'rw_w2', 'rw_a0', 'rw_a2', 'rw_g2', 'rw_k_k', 'rw_k_a'], dpre):
        g[n] = d

    dypre, du_a, g['s5_d'], g['s5_glu_w'], g['s5_glu_b'] = _tok_bwd(
        "s5_post_bwd", _f_s5_post, [ypre, u], s5_par, [ds5_out], [F32, F32], [True] * 3, tw)
    dxs = _s5_expand("d_s5_x", dypre, c_blk)
    dc_blk = _s5_block_grad("d_s5_c", dypre, xs)
    dbu, dlam_row = _s5_scan_bwd(dxs, xs, lam_row, tm)
    du_b = _s5_contract("d_s5_u", dbu, b_blk)
    db_blk = _s5_block_grad("d_s5_b", u, dbu)
    dbuild = _small_bwd("s5_build_bwd", _f_s5_build, build_ins, (db_blk, dc_blk))
    lam_cots = (dlam_row[:, :S5_LANES].reshape(gp), dlam_row[:, S5_LANES:].reshape(gp),
                dbuild[0].reshape(gp), dbuild[1].reshape(gp))
    g['s5_lam_re'], g['s5_lam_im'], g['s5_log_step'] = _small_bwd("s5_lam_bwd", _f_s5_lam, lam_ins, lam_cots)
    from_t = lambda a, perm: a.reshape(S5_GROUP, S5_GROUPS, S5_STATE).transpose(perm)
    g['s5_b_re'], g['s5_b_im'] = from_t(dbuild[2], (1, 2, 0)), from_t(dbuild[3], (1, 2, 0))
    g['s5_c_re'], g['s5_c_im'] = from_t(dbuild[4], (1, 0, 2)), from_t(dbuild[5], (1, 0, 2))

    dproj = jnp.concatenate([(du_a + du_b).astype(BF16), dz.astype(BF16)], axis=1)
    g['w_in'] = _mm("d_w_in", xn, dproj, 'tn', out_dtype=WIRE)
    dxn = _mm("d_xn", dproj, wf['w_in'], 'nt')
    grad_x, g['norm_mix'] = _tok_bwd("norm_in_bwd", _f_norm_in, [x], [norm_mix], [dxn], [F32], [True], tw,
                                     add_to=(0, dx_a))
    return loss[0, 0], grad_x, g, late_pieces


def _step(x, p, target, w, m, v):
    shards = {n: _mat(w[n]).astype(BF16) for n in SHARDED_NAMES}
    early = _run_plan("gather_early", _gather_plan([shards[n] for n in EARLY_NAMES]))
    wf = {n: _from_shard_major(a, SHARDED[n]) for n, a in zip(EARLY_NAMES, early)}
    ws = {n: w[n] for n in SMALL_NAMES}

    loss, grad_x, g, late_pieces = _local_step(x[0], p[0, 0], target[0], wf, ws, shards)

    early_plan = _exchange_plan([_to_shard_major(g[n], SHARDED[n]).astype(WIRE) for n in EARLY_NAMES],
                                _pack_small({n: g[n] for n in SMALL_NAMES}, tail=loss))
    *early_pieces, by_dev = _run_plan("exchange_early", early_plan)
    pieces = dict(zip(LATE_NAMES + EARLY_NAMES, list(late_pieces) + early_pieces))
    halves = [_ew("add_devices_" + n, _sum_slots, [pieces[n]], [pieces[n].shape[1:]])[0] for n in SHARDED_NAMES]
    (small_piece,) = _ew("add_devices_small", _sum_slots, [by_dev], [by_dev.shape[1:]])
    both, small_g = _share_cores(halves, small_piece)

    kinds = [{}, {}, {}, {}]
    for n, gn in zip(SHARDED_NAMES, both):
        shard = _mat(w[n]).shape
        res = _ew("adamw_" + n, _adamw, [gn.reshape(shard), _mat(w[n]), _mat(m[n]), _mat(v[n])], [shard] * 3)
        for kind, a in zip(kinds, [gn] + list(res)):
            kind[n] = a.reshape(w[n].shape)
    flat = (8 * SMALL_ROWS, PACK_COLS)
    packed = [_pack_small({n: d[n] for n in SMALL_NAMES}).reshape(flat) for d in (w, m, v)]
    small_res = _ew("adamw_small", _adamw, [small_g.reshape(flat)] + packed, [flat] * 3)
    small_shapes = {n: w[n].shape for n in SMALL_NAMES}
    for kind, a in zip(kinds, [small_g] + list(small_res)):
        kind.update(_unpack_small(a, small_shapes))
    total = small_g.reshape(-1)[sum(math.prod(s) for s in small_shapes.values())]
    return (total, grad_x[None], *[kind[n] for kind in kinds for n in WEIGHT_NAMES])


def kernel(x, p, norm_mix, w_in, s5_lam_re, s5_lam_im, s5_log_step, s5_b_re, s5_b_im, s5_c_re, s5_c_im, s5_d, s5_glu_w, s5_glu_b, rw_shift_mu, rw_w0, rw_w2, rw_a0, rw_a2, rw_g2, rw_k_k, rw_k_a, rw_r_k, rw_ln_w, rw_ln_b, w_out, norm_ffn, ffn_w1, ffn_w3, ffn_w2, norm_ple, ple_gate_w, ple_up_w, final_norm, loss_target, m_norm_mix, m_w_in, m_s5_lam_re, m_s5_lam_im, m_s5_log_step, m_s5_b_re, m_s5_b_im, m_s5_c_re, m_s5_c_im, m_s5_d, m_s5_glu_w, m_s5_glu_b, m_rw_shift_mu, m_rw_w0, m_rw_w2, m_rw_a0, m_rw_a2, m_rw_g2, m_rw_k_k, m_rw_k_a, m_rw_r_k, m_rw_ln_w, m_rw_ln_b, m_w_out, m_norm_ffn, m_ffn_w1, m_ffn_w3, m_ffn_w2, m_norm_ple, m_ple_gate_w, m_ple_up_w, m_final_norm, v_norm_mix, v_w_in, v_s5_lam_re, v_s5_lam_im, v_s5_log_step, v_s5_b_re, v_s5_b_im, v_s5_c_re, v_s5_c_im, v_s5_d, v_s5_glu_w, v_s5_glu_b, v_rw_shift_mu, v_rw_w0, v_rw_w2, v_rw_a0, v_rw_a2, v_rw_g2, v_rw_k_k, v_rw_k_a, v_rw_r_k, v_rw_ln_w, v_rw_ln_b, v_w_out, v_norm_ffn, v_ffn_w1, v_ffn_w3, v_ffn_w2, v_norm_ple, v_ple_gate_w, v_ple_up_w, v_final_norm):
    args = dict(locals())
    w = {n: args[n] for n in WEIGHT_NAMES}
    m = {n: args["m_" + n] for n in WEIGHT_NAMES}
    v = {n: args["v_" + n] for n in WEIGHT_NAMES}
    return _step(x, p, loss_target, w, m, v)
```

```python
import functools
import math
from typing import Any, Callable, NamedTuple, Sequence

import jax
import jax.numpy as jnp
from jax import lax
from jax.experimental import pallas as pl
from jax.experimental.pallas import tpu as pltpu

F32 = jnp.float32
BF16 = jnp.bfloat16
MESH = pl.DeviceIdType.MESH

S5_WIDTH = 512
RWKV_WIDTH = 512
S5_GROUP = 16
S5_GROUPS = 32
S5_STATE = 64
S5_LANES = S5_GROUPS * S5_STATE
S5_TILE_GROUPS = 8
S5_TILES = S5_GROUPS // S5_TILE_GROUPS
S5_TILE_CH = S5_TILE_GROUPS * S5_GROUP
S5_TILE_LANES = S5_TILE_GROUPS * S5_STATE
HEAD = 64
HEADS = 8
DECAY_LORA = 64
AAA_LORA = 64
GATE_LORA = 128
FFN_HIDDEN = 2816
FFN_TILE = FFN_HIDDEN // 2
RMS_EPS = 1e-6
GN_EPS = 64e-5
L2_EPS = 1e-12
ADAM_LR = 0.001
ADAM_B1 = 0.9
ADAM_B2 = 0.999
ADAM_EPS = 1e-08
ADAM_WD = 0.01
ADAM_STEP = 10

WKV_CHUNK = 64
SCAN_UNROLL = 4
WIRE = jnp.bfloat16
WKV_PASSES = 1
VMEM_LIMIT_BYTES = 48 * 1024 * 1024
LANE = 128
PACK_COLS = 1024
SMALL_ROWS = 24

WEIGHT_NAMES = ['norm_mix', 'w_in', 's5_lam_re', 's5_lam_im', 's5_log_step', 's5_b_re', 's5_b_im', 's5_c_re',
                's5_c_im', 's5_d', 's5_glu_w', 's5_glu_b', 'rw_shift_mu', 'rw_w0', 'rw_w2', 'rw_a0', 'rw_a2',
                'rw_g2', 'rw_k_k', 'rw_k_a', 'rw_r_k', 'rw_ln_w', 'rw_ln_b', 'w_out', 'norm_ffn', 'ffn_w1',
                'ffn_w3', 'ffn_w2', 'norm_ple', 'ple_gate_w', 'ple_up_w', 'final_norm']
SHARDED = {'w_in': 1, 's5_glu_w': 0, 'rw_w2': 1, 'rw_a2': 1, 'rw_g2': 1, 'w_out': 0, 'ffn_w1': 1, 'ffn_w3': 1,
           'ffn_w2': 0, 'ple_gate_w': 0, 'ple_up_w': 1}
SHARDED_NAMES = [n for n in WEIGHT_NAMES if n in SHARDED]
LATE_NAMES = ['w_out', 'ffn_w1', 'ffn_w3', 'ffn_w2', 'ple_gate_w', 'ple_up_w']
EARLY_NAMES = [n for n in SHARDED_NAMES if n not in LATE_NAMES]
LATE_GATHER = {'s5_scan': ['w_out', 'ple_gate_w', 'ple_up_w'], 'wkv_fwd': ['ffn_w1', 'ffn_w3'], 'ffn_up': ['ffn_w2']}
SMALL_NAMES = [n for n in WEIGHT_NAMES if n not in SHARDED]


def _params(sem=None):
    return pltpu.CompilerParams(dimension_semantics=sem, vmem_limit_bytes=VMEM_LIMIT_BYTES)


def _tile(n, target):
    best = None
    for d in range(LANE, min(n, target) + 1, LANE):
        if n % d == 0:
            best = d
    return n if best is None else best


_NN = (((1,), (0,)), ((), ()))
_NT = (((1,), (1,)), ((), ()))
_TN = (((0,), (0,)), ((), ()))


def _split(a):
    a = a.astype(F32)
    hi = a.astype(BF16)
    return hi, (a - hi.astype(F32)).astype(BF16)


def _dg(a, b, dims, passes):
    dg = lambda p, q: lax.dot_general(p, q, dims, preferred_element_type=F32)
    if passes == 1:
        return dg(a.astype(BF16), b.astype(BF16))
    bh, bl = _split(b)
    if passes == 2:
        return dg(a.astype(BF16), bh) + dg(a.astype(BF16), bl)
    ah, al = _split(a)
    return dg(ah, bh) + (dg(ah, bl) + dg(al, bh))


_DOT_BWD = {_NN: (("g", "b", _NT), ("a", "g", _TN)),
            _NT: (("g", "b", _NN), ("g", "a", _TN)),
            _TN: (("b", "g", _NT), ("a", "g", _NN))}


@functools.partial(jax.custom_vjp, nondiff_argnums=(2, 3))
def _dot(a, b, dims, passes):
    return _dg(a, b, dims, passes)


def _dot_fwd(a, b, dims, passes):
    return _dg(a, b, dims, passes), (a, b)


def _dot_bwd(dims, passes, res, g):
    env = {"a": res[0], "b": res[1], "g": g}
    return tuple(_dg(env[p], env[q], d, passes) for p, q, d in _DOT_BWD[dims])


_dot.defvjp(_dot_fwd, _dot_bwd)


def _bdot(x, w):
    return _dot(x, w, _NN, 1)


@jax.custom_vjp
def _shift_down(z):
    return pltpu.roll(z, 1, 0)


def _shift_down_fwd(z):
    return pltpu.roll(z, 1, 0), None


def _shift_down_bwd(_, g):
    return (pltpu.roll(g, g.shape[0] - 1, 0),)


_shift_down.defvjp(_shift_down_fwd, _shift_down_bwd)


def _head_sum_impl(x):
    r = lax.broadcasted_iota(jnp.int32, (LANE, LANE), 0) // HEAD
    c = lax.broadcasted_iota(jnp.int32, (LANE, LANE), 1) // HEAD
    ones = (r == c).astype(BF16)
    hi, lo = _split(x)
    dg = lambda p: lax.dot_general(p, ones, _NN, preferred_element_type=F32)
    tiles = [slice(j, j + LANE) for j in range(0, x.shape[1], LANE)]
    return jnp.concatenate([dg(hi[:, s]) + dg(lo[:, s]) for s in tiles], axis=1)


@jax.custom_vjp
def _head_sum(x):
    return _head_sum_impl(x)


_head_sum.defvjp(lambda x: (_head_sum_impl(x), None), lambda _, g: (_head_sum_impl(g),))


def _mm(name, a, b, mode, out_dtype=F32, precise=False, tm=1024, tn=1024, tk=1536, plan=None):
    if mode == 'nn':
        (m, k), (_, n) = a.shape, b.shape
    elif mode == 'nt':
        (m, k), (n, _) = a.shape, b.shape
    else:
        (k, m), (_, n) = a.shape, b.shape
    tm, tn, tk = _tile(m, tm), _tile(n, tn), _tile(k, tk)
    nm, nn, nk = m // tm, n // tn, k // tk
    dims = {'nn': _NN, 'nt': _NT, 'tn': _TN}[mode]
    plan = _NO_PLAN if plan is None else plan
    parts, plan_in_specs, plan_out_shape, plan_out_specs, plan_sems = _carry(plan, 2, 1)

    def body(*refs):
        a_ref, b_ref, o_ref = refs[0], refs[1], refs[2 + len(plan.ins)]
        acc_ref = refs[3 + len(plan.ins) + len(plan.out_shape)]
        i, j, kk = pl.program_id(0), pl.program_id(1), pl.program_id(2)

        if plan is not _NO_PLAN:
            pl.when((i == 0) & (j == 0) & (kk == 0))(lambda: plan.start(*parts(refs)))

        @pl.when(kk == 0)
        def _():
            acc_ref[...] = jnp.zeros_like(acc_ref)

        acc_ref[...] += _dg(a_ref[...], b_ref[...], dims, 3 if precise else 1)

        @pl.when(kk == nk - 1)
        def _():
            o_ref[...] = acc_ref[...].astype(o_ref.dtype)

        if plan is not _NO_PLAN:
            pl.when((i == nm - 1) & (j == nn - 1) & (kk == nk - 1))(lambda: plan.wait(*parts(refs)))

    if mode == 'tn':
        a_spec = pl.BlockSpec((tk, tm), lambda i, j, l: (l, i))
    else:
        a_spec = pl.BlockSpec((tm, tk), lambda i, j, l: (i, l))
    if mode == 'nt':
        b_spec = pl.BlockSpec((tn, tk), lambda i, j, l: (j, l))
    else:
        b_spec = pl.BlockSpec((tk, tn), lambda i, j, l: (l, j))
    res = pl.pallas_call(
        body, name=name,
        out_shape=[jax.ShapeDtypeStruct((m, n), out_dtype)] + plan_out_shape,
        grid=(nm, nn, nk),
        in_specs=[a_spec, b_spec] + plan_in_specs,
        out_specs=[pl.BlockSpec((tm, tn), lambda i, j, l: (i, j))] + plan_out_specs,
        scratch_shapes=[pltpu.VMEM((tm, tn), F32)] + plan_sems,
        compiler_params=_params(("parallel", "parallel", "arbitrary") if plan is _NO_PLAN else ("arbitrary",) * 3),
    )(a, b, *plan.ins)
    return res[0] if plan is _NO_PLAN else (res[0], res[1:])


def _mm_tiles(name, a, b, mode, out_shape, grid, a_spec, b_spec, o_spec):
    dims = {'nn': _NN, 'nt': _NT, 'tn': _TN}[mode]
    nk = grid[2]

    def body(a_ref, b_ref, o_ref, acc_ref):
        kk = pl.program_id(2)

        @pl.when(kk == 0)
        def _():
            acc_ref[...] = jnp.zeros_like(acc_ref)

        acc_ref[...] += _dg(a_ref[...], b_ref[...], dims, 1)

        @pl.when(kk == nk - 1)
        def _():
            o_ref[...] = acc_ref[...]

    return pl.pallas_call(
        body, name=name,
        out_shape=jax.ShapeDtypeStruct(out_shape, F32),
        grid=grid, in_specs=[a_spec, b_spec], out_specs=o_spec,
        scratch_shapes=[pltpu.VMEM(o_spec.block_shape, F32)],
        compiler_params=_params(("parallel", "parallel", "arbitrary")),
    )(a, b)


def _s5_expand(name, u, blk, tm=2048):
    t = u.shape[0]
    tm = min(tm, t)
    ch, ln, nt = S5_TILE_CH, S5_TILE_LANES, S5_TILES
    return _mm_tiles(name, u, blk, 'nn', (t, 2 * S5_LANES), (t // tm, 2 * nt, 1),
                     pl.BlockSpec((tm, ch), lambda i, j, l: (i, j % nt)),
                     pl.BlockSpec((ch, ln), lambda i, j, l: (j % nt, j // nt)),
                     pl.BlockSpec((tm, ln), lambda i, j, l: (i, j)))


def _s5_contract(name, x, blk, tm=2048):
    t = x.shape[0]
    tm = min(tm, t)
    ch, ln, nt = S5_TILE_CH, S5_TILE_LANES, S5_TILES
    return _mm_tiles(name, x, blk, 'nt', (t, S5_WIDTH), (t // tm, nt, 2),
                     pl.BlockSpec((tm, ln), lambda i, j, l: (i, j + nt * l)),
                     pl.BlockSpec((ch, ln), lambda i, j, l: (j, l)),
                     pl.BlockSpec((tm, ch), lambda i, j, l: (i, j)))


def _s5_block_grad(name, u, x, tk=2048):
    t = u.shape[0]
    tk = min(tk, t)
    ch, ln, nt = S5_TILE_CH, S5_TILE_LANES, S5_TILES
    return _mm_tiles(name, u, x, 'tn', (S5_WIDTH, 2 * ln), (nt, 2, t // tk),
                     pl.BlockSpec((tk, ch), lambda i, j, l: (l, i)),
                     pl.BlockSpec((tk, ln), lambda i, j, l: (l, i + nt * j)),
                     pl.BlockSpec((ch, ln), lambda i, j, l: (i, j)))


def _full_spec(p):
    nd = p.ndim
    return pl.BlockSpec(p.shape, lambda i, nd=nd: (0,) * nd)


def _tok_fwd(name, fn, toks, params, outs, tm):
    t = toks[0].shape[0]
    nt, npar = len(toks), len(params)

    def body(*refs):
        tv = [r[...].astype(F32) for r in refs[:nt]]
        pv = [r[...].astype(F32) for r in refs[nt:nt + npar]]
        res = fn(*tv, *pv)
        for r, v in zip(refs[nt + npar:], res):
            r[...] = v.astype(r.dtype)

    return pl.pallas_call(
        body, name=name,
        out_shape=[jax.ShapeDtypeStruct((t, w), d) for w, d in outs],
        grid=(t // tm,),
        in_specs=[pl.BlockSpec((tm, a.shape[1]), lambda i: (i, 0)) for a in toks] + [_full_spec(p) for p in params],
        out_specs=[pl.BlockSpec((tm, w), lambda i: (i, 0)) for w, _ in outs],
        compiler_params=_params(("parallel",)),
    )(*toks, *params)


def _tok_bwd(name, fn, toks, params, cots, dtok, dpar, tm, acc_out=None, add_to=None):
    t = toks[0].shape[0]
    nt, npar = len(toks), len(params)
    cot_arrays = [c for c in cots if c is not None]
    ncot = len(cot_arrays)
    extra = [] if add_to is None else [add_to[1]]
    dtok_idx = [i for i, d in enumerate(dtok) if d is not None]
    dpar_idx = [i for i, d in enumerate(dpar) if d]

    def body(*refs):
        pos = 0
        tin = refs[pos:pos + nt]; pos += nt
        pin = refs[pos:pos + npar]; pos += npar
        cin = refs[pos:pos + ncot]; pos += ncot
        ein = refs[pos:pos + len(extra)]; pos += len(extra)
        dto = refs[pos:pos + len(dtok_idx)]; pos += len(dtok_idx)
        dpo = refs[pos:pos + len(dpar_idx)]; pos += len(dpar_idx)
        acc = refs[pos] if acc_out is not None else None
        first = pl.program_id(0) == 0

        tv = [r[...].astype(F32) for r in tin]
        pv = [r[...].astype(F32) for r in pin]
        res, vjp = jax.vjp(fn, *tv, *pv)
        cit = iter(cin)
        cs = tuple(jnp.ones_like(o) if c is None else next(cit)[...].astype(F32) for c, o in zip(cots, res))
        g = vjp(cs)
        for r, i in zip(dto, dtok_idx):
            v = g[i]
            if add_to is not None and add_to[0] == i:
                v = v + ein[0][...].astype(F32)
            r[...] = v.astype(r.dtype)

        @pl.when(first)
        def _():
            for r in dpo:
                r[...] = jnp.zeros_like(r)
            if acc is not None:
                acc[...] = jnp.zeros_like(acc)

        for r, i in zip(dpo, dpar_idx):
            r[...] += g[nt + i]
        if acc is not None:
            acc[...] += res[acc_out]

    out_shape = [jax.ShapeDtypeStruct(toks[i].shape, dtok[i]) for i in dtok_idx]
    out_shape += [jax.ShapeDtypeStruct(params[i].shape, F32) for i in dpar_idx]
    out_specs = [pl.BlockSpec((tm, toks[i].shape[1]), lambda i_: (i_, 0)) for i in dtok_idx]
    out_specs += [_full_spec(params[i]) for i in dpar_idx]
    if acc_out is not None:
        out_shape.append(jax.ShapeDtypeStruct((1, 1), F32))
        out_specs.append(pl.BlockSpec((1, 1), lambda i_: (0, 0)))
    tok_spec = lambda a: pl.BlockSpec((tm, a.shape[1]), lambda i_: (i_, 0))
    return pl.pallas_call(
        body, name=name,
        out_shape=out_shape,
        grid=(t // tm,),
        in_specs=[tok_spec(a) for a in toks] + [_full_spec(p) for p in params]
        + [tok_spec(c) for c in cot_arrays] + [tok_spec(e) for e in extra],
        out_specs=out_specs,
        compiler_params=_params(("arbitrary",)),
    )(*toks, *params, *cot_arrays, *extra)


def _small_fwd(name, fn, ins, outs):
    n = len(ins)

    def body(*refs):
        res = fn(*[r[...] for r in refs[:n]])
        for r, v in zip(refs[n:], res):
            r[...] = v.astype(r.dtype)

    return pl.pallas_call(
        body, name=name,
        out_shape=[jax.ShapeDtypeStruct(s, d) for s, d in outs],
        compiler_params=_params(),
    )(*ins)


def _small_bwd(name, fn, ins, cots):
    n = len(ins)

    def body(*refs):
        _, vjp = jax.vjp(fn, *[r[...] for r in refs[:n]])
        g = vjp(tuple(r[...] for r in refs[n:n + len(cots)]))
        for r, v in zip(refs[n + len(cots):], g):
            r[...] = v

    return pl.pallas_call(
        body, name=name,
        out_shape=[jax.ShapeDtypeStruct(a.shape, F32) for a in ins],
        compiler_params=_params(),
    )(*ins, *cots)


def _rms(x, g):
    return x * lax.rsqrt(jnp.mean(x * x, axis=-1, keepdims=True) + RMS_EPS) * g


def _f_norm_in(x, g):
    return (_rms(x, g),)


def _f_mix_res(x, mixed, g):
    h1 = x + mixed
    return h1, _rms(h1, g)


def _f_ffn_act(a13):
    a1, a3 = a13[:, :FFN_HIDDEN], a13[:, FFN_HIDDEN:]
    return (jax.nn.silu(a1) * a3,)


def _f_ffn_res(h1, ffo, g):
    h2 = h1 + ffo
    return h2, _rms(h2, g)


def _f_loss(h2, gpre, pu, target, g):
    h3 = h2 + jax.nn.sigmoid(gpre) * pu
    y = _rms(h3, g)
    err = jnp.square(y - target)
    return (0.5 * jnp.sum(jnp.mean(err, axis=-1, keepdims=True), axis=0, keepdims=True),)


def _f_s5_post(ypre, u, d, glu_w, glu_b):
    z = jax.nn.gelu(ypre + u * d)
    return (z * jax.nn.sigmoid(_bdot(z, glu_w) + glu_b),)


def _softplus(x):
    return jnp.maximum(x, 0.0) + jnp.log(1.0 + jnp.exp(-jnp.abs(x)))


def _f_rw_pre(z, carry, shift_mu, w0, w2, a0, a2, g2, k_k, k_a):
    rw = RWKV_WIDTH
    first_row = lax.broadcasted_iota(jnp.int32, z.shape, 0) == 0
    prev = jnp.where(first_row, carry, _shift_down(z))
    zs = z + (prev - z) * shift_mu
    o1, o2 = 3 * rw + DECAY_LORA, 3 * rw + DECAY_LORA + AAA_LORA
    r, k, v = zs[:, :rw], zs[:, rw:2 * rw], zs[:, 2 * rw:3 * rw]
    wl, al, gl = zs[:, 3 * rw:o1], zs[:, o1:o2], zs[:, o2:]
    w = -_softplus(-(w0 + _bdot(jnp.tanh(wl), w2))) - 0.5
    log_decay = -jnp.exp(w)
    a = jax.nn.sigmoid(a0 + _bdot(al, a2))
    g = _bdot(jax.nn.sigmoid(gl), g2)
    kk = k * k_k
    norm = jnp.sqrt(_head_sum(kk * kk))
    kk = kk / jnp.maximum(norm, L2_EPS)
    kp = k * (1.0 + (a - 1.0) * k_a)
    return r, log_decay, kp, v, -kk, kk * a, g


def _f_rw_post(y, r, kp, v, g, ln_w, ln_b, r_k):
    yc = y - _head_sum(y) * (1.0 / HEAD)
    var = _head_sum(yc * yc) * (1.0 / HEAD)
    yn = yc * lax.rsqrt(var + GN_EPS) * ln_w + ln_b
    bonus = _head_sum(r * kp * r_k) * v
    return ((yn + bonus) * g,)


def _f_s5_lam(lam_re, lam_im, log_step):
    step = jnp.exp(log_step)
    dr, di = lam_re * step, lam_im * step
    e = jnp.exp(dr)
    lbr, lbi = e * jnp.cos(di), e * jnp.sin(di)
    nr, ni = lbr - 1.0, lbi
    den = lam_re * lam_re + lam_im * lam_im
    return lbr, lbi, (nr * lam_re + ni * lam_im) / den, (ni * lam_re - nr * lam_im) / den


def _f_s5_build(coef_r, coef_i, btr, bti, ctr, cti):
    bbr = coef_r * btr - coef_i * bti
    bbi = coef_r * bti + coef_i * btr
    shape = (S5_WIDTH, S5_TILE_LANES)
    rows = (lax.broadcasted_iota(jnp.int32, shape, 0) % S5_TILE_CH) // S5_GROUP
    cols = lax.broadcasted_iota(jnp.int32, shape, 1) // S5_STATE
    mask = (rows == cols).astype(F32)

    def blocks(m):
        per_tile = [m[:, S5_TILE_LANES * i:S5_TILE_LANES * (i + 1)] for i in range(S5_TILES)]
        return jnp.concatenate([t for t in per_tile for _ in range(S5_TILE_GROUPS)], axis=0) * mask

    return (jnp.concatenate([blocks(bbr), blocks(bbi)], axis=1),
            jnp.concatenate([blocks(ctr), -blocks(cti)], axis=1))


HALO = 8


def _rw_pre_specs(z, params, tm, order):
    halo_blocks = tm // HALO
    return ([pl.BlockSpec((tm, z.shape[1]), lambda i: (order(i), 0)),
             pl.BlockSpec((HALO, z.shape[1]), lambda i: (jnp.maximum(order(i) * halo_blocks - 1, 0), 0))]
            + [_full_spec(p) for p in params])


def _rw_pre_fwd(z, params, tm):
    t = z.shape[0]
    npar = len(params)

    def body(z_ref, halo_ref, *refs):
        carry = jnp.where(pl.program_id(0) == 0, 0.0, halo_ref[pl.ds(HALO - 1, 1), :])
        res = _f_rw_pre(z_ref[...], carry, *[r[...].astype(F32) for r in refs[:npar]])
        for r, v in zip(refs[npar:], res):
            r[...] = v

    return pl.pallas_call(
        body, name="rw_pre",
        out_shape=[jax.ShapeDtypeStruct((t, RWKV_WIDTH), F32)] * 7,
        grid=(t // tm,),
        in_specs=_rw_pre_specs(z, params, tm, lambda i: i),
        out_specs=[pl.BlockSpec((tm, RWKV_WIDTH), lambda i: (i, 0))] * 7,
        compiler_params=_params(("parallel",)),
    )(z, z, *params)


def _rw_pre_bwd(z, params, cots, tm):
    t = z.shape[0]
    nt = t // tm
    npar = len(params)
    order = lambda i: nt - 1 - i
    flat_cots = [a for group in cots for a in group]
    ncot = len(flat_cots)

    def body(z_ref, halo_ref, *refs):
        pin, cin = refs[:npar], list(refs[npar:npar + ncot])
        dz_ref = refs[npar + ncot]
        dpo = refs[npar + ncot + 1:npar + ncot + 1 + npar]
        dcarry_ref = refs[npar + ncot + 1 + npar]
        i = pl.program_id(0)

        @pl.when(i == 0)
        def _():
            dcarry_ref[...] = jnp.zeros_like(dcarry_ref)
            for r in dpo:
                r[...] = jnp.zeros_like(r)

        carry = jnp.where(i == nt - 1, 0.0, halo_ref[pl.ds(HALO - 1, 1), :])
        _, vjp = jax.vjp(_f_rw_pre, z_ref[...], carry, *[r[...].astype(F32) for r in pin])
        g = vjp(tuple(sum(cin.pop(0)[...] for _ in group) for group in cots))
        last_row = lax.broadcasted_iota(jnp.int32, z_ref.shape, 0) == tm - 1
        dz_ref[...] = g[0] + jnp.where(last_row, dcarry_ref[...], 0.0)
        dcarry_ref[...] = g[1]
        for r, v in zip(dpo, g[2:]):
            r[...] += v

    tok = lambda w: pl.BlockSpec((tm, w), lambda i: (order(i), 0))
    return pl.pallas_call(
        body, name="rw_pre_bwd",
        out_shape=[jax.ShapeDtypeStruct(z.shape, F32)] + [jax.ShapeDtypeStruct(p.shape, F32) for p in params],
        grid=(nt,),
        in_specs=_rw_pre_specs(z, params, tm, order) + [tok(RWKV_WIDTH)] * ncot,
        out_specs=[tok(z.shape[1])] + [_full_spec(p) for p in params],
        scratch_shapes=[pltpu.VMEM((1, z.shape[1]), F32)],
        compiler_params=_params(("arbitrary",)),
    )(z, z, *params, *flat_cots)


def _s5_scan(bu, lam, tm, plan):
    t, w = bu.shape
    h = w // 2
    nt = t // tm
    parts, plan_in_specs, plan_out_shape, plan_out_specs, plan_sems = _carry(plan, 2, 1)

    def body(*refs):
        bu_ref, lam_ref = refs[:2]
        xb_ref = refs[2 + len(plan.ins)]
        carry_ref = refs[3 + len(plan.ins) + len(plan.out_shape)]
        x_ref, refs = refs[-1], refs[:-1]

        @pl.when(pl.program_id(0) == 0)
        def _():
            carry_ref[...] = jnp.zeros_like(carry_ref)
            plan.start(*parts(refs))

        lr, li = lam_ref[:, :h], lam_ref[:, h:]

        def step(s, c):
            cr, ci = c
            row = pl.ds(s, 1)
            nr = lr * cr - li * ci + bu_ref[row, :h]
            ni = lr * ci + li * cr + bu_ref[row, h:]
            x_ref[row, :h] = nr
            x_ref[row, h:] = ni
            return nr, ni

        cr, ci = lax.fori_loop(0, tm, step, (carry_ref[:, :h], carry_ref[:, h:]), unroll=SCAN_UNROLL)
        carry_ref[:, :h] = cr
        carry_ref[:, h:] = ci
        xb_ref[...] = x_ref[...].astype(BF16)

        @pl.when(pl.program_id(0) == nt - 1)
        def _():
            plan.wait(*parts(refs))

    spec = pl.BlockSpec((tm, w), lambda i: (i, 0))
    res = pl.pallas_call(
        body, name="s5_scan",
        out_shape=[jax.ShapeDtypeStruct((t, w), BF16)] + plan_out_shape,
        grid=(nt,),
        in_specs=[spec, pl.BlockSpec((1, w), lambda i: (0, 0))] + plan_in_specs,
        out_specs=[spec] + plan_out_specs,
        scratch_shapes=[pltpu.VMEM((1, w), F32)] + plan_sems + [pltpu.VMEM((tm, w), F32)],
        compiler_params=_params(("arbitrary",)),
    )(bu, lam, *plan.ins)
    return res[0], res[1:]


def _s5_scan_bwd(dx, xb, lam, tm):
    t, w = dx.shape
    h = w // 2
    nt = t // tm
    halo = BF16_SUBLANES

    def body(dx_ref, xb_ref, halo_ref, lam_ref, dbu_out_ref, dlam_ref, carry_ref, dbu_ref):
        @pl.when(pl.program_id(0) == 0)
        def _():
            carry_ref[...] = jnp.zeros_like(carry_ref)
            dlam_ref[...] = jnp.zeros_like(dlam_ref)

        lr, li = lam_ref[:, :h], lam_ref[:, h:]

        def step(s, c):
            cr, ci = c
            row = pl.ds(tm - 1 - s, 1)
            nr = lr * cr + li * ci + dx_ref[row, :h]
            ni = lr * ci - li * cr + dx_ref[row, h:]
            dbu_ref[row, :h] = nr
            dbu_ref[row, h:] = ni
            return nr, ni

        cr, ci = lax.fori_loop(0, tm, step, (carry_ref[:, :h], carry_ref[:, h:]), unroll=SCAN_UNROLL)
        carry_ref[:, :h] = cr
        carry_ref[:, h:] = ci
        halo_rows = lax.broadcasted_iota(jnp.int32, (halo, w), 0)
        before = jnp.sum(jnp.where(halo_rows == halo - 1, halo_ref[...].astype(F32), 0.0), axis=0, keepdims=True)
        before = jnp.where(pl.program_id(0) == nt - 1, 0.0, before)
        first_row = lax.broadcasted_iota(jnp.int32, (tm, w), 0) == 0
        xp = jnp.where(first_row, before, pltpu.roll(xb_ref[...].astype(F32), 1, 0))
        gr, gi = dbu_ref[:, :h], dbu_ref[:, h:]
        pr, pi_ = xp[:, :h], xp[:, h:]
        dlam_ref[:, :h] += jnp.sum(gr * pr + gi * pi_, axis=0, keepdims=True)
        dlam_ref[:, h:] += jnp.sum(gi * pr - gr * pi_, axis=0, keepdims=True)
        dbu_out_ref[...] = dbu_ref[...].astype(BF16)

    spec = pl.BlockSpec((tm, w), lambda i: (nt - 1 - i, 0))
    halo_spec = pl.BlockSpec((halo, w), lambda i: (jnp.maximum((nt - 1 - i) * (tm // halo) - 1, 0), 0))
    row_spec = pl.BlockSpec((1, w), lambda i: (0, 0))
    return pl.pallas_call(
        body, name="s5_scan_bwd",
        out_shape=[jax.ShapeDtypeStruct((t, w), BF16), jax.ShapeDtypeStruct((1, w), F32)],
        grid=(nt,),
        in_specs=[spec, spec, halo_spec, row_spec],
        out_specs=[spec, row_spec],
        scratch_shapes=[pltpu.VMEM((1, w), F32), pltpu.VMEM((tm, w), F32)],
        compiler_params=_params(("arbitrary",)),
    )(dx, xb, xb, lam)


def _unit_lower_inverses_impl(ns):
    c = ns[0].shape[0]
    eye = (lax.broadcasted_iota(jnp.int32, (c, c), 0) == lax.broadcasted_iota(jnp.int32, (c, c), 1)).astype(F32)
    inv = [eye + n for n in ns]
    pw = [_dg(n, n, _NN, WKV_PASSES) for n in ns]
    for _ in range(int(math.log2(c)) - 2):
        both = [_dg(jnp.concatenate([i, q], axis=0), q, _NN, WKV_PASSES) for i, q in zip(inv, pw)]
        inv = [i + q[:c] for i, q in zip(inv, both)]
        pw = [q[c:] for q in both]
    return tuple(i + _dg(i, q, _NN, WKV_PASSES) for i, q in zip(inv, pw))


@jax.custom_vjp
def _unit_lower_inverses(ns):
    return _unit_lower_inverses_impl(ns)


def _unit_lower_inverses_fwd(ns):
    inv = _unit_lower_inverses_impl(ns)
    return inv, inv


def _unit_lower_inverses_bwd(inv, g):
    left = [_dg(i, gi, _TN, WKV_PASSES) for i, gi in zip(inv, g)]
    return (tuple(_dg(q, i, _NT, WKV_PASSES) for q, i in zip(left, inv)),)


_unit_lower_inverses.defvjp(_unit_lower_inverses_fwd, _unit_lower_inverses_bwd)


def _wkv_chunks(s0, r, lw, k, v, a, b):
    c = r[0].shape[0]
    row = lax.broadcasted_iota(jnp.int32, (c, c), 0)
    col = lax.broadcasted_iota(jnp.int32, (c, c), 1)
    incl, strict = col <= row, col < row
    tri = incl.astype(F32)
    each = lambda f, *xs: [f(*t) for t in zip(*xs)]
    stack = lambda p, q: jnp.concatenate([p, q], axis=0)
    dot = lambda p, q, dims=_NN: _dot(p, q, dims, WKV_PASSES)
    lc = each(lambda l: _dot(tri, l, _NN, 2), lw)
    e_neg = each(lambda l: jnp.exp(-l), lc)
    ar = each(lambda x, z, l, w: stack(x * jnp.exp(l - w), z * jnp.exp(l)), a, r, lc, lw)
    bk = each(lambda x, z, e: stack(x * e, z * e), b, k, e_neg)
    m = each(lambda p, q: dot(p, q, _NT), ar, bk)
    mab = each(lambda q: jnp.where(strict, q[:c, :c], 0.0), m)
    mak_mrk = each(lambda q: stack(jnp.where(strict, q[:c, c:], 0.0), jnp.where(incl, q[c:, c:], 0.0)), m)
    mrb = each(lambda q: jnp.where(incl, q[c:, :c], 0.0), m)
    xy = each(lambda p, s, q, z: dot(p, s, _NT) + dot(q, z), ar, s0, mak_mrk, v)
    inv = _unit_lower_inverses(tuple(mab))
    u = each(lambda i, q: dot(i, q[:c]), inv, xy)
    y = each(lambda q, z, p: q[c:] + dot(z, p), xy, mrb, u)
    e_tot = each(lambda l: jnp.exp(jnp.sum(l, axis=0, keepdims=True)), lw)
    s1 = each(lambda s, p, z, q, e: (s + dot(stack(p, z), q, _TN)) * e, s0, u, v, bk, e_tot)
    return y, s1


def _carry(plan, n_args, n_outs):
    n_in, n_out = len(plan.ins), len(plan.out_shape)

    def parts(refs):
        base = n_args + n_in + n_outs
        return refs[n_args:n_args + n_in], refs[base:base + n_out], refs[base + n_out + 1:]

    return parts, [HBM_SPEC] * n_in, list(plan.out_shape), [HBM_SPEC] * n_out, list(plan.sems)


def _head_cols(ref):
    return tuple(ref[:, h * HEAD:(h + 1) * HEAD] for h in range(HEADS))


def _wkv_fwd(seqs, plan):
    t, w = seqs[0].shape
    c, n = WKV_CHUNK, HEAD
    nc = t // c
    parts, plan_in_specs, plan_out_shape, plan_out_specs, plan_sems = _carry(plan, 6, 2)

    def body(*refs):
        ins, (y_ref, ck_ref) = refs[:6], refs[6 + len(plan.ins):8 + len(plan.ins)]
        s_ref = refs[8 + len(plan.ins) + len(plan.out_shape)]

        @pl.when(pl.program_id(0) == 0)
        def _():
            s_ref[...] = jnp.zeros_like(s_ref)
            plan.start(*parts(refs))

        s0 = tuple(s_ref[h] for h in range(HEADS))
        ys, s1 = _wkv_chunks(s0, *[_head_cols(r) for r in ins])
        for h in range(HEADS):
            ck_ref[0, h] = s0[h]
            y_ref[:, h * n:(h + 1) * n] = ys[h]
            s_ref[h] = s1[h]

        @pl.when(pl.program_id(0) == nc - 1)
        def _():
            plan.wait(*parts(refs))

    spec = pl.BlockSpec((c, w), lambda i: (i, 0))
    res = pl.pallas_call(
        body, name="wkv_fwd",
        out_shape=[jax.ShapeDtypeStruct((t, w), F32), jax.ShapeDtypeStruct((nc, HEADS, n, n), F32)] + plan_out_shape,
        grid=(nc,),
        in_specs=[spec] * 6 + plan_in_specs,
        out_specs=[spec, pl.BlockSpec((1, HEADS, n, n), lambda i: (i, 0, 0, 0))] + plan_out_specs,
        scratch_shapes=[pltpu.VMEM((HEADS, n, n), F32)] + plan_sems,
        compiler_params=_params(("arbitrary",)),
    )(*seqs, *plan.ins)
    return res[0], res[1], res[2:]


def _wkv_bwd(seqs, ck, dy, plan):
    t, w = seqs[0].shape
    c, n = WKV_CHUNK, HEAD
    nc = t // c
    parts, plan_in_specs, plan_out_shape, plan_out_specs, plan_sems = _carry(plan, 8, 6)

    def body(*refs):
        ins, ck_ref, dy_ref = refs[:6], refs[6], refs[7]
        outs = refs[8 + len(plan.ins):14 + len(plan.ins)]
        ds_ref = refs[14 + len(plan.ins) + len(plan.out_shape)]

        @pl.when(pl.program_id(0) == 0)
        def _():
            ds_ref[...] = jnp.zeros_like(ds_ref)
            plan.start(*parts(refs))

        s0 = tuple(ck_ref[0, h] for h in range(HEADS))
        _, vjp = jax.vjp(_wkv_chunks, s0, *[_head_cols(r) for r in ins])
        g = vjp((list(_head_cols(dy_ref)), [ds_ref[h] for h in range(HEADS)]))
        for h in range(HEADS):
            ds_ref[h] = g[0][h]
            for o, d in zip(outs, g[1:]):
                o[:, h * n:(h + 1) * n] = d[h]

        @pl.when(pl.program_id(0) == nc - 1)
        def _():
            plan.wait(*parts(refs))

    spec = pl.BlockSpec((c, w), lambda i: (nc - 1 - i, 0))
    res = pl.pallas_call(
        body, name="wkv_bwd",
        out_shape=[jax.ShapeDtypeStruct((t, w), F32)] * 6 + plan_out_shape,
        grid=(nc,),
        in_specs=[spec] * 6 + [pl.BlockSpec((1, HEADS, n, n), lambda i: (nc - 1 - i, 0, 0, 0)), spec] + plan_in_specs,
        out_specs=[spec] * 6 + plan_out_specs,
        scratch_shapes=[pltpu.VMEM((HEADS, n, n), F32)] + plan_sems,
        compiler_params=_params(("arbitrary",)),
    )(*seqs, ck, dy, *plan.ins)
    return res[:6], res[6:]


def _coords():
    return lax.axis_index("x"), lax.axis_index("y"), lax.axis_index("c")


def _flip(v, f):
    return 1 - v if f else v


_CHIP_FLIPS = [(1, 0), (0, 1), (1, 1)]
_DEV_FLIPS = [(fx, fy, fc) for fx in (0, 1) for fy in (0, 1) for fc in (0, 1) if (fx, fy, fc) != (0, 0, 0)]
HBM_SPEC = pl.BlockSpec(memory_space=pl.ANY)


def _chip_peer(k, x, y):
    fx, fy = _CHIP_FLIPS[k]
    return _flip(x, fx), _flip(y, fy)


def _dev_peer(k, x, y, c):
    fx, fy, fc = _DEV_FLIPS[k]
    return _flip(x, fx), _flip(y, fy), _flip(c, fc)


def _rows_of_core(ref, core):
    h = ref.shape[-2] // 2
    rows = pl.ds(pl.multiple_of(core * h, 8), h)
    return ref.at[rows, :] if len(ref.shape) == 2 else ref.at[:, rows, :]


class _Plan(NamedTuple):
    ins: Sequence[Any]
    out_shape: Sequence[Any]
    sems: Sequence[Any]
    start: Callable
    wait: Callable


_NO_PLAN = _Plan([], [], [], lambda *_: None, lambda *_: None)


def _run_plan(name, plan):
    n_in, n_out = len(plan.ins), len(plan.out_shape)

    def body(*refs):
        parts = refs[:n_in], refs[n_in:n_in + n_out], refs[n_in + n_out:]
        plan.start(*parts)
        plan.wait(*parts)

    return pl.pallas_call(
        body, name=name, out_shape=list(plan.out_shape),
        in_specs=[HBM_SPEC] * n_in, out_specs=[HBM_SPEC] * n_out, scratch_shapes=list(plan.sems),
    )(*plan.ins)


def _gather_plan(shards):
    n = len(shards)

    def copies(srcs, outs, sems):
        send_sems, recv_sems, local_sems = sems
        x, y, c = _coords()
        me = 2 * x + y

        def remote(i, k, arriving):
            px, py = _chip_peer(k, x, y)
            return pltpu.make_async_remote_copy(
                src_ref=srcs[i], dst_ref=outs[i].at[2 * px + py if arriving else me],
                send_sem=send_sems.at[i, k], recv_sem=recv_sems.at[i, k],
                device_id=(px, py, c), device_id_type=MESH)

        own = [pltpu.make_async_copy(srcs[i], outs[i].at[me], local_sems.at[i]) for i in range(n)]
        pairs = [(i, k) for k in range(3) for i in range(n)]
        return own, [remote(i, k, False) for i, k in pairs], [remote(i, k, True) for i, k in pairs]

    return _Plan(
        ins=shards, out_shape=[jax.ShapeDtypeStruct((4,) + s.shape, s.dtype) for s in shards],
        sems=[pltpu.SemaphoreType.DMA((n, 3)), pltpu.SemaphoreType.DMA((n, 3)), pltpu.SemaphoreType.DMA((n,))],
        start=functools.partial(_start_copies, copies), wait=functools.partial(_wait_copies, copies))


def _start_copies(copies, ins, outs, sems):
    own, sends, _ = copies(ins, outs, sems)
    for cp in own + sends:
        cp.start()


def _wait_copies(copies, ins, outs, sems):
    own, sends, arrivals = copies(ins, outs, sems)
    for cp in arrivals:
        cp.wait_recv()
    for cp in sends:
        cp.wait_send()
    for cp in own:
        cp.wait()


def _exchange_plan(gs, small=None):
    n = len(gs)
    arrays = list(gs) + ([] if small is None else [small])

    def copies(srcs, outs, sems):
        send_sems, recv_sems, local_sems = sems
        x, y, c = _coords()
        me = 4 * x + 2 * y + c

        def piece(i, px, py, pc):
            if i == n:
                return srcs[i].at[4 * px + 2 * py + pc]
            return _rows_of_core(srcs[i].at[2 * px + py], pc)

        def remote(i, k, arriving):
            px, py, pc = _dev_peer(k, x, y, c)
            return pltpu.make_async_remote_copy(
                src_ref=piece(i, px, py, pc), dst_ref=outs[i].at[4 * px + 2 * py + pc if arriving else me],
                send_sem=send_sems.at[i, k], recv_sem=recv_sems.at[i, k],
                device_id=(px, py, pc), device_id_type=MESH)

        own = [pltpu.make_async_copy(piece(i, x, y, c), outs[i].at[me], local_sems.at[i]) for i in range(len(arrays))]
        pairs = [(i, k) for k in range(7) for i in range(len(arrays))]
        return own, [remote(i, k, False) for i, k in pairs], [remote(i, k, True) for i, k in pairs]

    out_shape = [jax.ShapeDtypeStruct((8, g.shape[1] // 2, g.shape[2]), g.dtype) for g in gs]
    if small is not None:
        out_shape.append(jax.ShapeDtypeStruct(small.shape, small.dtype))
    m = len(arrays)
    return _Plan(
        ins=arrays, out_shape=out_shape,
        sems=[pltpu.SemaphoreType.DMA((m, 7)), pltpu.SemaphoreType.DMA((m, 7)), pltpu.SemaphoreType.DMA((m,))],
        start=functools.partial(_start_copies, copies), wait=functools.partial(_wait_copies, copies))


def _share_cores(halves, small):
    n = len(halves)

    def body(*refs):
        srcs, small_src, outs, small_out = refs[:n], refs[n], refs[n + 1:2 * n + 1], refs[2 * n + 1]
        mine, theirs = refs[2 * n + 2:3 * n + 2], refs[3 * n + 2:4 * n + 2]
        send_sems, recv_sems, ssend, srecv, local_sems = refs[4 * n + 2:]
        x, y, c = _coords()
        me = 4 * x + 2 * y + c

        def big(i):
            return pltpu.make_async_remote_copy(
                src_ref=mine[i], dst_ref=theirs[i], send_sem=send_sems.at[i], recv_sem=recv_sems.at[i],
                device_id=(x, y, 1 - c), device_id_type=MESH)

        def tiny(k, arriving):
            px, py, pc = _dev_peer(k, x, y, c)
            return pltpu.make_async_remote_copy(
                src_ref=small_src, dst_ref=small_out.at[4 * px + 2 * py + pc if arriving else me],
                send_sem=ssend.at[k], recv_sem=srecv.at[k], device_id=(px, py, pc), device_id_type=MESH)

        small_sends = [tiny(k, False) for k in range(7)]
        own_small = pltpu.make_async_copy(small_src, small_out.at[me], local_sems.at[2 * n])
        stage = [pltpu.make_async_copy(srcs[i], mine[i], local_sems.at[i]) for i in range(n)]
        for cp in small_sends + [own_small] + stage:
            cp.start()
        sends = []
        for i in range(n):
            stage[i].wait()
            sends.append(big(i))
            sends[-1].start()
        store = [pltpu.make_async_copy(mine[i], outs[i].at[c], local_sems.at[i]) for i in range(n)]
        for cp in store:
            cp.start()
        for i in range(n):
            big(i).wait_recv()
            store.append(pltpu.make_async_copy(theirs[i], outs[i].at[1 - c], local_sems.at[n + i]))
            store[-1].start()
        for k in range(7):
            tiny(k, True).wait_recv()
        for cp in sends + small_sends:
            cp.wait_send()
        for cp in store + [own_small]:
            cp.wait()

    staged = [pltpu.VMEM(s.shape, s.dtype) for s in halves]
    res = pl.pallas_call(
        body, name="share_cores",
        out_shape=[jax.ShapeDtypeStruct((2,) + s.shape, s.dtype) for s in halves]
        + [jax.ShapeDtypeStruct((8,) + small.shape, small.dtype)],
        in_specs=[HBM_SPEC] * (n + 1), out_specs=[HBM_SPEC] * (n + 1),
        scratch_shapes=staged + staged + [
            pltpu.SemaphoreType.DMA((n,)), pltpu.SemaphoreType.DMA((n,)),
            pltpu.SemaphoreType.DMA((7,)), pltpu.SemaphoreType.DMA((7,)),
            pltpu.SemaphoreType.DMA((2 * n + 1,))],
        compiler_params=pltpu.CompilerParams(vmem_limit_bytes=VMEM_LIMIT_BYTES),
    )(*halves, small)
    return res[:n], res[n]


BF16_SUBLANES = 16


def _row_tile(n, target, step=BF16_SUBLANES):
    return max([d for d in range(step, min(n, target) + 1, step) if n % d == 0] or [n])


def _ew(name, fn, ins, outs, block_bytes=2 << 20):
    rows, cols = ins[0].shape[-2:]
    lead = max(math.prod(a.shape[:-2]) for a in ins)
    tr = _row_tile(rows, max(8, block_bytes // (4 * cols * lead)))
    n = len(ins)

    def spec(shape):
        if len(shape) == 2:
            return pl.BlockSpec((tr, cols), lambda i: (i, 0))
        return pl.BlockSpec((shape[0], tr, cols), lambda i: (0, i, 0))

    def body(*refs):
        res = fn(*[r[...] for r in refs[:n]])
        for r, v in zip(refs[n:], res):
            r[...] = v

    return pl.pallas_call(
        body, name=name,
        out_shape=[jax.ShapeDtypeStruct(s, F32) for s in outs],
        grid=(rows // tr,),
        in_specs=[spec(a.shape) for a in ins],
        out_specs=[spec(s) for s in outs],
        compiler_params=_params(("parallel",)),
    )(*ins)


def _sum_slots(a):
    total = a[0].astype(F32)
    for s in range(1, a.shape[0]):
        total = total + a[s].astype(F32)
    return (total,)


def _adamw(g, w, m, v):
    bc1 = 1.0 - ADAM_B1 ** ADAM_STEP
    bc2 = 1.0 - ADAM_B2 ** ADAM_STEP
    m_new = ADAM_B1 * m + (1.0 - ADAM_B1) * g
    v_new = ADAM_B2 * v + (1.0 - ADAM_B2) * jnp.square(g)
    delta = -ADAM_LR * ((m_new / bc1) / (jnp.sqrt(v_new / bc2) + ADAM_EPS) + ADAM_WD * w)
    return delta, m_new, v_new


def _mat(a):
    return a.reshape(a.shape[-2:])


def _to_shard_major(full, axis):
    rows, cols = full.shape
    if axis == 0:
        return full.reshape(4, rows // 4, cols)
    return full.reshape(rows, 4, cols // 4).transpose(1, 0, 2)


def _from_shard_major(a, axis):
    _, r, cs = a.shape
    if axis == 0:
        return a.reshape(4 * r, cs)
    return a.transpose(1, 0, 2).reshape(r, 4 * cs)


def _pack_small(arrays, tail=None):
    flat = [arrays[n].reshape(-1) for n in SMALL_NAMES] + ([] if tail is None else [tail.reshape(1)])
    used = sum(a.shape[0] for a in flat)
    flat.append(jnp.zeros((8 * SMALL_ROWS * PACK_COLS - used,), F32))
    return jnp.concatenate(flat).reshape(8, SMALL_ROWS, PACK_COLS)


def _unpack_small(packed, shapes):
    flat = packed.reshape(-1)
    out, off = {}, 0
    for n in SMALL_NAMES:
        size = math.prod(shapes[n])
        out[n] = flat[off:off + size].reshape(shapes[n])
        off += size
    return out


def _row(a):
    return a.reshape(1, -1)


def _local_step(x, p, target, wf, ws, late_shards):
    wf = dict(wf)
    t = x.shape[0]
    tm = min(256, t)
    tw = min(512, t)
    g = {}

    lam_re, lam_im = ws['s5_lam_re'].reshape(S5_GROUPS, S5_STATE), ws['s5_lam_im'].reshape(S5_GROUPS, S5_STATE)
    log_step = ws['s5_log_step'].reshape(S5_GROUPS, 1)
    gp = (S5_GROUPS, S5_STATE)
    lam_ins = (lam_re, lam_im, log_step)
    lbr, lbi, cfr, cfi = _small_fwd("s5_lam", _f_s5_lam, lam_ins, [(gp, F32)] * 4)
    lam_row = jnp.concatenate([_row(lbr), _row(lbi)], axis=1)
    to_t = lambda a, perm: a.reshape((S5_GROUPS,) + a.shape[-2:]).transpose(perm).reshape(S5_GROUP, S5_LANES)
    build_ins = (_row(cfr), _row(cfi), to_t(ws['s5_b_re'], (2, 0, 1)), to_t(ws['s5_b_im'], (2, 0, 1)),
                 to_t(ws['s5_c_re'], (1, 0, 2)), to_t(ws['s5_c_im'], (1, 0, 2)))
    block_shape = (S5_WIDTH, 2 * S5_TILE_LANES)
    b_blk, c_blk = _small_fwd("s5_build", _f_s5_build, build_ins, [(block_shape, F32)] * 2)

    norm_mix, norm_ffn, norm_ple = _row(ws['norm_mix']), _row(ws['norm_ffn']), _row(ws['norm_ple'])
    final_norm = _row(ws['final_norm'])
    (xn,) = _tok_fwd("norm_in", _f_norm_in, [x], [norm_mix], [(x.shape[1], BF16)], tw)
    u = _mm("proj_s5", xn, wf['w_in'][:, :S5_WIDTH], 'nn')
    z = _mm("proj_rw", xn, wf['w_in'][:, S5_WIDTH:], 'nn')

    bu = _s5_expand("s5_bu", u, b_blk)
    def late_plan(carrier):
        return _gather_plan([late_shards[n] for n in LATE_GATHER[carrier]])

    def arrived(carrier, got):
        wf.update({n: _from_shard_major(a, SHARDED[n]) for n, a in zip(LATE_GATHER[carrier], got)})

    xs, got = _s5_scan(bu, lam_row, tm, late_plan('s5_scan'))
    arrived('s5_scan', got)
    ypre = _s5_contract("s5_y", xs, c_blk)
    s5_par = [_row(ws['s5_d']), wf['s5_glu_w'], _row(ws['s5_glu_b'])]
    (s5_out,) = _tok_fwd("s5_post", _f_s5_post, [ypre, u], s5_par, [(S5_WIDTH, BF16)], tw)

    pre_par = [_row(ws['rw_shift_mu']), _row(ws['rw_w0']), wf['rw_w2'], _row(ws['rw_a0']), wf['rw_a2'],
               wf['rw_g2'], _row(ws['rw_k_k']), _row(ws['rw_k_a'])]
    r, lw, kp, v, an, bn, gate = _rw_pre_fwd(z, pre_par, tw)
    seqs = [r, lw, kp, v, an, bn]
    y_wkv, ck, got = _wkv_fwd(seqs, late_plan('wkv_fwd'))
    arrived('wkv_fwd', got)
    post_par = [_row(ws['rw_ln_w']), _row(ws['rw_ln_b']), _row(ws['rw_r_k'])]
    post_toks = [y_wkv, r, kp, v, gate]
    (rw_out,) = _tok_fwd("rw_post", _f_rw_post, post_toks, post_par, [(RWKV_WIDTH, BF16)], tw)

    mixcat = jnp.concatenate([s5_out, rw_out], axis=1)
    mixed = _mm("mix_out", mixcat, wf['w_out'], 'nn')
    h1, hn = _tok_fwd("mix_res", _f_mix_res, [x, mixed], [norm_ffn], [(x.shape[1], F32), (x.shape[1], BF16)], tw)
    w13 = jnp.concatenate([wf['ffn_w1'], wf['ffn_w3']], axis=1)
    a13, got = _mm("ffn_up", hn, w13, 'nn', out_dtype=BF16, tn=FFN_TILE, plan=late_plan('ffn_up'))
    arrived('ffn_up', got)
    (f,) = _tok_fwd("ffn_act", _f_ffn_act, [a13], [], [(FFN_HIDDEN, BF16)], tw)
    ffo = _mm("ffn_down", f, wf['ffn_w2'], 'nn')
    h2, hp = _tok_fwd("ffn_res", _f_ffn_res, [h1, ffo], [norm_ple], [(x.shape[1], F32), (x.shape[1], BF16)], tw)
    gpre = _mm("ple_gate", hp, wf['ple_gate_w'], 'nn')
    pu = _mm("ple_up", p, wf['ple_up_w'], 'nn')

    dh2, dgpre, dpu, g['final_norm'], loss = _tok_bwd(
        "loss", _f_loss, [h2, gpre, pu, target], [final_norm], [None],
        [F32, BF16, BF16, None], [True], tw, acc_out=0)
    g['ple_gate_w'] = _mm("d_ple_gate_w", hp, dgpre, 'tn', out_dtype=WIRE)
    g['ple_up_w'] = _mm("d_ple_up_w", p, dpu, 'tn', out_dtype=WIRE)
    dhp = _mm("d_hp", dgpre, wf['ple_gate_w'], 'nt')
    dh1, dffo, g['norm_ple'] = _tok_bwd("ffn_res_bwd", _f_ffn_res, [h1, ffo], [norm_ple], [dh2, dhp],
                                        [F32, BF16], [True], tw)
    g['ffn_w2'] = _mm("d_ffn_w2", f, dffo, 'tn', out_dtype=WIRE, tm=FFN_TILE)
    df = _mm("d_f", dffo, wf['ffn_w2'], 'nt', out_dtype=BF16, tn=FFN_TILE)
    (da13,) = _tok_bwd("ffn_act_bwd", _f_ffn_act, [a13], [], [df], [BF16], [], tw)
    dw13 = _mm("d_ffn_w13", hn, da13, 'tn', out_dtype=WIRE, tn=FFN_TILE)
    dw13 = dw13.reshape(dw13.shape[0], 8, FFN_HIDDEN // 4).transpose(1, 0, 2)
    shard_major = {'ffn_w1': dw13[:4], 'ffn_w3': dw13[4:]}
    dhn = _mm("d_hn", da13, w13, 'nt')
    dx_a, dmixed, g['norm_ffn'] = _tok_bwd("mix_res_bwd", _f_mix_res, [x, mixed], [norm_ffn], [dh1, dhn],
                                           [F32, BF16], [True], tw)
    g['w_out'] = _mm("d_w_out", mixcat, dmixed, 'tn', out_dtype=WIRE)
    dmixcat = _mm("d_mixcat", dmixed, wf['w_out'], 'nt')
    ds5_out, drw_out = dmixcat[:, :S5_WIDTH], dmixcat[:, S5_WIDTH:]

    dy_wkv, dr_b, dkp_b, dv_b, dgate, g['rw_ln_w'], g['rw_ln_b'], g['rw_r_k'] = _tok_bwd(
        "rw_post_bwd", _f_rw_post, post_toks, post_par, [drw_out], [F32] * 5, [True] * 3, tw)
    late_exchange = _exchange_plan([shard_major[n] if n in shard_major else
                                    _to_shard_major(g[n], SHARDED[n]).astype(WIRE) for n in LATE_NAMES])
    dseqs, late_pieces = _wkv_bwd(seqs, ck, dy_wkv, late_exchange)
    pre_cots = [(dseqs[0], dr_b), (dseqs[1],), (dseqs[2], dkp_b), (dseqs[3], dv_b), (dseqs[4],), (dseqs[5],),
                (dgate,)]
    dz, *dpre = _rw_pre_bwd(z, pre_par, pre_cots, tw)
    for n, d in zip(['rw_shift_mu', 'rw_w0', 'rw_w2', 'rw_a0', 'rw_a2', 'rw_g2', 'rw_k_k', 'rw_k_a'], dpre):
        g[n] = d

    dypre, du_a, g['s5_d'], g['s5_glu_w'], g['s5_glu_b'] = _tok_bwd(
        "s5_post_bwd", _f_s5_post, [ypre, u], s5_par, [ds5_out], [F32, F32], [True] * 3, tw)
    dxs = _s5_expand("d_s5_x", dypre, c_blk)
    dc_blk = _s5_block_grad("d_s5_c", dypre, xs)
    dbu, dlam_row = _s5_scan_bwd(dxs, xs, lam_row, tm)
    du_b = _s5_contract("d_s5_u", dbu, b_blk)
    db_blk = _s5_block_grad("d_s5_b", u, dbu)
    dbuild = _small_bwd("s5_build_bwd", _f_s5_build, build_ins, (db_blk, dc_blk))
    lam_cots = (dlam_row[:, :S5_LANES].reshape(gp), dlam_row[:, S5_LANES:].reshape(gp),
                dbuild[0].reshape(gp), dbuild[1].reshape(gp))
    g['s5_lam_re'], g['s5_lam_im'], g['s5_log_step'] = _small_bwd("s5_lam_bwd", _f_s5_lam, lam_ins, lam_cots)
    from_t = lambda a, perm: a.reshape(S5_GROUP, S5_GROUPS, S5_STATE).transpose(perm)
    g['s5_b_re'], g['s5_b_im'] = from_t(dbuild[2], (1, 2, 0)), from_t(dbuild[3], (1, 2, 0))
    g['s5_c_re'], g['s5_c_im'] = from_t(dbuild[4], (1, 0, 2)), from_t(dbuild[5], (1, 0, 2))

    dproj = jnp.concatenate([(du_a + du_b).astype(BF16), dz.astype(BF16)], axis=1)
    g['w_in'] = _mm("d_w_in", xn, dproj, 'tn', out_dtype=WIRE)
    dxn = _mm("d_xn", dproj, wf['w_in'], 'nt')
    grad_x, g['norm_mix'] = _tok_bwd("norm_in_bwd", _f_norm_in, [x], [norm_mix], [dxn], [F32], [True], tw,
                                     add_to=(0, dx_a))
    return loss[0, 0], grad_x, g, late_pieces


def _step(x, p, target, w, m, v):
    shards = {n: _mat(w[n]).astype(BF16) for n in SHARDED_NAMES}
    early = _run_plan("gather_early", _gather_plan([shards[n] for n in EARLY_NAMES]))
    wf = {n: _from_shard_major(a, SHARDED[n]) for n, a in zip(EARLY_NAMES, early)}
    ws = {n: w[n] for n in SMALL_NAMES}

    loss, grad_x, g, late_pieces = _local_step(x[0], p[0, 0], target[0], wf, ws, shards)

    early_plan = _exchange_plan([_to_shard_major(g[n], SHARDED[n]).astype(WIRE) for n in EARLY_NAMES],
                                _pack_small({n: g[n] for n in SMALL_NAMES}, tail=loss))
    *early_pieces, by_dev = _run_plan("exchange_early", early_plan)
    pieces = dict(zip(LATE_NAMES + EARLY_NAMES, list(late_pieces) + early_pieces))
    halves = [_ew("add_devices_" + n, _sum_slots, [pieces[n]], [pieces[n].shape[1:]])[0] for n in SHARDED_NAMES]
    (small_piece,) = _ew("add_devices_small", _sum_slots, [by_dev], [by_dev.shape[1:]])
    both, small_g = _share_cores(halves, small_piece)

    kinds = [{}, {}, {}, {}]
    for n, gn in zip(SHARDED_NAMES, both):
        shard = _mat(w[n]).shape
        res = _ew("adamw_" + n, _adamw, [gn.reshape(shard), _mat(w[n]), _mat(m[n]), _mat(v[n])], [shard] * 3)
        for kind, a in zip(kinds, [gn] + list(res)):
            kind[n] = a.reshape(w[n].shape)
    flat = (8 * SMALL_ROWS, PACK_COLS)
    packed = [_pack_small({n: d[n] for n in SMALL_NAMES}).reshape(flat) for d in (w, m, v)]
    small_res = _ew("adamw_small", _adamw, [small_g.reshape(flat)] + packed, [flat] * 3)
    small_shapes = {n: w[n].shape for n in SMALL_NAMES}
    for kind, a in zip(kinds, [small_g] + list(small_res)):
        kind.update(_unpack_small(a, small_shapes))
    total = small_g.reshape(-1)[sum(math.prod(s) for s in small_shapes.values())]
    return (total, grad_x[None], *[kind[n] for kind in kinds for n in WEIGHT_NAMES])


def kernel(x, p, norm_mix, w_in, s5_lam_re, s5_lam_im, s5_log_step, s5_b_re, s5_b_im, s5_c_re, s5_c_im, s5_d, s5_glu_w, s5_glu_b, rw_shift_mu, rw_w0, rw_w2, rw_a0, rw_a2, rw_g2, rw_k_k, rw_k_a, rw_r_k, rw_ln_w, rw_ln_b, w_out, norm_ffn, ffn_w1, ffn_w3, ffn_w2, norm_ple, ple_gate_w, ple_up_w, final_norm, loss_target, m_norm_mix, m_w_in, m_s5_lam_re, m_s5_lam_im, m_s5_log_step, m_s5_b_re, m_s5_b_im, m_s5_c_re, m_s5_c_im, m_s5_d, m_s5_glu_w, m_s5_glu_b, m_rw_shift_mu, m_rw_w0, m_rw_w2, m_rw_a0, m_rw_a2, m_rw_g2, m_rw_k_k, m_rw_k_a, m_rw_r_k, m_rw_ln_w, m_rw_ln_b, m_w_out, m_norm_ffn, m_ffn_w1, m_ffn_w3, m_ffn_w2, m_norm_ple, m_ple_gate_w, m_ple_up_w, m_final_norm, v_norm_mix, v_w_in, v_s5_lam_re, v_s5_lam_im, v_s5_log_step, v_s5_b_re, v_s5_b_im, v_s5_c_re, v_s5_c_im, v_s5_d, v_s5_glu_w, v_s5_glu_b, v_rw_shift_mu, v_rw_w0, v_rw_w2, v_rw_a0, v_rw_a2, v_rw_g2, v_rw_k_k, v_rw_k_a, v_rw_r_k, v_rw_ln_w, v_rw_ln_b, v_w_out, v_norm_ffn, v_ffn_w1, v_ffn_w3, v_ffn_w2, v_norm_ple, v_ple_gate_w, v_ple_up_w, v_final_norm):
    args = dict(locals())
    w = {n: args[n] for n in WEIGHT_NAMES}
    m = {n: args["m_" + n] for n in WEIGHT_NAMES}
    v = {n: args["v_" + n] for n in WEIGHT_NAMES}
    return _step(x, p, loss_target, w, m, v)
```

```python
import functools
import math
from typing import Any, Callable, NamedTuple, Sequence

import jax
import jax.numpy as jnp
from jax import lax
from jax.experimental import pallas as pl
from jax.experimental.pallas import tpu as pltpu

F32 = jnp.float32
BF16 = jnp.bfloat16
MESH = pl.DeviceIdType.MESH

S5_WIDTH = 512
RWKV_WIDTH = 512
S5_GROUP = 16
S5_GROUPS = 32
S5_STATE = 64
S5_LANES = S5_GROUPS * S5_STATE
S5_TILE_GROUPS = 8
S5_TILES = S5_GROUPS // S5_TILE_GROUPS
S5_TILE_CH = S5_TILE_GROUPS * S5_GROUP
S5_TILE_LANES = S5_TILE_GROUPS * S5_STATE
HEAD = 64
HEADS = 8
DECAY_LORA = 64
AAA_LORA = 64
GATE_LORA = 128
FFN_HIDDEN = 2816
FFN_TILE = FFN_HIDDEN // 2
RMS_EPS = 1e-6
GN_EPS = 64e-5
L2_EPS = 1e-12
ADAM_LR = 0.001
ADAM_B1 = 0.9
ADAM_B2 = 0.999
ADAM_EPS = 1e-08
ADAM_WD = 0.01
ADAM_STEP = 10

WKV_CHUNK = 64
SCAN_UNROLL = 4
WIRE = jnp.bfloat16
WKV_PASSES = 1
VMEM_LIMIT_BYTES = 48 * 1024 * 1024
LANE = 128
PACK_COLS = 1024
SMALL_ROWS = 24

WEIGHT_NAMES = ['norm_mix', 'w_in', 's5_lam_re', 's5_lam_im', 's5_log_step', 's5_b_re', 's5_b_im', 's5_c_re',
                's5_c_im', 's5_d', 's5_glu_w', 's5_glu_b', 'rw_shift_mu', 'rw_w0', 'rw_w2', 'rw_a0', 'rw_a2',
                'rw_g2', 'rw_k_k', 'rw_k_a', 'rw_r_k', 'rw_ln_w', 'rw_ln_b', 'w_out', 'norm_ffn', 'ffn_w1',
                'ffn_w3', 'ffn_w2', 'norm_ple', 'ple_gate_w', 'ple_up_w', 'final_norm']
SHARDED = {'w_in': 1, 's5_glu_w': 0, 'rw_w2': 1, 'rw_a2': 1, 'rw_g2': 1, 'w_out': 0, 'ffn_w1': 1, 'ffn_w3': 1,
           'ffn_w2': 0, 'ple_gate_w': 0, 'ple_up_w': 1}
SHARDED_NAMES = [n for n in WEIGHT_NAMES if n in SHARDED]
LATE_NAMES = ['w_out', 'ffn_w1', 'ffn_w3', 'ffn_w2', 'ple_gate_w', 'ple_up_w']
EARLY_NAMES = [n for n in SHARDED_NAMES if n not in LATE_NAMES]
LATE_GATHER = {'s5_scan': ['w_out', 'ple_gate_w', 'ple_up_w'], 'wkv_fwd': ['ffn_w1', 'ffn_w3'], 'ffn_up': ['ffn_w2']}
SMALL_NAMES = [n for n in WEIGHT_NAMES if n not in SHARDED]


def _params(sem=None):
    return pltpu.CompilerParams(dimension_semantics=sem, vmem_limit_bytes=VMEM_LIMIT_BYTES)


def _tile(n, target):
    best = None
    for d in range(LANE, min(n, target) + 1, LANE):
        if n % d == 0:
            best = d
    return n if best is None else best


_NN = (((1,), (0,)), ((), ()))
_NT = (((1,), (1,)), ((), ()))
_TN = (((0,), (0,)), ((), ()))


def _split(a):
    a = a.astype(F32)
    hi = a.astype(BF16)
    return hi, (a - hi.astype(F32)).astype(BF16)


def _dg(a, b, dims, passes):
    dg = lambda p, q: lax.dot_general(p, q, dims, preferred_element_type=F32)
    if passes == 1:
        return dg(a.astype(BF16), b.astype(BF16))
    bh, bl = _split(b)
    if passes == 2:
        return dg(a.astype(BF16), bh) + dg(a.astype(BF16), bl)
    ah, al = _split(a)
    return dg(ah, bh) + (dg(ah, bl) + dg(al, bh))


_DOT_BWD = {_NN: (("g", "b", _NT), ("a", "g", _TN)),
            _NT: (("g", "b", _NN), ("g", "a", _TN)),
            _TN: (("b", "g", _NT), ("a", "g", _NN))}


@functools.partial(jax.custom_vjp, nondiff_argnums=(2, 3))
def _dot(a, b, dims, passes):
    return _dg(a, b, dims, passes)


def _dot_fwd(a, b, dims, passes):
    return _dg(a, b, dims, passes), (a, b)


def _dot_bwd(dims, passes, res, g):
    env = {"a": res[0], "b": res[1], "g": g}
    return tuple(_dg(env[p], env[q], d, passes) for p, q, d in _DOT_BWD[dims])


_dot.defvjp(_dot_fwd, _dot_bwd)


def _bdot(x, w):
    return _dot(x, w, _NN, 1)


@jax.custom_vjp
def _shift_down(z):
    return pltpu.roll(z, 1, 0)


def _shift_down_fwd(z):
    return pltpu.roll(z, 1, 0), None


def _shift_down_bwd(_, g):
    return (pltpu.roll(g, g.shape[0] - 1, 0),)


_shift_down.defvjp(_shift_down_fwd, _shift_down_bwd)


def _head_sum_impl(x):
    r = lax.broadcasted_iota(jnp.int32, (LANE, LANE), 0) // HEAD
    c = lax.broadcasted_iota(jnp.int32, (LANE, LANE), 1) // HEAD
    ones = (r == c).astype(BF16)
    hi, lo = _split(x)
    dg = lambda p: lax.dot_general(p, ones, _NN, preferred_element_type=F32)
    tiles = [slice(j, j + LANE) for j in range(0, x.shape[1], LANE)]
    return jnp.concatenate([dg(hi[:, s]) + dg(lo[:, s]) for s in tiles], axis=1)


@jax.custom_vjp
def _head_sum(x):
    return _head_sum_impl(x)


_head_sum.defvjp(lambda x: (_head_sum_impl(x), None), lambda _, g: (_head_sum_impl(g),))


def _mm(name, a, b, mode, out_dtype=F32, precise=False, tm=1024, tn=1024, tk=1536, plan=None):
    if mode == 'nn':
        (m, k), (_, n) = a.shape, b.shape
    elif mode == 'nt':
        (m, k), (n, _) = a.shape, b.shape
    else:
        (k, m), (_, n) = a.shape, b.shape
    tm, tn, tk = _tile(m, tm), _tile(n, tn), _tile(k, tk)
    nm, nn, nk = m // tm, n // tn, k // tk
    dims = {'nn': _NN, 'nt': _NT, 'tn': _TN}[mode]
    plan = _NO_PLAN if plan is None else plan
    parts, plan_in_specs, plan_out_shape, plan_out_specs, plan_sems = _carry(plan, 2, 1)

    def body(*refs):
        a_ref, b_ref, o_ref = refs[0], refs[1], refs[2 + len(plan.ins)]
        acc_ref = refs[3 + len(plan.ins) + len(plan.out_shape)]
        i, j, kk = pl.program_id(0), pl.program_id(1), pl.program_id(2)

        if plan is not _NO_PLAN:
            pl.when((i == 0) & (j == 0) & (kk == 0))(lambda: plan.start(*parts(refs)))

        @pl.when(kk == 0)
        def _():
            acc_ref[...] = jnp.zeros_like(acc_ref)

        acc_ref[...] += _dg(a_ref[...], b_ref[...], dims, 3 if precise else 1)

        @pl.when(kk == nk - 1)
        def _():
            o_ref[...] = acc_ref[...].astype(o_ref.dtype)

        if plan is not _NO_PLAN:
            pl.when((i == nm - 1) & (j == nn - 1) & (kk == nk - 1))(lambda: plan.wait(*parts(refs)))

    if mode == 'tn':
        a_spec = pl.BlockSpec((tk, tm), lambda i, j, l: (l, i))
    else:
        a_spec = pl.BlockSpec((tm, tk), lambda i, j, l: (i, l))
    if mode == 'nt':
        b_spec = pl.BlockSpec((tn, tk), lambda i, j, l: (j, l))
    else:
        b_spec = pl.BlockSpec((tk, tn), lambda i, j, l: (l, j))
    res = pl.pallas_call(
        body, name=name,
        out_shape=[jax.ShapeDtypeStruct((m, n), out_dtype)] + plan_out_shape,
        grid=(nm, nn, nk),
        in_specs=[a_spec, b_spec] + plan_in_specs,
        out_specs=[pl.BlockSpec((tm, tn), lambda i, j, l: (i, j))] + plan_out_specs,
        scratch_shapes=[pltpu.VMEM((tm, tn), F32)] + plan_sems,
        compiler_params=_params(("parallel", "parallel", "arbitrary") if plan is _NO_PLAN else ("arbitrary",) * 3),
    )(a, b, *plan.ins)
    return res[0] if plan is _NO_PLAN else (res[0], res[1:])


def _mm_tiles(name, a, b, mode, out_shape, grid, a_spec, b_spec, o_spec):
    dims = {'nn': _NN, 'nt': _NT, 'tn': _TN}[mode]
    nk = grid[2]

    def body(a_ref, b_ref, o_ref, acc_ref):
        kk = pl.program_id(2)

        @pl.when(kk == 0)
        def _():
            acc_ref[...] = jnp.zeros_like(acc_ref)

        acc_ref[...] += _dg(a_ref[...], b_ref[...], dims, 1)

        @pl.when(kk == nk - 1)
        def _():
            o_ref[...] = acc_ref[...]

    return pl.pallas_call(
        body, name=name,
        out_shape=jax.ShapeDtypeStruct(out_shape, F32),
        grid=grid, in_specs=[a_spec, b_spec], out_specs=o_spec,
        scratch_shapes=[pltpu.VMEM(o_spec.block_shape, F32)],
        compiler_params=_params(("parallel", "parallel", "arbitrary")),
    )(a, b)


def _s5_expand(name, u, blk, tm=2048):
    t = u.shape[0]
    tm = min(tm, t)
    ch, ln, nt = S5_TILE_CH, S5_TILE_LANES, S5_TILES
    return _mm_tiles(name, u, blk, 'nn', (t, 2 * S5_LANES), (t // tm, 2 * nt, 1),
                     pl.BlockSpec((tm, ch), lambda i, j, l: (i, j % nt)),
                     pl.BlockSpec((ch, ln), lambda i, j, l: (j % nt, j // nt)),
                     pl.BlockSpec((tm, ln), lambda i, j, l: (i, j)))


def _s5_contract(name, x, blk, tm=2048):
    t = x.shape[0]
    tm = min(tm, t)
    ch, ln, nt = S5_TILE_CH, S5_TILE_LANES, S5_TILES
    return _mm_tiles(name, x, blk, 'nt', (t, S5_WIDTH), (t // tm, nt, 2),
                     pl.BlockSpec((tm, ln), lambda i, j, l: (i, j + nt * l)),
                     pl.BlockSpec((ch, ln), lambda i, j, l: (j, l)),
                     pl.BlockSpec((tm, ch), lambda i, j, l: (i, j)))


def _s5_block_grad(name, u, x, tk=2048):
    t = u.shape[0]
    tk = min(tk, t)
    ch, ln, nt = S5_TILE_CH, S5_TILE_LANES, S5_TILES
    return _mm_tiles(name, u, x, 'tn', (S5_WIDTH, 2 * ln), (nt, 2, t // tk),
                     pl.BlockSpec((tk, ch), lambda i, j, l: (l, i)),
                     pl.BlockSpec((tk, ln), lambda i, j, l: (l, i + nt * j)),
                     pl.BlockSpec((ch, ln), lambda i, j, l: (i, j)))


def _full_spec(p):
    nd = p.ndim
    return pl.BlockSpec(p.shape, lambda i, nd=nd: (0,) * nd)


def _tok_fwd(name, fn, toks, params, outs, tm):
    t = toks[0].shape[0]
    nt, npar = len(toks), len(params)

    def body(*refs):
        tv = [r[...].astype(F32) for r in refs[:nt]]
        pv = [r[...].astype(F32) for r in refs[nt:nt + npar]]
        res = fn(*tv, *pv)
        for r, v in zip(refs[nt + npar:], res):
            r[...] = v.astype(r.dtype)

    return pl.pallas_call(
        body, name=name,
        out_shape=[jax.ShapeDtypeStruct((t, w), d) for w, d in outs],
        grid=(t // tm,),
        in_specs=[pl.BlockSpec((tm, a.shape[1]), lambda i: (i, 0)) for a in toks] + [_full_spec(p) for p in params],
        out_specs=[pl.BlockSpec((tm, w), lambda i: (i, 0)) for w, _ in outs],
        compiler_params=_params(("parallel",)),
    )(*toks, *params)


def _tok_bwd(name, fn, toks, params, cots, dtok, dpar, tm, acc_out=None, add_to=None):
    t = toks[0].shape[0]
    nt, npar = len(toks), len(params)
    cot_arrays = [c for c in cots if c is not None]
    ncot = len(cot_arrays)
    extra = [] if add_to is None else [add_to[1]]
    dtok_idx = [i for i, d in enumerate(dtok) if d is not None]
    dpar_idx = [i for i, d in enumerate(dpar) if d]

    def body(*refs):
        pos = 0
        tin = refs[pos:pos + nt]; pos += nt
        pin = refs[pos:pos + npar]; pos += npar
        cin = refs[pos:pos + ncot]; pos += ncot
        ein = refs[pos:pos + len(extra)]; pos += len(extra)
        dto = refs[pos:pos + len(dtok_idx)]; pos += len(dtok_idx)
        dpo = refs[pos:pos + len(dpar_idx)]; pos += len(dpar_idx)
        acc = refs[pos] if acc_out is not None else None
        first = pl.program_id(0) == 0

        tv = [r[...].astype(F32) for r in tin]
        pv = [r[...].astype(F32) for r in pin]
        res, vjp = jax.vjp(fn, *tv, *pv)
        cit = iter(cin)
        cs = tuple(jnp.ones_like(o) if c is None else next(cit)[...].astype(F32) for c, o in zip(cots, res))
        g = vjp(cs)
        for r, i in zip(dto, dtok_idx):
            v = g[i]
            if add_to is not None and add_to[0] == i:
                v = v + ein[0][...].astype(F32)
            r[...] = v.astype(r.dtype)

        @pl.when(first)
        def _():
            for r in dpo:
                r[...] = jnp.zeros_like(r)
            if acc is not None:
                acc[...] = jnp.zeros_like(acc)

        for r, i in zip(dpo, dpar_idx):
            r[...] += g[nt + i]
        if acc is not None:
            acc[...] += res[acc_out]

    out_shape = [jax.ShapeDtypeStruct(toks[i].shape, dtok[i]) for i in dtok_idx]
    out_shape += [jax.ShapeDtypeStruct(params[i].shape, F32) for i in dpar_idx]
    out_specs = [pl.BlockSpec((tm, toks[i].shape[1]), lambda i_: (i_, 0)) for i in dtok_idx]
    out_specs += [_full_spec(params[i]) for i in dpar_idx]
    if acc_out is not None:
        out_shape.append(jax.ShapeDtypeStruct((1, 1), F32))
        out_specs.append(pl.BlockSpec((1, 1), lambda i_: (0, 0)))
    tok_spec = lambda a: pl.BlockSpec((tm, a.shape[1]), lambda i_: (i_, 0))
    return pl.pallas_call(
        body, name=name,
        out_shape=out_shape,
        grid=(t // tm,),
        in_specs=[tok_spec(a) for a in toks] + [_full_spec(p) for p in params]
        + [tok_spec(c) for c in cot_arrays] + [tok_spec(e) for e in extra],
        out_specs=out_specs,
        compiler_params=_params(("arbitrary",)),
    )(*toks, *params, *cot_arrays, *extra)


def _small_fwd(name, fn, ins, outs):
    n = len(ins)

    def body(*refs):
        res = fn(*[r[...] for r in refs[:n]])
        for r, v in zip(refs[n:], res):
            r[...] = v.astype(r.dtype)

    return pl.pallas_call(
        body, name=name,
        out_shape=[jax.ShapeDtypeStruct(s, d) for s, d in outs],
        compiler_params=_params(),
    )(*ins)


def _small_bwd(name, fn, ins, cots):
    n = len(ins)

    def body(*refs):
        _, vjp = jax.vjp(fn, *[r[...] for r in refs[:n]])
        g = vjp(tuple(r[...] for r in refs[n:n + len(cots)]))
        for r, v in zip(refs[n + len(cots):], g):
            r[...] = v

    return pl.pallas_call(
        body, name=name,
        out_shape=[jax.ShapeDtypeStruct(a.shape, F32) for a in ins],
        compiler_params=_params(),
    )(*ins, *cots)


def _rms(x, g):
    return x * lax.rsqrt(jnp.mean(x * x, axis=-1, keepdims=True) + RMS_EPS) * g


def _f_norm_in(x, g):
    return (_rms(x, g),)


def _f_mix_res(x, mixed, g):
    h1 = x + mixed
    return h1, _rms(h1, g)


def _f_ffn_act(a13):
    a1, a3 = a13[:, :FFN_HIDDEN], a13[:, FFN_HIDDEN:]
    return (jax.nn.silu(a1) * a3,)


def _f_ffn_res(h1, ffo, g):
    h2 = h1 + ffo
    return h2, _rms(h2, g)


def _f_loss(h2, gpre, pu, target, g):
    h3 = h2 + jax.nn.sigmoid(gpre) * pu
    y = _rms(h3, g)
    err = jnp.square(y - target)
    return (0.5 * jnp.sum(jnp.mean(err, axis=-1, keepdims=True), axis=0, keepdims=True),)


def _f_s5_post(ypre, u, d, glu_w, glu_b):
    z = jax.nn.gelu(ypre + u * d)
    return (z * jax.nn.sigmoid(_bdot(z, glu_w) + glu_b),)


def _softplus(x):
    return jnp.maximum(x, 0.0) + jnp.log(1.0 + jnp.exp(-jnp.abs(x)))


def _f_rw_pre(z, carry, shift_mu, w0, w2, a0, a2, g2, k_k, k_a):
    rw = RWKV_WIDTH
    first_row = lax.broadcasted_iota(jnp.int32, z.shape, 0) == 0
    prev = jnp.where(first_row, carry, _shift_down(z))
    zs = z + (prev - z) * shift_mu
    o1, o2 = 3 * rw + DECAY_LORA, 3 * rw + DECAY_LORA + AAA_LORA
    r, k, v = zs[:, :rw], zs[:, rw:2 * rw], zs[:, 2 * rw:3 * rw]
    wl, al, gl = zs[:, 3 * rw:o1], zs[:, o1:o2], zs[:, o2:]
    w = -_softplus(-(w0 + _bdot(jnp.tanh(wl), w2))) - 0.5
    log_decay = -jnp.exp(w)
    a = jax.nn.sigmoid(a0 + _bdot(al, a2))
    g = _bdot(jax.nn.sigmoid(gl), g2)
    kk = k * k_k
    norm = jnp.sqrt(_head_sum(kk * kk))
    kk = kk / jnp.maximum(norm, L2_EPS)
    kp = k * (1.0 + (a - 1.0) * k_a)
    return r, log_decay, kp, v, -kk, kk * a, g


def _f_rw_post(y, r, kp, v, g, ln_w, ln_b, r_k):
    yc = y - _head_sum(y) * (1.0 / HEAD)
    var = _head_sum(yc * yc) * (1.0 / HEAD)
    yn = yc * lax.rsqrt(var + GN_EPS) * ln_w + ln_b
    bonus = _head_sum(r * kp * r_k) * v
    return ((yn + bonus) * g,)


def _f_s5_lam(lam_re, lam_im, log_step):
    step = jnp.exp(log_step)
    dr, di = lam_re * step, lam_im * step
    e = jnp.exp(dr)
    lbr, lbi = e * jnp.cos(di), e * jnp.sin(di)
    nr, ni = lbr - 1.0, lbi
    den = lam_re * lam_re + lam_im * lam_im
    return lbr, lbi, (nr * lam_re + ni * lam_im) / den, (ni * lam_re - nr * lam_im) / den


def _f_s5_build(coef_r, coef_i, btr, bti, ctr, cti):
    bbr = coef_r * btr - coef_i * bti
    bbi = coef_r * bti + coef_i * btr
    shape = (S5_WIDTH, S5_TILE_LANES)
    rows = (lax.broadcasted_iota(jnp.int32, shape, 0) % S5_TILE_CH) // S5_GROUP
    cols = lax.broadcasted_iota(jnp.int32, shape, 1) // S5_STATE
    mask = (rows == cols).astype(F32)

    def blocks(m):
        per_tile = [m[:, S5_TILE_LANES * i:S5_TILE_LANES * (i + 1)] for i in range(S5_TILES)]
        return jnp.concatenate([t for t in per_tile for _ in range(S5_TILE_GROUPS)], axis=0) * mask

    return (jnp.concatenate([blocks(bbr), blocks(bbi)], axis=1),
            jnp.concatenate([blocks(ctr), -blocks(cti)], axis=1))


HALO = 8


def _rw_pre_specs(z, params, tm, order):
    halo_blocks = tm // HALO
    return ([pl.BlockSpec((tm, z.shape[1]), lambda i: (order(i), 0)),
             pl.BlockSpec((HALO, z.shape[1]), lambda i: (jnp.maximum(order(i) * halo_blocks - 1, 0), 0))]
            + [_full_spec(p) for p in params])


def _rw_pre_fwd(z, params, tm):
    t = z.shape[0]
    npar = len(params)

    def body(z_ref, halo_ref, *refs):
        carry = jnp.where(pl.program_id(0) == 0, 0.0, halo_ref[pl.ds(HALO - 1, 1), :])
        res = _f_rw_pre(z_ref[...], carry, *[r[...].astype(F32) for r in refs[:npar]])
        for r, v in zip(refs[npar:], res):
            r[...] = v

    return pl.pallas_call(
        body, name="rw_pre",
        out_shape=[jax.ShapeDtypeStruct((t, RWKV_WIDTH), F32)] * 7,
        grid=(t // tm,),
        in_specs=_rw_pre_specs(z, params, tm, lambda i: i),
        out_specs=[pl.BlockSpec((tm, RWKV_WIDTH), lambda i: (i, 0))] * 7,
        compiler_params=_params(("parallel",)),
    )(z, z, *params)


def _rw_pre_bwd(z, params, cots, tm):
    t = z.shape[0]
    nt = t // tm
    npar = len(params)
    order = lambda i: nt - 1 - i
    flat_cots = [a for group in cots for a in group]
    ncot = len(flat_cots)

    def body(z_ref, halo_ref, *refs):
        pin, cin = refs[:npar], list(refs[npar:npar + ncot])
        dz_ref = refs[npar + ncot]
        dpo = refs[npar + ncot + 1:npar + ncot + 1 + npar]
        dcarry_ref = refs[npar + ncot + 1 + npar]
        i = pl.program_id(0)

        @pl.when(i == 0)
        def _():
            dcarry_ref[...] = jnp.zeros_like(dcarry_ref)
            for r in dpo:
                r[...] = jnp.zeros_like(r)

        carry = jnp.where(i == nt - 1, 0.0, halo_ref[pl.ds(HALO - 1, 1), :])
        _, vjp = jax.vjp(_f_rw_pre, z_ref[...], carry, *[r[...].astype(F32) for r in pin])
        g = vjp(tuple(sum(cin.pop(0)[...] for _ in group) for group in cots))
        last_row = lax.broadcasted_iota(jnp.int32, z_ref.shape, 0) == tm - 1
        dz_ref[...] = g[0] + jnp.where(last_row, dcarry_ref[...], 0.0)
        dcarry_ref[...] = g[1]
        for r, v in zip(dpo, g[2:]):
            r[...] += v

    tok = lambda w: pl.BlockSpec((tm, w), lambda i: (order(i), 0))
    return pl.pallas_call(
        body, name="rw_pre_bwd",
        out_shape=[jax.ShapeDtypeStruct(z.shape, F32)] + [jax.ShapeDtypeStruct(p.shape, F32) for p in params],
        grid=(nt,),
        in_specs=_rw_pre_specs(z, params, tm, order) + [tok(RWKV_WIDTH)] * ncot,
        out_specs=[tok(z.shape[1])] + [_full_spec(p) for p in params],
        scratch_shapes=[pltpu.VMEM((1, z.shape[1]), F32)],
        compiler_params=_params(("arbitrary",)),
    )(z, z, *params, *flat_cots)


def _s5_scan(bu, lam, tm, plan):
    t, w = bu.shape
    h = w // 2
    nt = t // tm
    parts, plan_in_specs, plan_out_shape, plan_out_specs, plan_sems = _carry(plan, 2, 1)

    def body(*refs):
        bu_ref, lam_ref = refs[:2]
        xb_ref = refs[2 + len(plan.ins)]
        carry_ref = refs[3 + len(plan.ins) + len(plan.out_shape)]
        x_ref, refs = refs[-1], refs[:-1]

        @pl.when(pl.program_id(0) == 0)
        def _():
            carry_ref[...] = jnp.zeros_like(carry_ref)
            plan.start(*parts(refs))

        lr, li = lam_ref[:, :h], lam_ref[:, h:]

        def step(s, c):
            cr, ci = c
            row = pl.ds(s, 1)
            nr = lr * cr - li * ci + bu_ref[row, :h]
            ni = lr * ci + li * cr + bu_ref[row, h:]
            x_ref[row, :h] = nr
            x_ref[row, h:] = ni
            return nr, ni

        cr, ci = lax.fori_loop(0, tm, step, (carry_ref[:, :h], carry_ref[:, h:]), unroll=SCAN_UNROLL)
        carry_ref[:, :h] = cr
        carry_ref[:, h:] = ci
        xb_ref[...] = x_ref[...].astype(BF16)

        @pl.when(pl.program_id(0) == nt - 1)
        def _():
            plan.wait(*parts(refs))

    spec = pl.BlockSpec((tm, w), lambda i: (i, 0))
    res = pl.pallas_call(
        body, name="s5_scan",
        out_shape=[jax.ShapeDtypeStruct((t, w), BF16)] + plan_out_shape,
        grid=(nt,),
        in_specs=[spec, pl.BlockSpec((1, w), lambda i: (0, 0))] + plan_in_specs,
        out_specs=[spec] + plan_out_specs,
        scratch_shapes=[pltpu.VMEM((1, w), F32)] + plan_sems + [pltpu.VMEM((tm, w), F32)],
        compiler_params=_params(("arbitrary",)),
    )(bu, lam, *plan.ins)
    return res[0], res[1:]


def _s5_scan_bwd(dx, xb, lam, tm):
    t, w = dx.shape
    h = w // 2
    nt = t // tm
    halo = BF16_SUBLANES

    rows8 = 8

    def body(dx_ref, xb_ref, halo_ref, lam_ref, dbu_out_ref, dlam_ref, carry_ref, dbu_ref, xp_ref):
        @pl.when(pl.program_id(0) == 0)
        def _():
            carry_ref[...] = jnp.zeros_like(carry_ref)
            dlam_ref[...] = jnp.zeros_like(dlam_ref)

        lr, li = lam_ref[:, :h], lam_ref[:, h:]

        def step(s, c):
            cr, ci = c
            row = pl.ds(tm - 1 - s, 1)
            nr = lr * cr + li * ci + dx_ref[row, :h]
            ni = lr * ci - li * cr + dx_ref[row, h:]
            dbu_ref[row, :h] = nr
            dbu_ref[row, h:] = ni
            return nr, ni

        cr, ci = lax.fori_loop(0, tm, step, (carry_ref[:, :h], carry_ref[:, h:]), unroll=SCAN_UNROLL)
        carry_ref[:, :h] = cr
        carry_ref[:, h:] = ci
        halo_rows = lax.broadcasted_iota(jnp.int32, (halo, w), 0)
        before = jnp.sum(jnp.where(halo_rows == halo - 1, halo_ref[...].astype(F32), 0.0), axis=0, keepdims=True)
        before = jnp.where(pl.program_id(0) == nt - 1, 0.0, before)
        first_row = lax.broadcasted_iota(jnp.int32, (tm, w), 0) == 0
        xp_ref[...] = jnp.where(first_row, before, pltpu.roll(xb_ref[...].astype(F32), 1, 0))

        def accumulate(s, acc):
            ar, ai = acc
            rows = pl.ds(pl.multiple_of(s * rows8, rows8), rows8)
            gr, gi = dbu_ref[rows, :h], dbu_ref[rows, h:]
            pr, pi_ = xp_ref[rows, :h], xp_ref[rows, h:]
            return ar + (gr * pr + gi * pi_), ai + (gi * pr - gr * pi_)

        zero = jnp.zeros((rows8, h), F32)
        ar, ai = lax.fori_loop(0, tm // rows8, accumulate, (zero, zero))
        dlam_ref[:, :h] += jnp.sum(ar, axis=0, keepdims=True)
        dlam_ref[:, h:] += jnp.sum(ai, axis=0, keepdims=True)
        dbu_out_ref[...] = dbu_ref[...].astype(BF16)

    spec = pl.BlockSpec((tm, w), lambda i: (nt - 1 - i, 0))
    halo_spec = pl.BlockSpec((halo, w), lambda i: (jnp.maximum((nt - 1 - i) * (tm // halo) - 1, 0), 0))
    row_spec = pl.BlockSpec((1, w), lambda i: (0, 0))
    return pl.pallas_call(
        body, name="s5_scan_bwd",
        out_shape=[jax.ShapeDtypeStruct((t, w), BF16), jax.ShapeDtypeStruct((1, w), F32)],
        grid=(nt,),
        in_specs=[spec, spec, halo_spec, row_spec],
        out_specs=[spec, row_spec],
        scratch_shapes=[pltpu.VMEM((1, w), F32), pltpu.VMEM((tm, w), F32), pltpu.VMEM((tm, w), F32)],
        compiler_params=_params(("arbitrary",)),
    )(dx, xb, xb, lam)


def _unit_lower_inverses_impl(ns):
    c = ns[0].shape[0]
    eye = (lax.broadcasted_iota(jnp.int32, (c, c), 0) == lax.broadcasted_iota(jnp.int32, (c, c), 1)).astype(F32)
    inv = [eye + n for n in ns]
    pw = [_dg(n, n, _NN, WKV_PASSES) for n in ns]
    for _ in range(int(math.log2(c)) - 2):
        both = [_dg(jnp.concatenate([i, q], axis=0), q, _NN, WKV_PASSES) for i, q in zip(inv, pw)]
        inv = [i + q[:c] for i, q in zip(inv, both)]
        pw = [q[c:] for q in both]
    return tuple(i + _dg(i, q, _NN, WKV_PASSES) for i, q in zip(inv, pw))


@jax.custom_vjp
def _unit_lower_inverses(ns):
    return _unit_lower_inverses_impl(ns)


def _unit_lower_inverses_fwd(ns):
    inv = _unit_lower_inverses_impl(ns)
    return inv, inv


def _unit_lower_inverses_bwd(inv, g):
    left = [_dg(i, gi, _TN, WKV_PASSES) for i, gi in zip(inv, g)]
    return (tuple(_dg(q, i, _NT, WKV_PASSES) for q, i in zip(left, inv)),)


_unit_lower_inverses.defvjp(_unit_lower_inverses_fwd, _unit_lower_inverses_bwd)


def _wkv_chunks(s0, r, lw, k, v, a, b):
    c = r[0].shape[0]
    row = lax.broadcasted_iota(jnp.int32, (c, c), 0)
    col = lax.broadcasted_iota(jnp.int32, (c, c), 1)
    incl, strict = col <= row, col < row
    tri = incl.astype(F32)
    each = lambda f, *xs: [f(*t) for t in zip(*xs)]
    stack = lambda p, q: jnp.concatenate([p, q], axis=0)
    dot = lambda p, q, dims=_NN: _dot(p, q, dims, WKV_PASSES)
    lc = each(lambda l: _dot(tri, l, _NN, 2), lw)
    e_neg = each(lambda l: jnp.exp(-l), lc)
    ar = each(lambda x, z, l, w: stack(x * jnp.exp(l - w), z * jnp.exp(l)), a, r, lc, lw)
    bk = each(lambda x, z, e: stack(x * e, z * e), b, k, e_neg)
    m = each(lambda p, q: dot(p, q, _NT), ar, bk)
    mab = each(lambda q: jnp.where(strict, q[:c, :c], 0.0), m)
    mak_mrk = each(lambda q: stack(jnp.where(strict, q[:c, c:], 0.0), jnp.where(incl, q[c:, c:], 0.0)), m)
    mrb = each(lambda q: jnp.where(incl, q[c:, :c], 0.0), m)
    xy = each(lambda p, s, q, z: dot(p, s, _NT) + dot(q, z), ar, s0, mak_mrk, v)
    inv = _unit_lower_inverses(tuple(mab))
    u = each(lambda i, q: dot(i, q[:c]), inv, xy)
    y = each(lambda q, z, p: q[c:] + dot(z, p), xy, mrb, u)
    e_tot = each(lambda l: jnp.exp(jnp.sum(l, axis=0, keepdims=True)), lw)
    s1 = each(lambda s, p, z, q, e: (s + dot(stack(p, z), q, _TN)) * e, s0, u, v, bk, e_tot)
    return y, s1


def _carry(plan, n_args, n_outs):
    n_in, n_out = len(plan.ins), len(plan.out_shape)

    def parts(refs):
        base = n_args + n_in + n_outs
        return refs[n_args:n_args + n_in], refs[base:base + n_out], refs[base + n_out + 1:]

    return parts, [HBM_SPEC] * n_in, list(plan.out_shape), [HBM_SPEC] * n_out, list(plan.sems)


def _head_cols(ref):
    return tuple(ref[:, h * HEAD:(h + 1) * HEAD] for h in range(HEADS))


def _wkv_fwd(seqs, plan):
    t, w = seqs[0].shape
    c, n = WKV_CHUNK, HEAD
    nc = t // c
    parts, plan_in_specs, plan_out_shape, plan_out_specs, plan_sems = _carry(plan, 6, 2)

    def body(*refs):
        ins, (y_ref, ck_ref) = refs[:6], refs[6 + len(plan.ins):8 + len(plan.ins)]
        s_ref = refs[8 + len(plan.ins) + len(plan.out_shape)]

        @pl.when(pl.program_id(0) == 0)
        def _():
            s_ref[...] = jnp.zeros_like(s_ref)
            plan.start(*parts(refs))

        s0 = tuple(s_ref[h] for h in range(HEADS))
        ys, s1 = _wkv_chunks(s0, *[_head_cols(r) for r in ins])
        for h in range(HEADS):
            ck_ref[0, h] = s0[h]
            y_ref[:, h * n:(h + 1) * n] = ys[h]
            s_ref[h] = s1[h]

        @pl.when(pl.program_id(0) == nc - 1)
        def _():
            plan.wait(*parts(refs))

    spec = pl.BlockSpec((c, w), lambda i: (i, 0))
    res = pl.pallas_call(
        body, name="wkv_fwd",
        out_shape=[jax.ShapeDtypeStruct((t, w), F32), jax.ShapeDtypeStruct((nc, HEADS, n, n), F32)] + plan_out_shape,
        grid=(nc,),
        in_specs=[spec] * 6 + plan_in_specs,
        out_specs=[spec, pl.BlockSpec((1, HEADS, n, n), lambda i: (i, 0, 0, 0))] + plan_out_specs,
        scratch_shapes=[pltpu.VMEM((HEADS, n, n), F32)] + plan_sems,
        compiler_params=_params(("arbitrary",)),
    )(*seqs, *plan.ins)
    return res[0], res[1], res[2:]


def _wkv_bwd(seqs, ck, dy, plan):
    t, w = seqs[0].shape
    c, n = WKV_CHUNK, HEAD
    nc = t // c
    parts, plan_in_specs, plan_out_shape, plan_out_specs, plan_sems = _carry(plan, 8, 6)

    def body(*refs):
        ins, ck_ref, dy_ref = refs[:6], refs[6], refs[7]
        outs = refs[8 + len(plan.ins):14 + len(plan.ins)]
        ds_ref = refs[14 + len(plan.ins) + len(plan.out_shape)]

        @pl.when(pl.program_id(0) == 0)
        def _():
            ds_ref[...] = jnp.zeros_like(ds_ref)
            plan.start(*parts(refs))

        s0 = tuple(ck_ref[0, h] for h in range(HEADS))
        _, vjp = jax.vjp(_wkv_chunks, s0, *[_head_cols(r) for r in ins])
        g = vjp((list(_head_cols(dy_ref)), [ds_ref[h] for h in range(HEADS)]))
        for h in range(HEADS):
            ds_ref[h] = g[0][h]
            for o, d in zip(outs, g[1:]):
                o[:, h * n:(h + 1) * n] = d[h]

        @pl.when(pl.program_id(0) == nc - 1)
        def _():
            plan.wait(*parts(refs))

    spec = pl.BlockSpec((c, w), lambda i: (nc - 1 - i, 0))
    res = pl.pallas_call(
        body, name="wkv_bwd",
        out_shape=[jax.ShapeDtypeStruct((t, w), F32)] * 6 + plan_out_shape,
        grid=(nc,),
        in_specs=[spec] * 6 + [pl.BlockSpec((1, HEADS, n, n), lambda i: (nc - 1 - i, 0, 0, 0)), spec] + plan_in_specs,
        out_specs=[spec] * 6 + plan_out_specs,
        scratch_shapes=[pltpu.VMEM((HEADS, n, n), F32)] + plan_sems,
        compiler_params=_params(("arbitrary",)),
    )(*seqs, ck, dy, *plan.ins)
    return res[:6], res[6:]


def _coords():
    return lax.axis_index("x"), lax.axis_index("y"), lax.axis_index("c")


def _flip(v, f):
    return 1 - v if f else v


_CHIP_FLIPS = [(1, 0), (0, 1), (1, 1)]
_DEV_FLIPS = [(fx, fy, fc) for fx in (0, 1) for fy in (0, 1) for fc in (0, 1) if (fx, fy, fc) != (0, 0, 0)]
HBM_SPEC = pl.BlockSpec(memory_space=pl.ANY)


def _chip_peer(k, x, y):
    fx, fy = _CHIP_FLIPS[k]
    return _flip(x, fx), _flip(y, fy)


def _dev_peer(k, x, y, c):
    fx, fy, fc = _DEV_FLIPS[k]
    return _flip(x, fx), _flip(y, fy), _flip(c, fc)


def _rows_of_core(ref, core):
    h = ref.shape[-2] // 2
    rows = pl.ds(pl.multiple_of(core * h, 8), h)
    return ref.at[rows, :] if len(ref.shape) == 2 else ref.at[:, rows, :]


class _Plan(NamedTuple):
    ins: Sequence[Any]
    out_shape: Sequence[Any]
    sems: Sequence[Any]
    start: Callable
    wait: Callable


_NO_PLAN = _Plan([], [], [], lambda *_: None, lambda *_: None)


def _run_plan(name, plan):
    n_in, n_out = len(plan.ins), len(plan.out_shape)

    def body(*refs):
        parts = refs[:n_in], refs[n_in:n_in + n_out], refs[n_in + n_out:]
        plan.start(*parts)
        plan.wait(*parts)

    return pl.pallas_call(
        body, name=name, out_shape=list(plan.out_shape),
        in_specs=[HBM_SPEC] * n_in, out_specs=[HBM_SPEC] * n_out, scratch_shapes=list(plan.sems),
    )(*plan.ins)


def _gather_plan(shards):
    n = len(shards)

    def copies(srcs, outs, sems):
        send_sems, recv_sems, local_sems = sems
        x, y, c = _coords()
        me = 2 * x + y

        def remote(i, k, arriving):
            px, py = _chip_peer(k, x, y)
            return pltpu.make_async_remote_copy(
                src_ref=srcs[i], dst_ref=outs[i].at[2 * px + py if arriving else me],
                send_sem=send_sems.at[i, k], recv_sem=recv_sems.at[i, k],
                device_id=(px, py, c), device_id_type=MESH)

        own = [pltpu.make_async_copy(srcs[i], outs[i].at[me], local_sems.at[i]) for i in range(n)]
        pairs = [(i, k) for k in range(3) for i in range(n)]
        return own, [remote(i, k, False) for i, k in pairs], [remote(i, k, True) for i, k in pairs]

    return _Plan(
        ins=shards, out_shape=[jax.ShapeDtypeStruct((4,) + s.shape, s.dtype) for s in shards],
        sems=[pltpu.SemaphoreType.DMA((n, 3)), pltpu.SemaphoreType.DMA((n, 3)), pltpu.SemaphoreType.DMA((n,))],
        start=functools.partial(_start_copies, copies), wait=functools.partial(_wait_copies, copies))


def _start_copies(copies, ins, outs, sems):
    own, sends, _ = copies(ins, outs, sems)
    for cp in own + sends:
        cp.start()


def _wait_copies(copies, ins, outs, sems):
    own, sends, arrivals = copies(ins, outs, sems)
    for cp in arrivals:
        cp.wait_recv()
    for cp in sends:
        cp.wait_send()
    for cp in own:
        cp.wait()


def _exchange_plan(gs, small=None):
    n = len(gs)
    arrays = list(gs) + ([] if small is None else [small])

    def copies(srcs, outs, sems):
        send_sems, recv_sems, local_sems = sems
        x, y, c = _coords()
        me = 4 * x + 2 * y + c

        def piece(i, px, py, pc):
            if i == n:
                return srcs[i].at[4 * px + 2 * py + pc]
            return _rows_of_core(srcs[i].at[2 * px + py], pc)

        def remote(i, k, arriving):
            px, py, pc = _dev_peer(k, x, y, c)
            return pltpu.make_async_remote_copy(
                src_ref=piece(i, px, py, pc), dst_ref=outs[i].at[4 * px + 2 * py + pc if arriving else me],
                send_sem=send_sems.at[i, k], recv_sem=recv_sems.at[i, k],
                device_id=(px, py, pc), device_id_type=MESH)

        own = [pltpu.make_async_copy(piece(i, x, y, c), outs[i].at[me], local_sems.at[i]) for i in range(len(arrays))]
        pairs = [(i, k) for k in range(7) for i in range(len(arrays))]
        return own, [remote(i, k, False) for i, k in pairs], [remote(i, k, True) for i, k in pairs]

    out_shape = [jax.ShapeDtypeStruct((8, g.shape[1] // 2, g.shape[2]), g.dtype) for g in gs]
    if small is not None:
        out_shape.append(jax.ShapeDtypeStruct(small.shape, small.dtype))
    m = len(arrays)
    return _Plan(
        ins=arrays, out_shape=out_shape,
        sems=[pltpu.SemaphoreType.DMA((m, 7)), pltpu.SemaphoreType.DMA((m, 7)), pltpu.SemaphoreType.DMA((m,))],
        start=functools.partial(_start_copies, copies), wait=functools.partial(_wait_copies, copies))


def _share_cores(halves, small):
    n = len(halves)

    def body(*refs):
        srcs, small_src, outs, small_out = refs[:n], refs[n], refs[n + 1:2 * n + 1], refs[2 * n + 1]
        mine, theirs = refs[2 * n + 2:3 * n + 2], refs[3 * n + 2:4 * n + 2]
        send_sems, recv_sems, ssend, srecv, local_sems = refs[4 * n + 2:]
        x, y, c = _coords()
        me = 4 * x + 2 * y + c

        def big(i):
            return pltpu.make_async_remote_copy(
                src_ref=mine[i], dst_ref=theirs[i], send_sem=send_sems.at[i], recv_sem=recv_sems.at[i],
                device_id=(x, y, 1 - c), device_id_type=MESH)

        def tiny(k, arriving):
            px, py, pc = _dev_peer(k, x, y, c)
            return pltpu.make_async_remote_copy(
                src_ref=small_src, dst_ref=small_out.at[4 * px + 2 * py + pc if arriving else me],
                send_sem=ssend.at[k], recv_sem=srecv.at[k], device_id=(px, py, pc), device_id_type=MESH)

        small_sends = [tiny(k, False) for k in range(7)]
        own_small = pltpu.make_async_copy(small_src, small_out.at[me], local_sems.at[2 * n])
        stage = [pltpu.make_async_copy(srcs[i], mine[i], local_sems.at[i]) for i in range(n)]
        for cp in small_sends + [own_small] + stage:
            cp.start()
        sends = []
        for i in range(n):
            stage[i].wait()
            sends.append(big(i))
            sends[-1].start()
        store = [pltpu.make_async_copy(mine[i], outs[i].at[c], local_sems.at[i]) for i in range(n)]
        for cp in store:
            cp.start()
        for i in range(n):
            big(i).wait_recv()
            store.append(pltpu.make_async_copy(theirs[i], outs[i].at[1 - c], local_sems.at[n + i]))
            store[-1].start()
        for k in range(7):
            tiny(k, True).wait_recv()
        for cp in sends + small_sends:
            cp.wait_send()
        for cp in store + [own_small]:
            cp.wait()

    staged = [pltpu.VMEM(s.shape, s.dtype) for s in halves]
    res = pl.pallas_call(
        body, name="share_cores",
        out_shape=[jax.ShapeDtypeStruct((2,) + s.shape, s.dtype) for s in halves]
        + [jax.ShapeDtypeStruct((8,) + small.shape, small.dtype)],
        in_specs=[HBM_SPEC] * (n + 1), out_specs=[HBM_SPEC] * (n + 1),
        scratch_shapes=staged + staged + [
            pltpu.SemaphoreType.DMA((n,)), pltpu.SemaphoreType.DMA((n,)),
            pltpu.SemaphoreType.DMA((7,)), pltpu.SemaphoreType.DMA((7,)),
            pltpu.SemaphoreType.DMA((2 * n + 1,))],
        compiler_params=pltpu.CompilerParams(vmem_limit_bytes=VMEM_LIMIT_BYTES),
    )(*halves, small)
    return res[:n], res[n]


BF16_SUBLANES = 16


def _row_tile(n, target, step=BF16_SUBLANES):
    return max([d for d in range(step, min(n, target) + 1, step) if n % d == 0] or [n])


def _ew(name, fn, ins, outs, block_bytes=2 << 20):
    rows, cols = ins[0].shape[-2:]
    lead = max(math.prod(a.shape[:-2]) for a in ins)
    tr = _row_tile(rows, max(8, block_bytes // (4 * cols * lead)))
    n = len(ins)

    def spec(shape):
        if len(shape) == 2:
            return pl.BlockSpec((tr, cols), lambda i: (i, 0))
        return pl.BlockSpec((shape[0], tr, cols), lambda i: (0, i, 0))

    def body(*refs):
        res = fn(*[r[...] for r in refs[:n]])
        for r, v in zip(refs[n:], res):
            r[...] = v

    return pl.pallas_call(
        body, name=name,
        out_shape=[jax.ShapeDtypeStruct(s, F32) for s in outs],
        grid=(rows // tr,),
        in_specs=[spec(a.shape) for a in ins],
        out_specs=[spec(s) for s in outs],
        compiler_params=_params(("parallel",)),
    )(*ins)


def _sum_slots(a):
    total = a[0].astype(F32)
    for s in range(1, a.shape[0]):
        total = total + a[s].astype(F32)
    return (total,)


def _adamw(g, w, m, v):
    bc1 = 1.0 - ADAM_B1 ** ADAM_STEP
    bc2 = 1.0 - ADAM_B2 ** ADAM_STEP
    m_new = ADAM_B1 * m + (1.0 - ADAM_B1) * g
    v_new = ADAM_B2 * v + (1.0 - ADAM_B2) * jnp.square(g)
    delta = -ADAM_LR * ((m_new / bc1) / (jnp.sqrt(v_new / bc2) + ADAM_EPS) + ADAM_WD * w)
    return delta, m_new, v_new


def _mat(a):
    return a.reshape(a.shape[-2:])


def _to_shard_major(full, axis):
    rows, cols = full.shape
    if axis == 0:
        return full.reshape(4, rows // 4, cols)
    return full.reshape(rows, 4, cols // 4).transpose(1, 0, 2)


def _from_shard_major(a, axis):
    _, r, cs = a.shape
    if axis == 0:
        return a.reshape(4 * r, cs)
    return a.transpose(1, 0, 2).reshape(r, 4 * cs)


def _pack_small(arrays, tail=None):
    flat = [arrays[n].reshape(-1) for n in SMALL_NAMES] + ([] if tail is None else [tail.reshape(1)])
    used = sum(a.shape[0] for a in flat)
    flat.append(jnp.zeros((8 * SMALL_ROWS * PACK_COLS - used,), F32))
    return jnp.concatenate(flat).reshape(8, SMALL_ROWS, PACK_COLS)


def _unpack_small(packed, shapes):
    flat = packed.reshape(-1)
    out, off = {}, 0
    for n in SMALL_NAMES:
        size = math.prod(shapes[n])
        out[n] = flat[off:off + size].reshape(shapes[n])
        off += size
    return out


def _row(a):
    return a.reshape(1, -1)


def _local_step(x, p, target, wf, ws, late_shards):
    wf = dict(wf)
    t = x.shape[0]
    tm = min(256, t)
    tw = min(512, t)
    g = {}

    lam_re, lam_im = ws['s5_lam_re'].reshape(S5_GROUPS, S5_STATE), ws['s5_lam_im'].reshape(S5_GROUPS, S5_STATE)
    log_step = ws['s5_log_step'].reshape(S5_GROUPS, 1)
    gp = (S5_GROUPS, S5_STATE)
    lam_ins = (lam_re, lam_im, log_step)
    lbr, lbi, cfr, cfi = _small_fwd("s5_lam", _f_s5_lam, lam_ins, [(gp, F32)] * 4)
    lam_row = jnp.concatenate([_row(lbr), _row(lbi)], axis=1)
    to_t = lambda a, perm: a.reshape((S5_GROUPS,) + a.shape[-2:]).transpose(perm).reshape(S5_GROUP, S5_LANES)
    build_ins = (_row(cfr), _row(cfi), to_t(ws['s5_b_re'], (2, 0, 1)), to_t(ws['s5_b_im'], (2, 0, 1)),
                 to_t(ws['s5_c_re'], (1, 0, 2)), to_t(ws['s5_c_im'], (1, 0, 2)))
    block_shape = (S5_WIDTH, 2 * S5_TILE_LANES)
    b_blk, c_blk = _small_fwd("s5_build", _f_s5_build, build_ins, [(block_shape, F32)] * 2)

    norm_mix, norm_ffn, norm_ple = _row(ws['norm_mix']), _row(ws['norm_ffn']), _row(ws['norm_ple'])
    final_norm = _row(ws['final_norm'])
    (xn,) = _tok_fwd("norm_in", _f_norm_in, [x], [norm_mix], [(x.shape[1], BF16)], tw)
    u = _mm("proj_s5", xn, wf['w_in'][:, :S5_WIDTH], 'nn')
    z = _mm("proj_rw", xn, wf['w_in'][:, S5_WIDTH:], 'nn')

    bu = _s5_expand("s5_bu", u, b_blk)
    def late_plan(carrier):
        return _gather_plan([late_shards[n] for n in LATE_GATHER[carrier]])

    def arrived(carrier, got):
        wf.update({n: _from_shard_major(a, SHARDED[n]) for n, a in zip(LATE_GATHER[carrier], got)})

    xs, got = _s5_scan(bu, lam_row, tm, late_plan('s5_scan'))
    arrived('s5_scan', got)
    ypre = _s5_contract("s5_y", xs, c_blk)
    s5_par = [_row(ws['s5_d']), wf['s5_glu_w'], _row(ws['s5_glu_b'])]
    (s5_out,) = _tok_fwd("s5_post", _f_s5_post, [ypre, u], s5_par, [(S5_WIDTH, BF16)], tw)

    pre_par = [_row(ws['rw_shift_mu']), _row(ws['rw_w0']), wf['rw_w2'], _row(ws['rw_a0']), wf['rw_a2'],
               wf['rw_g2'], _row(ws['rw_k_k']), _row(ws['rw_k_a'])]
    r, lw, kp, v, an, bn, gate = _rw_pre_fwd(z, pre_par, tw)
    seqs = [r, lw, kp, v, an, bn]
    y_wkv, ck, got = _wkv_fwd(seqs, late_plan('wkv_fwd'))
    arrived('wkv_fwd', got)
    post_par = [_row(ws['rw_ln_w']), _row(ws['rw_ln_b']), _row(ws['rw_r_k'])]
    post_toks = [y_wkv, r, kp, v, gate]
    (rw_out,) = _tok_fwd("rw_post", _f_rw_post, post_toks, post_par, [(RWKV_WIDTH, BF16)], tw)

    mixcat = jnp.concatenate([s5_out, rw_out], axis=1)
    mixed = _mm("mix_out", mixcat, wf['w_out'], 'nn')
    h1, hn = _tok_fwd("mix_res", _f_mix_res, [x, mixed], [norm_ffn], [(x.shape[1], F32), (x.shape[1], BF16)], tw)
    w13 = jnp.concatenate([wf['ffn_w1'], wf['ffn_w3']], axis=1)
    a13, got = _mm("ffn_up", hn, w13, 'nn', out_dtype=BF16, tn=FFN_TILE, plan=late_plan('ffn_up'))
    arrived('ffn_up', got)
    (f,) = _tok_fwd("ffn_act", _f_ffn_act, [a13], [], [(FFN_HIDDEN, BF16)], tw)
    ffo = _mm("ffn_down", f, wf['ffn_w2'], 'nn')
    h2, hp = _tok_fwd("ffn_res", _f_ffn_res, [h1, ffo], [norm_ple], [(x.shape[1], F32), (x.shape[1], BF16)], tw)
    gpre = _mm("ple_gate", hp, wf['ple_gate_w'], 'nn')
    pu = _mm("ple_up", p, wf['ple_up_w'], 'nn')

    dh2, dgpre, dpu, g['final_norm'], loss = _tok_bwd(
        "loss", _f_loss, [h2, gpre, pu, target], [final_norm], [None],
        [F32, BF16, BF16, None], [True], tw, acc_out=0)
    g['ple_gate_w'] = _mm("d_ple_gate_w", hp, dgpre, 'tn', out_dtype=WIRE)
    g['ple_up_w'] = _mm("d_ple_up_w", p, dpu, 'tn', out_dtype=WIRE)
    dhp = _mm("d_hp", dgpre, wf['ple_gate_w'], 'nt')
    dh1, dffo, g['norm_ple'] = _tok_bwd("ffn_res_bwd", _f_ffn_res, [h1, ffo], [norm_ple], [dh2, dhp],
                                        [F32, BF16], [True], tw)
    g['ffn_w2'] = _mm("d_ffn_w2", f, dffo, 'tn', out_dtype=WIRE, tm=FFN_TILE)
    df = _mm("d_f", dffo, wf['ffn_w2'], 'nt', out_dtype=BF16, tn=FFN_TILE)
    (da13,) = _tok_bwd("ffn_act_bwd", _f_ffn_act, [a13], [], [df], [BF16], [], tw)
    dw13 = _mm("d_ffn_w13", hn, da13, 'tn', out_dtype=WIRE, tn=FFN_TILE)
    dw13 = dw13.reshape(dw13.shape[0], 8, FFN_HIDDEN // 4).transpose(1, 0, 2)
    shard_major = {'ffn_w1': dw13[:4], 'ffn_w3': dw13[4:]}
    dhn = _mm("d_hn", da13, w13, 'nt')
    dx_a, dmixed, g['norm_ffn'] = _tok_bwd("mix_res_bwd", _f_mix_res, [x, mixed], [norm_ffn], [dh1, dhn],
                                           [F32, BF16], [True], tw)
    g['w_out'] = _mm("d_w_out", mixcat, dmixed, 'tn', out_dtype=WIRE)
    dmixcat = _mm("d_mixcat", dmixed, wf['w_out'], 'nt')
    ds5_out, drw_out = dmixcat[:, :S5_WIDTH], dmixcat[:, S5_WIDTH:]

    dy_wkv, dr_b, dkp_b, dv_b, dgate, g['rw_ln_w'], g['rw_ln_b'], g['rw_r_k'] = _tok_bwd(
        "rw_post_bwd", _f_rw_post, post_toks, post_par, [drw_out], [F32] * 5, [True] * 3, tw)
    late_exchange = _exchange_plan([shard_major[n] if n in shard_major else
                                    _to_shard_major(g[n], SHARDED[n]).astype(WIRE) for n in LATE_NAMES])
    dseqs, late_pieces = _wkv_bwd(seqs, ck, dy_wkv, late_exchange)
    pre_cots = [(dseqs[0], dr_b), (dseqs[1],), (dseqs[2], dkp_b), (dseqs[3], dv_b), (dseqs[4],), (dseqs[5],),
                (dgate,)]
    dz, *dpre = _rw_pre_bwd(z, pre_par, pre_cots, tw)
    for n, d in zip(['rw_shift_mu', 'rw_w0', 'rw_w2', 'rw_a0', 'rw_a2', 'rw_g2', 'rw_k_k', 'rw_k_a'], dpre):
        g[n] = d

    dypre, du_a, g['s5_d'], g['s5_glu_w'], g['s5_glu_b'] = _tok_bwd(
        "s5_post_bwd", _f_s5_post, [ypre, u], s5_par, [ds5_out], [F32, F32], [True] * 3, tw)
    dxs = _s5_expand("d_s5_x", dypre, c_blk)
    dc_blk = _s5_block_grad("d_s5_c", dypre, xs)
    dbu, dlam_row = _s5_scan_bwd(dxs, xs, lam_row, tm)
    du_b = _s5_contract("d_s5_u", dbu, b_blk)
    db_blk = _s5_block_grad("d_s5_b", u, dbu)
    dbuild = _small_bwd("s5_build_bwd", _f_s5_build, build_ins, (db_blk, dc_blk))
    lam_cots = (dlam_row[:, :S5_LANES].reshape(gp), dlam_row[:, S5_LANES:].reshape(gp),
                dbuild[0].reshape(gp), dbuild[1].reshape(gp))
    g['s5_lam_re'], g['s5_lam_im'], g['s5_log_step'] = _small_bwd("s5_lam_bwd", _f_s5_lam, lam_ins, lam_cots)
    from_t = lambda a, perm: a.reshape(S5_GROUP, S5_GROUPS, S5_STATE).transpose(perm)
    g['s5_b_re'], g['s5_b_im'] = from_t(dbuild[2], (1, 2, 0)), from_t(dbuild[3], (1, 2, 0))
    g['s5_c_re'], g['s5_c_im'] = from_t(dbuild[4], (1, 0, 2)), from_t(dbuild[5], (1, 0, 2))

    dproj = jnp.concatenate([(du_a + du_b).astype(BF16), dz.astype(BF16)], axis=1)
    g['w_in'] = _mm("d_w_in", xn, dproj, 'tn', out_dtype=WIRE)
    dxn = _mm("d_xn", dproj, wf['w_in'], 'nt')
    grad_x, g['norm_mix'] = _tok_bwd("norm_in_bwd", _f_norm_in, [x], [norm_mix], [dxn], [F32], [True], tw,
                                     add_to=(0, dx_a))
    return loss[0, 0], grad_x, g, late_pieces


def _step(x, p, target, w, m, v):
    shards = {n: _mat(w[n]).astype(BF16) for n in SHARDED_NAMES}
    early = _run_plan("gather_early", _gather_plan([shards[n] for n in EARLY_NAMES]))
    wf = {n: _from_shard_major(a, SHARDED[n]) for n, a in zip(EARLY_NAMES, early)}
    ws = {n: w[n] for n in SMALL_NAMES}

    loss, grad_x, g, late_pieces = _local_step(x[0], p[0, 0], target[0], wf, ws, shards)

    early_plan = _exchange_plan([_to_shard_major(g[n], SHARDED[n]).astype(WIRE) for n in EARLY_NAMES],
                                _pack_small({n: g[n] for n in SMALL_NAMES}, tail=loss))
    *early_pieces, by_dev = _run_plan("exchange_early", early_plan)
    pieces = dict(zip(LATE_NAMES + EARLY_NAMES, list(late_pieces) + early_pieces))
    halves = [_ew("add_devices_" + n, _sum_slots, [pieces[n]], [pieces[n].shape[1:]])[0] for n in SHARDED_NAMES]
    (small_piece,) = _ew("add_devices_small", _sum_slots, [by_dev], [by_dev.shape[1:]])
    both, small_g = _share_cores(halves, small_piece)

    kinds = [{}, {}, {}, {}]
    for n, gn in zip(SHARDED_NAMES, both):
        shard = _mat(w[n]).shape
        res = _ew("adamw_" + n, _adamw, [gn.reshape(shard), _mat(w[n]), _mat(m[n]), _mat(v[n])], [shard] * 3)
        for kind, a in zip(kinds, [gn] + list(res)):
            kind[n] = a.reshape(w[n].shape)
    flat = (8 * SMALL_ROWS, PACK_COLS)
    packed = [_pack_small({n: d[n] for n in SMALL_NAMES}).reshape(flat) for d in (w, m, v)]
    small_res = _ew("adamw_small", _adamw, [small_g.reshape(flat)] + packed, [flat] * 3)
    small_shapes = {n: w[n].shape for n in SMALL_NAMES}
    for kind, a in zip(kinds, [small_g] + list(small_res)):
        kind.update(_unpack_small(a, small_shapes))
    total = small_g.reshape(-1)[sum(math.prod(s) for s in small_shapes.values())]
    return (total, grad_x[None], *[kind[n] for kind in kinds for n in WEIGHT_NAMES])


def kernel(x, p, norm_mix, w_in, s5_lam_re, s5_lam_im, s5_log_step, s5_b_re, s5_b_im, s5_c_re, s5_c_im, s5_d, s5_glu_w, s5_glu_b, rw_shift_mu, rw_w0, rw_w2, rw_a0, rw_a2, rw_g2, rw_k_k, rw_k_a, rw_r_k, rw_ln_w, rw_ln_b, w_out, norm_ffn, ffn_w1, ffn_w3, ffn_w2, norm_ple, ple_gate_w, ple_up_w, final_norm, loss_target, m_norm_mix, m_w_in, m_s5_lam_re, m_s5_lam_im, m_s5_log_step, m_s5_b_re, m_s5_b_im, m_s5_c_re, m_s5_c_im, m_s5_d, m_s5_glu_w, m_s5_glu_b, m_rw_shift_mu, m_rw_w0, m_rw_w2, m_rw_a0, m_rw_a2, m_rw_g2, m_rw_k_k, m_rw_k_a, m_rw_r_k, m_rw_ln_w, m_rw_ln_b, m_w_out, m_norm_ffn, m_ffn_w1, m_ffn_w3, m_ffn_w2, m_norm_ple, m_ple_gate_w, m_ple_up_w, m_final_norm, v_norm_mix, v_w_in, v_s5_lam_re, v_s5_lam_im, v_s5_log_step, v_s5_b_re, v_s5_b_im, v_s5_c_re, v_s5_c_im, v_s5_d, v_s5_glu_w, v_s5_glu_b, v_rw_shift_mu, v_rw_w0, v_rw_w2, v_rw_a0, v_rw_a2, v_rw_g2, v_rw_k_k, v_rw_k_a, v_rw_r_k, v_rw_ln_w, v_rw_ln_b, v_w_out, v_norm_ffn, v_ffn_w1, v_ffn_w3, v_ffn_w2, v_norm_ple, v_ple_gate_w, v_ple_up_w, v_final_norm):
    args = dict(locals())
    w = {n: args[n] for n in WEIGHT_NAMES}
    m = {n: args["m_" + n] for n in WEIGHT_NAMES}
    v = {n: args["v_" + n] for n in WEIGHT_NAMES}
    return _step(x, p, loss_target, w, m, v)
```

```python
import functools
import math
from typing import Any, Callable, NamedTuple, Sequence

import jax
import jax.numpy as jnp
from jax import lax
from jax.experimental import pallas as pl
from jax.experimental.pallas import tpu as pltpu

F32 = jnp.float32
BF16 = jnp.bfloat16
MESH = pl.DeviceIdType.MESH

S5_WIDTH = 512
RWKV_WIDTH = 512
S5_GROUP = 16
S5_GROUPS = 32
S5_STATE = 64
S5_LANES = S5_GROUPS * S5_STATE
S5_TILE_GROUPS = 8
S5_TILES = S5_GROUPS // S5_TILE_GROUPS
S5_TILE_CH = S5_TILE_GROUPS * S5_GROUP
S5_TILE_LANES = S5_TILE_GROUPS * S5_STATE
HEAD = 64
HEADS = 8
DECAY_LORA = 64
AAA_LORA = 64
GATE_LORA = 128
FFN_HIDDEN = 2816
FFN_TILE = FFN_HIDDEN // 2
RMS_EPS = 1e-6
GN_EPS = 64e-5
L2_EPS = 1e-12
ADAM_LR = 0.001
ADAM_B1 = 0.9
ADAM_B2 = 0.999
ADAM_EPS = 1e-08
ADAM_WD = 0.01
ADAM_STEP = 10

WKV_CHUNK = 64
SCAN_UNROLL = 4
WIRE = jnp.bfloat16
WKV_PASSES = 1
VMEM_LIMIT_BYTES = 48 * 1024 * 1024
LANE = 128
PACK_COLS = 1024
SMALL_ROWS = 24

WEIGHT_NAMES = ['norm_mix', 'w_in', 's5_lam_re', 's5_lam_im', 's5_log_step', 's5_b_re', 's5_b_im', 's5_c_re',
                's5_c_im', 's5_d', 's5_glu_w', 's5_glu_b', 'rw_shift_mu', 'rw_w0', 'rw_w2', 'rw_a0', 'rw_a2',
                'rw_g2', 'rw_k_k', 'rw_k_a', 'rw_r_k', 'rw_ln_w', 'rw_ln_b', 'w_out', 'norm_ffn', 'ffn_w1',
                'ffn_w3', 'ffn_w2', 'norm_ple', 'ple_gate_w', 'ple_up_w', 'final_norm']
SHARDED = {'w_in': 1, 's5_glu_w': 0, 'rw_w2': 1, 'rw_a2': 1, 'rw_g2': 1, 'w_out': 0, 'ffn_w1': 1, 'ffn_w3': 1,
           'ffn_w2': 0, 'ple_gate_w': 0, 'ple_up_w': 1}
SHARDED_NAMES = [n for n in WEIGHT_NAMES if n in SHARDED]
LATE_NAMES = ['w_out', 'ffn_w1', 'ffn_w3', 'ffn_w2', 'ple_gate_w', 'ple_up_w']
EARLY_NAMES = [n for n in SHARDED_NAMES if n not in LATE_NAMES]
LATE_GATHER = {'s5_scan': ['w_out', 'ple_gate_w', 'ple_up_w'], 'wkv_fwd': ['ffn_w1', 'ffn_w3'], 'ffn_up': ['ffn_w2']}
SMALL_NAMES = [n for n in WEIGHT_NAMES if n not in SHARDED]


def _params(sem=None):
    return pltpu.CompilerParams(dimension_semantics=sem, vmem_limit_bytes=VMEM_LIMIT_BYTES)


def _tile(n, target):
    best = None
    for d in range(LANE, min(n, target) + 1, LANE):
        if n % d == 0:
            best = d
    return n if best is None else best


_NN = (((1,), (0,)), ((), ()))
_NT = (((1,), (1,)), ((), ()))
_TN = (((0,), (0,)), ((), ()))


def _split(a):
    a = a.astype(F32)
    hi = a.astype(BF16)
    return hi, (a - hi.astype(F32)).astype(BF16)


def _dg(a, b, dims, passes):
    dg = lambda p, q: lax.dot_general(p, q, dims, preferred_element_type=F32)
    if passes == 1:
        return dg(a.astype(BF16), b.astype(BF16))
    bh, bl = _split(b)
    if passes == 2:
        return dg(a.astype(BF16), bh) + dg(a.astype(BF16), bl)
    ah, al = _split(a)
    return dg(ah, bh) + (dg(ah, bl) + dg(al, bh))


_DOT_BWD = {_NN: (("g", "b", _NT), ("a", "g", _TN)),
            _NT: (("g", "b", _NN), ("g", "a", _TN)),
            _TN: (("b", "g", _NT), ("a", "g", _NN))}


@functools.partial(jax.custom_vjp, nondiff_argnums=(2, 3))
def _dot(a, b, dims, passes):
    return _dg(a, b, dims, passes)


def _dot_fwd(a, b, dims, passes):
    return _dg(a, b, dims, passes), (a, b)


def _dot_bwd(dims, passes, res, g):
    env = {"a": res[0], "b": res[1], "g": g}
    return tuple(_dg(env[p], env[q], d, passes) for p, q, d in _DOT_BWD[dims])


_dot.defvjp(_dot_fwd, _dot_bwd)


def _bdot(x, w):
    return _dot(x, w, _NN, 1)


@jax.custom_vjp
def _shift_down(z):
    return pltpu.roll(z, 1, 0)


def _shift_down_fwd(z):
    return pltpu.roll(z, 1, 0), None


def _shift_down_bwd(_, g):
    return (pltpu.roll(g, g.shape[0] - 1, 0),)


_shift_down.defvjp(_shift_down_fwd, _shift_down_bwd)


def _head_sum_impl(x):
    r = lax.broadcasted_iota(jnp.int32, (LANE, LANE), 0) // HEAD
    c = lax.broadcasted_iota(jnp.int32, (LANE, LANE), 1) // HEAD
    ones = (r == c).astype(BF16)
    hi, lo = _split(x)
    dg = lambda p: lax.dot_general(p, ones, _NN, preferred_element_type=F32)
    tiles = [slice(j, j + LANE) for j in range(0, x.shape[1], LANE)]
    return jnp.concatenate([dg(hi[:, s]) + dg(lo[:, s]) for s in tiles], axis=1)


@jax.custom_vjp
def _head_sum(x):
    return _head_sum_impl(x)


_head_sum.defvjp(lambda x: (_head_sum_impl(x), None), lambda _, g: (_head_sum_impl(g),))


def _mm(name, a, b, mode, out_dtype=F32, precise=False, tm=1024, tn=1024, tk=1536, plan=None):
    if mode == 'nn':
        (m, k), (_, n) = a.shape, b.shape
    elif mode == 'nt':
        (m, k), (n, _) = a.shape, b.shape
    else:
        (k, m), (_, n) = a.shape, b.shape
    tm, tn, tk = _tile(m, tm), _tile(n, tn), _tile(k, tk)
    nm, nn, nk = m // tm, n // tn, k // tk
    dims = {'nn': _NN, 'nt': _NT, 'tn': _TN}[mode]
    plan = _NO_PLAN if plan is None else plan
    parts, plan_in_specs, plan_out_shape, plan_out_specs, plan_sems = _carry(plan, 2, 1)

    def body(*refs):
        a_ref, b_ref, o_ref = refs[0], refs[1], refs[2 + len(plan.ins)]
        acc_ref = refs[3 + len(plan.ins) + len(plan.out_shape)]
        i, j, kk = pl.program_id(0), pl.program_id(1), pl.program_id(2)

        if plan is not _NO_PLAN:
            pl.when((i == 0) & (j == 0) & (kk == 0))(lambda: plan.start(*parts(refs)))

        @pl.when(kk == 0)
        def _():
            acc_ref[...] = jnp.zeros_like(acc_ref)

        acc_ref[...] += _dg(a_ref[...], b_ref[...], dims, 3 if precise else 1)

        @pl.when(kk == nk - 1)
        def _():
            o_ref[...] = acc_ref[...].astype(o_ref.dtype)

        if plan is not _NO_PLAN:
            pl.when((i == nm - 1) & (j == nn - 1) & (kk == nk - 1))(lambda: plan.wait(*parts(refs)))

    if mode == 'tn':
        a_spec = pl.BlockSpec((tk, tm), lambda i, j, l: (l, i))
    else:
        a_spec = pl.BlockSpec((tm, tk), lambda i, j, l: (i, l))
    if mode == 'nt':
        b_spec = pl.BlockSpec((tn, tk), lambda i, j, l: (j, l))
    else:
        b_spec = pl.BlockSpec((tk, tn), lambda i, j, l: (l, j))
    res = pl.pallas_call(
        body, name=name,
        out_shape=[jax.ShapeDtypeStruct((m, n), out_dtype)] + plan_out_shape,
        grid=(nm, nn, nk),
        in_specs=[a_spec, b_spec] + plan_in_specs,
        out_specs=[pl.BlockSpec((tm, tn), lambda i, j, l: (i, j))] + plan_out_specs,
        scratch_shapes=[pltpu.VMEM((tm, tn), F32)] + plan_sems,
        compiler_params=_params(("parallel", "parallel", "arbitrary") if plan is _NO_PLAN else ("arbitrary",) * 3),
    )(a, b, *plan.ins)
    return res[0] if plan is _NO_PLAN else (res[0], res[1:])


def _mm_tiles(name, a, b, mode, out_shape, grid, a_spec, b_spec, o_spec, add=None, out_dtype=F32):
    dims = {'nn': _NN, 'nt': _NT, 'tn': _TN}[mode]
    nk = grid[2]
    extra = [] if add is None else [add]

    def body(a_ref, b_ref, *refs):
        o_ref, acc_ref = refs[len(extra):]
        kk = pl.program_id(2)

        @pl.when(kk == 0)
        def _():
            acc_ref[...] = refs[0][...].astype(F32) if extra else jnp.zeros_like(acc_ref)

        acc_ref[...] += _dg(a_ref[...], b_ref[...], dims, 1)

        @pl.when(kk == nk - 1)
        def _():
            o_ref[...] = acc_ref[...].astype(o_ref.dtype)

    return pl.pallas_call(
        body, name=name,
        out_shape=jax.ShapeDtypeStruct(out_shape, out_dtype),
        grid=grid, in_specs=[a_spec, b_spec] + [o_spec] * len(extra), out_specs=o_spec,
        scratch_shapes=[pltpu.VMEM(o_spec.block_shape, F32)],
        compiler_params=_params(("parallel", "parallel", "arbitrary")),
    )(a, b, *extra)


def _s5_expand(name, u, blk, tm=2048):
    t = u.shape[0]
    tm = min(tm, t)
    ch, ln, nt = S5_TILE_CH, S5_TILE_LANES, S5_TILES
    return _mm_tiles(name, u, blk, 'nn', (t, 2 * S5_LANES), (t // tm, 2 * nt, 1),
                     pl.BlockSpec((tm, ch), lambda i, j, l: (i, j % nt)),
                     pl.BlockSpec((ch, ln), lambda i, j, l: (j % nt, j // nt)),
                     pl.BlockSpec((tm, ln), lambda i, j, l: (i, j)))


def _s5_contract(name, x, blk, tm=2048, add=None, out_dtype=F32):
    t = x.shape[0]
    tm = min(tm, t)
    ch, ln, nt = S5_TILE_CH, S5_TILE_LANES, S5_TILES
    return _mm_tiles(name, x, blk, 'nt', (t, S5_WIDTH), (t // tm, nt, 2),
                     pl.BlockSpec((tm, ln), lambda i, j, l: (i, j + nt * l)),
                     pl.BlockSpec((ch, ln), lambda i, j, l: (j, l)),
                     pl.BlockSpec((tm, ch), lambda i, j, l: (i, j)), add=add, out_dtype=out_dtype)


def _s5_block_grad(name, u, x, tk=2048):
    t = u.shape[0]
    tk = min(tk, t)
    ch, ln, nt = S5_TILE_CH, S5_TILE_LANES, S5_TILES
    return _mm_tiles(name, u, x, 'tn', (S5_WIDTH, 2 * ln), (nt, 2, t // tk),
                     pl.BlockSpec((tk, ch), lambda i, j, l: (l, i)),
                     pl.BlockSpec((tk, ln), lambda i, j, l: (l, i + nt * j)),
                     pl.BlockSpec((ch, ln), lambda i, j, l: (i, j)))


def _full_spec(p):
    nd = p.ndim
    return pl.BlockSpec(p.shape, lambda i, nd=nd: (0,) * nd)


def _tok_fwd(name, fn, toks, params, outs, tm):
    t = toks[0].shape[0]
    nt, npar = len(toks), len(params)

    def body(*refs):
        tv = [r[...].astype(F32) for r in refs[:nt]]
        pv = [r[...].astype(F32) for r in refs[nt:nt + npar]]
        res = fn(*tv, *pv)
        for r, v in zip(refs[nt + npar:], res):
            r[...] = v.astype(r.dtype)

    return pl.pallas_call(
        body, name=name,
        out_shape=[jax.ShapeDtypeStruct((t, w), d) for w, d in outs],
        grid=(t // tm,),
        in_specs=[pl.BlockSpec((tm, a.shape[1]), lambda i: (i, 0)) for a in toks] + [_full_spec(p) for p in params],
        out_specs=[pl.BlockSpec((tm, w), lambda i: (i, 0)) for w, _ in outs],
        compiler_params=_params(("parallel",)),
    )(*toks, *params)


def _tok_bwd(name, fn, toks, params, cots, dtok, dpar, tm, acc_out=None, add_to=None):
    t = toks[0].shape[0]
    nt, npar = len(toks), len(params)
    cot_arrays = [c for c in cots if c is not None]
    ncot = len(cot_arrays)
    extra = [] if add_to is None else [add_to[1]]
    dtok_idx = [i for i, d in enumerate(dtok) if d is not None]
    dpar_idx = [i for i, d in enumerate(dpar) if d]

    def body(*refs):
        pos = 0
        tin = refs[pos:pos + nt]; pos += nt
        pin = refs[pos:pos + npar]; pos += npar
        cin = refs[pos:pos + ncot]; pos += ncot
        ein = refs[pos:pos + len(extra)]; pos += len(extra)
        dto = refs[pos:pos + len(dtok_idx)]; pos += len(dtok_idx)
        dpo = refs[pos:pos + len(dpar_idx)]; pos += len(dpar_idx)
        acc = refs[pos] if acc_out is not None else None
        first = pl.program_id(0) == 0

        tv = [r[...].astype(F32) for r in tin]
        pv = [r[...].astype(F32) for r in pin]
        res, vjp = jax.vjp(fn, *tv, *pv)
        cit = iter(cin)
        cs = tuple(jnp.ones_like(o) if c is None else next(cit)[...].astype(F32) for c, o in zip(cots, res))
        g = vjp(cs)
        for r, i in zip(dto, dtok_idx):
            v = g[i]
            if add_to is not None and add_to[0] == i:
                v = v + ein[0][...].astype(F32)
            r[...] = v.astype(r.dtype)

        @pl.when(first)
        def _():
            for r in dpo:
                r[...] = jnp.zeros_like(r)
            if acc is not None:
                acc[...] = jnp.zeros_like(acc)

        for r, i in zip(dpo, dpar_idx):
            r[...] += g[nt + i]
        if acc is not None:
            acc[...] += res[acc_out]

    out_shape = [jax.ShapeDtypeStruct(toks[i].shape, dtok[i]) for i in dtok_idx]
    out_shape += [jax.ShapeDtypeStruct(params[i].shape, F32) for i in dpar_idx]
    out_specs = [pl.BlockSpec((tm, toks[i].shape[1]), lambda i_: (i_, 0)) for i in dtok_idx]
    out_specs += [_full_spec(params[i]) for i in dpar_idx]
    if acc_out is not None:
        out_shape.append(jax.ShapeDtypeStruct((1, 1), F32))
        out_specs.append(pl.BlockSpec((1, 1), lambda i_: (0, 0)))
    tok_spec = lambda a: pl.BlockSpec((tm, a.shape[1]), lambda i_: (i_, 0))
    return pl.pallas_call(
        body, name=name,
        out_shape=out_shape,
        grid=(t // tm,),
        in_specs=[tok_spec(a) for a in toks] + [_full_spec(p) for p in params]
        + [tok_spec(c) for c in cot_arrays] + [tok_spec(e) for e in extra],
        out_specs=out_specs,
        compiler_params=_params(("arbitrary",)),
    )(*toks, *params, *cot_arrays, *extra)


def _small_fwd(name, fn, ins, outs):
    n = len(ins)

    def body(*refs):
        res = fn(*[r[...] for r in refs[:n]])
        for r, v in zip(refs[n:], res):
            r[...] = v.astype(r.dtype)

    return pl.pallas_call(
        body, name=name,
        out_shape=[jax.ShapeDtypeStruct(s, d) for s, d in outs],
        compiler_params=_params(),
    )(*ins)


def _small_bwd(name, fn, ins, cots):
    n = len(ins)

    def body(*refs):
        _, vjp = jax.vjp(fn, *[r[...] for r in refs[:n]])
        g = vjp(tuple(r[...] for r in refs[n:n + len(cots)]))
        for r, v in zip(refs[n + len(cots):], g):
            r[...] = v

    return pl.pallas_call(
        body, name=name,
        out_shape=[jax.ShapeDtypeStruct(a.shape, F32) for a in ins],
        compiler_params=_params(),
    )(*ins, *cots)


def _rms(x, g):
    return x * lax.rsqrt(jnp.mean(x * x, axis=-1, keepdims=True) + RMS_EPS) * g


def _f_norm_in(x, g):
    return (_rms(x, g),)


def _f_mix_res(x, mixed, g):
    h1 = x + mixed
    return h1, _rms(h1, g)


def _f_ffn_act(a13):
    a1, a3 = a13[:, :FFN_HIDDEN], a13[:, FFN_HIDDEN:]
    return (jax.nn.silu(a1) * a3,)


def _f_ffn_res(h1, ffo, g):
    h2 = h1 + ffo
    return h2, _rms(h2, g)


def _f_loss(h2, gpre, pu, target, g):
    h3 = h2 + jax.nn.sigmoid(gpre) * pu
    y = _rms(h3, g)
    err = jnp.square(y - target)
    return (0.5 * jnp.sum(jnp.mean(err, axis=-1, keepdims=True), axis=0, keepdims=True),)


def _f_s5_post(ypre, u, d, glu_w, glu_b):
    z = jax.nn.gelu(ypre + u * d)
    return (z * jax.nn.sigmoid(_bdot(z, glu_w) + glu_b),)


def _softplus(x):
    return jnp.maximum(x, 0.0) + jnp.log(1.0 + jnp.exp(-jnp.abs(x)))


def _f_rw_pre(z, carry, shift_mu, w0, w2, a0, a2, g2, k_k, k_a):
    rw = RWKV_WIDTH
    first_row = lax.broadcasted_iota(jnp.int32, z.shape, 0) == 0
    prev = jnp.where(first_row, carry, _shift_down(z))
    zs = z + (prev - z) * shift_mu
    o1, o2 = 3 * rw + DECAY_LORA, 3 * rw + DECAY_LORA + AAA_LORA
    r, k, v = zs[:, :rw], zs[:, rw:2 * rw], zs[:, 2 * rw:3 * rw]
    wl, al, gl = zs[:, 3 * rw:o1], zs[:, o1:o2], zs[:, o2:]
    w = -_softplus(-(w0 + _bdot(jnp.tanh(wl), w2))) - 0.5
    log_decay = -jnp.exp(w)
    a = jax.nn.sigmoid(a0 + _bdot(al, a2))
    g = _bdot(jax.nn.sigmoid(gl), g2)
    kk = k * k_k
    norm = jnp.sqrt(_head_sum(kk * kk))
    kk = kk / jnp.maximum(norm, L2_EPS)
    kp = k * (1.0 + (a - 1.0) * k_a)
    return r, log_decay, kp, v, -kk, kk * a, g


def _f_rw_post(y, r, kp, v, g, ln_w, ln_b, r_k):
    yc = y - _head_sum(y) * (1.0 / HEAD)
    var = _head_sum(yc * yc) * (1.0 / HEAD)
    yn = yc * lax.rsqrt(var + GN_EPS) * ln_w + ln_b
    bonus = _head_sum(r * kp * r_k) * v
    return ((yn + bonus) * g,)


def _f_s5_lam(lam_re, lam_im, log_step):
    step = jnp.exp(log_step)
    dr, di = lam_re * step, lam_im * step
    e = jnp.exp(dr)
    lbr, lbi = e * jnp.cos(di), e * jnp.sin(di)
    nr, ni = lbr - 1.0, lbi
    den = lam_re * lam_re + lam_im * lam_im
    return lbr, lbi, (nr * lam_re + ni * lam_im) / den, (ni * lam_re - nr * lam_im) / den


def _f_s5_build(coef_r, coef_i, btr, bti, ctr, cti):
    bbr = coef_r * btr - coef_i * bti
    bbi = coef_r * bti + coef_i * btr
    shape = (S5_WIDTH, S5_TILE_LANES)
    rows = (lax.broadcasted_iota(jnp.int32, shape, 0) % S5_TILE_CH) // S5_GROUP
    cols = lax.broadcasted_iota(jnp.int32, shape, 1) // S5_STATE
    mask = (rows == cols).astype(F32)

    def blocks(m):
        per_tile = [m[:, S5_TILE_LANES * i:S5_TILE_LANES * (i + 1)] for i in range(S5_TILES)]
        return jnp.concatenate([t for t in per_tile for _ in range(S5_TILE_GROUPS)], axis=0) * mask

    return (jnp.concatenate([blocks(bbr), blocks(bbi)], axis=1),
            jnp.concatenate([blocks(ctr), -blocks(cti)], axis=1))


HALO = 8


def _rw_pre_specs(z, params, tm, order):
    halo_blocks = tm // HALO
    return ([pl.BlockSpec((tm, z.shape[1]), lambda i: (order(i), 0)),
             pl.BlockSpec((HALO, z.shape[1]), lambda i: (jnp.maximum(order(i) * halo_blocks - 1, 0), 0))]
            + [_full_spec(p) for p in params])


def _rw_pre_fwd(z, params, tm):
    t = z.shape[0]
    npar = len(params)

    def body(z_ref, halo_ref, *refs):
        carry = jnp.where(pl.program_id(0) == 0, 0.0, halo_ref[pl.ds(HALO - 1, 1), :])
        res = _f_rw_pre(z_ref[...], carry, *[r[...].astype(F32) for r in refs[:npar]])
        for r, v in zip(refs[npar:], res):
            r[...] = v

    return pl.pallas_call(
        body, name="rw_pre",
        out_shape=[jax.ShapeDtypeStruct((t, RWKV_WIDTH), F32)] * 7,
        grid=(t // tm,),
        in_specs=_rw_pre_specs(z, params, tm, lambda i: i),
        out_specs=[pl.BlockSpec((tm, RWKV_WIDTH), lambda i: (i, 0))] * 7,
        compiler_params=_params(("parallel",)),
    )(z, z, *params)


def _rw_pre_bwd(z, params, cots, tm):
    t = z.shape[0]
    nt = t // tm
    npar = len(params)
    order = lambda i: nt - 1 - i
    flat_cots = [a for group in cots for a in group]
    ncot = len(flat_cots)

    def body(z_ref, halo_ref, *refs):
        pin, cin = refs[:npar], list(refs[npar:npar + ncot])
        dz_ref = refs[npar + ncot]
        dpo = refs[npar + ncot + 1:npar + ncot + 1 + npar]
        dcarry_ref = refs[npar + ncot + 1 + npar]
        i = pl.program_id(0)

        @pl.when(i == 0)
        def _():
            dcarry_ref[...] = jnp.zeros_like(dcarry_ref)
            for r in dpo:
                r[...] = jnp.zeros_like(r)

        carry = jnp.where(i == nt - 1, 0.0, halo_ref[pl.ds(HALO - 1, 1), :])
        _, vjp = jax.vjp(_f_rw_pre, z_ref[...], carry, *[r[...].astype(F32) for r in pin])
        g = vjp(tuple(sum(cin.pop(0)[...] for _ in group) for group in cots))
        last_row = lax.broadcasted_iota(jnp.int32, z_ref.shape, 0) == tm - 1
        dz_ref[...] = (g[0] + jnp.where(last_row, dcarry_ref[...], 0.0)).astype(dz_ref.dtype)
        dcarry_ref[...] = g[1]
        for r, v in zip(dpo, g[2:]):
            r[...] += v

    tok = lambda w: pl.BlockSpec((tm, w), lambda i: (order(i), 0))
    return pl.pallas_call(
        body, name="rw_pre_bwd",
        out_shape=[jax.ShapeDtypeStruct(z.shape, BF16)] + [jax.ShapeDtypeStruct(p.shape, F32) for p in params],
        grid=(nt,),
        in_specs=_rw_pre_specs(z, params, tm, order) + [tok(RWKV_WIDTH)] * ncot,
        out_specs=[tok(z.shape[1])] + [_full_spec(p) for p in params],
        scratch_shapes=[pltpu.VMEM((1, z.shape[1]), F32)],
        compiler_params=_params(("arbitrary",)),
    )(z, z, *params, *flat_cots)


def _s5_scan(bu, lam, tm, plan):
    t, w = bu.shape
    h = w // 2
    nt = t // tm
    parts, plan_in_specs, plan_out_shape, plan_out_specs, plan_sems = _carry(plan, 2, 1)

    def body(*refs):
        bu_ref, lam_ref = refs[:2]
        xb_ref = refs[2 + len(plan.ins)]
        carry_ref = refs[3 + len(plan.ins) + len(plan.out_shape)]
        x_ref, refs = refs[-1], refs[:-1]

        @pl.when(pl.program_id(0) == 0)
        def _():
            carry_ref[...] = jnp.zeros_like(carry_ref)
            plan.start(*parts(refs))

        lr, li = lam_ref[:, :h], lam_ref[:, h:]

        def step(s, c):
            cr, ci = c
            row = pl.ds(s, 1)
            nr = lr * cr - li * ci + bu_ref[row, :h]
            ni = lr * ci + li * cr + bu_ref[row, h:]
            x_ref[row, :h] = nr
            x_ref[row, h:] = ni
            return nr, ni

        cr, ci = lax.fori_loop(0, tm, step, (carry_ref[:, :h], carry_ref[:, h:]), unroll=SCAN_UNROLL)
        carry_ref[:, :h] = cr
        carry_ref[:, h:] = ci
        xb_ref[...] = x_ref[...].astype(BF16)

        @pl.when(pl.program_id(0) == nt - 1)
        def _():
            plan.wait(*parts(refs))

    spec = pl.BlockSpec((tm, w), lambda i: (i, 0))
    res = pl.pallas_call(
        body, name="s5_scan",
        out_shape=[jax.ShapeDtypeStruct((t, w), BF16)] + plan_out_shape,
        grid=(nt,),
        in_specs=[spec, pl.BlockSpec((1, w), lambda i: (0, 0))] + plan_in_specs,
        out_specs=[spec] + plan_out_specs,
        scratch_shapes=[pltpu.VMEM((1, w), F32)] + plan_sems + [pltpu.VMEM((tm, w), F32)],
        compiler_params=_params(("arbitrary",)),
    )(bu, lam, *plan.ins)
    return res[0], res[1:]


def _s5_scan_bwd(dx, xb, lam, tm):
    t, w = dx.shape
    h = w // 2
    nt = t // tm
    halo = BF16_SUBLANES

    rows8 = 8

    def body(dx_ref, xb_ref, halo_ref, lam_ref, dbu_out_ref, dlam_ref, carry_ref, dbu_ref, xp_ref):
        @pl.when(pl.program_id(0) == 0)
        def _():
            carry_ref[...] = jnp.zeros_like(carry_ref)
            dlam_ref[...] = jnp.zeros_like(dlam_ref)

        lr, li = lam_ref[:, :h], lam_ref[:, h:]

        def step(s, c):
            cr, ci = c
            row = pl.ds(tm - 1 - s, 1)
            nr = lr * cr + li * ci + dx_ref[row, :h]
            ni = lr * ci - li * cr + dx_ref[row, h:]
            dbu_ref[row, :h] = nr
            dbu_ref[row, h:] = ni
            return nr, ni

        cr, ci = lax.fori_loop(0, tm, step, (carry_ref[:, :h], carry_ref[:, h:]), unroll=SCAN_UNROLL)
        carry_ref[:, :h] = cr
        carry_ref[:, h:] = ci
        halo_rows = lax.broadcasted_iota(jnp.int32, (halo, w), 0)
        before = jnp.sum(jnp.where(halo_rows == halo - 1, halo_ref[...].astype(F32), 0.0), axis=0, keepdims=True)
        before = jnp.where(pl.program_id(0) == nt - 1, 0.0, before)
        first_row = lax.broadcasted_iota(jnp.int32, (tm, w), 0) == 0
        xp_ref[...] = jnp.where(first_row, before, pltpu.roll(xb_ref[...].astype(F32), 1, 0))

        def accumulate(s, acc):
            ar, ai = acc
            rows = pl.ds(pl.multiple_of(s * rows8, rows8), rows8)
            gr, gi = dbu_ref[rows, :h], dbu_ref[rows, h:]
            pr, pi_ = xp_ref[rows, :h], xp_ref[rows, h:]
            return ar + (gr * pr + gi * pi_), ai + (gi * pr - gr * pi_)

        zero = jnp.zeros((rows8, h), F32)
        ar, ai = lax.fori_loop(0, tm // rows8, accumulate, (zero, zero))
        dlam_ref[:, :h] += jnp.sum(ar, axis=0, keepdims=True)
        dlam_ref[:, h:] += jnp.sum(ai, axis=0, keepdims=True)
        dbu_out_ref[...] = dbu_ref[...].astype(BF16)

    spec = pl.BlockSpec((tm, w), lambda i: (nt - 1 - i, 0))
    halo_spec = pl.BlockSpec((halo, w), lambda i: (jnp.maximum((nt - 1 - i) * (tm // halo) - 1, 0), 0))
    row_spec = pl.BlockSpec((1, w), lambda i: (0, 0))
    return pl.pallas_call(
        body, name="s5_scan_bwd",
        out_shape=[jax.ShapeDtypeStruct((t, w), BF16), jax.ShapeDtypeStruct((1, w), F32)],
        grid=(nt,),
        in_specs=[spec, spec, halo_spec, row_spec],
        out_specs=[spec, row_spec],
        scratch_shapes=[pltpu.VMEM((1, w), F32), pltpu.VMEM((tm, w), F32), pltpu.VMEM((tm, w), F32)],
        compiler_params=_params(("arbitrary",)),
    )(dx, xb, xb, lam)


def _unit_lower_inverses_impl(ns):
    c = ns[0].shape[0]
    eye = (lax.broadcasted_iota(jnp.int32, (c, c), 0) == lax.broadcasted_iota(jnp.int32, (c, c), 1)).astype(F32)
    inv = [eye + n for n in ns]
    pw = [_dg(n, n, _NN, WKV_PASSES) for n in ns]
    for _ in range(int(math.log2(c)) - 2):
        both = [_dg(jnp.concatenate([i, q], axis=0), q, _NN, WKV_PASSES) for i, q in zip(inv, pw)]
        inv = [i + q[:c] for i, q in zip(inv, both)]
        pw = [q[c:] for q in both]
    return tuple(i + _dg(i, q, _NN, WKV_PASSES) for i, q in zip(inv, pw))


@jax.custom_vjp
def _unit_lower_inverses(ns):
    return _unit_lower_inverses_impl(ns)


def _unit_lower_inverses_fwd(ns):
    inv = _unit_lower_inverses_impl(ns)
    return inv, inv


def _unit_lower_inverses_bwd(inv, g):
    left = [_dg(i, gi, _TN, WKV_PASSES) for i, gi in zip(inv, g)]
    return (tuple(_dg(q, i, _NT, WKV_PASSES) for q, i in zip(left, inv)),)


_unit_lower_inverses.defvjp(_unit_lower_inverses_fwd, _unit_lower_inverses_bwd)


def _wkv_chunks(s0, r, lw, k, v, a, b):
    c = r[0].shape[0]
    row = lax.broadcasted_iota(jnp.int32, (c, c), 0)
    col = lax.broadcasted_iota(jnp.int32, (c, c), 1)
    incl, strict = col <= row, col < row
    tri = incl.astype(F32)
    each = lambda f, *xs: [f(*t) for t in zip(*xs)]
    stack = lambda p, q: jnp.concatenate([p, q], axis=0)
    dot = lambda p, q, dims=_NN: _dot(p, q, dims, WKV_PASSES)
    lc = each(lambda l: _dot(tri, l, _NN, 2), lw)
    e_neg = each(lambda l: jnp.exp(-l), lc)
    ar = each(lambda x, z, l, w: stack(x * jnp.exp(l - w), z * jnp.exp(l)), a, r, lc, lw)
    bk = each(lambda x, z, e: stack(x * e, z * e), b, k, e_neg)
    m = each(lambda p, q: dot(p, q, _NT), ar, bk)
    mab = each(lambda q: jnp.where(strict, q[:c, :c], 0.0), m)
    mak_mrk = each(lambda q: stack(jnp.where(strict, q[:c, c:], 0.0), jnp.where(incl, q[c:, c:], 0.0)), m)
    mrb = each(lambda q: jnp.where(incl, q[c:, :c], 0.0), m)
    xy = each(lambda p, s, q, z: dot(p, s, _NT) + dot(q, z), ar, s0, mak_mrk, v)
    inv = _unit_lower_inverses(tuple(mab))
    u = each(lambda i, q: dot(i, q[:c]), inv, xy)
    y = each(lambda q, z, p: q[c:] + dot(z, p), xy, mrb, u)
    e_tot = each(lambda l: jnp.exp(jnp.sum(l, axis=0, keepdims=True)), lw)
    s1 = each(lambda s, p, z, q, e: (s + dot(stack(p, z), q, _TN)) * e, s0, u, v, bk, e_tot)
    return y, s1


def _carry(plan, n_args, n_outs):
    n_in, n_out = len(plan.ins), len(plan.out_shape)

    def parts(refs):
        base = n_args + n_in + n_outs
        return refs[n_args:n_args + n_in], refs[base:base + n_out], refs[base + n_out + 1:]

    return parts, [HBM_SPEC] * n_in, list(plan.out_shape), [HBM_SPEC] * n_out, list(plan.sems)


def _head_cols(ref):
    return tuple(ref[:, h * HEAD:(h + 1) * HEAD] for h in range(HEADS))


def _wkv_fwd(seqs, plan):
    t, w = seqs[0].shape
    c, n = WKV_CHUNK, HEAD
    nc = t // c
    parts, plan_in_specs, plan_out_shape, plan_out_specs, plan_sems = _carry(plan, 6, 2)

    def body(*refs):
        ins, (y_ref, ck_ref) = refs[:6], refs[6 + len(plan.ins):8 + len(plan.ins)]
        s_ref = refs[8 + len(plan.ins) + len(plan.out_shape)]

        @pl.when(pl.program_id(0) == 0)
        def _():
            s_ref[...] = jnp.zeros_like(s_ref)
            plan.start(*parts(refs))

        s0 = tuple(s_ref[h] for h in range(HEADS))
        ys, s1 = _wkv_chunks(s0, *[_head_cols(r) for r in ins])
        for h in range(HEADS):
            ck_ref[0, h] = s0[h]
            y_ref[:, h * n:(h + 1) * n] = ys[h]
            s_ref[h] = s1[h]

        @pl.when(pl.program_id(0) == nc - 1)
        def _():
            plan.wait(*parts(refs))

    spec = pl.BlockSpec((c, w), lambda i: (i, 0))
    res = pl.pallas_call(
        body, name="wkv_fwd",
        out_shape=[jax.ShapeDtypeStruct((t, w), F32), jax.ShapeDtypeStruct((nc, HEADS, n, n), F32)] + plan_out_shape,
        grid=(nc,),
        in_specs=[spec] * 6 + plan_in_specs,
        out_specs=[spec, pl.BlockSpec((1, HEADS, n, n), lambda i: (i, 0, 0, 0))] + plan_out_specs,
        scratch_shapes=[pltpu.VMEM((HEADS, n, n), F32)] + plan_sems,
        compiler_params=_params(("arbitrary",)),
    )(*seqs, *plan.ins)
    return res[0], res[1], res[2:]


def _wkv_bwd(seqs, ck, dy, plan):
    t, w = seqs[0].shape
    c, n = WKV_CHUNK, HEAD
    nc = t // c
    parts, plan_in_specs, plan_out_shape, plan_out_specs, plan_sems = _carry(plan, 8, 6)

    def body(*refs):
        ins, ck_ref, dy_ref = refs[:6], refs[6], refs[7]
        outs = refs[8 + len(plan.ins):14 + len(plan.ins)]
        ds_ref = refs[14 + len(plan.ins) + len(plan.out_shape)]

        @pl.when(pl.program_id(0) == 0)
        def _():
            ds_ref[...] = jnp.zeros_like(ds_ref)
            plan.start(*parts(refs))

        s0 = tuple(ck_ref[0, h] for h in range(HEADS))
        _, vjp = jax.vjp(_wkv_chunks, s0, *[_head_cols(r) for r in ins])
        g = vjp((list(_head_cols(dy_ref)), [ds_ref[h] for h in range(HEADS)]))
        for h in range(HEADS):
            ds_ref[h] = g[0][h]
            for o, d in zip(outs, g[1:]):
                o[:, h * n:(h + 1) * n] = d[h]

        @pl.when(pl.program_id(0) == nc - 1)
        def _():
            plan.wait(*parts(refs))

    spec = pl.BlockSpec((c, w), lambda i: (nc - 1 - i, 0))
    res = pl.pallas_call(
        body, name="wkv_bwd",
        out_shape=[jax.ShapeDtypeStruct((t, w), F32)] * 6 + plan_out_shape,
        grid=(nc,),
        in_specs=[spec] * 6 + [pl.BlockSpec((1, HEADS, n, n), lambda i: (nc - 1 - i, 0, 0, 0)), spec] + plan_in_specs,
        out_specs=[spec] * 6 + plan_out_specs,
        scratch_shapes=[pltpu.VMEM((HEADS, n, n), F32)] + plan_sems,
        compiler_params=_params(("arbitrary",)),
    )(*seqs, ck, dy, *plan.ins)
    return res[:6], res[6:]


def _coords():
    return lax.axis_index("x"), lax.axis_index("y"), lax.axis_index("c")


def _flip(v, f):
    return 1 - v if f else v


_CHIP_FLIPS = [(1, 0), (0, 1), (1, 1)]
_DEV_FLIPS = [(fx, fy, fc) for fx in (0, 1) for fy in (0, 1) for fc in (0, 1) if (fx, fy, fc) != (0, 0, 0)]
HBM_SPEC = pl.BlockSpec(memory_space=pl.ANY)


def _chip_peer(k, x, y):
    fx, fy = _CHIP_FLIPS[k]
    return _flip(x, fx), _flip(y, fy)


def _dev_peer(k, x, y, c):
    fx, fy, fc = _DEV_FLIPS[k]
    return _flip(x, fx), _flip(y, fy), _flip(c, fc)


def _rows_of_core(ref, core):
    h = ref.shape[-2] // 2
    rows = pl.ds(pl.multiple_of(core * h, 8), h)
    return ref.at[rows, :] if len(ref.shape) == 2 else ref.at[:, rows, :]


class _Plan(NamedTuple):
    ins: Sequence[Any]
    out_shape: Sequence[Any]
    sems: Sequence[Any]
    start: Callable
    wait: Callable


_NO_PLAN = _Plan([], [], [], lambda *_: None, lambda *_: None)


def _run_plan(name, plan):
    n_in, n_out = len(plan.ins), len(plan.out_shape)

    def body(*refs):
        parts = refs[:n_in], refs[n_in:n_in + n_out], refs[n_in + n_out:]
        plan.start(*parts)
        plan.wait(*parts)

    return pl.pallas_call(
        body, name=name, out_shape=list(plan.out_shape),
        in_specs=[HBM_SPEC] * n_in, out_specs=[HBM_SPEC] * n_out, scratch_shapes=list(plan.sems),
    )(*plan.ins)


def _gather_plan(shards):
    n = len(shards)

    def copies(srcs, outs, sems):
        send_sems, recv_sems, local_sems = sems
        x, y, c = _coords()
        me = 2 * x + y

        def remote(i, k, arriving):
            px, py = _chip_peer(k, x, y)
            return pltpu.make_async_remote_copy(
                src_ref=srcs[i], dst_ref=outs[i].at[2 * px + py if arriving else me],
                send_sem=send_sems.at[i, k], recv_sem=recv_sems.at[i, k],
                device_id=(px, py, c), device_id_type=MESH)

        own = [pltpu.make_async_copy(srcs[i], outs[i].at[me], local_sems.at[i]) for i in range(n)]
        pairs = [(i, k) for k in range(3) for i in range(n)]
        return own, [remote(i, k, False) for i, k in pairs], [remote(i, k, True) for i, k in pairs]

    return _Plan(
        ins=shards, out_shape=[jax.ShapeDtypeStruct((4,) + s.shape, s.dtype) for s in shards],
        sems=[pltpu.SemaphoreType.DMA((n, 3)), pltpu.SemaphoreType.DMA((n, 3)), pltpu.SemaphoreType.DMA((n,))],
        start=functools.partial(_start_copies, copies), wait=functools.partial(_wait_copies, copies))


def _start_copies(copies, ins, outs, sems):
    own, sends, _ = copies(ins, outs, sems)
    for cp in own + sends:
        cp.start()


def _wait_copies(copies, ins, outs, sems):
    own, sends, arrivals = copies(ins, outs, sems)
    for cp in arrivals:
        cp.wait_recv()
    for cp in sends:
        cp.wait_send()
    for cp in own:
        cp.wait()


def _exchange_plan(gs, small=None):
    n = len(gs)
    arrays = list(gs) + ([] if small is None else [small])

    def copies(srcs, outs, sems):
        send_sems, recv_sems, local_sems = sems
        x, y, c = _coords()
        me = 4 * x + 2 * y + c

        def piece(i, px, py, pc):
            if i == n:
                return srcs[i].at[4 * px + 2 * py + pc]
            return _rows_of_core(srcs[i].at[2 * px + py], pc)

        def remote(i, k, arriving):
            px, py, pc = _dev_peer(k, x, y, c)
            return pltpu.make_async_remote_copy(
                src_ref=piece(i, px, py, pc), dst_ref=outs[i].at[4 * px + 2 * py + pc if arriving else me],
                send_sem=send_sems.at[i, k], recv_sem=recv_sems.at[i, k],
                device_id=(px, py, pc), device_id_type=MESH)

        own = [pltpu.make_async_copy(piece(i, x, y, c), outs[i].at[me], local_sems.at[i]) for i in range(len(arrays))]
        pairs = [(i, k) for k in range(7) for i in range(len(arrays))]
        return own, [remote(i, k, False) for i, k in pairs], [remote(i, k, True) for i, k in pairs]

    out_shape = [jax.ShapeDtypeStruct((8, g.shape[1] // 2, g.shape[2]), g.dtype) for g in gs]
    if small is not None:
        out_shape.append(jax.ShapeDtypeStruct(small.shape, small.dtype))
    m = len(arrays)
    return _Plan(
        ins=arrays, out_shape=out_shape,
        sems=[pltpu.SemaphoreType.DMA((m, 7)), pltpu.SemaphoreType.DMA((m, 7)), pltpu.SemaphoreType.DMA((m,))],
        start=functools.partial(_start_copies, copies), wait=functools.partial(_wait_copies, copies))


def _share_cores(halves, small):
    n = len(halves)

    def body(*refs):
        srcs, small_src, outs, small_out = refs[:n], refs[n], refs[n + 1:2 * n + 1], refs[2 * n + 1]
        mine, theirs = refs[2 * n + 2:3 * n + 2], refs[3 * n + 2:4 * n + 2]
        send_sems, recv_sems, ssend, srecv, local_sems = refs[4 * n + 2:]
        x, y, c = _coords()
        me = 4 * x + 2 * y + c

        def big(i):
            return pltpu.make_async_remote_copy(
                src_ref=mine[i], dst_ref=theirs[i], send_sem=send_sems.at[i], recv_sem=recv_sems.at[i],
                device_id=(x, y, 1 - c), device_id_type=MESH)

        def tiny(k, arriving):
            px, py, pc = _dev_peer(k, x, y, c)
            return pltpu.make_async_remote_copy(
                src_ref=small_src, dst_ref=small_out.at[4 * px + 2 * py + pc if arriving else me],
                send_sem=ssend.at[k], recv_sem=srecv.at[k], device_id=(px, py, pc), device_id_type=MESH)

        small_sends = [tiny(k, False) for k in range(7)]
        own_small = pltpu.make_async_copy(small_src, small_out.at[me], local_sems.at[2 * n])
        stage = [pltpu.make_async_copy(srcs[i], mine[i], local_sems.at[i]) for i in range(n)]
        for cp in small_sends + [own_small] + stage:
            cp.start()
        sends = []
        for i in range(n):
            stage[i].wait()
            sends.append(big(i))
            sends[-1].start()
        store = [pltpu.make_async_copy(mine[i], outs[i].at[c], local_sems.at[i]) for i in range(n)]
        for cp in store:
            cp.start()
        for i in range(n):
            big(i).wait_recv()
            store.append(pltpu.make_async_copy(theirs[i], outs[i].at[1 - c], local_sems.at[n + i]))
            store[-1].start()
        for k in range(7):
            tiny(k, True).wait_recv()
        for cp in sends + small_sends:
            cp.wait_send()
        for cp in store + [own_small]:
            cp.wait()

    staged = [pltpu.VMEM(s.shape, s.dtype) for s in halves]
    res = pl.pallas_call(
        body, name="share_cores",
        out_shape=[jax.ShapeDtypeStruct((2,) + s.shape, s.dtype) for s in halves]
        + [jax.ShapeDtypeStruct((8,) + small.shape, small.dtype)],
        in_specs=[HBM_SPEC] * (n + 1), out_specs=[HBM_SPEC] * (n + 1),
        scratch_shapes=staged + staged + [
            pltpu.SemaphoreType.DMA((n,)), pltpu.SemaphoreType.DMA((n,)),
            pltpu.SemaphoreType.DMA((7,)), pltpu.SemaphoreType.DMA((7,)),
            pltpu.SemaphoreType.DMA((2 * n + 1,))],
        compiler_params=pltpu.CompilerParams(vmem_limit_bytes=VMEM_LIMIT_BYTES),
    )(*halves, small)
    return res[:n], res[n]


BF16_SUBLANES = 16


def _row_tile(n, target, step=BF16_SUBLANES):
    return max([d for d in range(step, min(n, target) + 1, step) if n % d == 0] or [n])


def _ew(name, fn, ins, outs, block_bytes=2 << 20):
    rows, cols = ins[0].shape[-2:]
    lead = max(math.prod(a.shape[:-2]) for a in ins)
    tr = _row_tile(rows, max(8, block_bytes // (4 * cols * lead)))
    n = len(ins)

    def spec(shape):
        if len(shape) == 2:
            return pl.BlockSpec((tr, cols), lambda i: (i, 0))
        return pl.BlockSpec((shape[0], tr, cols), lambda i: (0, i, 0))

    def body(*refs):
        res = fn(*[r[...] for r in refs[:n]])
        for r, v in zip(refs[n:], res):
            r[...] = v

    return pl.pallas_call(
        body, name=name,
        out_shape=[jax.ShapeDtypeStruct(s, F32) for s in outs],
        grid=(rows // tr,),
        in_specs=[spec(a.shape) for a in ins],
        out_specs=[spec(s) for s in outs],
        compiler_params=_params(("parallel",)),
    )(*ins)


def _sum_slots(a):
    total = a[0].astype(F32)
    for s in range(1, a.shape[0]):
        total = total + a[s].astype(F32)
    return (total,)


def _adamw(g, w, m, v):
    bc1 = 1.0 - ADAM_B1 ** ADAM_STEP
    bc2 = 1.0 - ADAM_B2 ** ADAM_STEP
    m_new = ADAM_B1 * m + (1.0 - ADAM_B1) * g
    v_new = ADAM_B2 * v + (1.0 - ADAM_B2) * jnp.square(g)
    delta = -ADAM_LR * ((m_new / bc1) / (jnp.sqrt(v_new / bc2) + ADAM_EPS) + ADAM_WD * w)
    return delta, m_new, v_new


def _mat(a):
    return a.reshape(a.shape[-2:])


def _to_shard_major(full, axis):
    rows, cols = full.shape
    if axis == 0:
        return full.reshape(4, rows // 4, cols)
    return full.reshape(rows, 4, cols // 4).transpose(1, 0, 2)


def _from_shard_major(a, axis):
    _, r, cs = a.shape
    if axis == 0:
        return a.reshape(4 * r, cs)
    return a.transpose(1, 0, 2).reshape(r, 4 * cs)


def _pack_small(arrays, tail=None):
    flat = [arrays[n].reshape(-1) for n in SMALL_NAMES] + ([] if tail is None else [tail.reshape(1)])
    used = sum(a.shape[0] for a in flat)
    flat.append(jnp.zeros((8 * SMALL_ROWS * PACK_COLS - used,), F32))
    return jnp.concatenate(flat).reshape(8, SMALL_ROWS, PACK_COLS)


def _unpack_small(packed, shapes):
    flat = packed.reshape(-1)
    out, off = {}, 0
    for n in SMALL_NAMES:
        size = math.prod(shapes[n])
        out[n] = flat[off:off + size].reshape(shapes[n])
        off += size
    return out


def _row(a):
    return a.reshape(1, -1)


def _local_step(x, p, target, wf, ws, late_shards):
    wf = dict(wf)
    t = x.shape[0]
    tm = min(256, t)
    tw = min(512, t)
    g = {}

    lam_re, lam_im = ws['s5_lam_re'].reshape(S5_GROUPS, S5_STATE), ws['s5_lam_im'].reshape(S5_GROUPS, S5_STATE)
    log_step = ws['s5_log_step'].reshape(S5_GROUPS, 1)
    gp = (S5_GROUPS, S5_STATE)
    lam_ins = (lam_re, lam_im, log_step)
    lbr, lbi, cfr, cfi = _small_fwd("s5_lam", _f_s5_lam, lam_ins, [(gp, F32)] * 4)
    lam_row = jnp.concatenate([_row(lbr), _row(lbi)], axis=1)
    to_t = lambda a, perm: a.reshape((S5_GROUPS,) + a.shape[-2:]).transpose(perm).reshape(S5_GROUP, S5_LANES)
    build_ins = (_row(cfr), _row(cfi), to_t(ws['s5_b_re'], (2, 0, 1)), to_t(ws['s5_b_im'], (2, 0, 1)),
                 to_t(ws['s5_c_re'], (1, 0, 2)), to_t(ws['s5_c_im'], (1, 0, 2)))
    block_shape = (S5_WIDTH, 2 * S5_TILE_LANES)
    b_blk, c_blk = _small_fwd("s5_build", _f_s5_build, build_ins, [(block_shape, F32)] * 2)

    norm_mix, norm_ffn, norm_ple = _row(ws['norm_mix']), _row(ws['norm_ffn']), _row(ws['norm_ple'])
    final_norm = _row(ws['final_norm'])
    (xn,) = _tok_fwd("norm_in", _f_norm_in, [x], [norm_mix], [(x.shape[1], BF16)], tw)
    u = _mm("proj_s5", xn, wf['w_in'][:, :S5_WIDTH], 'nn')
    z = _mm("proj_rw", xn, wf['w_in'][:, S5_WIDTH:], 'nn')

    bu = _s5_expand("s5_bu", u, b_blk)
    def late_plan(carrier):
        return _gather_plan([late_shards[n] for n in LATE_GATHER[carrier]])

    def arrived(carrier, got):
        wf.update({n: _from_shard_major(a, SHARDED[n]) for n, a in zip(LATE_GATHER[carrier], got)})

    xs, got = _s5_scan(bu, lam_row, tm, late_plan('s5_scan'))
    arrived('s5_scan', got)
    ypre = _s5_contract("s5_y", xs, c_blk)
    s5_par = [_row(ws['s5_d']), wf['s5_glu_w'], _row(ws['s5_glu_b'])]
    (s5_out,) = _tok_fwd("s5_post", _f_s5_post, [ypre, u], s5_par, [(S5_WIDTH, BF16)], tw)

    pre_par = [_row(ws['rw_shift_mu']), _row(ws['rw_w0']), wf['rw_w2'], _row(ws['rw_a0']), wf['rw_a2'],
               wf['rw_g2'], _row(ws['rw_k_k']), _row(ws['rw_k_a'])]
    r, lw, kp, v, an, bn, gate = _rw_pre_fwd(z, pre_par, tw)
    seqs = [r, lw, kp, v, an, bn]
    y_wkv, ck, got = _wkv_fwd(seqs, late_plan('wkv_fwd'))
    arrived('wkv_fwd', got)
    post_par = [_row(ws['rw_ln_w']), _row(ws['rw_ln_b']), _row(ws['rw_r_k'])]
    post_toks = [y_wkv, r, kp, v, gate]
    (rw_out,) = _tok_fwd("rw_post", _f_rw_post, post_toks, post_par, [(RWKV_WIDTH, BF16)], tw)

    mixcat = jnp.concatenate([s5_out, rw_out], axis=1)
    mixed = _mm("mix_out", mixcat, wf['w_out'], 'nn')
    h1, hn = _tok_fwd("mix_res", _f_mix_res, [x, mixed], [norm_ffn], [(x.shape[1], F32), (x.shape[1], BF16)], tw)
    w13 = jnp.concatenate([wf['ffn_w1'], wf['ffn_w3']], axis=1)
    a13, got = _mm("ffn_up", hn, w13, 'nn', out_dtype=BF16, tn=FFN_TILE, plan=late_plan('ffn_up'))
    arrived('ffn_up', got)
    (f,) = _tok_fwd("ffn_act", _f_ffn_act, [a13], [], [(FFN_HIDDEN, BF16)], tw)
    ffo = _mm("ffn_down", f, wf['ffn_w2'], 'nn')
    h2, hp = _tok_fwd("ffn_res", _f_ffn_res, [h1, ffo], [norm_ple], [(x.shape[1], F32), (x.shape[1], BF16)], tw)
    gpre = _mm("ple_gate", hp, wf['ple_gate_w'], 'nn')
    pu = _mm("ple_up", p, wf['ple_up_w'], 'nn')

    dh2, dgpre, dpu, g['final_norm'], loss = _tok_bwd(
        "loss", _f_loss, [h2, gpre, pu, target], [final_norm], [None],
        [F32, BF16, BF16, None], [True], tw, acc_out=0)
    g['ple_gate_w'] = _mm("d_ple_gate_w", hp, dgpre, 'tn', out_dtype=WIRE)
    g['ple_up_w'] = _mm("d_ple_up_w", p, dpu, 'tn', out_dtype=WIRE)
    dhp = _mm("d_hp", dgpre, wf['ple_gate_w'], 'nt')
    dh1, dffo, g['norm_ple'] = _tok_bwd("ffn_res_bwd", _f_ffn_res, [h1, ffo], [norm_ple], [dh2, dhp],
                                        [F32, BF16], [True], tw)
    g['ffn_w2'] = _mm("d_ffn_w2", f, dffo, 'tn', out_dtype=WIRE, tm=FFN_TILE)
    df = _mm("d_f", dffo, wf['ffn_w2'], 'nt', out_dtype=BF16, tn=FFN_TILE)
    (da13,) = _tok_bwd("ffn_act_bwd", _f_ffn_act, [a13], [], [df], [BF16], [], tw)
    dw13 = _mm("d_ffn_w13", hn, da13, 'tn', out_dtype=WIRE, tn=FFN_TILE)
    dw13 = dw13.reshape(dw13.shape[0], 8, FFN_HIDDEN // 4).transpose(1, 0, 2)
    shard_major = {'ffn_w1': dw13[:4], 'ffn_w3': dw13[4:]}
    dhn = _mm("d_hn", da13, w13, 'nt')
    dx_a, dmixed, g['norm_ffn'] = _tok_bwd("mix_res_bwd", _f_mix_res, [x, mixed], [norm_ffn], [dh1, dhn],
                                           [F32, BF16], [True], tw)
    g['w_out'] = _mm("d_w_out", mixcat, dmixed, 'tn', out_dtype=WIRE)
    dmixcat = _mm("d_mixcat", dmixed, wf['w_out'], 'nt')
    ds5_out, drw_out = dmixcat[:, :S5_WIDTH], dmixcat[:, S5_WIDTH:]

    dy_wkv, dr_b, dkp_b, dv_b, dgate, g['rw_ln_w'], g['rw_ln_b'], g['rw_r_k'] = _tok_bwd(
        "rw_post_bwd", _f_rw_post, post_toks, post_par, [drw_out], [F32] * 5, [True] * 3, tw)
    late_exchange = _exchange_plan([shard_major[n] if n in shard_major else
                                    _to_shard_major(g[n], SHARDED[n]).astype(WIRE) for n in LATE_NAMES])
    dseqs, late_pieces = _wkv_bwd(seqs, ck, dy_wkv, late_exchange)
    pre_cots = [(dseqs[0], dr_b), (dseqs[1],), (dseqs[2], dkp_b), (dseqs[3], dv_b), (dseqs[4],), (dseqs[5],),
                (dgate,)]
    dz, *dpre = _rw_pre_bwd(z, pre_par, pre_cots, tw)
    for n, d in zip(['rw_shift_mu', 'rw_w0', 'rw_w2', 'rw_a0', 'rw_a2', 'rw_g2', 'rw_k_k', 'rw_k_a'], dpre):
        g[n] = d

    dypre, du_a, g['s5_d'], g['s5_glu_w'], g['s5_glu_b'] = _tok_bwd(
        "s5_post_bwd", _f_s5_post, [ypre, u], s5_par, [ds5_out], [F32, F32], [True] * 3, tw)
    dxs = _s5_expand("d_s5_x", dypre, c_blk)
    dc_blk = _s5_block_grad("d_s5_c", dypre, xs)
    dbu, dlam_row = _s5_scan_bwd(dxs, xs, lam_row, tm)
    du = _s5_contract("d_s5_u", dbu, b_blk, add=du_a, out_dtype=BF16)
    db_blk = _s5_block_grad("d_s5_b", u, dbu)
    dbuild = _small_bwd("s5_build_bwd", _f_s5_build, build_ins, (db_blk, dc_blk))
    lam_cots = (dlam_row[:, :S5_LANES].reshape(gp), dlam_row[:, S5_LANES:].reshape(gp),
                dbuild[0].reshape(gp), dbuild[1].reshape(gp))
    g['s5_lam_re'], g['s5_lam_im'], g['s5_log_step'] = _small_bwd("s5_lam_bwd", _f_s5_lam, lam_ins, lam_cots)
    from_t = lambda a, perm: a.reshape(S5_GROUP, S5_GROUPS, S5_STATE).transpose(perm)
    g['s5_b_re'], g['s5_b_im'] = from_t(dbuild[2], (1, 2, 0)), from_t(dbuild[3], (1, 2, 0))
    g['s5_c_re'], g['s5_c_im'] = from_t(dbuild[4], (1, 0, 2)), from_t(dbuild[5], (1, 0, 2))

    dproj = jnp.concatenate([du, dz], axis=1)
    g['w_in'] = _mm("d_w_in", xn, dproj, 'tn', out_dtype=WIRE)
    dxn = _mm("d_xn", dproj, wf['w_in'], 'nt')
    grad_x, g['norm_mix'] = _tok_bwd("norm_in_bwd", _f_norm_in, [x], [norm_mix], [dxn], [F32], [True], tw,
                                     add_to=(0, dx_a))
    return loss[0, 0], grad_x, g, late_pieces


def _step(x, p, target, w, m, v):
    shards = {n: _mat(w[n]).astype(BF16) for n in SHARDED_NAMES}
    early = _run_plan("gather_early", _gather_plan([shards[n] for n in EARLY_NAMES]))
    wf = {n: _from_shard_major(a, SHARDED[n]) for n, a in zip(EARLY_NAMES, early)}
    ws = {n: w[n] for n in SMALL_NAMES}

    loss, grad_x, g, late_pieces = _local_step(x[0], p[0, 0], target[0], wf, ws, shards)

    early_plan = _exchange_plan([_to_shard_major(g[n], SHARDED[n]).astype(WIRE) for n in EARLY_NAMES],
                                _pack_small({n: g[n] for n in SMALL_NAMES}, tail=loss))
    *early_pieces, by_dev = _run_plan("exchange_early", early_plan)
    pieces = dict(zip(LATE_NAMES + EARLY_NAMES, list(late_pieces) + early_pieces))
    halves = [_ew("add_devices_" + n, _sum_slots, [pieces[n]], [pieces[n].shape[1:]])[0] for n in SHARDED_NAMES]
    (small_piece,) = _ew("add_devices_small", _sum_slots, [by_dev], [by_dev.shape[1:]])
    both, small_g = _share_cores(halves, small_piece)

    kinds = [{}, {}, {}, {}]
    for n, gn in zip(SHARDED_NAMES, both):
        shard = _mat(w[n]).shape
        res = _ew("adamw_" + n, _adamw, [gn.reshape(shard), _mat(w[n]), _mat(m[n]), _mat(v[n])], [shard] * 3)
        for kind, a in zip(kinds, [gn] + list(res)):
            kind[n] = a.reshape(w[n].shape)
    flat = (8 * SMALL_ROWS, PACK_COLS)
    packed = [_pack_small({n: d[n] for n in SMALL_NAMES}).reshape(flat) for d in (w, m, v)]
    small_res = _ew("adamw_small", _adamw, [small_g.reshape(flat)] + packed, [flat] * 3)
    small_shapes = {n: w[n].shape for n in SMALL_NAMES}
    for kind, a in zip(kinds, [small_g] + list(small_res)):
        kind.update(_unpack_small(a, small_shapes))
    total = small_g.reshape(-1)[sum(math.prod(s) for s in small_shapes.values())]
    return (total, grad_x[None], *[kind[n] for kind in kinds for n in WEIGHT_NAMES])


def kernel(x, p, norm_mix, w_in, s5_lam_re, s5_lam_im, s5_log_step, s5_b_re, s5_b_im, s5_c_re, s5_c_im, s5_d, s5_glu_w, s5_glu_b, rw_shift_mu, rw_w0, rw_w2, rw_a0, rw_a2, rw_g2, rw_k_k, rw_k_a, rw_r_k, rw_ln_w, rw_ln_b, w_out, norm_ffn, ffn_w1, ffn_w3, ffn_w2, norm_ple, ple_gate_w, ple_up_w, final_norm, loss_target, m_norm_mix, m_w_in, m_s5_lam_re, m_s5_lam_im, m_s5_log_step, m_s5_b_re, m_s5_b_im, m_s5_c_re, m_s5_c_im, m_s5_d, m_s5_glu_w, m_s5_glu_b, m_rw_shift_mu, m_rw_w0, m_rw_w2, m_rw_a0, m_rw_a2, m_rw_g2, m_rw_k_k, m_rw_k_a, m_rw_r_k, m_rw_ln_w, m_rw_ln_b, m_w_out, m_norm_ffn, m_ffn_w1, m_ffn_w3, m_ffn_w2, m_norm_ple, m_ple_gate_w, m_ple_up_w, m_final_norm, v_norm_mix, v_w_in, v_s5_lam_re, v_s5_lam_im, v_s5_log_step, v_s5_b_re, v_s5_b_im, v_s5_c_re, v_s5_c_im, v_s5_d, v_s5_glu_w, v_s5_glu_b, v_rw_shift_mu, v_rw_w0, v_rw_w2, v_rw_a0, v_rw_a2, v_rw_g2, v_rw_k_k, v_rw_k_a, v_rw_r_k, v_rw_ln_w, v_rw_ln_b, v_w_out, v_norm_ffn, v_ffn_w1, v_ffn_w3, v_ffn_w2, v_norm_ple, v_ple_gate_w, v_ple_up_w, v_final_norm):
    args = dict(locals())
    w = {n: args[n] for n in WEIGHT_NAMES}
    m = {n: args["m_" + n] for n in WEIGHT_NAMES}
    v = {n: args["v_" + n] for n in WEIGHT_NAMES}
    return _step(x, p, loss_target, w, m, v)
```

```python
import functools
import math
from typing import Any, Callable, NamedTuple, Sequence

import jax
import jax.numpy as jnp
from jax import lax
from jax.experimental import pallas as pl
from jax.experimental.pallas import tpu as pltpu

F32 = jnp.float32
BF16 = jnp.bfloat16
MESH = pl.DeviceIdType.MESH

S5_WIDTH = 512
RWKV_WIDTH = 512
S5_GROUP = 16
S5_GROUPS = 32
S5_STATE = 64
S5_LANES = S5_GROUPS * S5_STATE
S5_TILE_GROUPS = 8
S5_TILES = S5_GROUPS // S5_TILE_GROUPS
S5_TILE_CH = S5_TILE_GROUPS * S5_GROUP
S5_TILE_LANES = S5_TILE_GROUPS * S5_STATE
HEAD = 64
HEADS = 8
DECAY_LORA = 64
AAA_LORA = 64
GATE_LORA = 128
FFN_HIDDEN = 2816
FFN_TILE = FFN_HIDDEN // 2
RMS_EPS = 1e-6
GN_EPS = 64e-5
L2_EPS = 1e-12
ADAM_LR = 0.001
ADAM_B1 = 0.9
ADAM_B2 = 0.999
ADAM_EPS = 1e-08
ADAM_WD = 0.01
ADAM_STEP = 10

WKV_CHUNK = 64
SCAN_UNROLL = 4
WIRE = jnp.bfloat16
WKV_PASSES = 1
VMEM_LIMIT_BYTES = 48 * 1024 * 1024
LANE = 128
PACK_COLS = 1024
SMALL_ROWS = 24

WEIGHT_NAMES = ['norm_mix', 'w_in', 's5_lam_re', 's5_lam_im', 's5_log_step', 's5_b_re', 's5_b_im', 's5_c_re',
                's5_c_im', 's5_d', 's5_glu_w', 's5_glu_b', 'rw_shift_mu', 'rw_w0', 'rw_w2', 'rw_a0', 'rw_a2',
                'rw_g2', 'rw_k_k', 'rw_k_a', 'rw_r_k', 'rw_ln_w', 'rw_ln_b', 'w_out', 'norm_ffn', 'ffn_w1',
                'ffn_w3', 'ffn_w2', 'norm_ple', 'ple_gate_w', 'ple_up_w', 'final_norm']
SHARDED = {'w_in': 1, 's5_glu_w': 0, 'rw_w2': 1, 'rw_a2': 1, 'rw_g2': 1, 'w_out': 0, 'ffn_w1': 1, 'ffn_w3': 1,
           'ffn_w2': 0, 'ple_gate_w': 0, 'ple_up_w': 1}
SHARDED_NAMES = [n for n in WEIGHT_NAMES if n in SHARDED]
LATE_NAMES = ['w_out', 'ffn_w1', 'ffn_w3', 'ffn_w2', 'ple_gate_w', 'ple_up_w']
EARLY_NAMES = [n for n in SHARDED_NAMES if n not in LATE_NAMES]
LATE_GATHER = {'s5_scan': ['w_out', 'ple_gate_w', 'ple_up_w'], 'wkv_fwd': ['ffn_w1', 'ffn_w3'], 'ffn_up': ['ffn_w2']}
SMALL_NAMES = [n for n in WEIGHT_NAMES if n not in SHARDED]


def _params(sem=None):
    return pltpu.CompilerParams(dimension_semantics=sem, vmem_limit_bytes=VMEM_LIMIT_BYTES)


def _tile(n, target):
    best = None
    for d in range(LANE, min(n, target) + 1, LANE):
        if n % d == 0:
            best = d
    return n if best is None else best


_NN = (((1,), (0,)), ((), ()))
_NT = (((1,), (1,)), ((), ()))
_TN = (((0,), (0,)), ((), ()))


def _split(a):
    a = a.astype(F32)
    hi = a.astype(BF16)
    return hi, (a - hi.astype(F32)).astype(BF16)


def _dg(a, b, dims, passes):
    dg = lambda p, q: lax.dot_general(p, q, dims, preferred_element_type=F32)
    if passes == 1:
        return dg(a.astype(BF16), b.astype(BF16))
    bh, bl = _split(b)
    if passes == 2:
        return dg(a.astype(BF16), bh) + dg(a.astype(BF16), bl)
    ah, al = _split(a)
    return dg(ah, bh) + (dg(ah, bl) + dg(al, bh))


_DOT_BWD = {_NN: (("g", "b", _NT), ("a", "g", _TN)),
            _NT: (("g", "b", _NN), ("g", "a", _TN)),
            _TN: (("b", "g", _NT), ("a", "g", _NN))}


@functools.partial(jax.custom_vjp, nondiff_argnums=(2, 3))
def _dot(a, b, dims, passes):
    return _dg(a, b, dims, passes)


def _dot_fwd(a, b, dims, passes):
    return _dg(a, b, dims, passes), (a, b)


def _dot_bwd(dims, passes, res, g):
    env = {"a": res[0], "b": res[1], "g": g}
    return tuple(_dg(env[p], env[q], d, passes) for p, q, d in _DOT_BWD[dims])


_dot.defvjp(_dot_fwd, _dot_bwd)


def _bdot(x, w):
    return _dot(x, w, _NN, 1)


@jax.custom_vjp
def _shift_down(z):
    return pltpu.roll(z, 1, 0)


def _shift_down_fwd(z):
    return pltpu.roll(z, 1, 0), None


def _shift_down_bwd(_, g):
    return (pltpu.roll(g, g.shape[0] - 1, 0),)


_shift_down.defvjp(_shift_down_fwd, _shift_down_bwd)


def _head_sum_impl(x):
    r = lax.broadcasted_iota(jnp.int32, (LANE, LANE), 0) // HEAD
    c = lax.broadcasted_iota(jnp.int32, (LANE, LANE), 1) // HEAD
    ones = (r == c).astype(BF16)
    hi, lo = _split(x)
    dg = lambda p: lax.dot_general(p, ones, _NN, preferred_element_type=F32)
    tiles = [slice(j, j + LANE) for j in range(0, x.shape[1], LANE)]
    return jnp.concatenate([dg(hi[:, s]) + dg(lo[:, s]) for s in tiles], axis=1)


@jax.custom_vjp
def _head_sum(x):
    return _head_sum_impl(x)


_head_sum.defvjp(lambda x: (_head_sum_impl(x), None), lambda _, g: (_head_sum_impl(g),))


def _mm(name, a, b, mode, out_dtype=F32, precise=False, tm=1024, tn=1024, tk=1536, plan=None):
    if mode == 'nn':
        (m, k), (_, n) = a.shape, b.shape
    elif mode == 'nt':
        (m, k), (n, _) = a.shape, b.shape
    else:
        (k, m), (_, n) = a.shape, b.shape
    tm, tn, tk = _tile(m, tm), _tile(n, tn), _tile(k, tk)
    nm, nn, nk = m // tm, n // tn, k // tk
    dims = {'nn': _NN, 'nt': _NT, 'tn': _TN}[mode]
    plan = _NO_PLAN if plan is None else plan
    parts, plan_in_specs, plan_out_shape, plan_out_specs, plan_sems = _carry(plan, 2, 1)

    def body(*refs):
        a_ref, b_ref, o_ref = refs[0], refs[1], refs[2 + len(plan.ins)]
        acc_ref = refs[3 + len(plan.ins) + len(plan.out_shape)]
        i, j, kk = pl.program_id(0), pl.program_id(1), pl.program_id(2)

        if plan is not _NO_PLAN:
            pl.when((i == 0) & (j == 0) & (kk == 0))(lambda: plan.start(*parts(refs)))

        @pl.when(kk == 0)
        def _():
            acc_ref[...] = jnp.zeros_like(acc_ref)

        acc_ref[...] += _dg(a_ref[...], b_ref[...], dims, 3 if precise else 1)

        @pl.when(kk == nk - 1)
        def _():
            o_ref[...] = acc_ref[...].astype(o_ref.dtype)

        if plan is not _NO_PLAN:
            pl.when((i == nm - 1) & (j == nn - 1) & (kk == nk - 1))(lambda: plan.wait(*parts(refs)))

    if mode == 'tn':
        a_spec = pl.BlockSpec((tk, tm), lambda i, j, l: (l, i))
    else:
        a_spec = pl.BlockSpec((tm, tk), lambda i, j, l: (i, l))
    if mode == 'nt':
        b_spec = pl.BlockSpec((tn, tk), lambda i, j, l: (j, l))
    else:
        b_spec = pl.BlockSpec((tk, tn), lambda i, j, l: (l, j))
    res = pl.pallas_call(
        body, name=name,
        out_shape=[jax.ShapeDtypeStruct((m, n), out_dtype)] + plan_out_shape,
        grid=(nm, nn, nk),
        in_specs=[a_spec, b_spec] + plan_in_specs,
        out_specs=[pl.BlockSpec((tm, tn), lambda i, j, l: (i, j))] + plan_out_specs,
        scratch_shapes=[pltpu.VMEM((tm, tn), F32)] + plan_sems,
        compiler_params=_params(("parallel", "parallel", "arbitrary") if plan is _NO_PLAN else ("arbitrary",) * 3),
    )(a, b, *plan.ins)
    return res[0] if plan is _NO_PLAN else (res[0], res[1:])


def _mm_tiles(name, a, b, mode, out_shape, grid, a_spec, b_spec, o_spec, add=None, out_dtype=F32):
    dims = {'nn': _NN, 'nt': _NT, 'tn': _TN}[mode]
    nk = grid[2]
    extra = [] if add is None else [add]

    def body(a_ref, b_ref, *refs):
        o_ref, acc_ref = refs[len(extra):]
        kk = pl.program_id(2)

        @pl.when(kk == 0)
        def _():
            acc_ref[...] = refs[0][...].astype(F32) if extra else jnp.zeros_like(acc_ref)

        acc_ref[...] += _dg(a_ref[...], b_ref[...], dims, 1)

        @pl.when(kk == nk - 1)
        def _():
            o_ref[...] = acc_ref[...].astype(o_ref.dtype)

    return pl.pallas_call(
        body, name=name,
        out_shape=jax.ShapeDtypeStruct(out_shape, out_dtype),
        grid=grid, in_specs=[a_spec, b_spec] + [o_spec] * len(extra), out_specs=o_spec,
        scratch_shapes=[pltpu.VMEM(o_spec.block_shape, F32)],
        compiler_params=_params(("parallel", "parallel", "arbitrary")),
    )(a, b, *extra)


def _s5_expand(name, u, blk, tm=2048):
    t = u.shape[0]
    tm = min(tm, t)
    ch, ln, nt = S5_TILE_CH, S5_TILE_LANES, S5_TILES
    return _mm_tiles(name, u, blk, 'nn', (t, 2 * S5_LANES), (t // tm, 2 * nt, 1),
                     pl.BlockSpec((tm, ch), lambda i, j, l: (i, j % nt)),
                     pl.BlockSpec((ch, ln), lambda i, j, l: (j % nt, j // nt)),
                     pl.BlockSpec((tm, ln), lambda i, j, l: (i, j)), out_dtype=BF16)


def _s5_contract(name, x, blk, tm=2048, add=None, out_dtype=F32):
    t = x.shape[0]
    tm = min(tm, t)
    ch, ln, nt = S5_TILE_CH, S5_TILE_LANES, S5_TILES
    return _mm_tiles(name, x, blk, 'nt', (t, S5_WIDTH), (t // tm, nt, 2),
                     pl.BlockSpec((tm, ln), lambda i, j, l: (i, j + nt * l)),
                     pl.BlockSpec((ch, ln), lambda i, j, l: (j, l)),
                     pl.BlockSpec((tm, ch), lambda i, j, l: (i, j)), add=add, out_dtype=out_dtype)


def _s5_block_grad(name, u, x, tk=2048):
    t = u.shape[0]
    tk = min(tk, t)
    ch, ln, nt = S5_TILE_CH, S5_TILE_LANES, S5_TILES
    return _mm_tiles(name, u, x, 'tn', (S5_WIDTH, 2 * ln), (nt, 2, t // tk),
                     pl.BlockSpec((tk, ch), lambda i, j, l: (l, i)),
                     pl.BlockSpec((tk, ln), lambda i, j, l: (l, i + nt * j)),
                     pl.BlockSpec((ch, ln), lambda i, j, l: (i, j)))


def _full_spec(p):
    nd = p.ndim
    return pl.BlockSpec(p.shape, lambda i, nd=nd: (0,) * nd)


def _tok_fwd(name, fn, toks, params, outs, tm):
    t = toks[0].shape[0]
    nt, npar = len(toks), len(params)

    def body(*refs):
        tv = [r[...].astype(F32) for r in refs[:nt]]
        pv = [r[...].astype(F32) for r in refs[nt:nt + npar]]
        res = fn(*tv, *pv)
        for r, v in zip(refs[nt + npar:], res):
            r[...] = v.astype(r.dtype)

    return pl.pallas_call(
        body, name=name,
        out_shape=[jax.ShapeDtypeStruct((t, w), d) for w, d in outs],
        grid=(t // tm,),
        in_specs=[pl.BlockSpec((tm, a.shape[1]), lambda i: (i, 0)) for a in toks] + [_full_spec(p) for p in params],
        out_specs=[pl.BlockSpec((tm, w), lambda i: (i, 0)) for w, _ in outs],
        compiler_params=_params(("parallel",)),
    )(*toks, *params)


def _tok_bwd(name, fn, toks, params, cots, dtok, dpar, tm, acc_out=None, add_to=None):
    t = toks[0].shape[0]
    nt, npar = len(toks), len(params)
    cot_arrays = [c for c in cots if c is not None]
    ncot = len(cot_arrays)
    extra = [] if add_to is None else [add_to[1]]
    dtok_idx = [i for i, d in enumerate(dtok) if d is not None]
    dpar_idx = [i for i, d in enumerate(dpar) if d]

    def body(*refs):
        pos = 0
        tin = refs[pos:pos + nt]; pos += nt
        pin = refs[pos:pos + npar]; pos += npar
        cin = refs[pos:pos + ncot]; pos += ncot
        ein = refs[pos:pos + len(extra)]; pos += len(extra)
        dto = refs[pos:pos + len(dtok_idx)]; pos += len(dtok_idx)
        dpo = refs[pos:pos + len(dpar_idx)]; pos += len(dpar_idx)
        acc = refs[pos] if acc_out is not None else None
        first = pl.program_id(0) == 0

        tv = [r[...].astype(F32) for r in tin]
        pv = [r[...].astype(F32) for r in pin]
        res, vjp = jax.vjp(fn, *tv, *pv)
        cit = iter(cin)
        cs = tuple(jnp.ones_like(o) if c is None else next(cit)[...].astype(F32) for c, o in zip(cots, res))
        g = vjp(cs)
        for r, i in zip(dto, dtok_idx):
            v = g[i]
            if add_to is not None and add_to[0] == i:
                v = v + ein[0][...].astype(F32)
            r[...] = v.astype(r.dtype)

        @pl.when(first)
        def _():
            for r in dpo:
                r[...] = jnp.zeros_like(r)
            if acc is not None:
                acc[...] = jnp.zeros_like(acc)

        for r, i in zip(dpo, dpar_idx):
            r[...] += g[nt + i]
        if acc is not None:
            acc[...] += res[acc_out]

    out_shape = [jax.ShapeDtypeStruct(toks[i].shape, dtok[i]) for i in dtok_idx]
    out_shape += [jax.ShapeDtypeStruct(params[i].shape, F32) for i in dpar_idx]
    out_specs = [pl.BlockSpec((tm, toks[i].shape[1]), lambda i_: (i_, 0)) for i in dtok_idx]
    out_specs += [_full_spec(params[i]) for i in dpar_idx]
    if acc_out is not None:
        out_shape.append(jax.ShapeDtypeStruct((1, 1), F32))
        out_specs.append(pl.BlockSpec((1, 1), lambda i_: (0, 0)))
    tok_spec = lambda a: pl.BlockSpec((tm, a.shape[1]), lambda i_: (i_, 0))
    return pl.pallas_call(
        body, name=name,
        out_shape=out_shape,
        grid=(t // tm,),
        in_specs=[tok_spec(a) for a in toks] + [_full_spec(p) for p in params]
        + [tok_spec(c) for c in cot_arrays] + [tok_spec(e) for e in extra],
        out_specs=out_specs,
        compiler_params=_params(("arbitrary",)),
    )(*toks, *params, *cot_arrays, *extra)


def _small_fwd(name, fn, ins, outs):
    n = len(ins)

    def body(*refs):
        res = fn(*[r[...] for r in refs[:n]])
        for r, v in zip(refs[n:], res):
            r[...] = v.astype(r.dtype)

    return pl.pallas_call(
        body, name=name,
        out_shape=[jax.ShapeDtypeStruct(s, d) for s, d in outs],
        compiler_params=_params(),
    )(*ins)


def _small_bwd(name, fn, ins, cots):
    n = len(ins)

    def body(*refs):
        _, vjp = jax.vjp(fn, *[r[...] for r in refs[:n]])
        g = vjp(tuple(r[...] for r in refs[n:n + len(cots)]))
        for r, v in zip(refs[n + len(cots):], g):
            r[...] = v

    return pl.pallas_call(
        body, name=name,
        out_shape=[jax.ShapeDtypeStruct(a.shape, F32) for a in ins],
        compiler_params=_params(),
    )(*ins, *cots)


def _rms(x, g):
    return x * lax.rsqrt(jnp.mean(x * x, axis=-1, keepdims=True) + RMS_EPS) * g


def _f_norm_in(x, g):
    return (_rms(x, g),)


def _f_mix_res(x, mixed, g):
    h1 = x + mixed
    return h1, _rms(h1, g)


def _f_ffn_act(a13):
    a1, a3 = a13[:, :FFN_HIDDEN], a13[:, FFN_HIDDEN:]
    return (jax.nn.silu(a1) * a3,)


def _f_ffn_res(h1, ffo, g):
    h2 = h1 + ffo
    return h2, _rms(h2, g)


def _f_loss(h2, gpre, pu, target, g):
    h3 = h2 + jax.nn.sigmoid(gpre) * pu
    y = _rms(h3, g)
    err = jnp.square(y - target)
    return (0.5 * jnp.sum(jnp.mean(err, axis=-1, keepdims=True), axis=0, keepdims=True),)


def _f_s5_post(ypre, u, d, glu_w, glu_b):
    z = jax.nn.gelu(ypre + u * d)
    return (z * jax.nn.sigmoid(_bdot(z, glu_w) + glu_b),)


def _softplus(x):
    return jnp.maximum(x, 0.0) + jnp.log(1.0 + jnp.exp(-jnp.abs(x)))


def _f_rw_pre(z, carry, shift_mu, w0, w2, a0, a2, g2, k_k, k_a):
    rw = RWKV_WIDTH
    first_row = lax.broadcasted_iota(jnp.int32, z.shape, 0) == 0
    prev = jnp.where(first_row, carry, _shift_down(z))
    zs = z + (prev - z) * shift_mu
    o1, o2 = 3 * rw + DECAY_LORA, 3 * rw + DECAY_LORA + AAA_LORA
    r, k, v = zs[:, :rw], zs[:, rw:2 * rw], zs[:, 2 * rw:3 * rw]
    wl, al, gl = zs[:, 3 * rw:o1], zs[:, o1:o2], zs[:, o2:]
    w = -_softplus(-(w0 + _bdot(jnp.tanh(wl), w2))) - 0.5
    log_decay = -jnp.exp(w)
    a = jax.nn.sigmoid(a0 + _bdot(al, a2))
    g = _bdot(jax.nn.sigmoid(gl), g2)
    kk = k * k_k
    norm = jnp.sqrt(_head_sum(kk * kk))
    kk = kk / jnp.maximum(norm, L2_EPS)
    kp = k * (1.0 + (a - 1.0) * k_a)
    return r, log_decay, kp, v, -kk, kk * a, g


def _f_rw_post(y, r, kp, v, g, ln_w, ln_b, r_k):
    yc = y - _head_sum(y) * (1.0 / HEAD)
    var = _head_sum(yc * yc) * (1.0 / HEAD)
    yn = yc * lax.rsqrt(var + GN_EPS) * ln_w + ln_b
    bonus = _head_sum(r * kp * r_k) * v
    return ((yn + bonus) * g,)


def _f_s5_lam(lam_re, lam_im, log_step):
    step = jnp.exp(log_step)
    dr, di = lam_re * step, lam_im * step
    e = jnp.exp(dr)
    lbr, lbi = e * jnp.cos(di), e * jnp.sin(di)
    nr, ni = lbr - 1.0, lbi
    den = lam_re * lam_re + lam_im * lam_im
    return lbr, lbi, (nr * lam_re + ni * lam_im) / den, (ni * lam_re - nr * lam_im) / den


def _f_s5_build(coef_r, coef_i, btr, bti, ctr, cti):
    bbr = coef_r * btr - coef_i * bti
    bbi = coef_r * bti + coef_i * btr
    shape = (S5_WIDTH, S5_TILE_LANES)
    rows = (lax.broadcasted_iota(jnp.int32, shape, 0) % S5_TILE_CH) // S5_GROUP
    cols = lax.broadcasted_iota(jnp.int32, shape, 1) // S5_STATE
    mask = (rows == cols).astype(F32)

    def blocks(m):
        per_tile = [m[:, S5_TILE_LANES * i:S5_TILE_LANES * (i + 1)] for i in range(S5_TILES)]
        return jnp.concatenate([t for t in per_tile for _ in range(S5_TILE_GROUPS)], axis=0) * mask

    return (jnp.concatenate([blocks(bbr), blocks(bbi)], axis=1),
            jnp.concatenate([blocks(ctr), -blocks(cti)], axis=1))


HALO = 8


def _rw_pre_specs(z, params, tm, order):
    halo_blocks = tm // HALO
    return ([pl.BlockSpec((tm, z.shape[1]), lambda i: (order(i), 0)),
             pl.BlockSpec((HALO, z.shape[1]), lambda i: (jnp.maximum(order(i) * halo_blocks - 1, 0), 0))]
            + [_full_spec(p) for p in params])


def _rw_pre_fwd(z, params, tm):
    t = z.shape[0]
    npar = len(params)

    def body(z_ref, halo_ref, *refs):
        carry = jnp.where(pl.program_id(0) == 0, 0.0, halo_ref[pl.ds(HALO - 1, 1), :])
        res = _f_rw_pre(z_ref[...], carry, *[r[...].astype(F32) for r in refs[:npar]])
        for r, v in zip(refs[npar:], res):
            r[...] = v

    return pl.pallas_call(
        body, name="rw_pre",
        out_shape=[jax.ShapeDtypeStruct((t, RWKV_WIDTH), F32)] * 7,
        grid=(t // tm,),
        in_specs=_rw_pre_specs(z, params, tm, lambda i: i),
        out_specs=[pl.BlockSpec((tm, RWKV_WIDTH), lambda i: (i, 0))] * 7,
        compiler_params=_params(("parallel",)),
    )(z, z, *params)


def _rw_pre_bwd(z, params, cots, tm):
    t = z.shape[0]
    nt = t // tm
    npar = len(params)
    order = lambda i: nt - 1 - i
    flat_cots = [a for group in cots for a in group]
    ncot = len(flat_cots)

    def body(z_ref, halo_ref, *refs):
        pin, cin = refs[:npar], list(refs[npar:npar + ncot])
        dz_ref = refs[npar + ncot]
        dpo = refs[npar + ncot + 1:npar + ncot + 1 + npar]
        dcarry_ref = refs[npar + ncot + 1 + npar]
        i = pl.program_id(0)

        @pl.when(i == 0)
        def _():
            dcarry_ref[...] = jnp.zeros_like(dcarry_ref)
            for r in dpo:
                r[...] = jnp.zeros_like(r)

        carry = jnp.where(i == nt - 1, 0.0, halo_ref[pl.ds(HALO - 1, 1), :])
        _, vjp = jax.vjp(_f_rw_pre, z_ref[...], carry, *[r[...].astype(F32) for r in pin])
        g = vjp(tuple(sum(cin.pop(0)[...] for _ in group) for group in cots))
        last_row = lax.broadcasted_iota(jnp.int32, z_ref.shape, 0) == tm - 1
        dz_ref[...] = (g[0] + jnp.where(last_row, dcarry_ref[...], 0.0)).astype(dz_ref.dtype)
        dcarry_ref[...] = g[1]
        for r, v in zip(dpo, g[2:]):
            r[...] += v

    tok = lambda w: pl.BlockSpec((tm, w), lambda i: (order(i), 0))
    return pl.pallas_call(
        body, name="rw_pre_bwd",
        out_shape=[jax.ShapeDtypeStruct(z.shape, BF16)] + [jax.ShapeDtypeStruct(p.shape, F32) for p in params],
        grid=(nt,),
        in_specs=_rw_pre_specs(z, params, tm, order) + [tok(RWKV_WIDTH)] * ncot,
        out_specs=[tok(z.shape[1])] + [_full_spec(p) for p in params],
        scratch_shapes=[pltpu.VMEM((1, z.shape[1]), F32)],
        compiler_params=_params(("arbitrary",)),
    )(z, z, *params, *flat_cots)


def _s5_scan(bu, lam, tm, plan):
    t, w = bu.shape
    h = w // 2
    nt = t // tm
    parts, plan_in_specs, plan_out_shape, plan_out_specs, plan_sems = _carry(plan, 2, 1)

    def body(*refs):
        bu_ref, lam_ref = refs[:2]
        xb_ref = refs[2 + len(plan.ins)]
        carry_ref = refs[3 + len(plan.ins) + len(plan.out_shape)]
        x_ref, refs = refs[-1], refs[:-1]

        @pl.when(pl.program_id(0) == 0)
        def _():
            carry_ref[...] = jnp.zeros_like(carry_ref)
            plan.start(*parts(refs))

        lr, li = lam_ref[:, :h], lam_ref[:, h:]
        x_ref[...] = bu_ref[...].astype(F32)

        def step(s, c):
            cr, ci = c
            row = pl.ds(s, 1)
            nr = lr * cr - li * ci + x_ref[row, :h]
            ni = lr * ci + li * cr + x_ref[row, h:]
            x_ref[row, :h] = nr
            x_ref[row, h:] = ni
            return nr, ni

        cr, ci = lax.fori_loop(0, tm, step, (carry_ref[:, :h], carry_ref[:, h:]), unroll=SCAN_UNROLL)
        carry_ref[:, :h] = cr
        carry_ref[:, h:] = ci
        xb_ref[...] = x_ref[...].astype(BF16)

        @pl.when(pl.program_id(0) == nt - 1)
        def _():
            plan.wait(*parts(refs))

    spec = pl.BlockSpec((tm, w), lambda i: (i, 0))
    res = pl.pallas_call(
        body, name="s5_scan",
        out_shape=[jax.ShapeDtypeStruct((t, w), BF16)] + plan_out_shape,
        grid=(nt,),
        in_specs=[spec, pl.BlockSpec((1, w), lambda i: (0, 0))] + plan_in_specs,
        out_specs=[spec] + plan_out_specs,
        scratch_shapes=[pltpu.VMEM((1, w), F32)] + plan_sems + [pltpu.VMEM((tm, w), F32)],
        compiler_params=_params(("arbitrary",)),
    )(bu, lam, *plan.ins)
    return res[0], res[1:]


def _s5_scan_bwd(dx, xb, lam, tm):
    t, w = dx.shape
    h = w // 2
    nt = t // tm
    halo = BF16_SUBLANES

    rows8 = 8

    def body(dx_ref, xb_ref, halo_ref, lam_ref, dbu_out_ref, dlam_ref, carry_ref, dbu_ref, xp_ref):
        @pl.when(pl.program_id(0) == 0)
        def _():
            carry_ref[...] = jnp.zeros_like(carry_ref)
            dlam_ref[...] = jnp.zeros_like(dlam_ref)

        lr, li = lam_ref[:, :h], lam_ref[:, h:]
        dbu_ref[...] = dx_ref[...].astype(F32)

        def step(s, c):
            cr, ci = c
            row = pl.ds(tm - 1 - s, 1)
            nr = lr * cr + li * ci + dbu_ref[row, :h]
            ni = lr * ci - li * cr + dbu_ref[row, h:]
            dbu_ref[row, :h] = nr
            dbu_ref[row, h:] = ni
            return nr, ni

        cr, ci = lax.fori_loop(0, tm, step, (carry_ref[:, :h], carry_ref[:, h:]), unroll=SCAN_UNROLL)
        carry_ref[:, :h] = cr
        carry_ref[:, h:] = ci
        halo_rows = lax.broadcasted_iota(jnp.int32, (halo, w), 0)
        before = jnp.sum(jnp.where(halo_rows == halo - 1, halo_ref[...].astype(F32), 0.0), axis=0, keepdims=True)
        before = jnp.where(pl.program_id(0) == nt - 1, 0.0, before)
        first_row = lax.broadcasted_iota(jnp.int32, (tm, w), 0) == 0
        xp_ref[...] = jnp.where(first_row, before, pltpu.roll(xb_ref[...].astype(F32), 1, 0))

        def accumulate(s, acc):
            ar, ai = acc
            rows = pl.ds(pl.multiple_of(s * rows8, rows8), rows8)
            gr, gi = dbu_ref[rows, :h], dbu_ref[rows, h:]
            pr, pi_ = xp_ref[rows, :h], xp_ref[rows, h:]
            return ar + (gr * pr + gi * pi_), ai + (gi * pr - gr * pi_)

        zero = jnp.zeros((rows8, h), F32)
        ar, ai = lax.fori_loop(0, tm // rows8, accumulate, (zero, zero))
        dlam_ref[:, :h] += jnp.sum(ar, axis=0, keepdims=True)
        dlam_ref[:, h:] += jnp.sum(ai, axis=0, keepdims=True)
        dbu_out_ref[...] = dbu_ref[...].astype(BF16)

    spec = pl.BlockSpec((tm, w), lambda i: (nt - 1 - i, 0))
    halo_spec = pl.BlockSpec((halo, w), lambda i: (jnp.maximum((nt - 1 - i) * (tm // halo) - 1, 0), 0))
    row_spec = pl.BlockSpec((1, w), lambda i: (0, 0))
    return pl.pallas_call(
        body, name="s5_scan_bwd",
        out_shape=[jax.ShapeDtypeStruct((t, w), BF16), jax.ShapeDtypeStruct((1, w), F32)],
        grid=(nt,),
        in_specs=[spec, spec, halo_spec, row_spec],
        out_specs=[spec, row_spec],
        scratch_shapes=[pltpu.VMEM((1, w), F32), pltpu.VMEM((tm, w), F32), pltpu.VMEM((tm, w), F32)],
        compiler_params=_params(("arbitrary",)),
    )(dx, xb, xb, lam)


def _unit_lower_inverses_impl(ns):
    c = ns[0].shape[0]
    eye = (lax.broadcasted_iota(jnp.int32, (c, c), 0) == lax.broadcasted_iota(jnp.int32, (c, c), 1)).astype(F32)
    inv = [eye + n for n in ns]
    pw = [_dg(n, n, _NN, WKV_PASSES) for n in ns]
    for _ in range(int(math.log2(c)) - 2):
        both = [_dg(jnp.concatenate([i, q], axis=0), q, _NN, WKV_PASSES) for i, q in zip(inv, pw)]
        inv = [i + q[:c] for i, q in zip(inv, both)]
        pw = [q[c:] for q in both]
    return tuple(i + _dg(i, q, _NN, WKV_PASSES) for i, q in zip(inv, pw))


@jax.custom_vjp
def _unit_lower_inverses(ns):
    return _unit_lower_inverses_impl(ns)


def _unit_lower_inverses_fwd(ns):
    inv = _unit_lower_inverses_impl(ns)
    return inv, inv


def _unit_lower_inverses_bwd(inv, g):
    left = [_dg(i, gi, _TN, WKV_PASSES) for i, gi in zip(inv, g)]
    return (tuple(_dg(q, i, _NT, WKV_PASSES) for q, i in zip(left, inv)),)


_unit_lower_inverses.defvjp(_unit_lower_inverses_fwd, _unit_lower_inverses_bwd)


def _wkv_chunks(s0, r, lw, k, v, a, b):
    c = r[0].shape[0]
    row = lax.broadcasted_iota(jnp.int32, (c, c), 0)
    col = lax.broadcasted_iota(jnp.int32, (c, c), 1)
    incl, strict = col <= row, col < row
    tri = incl.astype(F32)
    each = lambda f, *xs: [f(*t) for t in zip(*xs)]
    stack = lambda p, q: jnp.concatenate([p, q], axis=0)
    dot = lambda p, q, dims=_NN: _dot(p, q, dims, WKV_PASSES)
    lc = each(lambda l: _dot(tri, l, _NN, 2), lw)
    e_neg = each(lambda l: jnp.exp(-l), lc)
    ar = each(lambda x, z, l, w: stack(x * jnp.exp(l - w), z * jnp.exp(l)), a, r, lc, lw)
    bk = each(lambda x, z, e: stack(x * e, z * e), b, k, e_neg)
    m = each(lambda p, q: dot(p, q, _NT), ar, bk)
    mab = each(lambda q: jnp.where(strict, q[:c, :c], 0.0), m)
    mak_mrk = each(lambda q: stack(jnp.where(strict, q[:c, c:], 0.0), jnp.where(incl, q[c:, c:], 0.0)), m)
    mrb = each(lambda q: jnp.where(incl, q[c:, :c], 0.0), m)
    xy = each(lambda p, s, q, z: dot(p, s, _NT) + dot(q, z), ar, s0, mak_mrk, v)
    inv = _unit_lower_inverses(tuple(mab))
    u = each(lambda i, q: dot(i, q[:c]), inv, xy)
    y = each(lambda q, z, p: q[c:] + dot(z, p), xy, mrb, u)
    e_tot = each(lambda l: jnp.exp(jnp.sum(l, axis=0, keepdims=True)), lw)
    s1 = each(lambda s, p, z, q, e: (s + dot(stack(p, z), q, _TN)) * e, s0, u, v, bk, e_tot)
    return y, s1


def _carry(plan, n_args, n_outs):
    n_in, n_out = len(plan.ins), len(plan.out_shape)

    def parts(refs):
        base = n_args + n_in + n_outs
        return refs[n_args:n_args + n_in], refs[base:base + n_out], refs[base + n_out + 1:]

    return parts, [HBM_SPEC] * n_in, list(plan.out_shape), [HBM_SPEC] * n_out, list(plan.sems)


def _head_cols(ref):
    return tuple(ref[:, h * HEAD:(h + 1) * HEAD] for h in range(HEADS))


def _wkv_fwd(seqs, plan):
    t, w = seqs[0].shape
    c, n = WKV_CHUNK, HEAD
    nc = t // c
    parts, plan_in_specs, plan_out_shape, plan_out_specs, plan_sems = _carry(plan, 6, 2)

    def body(*refs):
        ins, (y_ref, ck_ref) = refs[:6], refs[6 + len(plan.ins):8 + len(plan.ins)]
        s_ref = refs[8 + len(plan.ins) + len(plan.out_shape)]

        @pl.when(pl.program_id(0) == 0)
        def _():
            s_ref[...] = jnp.zeros_like(s_ref)
            plan.start(*parts(refs))

        s0 = tuple(s_ref[h] for h in range(HEADS))
        ys, s1 = _wkv_chunks(s0, *[_head_cols(r) for r in ins])
        for h in range(HEADS):
            ck_ref[0, h] = s0[h]
            y_ref[:, h * n:(h + 1) * n] = ys[h]
            s_ref[h] = s1[h]

        @pl.when(pl.program_id(0) == nc - 1)
        def _():
            plan.wait(*parts(refs))

    spec = pl.BlockSpec((c, w), lambda i: (i, 0))
    res = pl.pallas_call(
        body, name="wkv_fwd",
        out_shape=[jax.ShapeDtypeStruct((t, w), F32), jax.ShapeDtypeStruct((nc, HEADS, n, n), F32)] + plan_out_shape,
        grid=(nc,),
        in_specs=[spec] * 6 + plan_in_specs,
        out_specs=[spec, pl.BlockSpec((1, HEADS, n, n), lambda i: (i, 0, 0, 0))] + plan_out_specs,
        scratch_shapes=[pltpu.VMEM((HEADS, n, n), F32)] + plan_sems,
        compiler_params=_params(("arbitrary",)),
    )(*seqs, *plan.ins)
    return res[0], res[1], res[2:]


def _wkv_bwd(seqs, ck, dy, plan):
    t, w = seqs[0].shape
    c, n = WKV_CHUNK, HEAD
    nc = t // c
    parts, plan_in_specs, plan_out_shape, plan_out_specs, plan_sems = _carry(plan, 8, 6)

    def body(*refs):
        ins, ck_ref, dy_ref = refs[:6], refs[6], refs[7]
        outs = refs[8 + len(plan.ins):14 + len(plan.ins)]
        ds_ref = refs[14 + len(plan.ins) + len(plan.out_shape)]

        @pl.when(pl.program_id(0) == 0)
        def _():
            ds_ref[...] = jnp.zeros_like(ds_ref)
            plan.start(*parts(refs))

        s0 = tuple(ck_ref[0, h] for h in range(HEADS))
        _, vjp = jax.vjp(_wkv_chunks, s0, *[_head_cols(r) for r in ins])
        g = vjp((list(_head_cols(dy_ref)), [ds_ref[h] for h in range(HEADS)]))
        for h in range(HEADS):
            ds_ref[h] = g[0][h]
            for o, d in zip(outs, g[1:]):
                o[:, h * n:(h + 1) * n] = d[h]

        @pl.when(pl.program_id(0) == nc - 1)
        def _():
            plan.wait(*parts(refs))

    spec = pl.BlockSpec((c, w), lambda i: (nc - 1 - i, 0))
    res = pl.pallas_call(
        body, name="wkv_bwd",
        out_shape=[jax.ShapeDtypeStruct((t, w), F32)] * 6 + plan_out_shape,
        grid=(nc,),
        in_specs=[spec] * 6 + [pl.BlockSpec((1, HEADS, n, n), lambda i: (nc - 1 - i, 0, 0, 0)), spec] + plan_in_specs,
        out_specs=[spec] * 6 + plan_out_specs,
        scratch_shapes=[pltpu.VMEM((HEADS, n, n), F32)] + plan_sems,
        compiler_params=_params(("arbitrary",)),
    )(*seqs, ck, dy, *plan.ins)
    return res[:6], res[6:]


def _coords():
    return lax.axis_index("x"), lax.axis_index("y"), lax.axis_index("c")


def _flip(v, f):
    return 1 - v if f else v


_CHIP_FLIPS = [(1, 0), (0, 1), (1, 1)]
_DEV_FLIPS = [(fx, fy, fc) for fx in (0, 1) for fy in (0, 1) for fc in (0, 1) if (fx, fy, fc) != (0, 0, 0)]
HBM_SPEC = pl.BlockSpec(memory_space=pl.ANY)


def _chip_peer(k, x, y):
    fx, fy = _CHIP_FLIPS[k]
    return _flip(x, fx), _flip(y, fy)


def _dev_peer(k, x, y, c):
    fx, fy, fc = _DEV_FLIPS[k]
    return _flip(x, fx), _flip(y, fy), _flip(c, fc)


def _rows_of_core(ref, core):
    h = ref.shape[-2] // 2
    rows = pl.ds(pl.multiple_of(core * h, 8), h)
    return ref.at[rows, :] if len(ref.shape) == 2 else ref.at[:, rows, :]


class _Plan(NamedTuple):
    ins: Sequence[Any]
    out_shape: Sequence[Any]
    sems: Sequence[Any]
    start: Callable
    wait: Callable


_NO_PLAN = _Plan([], [], [], lambda *_: None, lambda *_: None)


def _run_plan(name, plan):
    n_in, n_out = len(plan.ins), len(plan.out_shape)

    def body(*refs):
        parts = refs[:n_in], refs[n_in:n_in + n_out], refs[n_in + n_out:]
        plan.start(*parts)
        plan.wait(*parts)

    return pl.pallas_call(
        body, name=name, out_shape=list(plan.out_shape),
        in_specs=[HBM_SPEC] * n_in, out_specs=[HBM_SPEC] * n_out, scratch_shapes=list(plan.sems),
    )(*plan.ins)


def _gather_plan(shards):
    n = len(shards)

    def copies(srcs, outs, sems):
        send_sems, recv_sems, local_sems = sems
        x, y, c = _coords()
        me = 2 * x + y

        def remote(i, k, arriving):
            px, py = _chip_peer(k, x, y)
            return pltpu.make_async_remote_copy(
                src_ref=srcs[i], dst_ref=outs[i].at[2 * px + py if arriving else me],
                send_sem=send_sems.at[i, k], recv_sem=recv_sems.at[i, k],
                device_id=(px, py, c), device_id_type=MESH)

        own = [pltpu.make_async_copy(srcs[i], outs[i].at[me], local_sems.at[i]) for i in range(n)]
        pairs = [(i, k) for k in range(3) for i in range(n)]
        return own, [remote(i, k, False) for i, k in pairs], [remote(i, k, True) for i, k in pairs]

    return _Plan(
        ins=shards, out_shape=[jax.ShapeDtypeStruct((4,) + s.shape, s.dtype) for s in shards],
        sems=[pltpu.SemaphoreType.DMA((n, 3)), pltpu.SemaphoreType.DMA((n, 3)), pltpu.SemaphoreType.DMA((n,))],
        start=functools.partial(_start_copies, copies), wait=functools.partial(_wait_copies, copies))


def _start_copies(copies, ins, outs, sems):
    own, sends, _ = copies(ins, outs, sems)
    for cp in own + sends:
        cp.start()


def _wait_copies(copies, ins, outs, sems):
    own, sends, arrivals = copies(ins, outs, sems)
    for cp in arrivals:
        cp.wait_recv()
    for cp in sends:
        cp.wait_send()
    for cp in own:
        cp.wait()


def _exchange_plan(gs, small=None):
    n = len(gs)
    arrays = list(gs) + ([] if small is None else [small])

    def copies(srcs, outs, sems):
        send_sems, recv_sems, local_sems = sems
        x, y, c = _coords()
        me = 4 * x + 2 * y + c

        def piece(i, px, py, pc):
            if i == n:
                return srcs[i].at[4 * px + 2 * py + pc]
            return _rows_of_core(srcs[i].at[2 * px + py], pc)

        def remote(i, k, arriving):
            px, py, pc = _dev_peer(k, x, y, c)
            return pltpu.make_async_remote_copy(
                src_ref=piece(i, px, py, pc), dst_ref=outs[i].at[4 * px + 2 * py + pc if arriving else me],
                send_sem=send_sems.at[i, k], recv_sem=recv_sems.at[i, k],
                device_id=(px, py, pc), device_id_type=MESH)

        own = [pltpu.make_async_copy(piece(i, x, y, c), outs[i].at[me], local_sems.at[i]) for i in range(len(arrays))]
        pairs = [(i, k) for k in range(7) for i in range(len(arrays))]
        return own, [remote(i, k, False) for i, k in pairs], [remote(i, k, True) for i, k in pairs]

    out_shape = [jax.ShapeDtypeStruct((8, g.shape[1] // 2, g.shape[2]), g.dtype) for g in gs]
    if small is not None:
        out_shape.append(jax.ShapeDtypeStruct(small.shape, small.dtype))
    m = len(arrays)
    return _Plan(
        ins=arrays, out_shape=out_shape,
        sems=[pltpu.SemaphoreType.DMA((m, 7)), pltpu.SemaphoreType.DMA((m, 7)), pltpu.SemaphoreType.DMA((m,))],
        start=functools.partial(_start_copies, copies), wait=functools.partial(_wait_copies, copies))


def _share_cores(halves, small):
    n = len(halves)

    def body(*refs):
        srcs, small_src, outs, small_out = refs[:n], refs[n], refs[n + 1:2 * n + 1], refs[2 * n + 1]
        mine, theirs = refs[2 * n + 2:3 * n + 2], refs[3 * n + 2:4 * n + 2]
        send_sems, recv_sems, ssend, srecv, local_sems = refs[4 * n + 2:]
        x, y, c = _coords()
        me = 4 * x + 2 * y + c

        def big(i):
            return pltpu.make_async_remote_copy(
                src_ref=mine[i], dst_ref=theirs[i], send_sem=send_sems.at[i], recv_sem=recv_sems.at[i],
                device_id=(x, y, 1 - c), device_id_type=MESH)

        def tiny(k, arriving):
            px, py, pc = _dev_peer(k, x, y, c)
            return pltpu.make_async_remote_copy(
                src_ref=small_src, dst_ref=small_out.at[4 * px + 2 * py + pc if arriving else me],
                send_sem=ssend.at[k], recv_sem=srecv.at[k], device_id=(px, py, pc), device_id_type=MESH)

        small_sends = [tiny(k, False) for k in range(7)]
        own_small = pltpu.make_async_copy(small_src, small_out.at[me], local_sems.at[2 * n])
        stage = [pltpu.make_async_copy(srcs[i], mine[i], local_sems.at[i]) for i in range(n)]
        for cp in small_sends + [own_small] + stage:
            cp.start()
        sends = []
        for i in range(n):
            stage[i].wait()
            sends.append(big(i))
            sends[-1].start()
        store = [pltpu.make_async_copy(mine[i], outs[i].at[c], local_sems.at[i]) for i in range(n)]
        for cp in store:
            cp.start()
        for i in range(n):
            big(i).wait_recv()
            store.append(pltpu.make_async_copy(theirs[i], outs[i].at[1 - c], local_sems.at[n + i]))
            store[-1].start()
        for k in range(7):
            tiny(k, True).wait_recv()
        for cp in sends + small_sends:
            cp.wait_send()
        for cp in store + [own_small]:
            cp.wait()

    staged = [pltpu.VMEM(s.shape, s.dtype) for s in halves]
    res = pl.pallas_call(
        body, name="share_cores",
        out_shape=[jax.ShapeDtypeStruct((2,) + s.shape, s.dtype) for s in halves]
        + [jax.ShapeDtypeStruct((8,) + small.shape, small.dtype)],
        in_specs=[HBM_SPEC] * (n + 1), out_specs=[HBM_SPEC] * (n + 1),
        scratch_shapes=staged + staged + [
            pltpu.SemaphoreType.DMA((n,)), pltpu.SemaphoreType.DMA((n,)),
            pltpu.SemaphoreType.DMA((7,)), pltpu.SemaphoreType.DMA((7,)),
            pltpu.SemaphoreType.DMA((2 * n + 1,))],
        compiler_params=pltpu.CompilerParams(vmem_limit_bytes=VMEM_LIMIT_BYTES),
    )(*halves, small)
    return res[:n], res[n]


BF16_SUBLANES = 16


def _row_tile(n, target, step=BF16_SUBLANES):
    return max([d for d in range(step, min(n, target) + 1, step) if n % d == 0] or [n])


def _ew(name, fn, ins, outs, block_bytes=2 << 20):
    rows, cols = ins[0].shape[-2:]
    lead = max(math.prod(a.shape[:-2]) for a in ins)
    tr = _row_tile(rows, max(8, block_bytes // (4 * cols * lead)))
    n = len(ins)

    def spec(shape):
        if len(shape) == 2:
            return pl.BlockSpec((tr, cols), lambda i: (i, 0))
        return pl.BlockSpec((shape[0], tr, cols), lambda i: (0, i, 0))

    def body(*refs):
        res = fn(*[r[...] for r in refs[:n]])
        for r, v in zip(refs[n:], res):
            r[...] = v

    return pl.pallas_call(
        body, name=name,
        out_shape=[jax.ShapeDtypeStruct(s, F32) for s in outs],
        grid=(rows // tr,),
        in_specs=[spec(a.shape) for a in ins],
        out_specs=[spec(s) for s in outs],
        compiler_params=_params(("parallel",)),
    )(*ins)


def _sum_slots(a):
    total = a[0].astype(F32)
    for s in range(1, a.shape[0]):
        total = total + a[s].astype(F32)
    return (total,)


def _adamw(g, w, m, v):
    bc1 = 1.0 - ADAM_B1 ** ADAM_STEP
    bc2 = 1.0 - ADAM_B2 ** ADAM_STEP
    m_new = ADAM_B1 * m + (1.0 - ADAM_B1) * g
    v_new = ADAM_B2 * v + (1.0 - ADAM_B2) * jnp.square(g)
    delta = -ADAM_LR * ((m_new / bc1) / (jnp.sqrt(v_new / bc2) + ADAM_EPS) + ADAM_WD * w)
    return delta, m_new, v_new


def _mat(a):
    return a.reshape(a.shape[-2:])


def _to_shard_major(full, axis):
    rows, cols = full.shape
    if axis == 0:
        return full.reshape(4, rows // 4, cols)
    return full.reshape(rows, 4, cols // 4).transpose(1, 0, 2)


def _from_shard_major(a, axis):
    _, r, cs = a.shape
    if axis == 0:
        return a.reshape(4 * r, cs)
    return a.transpose(1, 0, 2).reshape(r, 4 * cs)


def _pack_small(arrays, tail=None):
    flat = [arrays[n].reshape(-1) for n in SMALL_NAMES] + ([] if tail is None else [tail.reshape(1)])
    used = sum(a.shape[0] for a in flat)
    flat.append(jnp.zeros((8 * SMALL_ROWS * PACK_COLS - used,), F32))
    return jnp.concatenate(flat).reshape(8, SMALL_ROWS, PACK_COLS)


def _unpack_small(packed, shapes):
    flat = packed.reshape(-1)
    out, off = {}, 0
    for n in SMALL_NAMES:
        size = math.prod(shapes[n])
        out[n] = flat[off:off + size].reshape(shapes[n])
        off += size
    return out


def _row(a):
    return a.reshape(1, -1)


def _local_step(x, p, target, wf, ws, late_shards):
    wf = dict(wf)
    t = x.shape[0]
    tm = min(256, t)
    tw = min(512, t)
    g = {}

    lam_re, lam_im = ws['s5_lam_re'].reshape(S5_GROUPS, S5_STATE), ws['s5_lam_im'].reshape(S5_GROUPS, S5_STATE)
    log_step = ws['s5_log_step'].reshape(S5_GROUPS, 1)
    gp = (S5_GROUPS, S5_STATE)
    lam_ins = (lam_re, lam_im, log_step)
    lbr, lbi, cfr, cfi = _small_fwd("s5_lam", _f_s5_lam, lam_ins, [(gp, F32)] * 4)
    lam_row = jnp.concatenate([_row(lbr), _row(lbi)], axis=1)
    to_t = lambda a, perm: a.reshape((S5_GROUPS,) + a.shape[-2:]).transpose(perm).reshape(S5_GROUP, S5_LANES)
    build_ins = (_row(cfr), _row(cfi), to_t(ws['s5_b_re'], (2, 0, 1)), to_t(ws['s5_b_im'], (2, 0, 1)),
                 to_t(ws['s5_c_re'], (1, 0, 2)), to_t(ws['s5_c_im'], (1, 0, 2)))
    block_shape = (S5_WIDTH, 2 * S5_TILE_LANES)
    b_blk, c_blk = _small_fwd("s5_build", _f_s5_build, build_ins, [(block_shape, F32)] * 2)

    norm_mix, norm_ffn, norm_ple = _row(ws['norm_mix']), _row(ws['norm_ffn']), _row(ws['norm_ple'])
    final_norm = _row(ws['final_norm'])
    (xn,) = _tok_fwd("norm_in", _f_norm_in, [x], [norm_mix], [(x.shape[1], BF16)], tw)
    u = _mm("proj_s5", xn, wf['w_in'][:, :S5_WIDTH], 'nn')
    z = _mm("proj_rw", xn, wf['w_in'][:, S5_WIDTH:], 'nn')

    bu = _s5_expand("s5_bu", u, b_blk)
    def late_plan(carrier):
        return _gather_plan([late_shards[n] for n in LATE_GATHER[carrier]])

    def arrived(carrier, got):
        wf.update({n: _from_shard_major(a, SHARDED[n]) for n, a in zip(LATE_GATHER[carrier], got)})

    xs, got = _s5_scan(bu, lam_row, tm, late_plan('s5_scan'))
    arrived('s5_scan', got)
    ypre = _s5_contract("s5_y", xs, c_blk)
    s5_par = [_row(ws['s5_d']), wf['s5_glu_w'], _row(ws['s5_glu_b'])]
    (s5_out,) = _tok_fwd("s5_post", _f_s5_post, [ypre, u], s5_par, [(S5_WIDTH, BF16)], tw)

    pre_par = [_row(ws['rw_shift_mu']), _row(ws['rw_w0']), wf['rw_w2'], _row(ws['rw_a0']), wf['rw_a2'],
               wf['rw_g2'], _row(ws['rw_k_k']), _row(ws['rw_k_a'])]
    r, lw, kp, v, an, bn, gate = _rw_pre_fwd(z, pre_par, tw)
    seqs = [r, lw, kp, v, an, bn]
    y_wkv, ck, got = _wkv_fwd(seqs, late_plan('wkv_fwd'))
    arrived('wkv_fwd', got)
    post_par = [_row(ws['rw_ln_w']), _row(ws['rw_ln_b']), _row(ws['rw_r_k'])]
    post_toks = [y_wkv, r, kp, v, gate]
    (rw_out,) = _tok_fwd("rw_post", _f_rw_post, post_toks, post_par, [(RWKV_WIDTH, BF16)], tw)

    mixcat = jnp.concatenate([s5_out, rw_out], axis=1)
    mixed = _mm("mix_out", mixcat, wf['w_out'], 'nn')
    h1, hn = _tok_fwd("mix_res", _f_mix_res, [x, mixed], [norm_ffn], [(x.shape[1], F32), (x.shape[1], BF16)], tw)
    w13 = jnp.concatenate([wf['ffn_w1'], wf['ffn_w3']], axis=1)
    a13, got = _mm("ffn_up", hn, w13, 'nn', out_dtype=BF16, tn=FFN_TILE, plan=late_plan('ffn_up'))
    arrived('ffn_up', got)
    (f,) = _tok_fwd("ffn_act", _f_ffn_act, [a13], [], [(FFN_HIDDEN, BF16)], tw)
    ffo = _mm("ffn_down", f, wf['ffn_w2'], 'nn')
    h2, hp = _tok_fwd("ffn_res", _f_ffn_res, [h1, ffo], [norm_ple], [(x.shape[1], F32), (x.shape[1], BF16)], tw)
    gpre = _mm("ple_gate", hp, wf['ple_gate_w'], 'nn')
    pu = _mm("ple_up", p, wf['ple_up_w'], 'nn')

    dh2, dgpre, dpu, g['final_norm'], loss = _tok_bwd(
        "loss", _f_loss, [h2, gpre, pu, target], [final_norm], [None],
        [F32, BF16, BF16, None], [True], tw, acc_out=0)
    g['ple_gate_w'] = _mm("d_ple_gate_w", hp, dgpre, 'tn', out_dtype=WIRE)
    g['ple_up_w'] = _mm("d_ple_up_w", p, dpu, 'tn', out_dtype=WIRE)
    dhp = _mm("d_hp", dgpre, wf['ple_gate_w'], 'nt')
    dh1, dffo, g['norm_ple'] = _tok_bwd("ffn_res_bwd", _f_ffn_res, [h1, ffo], [norm_ple], [dh2, dhp],
                                        [F32, BF16], [True], tw)
    g['ffn_w2'] = _mm("d_ffn_w2", f, dffo, 'tn', out_dtype=WIRE, tm=FFN_TILE)
    df = _mm("d_f", dffo, wf['ffn_w2'], 'nt', out_dtype=BF16, tn=FFN_TILE)
    (da13,) = _tok_bwd("ffn_act_bwd", _f_ffn_act, [a13], [], [df], [BF16], [], tw)
    dw13 = _mm("d_ffn_w13", hn, da13, 'tn', out_dtype=WIRE, tn=FFN_TILE)
    dw13 = dw13.reshape(dw13.shape[0], 8, FFN_HIDDEN // 4).transpose(1, 0, 2)
    shard_major = {'ffn_w1': dw13[:4], 'ffn_w3': dw13[4:]}
    dhn = _mm("d_hn", da13, w13, 'nt')
    dx_a, dmixed, g['norm_ffn'] = _tok_bwd("mix_res_bwd", _f_mix_res, [x, mixed], [norm_ffn], [dh1, dhn],
                                           [F32, BF16], [True], tw)
    g['w_out'] = _mm("d_w_out", mixcat, dmixed, 'tn', out_dtype=WIRE)
    ds5_out = _mm("d_s5_out", dmixed, wf['w_out'][:S5_WIDTH], 'nt')
    drw_out = _mm("d_rw_out", dmixed, wf['w_out'][S5_WIDTH:], 'nt')

    dy_wkv, dr_b, dkp_b, dv_b, dgate, g['rw_ln_w'], g['rw_ln_b'], g['rw_r_k'] = _tok_bwd(
        "rw_post_bwd", _f_rw_post, post_toks, post_par, [drw_out], [F32] * 5, [True] * 3, tw)
    late_exchange = _exchange_plan([shard_major[n] if n in shard_major else
                                    _to_shard_major(g[n], SHARDED[n]).astype(WIRE) for n in LATE_NAMES])
    dseqs, late_pieces = _wkv_bwd(seqs, ck, dy_wkv, late_exchange)
    pre_cots = [(dseqs[0], dr_b), (dseqs[1],), (dseqs[2], dkp_b), (dseqs[3], dv_b), (dseqs[4],), (dseqs[5],),
                (dgate,)]
    dz, *dpre = _rw_pre_bwd(z, pre_par, pre_cots, tw)
    for n, d in zip(['rw_shift_mu', 'rw_w0', 'rw_w2', 'rw_a0', 'rw_a2', 'rw_g2', 'rw_k_k', 'rw_k_a'], dpre):
        g[n] = d

    dypre, du_a, g['s5_d'], g['s5_glu_w'], g['s5_glu_b'] = _tok_bwd(
        "s5_post_bwd", _f_s5_post, [ypre, u], s5_par, [ds5_out], [F32, F32], [True] * 3, tw)
    dxs = _s5_expand("d_s5_x", dypre, c_blk)
    dc_blk = _s5_block_grad("d_s5_c", dypre, xs)
    dbu, dlam_row = _s5_scan_bwd(dxs, xs, lam_row, tm)
    du = _s5_contract("d_s5_u", dbu, b_blk, add=du_a, out_dtype=BF16)
    db_blk = _s5_block_grad("d_s5_b", u, dbu)
    dbuild = _small_bwd("s5_build_bwd", _f_s5_build, build_ins, (db_blk, dc_blk))
    lam_cots = (dlam_row[:, :S5_LANES].reshape(gp), dlam_row[:, S5_LANES:].reshape(gp),
                dbuild[0].reshape(gp), dbuild[1].reshape(gp))
    g['s5_lam_re'], g['s5_lam_im'], g['s5_log_step'] = _small_bwd("s5_lam_bwd", _f_s5_lam, lam_ins, lam_cots)
    from_t = lambda a, perm: a.reshape(S5_GROUP, S5_GROUPS, S5_STATE).transpose(perm)
    g['s5_b_re'], g['s5_b_im'] = from_t(dbuild[2], (1, 2, 0)), from_t(dbuild[3], (1, 2, 0))
    g['s5_c_re'], g['s5_c_im'] = from_t(dbuild[4], (1, 0, 2)), from_t(dbuild[5], (1, 0, 2))

    dproj = jnp.concatenate([du, dz], axis=1)
    g['w_in'] = _mm("d_w_in", xn, dproj, 'tn', out_dtype=WIRE)
    dxn = _mm("d_xn", dproj, wf['w_in'], 'nt')
    grad_x, g['norm_mix'] = _tok_bwd("norm_in_bwd", _f_norm_in, [x], [norm_mix], [dxn], [F32], [True], tw,
                                     add_to=(0, dx_a))
    return loss[0, 0], grad_x, g, late_pieces


def _step(x, p, target, w, m, v):
    shards = {n: _mat(w[n]).astype(BF16) for n in SHARDED_NAMES}
    early = _run_plan("gather_early", _gather_plan([shards[n] for n in EARLY_NAMES]))
    wf = {n: _from_shard_major(a, SHARDED[n]) for n, a in zip(EARLY_NAMES, early)}
    ws = {n: w[n] for n in SMALL_NAMES}

    loss, grad_x, g, late_pieces = _local_step(x[0], p[0, 0], target[0], wf, ws, shards)

    early_plan = _exchange_plan([_to_shard_major(g[n], SHARDED[n]).astype(WIRE) for n in EARLY_NAMES],
                                _pack_small({n: g[n] for n in SMALL_NAMES}, tail=loss))
    *early_pieces, by_dev = _run_plan("exchange_early", early_plan)
    pieces = dict(zip(LATE_NAMES + EARLY_NAMES, list(late_pieces) + early_pieces))
    halves = [_ew("add_devices_" + n, _sum_slots, [pieces[n]], [pieces[n].shape[1:]])[0] for n in SHARDED_NAMES]
    (small_piece,) = _ew("add_devices_small", _sum_slots, [by_dev], [by_dev.shape[1:]])
    both, small_g = _share_cores(halves, small_piece)

    kinds = [{}, {}, {}, {}]
    for n, gn in zip(SHARDED_NAMES, both):
        shard = _mat(w[n]).shape
        res = _ew("adamw_" + n, _adamw, [gn.reshape(shard), _mat(w[n]), _mat(m[n]), _mat(v[n])], [shard] * 3)
        for kind, a in zip(kinds, [gn] + list(res)):
            kind[n] = a.reshape(w[n].shape)
    flat = (8 * SMALL_ROWS, PACK_COLS)
    packed = [_pack_small({n: d[n] for n in SMALL_NAMES}).reshape(flat) for d in (w, m, v)]
    small_res = _ew("adamw_small", _adamw, [small_g.reshape(flat)] + packed, [flat] * 3)
    small_shapes = {n: w[n].shape for n in SMALL_NAMES}
    for kind, a in zip(kinds, [small_g] + list(small_res)):
        kind.update(_unpack_small(a, small_shapes))
    total = small_g.reshape(-1)[sum(math.prod(s) for s in small_shapes.values())]
    return (total, grad_x[None], *[kind[n] for kind in kinds for n in WEIGHT_NAMES])


def kernel(x, p, norm_mix, w_in, s5_lam_re, s5_lam_im, s5_log_step, s5_b_re, s5_b_im, s5_c_re, s5_c_im, s5_d, s5_glu_w, s5_glu_b, rw_shift_mu, rw_w0, rw_w2, rw_a0, rw_a2, rw_g2, rw_k_k, rw_k_a, rw_r_k, rw_ln_w, rw_ln_b, w_out, norm_ffn, ffn_w1, ffn_w3, ffn_w2, norm_ple, ple_gate_w, ple_up_w, final_norm, loss_target, m_norm_mix, m_w_in, m_s5_lam_re, m_s5_lam_im, m_s5_log_step, m_s5_b_re, m_s5_b_im, m_s5_c_re, m_s5_c_im, m_s5_d, m_s5_glu_w, m_s5_glu_b, m_rw_shift_mu, m_rw_w0, m_rw_w2, m_rw_a0, m_rw_a2, m_rw_g2, m_rw_k_k, m_rw_k_a, m_rw_r_k, m_rw_ln_w, m_rw_ln_b, m_w_out, m_norm_ffn, m_ffn_w1, m_ffn_w3, m_ffn_w2, m_norm_ple, m_ple_gate_w, m_ple_up_w, m_final_norm, v_norm_mix, v_w_in, v_s5_lam_re, v_s5_lam_im, v_s5_log_step, v_s5_b_re, v_s5_b_im, v_s5_c_re, v_s5_c_im, v_s5_d, v_s5_glu_w, v_s5_glu_b, v_rw_shift_mu, v_rw_w0, v_rw_w2, v_rw_a0, v_rw_a2, v_rw_g2, v_rw_k_k, v_rw_k_a, v_rw_r_k, v_rw_ln_w, v_rw_ln_b, v_w_out, v_norm_ffn, v_ffn_w1, v_ffn_w3, v_ffn_w2, v_norm_ple, v_ple_gate_w, v_ple_up_w, v_final_norm):
    args = dict(locals())
    w = {n: args[n] for n in WEIGHT_NAMES}
    m = {n: args["m_" + n] for n in WEIGHT_NAMES}
    v = {n: args["v_" + n] for n in WEIGHT_NAMES}
    return _step(x, p, loss_target, w, m, v)
```

```python
import functools
import math
from typing import Any, Callable, NamedTuple, Sequence

import jax
import jax.numpy as jnp
from jax import lax
from jax.experimental import pallas as pl
from jax.experimental.pallas import tpu as pltpu

F32 = jnp.float32
BF16 = jnp.bfloat16
MESH = pl.DeviceIdType.MESH

S5_WIDTH = 512
RWKV_WIDTH = 512
S5_GROUP = 16
S5_GROUPS = 32
S5_STATE = 64
S5_LANES = S5_GROUPS * S5_STATE
S5_TILE_GROUPS = 8
S5_TILES = S5_GROUPS // S5_TILE_GROUPS
S5_TILE_CH = S5_TILE_GROUPS * S5_GROUP
S5_TILE_LANES = S5_TILE_GROUPS * S5_STATE
HEAD = 64
HEADS = 8
DECAY_LORA = 64
AAA_LORA = 64
GATE_LORA = 128
FFN_HIDDEN = 2816
FFN_TILE = FFN_HIDDEN // 2
RMS_EPS = 1e-6
GN_EPS = 64e-5
L2_EPS = 1e-12
ADAM_LR = 0.001
ADAM_B1 = 0.9
ADAM_B2 = 0.999
ADAM_EPS = 1e-08
ADAM_WD = 0.01
ADAM_STEP = 10

WKV_CHUNK = 64
SCAN_UNROLL = 4
WIRE = jnp.bfloat16
WKV_PASSES = 1
VMEM_LIMIT_BYTES = 48 * 1024 * 1024
LANE = 128
PACK_COLS = 1024
SMALL_ROWS = 24

WEIGHT_NAMES = ['norm_mix', 'w_in', 's5_lam_re', 's5_lam_im', 's5_log_step', 's5_b_re', 's5_b_im', 's5_c_re',
                's5_c_im', 's5_d', 's5_glu_w', 's5_glu_b', 'rw_shift_mu', 'rw_w0', 'rw_w2', 'rw_a0', 'rw_a2',
                'rw_g2', 'rw_k_k', 'rw_k_a', 'rw_r_k', 'rw_ln_w', 'rw_ln_b', 'w_out', 'norm_ffn', 'ffn_w1',
                'ffn_w3', 'ffn_w2', 'norm_ple', 'ple_gate_w', 'ple_up_w', 'final_norm']
SHARDED = {'w_in': 1, 's5_glu_w': 0, 'rw_w2': 1, 'rw_a2': 1, 'rw_g2': 1, 'w_out': 0, 'ffn_w1': 1, 'ffn_w3': 1,
           'ffn_w2': 0, 'ple_gate_w': 0, 'ple_up_w': 1}
SHARDED_NAMES = [n for n in WEIGHT_NAMES if n in SHARDED]
LATE_NAMES = ['w_out', 'ffn_w1', 'ffn_w3', 'ffn_w2', 'ple_gate_w', 'ple_up_w']
EARLY_NAMES = [n for n in SHARDED_NAMES if n not in LATE_NAMES]
LATE_GATHER = {'s5_scan': ['w_out', 'ple_gate_w', 'ple_up_w'], 'wkv_fwd': ['ffn_w1', 'ffn_w3'], 'ffn_up': ['ffn_w2']}
SMALL_NAMES = [n for n in WEIGHT_NAMES if n not in SHARDED]


def _params(sem=None):
    return pltpu.CompilerParams(dimension_semantics=sem, vmem_limit_bytes=VMEM_LIMIT_BYTES)


def _tile(n, target):
    best = None
    for d in range(LANE, min(n, target) + 1, LANE):
        if n % d == 0:
            best = d
    return n if best is None else best


_NN = (((1,), (0,)), ((), ()))
_NT = (((1,), (1,)), ((), ()))
_TN = (((0,), (0,)), ((), ()))


def _split(a):
    a = a.astype(F32)
    hi = a.astype(BF16)
    return hi, (a - hi.astype(F32)).astype(BF16)


def _dg(a, b, dims, passes):
    dg = lambda p, q: lax.dot_general(p, q, dims, preferred_element_type=F32)
    if passes == 1:
        return dg(a.astype(BF16), b.astype(BF16))
    bh, bl = _split(b)
    return dg(a.astype(BF16), bh) + dg(a.astype(BF16), bl)


_DOT_BWD = {_NN: (("g", "b", _NT), ("a", "g", _TN)),
            _NT: (("g", "b", _NN), ("g", "a", _TN)),
            _TN: (("b", "g", _NT), ("a", "g", _NN))}


@functools.partial(jax.custom_vjp, nondiff_argnums=(2, 3))
def _dot(a, b, dims, passes):
    return _dg(a, b, dims, passes)


def _dot_fwd(a, b, dims, passes):
    return _dg(a, b, dims, passes), (a, b)


def _dot_bwd(dims, passes, res, g):
    env = {"a": res[0], "b": res[1], "g": g}
    return tuple(_dg(env[p], env[q], d, passes) for p, q, d in _DOT_BWD[dims])


_dot.defvjp(_dot_fwd, _dot_bwd)


def _bdot(x, w):
    return _dot(x, w, _NN, 1)


@jax.custom_vjp
def _shift_down(z):
    return pltpu.roll(z, 1, 0)


def _shift_down_fwd(z):
    return pltpu.roll(z, 1, 0), None


def _shift_down_bwd(_, g):
    return (pltpu.roll(g, g.shape[0] - 1, 0),)


_shift_down.defvjp(_shift_down_fwd, _shift_down_bwd)


def _head_sum_impl(x):
    r = lax.broadcasted_iota(jnp.int32, (LANE, LANE), 0) // HEAD
    c = lax.broadcasted_iota(jnp.int32, (LANE, LANE), 1) // HEAD
    ones = (r == c).astype(BF16)
    hi, lo = _split(x)
    dg = lambda p: lax.dot_general(p, ones, _NN, preferred_element_type=F32)
    tiles = [slice(j, j + LANE) for j in range(0, x.shape[1], LANE)]
    return jnp.concatenate([dg(hi[:, s]) + dg(lo[:, s]) for s in tiles], axis=1)


@jax.custom_vjp
def _head_sum(x):
    return _head_sum_impl(x)


_head_sum.defvjp(lambda x: (_head_sum_impl(x), None), lambda _, g: (_head_sum_impl(g),))


def _mm(name, a, b, mode, out_dtype=F32, tm=1024, tn=1024, tk=1536, plan=None):
    if mode == 'nn':
        (m, k), (_, n) = a.shape, b.shape
    elif mode == 'nt':
        (m, k), (n, _) = a.shape, b.shape
    else:
        (k, m), (_, n) = a.shape, b.shape
    tm, tn, tk = _tile(m, tm), _tile(n, tn), _tile(k, tk)
    nm, nn, nk = m // tm, n // tn, k // tk
    dims = {'nn': _NN, 'nt': _NT, 'tn': _TN}[mode]
    plan = _NO_PLAN if plan is None else plan
    parts, plan_in_specs, plan_out_shape, plan_out_specs, plan_sems = _carry(plan, 2, 1)

    def body(*refs):
        a_ref, b_ref, o_ref = refs[0], refs[1], refs[2 + len(plan.ins)]
        acc_ref = refs[3 + len(plan.ins) + len(plan.out_shape)]
        i, j, kk = pl.program_id(0), pl.program_id(1), pl.program_id(2)

        if plan is not _NO_PLAN:
            pl.when((i == 0) & (j == 0) & (kk == 0))(lambda: plan.start(*parts(refs)))

        @pl.when(kk == 0)
        def _():
            acc_ref[...] = jnp.zeros_like(acc_ref)

        acc_ref[...] += _dg(a_ref[...], b_ref[...], dims, 1)

        @pl.when(kk == nk - 1)
        def _():
            o_ref[...] = acc_ref[...].astype(o_ref.dtype)

        if plan is not _NO_PLAN:
            pl.when((i == nm - 1) & (j == nn - 1) & (kk == nk - 1))(lambda: plan.wait(*parts(refs)))

    if mode == 'tn':
        a_spec = pl.BlockSpec((tk, tm), lambda i, j, l: (l, i))
    else:
        a_spec = pl.BlockSpec((tm, tk), lambda i, j, l: (i, l))
    if mode == 'nt':
        b_spec = pl.BlockSpec((tn, tk), lambda i, j, l: (j, l))
    else:
        b_spec = pl.BlockSpec((tk, tn), lambda i, j, l: (l, j))
    res = pl.pallas_call(
        body, name=name,
        out_shape=[jax.ShapeDtypeStruct((m, n), out_dtype)] + plan_out_shape,
        grid=(nm, nn, nk),
        in_specs=[a_spec, b_spec] + plan_in_specs,
        out_specs=[pl.BlockSpec((tm, tn), lambda i, j, l: (i, j))] + plan_out_specs,
        scratch_shapes=[pltpu.VMEM((tm, tn), F32)] + plan_sems,
        compiler_params=_params(("parallel", "parallel", "arbitrary") if plan is _NO_PLAN else ("arbitrary",) * 3),
    )(a, b, *plan.ins)
    return res[0] if plan is _NO_PLAN else (res[0], res[1:])


def _mm_tiles(name, a, b, mode, out_shape, grid, a_spec, b_spec, o_spec, add=None, out_dtype=F32):
    dims = {'nn': _NN, 'nt': _NT, 'tn': _TN}[mode]
    nk = grid[2]
    extra = [] if add is None else [add]

    def body(a_ref, b_ref, *refs):
        o_ref, acc_ref = refs[len(extra):]
        kk = pl.program_id(2)

        @pl.when(kk == 0)
        def _():
            acc_ref[...] = refs[0][...].astype(F32) if extra else jnp.zeros_like(acc_ref)

        acc_ref[...] += _dg(a_ref[...], b_ref[...], dims, 1)

        @pl.when(kk == nk - 1)
        def _():
            o_ref[...] = acc_ref[...].astype(o_ref.dtype)

    return pl.pallas_call(
        body, name=name,
        out_shape=jax.ShapeDtypeStruct(out_shape, out_dtype),
        grid=grid, in_specs=[a_spec, b_spec] + [o_spec] * len(extra), out_specs=o_spec,
        scratch_shapes=[pltpu.VMEM(o_spec.block_shape, F32)],
        compiler_params=_params(("parallel", "parallel", "arbitrary")),
    )(a, b, *extra)


def _s5_expand(name, u, blk, tm=2048):
    t = u.shape[0]
    tm = min(tm, t)
    ch, ln, nt = S5_TILE_CH, S5_TILE_LANES, S5_TILES
    return _mm_tiles(name, u, blk, 'nn', (t, 2 * S5_LANES), (t // tm, 2 * nt, 1),
                     pl.BlockSpec((tm, ch), lambda i, j, l: (i, j % nt)),
                     pl.BlockSpec((ch, ln), lambda i, j, l: (j % nt, j // nt)),
                     pl.BlockSpec((tm, ln), lambda i, j, l: (i, j)), out_dtype=BF16)


def _s5_contract(name, x, blk, tm=2048, add=None, out_dtype=F32):
    t = x.shape[0]
    tm = min(tm, t)
    ch, ln, nt = S5_TILE_CH, S5_TILE_LANES, S5_TILES
    return _mm_tiles(name, x, blk, 'nt', (t, S5_WIDTH), (t // tm, nt, 2),
                     pl.BlockSpec((tm, ln), lambda i, j, l: (i, j + nt * l)),
                     pl.BlockSpec((ch, ln), lambda i, j, l: (j, l)),
                     pl.BlockSpec((tm, ch), lambda i, j, l: (i, j)), add=add, out_dtype=out_dtype)


def _s5_block_grad(name, u, x, tk=2048):
    t = u.shape[0]
    tk = min(tk, t)
    ch, ln, nt = S5_TILE_CH, S5_TILE_LANES, S5_TILES
    return _mm_tiles(name, u, x, 'tn', (S5_WIDTH, 2 * ln), (nt, 2, t // tk),
                     pl.BlockSpec((tk, ch), lambda i, j, l: (l, i)),
                     pl.BlockSpec((tk, ln), lambda i, j, l: (l, i + nt * j)),
                     pl.BlockSpec((ch, ln), lambda i, j, l: (i, j)))


def _full_spec(p):
    nd = p.ndim
    return pl.BlockSpec(p.shape, lambda i, nd=nd: (0,) * nd)


def _tok_fwd(name, fn, toks, params, outs, tm):
    t = toks[0].shape[0]
    nt, npar = len(toks), len(params)

    def body(*refs):
        tv = [r[...].astype(F32) for r in refs[:nt]]
        pv = [r[...].astype(F32) for r in refs[nt:nt + npar]]
        res = fn(*tv, *pv)
        for r, v in zip(refs[nt + npar:], res):
            r[...] = v.astype(r.dtype)

    return pl.pallas_call(
        body, name=name,
        out_shape=[jax.ShapeDtypeStruct((t, w), d) for w, d in outs],
        grid=(t // tm,),
        in_specs=[pl.BlockSpec((tm, a.shape[1]), lambda i: (i, 0)) for a in toks] + [_full_spec(p) for p in params],
        out_specs=[pl.BlockSpec((tm, w), lambda i: (i, 0)) for w, _ in outs],
        compiler_params=_params(("parallel",)),
    )(*toks, *params)


def _tok_bwd(name, fn, toks, params, cots, dtok, dpar, tm, acc_out=None, add_to=None):
    t = toks[0].shape[0]
    nt, npar = len(toks), len(params)
    cot_arrays = [c for c in cots if c is not None]
    ncot = len(cot_arrays)
    extra = [] if add_to is None else [add_to[1]]
    dtok_idx = [i for i, d in enumerate(dtok) if d is not None]
    dpar_idx = [i for i, d in enumerate(dpar) if d]

    def body(*refs):
        pos = 0
        tin = refs[pos:pos + nt]; pos += nt
        pin = refs[pos:pos + npar]; pos += npar
        cin = refs[pos:pos + ncot]; pos += ncot
        ein = refs[pos:pos + len(extra)]; pos += len(extra)
        dto = refs[pos:pos + len(dtok_idx)]; pos += len(dtok_idx)
        dpo = refs[pos:pos + len(dpar_idx)]; pos += len(dpar_idx)
        acc = refs[pos] if acc_out is not None else None
        first = pl.program_id(0) == 0

        tv = [r[...].astype(F32) for r in tin]
        pv = [r[...].astype(F32) for r in pin]
        res, vjp = jax.vjp(fn, *tv, *pv)
        cit = iter(cin)
        cs = tuple(jnp.ones_like(o) if c is None else next(cit)[...].astype(F32) for c, o in zip(cots, res))
        g = vjp(cs)
        for r, i in zip(dto, dtok_idx):
            v = g[i]
            if add_to is not None and add_to[0] == i:
                v = v + ein[0][...].astype(F32)
            r[...] = v.astype(r.dtype)

        @pl.when(first)
        def _():
            for r in dpo:
                r[...] = jnp.zeros_like(r)
            if acc is not None:
                acc[...] = jnp.zeros_like(acc)

        for r, i in zip(dpo, dpar_idx):
            r[...] += g[nt + i]
        if acc is not None:
            acc[...] += res[acc_out]

    out_shape = [jax.ShapeDtypeStruct(toks[i].shape, dtok[i]) for i in dtok_idx]
    out_shape += [jax.ShapeDtypeStruct(params[i].shape, F32) for i in dpar_idx]
    out_specs = [pl.BlockSpec((tm, toks[i].shape[1]), lambda i_: (i_, 0)) for i in dtok_idx]
    out_specs += [_full_spec(params[i]) for i in dpar_idx]
    if acc_out is not None:
        out_shape.append(jax.ShapeDtypeStruct((1, 1), F32))
        out_specs.append(pl.BlockSpec((1, 1), lambda i_: (0, 0)))
    tok_spec = lambda a: pl.BlockSpec((tm, a.shape[1]), lambda i_: (i_, 0))
    return pl.pallas_call(
        body, name=name,
        out_shape=out_shape,
        grid=(t // tm,),
        in_specs=[tok_spec(a) for a in toks] + [_full_spec(p) for p in params]
        + [tok_spec(c) for c in cot_arrays] + [tok_spec(e) for e in extra],
        out_specs=out_specs,
        compiler_params=_params(("arbitrary",)),
    )(*toks, *params, *cot_arrays, *extra)


def _small_fwd(name, fn, ins, outs):
    n = len(ins)

    def body(*refs):
        res = fn(*[r[...] for r in refs[:n]])
        for r, v in zip(refs[n:], res):
            r[...] = v.astype(r.dtype)

    return pl.pallas_call(
        body, name=name,
        out_shape=[jax.ShapeDtypeStruct(s, d) for s, d in outs],
        compiler_params=_params(),
    )(*ins)


def _small_bwd(name, fn, ins, cots):
    n = len(ins)

    def body(*refs):
        _, vjp = jax.vjp(fn, *[r[...] for r in refs[:n]])
        g = vjp(tuple(r[...] for r in refs[n:n + len(cots)]))
        for r, v in zip(refs[n + len(cots):], g):
            r[...] = v

    return pl.pallas_call(
        body, name=name,
        out_shape=[jax.ShapeDtypeStruct(a.shape, F32) for a in ins],
        compiler_params=_params(),
    )(*ins, *cots)


def _rms(x, g):
    return x * lax.rsqrt(jnp.mean(x * x, axis=-1, keepdims=True) + RMS_EPS) * g


def _f_norm_in(x, g):
    return (_rms(x, g),)


def _f_mix_res(x, mixed, g):
    h1 = x + mixed
    return h1, _rms(h1, g)


def _f_ffn_act(a13):
    a1, a3 = a13[:, :FFN_HIDDEN], a13[:, FFN_HIDDEN:]
    return (jax.nn.silu(a1) * a3,)


def _f_ffn_res(h1, ffo, g):
    h2 = h1 + ffo
    return h2, _rms(h2, g)


def _f_loss(h2, gpre, pu, target, g):
    h3 = h2 + jax.nn.sigmoid(gpre) * pu
    y = _rms(h3, g)
    err = jnp.square(y - target)
    return (0.5 * jnp.sum(jnp.mean(err, axis=-1, keepdims=True), axis=0, keepdims=True),)


def _f_s5_post(ypre, u, d, glu_w, glu_b):
    z = jax.nn.gelu(ypre + u * d)
    return (z * jax.nn.sigmoid(_bdot(z, glu_w) + glu_b),)


def _softplus(x):
    return jnp.maximum(x, 0.0) + jnp.log(1.0 + jnp.exp(-jnp.abs(x)))


def _f_rw_pre(z, carry, shift_mu, w0, w2, a0, a2, g2, k_k, k_a):
    rw = RWKV_WIDTH
    first_row = lax.broadcasted_iota(jnp.int32, z.shape, 0) == 0
    prev = jnp.where(first_row, carry, _shift_down(z))
    zs = z + (prev - z) * shift_mu
    o1, o2 = 3 * rw + DECAY_LORA, 3 * rw + DECAY_LORA + AAA_LORA
    r, k, v = zs[:, :rw], zs[:, rw:2 * rw], zs[:, 2 * rw:3 * rw]
    wl, al, gl = zs[:, 3 * rw:o1], zs[:, o1:o2], zs[:, o2:]
    w = -_softplus(-(w0 + _bdot(jnp.tanh(wl), w2))) - 0.5
    log_decay = -jnp.exp(w)
    a = jax.nn.sigmoid(a0 + _bdot(al, a2))
    g = _bdot(jax.nn.sigmoid(gl), g2)
    kk = k * k_k
    norm = jnp.sqrt(_head_sum(kk * kk))
    kk = kk / jnp.maximum(norm, L2_EPS)
    kp = k * (1.0 + (a - 1.0) * k_a)
    return r, log_decay, kp, v, -kk, kk * a, g


def _f_rw_post(y, r, kp, v, g, ln_w, ln_b, r_k):
    yc = y - _head_sum(y) * (1.0 / HEAD)
    var = _head_sum(yc * yc) * (1.0 / HEAD)
    yn = yc * lax.rsqrt(var + GN_EPS) * ln_w + ln_b
    bonus = _head_sum(r * kp * r_k) * v
    return ((yn + bonus) * g,)


def _f_s5_lam(lam_re, lam_im, log_step):
    step = jnp.exp(log_step)
    dr, di = lam_re * step, lam_im * step
    e = jnp.exp(dr)
    lbr, lbi = e * jnp.cos(di), e * jnp.sin(di)
    nr, ni = lbr - 1.0, lbi
    den = lam_re * lam_re + lam_im * lam_im
    return lbr, lbi, (nr * lam_re + ni * lam_im) / den, (ni * lam_re - nr * lam_im) / den


def _f_s5_build(coef_r, coef_i, btr, bti, ctr, cti):
    bbr = coef_r * btr - coef_i * bti
    bbi = coef_r * bti + coef_i * btr
    shape = (S5_WIDTH, S5_TILE_LANES)
    rows = (lax.broadcasted_iota(jnp.int32, shape, 0) % S5_TILE_CH) // S5_GROUP
    cols = lax.broadcasted_iota(jnp.int32, shape, 1) // S5_STATE
    mask = (rows == cols).astype(F32)

    def blocks(m):
        per_tile = [m[:, S5_TILE_LANES * i:S5_TILE_LANES * (i + 1)] for i in range(S5_TILES)]
        return jnp.concatenate([t for t in per_tile for _ in range(S5_TILE_GROUPS)], axis=0) * mask

    return (jnp.concatenate([blocks(bbr), blocks(bbi)], axis=1),
            jnp.concatenate([blocks(ctr), -blocks(cti)], axis=1))


HALO = 8


def _rw_pre_specs(z, params, tm, order):
    halo_blocks = tm // HALO
    return ([pl.BlockSpec((tm, z.shape[1]), lambda i: (order(i), 0)),
             pl.BlockSpec((HALO, z.shape[1]), lambda i: (jnp.maximum(order(i) * halo_blocks - 1, 0), 0))]
            + [_full_spec(p) for p in params])


def _rw_pre_fwd(z, params, tm):
    t = z.shape[0]
    npar = len(params)

    def body(z_ref, halo_ref, *refs):
        carry = jnp.where(pl.program_id(0) == 0, 0.0, halo_ref[pl.ds(HALO - 1, 1), :])
        res = _f_rw_pre(z_ref[...], carry, *[r[...].astype(F32) for r in refs[:npar]])
        for r, v in zip(refs[npar:], res):
            r[...] = v

    return pl.pallas_call(
        body, name="rw_pre",
        out_shape=[jax.ShapeDtypeStruct((t, RWKV_WIDTH), F32)] * 7,
        grid=(t // tm,),
        in_specs=_rw_pre_specs(z, params, tm, lambda i: i),
        out_specs=[pl.BlockSpec((tm, RWKV_WIDTH), lambda i: (i, 0))] * 7,
        compiler_params=_params(("parallel",)),
    )(z, z, *params)


def _rw_pre_bwd(z, params, cots, tm):
    t = z.shape[0]
    nt = t // tm
    npar = len(params)
    order = lambda i: nt - 1 - i
    flat_cots = [a for group in cots for a in group]
    ncot = len(flat_cots)

    def body(z_ref, halo_ref, *refs):
        pin, cin = refs[:npar], list(refs[npar:npar + ncot])
        dz_ref = refs[npar + ncot]
        dpo = refs[npar + ncot + 1:npar + ncot + 1 + npar]
        dcarry_ref = refs[npar + ncot + 1 + npar]
        i = pl.program_id(0)

        @pl.when(i == 0)
        def _():
            dcarry_ref[...] = jnp.zeros_like(dcarry_ref)
            for r in dpo:
                r[...] = jnp.zeros_like(r)

        carry = jnp.where(i == nt - 1, 0.0, halo_ref[pl.ds(HALO - 1, 1), :])
        _, vjp = jax.vjp(_f_rw_pre, z_ref[...], carry, *[r[...].astype(F32) for r in pin])
        g = vjp(tuple(sum(cin.pop(0)[...] for _ in group) for group in cots))
        last_row = lax.broadcasted_iota(jnp.int32, z_ref.shape, 0) == tm - 1
        dz_ref[...] = (g[0] + jnp.where(last_row, dcarry_ref[...], 0.0)).astype(dz_ref.dtype)
        dcarry_ref[...] = g[1]
        for r, v in zip(dpo, g[2:]):
            r[...] += v

    tok = lambda w: pl.BlockSpec((tm, w), lambda i: (order(i), 0))
    return pl.pallas_call(
        body, name="rw_pre_bwd",
        out_shape=[jax.ShapeDtypeStruct(z.shape, BF16)] + [jax.ShapeDtypeStruct(p.shape, F32) for p in params],
        grid=(nt,),
        in_specs=_rw_pre_specs(z, params, tm, order) + [tok(RWKV_WIDTH)] * ncot,
        out_specs=[tok(z.shape[1])] + [_full_spec(p) for p in params],
        scratch_shapes=[pltpu.VMEM((1, z.shape[1]), F32)],
        compiler_params=_params(("arbitrary",)),
    )(z, z, *params, *flat_cots)


def _s5_scan(bu, lam, tm, plan):
    t, w = bu.shape
    h = w // 2
    nt = t // tm
    parts, plan_in_specs, plan_out_shape, plan_out_specs, plan_sems = _carry(plan, 2, 1)

    def body(*refs):
        bu_ref, lam_ref = refs[:2]
        xb_ref = refs[2 + len(plan.ins)]
        carry_ref = refs[3 + len(plan.ins) + len(plan.out_shape)]
        x_ref, refs = refs[-1], refs[:-1]

        @pl.when(pl.program_id(0) == 0)
        def _():
            carry_ref[...] = jnp.zeros_like(carry_ref)
            plan.start(*parts(refs))

        lr, li = lam_ref[:, :h], lam_ref[:, h:]
        x_ref[...] = bu_ref[...].astype(F32)

        def step(s, c):
            cr, ci = c
            row = pl.ds(s, 1)
            nr = lr * cr - li * ci + x_ref[row, :h]
            ni = lr * ci + li * cr + x_ref[row, h:]
            x_ref[row, :h] = nr
            x_ref[row, h:] = ni
            return nr, ni

        cr, ci = lax.fori_loop(0, tm, step, (carry_ref[:, :h], carry_ref[:, h:]), unroll=SCAN_UNROLL)
        carry_ref[:, :h] = cr
        carry_ref[:, h:] = ci
        xb_ref[...] = x_ref[...].astype(BF16)

        @pl.when(pl.program_id(0) == nt - 1)
        def _():
            plan.wait(*parts(refs))

    spec = pl.BlockSpec((tm, w), lambda i: (i, 0))
    res = pl.pallas_call(
        body, name="s5_scan",
        out_shape=[jax.ShapeDtypeStruct((t, w), BF16)] + plan_out_shape,
        grid=(nt,),
        in_specs=[spec, pl.BlockSpec((1, w), lambda i: (0, 0))] + plan_in_specs,
        out_specs=[spec] + plan_out_specs,
        scratch_shapes=[pltpu.VMEM((1, w), F32)] + plan_sems + [pltpu.VMEM((tm, w), F32)],
        compiler_params=_params(("arbitrary",)),
    )(bu, lam, *plan.ins)
    return res[0], res[1:]


def _s5_scan_bwd(dx, xb, lam, tm):
    t, w = dx.shape
    h = w // 2
    nt = t // tm
    halo = BF16_SUBLANES

    rows8 = 8

    def body(dx_ref, xb_ref, halo_ref, lam_ref, dbu_out_ref, dlam_ref, carry_ref, dbu_ref, xp_ref):
        @pl.when(pl.program_id(0) == 0)
        def _():
            carry_ref[...] = jnp.zeros_like(carry_ref)
            dlam_ref[...] = jnp.zeros_like(dlam_ref)

        lr, li = lam_ref[:, :h], lam_ref[:, h:]
        dbu_ref[...] = dx_ref[...].astype(F32)

        def step(s, c):
            cr, ci = c
            row = pl.ds(tm - 1 - s, 1)
            nr = lr * cr + li * ci + dbu_ref[row, :h]
            ni = lr * ci - li * cr + dbu_ref[row, h:]
            dbu_ref[row, :h] = nr
            dbu_ref[row, h:] = ni
            return nr, ni

        cr, ci = lax.fori_loop(0, tm, step, (carry_ref[:, :h], carry_ref[:, h:]), unroll=SCAN_UNROLL)
        carry_ref[:, :h] = cr
        carry_ref[:, h:] = ci
        halo_rows = lax.broadcasted_iota(jnp.int32, (halo, w), 0)
        before = jnp.sum(jnp.where(halo_rows == halo - 1, halo_ref[...].astype(F32), 0.0), axis=0, keepdims=True)
        before = jnp.where(pl.program_id(0) == nt - 1, 0.0, before)
        first_row = lax.broadcasted_iota(jnp.int32, (tm, w), 0) == 0
        xp_ref[...] = jnp.where(first_row, before, pltpu.roll(xb_ref[...].astype(F32), 1, 0))

        def accumulate(s, acc):
            ar, ai = acc
            rows = pl.ds(pl.multiple_of(s * rows8, rows8), rows8)
            gr, gi = dbu_ref[rows, :h], dbu_ref[rows, h:]
            pr, pi_ = xp_ref[rows, :h], xp_ref[rows, h:]
            return ar + (gr * pr + gi * pi_), ai + (gi * pr - gr * pi_)

        zero = jnp.zeros((rows8, h), F32)
        ar, ai = lax.fori_loop(0, tm // rows8, accumulate, (zero, zero))
        dlam_ref[:, :h] += jnp.sum(ar, axis=0, keepdims=True)
        dlam_ref[:, h:] += jnp.sum(ai, axis=0, keepdims=True)
        dbu_out_ref[...] = dbu_ref[...].astype(BF16)

    spec = pl.BlockSpec((tm, w), lambda i: (nt - 1 - i, 0))
    halo_spec = pl.BlockSpec((halo, w), lambda i: (jnp.maximum((nt - 1 - i) * (tm // halo) - 1, 0), 0))
    row_spec = pl.BlockSpec((1, w), lambda i: (0, 0))
    return pl.pallas_call(
        body, name="s5_scan_bwd",
        out_shape=[jax.ShapeDtypeStruct((t, w), BF16), jax.ShapeDtypeStruct((1, w), F32)],
        grid=(nt,),
        in_specs=[spec, spec, halo_spec, row_spec],
        out_specs=[spec, row_spec],
        scratch_shapes=[pltpu.VMEM((1, w), F32), pltpu.VMEM((tm, w), F32), pltpu.VMEM((tm, w), F32)],
        compiler_params=_params(("arbitrary",)),
    )(dx, xb, xb, lam)


def _unit_lower_inverses_impl(ns):
    c = ns[0].shape[0]
    eye = (lax.broadcasted_iota(jnp.int32, (c, c), 0) == lax.broadcasted_iota(jnp.int32, (c, c), 1)).astype(F32)
    inv = [eye + n for n in ns]
    pw = [_dg(n, n, _NN, WKV_PASSES) for n in ns]
    for _ in range(int(math.log2(c)) - 2):
        both = [_dg(jnp.concatenate([i, q], axis=0), q, _NN, WKV_PASSES) for i, q in zip(inv, pw)]
        inv = [i + q[:c] for i, q in zip(inv, both)]
        pw = [q[c:] for q in both]
    return tuple(i + _dg(i, q, _NN, WKV_PASSES) for i, q in zip(inv, pw))


@jax.custom_vjp
def _unit_lower_inverses(ns):
    return _unit_lower_inverses_impl(ns)


def _unit_lower_inverses_fwd(ns):
    inv = _unit_lower_inverses_impl(ns)
    return inv, inv


def _unit_lower_inverses_bwd(inv, g):
    left = [_dg(i, gi, _TN, WKV_PASSES) for i, gi in zip(inv, g)]
    return (tuple(_dg(q, i, _NT, WKV_PASSES) for q, i in zip(left, inv)),)


_unit_lower_inverses.defvjp(_unit_lower_inverses_fwd, _unit_lower_inverses_bwd)


def _wkv_chunks(s0, r, lw, k, v, a, b):
    c = r[0].shape[0]
    row = lax.broadcasted_iota(jnp.int32, (c, c), 0)
    col = lax.broadcasted_iota(jnp.int32, (c, c), 1)
    incl, strict = col <= row, col < row
    tri = incl.astype(F32)
    each = lambda f, *xs: [f(*t) for t in zip(*xs)]
    stack = lambda p, q: jnp.concatenate([p, q], axis=0)
    dot = lambda p, q, dims=_NN: _dot(p, q, dims, WKV_PASSES)
    lc = each(lambda l: _dot(tri, l, _NN, 2), lw)
    e_neg = each(lambda l: jnp.exp(-l), lc)
    ar = each(lambda x, z, l, w: stack(x * jnp.exp(l - w), z * jnp.exp(l)), a, r, lc, lw)
    bk = each(lambda x, z, e: stack(x * e, z * e), b, k, e_neg)
    m = each(lambda p, q: dot(p, q, _NT), ar, bk)
    mab = each(lambda q: jnp.where(strict, q[:c, :c], 0.0), m)
    mak_mrk = each(lambda q: stack(jnp.where(strict, q[:c, c:], 0.0), jnp.where(incl, q[c:, c:], 0.0)), m)
    mrb = each(lambda q: jnp.where(incl, q[c:, :c], 0.0), m)
    xy = each(lambda p, s, q, z: dot(p, s, _NT) + dot(q, z), ar, s0, mak_mrk, v)
    inv = _unit_lower_inverses(tuple(mab))
    u = each(lambda i, q: dot(i, q[:c]), inv, xy)
    y = each(lambda q, z, p: q[c:] + dot(z, p), xy, mrb, u)
    e_tot = each(lambda l: jnp.exp(jnp.sum(l, axis=0, keepdims=True)), lw)
    s1 = each(lambda s, p, z, q, e: (s + dot(stack(p, z), q, _TN)) * e, s0, u, v, bk, e_tot)
    return y, s1


def _carry(plan, n_args, n_outs):
    n_in, n_out = len(plan.ins), len(plan.out_shape)

    def parts(refs):
        base = n_args + n_in + n_outs
        return refs[n_args:n_args + n_in], refs[base:base + n_out], refs[base + n_out + 1:]

    return parts, [HBM_SPEC] * n_in, list(plan.out_shape), [HBM_SPEC] * n_out, list(plan.sems)


def _head_cols(ref):
    return tuple(ref[:, h * HEAD:(h + 1) * HEAD] for h in range(HEADS))


def _wkv_fwd(seqs, plan):
    t, w = seqs[0].shape
    c, n = WKV_CHUNK, HEAD
    nc = t // c
    parts, plan_in_specs, plan_out_shape, plan_out_specs, plan_sems = _carry(plan, 6, 2)

    def body(*refs):
        ins, (y_ref, ck_ref) = refs[:6], refs[6 + len(plan.ins):8 + len(plan.ins)]
        s_ref = refs[8 + len(plan.ins) + len(plan.out_shape)]

        @pl.when(pl.program_id(0) == 0)
        def _():
            s_ref[...] = jnp.zeros_like(s_ref)
            plan.start(*parts(refs))

        s0 = tuple(s_ref[h] for h in range(HEADS))
        ys, s1 = _wkv_chunks(s0, *[_head_cols(r) for r in ins])
        for h in range(HEADS):
            ck_ref[0, h] = s0[h]
            y_ref[:, h * n:(h + 1) * n] = ys[h]
            s_ref[h] = s1[h]

        @pl.when(pl.program_id(0) == nc - 1)
        def _():
            plan.wait(*parts(refs))

    spec = pl.BlockSpec((c, w), lambda i: (i, 0))
    res = pl.pallas_call(
        body, name="wkv_fwd",
        out_shape=[jax.ShapeDtypeStruct((t, w), F32), jax.ShapeDtypeStruct((nc, HEADS, n, n), F32)] + plan_out_shape,
        grid=(nc,),
        in_specs=[spec] * 6 + plan_in_specs,
        out_specs=[spec, pl.BlockSpec((1, HEADS, n, n), lambda i: (i, 0, 0, 0))] + plan_out_specs,
        scratch_shapes=[pltpu.VMEM((HEADS, n, n), F32)] + plan_sems,
        compiler_params=_params(("arbitrary",)),
    )(*seqs, *plan.ins)
    return res[0], res[1], res[2:]


def _wkv_bwd(seqs, ck, dy, plan):
    t, w = seqs[0].shape
    c, n = WKV_CHUNK, HEAD
    nc = t // c
    parts, plan_in_specs, plan_out_shape, plan_out_specs, plan_sems = _carry(plan, 8, 6)

    def body(*refs):
        ins, ck_ref, dy_ref = refs[:6], refs[6], refs[7]
        outs = refs[8 + len(plan.ins):14 + len(plan.ins)]
        ds_ref = refs[14 + len(plan.ins) + len(plan.out_shape)]

        @pl.when(pl.program_id(0) == 0)
        def _():
            ds_ref[...] = jnp.zeros_like(ds_ref)
            plan.start(*parts(refs))

        s0 = tuple(ck_ref[0, h] for h in range(HEADS))
        _, vjp = jax.vjp(_wkv_chunks, s0, *[_head_cols(r) for r in ins])
        g = vjp((list(_head_cols(dy_ref)), [ds_ref[h] for h in range(HEADS)]))
        for h in range(HEADS):
            ds_ref[h] = g[0][h]
            for o, d in zip(outs, g[1:]):
                o[:, h * n:(h + 1) * n] = d[h]

        @pl.when(pl.program_id(0) == nc - 1)
        def _():
            plan.wait(*parts(refs))

    spec = pl.BlockSpec((c, w), lambda i: (nc - 1 - i, 0))
    res = pl.pallas_call(
        body, name="wkv_bwd",
        out_shape=[jax.ShapeDtypeStruct((t, w), F32)] * 6 + plan_out_shape,
        grid=(nc,),
        in_specs=[spec] * 6 + [pl.BlockSpec((1, HEADS, n, n), lambda i: (nc - 1 - i, 0, 0, 0)), spec] + plan_in_specs,
        out_specs=[spec] * 6 + plan_out_specs,
        scratch_shapes=[pltpu.VMEM((HEADS, n, n), F32)] + plan_sems,
        compiler_params=_params(("arbitrary",)),
    )(*seqs, ck, dy, *plan.ins)
    return res[:6], res[6:]


def _coords():
    return lax.axis_index("x"), lax.axis_index("y"), lax.axis_index("c")


def _flip(v, f):
    return 1 - v if f else v


_CHIP_FLIPS = [(1, 0), (0, 1), (1, 1)]
_DEV_FLIPS = [(fx, fy, fc) for fx in (0, 1) for fy in (0, 1) for fc in (0, 1) if (fx, fy, fc) != (0, 0, 0)]
HBM_SPEC = pl.BlockSpec(memory_space=pl.ANY)


def _chip_peer(k, x, y):
    fx, fy = _CHIP_FLIPS[k]
    return _flip(x, fx), _flip(y, fy)


def _dev_peer(k, x, y, c):
    fx, fy, fc = _DEV_FLIPS[k]
    return _flip(x, fx), _flip(y, fy), _flip(c, fc)


def _rows_of_core(ref, core):
    h = ref.shape[-2] // 2
    rows = pl.ds(pl.multiple_of(core * h, 8), h)
    return ref.at[rows, :] if len(ref.shape) == 2 else ref.at[:, rows, :]


class _Plan(NamedTuple):
    ins: Sequence[Any]
    out_shape: Sequence[Any]
    sems: Sequence[Any]
    start: Callable
    wait: Callable


_NO_PLAN = _Plan([], [], [], lambda *_: None, lambda *_: None)


def _run_plan(name, plan):
    n_in, n_out = len(plan.ins), len(plan.out_shape)

    def body(*refs):
        parts = refs[:n_in], refs[n_in:n_in + n_out], refs[n_in + n_out:]
        plan.start(*parts)
        plan.wait(*parts)

    return pl.pallas_call(
        body, name=name, out_shape=list(plan.out_shape),
        in_specs=[HBM_SPEC] * n_in, out_specs=[HBM_SPEC] * n_out, scratch_shapes=list(plan.sems),
    )(*plan.ins)


def _gather_plan(shards):
    n = len(shards)

    def copies(srcs, outs, sems):
        send_sems, recv_sems, local_sems = sems
        x, y, c = _coords()
        me = 2 * x + y

        def remote(i, k, arriving):
            px, py = _chip_peer(k, x, y)
            return pltpu.make_async_remote_copy(
                src_ref=srcs[i], dst_ref=outs[i].at[2 * px + py if arriving else me],
                send_sem=send_sems.at[i, k], recv_sem=recv_sems.at[i, k],
                device_id=(px, py, c), device_id_type=MESH)

        own = [pltpu.make_async_copy(srcs[i], outs[i].at[me], local_sems.at[i]) for i in range(n)]
        pairs = [(i, k) for k in range(3) for i in range(n)]
        return own, [remote(i, k, False) for i, k in pairs], [remote(i, k, True) for i, k in pairs]

    return _Plan(
        ins=shards, out_shape=[jax.ShapeDtypeStruct((4,) + s.shape, s.dtype) for s in shards],
        sems=[pltpu.SemaphoreType.DMA((n, 3)), pltpu.SemaphoreType.DMA((n, 3)), pltpu.SemaphoreType.DMA((n,))],
        start=functools.partial(_start_copies, copies), wait=functools.partial(_wait_copies, copies))


def _start_copies(copies, ins, outs, sems):
    own, sends, _ = copies(ins, outs, sems)
    for cp in own + sends:
        cp.start()


def _wait_copies(copies, ins, outs, sems):
    own, sends, arrivals = copies(ins, outs, sems)
    for cp in arrivals:
        cp.wait_recv()
    for cp in sends:
        cp.wait_send()
    for cp in own:
        cp.wait()


def _exchange_plan(gs, small=None):
    n = len(gs)
    arrays = list(gs) + ([] if small is None else [small])

    def copies(srcs, outs, sems):
        send_sems, recv_sems, local_sems = sems
        x, y, c = _coords()
        me = 4 * x + 2 * y + c

        def piece(i, px, py, pc):
            if i == n:
                return srcs[i].at[4 * px + 2 * py + pc]
            return _rows_of_core(srcs[i].at[2 * px + py], pc)

        def remote(i, k, arriving):
            px, py, pc = _dev_peer(k, x, y, c)
            return pltpu.make_async_remote_copy(
                src_ref=piece(i, px, py, pc), dst_ref=outs[i].at[4 * px + 2 * py + pc if arriving else me],
                send_sem=send_sems.at[i, k], recv_sem=recv_sems.at[i, k],
                device_id=(px, py, pc), device_id_type=MESH)

        own = [pltpu.make_async_copy(piece(i, x, y, c), outs[i].at[me], local_sems.at[i]) for i in range(len(arrays))]
        pairs = [(i, k) for k in range(7) for i in range(len(arrays))]
        return own, [remote(i, k, False) for i, k in pairs], [remote(i, k, True) for i, k in pairs]

    out_shape = [jax.ShapeDtypeStruct((8, g.shape[1] // 2, g.shape[2]), g.dtype) for g in gs]
    if small is not None:
        out_shape.append(jax.ShapeDtypeStruct(small.shape, small.dtype))
    m = len(arrays)
    return _Plan(
        ins=arrays, out_shape=out_shape,
        sems=[pltpu.SemaphoreType.DMA((m, 7)), pltpu.SemaphoreType.DMA((m, 7)), pltpu.SemaphoreType.DMA((m,))],
        start=functools.partial(_start_copies, copies), wait=functools.partial(_wait_copies, copies))


def _share_cores(halves, small):
    n = len(halves)

    def body(*refs):
        srcs, small_src, outs, small_out = refs[:n], refs[n], refs[n + 1:2 * n + 1], refs[2 * n + 1]
        mine, theirs = refs[2 * n + 2:3 * n + 2], refs[3 * n + 2:4 * n + 2]
        send_sems, recv_sems, ssend, srecv, local_sems = refs[4 * n + 2:]
        x, y, c = _coords()
        me = 4 * x + 2 * y + c

        def big(i):
            return pltpu.make_async_remote_copy(
                src_ref=mine[i], dst_ref=theirs[i], send_sem=send_sems.at[i], recv_sem=recv_sems.at[i],
                device_id=(x, y, 1 - c), device_id_type=MESH)

        def tiny(k, arriving):
            px, py, pc = _dev_peer(k, x, y, c)
            return pltpu.make_async_remote_copy(
                src_ref=small_src, dst_ref=small_out.at[4 * px + 2 * py + pc if arriving else me],
                send_sem=ssend.at[k], recv_sem=srecv.at[k], device_id=(px, py, pc), device_id_type=MESH)

        small_sends = [tiny(k, False) for k in range(7)]
        own_small = pltpu.make_async_copy(small_src, small_out.at[me], local_sems.at[2 * n])
        stage = [pltpu.make_async_copy(srcs[i], mine[i], local_sems.at[i]) for i in range(n)]
        for cp in small_sends + [own_small] + stage:
            cp.start()
        sends = []
        for i in range(n):
            stage[i].wait()
            sends.append(big(i))
            sends[-1].start()
        store = [pltpu.make_async_copy(mine[i], outs[i].at[c], local_sems.at[i]) for i in range(n)]
        for cp in store:
            cp.start()
        for i in range(n):
            big(i).wait_recv()
            store.append(pltpu.make_async_copy(theirs[i], outs[i].at[1 - c], local_sems.at[n + i]))
            store[-1].start()
        for k in range(7):
            tiny(k, True).wait_recv()
        for cp in sends + small_sends:
            cp.wait_send()
        for cp in store + [own_small]:
            cp.wait()

    staged = [pltpu.VMEM(s.shape, s.dtype) for s in halves]
    res = pl.pallas_call(
        body, name="share_cores",
        out_shape=[jax.ShapeDtypeStruct((2,) + s.shape, s.dtype) for s in halves]
        + [jax.ShapeDtypeStruct((8,) + small.shape, small.dtype)],
        in_specs=[HBM_SPEC] * (n + 1), out_specs=[HBM_SPEC] * (n + 1),
        scratch_shapes=staged + staged + [
            pltpu.SemaphoreType.DMA((n,)), pltpu.SemaphoreType.DMA((n,)),
            pltpu.SemaphoreType.DMA((7,)), pltpu.SemaphoreType.DMA((7,)),
            pltpu.SemaphoreType.DMA((2 * n + 1,))],
        compiler_params=pltpu.CompilerParams(vmem_limit_bytes=VMEM_LIMIT_BYTES),
    )(*halves, small)
    return res[:n], res[n]


BF16_SUBLANES = 16


def _row_tile(n, target, step=BF16_SUBLANES):
    return max([d for d in range(step, min(n, target) + 1, step) if n % d == 0] or [n])


def _ew(name, fn, ins, outs, block_bytes=2 << 20):
    rows, cols = ins[0].shape[-2:]
    lead = max(math.prod(a.shape[:-2]) for a in ins)
    tr = _row_tile(rows, max(8, block_bytes // (4 * cols * lead)))
    n = len(ins)

    def spec(shape):
        if len(shape) == 2:
            return pl.BlockSpec((tr, cols), lambda i: (i, 0))
        return pl.BlockSpec((shape[0], tr, cols), lambda i: (0, i, 0))

    def body(*refs):
        res = fn(*[r[...] for r in refs[:n]])
        for r, v in zip(refs[n:], res):
            r[...] = v

    return pl.pallas_call(
        body, name=name,
        out_shape=[jax.ShapeDtypeStruct(s, F32) for s in outs],
        grid=(rows // tr,),
        in_specs=[spec(a.shape) for a in ins],
        out_specs=[spec(s) for s in outs],
        compiler_params=_params(("parallel",)),
    )(*ins)


def _sum_slots(a):
    total = a[0].astype(F32)
    for s in range(1, a.shape[0]):
        total = total + a[s].astype(F32)
    return (total,)


def _adamw(g, w, m, v):
    bc1 = 1.0 - ADAM_B1 ** ADAM_STEP
    bc2 = 1.0 - ADAM_B2 ** ADAM_STEP
    m_new = ADAM_B1 * m + (1.0 - ADAM_B1) * g
    v_new = ADAM_B2 * v + (1.0 - ADAM_B2) * jnp.square(g)
    delta = -ADAM_LR * ((m_new / bc1) / (jnp.sqrt(v_new / bc2) + ADAM_EPS) + ADAM_WD * w)
    return delta, m_new, v_new


def _mat(a):
    return a.reshape(a.shape[-2:])


def _to_shard_major(full, axis):
    rows, cols = full.shape
    if axis == 0:
        return full.reshape(4, rows // 4, cols)
    return full.reshape(rows, 4, cols // 4).transpose(1, 0, 2)


def _from_shard_major(a, axis):
    _, r, cs = a.shape
    if axis == 0:
        return a.reshape(4 * r, cs)
    return a.transpose(1, 0, 2).reshape(r, 4 * cs)


def _pack_small(arrays, tail=None):
    flat = [arrays[n].reshape(-1) for n in SMALL_NAMES] + ([] if tail is None else [tail.reshape(1)])
    used = sum(a.shape[0] for a in flat)
    flat.append(jnp.zeros((8 * SMALL_ROWS * PACK_COLS - used,), F32))
    return jnp.concatenate(flat).reshape(8, SMALL_ROWS, PACK_COLS)


def _unpack_small(packed, shapes):
    flat = packed.reshape(-1)
    out, off = {}, 0
    for n in SMALL_NAMES:
        size = math.prod(shapes[n])
        out[n] = flat[off:off + size].reshape(shapes[n])
        off += size
    return out


def _row(a):
    return a.reshape(1, -1)


def _local_step(x, p, target, wf, ws, late_shards):
    wf = dict(wf)
    t = x.shape[0]
    tm = min(256, t)
    tw = min(512, t)
    tx = min(1024, t)
    g = {}

    lam_re, lam_im = ws['s5_lam_re'].reshape(S5_GROUPS, S5_STATE), ws['s5_lam_im'].reshape(S5_GROUPS, S5_STATE)
    log_step = ws['s5_log_step'].reshape(S5_GROUPS, 1)
    gp = (S5_GROUPS, S5_STATE)
    lam_ins = (lam_re, lam_im, log_step)
    lbr, lbi, cfr, cfi = _small_fwd("s5_lam", _f_s5_lam, lam_ins, [(gp, F32)] * 4)
    lam_row = jnp.concatenate([_row(lbr), _row(lbi)], axis=1)
    to_t = lambda a, perm: a.reshape((S5_GROUPS,) + a.shape[-2:]).transpose(perm).reshape(S5_GROUP, S5_LANES)
    build_ins = (_row(cfr), _row(cfi), to_t(ws['s5_b_re'], (2, 0, 1)), to_t(ws['s5_b_im'], (2, 0, 1)),
                 to_t(ws['s5_c_re'], (1, 0, 2)), to_t(ws['s5_c_im'], (1, 0, 2)))
    block_shape = (S5_WIDTH, 2 * S5_TILE_LANES)
    b_blk, c_blk = _small_fwd("s5_build", _f_s5_build, build_ins, [(block_shape, F32)] * 2)

    norm_mix, norm_ffn, norm_ple = _row(ws['norm_mix']), _row(ws['norm_ffn']), _row(ws['norm_ple'])
    final_norm = _row(ws['final_norm'])
    (xn,) = _tok_fwd("norm_in", _f_norm_in, [x], [norm_mix], [(x.shape[1], BF16)], tx)
    u = _mm("proj_s5", xn, wf['w_in'][:, :S5_WIDTH], 'nn')
    z = _mm("proj_rw", xn, wf['w_in'][:, S5_WIDTH:], 'nn')

    bu = _s5_expand("s5_bu", u, b_blk)
    def late_plan(carrier):
        return _gather_plan([late_shards[n] for n in LATE_GATHER[carrier]])

    def arrived(carrier, got):
        wf.update({n: _from_shard_major(a, SHARDED[n]) for n, a in zip(LATE_GATHER[carrier], got)})

    xs, got = _s5_scan(bu, lam_row, tm, late_plan('s5_scan'))
    arrived('s5_scan', got)
    ypre = _s5_contract("s5_y", xs, c_blk)
    s5_par = [_row(ws['s5_d']), wf['s5_glu_w'], _row(ws['s5_glu_b'])]
    (s5_out,) = _tok_fwd("s5_post", _f_s5_post, [ypre, u], s5_par, [(S5_WIDTH, BF16)], tx)

    pre_par = [_row(ws['rw_shift_mu']), _row(ws['rw_w0']), wf['rw_w2'], _row(ws['rw_a0']), wf['rw_a2'],
               wf['rw_g2'], _row(ws['rw_k_k']), _row(ws['rw_k_a'])]
    r, lw, kp, v, an, bn, gate = _rw_pre_fwd(z, pre_par, tw)
    seqs = [r, lw, kp, v, an, bn]
    y_wkv, ck, got = _wkv_fwd(seqs, late_plan('wkv_fwd'))
    arrived('wkv_fwd', got)
    post_par = [_row(ws['rw_ln_w']), _row(ws['rw_ln_b']), _row(ws['rw_r_k'])]
    post_toks = [y_wkv, r, kp, v, gate]
    (rw_out,) = _tok_fwd("rw_post", _f_rw_post, post_toks, post_par, [(RWKV_WIDTH, BF16)], tx)

    mixcat = jnp.concatenate([s5_out, rw_out], axis=1)
    mixed = _mm("mix_out", mixcat, wf['w_out'], 'nn')
    h1, hn = _tok_fwd("mix_res", _f_mix_res, [x, mixed], [norm_ffn], [(x.shape[1], F32), (x.shape[1], BF16)], tx)
    w13 = jnp.concatenate([wf['ffn_w1'], wf['ffn_w3']], axis=1)
    a13, got = _mm("ffn_up", hn, w13, 'nn', out_dtype=BF16, tn=FFN_TILE, plan=late_plan('ffn_up'))
    arrived('ffn_up', got)
    (f,) = _tok_fwd("ffn_act", _f_ffn_act, [a13], [], [(FFN_HIDDEN, BF16)], tw)
    ffo = _mm("ffn_down", f, wf['ffn_w2'], 'nn')
    h2, hp = _tok_fwd("ffn_res", _f_ffn_res, [h1, ffo], [norm_ple], [(x.shape[1], F32), (x.shape[1], BF16)], tx)
    gpre = _mm("ple_gate", hp, wf['ple_gate_w'], 'nn')
    pu = _mm("ple_up", p, wf['ple_up_w'], 'nn')

    dh2, dgpre, dpu, g['final_norm'], loss = _tok_bwd(
        "loss", _f_loss, [h2, gpre, pu, target], [final_norm], [None],
        [F32, BF16, BF16, None], [True], tw, acc_out=0)
    g['ple_gate_w'] = _mm("d_ple_gate_w", hp, dgpre, 'tn', out_dtype=WIRE)
    g['ple_up_w'] = _mm("d_ple_up_w", p, dpu, 'tn', out_dtype=WIRE)
    dhp = _mm("d_hp", dgpre, wf['ple_gate_w'], 'nt')
    dh1, dffo, g['norm_ple'] = _tok_bwd("ffn_res_bwd", _f_ffn_res, [h1, ffo], [norm_ple], [dh2, dhp],
                                        [F32, BF16], [True], tw)
    g['ffn_w2'] = _mm("d_ffn_w2", f, dffo, 'tn', out_dtype=WIRE, tm=FFN_TILE)
    df = _mm("d_f", dffo, wf['ffn_w2'], 'nt', out_dtype=BF16, tn=FFN_TILE)
    (da13,) = _tok_bwd("ffn_act_bwd", _f_ffn_act, [a13], [], [df], [BF16], [], tw)
    dw13 = _mm("d_ffn_w13", hn, da13, 'tn', out_dtype=WIRE, tn=FFN_TILE)
    dw13 = dw13.reshape(dw13.shape[0], 8, FFN_HIDDEN // 4).transpose(1, 0, 2)
    shard_major = {'ffn_w1': dw13[:4], 'ffn_w3': dw13[4:]}
    dhn = _mm("d_hn", da13, w13, 'nt')
    dx_a, dmixed, g['norm_ffn'] = _tok_bwd("mix_res_bwd", _f_mix_res, [x, mixed], [norm_ffn], [dh1, dhn],
                                           [F32, BF16], [True], tw)
    g['w_out'] = _mm("d_w_out", mixcat, dmixed, 'tn', out_dtype=WIRE)
    ds5_out = _mm("d_s5_out", dmixed, wf['w_out'][:S5_WIDTH], 'nt')
    drw_out = _mm("d_rw_out", dmixed, wf['w_out'][S5_WIDTH:], 'nt')

    dy_wkv, dr_b, dkp_b, dv_b, dgate, g['rw_ln_w'], g['rw_ln_b'], g['rw_r_k'] = _tok_bwd(
        "rw_post_bwd", _f_rw_post, post_toks, post_par, [drw_out], [F32] * 5, [True] * 3, tw)
    late_exchange = _exchange_plan([shard_major[n] if n in shard_major else
                                    _to_shard_major(g[n], SHARDED[n]).astype(WIRE) for n in LATE_NAMES])
    dseqs, late_pieces = _wkv_bwd(seqs, ck, dy_wkv, late_exchange)
    pre_cots = [(dseqs[0], dr_b), (dseqs[1],), (dseqs[2], dkp_b), (dseqs[3], dv_b), (dseqs[4],), (dseqs[5],),
                (dgate,)]
    dz, *dpre = _rw_pre_bwd(z, pre_par, pre_cots, tw)
    for n, d in zip(['rw_shift_mu', 'rw_w0', 'rw_w2', 'rw_a0', 'rw_a2', 'rw_g2', 'rw_k_k', 'rw_k_a'], dpre):
        g[n] = d

    dypre, du_a, g['s5_d'], g['s5_glu_w'], g['s5_glu_b'] = _tok_bwd(
        "s5_post_bwd", _f_s5_post, [ypre, u], s5_par, [ds5_out], [F32, F32], [True] * 3, tw)
    dxs = _s5_expand("d_s5_x", dypre, c_blk)
    dc_blk = _s5_block_grad("d_s5_c", dypre, xs)
    dbu, dlam_row = _s5_scan_bwd(dxs, xs, lam_row, tm)
    du = _s5_contract("d_s5_u", dbu, b_blk, add=du_a, out_dtype=BF16)
    db_blk = _s5_block_grad("d_s5_b", u, dbu)
    dbuild = _small_bwd("s5_build_bwd", _f_s5_build, build_ins, (db_blk, dc_blk))
    lam_cots = (dlam_row[:, :S5_LANES].reshape(gp), dlam_row[:, S5_LANES:].reshape(gp),
                dbuild[0].reshape(gp), dbuild[1].reshape(gp))
    g['s5_lam_re'], g['s5_lam_im'], g['s5_log_step'] = _small_bwd("s5_lam_bwd", _f_s5_lam, lam_ins, lam_cots)
    from_t = lambda a, perm: a.reshape(S5_GROUP, S5_GROUPS, S5_STATE).transpose(perm)
    g['s5_b_re'], g['s5_b_im'] = from_t(dbuild[2], (1, 2, 0)), from_t(dbuild[3], (1, 2, 0))
    g['s5_c_re'], g['s5_c_im'] = from_t(dbuild[4], (1, 0, 2)), from_t(dbuild[5], (1, 0, 2))

    dproj = jnp.concatenate([du, dz], axis=1)
    g['w_in'] = _mm("d_w_in", xn, dproj, 'tn', out_dtype=WIRE)
    dxn = _mm("d_xn", dproj, wf['w_in'], 'nt')
    grad_x, g['norm_mix'] = _tok_bwd("norm_in_bwd", _f_norm_in, [x], [norm_mix], [dxn], [F32], [True], tw,
                                     add_to=(0, dx_a))
    return loss[0, 0], grad_x, g, late_pieces


def _step(x, p, target, w, m, v):
    shards = {n: _mat(w[n]).astype(BF16) for n in SHARDED_NAMES}
    early = _run_plan("gather_early", _gather_plan([shards[n] for n in EARLY_NAMES]))
    wf = {n: _from_shard_major(a, SHARDED[n]) for n, a in zip(EARLY_NAMES, early)}
    ws = {n: w[n] for n in SMALL_NAMES}

    loss, grad_x, g, late_pieces = _local_step(x[0], p[0, 0], target[0], wf, ws, shards)

    early_plan = _exchange_plan([_to_shard_major(g[n], SHARDED[n]).astype(WIRE) for n in EARLY_NAMES],
                                _pack_small({n: g[n] for n in SMALL_NAMES}, tail=loss))
    *early_pieces, by_dev = _run_plan("exchange_early", early_plan)
    pieces = dict(zip(LATE_NAMES + EARLY_NAMES, list(late_pieces) + early_pieces))
    halves = [_ew("add_devices_" + n, _sum_slots, [pieces[n]], [pieces[n].shape[1:]])[0] for n in SHARDED_NAMES]
    (small_piece,) = _ew("add_devices_small", _sum_slots, [by_dev], [by_dev.shape[1:]])
    both, small_g = _share_cores(halves, small_piece)

    kinds = [{}, {}, {}, {}]
    for n, gn in zip(SHARDED_NAMES, both):
        shard = _mat(w[n]).shape
        res = _ew("adamw_" + n, _adamw, [gn.reshape(shard), _mat(w[n]), _mat(m[n]), _mat(v[n])], [shard] * 3)
        for kind, a in zip(kinds, [gn] + list(res)):
            kind[n] = a.reshape(w[n].shape)
    flat = (8 * SMALL_ROWS, PACK_COLS)
    packed = [_pack_small({n: d[n] for n in SMALL_NAMES}).reshape(flat) for d in (w, m, v)]
    small_res = _ew("adamw_small", _adamw, [small_g.reshape(flat)] + packed, [flat] * 3)
    small_shapes = {n: w[n].shape for n in SMALL_NAMES}
    for kind, a in zip(kinds, [small_g] + list(small_res)):
        kind.update(_unpack_small(a, small_shapes))
    total = small_g.reshape(-1)[sum(math.prod(s) for s in small_shapes.values())]
    return (total, grad_x[None], *[kind[n] for kind in kinds for n in WEIGHT_NAMES])


def kernel(x, p, norm_mix, w_in, s5_lam_re, s5_lam_im, s5_log_step, s5_b_re, s5_b_im, s5_c_re, s5_c_im, s5_d, s5_glu_w, s5_glu_b, rw_shift_mu, rw_w0, rw_w2, rw_a0, rw_a2, rw_g2, rw_k_k, rw_k_a, rw_r_k, rw_ln_w, rw_ln_b, w_out, norm_ffn, ffn_w1, ffn_w3, ffn_w2, norm_ple, ple_gate_w, ple_up_w, final_norm, loss_target, m_norm_mix, m_w_in, m_s5_lam_re, m_s5_lam_im, m_s5_log_step, m_s5_b_re, m_s5_b_im, m_s5_c_re, m_s5_c_im, m_s5_d, m_s5_glu_w, m_s5_glu_b, m_rw_shift_mu, m_rw_w0, m_rw_w2, m_rw_a0, m_rw_a2, m_rw_g2, m_rw_k_k, m_rw_k_a, m_rw_r_k, m_rw_ln_w, m_rw_ln_b, m_w_out, m_norm_ffn, m_ffn_w1, m_ffn_w3, m_ffn_w2, m_norm_ple, m_ple_gate_w, m_ple_up_w, m_final_norm, v_norm_mix, v_w_in, v_s5_lam_re, v_s5_lam_im, v_s5_log_step, v_s5_b_re, v_s5_b_im, v_s5_c_re, v_s5_c_im, v_s5_d, v_s5_glu_w, v_s5_glu_b, v_rw_shift_mu, v_rw_w0, v_rw_w2, v_rw_a0, v_rw_a2, v_rw_g2, v_rw_k_k, v_rw_k_a, v_rw_r_k, v_rw_ln_w, v_rw_ln_b, v_w_out, v_norm_ffn, v_ffn_w1, v_ffn_w3, v_ffn_w2, v_norm_ple, v_ple_gate_w, v_ple_up_w, v_final_norm):
    args = dict(locals())
    w = {n: args[n] for n in WEIGHT_NAMES}
    m = {n: args["m_" + n] for n in WEIGHT_NAMES}
    v = {n: args["v_" + n] for n in WEIGHT_NAMES}
    return _step(x, p, loss_target, w, m, v)
```

```python
import functools
import math
from typing import Any, Callable, NamedTuple, Sequence

import jax
import jax.numpy as jnp
from jax import lax
from jax.experimental import pallas as pl
from jax.experimental.pallas import tpu as pltpu

F32 = jnp.float32
BF16 = jnp.bfloat16
MESH = pl.DeviceIdType.MESH

S5_WIDTH = 512
RWKV_WIDTH = 512
S5_GROUP = 16
S5_GROUPS = 32
S5_STATE = 64
S5_LANES = S5_GROUPS * S5_STATE
S5_TILE_GROUPS = 8
S5_TILES = S5_GROUPS // S5_TILE_GROUPS
S5_TILE_CH = S5_TILE_GROUPS * S5_GROUP
S5_TILE_LANES = S5_TILE_GROUPS * S5_STATE
HEAD = 64
HEADS = 8
DECAY_LORA = 64
AAA_LORA = 64
GATE_LORA = 128
FFN_HIDDEN = 2816
FFN_TILE = FFN_HIDDEN // 2
RMS_EPS = 1e-6
GN_EPS = 64e-5
L2_EPS = 1e-12
ADAM_LR = 0.001
ADAM_B1 = 0.9
ADAM_B2 = 0.999
ADAM_EPS = 1e-08
ADAM_WD = 0.01
ADAM_STEP = 10

WKV_CHUNK = 64
SCAN_UNROLL = 4
WIRE = jnp.bfloat16
WKV_PASSES = 1
VMEM_LIMIT_BYTES = 48 * 1024 * 1024
LANE = 128
PACK_COLS = 1024
SMALL_ROWS = 24

WEIGHT_NAMES = ['norm_mix', 'w_in', 's5_lam_re', 's5_lam_im', 's5_log_step', 's5_b_re', 's5_b_im', 's5_c_re',
                's5_c_im', 's5_d', 's5_glu_w', 's5_glu_b', 'rw_shift_mu', 'rw_w0', 'rw_w2', 'rw_a0', 'rw_a2',
                'rw_g2', 'rw_k_k', 'rw_k_a', 'rw_r_k', 'rw_ln_w', 'rw_ln_b', 'w_out', 'norm_ffn', 'ffn_w1',
                'ffn_w3', 'ffn_w2', 'norm_ple', 'ple_gate_w', 'ple_up_w', 'final_norm']
SHARDED = {'w_in': 1, 's5_glu_w': 0, 'rw_w2': 1, 'rw_a2': 1, 'rw_g2': 1, 'w_out': 0, 'ffn_w1': 1, 'ffn_w3': 1,
           'ffn_w2': 0, 'ple_gate_w': 0, 'ple_up_w': 1}
SHARDED_NAMES = [n for n in WEIGHT_NAMES if n in SHARDED]
LATE_NAMES = ['w_out', 'ffn_w1', 'ffn_w3', 'ffn_w2', 'ple_gate_w', 'ple_up_w']
EARLY_NAMES = [n for n in SHARDED_NAMES if n not in LATE_NAMES]
LATE_GATHER = {'s5_scan': ['w_out', 'ple_gate_w', 'ple_up_w'], 'wkv_fwd': ['ffn_w1', 'ffn_w3'], 'ffn_up': ['ffn_w2']}
SMALL_NAMES = [n for n in WEIGHT_NAMES if n not in SHARDED]


def _params(sem=None):
    return pltpu.CompilerParams(dimension_semantics=sem, vmem_limit_bytes=VMEM_LIMIT_BYTES)


def _tile(n, target):
    best = None
    for d in range(LANE, min(n, target) + 1, LANE):
        if n % d == 0:
            best = d
    return n if best is None else best


_NN = (((1,), (0,)), ((), ()))
_NT = (((1,), (1,)), ((), ()))
_TN = (((0,), (0,)), ((), ()))


def _split(a):
    a = a.astype(F32)
    hi = a.astype(BF16)
    return hi, (a - hi.astype(F32)).astype(BF16)


def _dg(a, b, dims, passes):
    dg = lambda p, q: lax.dot_general(p, q, dims, preferred_element_type=F32)
    if passes == 1:
        return dg(a.astype(BF16), b.astype(BF16))
    bh, bl = _split(b)
    return dg(a.astype(BF16), bh) + dg(a.astype(BF16), bl)


_DOT_BWD = {_NN: (("g", "b", _NT), ("a", "g", _TN)),
            _NT: (("g", "b", _NN), ("g", "a", _TN)),
            _TN: (("b", "g", _NT), ("a", "g", _NN))}


@functools.partial(jax.custom_vjp, nondiff_argnums=(2, 3))
def _dot(a, b, dims, passes):
    return _dg(a, b, dims, passes)


def _dot_fwd(a, b, dims, passes):
    return _dg(a, b, dims, passes), (a, b)


def _dot_bwd(dims, passes, res, g):
    env = {"a": res[0], "b": res[1], "g": g}
    return tuple(_dg(env[p], env[q], d, passes) for p, q, d in _DOT_BWD[dims])


_dot.defvjp(_dot_fwd, _dot_bwd)


def _bdot(x, w):
    return _dot(x, w, _NN, 1)


@jax.custom_vjp
def _shift_down(z):
    return pltpu.roll(z, 1, 0)


def _shift_down_fwd(z):
    return pltpu.roll(z, 1, 0), None


def _shift_down_bwd(_, g):
    return (pltpu.roll(g, g.shape[0] - 1, 0),)


_shift_down.defvjp(_shift_down_fwd, _shift_down_bwd)


def _head_sum_impl(x):
    r = lax.broadcasted_iota(jnp.int32, (LANE, LANE), 0) // HEAD
    c = lax.broadcasted_iota(jnp.int32, (LANE, LANE), 1) // HEAD
    ones = (r == c).astype(BF16)
    hi, lo = _split(x)
    dg = lambda p: lax.dot_general(p, ones, _NN, preferred_element_type=F32)
    tiles = [slice(j, j + LANE) for j in range(0, x.shape[1], LANE)]
    return jnp.concatenate([dg(hi[:, s]) + dg(lo[:, s]) for s in tiles], axis=1)


@jax.custom_vjp
def _head_sum(x):
    return _head_sum_impl(x)


_head_sum.defvjp(lambda x: (_head_sum_impl(x), None), lambda _, g: (_head_sum_impl(g),))


def _mm(name, a, b, mode, out_dtype=F32, tm=1024, tn=1024, tk=1536, plan=None):
    if mode == 'nn':
        (m, k), (_, n) = a.shape, b.shape
    elif mode == 'nt':
        (m, k), (n, _) = a.shape, b.shape
    else:
        (k, m), (_, n) = a.shape, b.shape
    tm, tn, tk = _tile(m, tm), _tile(n, tn), _tile(k, tk)
    nm, nn, nk = m // tm, n // tn, k // tk
    dims = {'nn': _NN, 'nt': _NT, 'tn': _TN}[mode]
    plan = _NO_PLAN if plan is None else plan
    parts, plan_in_specs, plan_out_shape, plan_out_specs, plan_sems = _carry(plan, 2, 1)

    def body(*refs):
        a_ref, b_ref, o_ref = refs[0], refs[1], refs[2 + len(plan.ins)]
        acc_ref = refs[3 + len(plan.ins) + len(plan.out_shape)]
        i, j, kk = pl.program_id(0), pl.program_id(1), pl.program_id(2)

        if plan is not _NO_PLAN:
            pl.when((i == 0) & (j == 0) & (kk == 0))(lambda: plan.start(*parts(refs)))

        @pl.when(kk == 0)
        def _():
            acc_ref[...] = jnp.zeros_like(acc_ref)

        acc_ref[...] += _dg(a_ref[...], b_ref[...], dims, 1)

        @pl.when(kk == nk - 1)
        def _():
            o_ref[...] = acc_ref[...].astype(o_ref.dtype)

        if plan is not _NO_PLAN:
            pl.when((i == nm - 1) & (j == nn - 1) & (kk == nk - 1))(lambda: plan.wait(*parts(refs)))

    if mode == 'tn':
        a_spec = pl.BlockSpec((tk, tm), lambda i, j, l: (l, i))
    else:
        a_spec = pl.BlockSpec((tm, tk), lambda i, j, l: (i, l))
    if mode == 'nt':
        b_spec = pl.BlockSpec((tn, tk), lambda i, j, l: (j, l))
    else:
        b_spec = pl.BlockSpec((tk, tn), lambda i, j, l: (l, j))
    res = pl.pallas_call(
        body, name=name,
        out_shape=[jax.ShapeDtypeStruct((m, n), out_dtype)] + plan_out_shape,
        grid=(nm, nn, nk),
        in_specs=[a_spec, b_spec] + plan_in_specs,
        out_specs=[pl.BlockSpec((tm, tn), lambda i, j, l: (i, j))] + plan_out_specs,
        scratch_shapes=[pltpu.VMEM((tm, tn), F32)] + plan_sems,
        compiler_params=_params(("parallel", "parallel", "arbitrary") if plan is _NO_PLAN else ("arbitrary",) * 3),
    )(a, b, *plan.ins)
    return res[0] if plan is _NO_PLAN else (res[0], res[1:])


def _mm_tiles(name, a, b, mode, out_shape, grid, a_spec, b_spec, o_spec, add=None, out_dtype=F32):
    dims = {'nn': _NN, 'nt': _NT, 'tn': _TN}[mode]
    nk = grid[2]
    extra = [] if add is None else [add]

    def body(a_ref, b_ref, *refs):
        o_ref, acc_ref = refs[len(extra):]
        kk = pl.program_id(2)

        @pl.when(kk == 0)
        def _():
            acc_ref[...] = refs[0][...].astype(F32) if extra else jnp.zeros_like(acc_ref)

        acc_ref[...] += _dg(a_ref[...], b_ref[...], dims, 1)

        @pl.when(kk == nk - 1)
        def _():
            o_ref[...] = acc_ref[...].astype(o_ref.dtype)

    return pl.pallas_call(
        body, name=name,
        out_shape=jax.ShapeDtypeStruct(out_shape, out_dtype),
        grid=grid, in_specs=[a_spec, b_spec] + [o_spec] * len(extra), out_specs=o_spec,
        scratch_shapes=[pltpu.VMEM(o_spec.block_shape, F32)],
        compiler_params=_params(("parallel", "parallel", "arbitrary")),
    )(a, b, *extra)


def _s5_expand(name, u, blk, tm=2048):
    t = u.shape[0]
    tm = min(tm, t)
    ch, ln, nt = S5_TILE_CH, S5_TILE_LANES, S5_TILES
    return _mm_tiles(name, u, blk, 'nn', (t, 2 * S5_LANES), (t // tm, 2 * nt, 1),
                     pl.BlockSpec((tm, ch), lambda i, j, l: (i, j % nt)),
                     pl.BlockSpec((ch, ln), lambda i, j, l: (j % nt, j // nt)),
                     pl.BlockSpec((tm, ln), lambda i, j, l: (i, j)), out_dtype=BF16)


def _s5_contract(name, x, blk, tm=2048, add=None, out_dtype=F32):
    t = x.shape[0]
    tm = min(tm, t)
    ch, ln, nt = S5_TILE_CH, S5_TILE_LANES, S5_TILES
    return _mm_tiles(name, x, blk, 'nt', (t, S5_WIDTH), (t // tm, nt, 2),
                     pl.BlockSpec((tm, ln), lambda i, j, l: (i, j + nt * l)),
                     pl.BlockSpec((ch, ln), lambda i, j, l: (j, l)),
                     pl.BlockSpec((tm, ch), lambda i, j, l: (i, j)), add=add, out_dtype=out_dtype)


def _s5_block_grad(name, u, x, tk=2048):
    t = u.shape[0]
    tk = min(tk, t)
    ch, ln, nt = S5_TILE_CH, S5_TILE_LANES, S5_TILES
    return _mm_tiles(name, u, x, 'tn', (S5_WIDTH, 2 * ln), (nt, 2, t // tk),
                     pl.BlockSpec((tk, ch), lambda i, j, l: (l, i)),
                     pl.BlockSpec((tk, ln), lambda i, j, l: (l, i + nt * j)),
                     pl.BlockSpec((ch, ln), lambda i, j, l: (i, j)))


def _full_spec(p):
    nd = p.ndim
    return pl.BlockSpec(p.shape, lambda i, nd=nd: (0,) * nd)


def _tok_fwd(name, fn, toks, params, outs, tm):
    t = toks[0].shape[0]
    nt, npar = len(toks), len(params)

    def body(*refs):
        tv = [r[...].astype(F32) for r in refs[:nt]]
        pv = [r[...].astype(F32) for r in refs[nt:nt + npar]]
        res = fn(*tv, *pv)
        for r, v in zip(refs[nt + npar:], res):
            r[...] = v.astype(r.dtype)

    return pl.pallas_call(
        body, name=name,
        out_shape=[jax.ShapeDtypeStruct((t, w), d) for w, d in outs],
        grid=(t // tm,),
        in_specs=[pl.BlockSpec((tm, a.shape[1]), lambda i: (i, 0)) for a in toks] + [_full_spec(p) for p in params],
        out_specs=[pl.BlockSpec((tm, w), lambda i: (i, 0)) for w, _ in outs],
        compiler_params=_params(("parallel",)),
    )(*toks, *params)


def _tok_bwd(name, fn, toks, params, cots, dtok, dpar, tm, acc_out=None, add_to=None):
    t = toks[0].shape[0]
    nt, npar = len(toks), len(params)
    cot_arrays = [c for c in cots if c is not None]
    ncot = len(cot_arrays)
    extra = [] if add_to is None else [add_to[1]]
    dtok_idx = [i for i, d in enumerate(dtok) if d is not None]
    dpar_idx = [i for i, d in enumerate(dpar) if d]

    def body(*refs):
        pos = 0
        tin = refs[pos:pos + nt]; pos += nt
        pin = refs[pos:pos + npar]; pos += npar
        cin = refs[pos:pos + ncot]; pos += ncot
        ein = refs[pos:pos + len(extra)]; pos += len(extra)
        dto = refs[pos:pos + len(dtok_idx)]; pos += len(dtok_idx)
        dpo = refs[pos:pos + len(dpar_idx)]; pos += len(dpar_idx)
        acc = refs[pos] if acc_out is not None else None
        first = pl.program_id(0) == 0

        tv = [r[...].astype(F32) for r in tin]
        pv = [r[...].astype(F32) for r in pin]
        res, vjp = jax.vjp(fn, *tv, *pv)
        cit = iter(cin)
        cs = tuple(jnp.ones_like(o) if c is None else next(cit)[...].astype(F32) for c, o in zip(cots, res))
        g = vjp(cs)
        for r, i in zip(dto, dtok_idx):
            v = g[i]
            if add_to is not None and add_to[0] == i:
                v = v + ein[0][...].astype(F32)
            r[...] = v.astype(r.dtype)

        @pl.when(first)
        def _():
            for r in dpo:
                r[...] = jnp.zeros_like(r)
            if acc is not None:
                acc[...] = jnp.zeros_like(acc)

        for r, i in zip(dpo, dpar_idx):
            r[...] += g[nt + i]
        if acc is not None:
            acc[...] += res[acc_out]

    out_shape = [jax.ShapeDtypeStruct(toks[i].shape, dtok[i]) for i in dtok_idx]
    out_shape += [jax.ShapeDtypeStruct(params[i].shape, F32) for i in dpar_idx]
    out_specs = [pl.BlockSpec((tm, toks[i].shape[1]), lambda i_: (i_, 0)) for i in dtok_idx]
    out_specs += [_full_spec(params[i]) for i in dpar_idx]
    if acc_out is not None:
        out_shape.append(jax.ShapeDtypeStruct((1, 1), F32))
        out_specs.append(pl.BlockSpec((1, 1), lambda i_: (0, 0)))
    tok_spec = lambda a: pl.BlockSpec((tm, a.shape[1]), lambda i_: (i_, 0))
    return pl.pallas_call(
        body, name=name,
        out_shape=out_shape,
        grid=(t // tm,),
        in_specs=[tok_spec(a) for a in toks] + [_full_spec(p) for p in params]
        + [tok_spec(c) for c in cot_arrays] + [tok_spec(e) for e in extra],
        out_specs=out_specs,
        compiler_params=_params(("arbitrary",)),
    )(*toks, *params, *cot_arrays, *extra)


def _small_fwd(name, fn, ins, outs):
    n = len(ins)

    def body(*refs):
        res = fn(*[r[...] for r in refs[:n]])
        for r, v in zip(refs[n:], res):
            r[...] = v.astype(r.dtype)

    return pl.pallas_call(
        body, name=name,
        out_shape=[jax.ShapeDtypeStruct(s, d) for s, d in outs],
        compiler_params=_params(),
    )(*ins)


def _small_bwd(name, fn, ins, cots):
    n = len(ins)

    def body(*refs):
        _, vjp = jax.vjp(fn, *[r[...] for r in refs[:n]])
        g = vjp(tuple(r[...] for r in refs[n:n + len(cots)]))
        for r, v in zip(refs[n + len(cots):], g):
            r[...] = v

    return pl.pallas_call(
        body, name=name,
        out_shape=[jax.ShapeDtypeStruct(a.shape, F32) for a in ins],
        compiler_params=_params(),
    )(*ins, *cots)


def _rms(x, g):
    return x * lax.rsqrt(jnp.mean(x * x, axis=-1, keepdims=True) + RMS_EPS) * g


def _f_norm_in(x, g):
    return (_rms(x, g),)


def _f_mix_res(x, mixed, g):
    h1 = x + mixed
    return h1, _rms(h1, g)


def _f_ffn_act(a13):
    a1, a3 = a13[:, :FFN_HIDDEN], a13[:, FFN_HIDDEN:]
    return (jax.nn.silu(a1) * a3,)


def _f_ffn_res(h1, ffo, g):
    h2 = h1 + ffo
    return h2, _rms(h2, g)


def _f_loss(h2, gpre, pu, target, g):
    h3 = h2 + jax.nn.sigmoid(gpre) * pu
    y = _rms(h3, g)
    err = jnp.square(y - target)
    return (0.5 * jnp.sum(jnp.mean(err, axis=-1, keepdims=True), axis=0, keepdims=True),)


def _f_s5_post(ypre, u, d, glu_w, glu_b):
    z = jax.nn.gelu(ypre + u * d)
    return (z * jax.nn.sigmoid(_bdot(z, glu_w) + glu_b),)


def _softplus(x):
    return jnp.maximum(x, 0.0) + jnp.log(1.0 + jnp.exp(-jnp.abs(x)))


def _f_rw_pre(z, carry, shift_mu, w0, w2, a0, a2, g2, k_k, k_a):
    rw = RWKV_WIDTH
    first_row = lax.broadcasted_iota(jnp.int32, z.shape, 0) == 0
    prev = jnp.where(first_row, carry, _shift_down(z))
    zs = z + (prev - z) * shift_mu
    o1, o2 = 3 * rw + DECAY_LORA, 3 * rw + DECAY_LORA + AAA_LORA
    r, k, v = zs[:, :rw], zs[:, rw:2 * rw], zs[:, 2 * rw:3 * rw]
    wl, al, gl = zs[:, 3 * rw:o1], zs[:, o1:o2], zs[:, o2:]
    w = -_softplus(-(w0 + _bdot(jnp.tanh(wl), w2))) - 0.5
    log_decay = -jnp.exp(w)
    a = jax.nn.sigmoid(a0 + _bdot(al, a2))
    g = _bdot(jax.nn.sigmoid(gl), g2)
    kk = k * k_k
    norm = jnp.sqrt(_head_sum(kk * kk))
    kk = kk / jnp.maximum(norm, L2_EPS)
    kp = k * (1.0 + (a - 1.0) * k_a)
    return r, log_decay, kp, v, -kk, kk * a, g


def _f_rw_post(y, r, kp, v, g, ln_w, ln_b, r_k):
    yc = y - _head_sum(y) * (1.0 / HEAD)
    var = _head_sum(yc * yc) * (1.0 / HEAD)
    yn = yc * lax.rsqrt(var + GN_EPS) * ln_w + ln_b
    bonus = _head_sum(r * kp * r_k) * v
    return ((yn + bonus) * g,)


def _f_s5_lam(lam_re, lam_im, log_step):
    step = jnp.exp(log_step)
    dr, di = lam_re * step, lam_im * step
    e = jnp.exp(dr)
    lbr, lbi = e * jnp.cos(di), e * jnp.sin(di)
    nr, ni = lbr - 1.0, lbi
    den = lam_re * lam_re + lam_im * lam_im
    return lbr, lbi, (nr * lam_re + ni * lam_im) / den, (ni * lam_re - nr * lam_im) / den


def _f_s5_build(coef_r, coef_i, btr, bti, ctr, cti):
    bbr = coef_r * btr - coef_i * bti
    bbi = coef_r * bti + coef_i * btr
    shape = (S5_WIDTH, S5_TILE_LANES)
    rows = (lax.broadcasted_iota(jnp.int32, shape, 0) % S5_TILE_CH) // S5_GROUP
    cols = lax.broadcasted_iota(jnp.int32, shape, 1) // S5_STATE
    mask = (rows == cols).astype(F32)

    def blocks(m):
        per_tile = [m[:, S5_TILE_LANES * i:S5_TILE_LANES * (i + 1)] for i in range(S5_TILES)]
        return jnp.concatenate([t for t in per_tile for _ in range(S5_TILE_GROUPS)], axis=0) * mask

    return (jnp.concatenate([blocks(bbr), blocks(bbi)], axis=1),
            jnp.concatenate([blocks(ctr), -blocks(cti)], axis=1))


HALO = 8


def _rw_pre_specs(z, params, tm, order):
    halo_blocks = tm // HALO
    return ([pl.BlockSpec((tm, z.shape[1]), lambda i: (order(i), 0)),
             pl.BlockSpec((HALO, z.shape[1]), lambda i: (jnp.maximum(order(i) * halo_blocks - 1, 0), 0))]
            + [_full_spec(p) for p in params])


def _rw_pre_fwd(z, params, tm):
    t = z.shape[0]
    npar = len(params)

    def body(z_ref, halo_ref, *refs):
        carry = jnp.where(pl.program_id(0) == 0, 0.0, halo_ref[pl.ds(HALO - 1, 1), :])
        res = _f_rw_pre(z_ref[...], carry, *[r[...].astype(F32) for r in refs[:npar]])
        for r, v in zip(refs[npar:], res):
            r[...] = v

    return pl.pallas_call(
        body, name="rw_pre",
        out_shape=[jax.ShapeDtypeStruct((t, RWKV_WIDTH), F32)] * 7,
        grid=(t // tm,),
        in_specs=_rw_pre_specs(z, params, tm, lambda i: i),
        out_specs=[pl.BlockSpec((tm, RWKV_WIDTH), lambda i: (i, 0))] * 7,
        compiler_params=_params(("parallel",)),
    )(z, z, *params)


def _rw_pre_bwd(z, params, cots, tm):
    t = z.shape[0]
    nt = t // tm
    npar = len(params)
    order = lambda i: nt - 1 - i
    flat_cots = [a for group in cots for a in group]
    ncot = len(flat_cots)

    def body(z_ref, halo_ref, *refs):
        pin, cin = refs[:npar], list(refs[npar:npar + ncot])
        dz_ref = refs[npar + ncot]
        dpo = refs[npar + ncot + 1:npar + ncot + 1 + npar]
        dcarry_ref = refs[npar + ncot + 1 + npar]
        i = pl.program_id(0)

        @pl.when(i == 0)
        def _():
            dcarry_ref[...] = jnp.zeros_like(dcarry_ref)
            for r in dpo:
                r[...] = jnp.zeros_like(r)

        carry = jnp.where(i == nt - 1, 0.0, halo_ref[pl.ds(HALO - 1, 1), :])
        _, vjp = jax.vjp(_f_rw_pre, z_ref[...], carry, *[r[...].astype(F32) for r in pin])
        g = vjp(tuple(sum(cin.pop(0)[...] for _ in group) for group in cots))
        last_row = lax.broadcasted_iota(jnp.int32, z_ref.shape, 0) == tm - 1
        dz_ref[...] = (g[0] + jnp.where(last_row, dcarry_ref[...], 0.0)).astype(dz_ref.dtype)
        dcarry_ref[...] = g[1]
        for r, v in zip(dpo, g[2:]):
            r[...] += v

    tok = lambda w: pl.BlockSpec((tm, w), lambda i: (order(i), 0))
    return pl.pallas_call(
        body, name="rw_pre_bwd",
        out_shape=[jax.ShapeDtypeStruct(z.shape, BF16)] + [jax.ShapeDtypeStruct(p.shape, F32) for p in params],
        grid=(nt,),
        in_specs=_rw_pre_specs(z, params, tm, order) + [tok(RWKV_WIDTH)] * ncot,
        out_specs=[tok(z.shape[1])] + [_full_spec(p) for p in params],
        scratch_shapes=[pltpu.VMEM((1, z.shape[1]), F32)],
        compiler_params=_params(("arbitrary",)),
    )(z, z, *params, *flat_cots)


def _s5_scan(bu, lam, tm, plan):
    t, w = bu.shape
    h = w // 2
    nt = t // tm
    parts, plan_in_specs, plan_out_shape, plan_out_specs, plan_sems = _carry(plan, 2, 1)

    def body(*refs):
        bu_ref, lam_ref = refs[:2]
        xb_ref = refs[2 + len(plan.ins)]
        carry_ref = refs[3 + len(plan.ins) + len(plan.out_shape)]
        x_ref, refs = refs[-1], refs[:-1]

        @pl.when(pl.program_id(0) == 0)
        def _():
            carry_ref[...] = jnp.zeros_like(carry_ref)
            plan.start(*parts(refs))

        lr, li = lam_ref[:, :h], lam_ref[:, h:]
        x_ref[...] = bu_ref[...].astype(F32)

        def step(s, c):
            cr, ci = c
            row = pl.ds(s, 1)
            nr = lr * cr - li * ci + x_ref[row, :h]
            ni = lr * ci + li * cr + x_ref[row, h:]
            x_ref[row, :h] = nr
            x_ref[row, h:] = ni
            return nr, ni

        cr, ci = lax.fori_loop(0, tm, step, (carry_ref[:, :h], carry_ref[:, h:]), unroll=SCAN_UNROLL)
        carry_ref[:, :h] = cr
        carry_ref[:, h:] = ci
        xb_ref[...] = x_ref[...].astype(BF16)

        @pl.when(pl.program_id(0) == nt - 1)
        def _():
            plan.wait(*parts(refs))

    spec = pl.BlockSpec((tm, w), lambda i: (i, 0))
    res = pl.pallas_call(
        body, name="s5_scan",
        out_shape=[jax.ShapeDtypeStruct((t, w), BF16)] + plan_out_shape,
        grid=(nt,),
        in_specs=[spec, pl.BlockSpec((1, w), lambda i: (0, 0))] + plan_in_specs,
        out_specs=[spec] + plan_out_specs,
        scratch_shapes=[pltpu.VMEM((1, w), F32)] + plan_sems + [pltpu.VMEM((tm, w), F32)],
        compiler_params=_params(("arbitrary",)),
    )(bu, lam, *plan.ins)
    return res[0], res[1:]


def _s5_scan_bwd(dx, xb, lam, tm):
    t, w = dx.shape
    h = w // 2
    nt = t // tm
    halo = BF16_SUBLANES

    rows8 = 8

    def body(dx_ref, xb_ref, halo_ref, lam_ref, dbu_out_ref, dlam_ref, carry_ref, dbu_ref, xp_ref):
        @pl.when(pl.program_id(0) == 0)
        def _():
            carry_ref[...] = jnp.zeros_like(carry_ref)
            dlam_ref[...] = jnp.zeros_like(dlam_ref)

        lr, li = lam_ref[:, :h], lam_ref[:, h:]
        dbu_ref[...] = dx_ref[...].astype(F32)

        def step(s, c):
            cr, ci = c
            row = pl.ds(tm - 1 - s, 1)
            nr = lr * cr + li * ci + dbu_ref[row, :h]
            ni = lr * ci - li * cr + dbu_ref[row, h:]
            dbu_ref[row, :h] = nr
            dbu_ref[row, h:] = ni
            return nr, ni

        cr, ci = lax.fori_loop(0, tm, step, (carry_ref[:, :h], carry_ref[:, h:]), unroll=SCAN_UNROLL)
        carry_ref[:, :h] = cr
        carry_ref[:, h:] = ci
        halo_rows = lax.broadcasted_iota(jnp.int32, (halo, w), 0)
        before = jnp.sum(jnp.where(halo_rows == halo - 1, halo_ref[...].astype(F32), 0.0), axis=0, keepdims=True)
        before = jnp.where(pl.program_id(0) == nt - 1, 0.0, before)
        first_row = lax.broadcasted_iota(jnp.int32, (tm, w), 0) == 0
        xp_ref[...] = jnp.where(first_row, before, pltpu.roll(xb_ref[...].astype(F32), 1, 0))

        def accumulate(s, acc):
            ar, ai = acc
            rows = pl.ds(pl.multiple_of(s * rows8, rows8), rows8)
            gr, gi = dbu_ref[rows, :h], dbu_ref[rows, h:]
            pr, pi_ = xp_ref[rows, :h], xp_ref[rows, h:]
            return ar + (gr * pr + gi * pi_), ai + (gi * pr - gr * pi_)

        zero = jnp.zeros((rows8, h), F32)
        ar, ai = lax.fori_loop(0, tm // rows8, accumulate, (zero, zero))
        dlam_ref[:, :h] += jnp.sum(ar, axis=0, keepdims=True)
        dlam_ref[:, h:] += jnp.sum(ai, axis=0, keepdims=True)
        dbu_out_ref[...] = dbu_ref[...].astype(BF16)

    spec = pl.BlockSpec((tm, w), lambda i: (nt - 1 - i, 0))
    halo_spec = pl.BlockSpec((halo, w), lambda i: (jnp.maximum((nt - 1 - i) * (tm // halo) - 1, 0), 0))
    row_spec = pl.BlockSpec((1, w), lambda i: (0, 0))
    return pl.pallas_call(
        body, name="s5_scan_bwd",
        out_shape=[jax.ShapeDtypeStruct((t, w), BF16), jax.ShapeDtypeStruct((1, w), F32)],
        grid=(nt,),
        in_specs=[spec, spec, halo_spec, row_spec],
        out_specs=[spec, row_spec],
        scratch_shapes=[pltpu.VMEM((1, w), F32), pltpu.VMEM((tm, w), F32), pltpu.VMEM((tm, w), F32)],
        compiler_params=_params(("arbitrary",)),
    )(dx, xb, xb, lam)


def _unit_lower_inverses_impl(ns):
    c = ns[0].shape[0]
    eye = (lax.broadcasted_iota(jnp.int32, (c, c), 0) == lax.broadcasted_iota(jnp.int32, (c, c), 1)).astype(F32)
    inv = [eye + n for n in ns]
    pw = [_dg(n, n, _NN, WKV_PASSES) for n in ns]
    for _ in range(int(math.log2(c)) - 2):
        both = [_dg(jnp.concatenate([i, q], axis=0), q, _NN, WKV_PASSES) for i, q in zip(inv, pw)]
        inv = [i + q[:c] for i, q in zip(inv, both)]
        pw = [q[c:] for q in both]
    return tuple(i + _dg(i, q, _NN, WKV_PASSES) for i, q in zip(inv, pw))


@jax.custom_vjp
def _unit_lower_inverses(ns):
    return _unit_lower_inverses_impl(ns)


def _unit_lower_inverses_fwd(ns):
    inv = _unit_lower_inverses_impl(ns)
    return inv, inv


def _unit_lower_inverses_bwd(inv, g):
    left = [_dg(i, gi, _TN, WKV_PASSES) for i, gi in zip(inv, g)]
    return (tuple(_dg(q, i, _NT, WKV_PASSES) for q, i in zip(left, inv)),)


_unit_lower_inverses.defvjp(_unit_lower_inverses_fwd, _unit_lower_inverses_bwd)


def _wkv_chunks(s0, r, lw, k, v, a, b):
    c = r[0].shape[0]
    row = lax.broadcasted_iota(jnp.int32, (c, c), 0)
    col = lax.broadcasted_iota(jnp.int32, (c, c), 1)
    incl, strict = col <= row, col < row
    tri = incl.astype(F32)
    each = lambda f, *xs: [f(*t) for t in zip(*xs)]
    stack = lambda p, q: jnp.concatenate([p, q], axis=0)
    dot = lambda p, q, dims=_NN: _dot(p, q, dims, WKV_PASSES)
    lc = each(lambda l: _dot(tri, l, _NN, 2), lw)
    e_neg = each(lambda l: jnp.exp(-l), lc)
    ar = each(lambda x, z, l, w: stack(x * jnp.exp(l - w), z * jnp.exp(l)), a, r, lc, lw)
    bk = each(lambda x, z, e: stack(x * e, z * e), b, k, e_neg)
    m = each(lambda p, q: dot(p, q, _NT), ar, bk)
    mab = each(lambda q: jnp.where(strict, q[:c, :c], 0.0), m)
    mak_mrk = each(lambda q: stack(jnp.where(strict, q[:c, c:], 0.0), jnp.where(incl, q[c:, c:], 0.0)), m)
    mrb = each(lambda q: jnp.where(incl, q[c:, :c], 0.0), m)
    xy = each(lambda p, s, q, z: dot(p, s, _NT) + dot(q, z), ar, s0, mak_mrk, v)
    inv = _unit_lower_inverses(tuple(mab))
    u = each(lambda i, q: dot(i, q[:c]), inv, xy)
    y = each(lambda q, z, p: q[c:] + dot(z, p), xy, mrb, u)
    e_tot = each(lambda l: jnp.exp(jnp.sum(l, axis=0, keepdims=True)), lw)
    s1 = each(lambda s, p, z, q, e: (s + dot(stack(p, z), q, _TN)) * e, s0, u, v, bk, e_tot)
    return y, s1


def _carry(plan, n_args, n_outs):
    n_in, n_out = len(plan.ins), len(plan.out_shape)

    def parts(refs):
        base = n_args + n_in + n_outs
        return refs[n_args:n_args + n_in], refs[base:base + n_out], refs[base + n_out + 1:]

    return parts, [HBM_SPEC] * n_in, list(plan.out_shape), [HBM_SPEC] * n_out, list(plan.sems)


def _head_cols(ref):
    return tuple(ref[:, h * HEAD:(h + 1) * HEAD] for h in range(HEADS))


def _wkv_fwd(seqs, plan):
    t, w = seqs[0].shape
    c, n = WKV_CHUNK, HEAD
    nc = t // c
    parts, plan_in_specs, plan_out_shape, plan_out_specs, plan_sems = _carry(plan, 6, 2)

    def body(*refs):
        ins, (y_ref, ck_ref) = refs[:6], refs[6 + len(plan.ins):8 + len(plan.ins)]
        s_ref = refs[8 + len(plan.ins) + len(plan.out_shape)]

        @pl.when(pl.program_id(0) == 0)
        def _():
            s_ref[...] = jnp.zeros_like(s_ref)
            plan.start(*parts(refs))

        s0 = tuple(s_ref[h] for h in range(HEADS))
        ys, s1 = _wkv_chunks(s0, *[_head_cols(r) for r in ins])
        for h in range(HEADS):
            ck_ref[0, h] = s0[h]
            y_ref[:, h * n:(h + 1) * n] = ys[h]
            s_ref[h] = s1[h]

        @pl.when(pl.program_id(0) == nc - 1)
        def _():
            plan.wait(*parts(refs))

    spec = pl.BlockSpec((c, w), lambda i: (i, 0))
    res = pl.pallas_call(
        body, name="wkv_fwd",
        out_shape=[jax.ShapeDtypeStruct((t, w), F32), jax.ShapeDtypeStruct((nc, HEADS, n, n), F32)] + plan_out_shape,
        grid=(nc,),
        in_specs=[spec] * 6 + plan_in_specs,
        out_specs=[spec, pl.BlockSpec((1, HEADS, n, n), lambda i: (i, 0, 0, 0))] + plan_out_specs,
        scratch_shapes=[pltpu.VMEM((HEADS, n, n), F32)] + plan_sems,
        compiler_params=_params(("arbitrary",)),
    )(*seqs, *plan.ins)
    return res[0], res[1], res[2:]


def _wkv_bwd(seqs, ck, dy, plan):
    t, w = seqs[0].shape
    c, n = WKV_CHUNK, HEAD
    nc = t // c
    parts, plan_in_specs, plan_out_shape, plan_out_specs, plan_sems = _carry(plan, 8, 6)

    def body(*refs):
        ins, ck_ref, dy_ref = refs[:6], refs[6], refs[7]
        outs = refs[8 + len(plan.ins):14 + len(plan.ins)]
        ds_ref = refs[14 + len(plan.ins) + len(plan.out_shape)]

        @pl.when(pl.program_id(0) == 0)
        def _():
            ds_ref[...] = jnp.zeros_like(ds_ref)
            plan.start(*parts(refs))

        s0 = tuple(ck_ref[0, h] for h in range(HEADS))
        _, vjp = jax.vjp(_wkv_chunks, s0, *[_head_cols(r) for r in ins])
        g = vjp((list(_head_cols(dy_ref)), [ds_ref[h] for h in range(HEADS)]))
        for h in range(HEADS):
            ds_ref[h] = g[0][h]
            for o, d in zip(outs, g[1:]):
                o[:, h * n:(h + 1) * n] = d[h]

        @pl.when(pl.program_id(0) == nc - 1)
        def _():
            plan.wait(*parts(refs))

    spec = pl.BlockSpec((c, w), lambda i: (nc - 1 - i, 0))
    res = pl.pallas_call(
        body, name="wkv_bwd",
        out_shape=[jax.ShapeDtypeStruct((t, w), F32)] * 6 + plan_out_shape,
        grid=(nc,),
        in_specs=[spec] * 6 + [pl.BlockSpec((1, HEADS, n, n), lambda i: (nc - 1 - i, 0, 0, 0)), spec] + plan_in_specs,
        out_specs=[spec] * 6 + plan_out_specs,
        scratch_shapes=[pltpu.VMEM((HEADS, n, n), F32)] + plan_sems,
        compiler_params=_params(("arbitrary",)),
    )(*seqs, ck, dy, *plan.ins)
    return res[:6], res[6:]


def _coords():
    return lax.axis_index("x"), lax.axis_index("y"), lax.axis_index("c")


def _flip(v, f):
    return 1 - v if f else v


_CHIP_FLIPS = [(1, 0), (0, 1), (1, 1)]
_DEV_FLIPS = [(fx, fy, fc) for fx in (0, 1) for fy in (0, 1) for fc in (0, 1) if (fx, fy, fc) != (0, 0, 0)]
HBM_SPEC = pl.BlockSpec(memory_space=pl.ANY)


def _chip_peer(k, x, y):
    fx, fy = _CHIP_FLIPS[k]
    return _flip(x, fx), _flip(y, fy)


def _dev_peer(k, x, y, c):
    fx, fy, fc = _DEV_FLIPS[k]
    return _flip(x, fx), _flip(y, fy), _flip(c, fc)


def _rows_of_core(ref, core):
    h = ref.shape[-2] // 2
    rows = pl.ds(pl.multiple_of(core * h, 8), h)
    return ref.at[rows, :] if len(ref.shape) == 2 else ref.at[:, rows, :]


class _Plan(NamedTuple):
    ins: Sequence[Any]
    out_shape: Sequence[Any]
    sems: Sequence[Any]
    start: Callable
    wait: Callable


_NO_PLAN = _Plan([], [], [], lambda *_: None, lambda *_: None)


def _run_plan(name, plan):
    n_in, n_out = len(plan.ins), len(plan.out_shape)

    def body(*refs):
        parts = refs[:n_in], refs[n_in:n_in + n_out], refs[n_in + n_out:]
        plan.start(*parts)
        plan.wait(*parts)

    return pl.pallas_call(
        body, name=name, out_shape=list(plan.out_shape),
        in_specs=[HBM_SPEC] * n_in, out_specs=[HBM_SPEC] * n_out, scratch_shapes=list(plan.sems),
    )(*plan.ins)


def _gather_plan(shards):
    n = len(shards)

    def copies(srcs, outs, sems):
        send_sems, recv_sems, local_sems = sems
        x, y, c = _coords()
        me = 2 * x + y

        def remote(i, k, arriving):
            px, py = _chip_peer(k, x, y)
            return pltpu.make_async_remote_copy(
                src_ref=srcs[i], dst_ref=outs[i].at[2 * px + py if arriving else me],
                send_sem=send_sems.at[i, k], recv_sem=recv_sems.at[i, k],
                device_id=(px, py, c), device_id_type=MESH)

        own = [pltpu.make_async_copy(srcs[i], outs[i].at[me], local_sems.at[i]) for i in range(n)]
        pairs = [(i, k) for k in range(3) for i in range(n)]
        return own, [remote(i, k, False) for i, k in pairs], [remote(i, k, True) for i, k in pairs]

    return _Plan(
        ins=shards, out_shape=[jax.ShapeDtypeStruct((4,) + s.shape, s.dtype) for s in shards],
        sems=[pltpu.SemaphoreType.DMA((n, 3)), pltpu.SemaphoreType.DMA((n, 3)), pltpu.SemaphoreType.DMA((n,))],
        start=functools.partial(_start_copies, copies), wait=functools.partial(_wait_copies, copies))


def _start_copies(copies, ins, outs, sems):
    own, sends, _ = copies(ins, outs, sems)
    for cp in own + sends:
        cp.start()


def _wait_copies(copies, ins, outs, sems):
    own, sends, arrivals = copies(ins, outs, sems)
    for cp in arrivals:
        cp.wait_recv()
    for cp in sends:
        cp.wait_send()
    for cp in own:
        cp.wait()


def _exchange_plan(gs, small=None):
    n = len(gs)
    arrays = list(gs) + ([] if small is None else [small])

    def copies(srcs, outs, sems):
        send_sems, recv_sems, local_sems = sems
        x, y, c = _coords()
        me = 4 * x + 2 * y + c

        def piece(i, px, py, pc):
            if i == n:
                return srcs[i].at[4 * px + 2 * py + pc]
            return _rows_of_core(srcs[i].at[2 * px + py], pc)

        def remote(i, k, arriving):
            px, py, pc = _dev_peer(k, x, y, c)
            return pltpu.make_async_remote_copy(
                src_ref=piece(i, px, py, pc), dst_ref=outs[i].at[4 * px + 2 * py + pc if arriving else me],
                send_sem=send_sems.at[i, k], recv_sem=recv_sems.at[i, k],
                device_id=(px, py, pc), device_id_type=MESH)

        own = [pltpu.make_async_copy(piece(i, x, y, c), outs[i].at[me], local_sems.at[i]) for i in range(len(arrays))]
        pairs = [(i, k) for k in range(7) for i in range(len(arrays))]
        return own, [remote(i, k, False) for i, k in pairs], [remote(i, k, True) for i, k in pairs]

    out_shape = [jax.ShapeDtypeStruct((8, g.shape[1] // 2, g.shape[2]), g.dtype) for g in gs]
    if small is not None:
        out_shape.append(jax.ShapeDtypeStruct(small.shape, small.dtype))
    m = len(arrays)
    return _Plan(
        ins=arrays, out_shape=out_shape,
        sems=[pltpu.SemaphoreType.DMA((m, 7)), pltpu.SemaphoreType.DMA((m, 7)), pltpu.SemaphoreType.DMA((m,))],
        start=functools.partial(_start_copies, copies), wait=functools.partial(_wait_copies, copies))


def _share_cores(halves, small):
    n = len(halves)

    def body(*refs):
        srcs, small_src, outs, small_out = refs[:n], refs[n], refs[n + 1:2 * n + 1], refs[2 * n + 1]
        mine, theirs = refs[2 * n + 2:3 * n + 2], refs[3 * n + 2:4 * n + 2]
        send_sems, recv_sems, ssend, srecv, local_sems = refs[4 * n + 2:]
        x, y, c = _coords()
        me = 4 * x + 2 * y + c

        def big(i):
            return pltpu.make_async_remote_copy(
                src_ref=mine[i], dst_ref=theirs[i], send_sem=send_sems.at[i], recv_sem=recv_sems.at[i],
                device_id=(x, y, 1 - c), device_id_type=MESH)

        def tiny(k, arriving):
            px, py, pc = _dev_peer(k, x, y, c)
            return pltpu.make_async_remote_copy(
                src_ref=small_src, dst_ref=small_out.at[4 * px + 2 * py + pc if arriving else me],
                send_sem=ssend.at[k], recv_sem=srecv.at[k], device_id=(px, py, pc), device_id_type=MESH)

        small_sends = [tiny(k, False) for k in range(7)]
        own_small = pltpu.make_async_copy(small_src, small_out.at[me], local_sems.at[2 * n])
        stage = [pltpu.make_async_copy(srcs[i], mine[i], local_sems.at[i]) for i in range(n)]
        for cp in small_sends + [own_small] + stage:
            cp.start()
        sends = []
        for i in range(n):
            stage[i].wait()
            sends.append(big(i))
            sends[-1].start()
        store = [pltpu.make_async_copy(mine[i], outs[i].at[c], local_sems.at[i]) for i in range(n)]
        for cp in store:
            cp.start()
        for i in range(n):
            big(i).wait_recv()
            store.append(pltpu.make_async_copy(theirs[i], outs[i].at[1 - c], local_sems.at[n + i]))
            store[-1].start()
        for k in range(7):
            tiny(k, True).wait_recv()
        for cp in sends + small_sends:
            cp.wait_send()
        for cp in store + [own_small]:
            cp.wait()

    staged = [pltpu.VMEM(s.shape, s.dtype) for s in halves]
    res = pl.pallas_call(
        body, name="share_cores",
        out_shape=[jax.ShapeDtypeStruct((2,) + s.shape, s.dtype) for s in halves]
        + [jax.ShapeDtypeStruct((8,) + small.shape, small.dtype)],
        in_specs=[HBM_SPEC] * (n + 1), out_specs=[HBM_SPEC] * (n + 1),
        scratch_shapes=staged + staged + [
            pltpu.SemaphoreType.DMA((n,)), pltpu.SemaphoreType.DMA((n,)),
            pltpu.SemaphoreType.DMA((7,)), pltpu.SemaphoreType.DMA((7,)),
            pltpu.SemaphoreType.DMA((2 * n + 1,))],
        compiler_params=pltpu.CompilerParams(vmem_limit_bytes=VMEM_LIMIT_BYTES),
    )(*halves, small)
    return res[:n], res[n]


BF16_SUBLANES = 16


def _row_tile(n, target, step=BF16_SUBLANES):
    return max([d for d in range(step, min(n, target) + 1, step) if n % d == 0] or [n])


def _ew(name, fn, ins, outs, block_bytes=2 << 20):
    rows, cols = ins[0].shape[-2:]
    lead = max(math.prod(a.shape[:-2]) for a in ins)
    tr = _row_tile(rows, max(8, block_bytes // (4 * cols * lead)))
    n = len(ins)

    def spec(shape):
        if len(shape) == 2:
            return pl.BlockSpec((tr, cols), lambda i: (i, 0))
        return pl.BlockSpec((shape[0], tr, cols), lambda i: (0, i, 0))

    def body(*refs):
        res = fn(*[r[...] for r in refs[:n]])
        for r, v in zip(refs[n:], res):
            r[...] = v

    return pl.pallas_call(
        body, name=name,
        out_shape=[jax.ShapeDtypeStruct(s, F32) for s in outs],
        grid=(rows // tr,),
        in_specs=[spec(a.shape) for a in ins],
        out_specs=[spec(s) for s in outs],
        compiler_params=_params(("parallel",)),
    )(*ins)


def _sum_slots(a):
    total = a[0].astype(F32)
    for s in range(1, a.shape[0]):
        total = total + a[s].astype(F32)
    return (total,)


def _adamw(g, w, m, v):
    bc1 = 1.0 - ADAM_B1 ** ADAM_STEP
    bc2 = 1.0 - ADAM_B2 ** ADAM_STEP
    m_new = ADAM_B1 * m + (1.0 - ADAM_B1) * g
    v_new = ADAM_B2 * v + (1.0 - ADAM_B2) * jnp.square(g)
    delta = -ADAM_LR * ((m_new / bc1) / (jnp.sqrt(v_new / bc2) + ADAM_EPS) + ADAM_WD * w)
    return delta, m_new, v_new


def _mat(a):
    return a.reshape(a.shape[-2:])


def _to_shard_major(full, axis):
    rows, cols = full.shape
    if axis == 0:
        return full.reshape(4, rows // 4, cols)
    return full.reshape(rows, 4, cols // 4).transpose(1, 0, 2)


def _from_shard_major(a, axis):
    _, r, cs = a.shape
    if axis == 0:
        return a.reshape(4 * r, cs)
    return a.transpose(1, 0, 2).reshape(r, 4 * cs)


def _pack_small(arrays, tail=None):
    flat = [arrays[n].reshape(-1) for n in SMALL_NAMES] + ([] if tail is None else [tail.reshape(1)])
    used = sum(a.shape[0] for a in flat)
    flat.append(jnp.zeros((8 * SMALL_ROWS * PACK_COLS - used,), F32))
    return jnp.concatenate(flat).reshape(8, SMALL_ROWS, PACK_COLS)


def _unpack_small(packed, shapes):
    flat = packed.reshape(-1)
    out, off = {}, 0
    for n in SMALL_NAMES:
        size = math.prod(shapes[n])
        out[n] = flat[off:off + size].reshape(shapes[n])
        off += size
    return out


def _row(a):
    return a.reshape(1, -1)


def _local_step(x, p, target, wf, ws, late_shards):
    wf = dict(wf)
    t = x.shape[0]
    tm = min(256, t)
    tw = min(512, t)
    tx = min(1024, t)
    g = {}

    lam_re, lam_im = ws['s5_lam_re'].reshape(S5_GROUPS, S5_STATE), ws['s5_lam_im'].reshape(S5_GROUPS, S5_STATE)
    log_step = ws['s5_log_step'].reshape(S5_GROUPS, 1)
    gp = (S5_GROUPS, S5_STATE)
    lam_ins = (lam_re, lam_im, log_step)
    lbr, lbi, cfr, cfi = _small_fwd("s5_lam", _f_s5_lam, lam_ins, [(gp, F32)] * 4)
    lam_row = jnp.concatenate([_row(lbr), _row(lbi)], axis=1)
    to_t = lambda a, perm: a.reshape((S5_GROUPS,) + a.shape[-2:]).transpose(perm).reshape(S5_GROUP, S5_LANES)
    build_ins = (_row(cfr), _row(cfi), to_t(ws['s5_b_re'], (2, 0, 1)), to_t(ws['s5_b_im'], (2, 0, 1)),
                 to_t(ws['s5_c_re'], (1, 0, 2)), to_t(ws['s5_c_im'], (1, 0, 2)))
    block_shape = (S5_WIDTH, 2 * S5_TILE_LANES)
    b_blk, c_blk = _small_fwd("s5_build", _f_s5_build, build_ins, [(block_shape, F32)] * 2)

    norm_mix, norm_ffn, norm_ple = _row(ws['norm_mix']), _row(ws['norm_ffn']), _row(ws['norm_ple'])
    final_norm = _row(ws['final_norm'])
    (xn,) = _tok_fwd("norm_in", _f_norm_in, [x], [norm_mix], [(x.shape[1], BF16)], tx)
    u = _mm("proj_s5", xn, wf['w_in'][:, :S5_WIDTH], 'nn')
    z = _mm("proj_rw", xn, wf['w_in'][:, S5_WIDTH:], 'nn')

    bu = _s5_expand("s5_bu", u, b_blk)
    def late_plan(carrier):
        return _gather_plan([late_shards[n] for n in LATE_GATHER[carrier]])

    def arrived(carrier, got):
        wf.update({n: _from_shard_major(a, SHARDED[n]) for n, a in zip(LATE_GATHER[carrier], got)})

    xs, got = _s5_scan(bu, lam_row, tm, late_plan('s5_scan'))
    arrived('s5_scan', got)
    ypre = _s5_contract("s5_y", xs, c_blk)
    s5_par = [_row(ws['s5_d']), wf['s5_glu_w'], _row(ws['s5_glu_b'])]
    (s5_out,) = _tok_fwd("s5_post", _f_s5_post, [ypre, u], s5_par, [(S5_WIDTH, BF16)], tx)

    pre_par = [_row(ws['rw_shift_mu']), _row(ws['rw_w0']), wf['rw_w2'], _row(ws['rw_a0']), wf['rw_a2'],
               wf['rw_g2'], _row(ws['rw_k_k']), _row(ws['rw_k_a'])]
    r, lw, kp, v, an, bn, gate = _rw_pre_fwd(z, pre_par, tw)
    seqs = [r, lw, kp, v, an, bn]
    y_wkv, ck, got = _wkv_fwd(seqs, late_plan('wkv_fwd'))
    arrived('wkv_fwd', got)
    post_par = [_row(ws['rw_ln_w']), _row(ws['rw_ln_b']), _row(ws['rw_r_k'])]
    post_toks = [y_wkv, r, kp, v, gate]
    (rw_out,) = _tok_fwd("rw_post", _f_rw_post, post_toks, post_par, [(RWKV_WIDTH, BF16)], tx)

    mixcat = jnp.concatenate([s5_out, rw_out], axis=1)
    mixed = _mm("mix_out", mixcat, wf['w_out'], 'nn')
    h1, hn = _tok_fwd("mix_res", _f_mix_res, [x, mixed], [norm_ffn], [(x.shape[1], F32), (x.shape[1], BF16)], tx)
    w13 = jnp.concatenate([wf['ffn_w1'], wf['ffn_w3']], axis=1)
    a13, got = _mm("ffn_up", hn, w13, 'nn', out_dtype=BF16, tn=FFN_TILE, plan=late_plan('ffn_up'))
    arrived('ffn_up', got)
    (f,) = _tok_fwd("ffn_act", _f_ffn_act, [a13], [], [(FFN_HIDDEN, BF16)], tw)
    ffo = _mm("ffn_down", f, wf['ffn_w2'], 'nn')
    h2, hp = _tok_fwd("ffn_res", _f_ffn_res, [h1, ffo], [norm_ple], [(x.shape[1], F32), (x.shape[1], BF16)], tx)
    gpre = _mm("ple_gate", hp, wf['ple_gate_w'], 'nn')
    pu = _mm("ple_up", p, wf['ple_up_w'], 'nn')

    dh2, dgpre, dpu, g['final_norm'], loss = _tok_bwd(
        "loss", _f_loss, [h2, gpre, pu, target], [final_norm], [None],
        [F32, BF16, BF16, None], [True], tw, acc_out=0)
    g['ple_gate_w'] = _mm("d_ple_gate_w", hp, dgpre, 'tn', out_dtype=WIRE)
    g['ple_up_w'] = _mm("d_ple_up_w", p, dpu, 'tn', out_dtype=WIRE)
    dhp = _mm("d_hp", dgpre, wf['ple_gate_w'], 'nt')
    dh1, dffo, g['norm_ple'] = _tok_bwd("ffn_res_bwd", _f_ffn_res, [h1, ffo], [norm_ple], [dh2, dhp],
                                        [F32, BF16], [True], tw)
    g['ffn_w2'] = _mm("d_ffn_w2", f, dffo, 'tn', out_dtype=WIRE, tm=FFN_TILE)
    df = _mm("d_f", dffo, wf['ffn_w2'], 'nt', out_dtype=BF16, tn=FFN_TILE)
    (da13,) = _tok_bwd("ffn_act_bwd", _f_ffn_act, [a13], [], [df], [BF16], [], tw)
    dw13 = _mm("d_ffn_w13", hn, da13, 'tn', out_dtype=WIRE, tn=FFN_TILE)
    dw13 = dw13.reshape(dw13.shape[0], 8, FFN_HIDDEN // 4).transpose(1, 0, 2)
    shard_major = {'ffn_w1': dw13[:4], 'ffn_w3': dw13[4:]}
    dhn = _mm("d_hn", da13, w13, 'nt')
    dx_a, dmixed, g['norm_ffn'] = _tok_bwd("mix_res_bwd", _f_mix_res, [x, mixed], [norm_ffn], [dh1, dhn],
                                           [F32, BF16], [True], tw)
    g['w_out'] = _mm("d_w_out", mixcat, dmixed, 'tn', out_dtype=WIRE)
    ds5_out = _mm("d_s5_out", dmixed, wf['w_out'][:S5_WIDTH], 'nt')
    drw_out = _mm("d_rw_out", dmixed, wf['w_out'][S5_WIDTH:], 'nt')

    dy_wkv, dr_b, dkp_b, dv_b, dgate, g['rw_ln_w'], g['rw_ln_b'], g['rw_r_k'] = _tok_bwd(
        "rw_post_bwd", _f_rw_post, post_toks, post_par, [drw_out], [F32] * 5, [True] * 3, tw)
    late_exchange = _exchange_plan([shard_major[n] if n in shard_major else
                                    _to_shard_major(g[n], SHARDED[n]).astype(WIRE) for n in LATE_NAMES])
    dseqs, late_pieces = _wkv_bwd(seqs, ck, dy_wkv, late_exchange)
    pre_cots = [(dseqs[0], dr_b), (dseqs[1],), (dseqs[2], dkp_b), (dseqs[3], dv_b), (dseqs[4],), (dseqs[5],),
                (dgate,)]
    dz, *dpre = _rw_pre_bwd(z, pre_par, pre_cots, tw)
    for n, d in zip(['rw_shift_mu', 'rw_w0', 'rw_w2', 'rw_a0', 'rw_a2', 'rw_g2', 'rw_k_k', 'rw_k_a'], dpre):
        g[n] = d

    dypre, du_a, g['s5_d'], g['s5_glu_w'], g['s5_glu_b'] = _tok_bwd(
        "s5_post_bwd", _f_s5_post, [ypre, u], s5_par, [ds5_out], [F32, F32], [True] * 3, tw)
    dxs = _s5_expand("d_s5_x", dypre, c_blk)
    dc_blk = _s5_block_grad("d_s5_c", dypre, xs)
    dbu, dlam_row = _s5_scan_bwd(dxs, xs, lam_row, tm)
    du = _s5_contract("d_s5_u", dbu, b_blk, add=du_a, out_dtype=BF16)
    db_blk = _s5_block_grad("d_s5_b", u, dbu)
    dbuild = _small_bwd("s5_build_bwd", _f_s5_build, build_ins, (db_blk, dc_blk))
    lam_cots = (dlam_row[:, :S5_LANES].reshape(gp), dlam_row[:, S5_LANES:].reshape(gp),
                dbuild[0].reshape(gp), dbuild[1].reshape(gp))
    g['s5_lam_re'], g['s5_lam_im'], g['s5_log_step'] = _small_bwd("s5_lam_bwd", _f_s5_lam, lam_ins, lam_cots)
    from_t = lambda a, perm: a.reshape(S5_GROUP, S5_GROUPS, S5_STATE).transpose(perm)
    g['s5_b_re'], g['s5_b_im'] = from_t(dbuild[2], (1, 2, 0)), from_t(dbuild[3], (1, 2, 0))
    g['s5_c_re'], g['s5_c_im'] = from_t(dbuild[4], (1, 0, 2)), from_t(dbuild[5], (1, 0, 2))

    dproj = jnp.concatenate([du, dz], axis=1)
    dw_in = _mm("d_w_in", xn, dproj, 'tn', out_dtype=WIRE)
    w_in_exchange = _exchange_plan([_to_shard_major(dw_in, SHARDED['w_in'])])
    dxn, (w_in_pieces,) = _mm("d_xn", dproj, wf['w_in'], 'nt', plan=w_in_exchange)
    pieces = dict(zip(LATE_NAMES, late_pieces), w_in=w_in_pieces)
    grad_x, g['norm_mix'] = _tok_bwd("norm_in_bwd", _f_norm_in, [x], [norm_mix], [dxn], [F32], [True], tw,
                                     add_to=(0, dx_a))
    return loss[0, 0], grad_x, g, pieces


def _step(x, p, target, w, m, v):
    shards = {n: _mat(w[n]).astype(BF16) for n in SHARDED_NAMES}
    early = _run_plan("gather_early", _gather_plan([shards[n] for n in EARLY_NAMES]))
    wf = {n: _from_shard_major(a, SHARDED[n]) for n, a in zip(EARLY_NAMES, early)}
    ws = {n: w[n] for n in SMALL_NAMES}

    loss, grad_x, g, pieces = _local_step(x[0], p[0, 0], target[0], wf, ws, shards)

    last_names = [n for n in SHARDED_NAMES if n not in pieces]
    last_plan = _exchange_plan([_to_shard_major(g[n], SHARDED[n]).astype(WIRE) for n in last_names],
                               _pack_small({n: g[n] for n in SMALL_NAMES}, tail=loss))
    *last_pieces, by_dev = _run_plan("exchange_last", last_plan)
    pieces.update(zip(last_names, last_pieces))
    halves = [_ew("add_devices_" + n, _sum_slots, [pieces[n]], [pieces[n].shape[1:]])[0] for n in SHARDED_NAMES]
    (small_piece,) = _ew("add_devices_small", _sum_slots, [by_dev], [by_dev.shape[1:]])
    both, small_g = _share_cores(halves, small_piece)

    kinds = [{}, {}, {}, {}]
    for n, gn in zip(SHARDED_NAMES, both):
        shard = _mat(w[n]).shape
        res = _ew("adamw_" + n, _adamw, [gn.reshape(shard), _mat(w[n]), _mat(m[n]), _mat(v[n])], [shard] * 3)
        for kind, a in zip(kinds, [gn] + list(res)):
            kind[n] = a.reshape(w[n].shape)
    flat = (8 * SMALL_ROWS, PACK_COLS)
    packed = [_pack_small({n: d[n] for n in SMALL_NAMES}).reshape(flat) for d in (w, m, v)]
    small_res = _ew("adamw_small", _adamw, [small_g.reshape(flat)] + packed, [flat] * 3)
    small_shapes = {n: w[n].shape for n in SMALL_NAMES}
    for kind, a in zip(kinds, [small_g] + list(small_res)):
        kind.update(_unpack_small(a, small_shapes))
    total = small_g.reshape(-1)[sum(math.prod(s) for s in small_shapes.values())]
    return (total, grad_x[None], *[kind[n] for kind in kinds for n in WEIGHT_NAMES])


def kernel(x, p, norm_mix, w_in, s5_lam_re, s5_lam_im, s5_log_step, s5_b_re, s5_b_im, s5_c_re, s5_c_im, s5_d, s5_glu_w, s5_glu_b, rw_shift_mu, rw_w0, rw_w2, rw_a0, rw_a2, rw_g2, rw_k_k, rw_k_a, rw_r_k, rw_ln_w, rw_ln_b, w_out, norm_ffn, ffn_w1, ffn_w3, ffn_w2, norm_ple, ple_gate_w, ple_up_w, final_norm, loss_target, m_norm_mix, m_w_in, m_s5_lam_re, m_s5_lam_im, m_s5_log_step, m_s5_b_re, m_s5_b_im, m_s5_c_re, m_s5_c_im, m_s5_d, m_s5_glu_w, m_s5_glu_b, m_rw_shift_mu, m_rw_w0, m_rw_w2, m_rw_a0, m_rw_a2, m_rw_g2, m_rw_k_k, m_rw_k_a, m_rw_r_k, m_rw_ln_w, m_rw_ln_b, m_w_out, m_norm_ffn, m_ffn_w1, m_ffn_w3, m_ffn_w2, m_norm_ple, m_ple_gate_w, m_ple_up_w, m_final_norm, v_norm_mix, v_w_in, v_s5_lam_re, v_s5_lam_im, v_s5_log_step, v_s5_b_re, v_s5_b_im, v_s5_c_re, v_s5_c_im, v_s5_d, v_s5_glu_w, v_s5_glu_b, v_rw_shift_mu, v_rw_w0, v_rw_w2, v_rw_a0, v_rw_a2, v_rw_g2, v_rw_k_k, v_rw_k_a, v_rw_r_k, v_rw_ln_w, v_rw_ln_b, v_w_out, v_norm_ffn, v_ffn_w1, v_ffn_w3, v_ffn_w2, v_norm_ple, v_ple_gate_w, v_ple_up_w, v_final_norm):
    args = dict(locals())
    w = {n: args[n] for n in WEIGHT_NAMES}
    m = {n: args["m_" + n] for n in WEIGHT_NAMES}
    v = {n: args["v_" + n] for n in WEIGHT_NAMES}
    return _step(x, p, loss_target, w, m, v)
```

```python
import functools
import math
from typing import Any, Callable, NamedTuple, Sequence

import jax
import jax.numpy as jnp
from jax import lax
from jax.experimental import pallas as pl
from jax.experimental.pallas import tpu as pltpu

F32 = jnp.float32
BF16 = jnp.bfloat16
MESH = pl.DeviceIdType.MESH

S5_WIDTH = 512
RWKV_WIDTH = 512
S5_GROUP = 16
S5_GROUPS = 32
S5_STATE = 64
S5_LANES = S5_GROUPS * S5_STATE
S5_TILE_GROUPS = 8
S5_TILES = S5_GROUPS // S5_TILE_GROUPS
S5_TILE_CH = S5_TILE_GROUPS * S5_GROUP
S5_TILE_LANES = S5_TILE_GROUPS * S5_STATE
HEAD = 64
HEADS = 8
DECAY_LORA = 64
AAA_LORA = 64
GATE_LORA = 128
FFN_HIDDEN = 2816
FFN_TILE = FFN_HIDDEN // 2
RMS_EPS = 1e-6
GN_EPS = 64e-5
L2_EPS = 1e-12
ADAM_LR = 0.001
ADAM_B1 = 0.9
ADAM_B2 = 0.999
ADAM_EPS = 1e-08
ADAM_WD = 0.01
ADAM_STEP = 10

WKV_CHUNK = 64
SCAN_UNROLL = 4
WIRE = jnp.bfloat16
WKV_PASSES = 1
VMEM_LIMIT_BYTES = 48 * 1024 * 1024
LANE = 128
PACK_COLS = 1024
SMALL_ROWS = 24

WEIGHT_NAMES = ['norm_mix', 'w_in', 's5_lam_re', 's5_lam_im', 's5_log_step', 's5_b_re', 's5_b_im', 's5_c_re',
                's5_c_im', 's5_d', 's5_glu_w', 's5_glu_b', 'rw_shift_mu', 'rw_w0', 'rw_w2', 'rw_a0', 'rw_a2',
                'rw_g2', 'rw_k_k', 'rw_k_a', 'rw_r_k', 'rw_ln_w', 'rw_ln_b', 'w_out', 'norm_ffn', 'ffn_w1',
                'ffn_w3', 'ffn_w2', 'norm_ple', 'ple_gate_w', 'ple_up_w', 'final_norm']
SHARDED = {'w_in': 1, 's5_glu_w': 0, 'rw_w2': 1, 'rw_a2': 1, 'rw_g2': 1, 'w_out': 0, 'ffn_w1': 1, 'ffn_w3': 1,
           'ffn_w2': 0, 'ple_gate_w': 0, 'ple_up_w': 1}
SHARDED_NAMES = [n for n in WEIGHT_NAMES if n in SHARDED]
LATE_NAMES = ['w_out', 'ffn_w1', 'ffn_w3', 'ffn_w2', 'ple_gate_w', 'ple_up_w']
EARLY_NAMES = [n for n in SHARDED_NAMES if n not in LATE_NAMES]
LATE_GATHER = {'s5_scan': ['w_out', 'ffn_w3'], 'wkv_fwd': ['ffn_w1'], 'ffn_up': ['ffn_w2'],
               'ffn_down': ['ple_gate_w', 'ple_up_w']}
SMALL_NAMES = [n for n in WEIGHT_NAMES if n not in SHARDED]


def _params(sem=None):
    return pltpu.CompilerParams(dimension_semantics=sem, vmem_limit_bytes=VMEM_LIMIT_BYTES)


def _tile(n, target):
    best = None
    for d in range(LANE, min(n, target) + 1, LANE):
        if n % d == 0:
            best = d
    return n if best is None else best


_NN = (((1,), (0,)), ((), ()))
_NT = (((1,), (1,)), ((), ()))
_TN = (((0,), (0,)), ((), ()))


def _split(a):
    a = a.astype(F32)
    hi = a.astype(BF16)
    return hi, (a - hi.astype(F32)).astype(BF16)


def _dg(a, b, dims, passes):
    dg = lambda p, q: lax.dot_general(p, q, dims, preferred_element_type=F32)
    if passes == 1:
        return dg(a.astype(BF16), b.astype(BF16))
    bh, bl = _split(b)
    return dg(a.astype(BF16), bh) + dg(a.astype(BF16), bl)


_DOT_BWD = {_NN: (("g", "b", _NT), ("a", "g", _TN)),
            _NT: (("g", "b", _NN), ("g", "a", _TN)),
            _TN: (("b", "g", _NT), ("a", "g", _NN))}


@functools.partial(jax.custom_vjp, nondiff_argnums=(2, 3))
def _dot(a, b, dims, passes):
    return _dg(a, b, dims, passes)


def _dot_fwd(a, b, dims, passes):
    return _dg(a, b, dims, passes), (a, b)


def _dot_bwd(dims, passes, res, g):
    env = {"a": res[0], "b": res[1], "g": g}
    return tuple(_dg(env[p], env[q], d, passes) for p, q, d in _DOT_BWD[dims])


_dot.defvjp(_dot_fwd, _dot_bwd)


def _bdot(x, w):
    return _dot(x, w, _NN, 1)


@jax.custom_vjp
def _shift_down(z):
    return pltpu.roll(z, 1, 0)


def _shift_down_fwd(z):
    return pltpu.roll(z, 1, 0), None


def _shift_down_bwd(_, g):
    return (pltpu.roll(g, g.shape[0] - 1, 0),)


_shift_down.defvjp(_shift_down_fwd, _shift_down_bwd)


def _head_sum_impl(x):
    r = lax.broadcasted_iota(jnp.int32, (LANE, LANE), 0) // HEAD
    c = lax.broadcasted_iota(jnp.int32, (LANE, LANE), 1) // HEAD
    ones = (r == c).astype(BF16)
    hi, lo = _split(x)
    dg = lambda p: lax.dot_general(p, ones, _NN, preferred_element_type=F32)
    tiles = [slice(j, j + LANE) for j in range(0, x.shape[1], LANE)]
    return jnp.concatenate([dg(hi[:, s]) + dg(lo[:, s]) for s in tiles], axis=1)


@jax.custom_vjp
def _head_sum(x):
    return _head_sum_impl(x)


_head_sum.defvjp(lambda x: (_head_sum_impl(x), None), lambda _, g: (_head_sum_impl(g),))


def _mm(name, a, b, mode, out_dtype=F32, tm=1024, tn=1024, tk=1536, plan=None):
    if mode == 'nn':
        (m, k), (_, n) = a.shape, b.shape
    elif mode == 'nt':
        (m, k), (n, _) = a.shape, b.shape
    else:
        (k, m), (_, n) = a.shape, b.shape
    tm, tn, tk = _tile(m, tm), _tile(n, tn), _tile(k, tk)
    nm, nn, nk = m // tm, n // tn, k // tk
    dims = {'nn': _NN, 'nt': _NT, 'tn': _TN}[mode]
    plan = _NO_PLAN if plan is None else plan
    parts, plan_in_specs, plan_out_shape, plan_out_specs, plan_sems = _carry(plan, 2, 1)

    def body(*refs):
        a_ref, b_ref, o_ref = refs[0], refs[1], refs[2 + len(plan.ins)]
        acc_ref = refs[3 + len(plan.ins) + len(plan.out_shape)]
        i, j, kk = pl.program_id(0), pl.program_id(1), pl.program_id(2)

        if plan is not _NO_PLAN:
            pl.when((i == 0) & (j == 0) & (kk == 0))(lambda: plan.start(*parts(refs)))

        @pl.when(kk == 0)
        def _():
            acc_ref[...] = jnp.zeros_like(acc_ref)

        acc_ref[...] += _dg(a_ref[...], b_ref[...], dims, 1)

        @pl.when(kk == nk - 1)
        def _():
            o_ref[...] = acc_ref[...].astype(o_ref.dtype)

        if plan is not _NO_PLAN:
            pl.when((i == nm - 1) & (j == nn - 1) & (kk == nk - 1))(lambda: plan.wait(*parts(refs)))

    if mode == 'tn':
        a_spec = pl.BlockSpec((tk, tm), lambda i, j, l: (l, i))
    else:
        a_spec = pl.BlockSpec((tm, tk), lambda i, j, l: (i, l))
    if mode == 'nt':
        b_spec = pl.BlockSpec((tn, tk), lambda i, j, l: (j, l))
    else:
        b_spec = pl.BlockSpec((tk, tn), lambda i, j, l: (l, j))
    res = pl.pallas_call(
        body, name=name,
        out_shape=[jax.ShapeDtypeStruct((m, n), out_dtype)] + plan_out_shape,
        grid=(nm, nn, nk),
        in_specs=[a_spec, b_spec] + plan_in_specs,
        out_specs=[pl.BlockSpec((tm, tn), lambda i, j, l: (i, j))] + plan_out_specs,
        scratch_shapes=[pltpu.VMEM((tm, tn), F32)] + plan_sems,
        compiler_params=_params(("parallel", "parallel", "arbitrary") if plan is _NO_PLAN else ("arbitrary",) * 3),
    )(a, b, *plan.ins)
    return res[0] if plan is _NO_PLAN else (res[0], res[1:])


def _mm_tiles(name, a, b, mode, out_shape, grid, a_spec, b_spec, o_spec, add=None, out_dtype=F32):
    dims = {'nn': _NN, 'nt': _NT, 'tn': _TN}[mode]
    nk = grid[2]
    extra = [] if add is None else [add]

    def body(a_ref, b_ref, *refs):
        o_ref, acc_ref = refs[len(extra):]
        kk = pl.program_id(2)

        @pl.when(kk == 0)
        def _():
            acc_ref[...] = refs[0][...].astype(F32) if extra else jnp.zeros_like(acc_ref)

        acc_ref[...] += _dg(a_ref[...], b_ref[...], dims, 1)

        @pl.when(kk == nk - 1)
        def _():
            o_ref[...] = acc_ref[...].astype(o_ref.dtype)

    return pl.pallas_call(
        body, name=name,
        out_shape=jax.ShapeDtypeStruct(out_shape, out_dtype),
        grid=grid, in_specs=[a_spec, b_spec] + [o_spec] * len(extra), out_specs=o_spec,
        scratch_shapes=[pltpu.VMEM(o_spec.block_shape, F32)],
        compiler_params=_params(("parallel", "parallel", "arbitrary")),
    )(a, b, *extra)


def _s5_expand(name, u, blk, tm=2048):
    t = u.shape[0]
    tm = min(tm, t)
    ch, ln, nt = S5_TILE_CH, S5_TILE_LANES, S5_TILES
    return _mm_tiles(name, u, blk, 'nn', (t, 2 * S5_LANES), (t // tm, 2 * nt, 1),
                     pl.BlockSpec((tm, ch), lambda i, j, l: (i, j % nt)),
                     pl.BlockSpec((ch, ln), lambda i, j, l: (j % nt, j // nt)),
                     pl.BlockSpec((tm, ln), lambda i, j, l: (i, j)), out_dtype=BF16)


def _s5_contract(name, x, blk, tm=2048, add=None, out_dtype=F32):
    t = x.shape[0]
    tm = min(tm, t)
    ch, ln, nt = S5_TILE_CH, S5_TILE_LANES, S5_TILES
    return _mm_tiles(name, x, blk, 'nt', (t, S5_WIDTH), (t // tm, nt, 2),
                     pl.BlockSpec((tm, ln), lambda i, j, l: (i, j + nt * l)),
                     pl.BlockSpec((ch, ln), lambda i, j, l: (j, l)),
                     pl.BlockSpec((tm, ch), lambda i, j, l: (i, j)), add=add, out_dtype=out_dtype)


def _s5_block_grad(name, u, x, tk=2048):
    t = u.shape[0]
    tk = min(tk, t)
    ch, ln, nt = S5_TILE_CH, S5_TILE_LANES, S5_TILES
    return _mm_tiles(name, u, x, 'tn', (S5_WIDTH, 2 * ln), (nt, 2, t // tk),
                     pl.BlockSpec((tk, ch), lambda i, j, l: (l, i)),
                     pl.BlockSpec((tk, ln), lambda i, j, l: (l, i + nt * j)),
                     pl.BlockSpec((ch, ln), lambda i, j, l: (i, j)))


def _full_spec(p):
    nd = p.ndim
    return pl.BlockSpec(p.shape, lambda i, nd=nd: (0,) * nd)


def _tok_fwd(name, fn, toks, params, outs, tm):
    t = toks[0].shape[0]
    nt, npar = len(toks), len(params)

    def body(*refs):
        tv = [r[...].astype(F32) for r in refs[:nt]]
        pv = [r[...].astype(F32) for r in refs[nt:nt + npar]]
        res = fn(*tv, *pv)
        for r, v in zip(refs[nt + npar:], res):
            r[...] = v.astype(r.dtype)

    return pl.pallas_call(
        body, name=name,
        out_shape=[jax.ShapeDtypeStruct((t, w), d) for w, d in outs],
        grid=(t // tm,),
        in_specs=[pl.BlockSpec((tm, a.shape[1]), lambda i: (i, 0)) for a in toks] + [_full_spec(p) for p in params],
        out_specs=[pl.BlockSpec((tm, w), lambda i: (i, 0)) for w, _ in outs],
        compiler_params=_params(("parallel",)),
    )(*toks, *params)


def _tok_bwd(name, fn, toks, params, cots, dtok, dpar, tm, acc_out=None, add_to=None):
    t = toks[0].shape[0]
    nt, npar = len(toks), len(params)
    cot_arrays = [c for c in cots if c is not None]
    ncot = len(cot_arrays)
    extra = [] if add_to is None else [add_to[1]]
    dtok_idx = [i for i, d in enumerate(dtok) if d is not None]
    dpar_idx = [i for i, d in enumerate(dpar) if d]

    def body(*refs):
        pos = 0
        tin = refs[pos:pos + nt]; pos += nt
        pin = refs[pos:pos + npar]; pos += npar
        cin = refs[pos:pos + ncot]; pos += ncot
        ein = refs[pos:pos + len(extra)]; pos += len(extra)
        dto = refs[pos:pos + len(dtok_idx)]; pos += len(dtok_idx)
        dpo = refs[pos:pos + len(dpar_idx)]; pos += len(dpar_idx)
        acc = refs[pos] if acc_out is not None else None
        first = pl.program_id(0) == 0

        tv = [r[...].astype(F32) for r in tin]
        pv = [r[...].astype(F32) for r in pin]
        res, vjp = jax.vjp(fn, *tv, *pv)
        cit = iter(cin)
        cs = tuple(jnp.ones_like(o) if c is None else next(cit)[...].astype(F32) for c, o in zip(cots, res))
        g = vjp(cs)
        for r, i in zip(dto, dtok_idx):
            v = g[i]
            if add_to is not None and add_to[0] == i:
                v = v + ein[0][...].astype(F32)
            r[...] = v.astype(r.dtype)

        @pl.when(first)
        def _():
            for r in dpo:
                r[...] = jnp.zeros_like(r)
            if acc is not None:
                acc[...] = jnp.zeros_like(acc)

        for r, i in zip(dpo, dpar_idx):
            r[...] += g[nt + i]
        if acc is not None:
            acc[...] += res[acc_out]

    out_shape = [jax.ShapeDtypeStruct(toks[i].shape, dtok[i]) for i in dtok_idx]
    out_shape += [jax.ShapeDtypeStruct(params[i].shape, F32) for i in dpar_idx]
    out_specs = [pl.BlockSpec((tm, toks[i].shape[1]), lambda i_: (i_, 0)) for i in dtok_idx]
    out_specs += [_full_spec(params[i]) for i in dpar_idx]
    if acc_out is not None:
        out_shape.append(jax.ShapeDtypeStruct((1, 1), F32))
        out_specs.append(pl.BlockSpec((1, 1), lambda i_: (0, 0)))
    tok_spec = lambda a: pl.BlockSpec((tm, a.shape[1]), lambda i_: (i_, 0))
    return pl.pallas_call(
        body, name=name,
        out_shape=out_shape,
        grid=(t // tm,),
        in_specs=[tok_spec(a) for a in toks] + [_full_spec(p) for p in params]
        + [tok_spec(c) for c in cot_arrays] + [tok_spec(e) for e in extra],
        out_specs=out_specs,
        compiler_params=_params(("arbitrary",)),
    )(*toks, *params, *cot_arrays, *extra)


def _small_fwd(name, fn, ins, outs):
    n = len(ins)

    def body(*refs):
        res = fn(*[r[...] for r in refs[:n]])
        for r, v in zip(refs[n:], res):
            r[...] = v.astype(r.dtype)

    return pl.pallas_call(
        body, name=name,
        out_shape=[jax.ShapeDtypeStruct(s, d) for s, d in outs],
        compiler_params=_params(),
    )(*ins)


def _small_bwd(name, fn, ins, cots):
    n = len(ins)

    def body(*refs):
        _, vjp = jax.vjp(fn, *[r[...] for r in refs[:n]])
        g = vjp(tuple(r[...] for r in refs[n:n + len(cots)]))
        for r, v in zip(refs[n + len(cots):], g):
            r[...] = v

    return pl.pallas_call(
        body, name=name,
        out_shape=[jax.ShapeDtypeStruct(a.shape, F32) for a in ins],
        compiler_params=_params(),
    )(*ins, *cots)


def _rms(x, g):
    return x * lax.rsqrt(jnp.mean(x * x, axis=-1, keepdims=True) + RMS_EPS) * g


def _f_norm_in(x, g):
    return (_rms(x, g),)


def _f_mix_res(x, mixed, g):
    h1 = x + mixed
    return h1, _rms(h1, g)


def _f_ffn_act(a13):
    a1, a3 = a13[:, :FFN_HIDDEN], a13[:, FFN_HIDDEN:]
    return (jax.nn.silu(a1) * a3,)


def _f_ffn_res(h1, ffo, g):
    h2 = h1 + ffo
    return h2, _rms(h2, g)


def _f_loss(h2, gpre, pu, target, g):
    h3 = h2 + jax.nn.sigmoid(gpre) * pu
    y = _rms(h3, g)
    err = jnp.square(y - target)
    return (0.5 * jnp.sum(jnp.mean(err, axis=-1, keepdims=True), axis=0, keepdims=True),)


def _f_s5_post(ypre, u, d, glu_w, glu_b):
    z = jax.nn.gelu(ypre + u * d)
    return (z * jax.nn.sigmoid(_bdot(z, glu_w) + glu_b),)


def _softplus(x):
    return jnp.maximum(x, 0.0) + jnp.log(1.0 + jnp.exp(-jnp.abs(x)))


def _f_rw_pre(z, carry, shift_mu, w0, w2, a0, a2, g2, k_k, k_a):
    rw = RWKV_WIDTH
    first_row = lax.broadcasted_iota(jnp.int32, z.shape, 0) == 0
    prev = jnp.where(first_row, carry, _shift_down(z))
    zs = z + (prev - z) * shift_mu
    o1, o2 = 3 * rw + DECAY_LORA, 3 * rw + DECAY_LORA + AAA_LORA
    r, k, v = zs[:, :rw], zs[:, rw:2 * rw], zs[:, 2 * rw:3 * rw]
    wl, al, gl = zs[:, 3 * rw:o1], zs[:, o1:o2], zs[:, o2:]
    w = -_softplus(-(w0 + _bdot(jnp.tanh(wl), w2))) - 0.5
    log_decay = -jnp.exp(w)
    a = jax.nn.sigmoid(a0 + _bdot(al, a2))
    g = _bdot(jax.nn.sigmoid(gl), g2)
    kk = k * k_k
    norm = jnp.sqrt(_head_sum(kk * kk))
    kk = kk / jnp.maximum(norm, L2_EPS)
    kp = k * (1.0 + (a - 1.0) * k_a)
    return r, log_decay, kp, v, -kk, kk * a, g


def _f_rw_post(y, r, kp, v, g, ln_w, ln_b, r_k):
    yc = y - _head_sum(y) * (1.0 / HEAD)
    var = _head_sum(yc * yc) * (1.0 / HEAD)
    yn = yc * lax.rsqrt(var + GN_EPS) * ln_w + ln_b
    bonus = _head_sum(r * kp * r_k) * v
    return ((yn + bonus) * g,)


def _f_s5_lam(lam_re, lam_im, log_step):
    step = jnp.exp(log_step)
    dr, di = lam_re * step, lam_im * step
    e = jnp.exp(dr)
    lbr, lbi = e * jnp.cos(di), e * jnp.sin(di)
    nr, ni = lbr - 1.0, lbi
    den = lam_re * lam_re + lam_im * lam_im
    return lbr, lbi, (nr * lam_re + ni * lam_im) / den, (ni * lam_re - nr * lam_im) / den


def _f_s5_build(coef_r, coef_i, btr, bti, ctr, cti):
    bbr = coef_r * btr - coef_i * bti
    bbi = coef_r * bti + coef_i * btr
    shape = (S5_WIDTH, S5_TILE_LANES)
    rows = (lax.broadcasted_iota(jnp.int32, shape, 0) % S5_TILE_CH) // S5_GROUP
    cols = lax.broadcasted_iota(jnp.int32, shape, 1) // S5_STATE
    mask = (rows == cols).astype(F32)

    def blocks(m):
        per_tile = [m[:, S5_TILE_LANES * i:S5_TILE_LANES * (i + 1)] for i in range(S5_TILES)]
        return jnp.concatenate([t for t in per_tile for _ in range(S5_TILE_GROUPS)], axis=0) * mask

    return (jnp.concatenate([blocks(bbr), blocks(bbi)], axis=1),
            jnp.concatenate([blocks(ctr), -blocks(cti)], axis=1))


HALO = 8


def _rw_pre_specs(z, params, tm, order):
    halo_blocks = tm // HALO
    return ([pl.BlockSpec((tm, z.shape[1]), lambda i: (order(i), 0)),
             pl.BlockSpec((HALO, z.shape[1]), lambda i: (jnp.maximum(order(i) * halo_blocks - 1, 0), 0))]
            + [_full_spec(p) for p in params])


def _rw_pre_fwd(z, params, tm):
    t = z.shape[0]
    npar = len(params)

    def body(z_ref, halo_ref, *refs):
        carry = jnp.where(pl.program_id(0) == 0, 0.0, halo_ref[pl.ds(HALO - 1, 1), :])
        res = _f_rw_pre(z_ref[...], carry, *[r[...].astype(F32) for r in refs[:npar]])
        for r, v in zip(refs[npar:], res):
            r[...] = v

    return pl.pallas_call(
        body, name="rw_pre",
        out_shape=[jax.ShapeDtypeStruct((t, RWKV_WIDTH), F32)] * 7,
        grid=(t // tm,),
        in_specs=_rw_pre_specs(z, params, tm, lambda i: i),
        out_specs=[pl.BlockSpec((tm, RWKV_WIDTH), lambda i: (i, 0))] * 7,
        compiler_params=_params(("parallel",)),
    )(z, z, *params)


def _rw_pre_bwd(z, params, cots, tm):
    t = z.shape[0]
    nt = t // tm
    npar = len(params)
    order = lambda i: nt - 1 - i
    flat_cots = [a for group in cots for a in group]
    ncot = len(flat_cots)

    def body(z_ref, halo_ref, *refs):
        pin, cin = refs[:npar], list(refs[npar:npar + ncot])
        dz_ref = refs[npar + ncot]
        dpo = refs[npar + ncot + 1:npar + ncot + 1 + npar]
        dcarry_ref = refs[npar + ncot + 1 + npar]
        i = pl.program_id(0)

        @pl.when(i == 0)
        def _():
            dcarry_ref[...] = jnp.zeros_like(dcarry_ref)
            for r in dpo:
                r[...] = jnp.zeros_like(r)

        carry = jnp.where(i == nt - 1, 0.0, halo_ref[pl.ds(HALO - 1, 1), :])
        _, vjp = jax.vjp(_f_rw_pre, z_ref[...], carry, *[r[...].astype(F32) for r in pin])
        g = vjp(tuple(sum(cin.pop(0)[...] for _ in group) for group in cots))
        last_row = lax.broadcasted_iota(jnp.int32, z_ref.shape, 0) == tm - 1
        dz_ref[...] = (g[0] + jnp.where(last_row, dcarry_ref[...], 0.0)).astype(dz_ref.dtype)
        dcarry_ref[...] = g[1]
        for r, v in zip(dpo, g[2:]):
            r[...] += v

    tok = lambda w: pl.BlockSpec((tm, w), lambda i: (order(i), 0))
    return pl.pallas_call(
        body, name="rw_pre_bwd",
        out_shape=[jax.ShapeDtypeStruct(z.shape, BF16)] + [jax.ShapeDtypeStruct(p.shape, F32) for p in params],
        grid=(nt,),
        in_specs=_rw_pre_specs(z, params, tm, order) + [tok(RWKV_WIDTH)] * ncot,
        out_specs=[tok(z.shape[1])] + [_full_spec(p) for p in params],
        scratch_shapes=[pltpu.VMEM((1, z.shape[1]), F32)],
        compiler_params=_params(("arbitrary",)),
    )(z, z, *params, *flat_cots)


def _s5_scan(bu, lam, tm, plan):
    t, w = bu.shape
    h = w // 2
    nt = t // tm
    parts, plan_in_specs, plan_out_shape, plan_out_specs, plan_sems = _carry(plan, 2, 1)

    def body(*refs):
        bu_ref, lam_ref = refs[:2]
        xb_ref = refs[2 + len(plan.ins)]
        carry_ref = refs[3 + len(plan.ins) + len(plan.out_shape)]
        x_ref, refs = refs[-1], refs[:-1]

        @pl.when(pl.program_id(0) == 0)
        def _():
            carry_ref[...] = jnp.zeros_like(carry_ref)
            plan.start(*parts(refs))

        lr, li = lam_ref[:, :h], lam_ref[:, h:]
        x_ref[...] = bu_ref[...].astype(F32)

        def step(s, c):
            cr, ci = c
            row = pl.ds(s, 1)
            nr = lr * cr - li * ci + x_ref[row, :h]
            ni = lr * ci + li * cr + x_ref[row, h:]
            x_ref[row, :h] = nr
            x_ref[row, h:] = ni
            return nr, ni

        cr, ci = lax.fori_loop(0, tm, step, (carry_ref[:, :h], carry_ref[:, h:]), unroll=SCAN_UNROLL)
        carry_ref[:, :h] = cr
        carry_ref[:, h:] = ci
        xb_ref[...] = x_ref[...].astype(BF16)

        @pl.when(pl.program_id(0) == nt - 1)
        def _():
            plan.wait(*parts(refs))

    spec = pl.BlockSpec((tm, w), lambda i: (i, 0))
    res = pl.pallas_call(
        body, name="s5_scan",
        out_shape=[jax.ShapeDtypeStruct((t, w), BF16)] + plan_out_shape,
        grid=(nt,),
        in_specs=[spec, pl.BlockSpec((1, w), lambda i: (0, 0))] + plan_in_specs,
        out_specs=[spec] + plan_out_specs,
        scratch_shapes=[pltpu.VMEM((1, w), F32)] + plan_sems + [pltpu.VMEM((tm, w), F32)],
        compiler_params=_params(("arbitrary",)),
    )(bu, lam, *plan.ins)
    return res[0], res[1:]


def _s5_scan_bwd(dx, xb, lam, tm):
    t, w = dx.shape
    h = w // 2
    nt = t // tm
    halo = BF16_SUBLANES

    rows8 = 8

    def body(dx_ref, xb_ref, halo_ref, lam_ref, dbu_out_ref, dlam_ref, carry_ref, dbu_ref, xp_ref):
        @pl.when(pl.program_id(0) == 0)
        def _():
            carry_ref[...] = jnp.zeros_like(carry_ref)
            dlam_ref[...] = jnp.zeros_like(dlam_ref)

        lr, li = lam_ref[:, :h], lam_ref[:, h:]
        dbu_ref[...] = dx_ref[...].astype(F32)

        def step(s, c):
            cr, ci = c
            row = pl.ds(tm - 1 - s, 1)
            nr = lr * cr + li * ci + dbu_ref[row, :h]
            ni = lr * ci - li * cr + dbu_ref[row, h:]
            dbu_ref[row, :h] = nr
            dbu_ref[row, h:] = ni
            return nr, ni

        cr, ci = lax.fori_loop(0, tm, step, (carry_ref[:, :h], carry_ref[:, h:]), unroll=SCAN_UNROLL)
        carry_ref[:, :h] = cr
        carry_ref[:, h:] = ci
        halo_rows = lax.broadcasted_iota(jnp.int32, (halo, w), 0)
        before = jnp.sum(jnp.where(halo_rows == halo - 1, halo_ref[...].astype(F32), 0.0), axis=0, keepdims=True)
        before = jnp.where(pl.program_id(0) == nt - 1, 0.0, before)
        first_row = lax.broadcasted_iota(jnp.int32, (tm, w), 0) == 0
        xp_ref[...] = jnp.where(first_row, before, pltpu.roll(xb_ref[...].astype(F32), 1, 0))

        def accumulate(s, acc):
            ar, ai = acc
            rows = pl.ds(pl.multiple_of(s * rows8, rows8), rows8)
            gr, gi = dbu_ref[rows, :h], dbu_ref[rows, h:]
            pr, pi_ = xp_ref[rows, :h], xp_ref[rows, h:]
            return ar + (gr * pr + gi * pi_), ai + (gi * pr - gr * pi_)

        zero = jnp.zeros((rows8, h), F32)
        ar, ai = lax.fori_loop(0, tm // rows8, accumulate, (zero, zero))
        dlam_ref[:, :h] += jnp.sum(ar, axis=0, keepdims=True)
        dlam_ref[:, h:] += jnp.sum(ai, axis=0, keepdims=True)
        dbu_out_ref[...] = dbu_ref[...].astype(BF16)

    spec = pl.BlockSpec((tm, w), lambda i: (nt - 1 - i, 0))
    halo_spec = pl.BlockSpec((halo, w), lambda i: (jnp.maximum((nt - 1 - i) * (tm // halo) - 1, 0), 0))
    row_spec = pl.BlockSpec((1, w), lambda i: (0, 0))
    return pl.pallas_call(
        body, name="s5_scan_bwd",
        out_shape=[jax.ShapeDtypeStruct((t, w), BF16), jax.ShapeDtypeStruct((1, w), F32)],
        grid=(nt,),
        in_specs=[spec, spec, halo_spec, row_spec],
        out_specs=[spec, row_spec],
        scratch_shapes=[pltpu.VMEM((1, w), F32), pltpu.VMEM((tm, w), F32), pltpu.VMEM((tm, w), F32)],
        compiler_params=_params(("arbitrary",)),
    )(dx, xb, xb, lam)


def _unit_lower_inverses_impl(ns):
    c = ns[0].shape[0]
    eye = (lax.broadcasted_iota(jnp.int32, (c, c), 0) == lax.broadcasted_iota(jnp.int32, (c, c), 1)).astype(F32)
    inv = [eye + n for n in ns]
    pw = [_dg(n, n, _NN, WKV_PASSES) for n in ns]
    for _ in range(int(math.log2(c)) - 2):
        both = [_dg(jnp.concatenate([i, q], axis=0), q, _NN, WKV_PASSES) for i, q in zip(inv, pw)]
        inv = [i + q[:c] for i, q in zip(inv, both)]
        pw = [q[c:] for q in both]
    return tuple(i + _dg(i, q, _NN, WKV_PASSES) for i, q in zip(inv, pw))


@jax.custom_vjp
def _unit_lower_inverses(ns):
    return _unit_lower_inverses_impl(ns)


def _unit_lower_inverses_fwd(ns):
    inv = _unit_lower_inverses_impl(ns)
    return inv, inv


def _unit_lower_inverses_bwd(inv, g):
    left = [_dg(i, gi, _TN, WKV_PASSES) for i, gi in zip(inv, g)]
    return (tuple(_dg(q, i, _NT, WKV_PASSES) for q, i in zip(left, inv)),)


_unit_lower_inverses.defvjp(_unit_lower_inverses_fwd, _unit_lower_inverses_bwd)


def _wkv_chunks(s0, r, lw, k, v, a, b):
    c = r[0].shape[0]
    row = lax.broadcasted_iota(jnp.int32, (c, c), 0)
    col = lax.broadcasted_iota(jnp.int32, (c, c), 1)
    incl, strict = col <= row, col < row
    tri = incl.astype(F32)
    each = lambda f, *xs: [f(*t) for t in zip(*xs)]
    stack = lambda p, q: jnp.concatenate([p, q], axis=0)
    dot = lambda p, q, dims=_NN: _dot(p, q, dims, WKV_PASSES)
    lc = each(lambda l: _dot(tri, l, _NN, 2), lw)
    e_neg = each(lambda l: jnp.exp(-l), lc)
    ar = each(lambda x, z, l, w: stack(x * jnp.exp(l - w), z * jnp.exp(l)), a, r, lc, lw)
    bk = each(lambda x, z, e: stack(x * e, z * e), b, k, e_neg)
    m = each(lambda p, q: dot(p, q, _NT), ar, bk)
    mab = each(lambda q: jnp.where(strict, q[:c, :c], 0.0), m)
    mak_mrk = each(lambda q: stack(jnp.where(strict, q[:c, c:], 0.0), jnp.where(incl, q[c:, c:], 0.0)), m)
    mrb = each(lambda q: jnp.where(incl, q[c:, :c], 0.0), m)
    xy = each(lambda p, s, q, z: dot(p, s, _NT) + dot(q, z), ar, s0, mak_mrk, v)
    inv = _unit_lower_inverses(tuple(mab))
    u = each(lambda i, q: dot(i, q[:c]), inv, xy)
    y = each(lambda q, z, p: q[c:] + dot(z, p), xy, mrb, u)
    e_tot = each(lambda l: jnp.exp(jnp.sum(l, axis=0, keepdims=True)), lw)
    s1 = each(lambda s, p, z, q, e: (s + dot(stack(p, z), q, _TN)) * e, s0, u, v, bk, e_tot)
    return y, s1


def _carry(plan, n_args, n_outs):
    n_in, n_out = len(plan.ins), len(plan.out_shape)

    def parts(refs):
        base = n_args + n_in + n_outs
        return refs[n_args:n_args + n_in], refs[base:base + n_out], refs[base + n_out + 1:]

    return parts, [HBM_SPEC] * n_in, list(plan.out_shape), [HBM_SPEC] * n_out, list(plan.sems)


def _head_cols(ref):
    return tuple(ref[:, h * HEAD:(h + 1) * HEAD] for h in range(HEADS))


def _wkv_fwd(seqs, plan):
    t, w = seqs[0].shape
    c, n = WKV_CHUNK, HEAD
    nc = t // c
    parts, plan_in_specs, plan_out_shape, plan_out_specs, plan_sems = _carry(plan, 6, 2)

    def body(*refs):
        ins, (y_ref, ck_ref) = refs[:6], refs[6 + len(plan.ins):8 + len(plan.ins)]
        s_ref = refs[8 + len(plan.ins) + len(plan.out_shape)]

        @pl.when(pl.program_id(0) == 0)
        def _():
            s_ref[...] = jnp.zeros_like(s_ref)
            plan.start(*parts(refs))

        s0 = tuple(s_ref[h] for h in range(HEADS))
        ys, s1 = _wkv_chunks(s0, *[_head_cols(r) for r in ins])
        for h in range(HEADS):
            ck_ref[0, h] = s0[h]
            y_ref[:, h * n:(h + 1) * n] = ys[h]
            s_ref[h] = s1[h]

        @pl.when(pl.program_id(0) == nc - 1)
        def _():
            plan.wait(*parts(refs))

    spec = pl.BlockSpec((c, w), lambda i: (i, 0))
    res = pl.pallas_call(
        body, name="wkv_fwd",
        out_shape=[jax.ShapeDtypeStruct((t, w), F32), jax.ShapeDtypeStruct((nc, HEADS, n, n), F32)] + plan_out_shape,
        grid=(nc,),
        in_specs=[spec] * 6 + plan_in_specs,
        out_specs=[spec, pl.BlockSpec((1, HEADS, n, n), lambda i: (i, 0, 0, 0))] + plan_out_specs,
        scratch_shapes=[pltpu.VMEM((HEADS, n, n), F32)] + plan_sems,
        compiler_params=_params(("arbitrary",)),
    )(*seqs, *plan.ins)
    return res[0], res[1], res[2:]


def _wkv_bwd(seqs, ck, dy, plan):
    t, w = seqs[0].shape
    c, n = WKV_CHUNK, HEAD
    nc = t // c
    parts, plan_in_specs, plan_out_shape, plan_out_specs, plan_sems = _carry(plan, 8, 6)

    def body(*refs):
        ins, ck_ref, dy_ref = refs[:6], refs[6], refs[7]
        outs = refs[8 + len(plan.ins):14 + len(plan.ins)]
        ds_ref = refs[14 + len(plan.ins) + len(plan.out_shape)]

        @pl.when(pl.program_id(0) == 0)
        def _():
            ds_ref[...] = jnp.zeros_like(ds_ref)
            plan.start(*parts(refs))

        s0 = tuple(ck_ref[0, h] for h in range(HEADS))
        _, vjp = jax.vjp(_wkv_chunks, s0, *[_head_cols(r) for r in ins])
        g = vjp((list(_head_cols(dy_ref)), [ds_ref[h] for h in range(HEADS)]))
        for h in range(HEADS):
            ds_ref[h] = g[0][h]
            for o, d in zip(outs, g[1:]):
                o[:, h * n:(h + 1) * n] = d[h]

        @pl.when(pl.program_id(0) == nc - 1)
        def _():
            plan.wait(*parts(refs))

    spec = pl.BlockSpec((c, w), lambda i: (nc - 1 - i, 0))
    res = pl.pallas_call(
        body, name="wkv_bwd",
        out_shape=[jax.ShapeDtypeStruct((t, w), F32)] * 6 + plan_out_shape,
        grid=(nc,),
        in_specs=[spec] * 6 + [pl.BlockSpec((1, HEADS, n, n), lambda i: (nc - 1 - i, 0, 0, 0)), spec] + plan_in_specs,
        out_specs=[spec] * 6 + plan_out_specs,
        scratch_shapes=[pltpu.VMEM((HEADS, n, n), F32)] + plan_sems,
        compiler_params=_params(("arbitrary",)),
    )(*seqs, ck, dy, *plan.ins)
    return res[:6], res[6:]


def _coords():
    return lax.axis_index("x"), lax.axis_index("y"), lax.axis_index("c")


def _flip(v, f):
    return 1 - v if f else v


_CHIP_FLIPS = [(1, 0), (0, 1), (1, 1)]
_DEV_FLIPS = [(fx, fy, fc) for fx in (0, 1) for fy in (0, 1) for fc in (0, 1) if (fx, fy, fc) != (0, 0, 0)]
HBM_SPEC = pl.BlockSpec(memory_space=pl.ANY)


def _chip_peer(k, x, y):
    fx, fy = _CHIP_FLIPS[k]
    return _flip(x, fx), _flip(y, fy)


def _dev_peer(k, x, y, c):
    fx, fy, fc = _DEV_FLIPS[k]
    return _flip(x, fx), _flip(y, fy), _flip(c, fc)


def _rows_of_core(ref, core):
    h = ref.shape[-2] // 2
    rows = pl.ds(pl.multiple_of(core * h, 8), h)
    return ref.at[rows, :] if len(ref.shape) == 2 else ref.at[:, rows, :]


class _Plan(NamedTuple):
    ins: Sequence[Any]
    out_shape: Sequence[Any]
    sems: Sequence[Any]
    start: Callable
    wait: Callable


_NO_PLAN = _Plan([], [], [], lambda *_: None, lambda *_: None)


def _run_plan(name, plan):
    n_in, n_out = len(plan.ins), len(plan.out_shape)

    def body(*refs):
        parts = refs[:n_in], refs[n_in:n_in + n_out], refs[n_in + n_out:]
        plan.start(*parts)
        plan.wait(*parts)

    return pl.pallas_call(
        body, name=name, out_shape=list(plan.out_shape),
        in_specs=[HBM_SPEC] * n_in, out_specs=[HBM_SPEC] * n_out, scratch_shapes=list(plan.sems),
    )(*plan.ins)


def _gather_plan(shards):
    n = len(shards)

    def copies(srcs, outs, sems):
        send_sems, recv_sems, local_sems = sems
        x, y, c = _coords()
        me = 2 * x + y

        def remote(i, k, arriving):
            px, py = _chip_peer(k, x, y)
            return pltpu.make_async_remote_copy(
                src_ref=srcs[i], dst_ref=outs[i].at[2 * px + py if arriving else me],
                send_sem=send_sems.at[i, k], recv_sem=recv_sems.at[i, k],
                device_id=(px, py, c), device_id_type=MESH)

        own = [pltpu.make_async_copy(srcs[i], outs[i].at[me], local_sems.at[i]) for i in range(n)]
        pairs = [(i, k) for k in range(3) for i in range(n)]
        return own, [remote(i, k, False) for i, k in pairs], [remote(i, k, True) for i, k in pairs]

    return _Plan(
        ins=shards, out_shape=[jax.ShapeDtypeStruct((4,) + s.shape, s.dtype) for s in shards],
        sems=[pltpu.SemaphoreType.DMA((n, 3)), pltpu.SemaphoreType.DMA((n, 3)), pltpu.SemaphoreType.DMA((n,))],
        start=functools.partial(_start_copies, copies), wait=functools.partial(_wait_copies, copies))


def _start_copies(copies, ins, outs, sems):
    own, sends, _ = copies(ins, outs, sems)
    for cp in own + sends:
        cp.start()


def _wait_copies(copies, ins, outs, sems):
    own, sends, arrivals = copies(ins, outs, sems)
    for cp in arrivals:
        cp.wait_recv()
    for cp in sends:
        cp.wait_send()
    for cp in own:
        cp.wait()


def _exchange_plan(gs, small=None):
    n = len(gs)
    arrays = list(gs) + ([] if small is None else [small])

    def copies(srcs, outs, sems):
        send_sems, recv_sems, local_sems = sems
        x, y, c = _coords()
        me = 4 * x + 2 * y + c

        def piece(i, px, py, pc):
            if i == n:
                return srcs[i].at[4 * px + 2 * py + pc]
            return _rows_of_core(srcs[i].at[2 * px + py], pc)

        def remote(i, k, arriving):
            px, py, pc = _dev_peer(k, x, y, c)
            return pltpu.make_async_remote_copy(
                src_ref=piece(i, px, py, pc), dst_ref=outs[i].at[4 * px + 2 * py + pc if arriving else me],
                send_sem=send_sems.at[i, k], recv_sem=recv_sems.at[i, k],
                device_id=(px, py, pc), device_id_type=MESH)

        own = [pltpu.make_async_copy(piece(i, x, y, c), outs[i].at[me], local_sems.at[i]) for i in range(len(arrays))]
        pairs = [(i, k) for k in range(7) for i in range(len(arrays))]
        return own, [remote(i, k, False) for i, k in pairs], [remote(i, k, True) for i, k in pairs]

    out_shape = [jax.ShapeDtypeStruct((8, g.shape[1] // 2, g.shape[2]), g.dtype) for g in gs]
    if small is not None:
        out_shape.append(jax.ShapeDtypeStruct(small.shape, small.dtype))
    m = len(arrays)
    return _Plan(
        ins=arrays, out_shape=out_shape,
        sems=[pltpu.SemaphoreType.DMA((m, 7)), pltpu.SemaphoreType.DMA((m, 7)), pltpu.SemaphoreType.DMA((m,))],
        start=functools.partial(_start_copies, copies), wait=functools.partial(_wait_copies, copies))


def _share_cores(halves, small):
    n = len(halves)

    def body(*refs):
        srcs, small_src, outs, small_out = refs[:n], refs[n], refs[n + 1:2 * n + 1], refs[2 * n + 1]
        mine, theirs = refs[2 * n + 2:3 * n + 2], refs[3 * n + 2:4 * n + 2]
        send_sems, recv_sems, ssend, srecv, local_sems = refs[4 * n + 2:]
        x, y, c = _coords()
        me = 4 * x + 2 * y + c

        def big(i):
            return pltpu.make_async_remote_copy(
                src_ref=mine[i], dst_ref=theirs[i], send_sem=send_sems.at[i], recv_sem=recv_sems.at[i],
                device_id=(x, y, 1 - c), device_id_type=MESH)

        def tiny(k, arriving):
            px, py, pc = _dev_peer(k, x, y, c)
            return pltpu.make_async_remote_copy(
                src_ref=small_src, dst_ref=small_out.at[4 * px + 2 * py + pc if arriving else me],
                send_sem=ssend.at[k], recv_sem=srecv.at[k], device_id=(px, py, pc), device_id_type=MESH)

        small_sends = [tiny(k, False) for k in range(7)]
        own_small = pltpu.make_async_copy(small_src, small_out.at[me], local_sems.at[2 * n])
        stage = [pltpu.make_async_copy(srcs[i], mine[i], local_sems.at[i]) for i in range(n)]
        for cp in small_sends + [own_small] + stage:
            cp.start()
        sends = []
        for i in range(n):
            stage[i].wait()
            sends.append(big(i))
            sends[-1].start()
        store = [pltpu.make_async_copy(mine[i], outs[i].at[c], local_sems.at[i]) for i in range(n)]
        for cp in store:
            cp.start()
        for i in range(n):
            big(i).wait_recv()
            store.append(pltpu.make_async_copy(theirs[i], outs[i].at[1 - c], local_sems.at[n + i]))
            store[-1].start()
        for k in range(7):
            tiny(k, True).wait_recv()
        for cp in sends + small_sends:
            cp.wait_send()
        for cp in store + [own_small]:
            cp.wait()

    staged = [pltpu.VMEM(s.shape, s.dtype) for s in halves]
    res = pl.pallas_call(
        body, name="share_cores",
        out_shape=[jax.ShapeDtypeStruct((2,) + s.shape, s.dtype) for s in halves]
        + [jax.ShapeDtypeStruct((8,) + small.shape, small.dtype)],
        in_specs=[HBM_SPEC] * (n + 1), out_specs=[HBM_SPEC] * (n + 1),
        scratch_shapes=staged + staged + [
            pltpu.SemaphoreType.DMA((n,)), pltpu.SemaphoreType.DMA((n,)),
            pltpu.SemaphoreType.DMA((7,)), pltpu.SemaphoreType.DMA((7,)),
            pltpu.SemaphoreType.DMA((2 * n + 1,))],
        compiler_params=pltpu.CompilerParams(vmem_limit_bytes=VMEM_LIMIT_BYTES),
    )(*halves, small)
    return res[:n], res[n]


BF16_SUBLANES = 16


def _row_tile(n, target, step=BF16_SUBLANES):
    return max([d for d in range(step, min(n, target) + 1, step) if n % d == 0] or [n])


def _ew(name, fn, ins, outs, block_bytes=2 << 20):
    rows, cols = ins[0].shape[-2:]
    lead = max(math.prod(a.shape[:-2]) for a in ins)
    tr = _row_tile(rows, max(8, block_bytes // (4 * cols * lead)))
    n = len(ins)

    def spec(shape):
        if len(shape) == 2:
            return pl.BlockSpec((tr, cols), lambda i: (i, 0))
        return pl.BlockSpec((shape[0], tr, cols), lambda i: (0, i, 0))

    def body(*refs):
        res = fn(*[r[...] for r in refs[:n]])
        for r, v in zip(refs[n:], res):
            r[...] = v

    return pl.pallas_call(
        body, name=name,
        out_shape=[jax.ShapeDtypeStruct(s, F32) for s in outs],
        grid=(rows // tr,),
        in_specs=[spec(a.shape) for a in ins],
        out_specs=[spec(s) for s in outs],
        compiler_params=_params(("parallel",)),
    )(*ins)


def _sum_slots(a):
    total = a[0].astype(F32)
    for s in range(1, a.shape[0]):
        total = total + a[s].astype(F32)
    return (total,)


def _adamw(g, w, m, v):
    bc1 = 1.0 - ADAM_B1 ** ADAM_STEP
    bc2 = 1.0 - ADAM_B2 ** ADAM_STEP
    m_new = ADAM_B1 * m + (1.0 - ADAM_B1) * g
    v_new = ADAM_B2 * v + (1.0 - ADAM_B2) * jnp.square(g)
    delta = -ADAM_LR * ((m_new / bc1) / (jnp.sqrt(v_new / bc2) + ADAM_EPS) + ADAM_WD * w)
    return delta, m_new, v_new


def _mat(a):
    return a.reshape(a.shape[-2:])


def _to_shard_major(full, axis):
    rows, cols = full.shape
    if axis == 0:
        return full.reshape(4, rows // 4, cols)
    return full.reshape(rows, 4, cols // 4).transpose(1, 0, 2)


def _from_shard_major(a, axis):
    _, r, cs = a.shape
    if axis == 0:
        return a.reshape(4 * r, cs)
    return a.transpose(1, 0, 2).reshape(r, 4 * cs)


def _pack_small(arrays, tail=None):
    flat = [arrays[n].reshape(-1) for n in SMALL_NAMES] + ([] if tail is None else [tail.reshape(1)])
    used = sum(a.shape[0] for a in flat)
    flat.append(jnp.zeros((8 * SMALL_ROWS * PACK_COLS - used,), F32))
    return jnp.concatenate(flat).reshape(8, SMALL_ROWS, PACK_COLS)


def _unpack_small(packed, shapes):
    flat = packed.reshape(-1)
    out, off = {}, 0
    for n in SMALL_NAMES:
        size = math.prod(shapes[n])
        out[n] = flat[off:off + size].reshape(shapes[n])
        off += size
    return out


def _row(a):
    return a.reshape(1, -1)


def _local_step(x, p, target, wf, ws, late_shards):
    wf = dict(wf)
    t = x.shape[0]
    tm = min(256, t)
    tw = min(512, t)
    tx = min(1024, t)
    g = {}

    lam_re, lam_im = ws['s5_lam_re'].reshape(S5_GROUPS, S5_STATE), ws['s5_lam_im'].reshape(S5_GROUPS, S5_STATE)
    log_step = ws['s5_log_step'].reshape(S5_GROUPS, 1)
    gp = (S5_GROUPS, S5_STATE)
    lam_ins = (lam_re, lam_im, log_step)
    lbr, lbi, cfr, cfi = _small_fwd("s5_lam", _f_s5_lam, lam_ins, [(gp, F32)] * 4)
    lam_row = jnp.concatenate([_row(lbr), _row(lbi)], axis=1)
    to_t = lambda a, perm: a.reshape((S5_GROUPS,) + a.shape[-2:]).transpose(perm).reshape(S5_GROUP, S5_LANES)
    build_ins = (_row(cfr), _row(cfi), to_t(ws['s5_b_re'], (2, 0, 1)), to_t(ws['s5_b_im'], (2, 0, 1)),
                 to_t(ws['s5_c_re'], (1, 0, 2)), to_t(ws['s5_c_im'], (1, 0, 2)))
    block_shape = (S5_WIDTH, 2 * S5_TILE_LANES)
    b_blk, c_blk = _small_fwd("s5_build", _f_s5_build, build_ins, [(block_shape, F32)] * 2)

    norm_mix, norm_ffn, norm_ple = _row(ws['norm_mix']), _row(ws['norm_ffn']), _row(ws['norm_ple'])
    final_norm = _row(ws['final_norm'])
    (xn,) = _tok_fwd("norm_in", _f_norm_in, [x], [norm_mix], [(x.shape[1], BF16)], tx)
    u = _mm("proj_s5", xn, wf['w_in'][:, :S5_WIDTH], 'nn')
    z = _mm("proj_rw", xn, wf['w_in'][:, S5_WIDTH:], 'nn')

    bu = _s5_expand("s5_bu", u, b_blk)
    def late_plan(carrier):
        return _gather_plan([late_shards[n] for n in LATE_GATHER[carrier]])

    def arrived(carrier, got):
        wf.update({n: _from_shard_major(a, SHARDED[n]) for n, a in zip(LATE_GATHER[carrier], got)})

    xs, got = _s5_scan(bu, lam_row, tm, late_plan('s5_scan'))
    arrived('s5_scan', got)
    ypre = _s5_contract("s5_y", xs, c_blk)
    s5_par = [_row(ws['s5_d']), wf['s5_glu_w'], _row(ws['s5_glu_b'])]
    (s5_out,) = _tok_fwd("s5_post", _f_s5_post, [ypre, u], s5_par, [(S5_WIDTH, BF16)], tx)

    pre_par = [_row(ws['rw_shift_mu']), _row(ws['rw_w0']), wf['rw_w2'], _row(ws['rw_a0']), wf['rw_a2'],
               wf['rw_g2'], _row(ws['rw_k_k']), _row(ws['rw_k_a'])]
    r, lw, kp, v, an, bn, gate = _rw_pre_fwd(z, pre_par, tw)
    seqs = [r, lw, kp, v, an, bn]
    y_wkv, ck, got = _wkv_fwd(seqs, late_plan('wkv_fwd'))
    arrived('wkv_fwd', got)
    post_par = [_row(ws['rw_ln_w']), _row(ws['rw_ln_b']), _row(ws['rw_r_k'])]
    post_toks = [y_wkv, r, kp, v, gate]
    (rw_out,) = _tok_fwd("rw_post", _f_rw_post, post_toks, post_par, [(RWKV_WIDTH, BF16)], tx)

    mixcat = jnp.concatenate([s5_out, rw_out], axis=1)
    mixed = _mm("mix_out", mixcat, wf['w_out'], 'nn')
    h1, hn = _tok_fwd("mix_res", _f_mix_res, [x, mixed], [norm_ffn], [(x.shape[1], F32), (x.shape[1], BF16)], tx)
    w13 = jnp.concatenate([wf['ffn_w1'], wf['ffn_w3']], axis=1)
    a13, got = _mm("ffn_up", hn, w13, 'nn', out_dtype=BF16, tn=FFN_TILE, plan=late_plan('ffn_up'))
    arrived('ffn_up', got)
    (f,) = _tok_fwd("ffn_act", _f_ffn_act, [a13], [], [(FFN_HIDDEN, BF16)], tw)
    ffo, got = _mm("ffn_down", f, wf['ffn_w2'], 'nn', plan=late_plan('ffn_down'))
    arrived('ffn_down', got)
    h2, hp = _tok_fwd("ffn_res", _f_ffn_res, [h1, ffo], [norm_ple], [(x.shape[1], F32), (x.shape[1], BF16)], tx)
    gpre = _mm("ple_gate", hp, wf['ple_gate_w'], 'nn')
    pu = _mm("ple_up", p, wf['ple_up_w'], 'nn')

    dh2, dgpre, dpu, g['final_norm'], loss = _tok_bwd(
        "loss", _f_loss, [h2, gpre, pu, target], [final_norm], [None],
        [F32, BF16, BF16, None], [True], tw, acc_out=0)
    g['ple_gate_w'] = _mm("d_ple_gate_w", hp, dgpre, 'tn', out_dtype=WIRE)
    g['ple_up_w'] = _mm("d_ple_up_w", p, dpu, 'tn', out_dtype=WIRE)
    dhp = _mm("d_hp", dgpre, wf['ple_gate_w'], 'nt')
    dh1, dffo, g['norm_ple'] = _tok_bwd("ffn_res_bwd", _f_ffn_res, [h1, ffo], [norm_ple], [dh2, dhp],
                                        [F32, BF16], [True], tw)
    g['ffn_w2'] = _mm("d_ffn_w2", f, dffo, 'tn', out_dtype=WIRE, tm=FFN_TILE)
    df = _mm("d_f", dffo, wf['ffn_w2'], 'nt', out_dtype=BF16, tn=FFN_TILE)
    (da13,) = _tok_bwd("ffn_act_bwd", _f_ffn_act, [a13], [], [df], [BF16], [], tw)
    dw13 = _mm("d_ffn_w13", hn, da13, 'tn', out_dtype=WIRE, tn=FFN_TILE)
    dw13 = dw13.reshape(dw13.shape[0], 8, FFN_HIDDEN // 4).transpose(1, 0, 2)
    shard_major = {'ffn_w1': dw13[:4], 'ffn_w3': dw13[4:]}
    dhn = _mm("d_hn", da13, w13, 'nt')
    dx_a, dmixed, g['norm_ffn'] = _tok_bwd("mix_res_bwd", _f_mix_res, [x, mixed], [norm_ffn], [dh1, dhn],
                                           [F32, BF16], [True], tw)
    g['w_out'] = _mm("d_w_out", mixcat, dmixed, 'tn', out_dtype=WIRE)
    ds5_out = _mm("d_s5_out", dmixed, wf['w_out'][:S5_WIDTH], 'nt')
    drw_out = _mm("d_rw_out", dmixed, wf['w_out'][S5_WIDTH:], 'nt')

    dy_wkv, dr_b, dkp_b, dv_b, dgate, g['rw_ln_w'], g['rw_ln_b'], g['rw_r_k'] = _tok_bwd(
        "rw_post_bwd", _f_rw_post, post_toks, post_par, [drw_out], [F32] * 5, [True] * 3, tw)
    late_exchange = _exchange_plan([shard_major[n] if n in shard_major else
                                    _to_shard_major(g[n], SHARDED[n]).astype(WIRE) for n in LATE_NAMES])
    dseqs, late_pieces = _wkv_bwd(seqs, ck, dy_wkv, late_exchange)
    pre_cots = [(dseqs[0], dr_b), (dseqs[1],), (dseqs[2], dkp_b), (dseqs[3], dv_b), (dseqs[4],), (dseqs[5],),
                (dgate,)]
    dz, *dpre = _rw_pre_bwd(z, pre_par, pre_cots, tw)
    for n, d in zip(['rw_shift_mu', 'rw_w0', 'rw_w2', 'rw_a0', 'rw_a2', 'rw_g2', 'rw_k_k', 'rw_k_a'], dpre):
        g[n] = d

    dypre, du_a, g['s5_d'], g['s5_glu_w'], g['s5_glu_b'] = _tok_bwd(
        "s5_post_bwd", _f_s5_post, [ypre, u], s5_par, [ds5_out], [F32, F32], [True] * 3, tw)
    dxs = _s5_expand("d_s5_x", dypre, c_blk)
    dc_blk = _s5_block_grad("d_s5_c", dypre, xs)
    dbu, dlam_row = _s5_scan_bwd(dxs, xs, lam_row, tm)
    du = _s5_contract("d_s5_u", dbu, b_blk, add=du_a, out_dtype=BF16)
    db_blk = _s5_block_grad("d_s5_b", u, dbu)
    dbuild = _small_bwd("s5_build_bwd", _f_s5_build, build_ins, (db_blk, dc_blk))
    lam_cots = (dlam_row[:, :S5_LANES].reshape(gp), dlam_row[:, S5_LANES:].reshape(gp),
                dbuild[0].reshape(gp), dbuild[1].reshape(gp))
    g['s5_lam_re'], g['s5_lam_im'], g['s5_log_step'] = _small_bwd("s5_lam_bwd", _f_s5_lam, lam_ins, lam_cots)
    from_t = lambda a, perm: a.reshape(S5_GROUP, S5_GROUPS, S5_STATE).transpose(perm)
    g['s5_b_re'], g['s5_b_im'] = from_t(dbuild[2], (1, 2, 0)), from_t(dbuild[3], (1, 2, 0))
    g['s5_c_re'], g['s5_c_im'] = from_t(dbuild[4], (1, 0, 2)), from_t(dbuild[5], (1, 0, 2))

    dproj = jnp.concatenate([du, dz], axis=1)
    dw_in = _mm("d_w_in", xn, dproj, 'tn', out_dtype=WIRE)
    w_in_exchange = _exchange_plan([_to_shard_major(dw_in, SHARDED['w_in'])])
    dxn, (w_in_pieces,) = _mm("d_xn", dproj, wf['w_in'], 'nt', plan=w_in_exchange)
    pieces = dict(zip(LATE_NAMES, late_pieces), w_in=w_in_pieces)
    grad_x, g['norm_mix'] = _tok_bwd("norm_in_bwd", _f_norm_in, [x], [norm_mix], [dxn], [F32], [True], tw,
                                     add_to=(0, dx_a))
    return loss[0, 0], grad_x, g, pieces


def _step(x, p, target, w, m, v):
    shards = {n: _mat(w[n]).astype(BF16) for n in SHARDED_NAMES}
    early = _run_plan("gather_early", _gather_plan([shards[n] for n in EARLY_NAMES]))
    wf = {n: _from_shard_major(a, SHARDED[n]) for n, a in zip(EARLY_NAMES, early)}
    ws = {n: w[n] for n in SMALL_NAMES}

    loss, grad_x, g, pieces = _local_step(x[0], p[0, 0], target[0], wf, ws, shards)

    last_names = [n for n in SHARDED_NAMES if n not in pieces]
    last_plan = _exchange_plan([_to_shard_major(g[n], SHARDED[n]).astype(WIRE) for n in last_names],
                               _pack_small({n: g[n] for n in SMALL_NAMES}, tail=loss))
    *last_pieces, by_dev = _run_plan("exchange_last", last_plan)
    pieces.update(zip(last_names, last_pieces))
    halves = [_ew("add_devices_" + n, _sum_slots, [pieces[n]], [pieces[n].shape[1:]])[0] for n in SHARDED_NAMES]
    (small_piece,) = _ew("add_devices_small", _sum_slots, [by_dev], [by_dev.shape[1:]])
    both, small_g = _share_cores(halves, small_piece)

    kinds = [{}, {}, {}, {}]
    for n, gn in zip(SHARDED_NAMES, both):
        shard = _mat(w[n]).shape
        res = _ew("adamw_" + n, _adamw, [gn.reshape(shard), _mat(w[n]), _mat(m[n]), _mat(v[n])], [shard] * 3)
        for kind, a in zip(kinds, [gn] + list(res)):
            kind[n] = a.reshape(w[n].shape)
    flat = (8 * SMALL_ROWS, PACK_COLS)
    packed = [_pack_small({n: d[n] for n in SMALL_NAMES}).reshape(flat) for d in (w, m, v)]
    small_res = _ew("adamw_small", _adamw, [small_g.reshape(flat)] + packed, [flat] * 3)
    small_shapes = {n: w[n].shape for n in SMALL_NAMES}
    for kind, a in zip(kinds, [small_g] + list(small_res)):
        kind.update(_unpack_small(a, small_shapes))
    total = small_g.reshape(-1)[sum(math.prod(s) for s in small_shapes.values())]
    return (total, grad_x[None], *[kind[n] for kind in kinds for n in WEIGHT_NAMES])


def kernel(x, p, norm_mix, w_in, s5_lam_re, s5_lam_im, s5_log_step, s5_b_re, s5_b_im, s5_c_re, s5_c_im, s5_d, s5_glu_w, s5_glu_b, rw_shift_mu, rw_w0, rw_w2, rw_a0, rw_a2, rw_g2, rw_k_k, rw_k_a, rw_r_k, rw_ln_w, rw_ln_b, w_out, norm_ffn, ffn_w1, ffn_w3, ffn_w2, norm_ple, ple_gate_w, ple_up_w, final_norm, loss_target, m_norm_mix, m_w_in, m_s5_lam_re, m_s5_lam_im, m_s5_log_step, m_s5_b_re, m_s5_b_im, m_s5_c_re, m_s5_c_im, m_s5_d, m_s5_glu_w, m_s5_glu_b, m_rw_shift_mu, m_rw_w0, m_rw_w2, m_rw_a0, m_rw_a2, m_rw_g2, m_rw_k_k, m_rw_k_a, m_rw_r_k, m_rw_ln_w, m_rw_ln_b, m_w_out, m_norm_ffn, m_ffn_w1, m_ffn_w3, m_ffn_w2, m_norm_ple, m_ple_gate_w, m_ple_up_w, m_final_norm, v_norm_mix, v_w_in, v_s5_lam_re, v_s5_lam_im, v_s5_log_step, v_s5_b_re, v_s5_b_im, v_s5_c_re, v_s5_c_im, v_s5_d, v_s5_glu_w, v_s5_glu_b, v_rw_shift_mu, v_rw_w0, v_rw_w2, v_rw_a0, v_rw_a2, v_rw_g2, v_rw_k_k, v_rw_k_a, v_rw_r_k, v_rw_ln_w, v_rw_ln_b, v_w_out, v_norm_ffn, v_ffn_w1, v_ffn_w3, v_ffn_w2, v_norm_ple, v_ple_gate_w, v_ple_up_w, v_final_norm):
    args = dict(locals())
    w = {n: args[n] for n in WEIGHT_NAMES}
    m = {n: args["m_" + n] for n in WEIGHT_NAMES}
    v = {n: args["v_" + n] for n in WEIGHT_NAMES}
    return _step(x, p, loss_target, w, m, v)
```

```python
import functools
import math
from typing import Any, Callable, NamedTuple, Sequence

import jax
import jax.numpy as jnp
from jax import lax
from jax.experimental import pallas as pl
from jax.experimental.pallas import tpu as pltpu

F32 = jnp.float32
BF16 = jnp.bfloat16
MESH = pl.DeviceIdType.MESH

S5_WIDTH = 512
RWKV_WIDTH = 512
S5_GROUP = 16
S5_GROUPS = 32
S5_STATE = 64
S5_LANES = S5_GROUPS * S5_STATE
S5_TILE_GROUPS = 8
S5_TILES = S5_GROUPS // S5_TILE_GROUPS
S5_TILE_CH = S5_TILE_GROUPS * S5_GROUP
S5_TILE_LANES = S5_TILE_GROUPS * S5_STATE
HEAD = 64
HEADS = 8
DECAY_LORA = 64
AAA_LORA = 64
GATE_LORA = 128
FFN_HIDDEN = 2816
FFN_TILE = FFN_HIDDEN // 2
RMS_EPS = 1e-6
GN_EPS = 64e-5
L2_EPS = 1e-12
ADAM_LR = 0.001
ADAM_B1 = 0.9
ADAM_B2 = 0.999
ADAM_EPS = 1e-08
ADAM_WD = 0.01
ADAM_STEP = 10

WKV_CHUNK = 64
SCAN_UNROLL = 4
WIRE = jnp.bfloat16
WKV_PASSES = 1
VMEM_LIMIT_BYTES = 48 * 1024 * 1024
LANE = 128
PACK_COLS = 1024
SMALL_ROWS = 24

WEIGHT_NAMES = ['norm_mix', 'w_in', 's5_lam_re', 's5_lam_im', 's5_log_step', 's5_b_re', 's5_b_im', 's5_c_re',
                's5_c_im', 's5_d', 's5_glu_w', 's5_glu_b', 'rw_shift_mu', 'rw_w0', 'rw_w2', 'rw_a0', 'rw_a2',
                'rw_g2', 'rw_k_k', 'rw_k_a', 'rw_r_k', 'rw_ln_w', 'rw_ln_b', 'w_out', 'norm_ffn', 'ffn_w1',
                'ffn_w3', 'ffn_w2', 'norm_ple', 'ple_gate_w', 'ple_up_w', 'final_norm']
SHARDED = {'w_in': 1, 's5_glu_w': 0, 'rw_w2': 1, 'rw_a2': 1, 'rw_g2': 1, 'w_out': 0, 'ffn_w1': 1, 'ffn_w3': 1,
           'ffn_w2': 0, 'ple_gate_w': 0, 'ple_up_w': 1}
SHARDED_NAMES = [n for n in WEIGHT_NAMES if n in SHARDED]
LATE_NAMES = ['w_out', 'ffn_w1', 'ffn_w3', 'ffn_w2', 'ple_gate_w', 'ple_up_w']
EARLY_NAMES = [n for n in SHARDED_NAMES if n not in LATE_NAMES]
LATE_GATHER = {'s5_scan': ['w_out', 'ple_gate_w', 'ple_up_w'], 'wkv_fwd': ['ffn_w1', 'ffn_w3'], 'ffn_up': ['ffn_w2']}
SMALL_NAMES = [n for n in WEIGHT_NAMES if n not in SHARDED]


def _params(sem=None):
    return pltpu.CompilerParams(dimension_semantics=sem, vmem_limit_bytes=VMEM_LIMIT_BYTES)


def _tile(n, target):
    best = None
    for d in range(LANE, min(n, target) + 1, LANE):
        if n % d == 0:
            best = d
    return n if best is None else best


_NN = (((1,), (0,)), ((), ()))
_NT = (((1,), (1,)), ((), ()))
_TN = (((0,), (0,)), ((), ()))


def _split(a):
    a = a.astype(F32)
    hi = a.astype(BF16)
    return hi, (a - hi.astype(F32)).astype(BF16)


def _dg(a, b, dims, passes):
    dg = lambda p, q: lax.dot_general(p, q, dims, preferred_element_type=F32)
    if passes == 1:
        return dg(a.astype(BF16), b.astype(BF16))
    bh, bl = _split(b)
    return dg(a.astype(BF16), bh) + dg(a.astype(BF16), bl)


_DOT_BWD = {_NN: (("g", "b", _NT), ("a", "g", _TN)),
            _NT: (("g", "b", _NN), ("g", "a", _TN)),
            _TN: (("b", "g", _NT), ("a", "g", _NN))}


@functools.partial(jax.custom_vjp, nondiff_argnums=(2, 3))
def _dot(a, b, dims, passes):
    return _dg(a, b, dims, passes)


def _dot_fwd(a, b, dims, passes):
    return _dg(a, b, dims, passes), (a, b)


def _dot_bwd(dims, passes, res, g):
    env = {"a": res[0], "b": res[1], "g": g}
    return tuple(_dg(env[p], env[q], d, passes) for p, q, d in _DOT_BWD[dims])


_dot.defvjp(_dot_fwd, _dot_bwd)


def _bdot(x, w):
    return _dot(x, w, _NN, 1)


@jax.custom_vjp
def _shift_down(z):
    return pltpu.roll(z, 1, 0)


def _shift_down_fwd(z):
    return pltpu.roll(z, 1, 0), None


def _shift_down_bwd(_, g):
    return (pltpu.roll(g, g.shape[0] - 1, 0),)


_shift_down.defvjp(_shift_down_fwd, _shift_down_bwd)


def _head_sum_impl(x):
    r = lax.broadcasted_iota(jnp.int32, (LANE, LANE), 0) // HEAD
    c = lax.broadcasted_iota(jnp.int32, (LANE, LANE), 1) // HEAD
    ones = (r == c).astype(BF16)
    hi, lo = _split(x)
    dg = lambda p: lax.dot_general(p, ones, _NN, preferred_element_type=F32)
    tiles = [slice(j, j + LANE) for j in range(0, x.shape[1], LANE)]
    return jnp.concatenate([dg(hi[:, s]) + dg(lo[:, s]) for s in tiles], axis=1)


@jax.custom_vjp
def _head_sum(x):
    return _head_sum_impl(x)


_head_sum.defvjp(lambda x: (_head_sum_impl(x), None), lambda _, g: (_head_sum_impl(g),))


def _mm(name, a, b, mode, out_dtype=F32, tm=2048, tn=1024, tk=1536, plan=None):
    if mode == 'nn':
        (m, k), (_, n) = a.shape, b.shape
    elif mode == 'nt':
        (m, k), (n, _) = a.shape, b.shape
    else:
        (k, m), (_, n) = a.shape, b.shape
    tm, tn, tk = _tile(m, tm), _tile(n, tn), _tile(k, tk)
    nm, nn, nk = m // tm, n // tn, k // tk
    dims = {'nn': _NN, 'nt': _NT, 'tn': _TN}[mode]
    plan = _NO_PLAN if plan is None else plan
    parts, plan_in_specs, plan_out_shape, plan_out_specs, plan_sems = _carry(plan, 2, 1)

    def body(*refs):
        a_ref, b_ref, o_ref = refs[0], refs[1], refs[2 + len(plan.ins)]
        acc_ref = refs[3 + len(plan.ins) + len(plan.out_shape)]
        i, j, kk = pl.program_id(0), pl.program_id(1), pl.program_id(2)

        if plan is not _NO_PLAN:
            pl.when((i == 0) & (j == 0) & (kk == 0))(lambda: plan.start(*parts(refs)))

        @pl.when(kk == 0)
        def _():
            acc_ref[...] = jnp.zeros_like(acc_ref)

        acc_ref[...] += _dg(a_ref[...], b_ref[...], dims, 1)

        @pl.when(kk == nk - 1)
        def _():
            o_ref[...] = acc_ref[...].astype(o_ref.dtype)

        if plan is not _NO_PLAN:
            pl.when((i == nm - 1) & (j == nn - 1) & (kk == nk - 1))(lambda: plan.wait(*parts(refs)))

    if mode == 'tn':
        a_spec = pl.BlockSpec((tk, tm), lambda i, j, l: (l, i))
    else:
        a_spec = pl.BlockSpec((tm, tk), lambda i, j, l: (i, l))
    if mode == 'nt':
        b_spec = pl.BlockSpec((tn, tk), lambda i, j, l: (j, l))
    else:
        b_spec = pl.BlockSpec((tk, tn), lambda i, j, l: (l, j))
    res = pl.pallas_call(
        body, name=name,
        out_shape=[jax.ShapeDtypeStruct((m, n), out_dtype)] + plan_out_shape,
        grid=(nm, nn, nk),
        in_specs=[a_spec, b_spec] + plan_in_specs,
        out_specs=[pl.BlockSpec((tm, tn), lambda i, j, l: (i, j))] + plan_out_specs,
        scratch_shapes=[pltpu.VMEM((tm, tn), F32)] + plan_sems,
        compiler_params=_params(("parallel", "parallel", "arbitrary") if plan is _NO_PLAN else ("arbitrary",) * 3),
    )(a, b, *plan.ins)
    return res[0] if plan is _NO_PLAN else (res[0], res[1:])


def _mm_tiles(name, a, b, mode, out_shape, grid, a_spec, b_spec, o_spec, add=None, out_dtype=F32):
    dims = {'nn': _NN, 'nt': _NT, 'tn': _TN}[mode]
    nk = grid[2]
    extra = [] if add is None else [add]

    def body(a_ref, b_ref, *refs):
        o_ref, acc_ref = refs[len(extra):]
        kk = pl.program_id(2)

        @pl.when(kk == 0)
        def _():
            acc_ref[...] = refs[0][...].astype(F32) if extra else jnp.zeros_like(acc_ref)

        acc_ref[...] += _dg(a_ref[...], b_ref[...], dims, 1)

        @pl.when(kk == nk - 1)
        def _():
            o_ref[...] = acc_ref[...].astype(o_ref.dtype)

    return pl.pallas_call(
        body, name=name,
        out_shape=jax.ShapeDtypeStruct(out_shape, out_dtype),
        grid=grid, in_specs=[a_spec, b_spec] + [o_spec] * len(extra), out_specs=o_spec,
        scratch_shapes=[pltpu.VMEM(o_spec.block_shape, F32)],
        compiler_params=_params(("parallel", "parallel", "arbitrary")),
    )(a, b, *extra)


def _s5_expand(name, u, blk, tm=2048):
    t = u.shape[0]
    tm = min(tm, t)
    ch, ln, nt = S5_TILE_CH, S5_TILE_LANES, S5_TILES
    return _mm_tiles(name, u, blk, 'nn', (t, 2 * S5_LANES), (t // tm, 2 * nt, 1),
                     pl.BlockSpec((tm, ch), lambda i, j, l: (i, j % nt)),
                     pl.BlockSpec((ch, ln), lambda i, j, l: (j % nt, j // nt)),
                     pl.BlockSpec((tm, ln), lambda i, j, l: (i, j)), out_dtype=BF16)


def _s5_contract(name, x, blk, tm=2048, add=None, out_dtype=F32):
    t = x.shape[0]
    tm = min(tm, t)
    ch, ln, nt = S5_TILE_CH, S5_TILE_LANES, S5_TILES
    return _mm_tiles(name, x, blk, 'nt', (t, S5_WIDTH), (t // tm, nt, 2),
                     pl.BlockSpec((tm, ln), lambda i, j, l: (i, j + nt * l)),
                     pl.BlockSpec((ch, ln), lambda i, j, l: (j, l)),
                     pl.BlockSpec((tm, ch), lambda i, j, l: (i, j)), add=add, out_dtype=out_dtype)


def _s5_block_grad(name, u, x, tk=2048):
    t = u.shape[0]
    tk = min(tk, t)
    ch, ln, nt = S5_TILE_CH, S5_TILE_LANES, S5_TILES
    return _mm_tiles(name, u, x, 'tn', (S5_WIDTH, 2 * ln), (nt, 2, t // tk),
                     pl.BlockSpec((tk, ch), lambda i, j, l: (l, i)),
                     pl.BlockSpec((tk, ln), lambda i, j, l: (l, i + nt * j)),
                     pl.BlockSpec((ch, ln), lambda i, j, l: (i, j)))


def _full_spec(p):
    nd = p.ndim
    return pl.BlockSpec(p.shape, lambda i, nd=nd: (0,) * nd)


def _tok_fwd(name, fn, toks, params, outs, tm):
    t = toks[0].shape[0]
    nt, npar = len(toks), len(params)

    def body(*refs):
        tv = [r[...].astype(F32) for r in refs[:nt]]
        pv = [r[...].astype(F32) for r in refs[nt:nt + npar]]
        res = fn(*tv, *pv)
        for r, v in zip(refs[nt + npar:], res):
            r[...] = v.astype(r.dtype)

    return pl.pallas_call(
        body, name=name,
        out_shape=[jax.ShapeDtypeStruct((t, w), d) for w, d in outs],
        grid=(t // tm,),
        in_specs=[pl.BlockSpec((tm, a.shape[1]), lambda i: (i, 0)) for a in toks] + [_full_spec(p) for p in params],
        out_specs=[pl.BlockSpec((tm, w), lambda i: (i, 0)) for w, _ in outs],
        compiler_params=_params(("parallel",)),
    )(*toks, *params)


def _tok_bwd(name, fn, toks, params, cots, dtok, dpar, tm, acc_out=None, add_to=None):
    t = toks[0].shape[0]
    nt, npar = len(toks), len(params)
    cot_arrays = [c for c in cots if c is not None]
    ncot = len(cot_arrays)
    extra = [] if add_to is None else [add_to[1]]
    dtok_idx = [i for i, d in enumerate(dtok) if d is not None]
    dpar_idx = [i for i, d in enumerate(dpar) if d]

    def body(*refs):
        pos = 0
        tin = refs[pos:pos + nt]; pos += nt
        pin = refs[pos:pos + npar]; pos += npar
        cin = refs[pos:pos + ncot]; pos += ncot
        ein = refs[pos:pos + len(extra)]; pos += len(extra)
        dto = refs[pos:pos + len(dtok_idx)]; pos += len(dtok_idx)
        dpo = refs[pos:pos + len(dpar_idx)]; pos += len(dpar_idx)
        acc = refs[pos] if acc_out is not None else None
        first = pl.program_id(0) == 0

        tv = [r[...].astype(F32) for r in tin]
        pv = [r[...].astype(F32) for r in pin]
        res, vjp = jax.vjp(fn, *tv, *pv)
        cit = iter(cin)
        cs = tuple(jnp.ones_like(o) if c is None else next(cit)[...].astype(F32) for c, o in zip(cots, res))
        g = vjp(cs)
        for r, i in zip(dto, dtok_idx):
            v = g[i]
            if add_to is not None and add_to[0] == i:
                v = v + ein[0][...].astype(F32)
            r[...] = v.astype(r.dtype)

        @pl.when(first)
        def _():
            for r in dpo:
                r[...] = jnp.zeros_like(r)
            if acc is not None:
                acc[...] = jnp.zeros_like(acc)

        for r, i in zip(dpo, dpar_idx):
            r[...] += g[nt + i]
        if acc is not None:
            acc[...] += res[acc_out]

    out_shape = [jax.ShapeDtypeStruct(toks[i].shape, dtok[i]) for i in dtok_idx]
    out_shape += [jax.ShapeDtypeStruct(params[i].shape, F32) for i in dpar_idx]
    out_specs = [pl.BlockSpec((tm, toks[i].shape[1]), lambda i_: (i_, 0)) for i in dtok_idx]
    out_specs += [_full_spec(params[i]) for i in dpar_idx]
    if acc_out is not None:
        out_shape.append(jax.ShapeDtypeStruct((1, 1), F32))
        out_specs.append(pl.BlockSpec((1, 1), lambda i_: (0, 0)))
    tok_spec = lambda a: pl.BlockSpec((tm, a.shape[1]), lambda i_: (i_, 0))
    return pl.pallas_call(
        body, name=name,
        out_shape=out_shape,
        grid=(t // tm,),
        in_specs=[tok_spec(a) for a in toks] + [_full_spec(p) for p in params]
        + [tok_spec(c) for c in cot_arrays] + [tok_spec(e) for e in extra],
        out_specs=out_specs,
        compiler_params=_params(("arbitrary",)),
    )(*toks, *params, *cot_arrays, *extra)


def _small_fwd(name, fn, ins, outs):
    n = len(ins)

    def body(*refs):
        res = fn(*[r[...] for r in refs[:n]])
        for r, v in zip(refs[n:], res):
            r[...] = v.astype(r.dtype)

    return pl.pallas_call(
        body, name=name,
        out_shape=[jax.ShapeDtypeStruct(s, d) for s, d in outs],
        compiler_params=_params(),
    )(*ins)


def _small_bwd(name, fn, ins, cots):
    n = len(ins)

    def body(*refs):
        _, vjp = jax.vjp(fn, *[r[...] for r in refs[:n]])
        g = vjp(tuple(r[...] for r in refs[n:n + len(cots)]))
        for r, v in zip(refs[n + len(cots):], g):
            r[...] = v

    return pl.pallas_call(
        body, name=name,
        out_shape=[jax.ShapeDtypeStruct(a.shape, F32) for a in ins],
        compiler_params=_params(),
    )(*ins, *cots)


def _rms(x, g):
    return x * lax.rsqrt(jnp.mean(x * x, axis=-1, keepdims=True) + RMS_EPS) * g


def _f_norm_in(x, g):
    return (_rms(x, g),)


def _f_mix_res(x, mixed, g):
    h1 = x + mixed
    return h1, _rms(h1, g)


def _f_ffn_act(a13):
    a1, a3 = a13[:, :FFN_HIDDEN], a13[:, FFN_HIDDEN:]
    return (jax.nn.silu(a1) * a3,)


def _f_ffn_res(h1, ffo, g):
    h2 = h1 + ffo
    return h2, _rms(h2, g)


def _f_loss(h2, gpre, pu, target, g):
    h3 = h2 + jax.nn.sigmoid(gpre) * pu
    y = _rms(h3, g)
    err = jnp.square(y - target)
    return (0.5 * jnp.sum(jnp.mean(err, axis=-1, keepdims=True), axis=0, keepdims=True),)


def _f_s5_post(ypre, u, d, glu_w, glu_b):
    z = jax.nn.gelu(ypre + u * d)
    return (z * jax.nn.sigmoid(_bdot(z, glu_w) + glu_b),)


def _softplus(x):
    return jnp.maximum(x, 0.0) + jnp.log(1.0 + jnp.exp(-jnp.abs(x)))


def _f_rw_pre(z, carry, shift_mu, w0, w2, a0, a2, g2, k_k, k_a):
    rw = RWKV_WIDTH
    first_row = lax.broadcasted_iota(jnp.int32, z.shape, 0) == 0
    prev = jnp.where(first_row, carry, _shift_down(z))
    zs = z + (prev - z) * shift_mu
    o1, o2 = 3 * rw + DECAY_LORA, 3 * rw + DECAY_LORA + AAA_LORA
    r, k, v = zs[:, :rw], zs[:, rw:2 * rw], zs[:, 2 * rw:3 * rw]
    wl, al, gl = zs[:, 3 * rw:o1], zs[:, o1:o2], zs[:, o2:]
    w = -_softplus(-(w0 + _bdot(jnp.tanh(wl), w2))) - 0.5
    log_decay = -jnp.exp(w)
    a = jax.nn.sigmoid(a0 + _bdot(al, a2))
    g = _bdot(jax.nn.sigmoid(gl), g2)
    kk = k * k_k
    norm = jnp.sqrt(_head_sum(kk * kk))
    kk = kk / jnp.maximum(norm, L2_EPS)
    kp = k * (1.0 + (a - 1.0) * k_a)
    return r, log_decay, kp, v, -kk, kk * a, g


def _f_rw_post(y, r, kp, v, g, ln_w, ln_b, r_k):
    yc = y - _head_sum(y) * (1.0 / HEAD)
    var = _head_sum(yc * yc) * (1.0 / HEAD)
    yn = yc * lax.rsqrt(var + GN_EPS) * ln_w + ln_b
    bonus = _head_sum(r * kp * r_k) * v
    return ((yn + bonus) * g,)


def _f_s5_lam(lam_re, lam_im, log_step):
    step = jnp.exp(log_step)
    dr, di = lam_re * step, lam_im * step
    e = jnp.exp(dr)
    lbr, lbi = e * jnp.cos(di), e * jnp.sin(di)
    nr, ni = lbr - 1.0, lbi
    den = lam_re * lam_re + lam_im * lam_im
    return lbr, lbi, (nr * lam_re + ni * lam_im) / den, (ni * lam_re - nr * lam_im) / den


def _f_s5_build(coef_r, coef_i, btr, bti, ctr, cti):
    bbr = coef_r * btr - coef_i * bti
    bbi = coef_r * bti + coef_i * btr
    shape = (S5_WIDTH, S5_TILE_LANES)
    rows = (lax.broadcasted_iota(jnp.int32, shape, 0) % S5_TILE_CH) // S5_GROUP
    cols = lax.broadcasted_iota(jnp.int32, shape, 1) // S5_STATE
    mask = (rows == cols).astype(F32)

    def blocks(m):
        per_tile = [m[:, S5_TILE_LANES * i:S5_TILE_LANES * (i + 1)] for i in range(S5_TILES)]
        return jnp.concatenate([t for t in per_tile for _ in range(S5_TILE_GROUPS)], axis=0) * mask

    return (jnp.concatenate([blocks(bbr), blocks(bbi)], axis=1),
            jnp.concatenate([blocks(ctr), -blocks(cti)], axis=1))


HALO = 8


def _rw_pre_specs(z, params, tm, order):
    halo_blocks = tm // HALO
    return ([pl.BlockSpec((tm, z.shape[1]), lambda i: (order(i), 0)),
             pl.BlockSpec((HALO, z.shape[1]), lambda i: (jnp.maximum(order(i) * halo_blocks - 1, 0), 0))]
            + [_full_spec(p) for p in params])


def _rw_pre_fwd(z, params, tm):
    t = z.shape[0]
    npar = len(params)

    def body(z_ref, halo_ref, *refs):
        carry = jnp.where(pl.program_id(0) == 0, 0.0, halo_ref[pl.ds(HALO - 1, 1), :])
        res = _f_rw_pre(z_ref[...], carry, *[r[...].astype(F32) for r in refs[:npar]])
        for r, v in zip(refs[npar:], res):
            r[...] = v

    return pl.pallas_call(
        body, name="rw_pre",
        out_shape=[jax.ShapeDtypeStruct((t, RWKV_WIDTH), F32)] * 7,
        grid=(t // tm,),
        in_specs=_rw_pre_specs(z, params, tm, lambda i: i),
        out_specs=[pl.BlockSpec((tm, RWKV_WIDTH), lambda i: (i, 0))] * 7,
        compiler_params=_params(("parallel",)),
    )(z, z, *params)


def _rw_pre_bwd(z, params, cots, tm):
    t = z.shape[0]
    nt = t // tm
    npar = len(params)
    order = lambda i: nt - 1 - i
    flat_cots = [a for group in cots for a in group]
    ncot = len(flat_cots)

    def body(z_ref, halo_ref, *refs):
        pin, cin = refs[:npar], list(refs[npar:npar + ncot])
        dz_ref = refs[npar + ncot]
        dpo = refs[npar + ncot + 1:npar + ncot + 1 + npar]
        dcarry_ref = refs[npar + ncot + 1 + npar]
        i = pl.program_id(0)

        @pl.when(i == 0)
        def _():
            dcarry_ref[...] = jnp.zeros_like(dcarry_ref)
            for r in dpo:
                r[...] = jnp.zeros_like(r)

        carry = jnp.where(i == nt - 1, 0.0, halo_ref[pl.ds(HALO - 1, 1), :])
        _, vjp = jax.vjp(_f_rw_pre, z_ref[...], carry, *[r[...].astype(F32) for r in pin])
        g = vjp(tuple(sum(cin.pop(0)[...] for _ in group) for group in cots))
        last_row = lax.broadcasted_iota(jnp.int32, z_ref.shape, 0) == tm - 1
        dz_ref[...] = (g[0] + jnp.where(last_row, dcarry_ref[...], 0.0)).astype(dz_ref.dtype)
        dcarry_ref[...] = g[1]
        for r, v in zip(dpo, g[2:]):
            r[...] += v

    tok = lambda w: pl.BlockSpec((tm, w), lambda i: (order(i), 0))
    return pl.pallas_call(
        body, name="rw_pre_bwd",
        out_shape=[jax.ShapeDtypeStruct(z.shape, BF16)] + [jax.ShapeDtypeStruct(p.shape, F32) for p in params],
        grid=(nt,),
        in_specs=_rw_pre_specs(z, params, tm, order) + [tok(RWKV_WIDTH)] * ncot,
        out_specs=[tok(z.shape[1])] + [_full_spec(p) for p in params],
        scratch_shapes=[pltpu.VMEM((1, z.shape[1]), F32)],
        compiler_params=_params(("arbitrary",)),
    )(z, z, *params, *flat_cots)


def _s5_scan(bu, lam, tm, plan):
    t, w = bu.shape
    h = w // 2
    nt = t // tm
    parts, plan_in_specs, plan_out_shape, plan_out_specs, plan_sems = _carry(plan, 2, 1)

    def body(*refs):
        bu_ref, lam_ref = refs[:2]
        xb_ref = refs[2 + len(plan.ins)]
        carry_ref = refs[3 + len(plan.ins) + len(plan.out_shape)]
        x_ref, refs = refs[-1], refs[:-1]

        @pl.when(pl.program_id(0) == 0)
        def _():
            carry_ref[...] = jnp.zeros_like(carry_ref)
            plan.start(*parts(refs))

        lr, li = lam_ref[:, :h], lam_ref[:, h:]
        x_ref[...] = bu_ref[...].astype(F32)

        def step(s, c):
            cr, ci = c
            row = pl.ds(s, 1)
            nr = lr * cr - li * ci + x_ref[row, :h]
            ni = lr * ci + li * cr + x_ref[row, h:]
            x_ref[row, :h] = nr
            x_ref[row, h:] = ni
            return nr, ni

        cr, ci = lax.fori_loop(0, tm, step, (carry_ref[:, :h], carry_ref[:, h:]), unroll=SCAN_UNROLL)
        carry_ref[:, :h] = cr
        carry_ref[:, h:] = ci
        xb_ref[...] = x_ref[...].astype(BF16)

        @pl.when(pl.program_id(0) == nt - 1)
        def _():
            plan.wait(*parts(refs))

    spec = pl.BlockSpec((tm, w), lambda i: (i, 0))
    res = pl.pallas_call(
        body, name="s5_scan",
        out_shape=[jax.ShapeDtypeStruct((t, w), BF16)] + plan_out_shape,
        grid=(nt,),
        in_specs=[spec, pl.BlockSpec((1, w), lambda i: (0, 0))] + plan_in_specs,
        out_specs=[spec] + plan_out_specs,
        scratch_shapes=[pltpu.VMEM((1, w), F32)] + plan_sems + [pltpu.VMEM((tm, w), F32)],
        compiler_params=_params(("arbitrary",)),
    )(bu, lam, *plan.ins)
    return res[0], res[1:]


def _s5_scan_bwd(dx, xb, lam, tm):
    t, w = dx.shape
    h = w // 2
    nt = t // tm
    halo = BF16_SUBLANES

    rows8 = 8

    def body(dx_ref, xb_ref, halo_ref, lam_ref, dbu_out_ref, dlam_ref, carry_ref, dbu_ref, xp_ref):
        @pl.when(pl.program_id(0) == 0)
        def _():
            carry_ref[...] = jnp.zeros_like(carry_ref)
            dlam_ref[...] = jnp.zeros_like(dlam_ref)

        lr, li = lam_ref[:, :h], lam_ref[:, h:]
        dbu_ref[...] = dx_ref[...].astype(F32)

        def step(s, c):
            cr, ci = c
            row = pl.ds(tm - 1 - s, 1)
            nr = lr * cr + li * ci + dbu_ref[row, :h]
            ni = lr * ci - li * cr + dbu_ref[row, h:]
            dbu_ref[row, :h] = nr
            dbu_ref[row, h:] = ni
            return nr, ni

        cr, ci = lax.fori_loop(0, tm, step, (carry_ref[:, :h], carry_ref[:, h:]), unroll=SCAN_UNROLL)
        carry_ref[:, :h] = cr
        carry_ref[:, h:] = ci
        halo_rows = lax.broadcasted_iota(jnp.int32, (halo, w), 0)
        before = jnp.sum(jnp.where(halo_rows == halo - 1, halo_ref[...].astype(F32), 0.0), axis=0, keepdims=True)
        before = jnp.where(pl.program_id(0) == nt - 1, 0.0, before)
        first_row = lax.broadcasted_iota(jnp.int32, (tm, w), 0) == 0
        xp_ref[...] = jnp.where(first_row, before, pltpu.roll(xb_ref[...].astype(F32), 1, 0))

        def accumulate(s, acc):
            ar, ai = acc
            rows = pl.ds(pl.multiple_of(s * rows8, rows8), rows8)
            gr, gi = dbu_ref[rows, :h], dbu_ref[rows, h:]
            pr, pi_ = xp_ref[rows, :h], xp_ref[rows, h:]
            return ar + (gr * pr + gi * pi_), ai + (gi * pr - gr * pi_)

        zero = jnp.zeros((rows8, h), F32)
        ar, ai = lax.fori_loop(0, tm // rows8, accumulate, (zero, zero))
        dlam_ref[:, :h] += jnp.sum(ar, axis=0, keepdims=True)
        dlam_ref[:, h:] += jnp.sum(ai, axis=0, keepdims=True)
        dbu_out_ref[...] = dbu_ref[...].astype(BF16)

    spec = pl.BlockSpec((tm, w), lambda i: (nt - 1 - i, 0))
    halo_spec = pl.BlockSpec((halo, w), lambda i: (jnp.maximum((nt - 1 - i) * (tm // halo) - 1, 0), 0))
    row_spec = pl.BlockSpec((1, w), lambda i: (0, 0))
    return pl.pallas_call(
        body, name="s5_scan_bwd",
        out_shape=[jax.ShapeDtypeStruct((t, w), BF16), jax.ShapeDtypeStruct((1, w), F32)],
        grid=(nt,),
        in_specs=[spec, spec, halo_spec, row_spec],
        out_specs=[spec, row_spec],
        scratch_shapes=[pltpu.VMEM((1, w), F32), pltpu.VMEM((tm, w), F32), pltpu.VMEM((tm, w), F32)],
        compiler_params=_params(("arbitrary",)),
    )(dx, xb, xb, lam)


def _unit_lower_inverses_impl(ns):
    c = ns[0].shape[0]
    eye = (lax.broadcasted_iota(jnp.int32, (c, c), 0) == lax.broadcasted_iota(jnp.int32, (c, c), 1)).astype(F32)
    inv = [eye + n for n in ns]
    pw = [_dg(n, n, _NN, WKV_PASSES) for n in ns]
    for _ in range(int(math.log2(c)) - 2):
        both = [_dg(jnp.concatenate([i, q], axis=0), q, _NN, WKV_PASSES) for i, q in zip(inv, pw)]
        inv = [i + q[:c] for i, q in zip(inv, both)]
        pw = [q[c:] for q in both]
    return tuple(i + _dg(i, q, _NN, WKV_PASSES) for i, q in zip(inv, pw))


@jax.custom_vjp
def _unit_lower_inverses(ns):
    return _unit_lower_inverses_impl(ns)


def _unit_lower_inverses_fwd(ns):
    inv = _unit_lower_inverses_impl(ns)
    return inv, inv


def _unit_lower_inverses_bwd(inv, g):
    left = [_dg(i, gi, _TN, WKV_PASSES) for i, gi in zip(inv, g)]
    return (tuple(_dg(q, i, _NT, WKV_PASSES) for q, i in zip(left, inv)),)


_unit_lower_inverses.defvjp(_unit_lower_inverses_fwd, _unit_lower_inverses_bwd)


def _wkv_chunks(s0, r, lw, k, v, a, b):
    c = r[0].shape[0]
    row = lax.broadcasted_iota(jnp.int32, (c, c), 0)
    col = lax.broadcasted_iota(jnp.int32, (c, c), 1)
    incl, strict = col <= row, col < row
    tri = incl.astype(F32)
    each = lambda f, *xs: [f(*t) for t in zip(*xs)]
    stack = lambda p, q: jnp.concatenate([p, q], axis=0)
    dot = lambda p, q, dims=_NN: _dot(p, q, dims, WKV_PASSES)
    lc = each(lambda l: _dot(tri, l, _NN, 2), lw)
    e_neg = each(lambda l: jnp.exp(-l), lc)
    ar = each(lambda x, z, l, w: stack(x * jnp.exp(l - w), z * jnp.exp(l)), a, r, lc, lw)
    bk = each(lambda x, z, e: stack(x * e, z * e), b, k, e_neg)
    m = each(lambda p, q: dot(p, q, _NT), ar, bk)
    mab = each(lambda q: jnp.where(strict, q[:c, :c], 0.0), m)
    mak_mrk = each(lambda q: stack(jnp.where(strict, q[:c, c:], 0.0), jnp.where(incl, q[c:, c:], 0.0)), m)
    mrb = each(lambda q: jnp.where(incl, q[c:, :c], 0.0), m)
    xy = each(lambda p, s, q, z: dot(p, s, _NT) + dot(q, z), ar, s0, mak_mrk, v)
    inv = _unit_lower_inverses(tuple(mab))
    u = each(lambda i, q: dot(i, q[:c]), inv, xy)
    y = each(lambda q, z, p: q[c:] + dot(z, p), xy, mrb, u)
    e_tot = each(lambda l: jnp.exp(jnp.sum(l, axis=0, keepdims=True)), lw)
    s1 = each(lambda s, p, z, q, e: (s + dot(stack(p, z), q, _TN)) * e, s0, u, v, bk, e_tot)
    return y, s1


def _carry(plan, n_args, n_outs):
    n_in, n_out = len(plan.ins), len(plan.out_shape)

    def parts(refs):
        base = n_args + n_in + n_outs
        return refs[n_args:n_args + n_in], refs[base:base + n_out], refs[base + n_out + 1:]

    return parts, [HBM_SPEC] * n_in, list(plan.out_shape), [HBM_SPEC] * n_out, list(plan.sems)


def _head_cols(ref):
    return tuple(ref[:, h * HEAD:(h + 1) * HEAD] for h in range(HEADS))


def _wkv_fwd(seqs, plan):
    t, w = seqs[0].shape
    c, n = WKV_CHUNK, HEAD
    nc = t // c
    parts, plan_in_specs, plan_out_shape, plan_out_specs, plan_sems = _carry(plan, 6, 2)

    def body(*refs):
        ins, (y_ref, ck_ref) = refs[:6], refs[6 + len(plan.ins):8 + len(plan.ins)]
        s_ref = refs[8 + len(plan.ins) + len(plan.out_shape)]

        @pl.when(pl.program_id(0) == 0)
        def _():
            s_ref[...] = jnp.zeros_like(s_ref)
            plan.start(*parts(refs))

        s0 = tuple(s_ref[h] for h in range(HEADS))
        ys, s1 = _wkv_chunks(s0, *[_head_cols(r) for r in ins])
        for h in range(HEADS):
            ck_ref[0, h] = s0[h]
            y_ref[:, h * n:(h + 1) * n] = ys[h]
            s_ref[h] = s1[h]

        @pl.when(pl.program_id(0) == nc - 1)
        def _():
            plan.wait(*parts(refs))

    spec = pl.BlockSpec((c, w), lambda i: (i, 0))
    res = pl.pallas_call(
        body, name="wkv_fwd",
        out_shape=[jax.ShapeDtypeStruct((t, w), F32), jax.ShapeDtypeStruct((nc, HEADS, n, n), F32)] + plan_out_shape,
        grid=(nc,),
        in_specs=[spec] * 6 + plan_in_specs,
        out_specs=[spec, pl.BlockSpec((1, HEADS, n, n), lambda i: (i, 0, 0, 0))] + plan_out_specs,
        scratch_shapes=[pltpu.VMEM((HEADS, n, n), F32)] + plan_sems,
        compiler_params=_params(("arbitrary",)),
    )(*seqs, *plan.ins)
    return res[0], res[1], res[2:]


def _wkv_bwd(seqs, ck, dy, plan):
    t, w = seqs[0].shape
    c, n = WKV_CHUNK, HEAD
    nc = t // c
    parts, plan_in_specs, plan_out_shape, plan_out_specs, plan_sems = _carry(plan, 8, 6)

    def body(*refs):
        ins, ck_ref, dy_ref = refs[:6], refs[6], refs[7]
        outs = refs[8 + len(plan.ins):14 + len(plan.ins)]
        ds_ref = refs[14 + len(plan.ins) + len(plan.out_shape)]

        @pl.when(pl.program_id(0) == 0)
        def _():
            ds_ref[...] = jnp.zeros_like(ds_ref)
            plan.start(*parts(refs))

        s0 = tuple(ck_ref[0, h] for h in range(HEADS))
        _, vjp = jax.vjp(_wkv_chunks, s0, *[_head_cols(r) for r in ins])
        g = vjp((list(_head_cols(dy_ref)), [ds_ref[h] for h in range(HEADS)]))
        for h in range(HEADS):
            ds_ref[h] = g[0][h]
            for o, d in zip(outs, g[1:]):
                o[:, h * n:(h + 1) * n] = d[h]

        @pl.when(pl.program_id(0) == nc - 1)
        def _():
            plan.wait(*parts(refs))

    spec = pl.BlockSpec((c, w), lambda i: (nc - 1 - i, 0))
    res = pl.pallas_call(
        body, name="wkv_bwd",
        out_shape=[jax.ShapeDtypeStruct((t, w), F32)] * 6 + plan_out_shape,
        grid=(nc,),
        in_specs=[spec] * 6 + [pl.BlockSpec((1, HEADS, n, n), lambda i: (nc - 1 - i, 0, 0, 0)), spec] + plan_in_specs,
        out_specs=[spec] * 6 + plan_out_specs,
        scratch_shapes=[pltpu.VMEM((HEADS, n, n), F32)] + plan_sems,
        compiler_params=_params(("arbitrary",)),
    )(*seqs, ck, dy, *plan.ins)
    return res[:6], res[6:]


def _coords():
    return lax.axis_index("x"), lax.axis_index("y"), lax.axis_index("c")


def _flip(v, f):
    return 1 - v if f else v


_CHIP_FLIPS = [(1, 0), (0, 1), (1, 1)]
_DEV_FLIPS = [(fx, fy, fc) for fx in (0, 1) for fy in (0, 1) for fc in (0, 1) if (fx, fy, fc) != (0, 0, 0)]
HBM_SPEC = pl.BlockSpec(memory_space=pl.ANY)


def _chip_peer(k, x, y):
    fx, fy = _CHIP_FLIPS[k]
    return _flip(x, fx), _flip(y, fy)


def _dev_peer(k, x, y, c):
    fx, fy, fc = _DEV_FLIPS[k]
    return _flip(x, fx), _flip(y, fy), _flip(c, fc)


def _rows_of_core(ref, core):
    h = ref.shape[-2] // 2
    rows = pl.ds(pl.multiple_of(core * h, 8), h)
    return ref.at[rows, :] if len(ref.shape) == 2 else ref.at[:, rows, :]


class _Plan(NamedTuple):
    ins: Sequence[Any]
    out_shape: Sequence[Any]
    sems: Sequence[Any]
    start: Callable
    wait: Callable


_NO_PLAN = _Plan([], [], [], lambda *_: None, lambda *_: None)


def _run_plan(name, plan):
    n_in, n_out = len(plan.ins), len(plan.out_shape)

    def body(*refs):
        parts = refs[:n_in], refs[n_in:n_in + n_out], refs[n_in + n_out:]
        plan.start(*parts)
        plan.wait(*parts)

    return pl.pallas_call(
        body, name=name, out_shape=list(plan.out_shape),
        in_specs=[HBM_SPEC] * n_in, out_specs=[HBM_SPEC] * n_out, scratch_shapes=list(plan.sems),
    )(*plan.ins)


def _gather_plan(shards):
    n = len(shards)

    def copies(srcs, outs, sems):
        send_sems, recv_sems, local_sems = sems
        x, y, c = _coords()
        me = 2 * x + y

        def remote(i, k, arriving):
            px, py = _chip_peer(k, x, y)
            return pltpu.make_async_remote_copy(
                src_ref=srcs[i], dst_ref=outs[i].at[2 * px + py if arriving else me],
                send_sem=send_sems.at[i, k], recv_sem=recv_sems.at[i, k],
                device_id=(px, py, c), device_id_type=MESH)

        own = [pltpu.make_async_copy(srcs[i], outs[i].at[me], local_sems.at[i]) for i in range(n)]
        pairs = [(i, k) for k in range(3) for i in range(n)]
        return own, [remote(i, k, False) for i, k in pairs], [remote(i, k, True) for i, k in pairs]

    return _Plan(
        ins=shards, out_shape=[jax.ShapeDtypeStruct((4,) + s.shape, s.dtype) for s in shards],
        sems=[pltpu.SemaphoreType.DMA((n, 3)), pltpu.SemaphoreType.DMA((n, 3)), pltpu.SemaphoreType.DMA((n,))],
        start=functools.partial(_start_copies, copies), wait=functools.partial(_wait_copies, copies))


def _start_copies(copies, ins, outs, sems):
    own, sends, _ = copies(ins, outs, sems)
    for cp in own + sends:
        cp.start()


def _wait_copies(copies, ins, outs, sems):
    own, sends, arrivals = copies(ins, outs, sems)
    for cp in arrivals:
        cp.wait_recv()
    for cp in sends:
        cp.wait_send()
    for cp in own:
        cp.wait()


def _exchange_plan(gs, small=None):
    n = len(gs)
    arrays = list(gs) + ([] if small is None else [small])

    def copies(srcs, outs, sems):
        send_sems, recv_sems, local_sems = sems
        x, y, c = _coords()
        me = 4 * x + 2 * y + c

        def piece(i, px, py, pc):
            if i == n:
                return srcs[i].at[4 * px + 2 * py + pc]
            return _rows_of_core(srcs[i].at[2 * px + py], pc)

        def remote(i, k, arriving):
            px, py, pc = _dev_peer(k, x, y, c)
            return pltpu.make_async_remote_copy(
                src_ref=piece(i, px, py, pc), dst_ref=outs[i].at[4 * px + 2 * py + pc if arriving else me],
                send_sem=send_sems.at[i, k], recv_sem=recv_sems.at[i, k],
                device_id=(px, py, pc), device_id_type=MESH)

        own = [pltpu.make_async_copy(piece(i, x, y, c), outs[i].at[me], local_sems.at[i]) for i in range(len(arrays))]
        pairs = [(i, k) for k in range(7) for i in range(len(arrays))]
        return own, [remote(i, k, False) for i, k in pairs], [remote(i, k, True) for i, k in pairs]

    out_shape = [jax.ShapeDtypeStruct((8, g.shape[1] // 2, g.shape[2]), g.dtype) for g in gs]
    if small is not None:
        out_shape.append(jax.ShapeDtypeStruct(small.shape, small.dtype))
    m = len(arrays)
    return _Plan(
        ins=arrays, out_shape=out_shape,
        sems=[pltpu.SemaphoreType.DMA((m, 7)), pltpu.SemaphoreType.DMA((m, 7)), pltpu.SemaphoreType.DMA((m,))],
        start=functools.partial(_start_copies, copies), wait=functools.partial(_wait_copies, copies))


def _share_cores(halves, small):
    n = len(halves)

    def body(*refs):
        srcs, small_src, outs, small_out = refs[:n], refs[n], refs[n + 1:2 * n + 1], refs[2 * n + 1]
        mine, theirs = refs[2 * n + 2:3 * n + 2], refs[3 * n + 2:4 * n + 2]
        send_sems, recv_sems, ssend, srecv, local_sems = refs[4 * n + 2:]
        x, y, c = _coords()
        me = 4 * x + 2 * y + c

        def big(i):
            return pltpu.make_async_remote_copy(
                src_ref=mine[i], dst_ref=theirs[i], send_sem=send_sems.at[i], recv_sem=recv_sems.at[i],
                device_id=(x, y, 1 - c), device_id_type=MESH)

        def tiny(k, arriving):
            px, py, pc = _dev_peer(k, x, y, c)
            return pltpu.make_async_remote_copy(
                src_ref=small_src, dst_ref=small_out.at[4 * px + 2 * py + pc if arriving else me],
                send_sem=ssend.at[k], recv_sem=srecv.at[k], device_id=(px, py, pc), device_id_type=MESH)

        small_sends = [tiny(k, False) for k in range(7)]
        own_small = pltpu.make_async_copy(small_src, small_out.at[me], local_sems.at[2 * n])
        stage = [pltpu.make_async_copy(srcs[i], mine[i], local_sems.at[i]) for i in range(n)]
        for cp in small_sends + [own_small] + stage:
            cp.start()
        sends = []
        for i in range(n):
            stage[i].wait()
            sends.append(big(i))
            sends[-1].start()
        store = [pltpu.make_async_copy(mine[i], outs[i].at[c], local_sems.at[i]) for i in range(n)]
        for cp in store:
            cp.start()
        for i in range(n):
            big(i).wait_recv()
            store.append(pltpu.make_async_copy(theirs[i], outs[i].at[1 - c], local_sems.at[n + i]))
            store[-1].start()
        for k in range(7):
            tiny(k, True).wait_recv()
        for cp in sends + small_sends:
            cp.wait_send()
        for cp in store + [own_small]:
            cp.wait()

    staged = [pltpu.VMEM(s.shape, s.dtype) for s in halves]
    res = pl.pallas_call(
        body, name="share_cores",
        out_shape=[jax.ShapeDtypeStruct((2,) + s.shape, s.dtype) for s in halves]
        + [jax.ShapeDtypeStruct((8,) + small.shape, small.dtype)],
        in_specs=[HBM_SPEC] * (n + 1), out_specs=[HBM_SPEC] * (n + 1),
        scratch_shapes=staged + staged + [
            pltpu.SemaphoreType.DMA((n,)), pltpu.SemaphoreType.DMA((n,)),
            pltpu.SemaphoreType.DMA((7,)), pltpu.SemaphoreType.DMA((7,)),
            pltpu.SemaphoreType.DMA((2 * n + 1,))],
        compiler_params=pltpu.CompilerParams(vmem_limit_bytes=VMEM_LIMIT_BYTES),
    )(*halves, small)
    return res[:n], res[n]


BF16_SUBLANES = 16


def _row_tile(n, target, step=BF16_SUBLANES):
    return max([d for d in range(step, min(n, target) + 1, step) if n % d == 0] or [n])


def _ew(name, fn, ins, outs, block_bytes=2 << 20):
    rows, cols = ins[0].shape[-2:]
    lead = max(math.prod(a.shape[:-2]) for a in ins)
    tr = _row_tile(rows, max(8, block_bytes // (4 * cols * lead)))
    n = len(ins)

    def spec(shape):
        if len(shape) == 2:
            return pl.BlockSpec((tr, cols), lambda i: (i, 0))
        return pl.BlockSpec((shape[0], tr, cols), lambda i: (0, i, 0))

    def body(*refs):
        res = fn(*[r[...] for r in refs[:n]])
        for r, v in zip(refs[n:], res):
            r[...] = v

    return pl.pallas_call(
        body, name=name,
        out_shape=[jax.ShapeDtypeStruct(s, F32) for s in outs],
        grid=(rows // tr,),
        in_specs=[spec(a.shape) for a in ins],
        out_specs=[spec(s) for s in outs],
        compiler_params=_params(("parallel",)),
    )(*ins)


def _sum_slots(a):
    total = a[0].astype(F32)
    for s in range(1, a.shape[0]):
        total = total + a[s].astype(F32)
    return (total,)


def _adamw(g, w, m, v):
    bc1 = 1.0 - ADAM_B1 ** ADAM_STEP
    bc2 = 1.0 - ADAM_B2 ** ADAM_STEP
    m_new = ADAM_B1 * m + (1.0 - ADAM_B1) * g
    v_new = ADAM_B2 * v + (1.0 - ADAM_B2) * jnp.square(g)
    delta = -ADAM_LR * ((m_new / bc1) / (jnp.sqrt(v_new / bc2) + ADAM_EPS) + ADAM_WD * w)
    return delta, m_new, v_new


def _mat(a):
    return a.reshape(a.shape[-2:])


def _to_shard_major(full, axis):
    rows, cols = full.shape
    if axis == 0:
        return full.reshape(4, rows // 4, cols)
    return full.reshape(rows, 4, cols // 4).transpose(1, 0, 2)


def _from_shard_major(a, axis):
    _, r, cs = a.shape
    if axis == 0:
        return a.reshape(4 * r, cs)
    return a.transpose(1, 0, 2).reshape(r, 4 * cs)


def _pack_small(arrays, tail=None):
    flat = [arrays[n].reshape(-1) for n in SMALL_NAMES] + ([] if tail is None else [tail.reshape(1)])
    used = sum(a.shape[0] for a in flat)
    flat.append(jnp.zeros((8 * SMALL_ROWS * PACK_COLS - used,), F32))
    return jnp.concatenate(flat).reshape(8, SMALL_ROWS, PACK_COLS)


def _unpack_small(packed, shapes):
    flat = packed.reshape(-1)
    out, off = {}, 0
    for n in SMALL_NAMES:
        size = math.prod(shapes[n])
        out[n] = flat[off:off + size].reshape(shapes[n])
        off += size
    return out


def _row(a):
    return a.reshape(1, -1)


def _local_step(x, p, target, wf, ws, late_shards):
    wf = dict(wf)
    t = x.shape[0]
    tm = min(256, t)
    tw = min(512, t)
    tx = min(1024, t)
    g = {}

    lam_re, lam_im = ws['s5_lam_re'].reshape(S5_GROUPS, S5_STATE), ws['s5_lam_im'].reshape(S5_GROUPS, S5_STATE)
    log_step = ws['s5_log_step'].reshape(S5_GROUPS, 1)
    gp = (S5_GROUPS, S5_STATE)
    lam_ins = (lam_re, lam_im, log_step)
    lbr, lbi, cfr, cfi = _small_fwd("s5_lam", _f_s5_lam, lam_ins, [(gp, F32)] * 4)
    lam_row = jnp.concatenate([_row(lbr), _row(lbi)], axis=1)
    to_t = lambda a, perm: a.reshape((S5_GROUPS,) + a.shape[-2:]).transpose(perm).reshape(S5_GROUP, S5_LANES)
    build_ins = (_row(cfr), _row(cfi), to_t(ws['s5_b_re'], (2, 0, 1)), to_t(ws['s5_b_im'], (2, 0, 1)),
                 to_t(ws['s5_c_re'], (1, 0, 2)), to_t(ws['s5_c_im'], (1, 0, 2)))
    block_shape = (S5_WIDTH, 2 * S5_TILE_LANES)
    b_blk, c_blk = _small_fwd("s5_build", _f_s5_build, build_ins, [(block_shape, F32)] * 2)

    norm_mix, norm_ffn, norm_ple = _row(ws['norm_mix']), _row(ws['norm_ffn']), _row(ws['norm_ple'])
    final_norm = _row(ws['final_norm'])
    (xn,) = _tok_fwd("norm_in", _f_norm_in, [x], [norm_mix], [(x.shape[1], BF16)], tx)
    u = _mm("proj_s5", xn, wf['w_in'][:, :S5_WIDTH], 'nn')
    z = _mm("proj_rw", xn, wf['w_in'][:, S5_WIDTH:], 'nn')

    bu = _s5_expand("s5_bu", u, b_blk)
    def late_plan(carrier):
        return _gather_plan([late_shards[n] for n in LATE_GATHER[carrier]])

    def arrived(carrier, got):
        wf.update({n: _from_shard_major(a, SHARDED[n]) for n, a in zip(LATE_GATHER[carrier], got)})

    xs, got = _s5_scan(bu, lam_row, tm, late_plan('s5_scan'))
    arrived('s5_scan', got)
    ypre = _s5_contract("s5_y", xs, c_blk)
    s5_par = [_row(ws['s5_d']), wf['s5_glu_w'], _row(ws['s5_glu_b'])]
    (s5_out,) = _tok_fwd("s5_post", _f_s5_post, [ypre, u], s5_par, [(S5_WIDTH, BF16)], tx)

    pre_par = [_row(ws['rw_shift_mu']), _row(ws['rw_w0']), wf['rw_w2'], _row(ws['rw_a0']), wf['rw_a2'],
               wf['rw_g2'], _row(ws['rw_k_k']), _row(ws['rw_k_a'])]
    r, lw, kp, v, an, bn, gate = _rw_pre_fwd(z, pre_par, tw)
    seqs = [r, lw, kp, v, an, bn]
    y_wkv, ck, got = _wkv_fwd(seqs, late_plan('wkv_fwd'))
    arrived('wkv_fwd', got)
    post_par = [_row(ws['rw_ln_w']), _row(ws['rw_ln_b']), _row(ws['rw_r_k'])]
    post_toks = [y_wkv, r, kp, v, gate]
    (rw_out,) = _tok_fwd("rw_post", _f_rw_post, post_toks, post_par, [(RWKV_WIDTH, BF16)], tx)

    mixcat = jnp.concatenate([s5_out, rw_out], axis=1)
    mixed = _mm("mix_out", mixcat, wf['w_out'], 'nn')
    h1, hn = _tok_fwd("mix_res", _f_mix_res, [x, mixed], [norm_ffn], [(x.shape[1], F32), (x.shape[1], BF16)], tx)
    w13 = jnp.concatenate([wf['ffn_w1'], wf['ffn_w3']], axis=1)
    a13, got = _mm("ffn_up", hn, w13, 'nn', out_dtype=BF16, tn=FFN_TILE, plan=late_plan('ffn_up'))
    arrived('ffn_up', got)
    (f,) = _tok_fwd("ffn_act", _f_ffn_act, [a13], [], [(FFN_HIDDEN, BF16)], tw)
    ffo = _mm("ffn_down", f, wf['ffn_w2'], 'nn')
    h2, hp = _tok_fwd("ffn_res", _f_ffn_res, [h1, ffo], [norm_ple], [(x.shape[1], F32), (x.shape[1], BF16)], tx)
    gpre = _mm("ple_gate", hp, wf['ple_gate_w'], 'nn')
    pu = _mm("ple_up", p, wf['ple_up_w'], 'nn')

    dh2, dgpre, dpu, g['final_norm'], loss = _tok_bwd(
        "loss", _f_loss, [h2, gpre, pu, target], [final_norm], [None],
        [F32, BF16, BF16, None], [True], tw, acc_out=0)
    g['ple_gate_w'] = _mm("d_ple_gate_w", hp, dgpre, 'tn', out_dtype=WIRE)
    g['ple_up_w'] = _mm("d_ple_up_w", p, dpu, 'tn', out_dtype=WIRE)
    dhp = _mm("d_hp", dgpre, wf['ple_gate_w'], 'nt')
    dh1, dffo, g['norm_ple'] = _tok_bwd("ffn_res_bwd", _f_ffn_res, [h1, ffo], [norm_ple], [dh2, dhp],
                                        [F32, BF16], [True], tw)
    g['ffn_w2'] = _mm("d_ffn_w2", f, dffo, 'tn', out_dtype=WIRE, tm=FFN_TILE)
    df = _mm("d_f", dffo, wf['ffn_w2'], 'nt', out_dtype=BF16, tn=FFN_TILE)
    (da13,) = _tok_bwd("ffn_act_bwd", _f_ffn_act, [a13], [], [df], [BF16], [], tw)
    dw13 = _mm("d_ffn_w13", hn, da13, 'tn', out_dtype=WIRE, tn=FFN_TILE)
    dw13 = dw13.reshape(dw13.shape[0], 8, FFN_HIDDEN // 4).transpose(1, 0, 2)
    shard_major = {'ffn_w1': dw13[:4], 'ffn_w3': dw13[4:]}
    dhn = _mm("d_hn", da13, w13, 'nt')
    dx_a, dmixed, g['norm_ffn'] = _tok_bwd("mix_res_bwd", _f_mix_res, [x, mixed], [norm_ffn], [dh1, dhn],
                                           [F32, BF16], [True], tw)
    g['w_out'] = _mm("d_w_out", mixcat, dmixed, 'tn', out_dtype=WIRE)
    ds5_out = _mm("d_s5_out", dmixed, wf['w_out'][:S5_WIDTH], 'nt')
    drw_out = _mm("d_rw_out", dmixed, wf['w_out'][S5_WIDTH:], 'nt')

    dy_wkv, dr_b, dkp_b, dv_b, dgate, g['rw_ln_w'], g['rw_ln_b'], g['rw_r_k'] = _tok_bwd(
        "rw_post_bwd", _f_rw_post, post_toks, post_par, [drw_out], [F32] * 5, [True] * 3, tw)
    late_exchange = _exchange_plan([shard_major[n] if n in shard_major else
                                    _to_shard_major(g[n], SHARDED[n]).astype(WIRE) for n in LATE_NAMES])
    dseqs, late_pieces = _wkv_bwd(seqs, ck, dy_wkv, late_exchange)
    pre_cots = [(dseqs[0], dr_b), (dseqs[1],), (dseqs[2], dkp_b), (dseqs[3], dv_b), (dseqs[4],), (dseqs[5],),
                (dgate,)]
    dz, *dpre = _rw_pre_bwd(z, pre_par, pre_cots, tw)
    for n, d in zip(['rw_shift_mu', 'rw_w0', 'rw_w2', 'rw_a0', 'rw_a2', 'rw_g2', 'rw_k_k', 'rw_k_a'], dpre):
        g[n] = d

    dypre, du_a, g['s5_d'], g['s5_glu_w'], g['s5_glu_b'] = _tok_bwd(
        "s5_post_bwd", _f_s5_post, [ypre, u], s5_par, [ds5_out], [F32, F32], [True] * 3, tw)
    dxs = _s5_expand("d_s5_x", dypre, c_blk)
    dc_blk = _s5_block_grad("d_s5_c", dypre, xs)
    dbu, dlam_row = _s5_scan_bwd(dxs, xs, lam_row, tm)
    du = _s5_contract("d_s5_u", dbu, b_blk, add=du_a, out_dtype=BF16)
    db_blk = _s5_block_grad("d_s5_b", u, dbu)
    dbuild = _small_bwd("s5_build_bwd", _f_s5_build, build_ins, (db_blk, dc_blk))
    lam_cots = (dlam_row[:, :S5_LANES].reshape(gp), dlam_row[:, S5_LANES:].reshape(gp),
                dbuild[0].reshape(gp), dbuild[1].reshape(gp))
    g['s5_lam_re'], g['s5_lam_im'], g['s5_log_step'] = _small_bwd("s5_lam_bwd", _f_s5_lam, lam_ins, lam_cots)
    from_t = lambda a, perm: a.reshape(S5_GROUP, S5_GROUPS, S5_STATE).transpose(perm)
    g['s5_b_re'], g['s5_b_im'] = from_t(dbuild[2], (1, 2, 0)), from_t(dbuild[3], (1, 2, 0))
    g['s5_c_re'], g['s5_c_im'] = from_t(dbuild[4], (1, 0, 2)), from_t(dbuild[5], (1, 0, 2))

    dproj = jnp.concatenate([du, dz], axis=1)
    dw_in = _mm("d_w_in", xn, dproj, 'tn', out_dtype=WIRE)
    w_in_exchange = _exchange_plan([_to_shard_major(dw_in, SHARDED['w_in'])])
    dxn, (w_in_pieces,) = _mm("d_xn", dproj, wf['w_in'], 'nt', plan=w_in_exchange)
    pieces = dict(zip(LATE_NAMES, late_pieces), w_in=w_in_pieces)
    grad_x, g['norm_mix'] = _tok_bwd("norm_in_bwd", _f_norm_in, [x], [norm_mix], [dxn], [F32], [True], tw,
                                     add_to=(0, dx_a))
    return loss[0, 0], grad_x, g, pieces


def _step(x, p, target, w, m, v):
    shards = {n: _mat(w[n]).astype(BF16) for n in SHARDED_NAMES}
    early = _run_plan("gather_early", _gather_plan([shards[n] for n in EARLY_NAMES]))
    wf = {n: _from_shard_major(a, SHARDED[n]) for n, a in zip(EARLY_NAMES, early)}
    ws = {n: w[n] for n in SMALL_NAMES}

    loss, grad_x, g, pieces = _local_step(x[0], p[0, 0], target[0], wf, ws, shards)

    last_names = [n for n in SHARDED_NAMES if n not in pieces]
    last_plan = _exchange_plan([_to_shard_major(g[n], SHARDED[n]).astype(WIRE) for n in last_names],
                               _pack_small({n: g[n] for n in SMALL_NAMES}, tail=loss))
    *last_pieces, by_dev = _run_plan("exchange_last", last_plan)
    pieces.update(zip(last_names, last_pieces))
    halves = [_ew("add_devices_" + n, _sum_slots, [pieces[n]], [pieces[n].shape[1:]])[0] for n in SHARDED_NAMES]
    (small_piece,) = _ew("add_devices_small", _sum_slots, [by_dev], [by_dev.shape[1:]])
    both, small_g = _share_cores(halves, small_piece)

    kinds = [{}, {}, {}, {}]
    for n, gn in zip(SHARDED_NAMES, both):
        shard = _mat(w[n]).shape
        res = _ew("adamw_" + n, _adamw, [gn.reshape(shard), _mat(w[n]), _mat(m[n]), _mat(v[n])], [shard] * 3)
        for kind, a in zip(kinds, [gn] + list(res)):
            kind[n] = a.reshape(w[n].shape)
    flat = (8 * SMALL_ROWS, PACK_COLS)
    packed = [_pack_small({n: d[n] for n in SMALL_NAMES}).reshape(flat) for d in (w, m, v)]
    small_res = _ew("adamw_small", _adamw, [small_g.reshape(flat)] + packed, [flat] * 3)
    small_shapes = {n: w[n].shape for n in SMALL_NAMES}
    for kind, a in zip(kinds, [small_g] + list(small_res)):
        kind.update(_unpack_small(a, small_shapes))
    total = small_g.reshape(-1)[sum(math.prod(s) for s in small_shapes.values())]
    return (total, grad_x[None], *[kind[n] for kind in kinds for n in WEIGHT_NAMES])


def kernel(x, p, norm_mix, w_in, s5_lam_re, s5_lam_im, s5_log_step, s5_b_re, s5_b_im, s5_c_re, s5_c_im, s5_d, s5_glu_w, s5_glu_b, rw_shift_mu, rw_w0, rw_w2, rw_a0, rw_a2, rw_g2, rw_k_k, rw_k_a, rw_r_k, rw_ln_w, rw_ln_b, w_out, norm_ffn, ffn_w1, ffn_w3, ffn_w2, norm_ple, ple_gate_w, ple_up_w, final_norm, loss_target, m_norm_mix, m_w_in, m_s5_lam_re, m_s5_lam_im, m_s5_log_step, m_s5_b_re, m_s5_b_im, m_s5_c_re, m_s5_c_im, m_s5_d, m_s5_glu_w, m_s5_glu_b, m_rw_shift_mu, m_rw_w0, m_rw_w2, m_rw_a0, m_rw_a2, m_rw_g2, m_rw_k_k, m_rw_k_a, m_rw_r_k, m_rw_ln_w, m_rw_ln_b, m_w_out, m_norm_ffn, m_ffn_w1, m_ffn_w3, m_ffn_w2, m_norm_ple, m_ple_gate_w, m_ple_up_w, m_final_norm, v_norm_mix, v_w_in, v_s5_lam_re, v_s5_lam_im, v_s5_log_step, v_s5_b_re, v_s5_b_im, v_s5_c_re, v_s5_c_im, v_s5_d, v_s5_glu_w, v_s5_glu_b, v_rw_shift_mu, v_rw_w0, v_rw_w2, v_rw_a0, v_rw_a2, v_rw_g2, v_rw_k_k, v_rw_k_a, v_rw_r_k, v_rw_ln_w, v_rw_ln_b, v_w_out, v_norm_ffn, v_ffn_w1, v_ffn_w3, v_ffn_w2, v_norm_ple, v_ple_gate_w, v_ple_up_w, v_final_norm):
    args = dict(locals())
    w = {n: args[n] for n in WEIGHT_NAMES}
    m = {n: args["m_" + n] for n in WEIGHT_NAMES}
    v = {n: args["v_" + n] for n in WEIGHT_NAMES}
    return _step(x, p, loss_target, w, m, v)
```

```python
import functools
import math
from typing import Any, Callable, NamedTuple, Sequence

import jax
import jax.numpy as jnp
from jax import lax
from jax.experimental import pallas as pl
from jax.experimental.pallas import tpu as pltpu

F32 = jnp.float32
BF16 = jnp.bfloat16
MESH = pl.DeviceIdType.MESH

S5_WIDTH = 512
RWKV_WIDTH = 512
S5_GROUP = 16
S5_GROUPS = 32
S5_STATE = 64
S5_LANES = S5_GROUPS * S5_STATE
S5_TILE_GROUPS = 8
S5_TILES = S5_GROUPS // S5_TILE_GROUPS
S5_TILE_CH = S5_TILE_GROUPS * S5_GROUP
S5_TILE_LANES = S5_TILE_GROUPS * S5_STATE
HEAD = 64
HEADS = 8
DECAY_LORA = 64
AAA_LORA = 64
GATE_LORA = 128
FFN_HIDDEN = 2816
FFN_TILE = FFN_HIDDEN // 2
RMS_EPS = 1e-6
GN_EPS = 64e-5
L2_EPS = 1e-12
ADAM_LR = 0.001
ADAM_B1 = 0.9
ADAM_B2 = 0.999
ADAM_EPS = 1e-08
ADAM_WD = 0.01
ADAM_STEP = 10

WKV_CHUNK = 64
SCAN_UNROLL = 4
WIRE = jnp.bfloat16
WKV_PASSES = 1
VMEM_LIMIT_BYTES = 48 * 1024 * 1024
SHARE_VMEM_LIMIT_BYTES = 56 * 1024 * 1024
LANE = 128
PACK_COLS = 1024
SMALL_ROWS = 24

WEIGHT_NAMES = ['norm_mix', 'w_in', 's5_lam_re', 's5_lam_im', 's5_log_step', 's5_b_re', 's5_b_im', 's5_c_re',
                's5_c_im', 's5_d', 's5_glu_w', 's5_glu_b', 'rw_shift_mu', 'rw_w0', 'rw_w2', 'rw_a0', 'rw_a2',
                'rw_g2', 'rw_k_k', 'rw_k_a', 'rw_r_k', 'rw_ln_w', 'rw_ln_b', 'w_out', 'norm_ffn', 'ffn_w1',
                'ffn_w3', 'ffn_w2', 'norm_ple', 'ple_gate_w', 'ple_up_w', 'final_norm']
SHARDED = {'w_in': 1, 's5_glu_w': 0, 'rw_w2': 1, 'rw_a2': 1, 'rw_g2': 1, 'w_out': 0, 'ffn_w1': 1, 'ffn_w3': 1,
           'ffn_w2': 0, 'ple_gate_w': 0, 'ple_up_w': 1}
SHARDED_NAMES = [n for n in WEIGHT_NAMES if n in SHARDED]
LATE_NAMES = ['w_out', 'ffn_w1', 'ffn_w3', 'ffn_w2', 'ple_gate_w', 'ple_up_w']
EARLY_NAMES = [n for n in SHARDED_NAMES if n not in LATE_NAMES]
LATE_GATHER = {'s5_scan': ['w_out', 'ple_gate_w', 'ple_up_w'], 'wkv_fwd': ['ffn_w1', 'ffn_w3'], 'ffn_up': ['ffn_w2']}
SMALL_NAMES = [n for n in WEIGHT_NAMES if n not in SHARDED]


def _params(sem=None):
    return pltpu.CompilerParams(dimension_semantics=sem, vmem_limit_bytes=VMEM_LIMIT_BYTES)


def _tile(n, target):
    best = None
    for d in range(LANE, min(n, target) + 1, LANE):
        if n % d == 0:
            best = d
    return n if best is None else best


_NN = (((1,), (0,)), ((), ()))
_NT = (((1,), (1,)), ((), ()))
_TN = (((0,), (0,)), ((), ()))


def _split(a):
    a = a.astype(F32)
    hi = a.astype(BF16)
    return hi, (a - hi.astype(F32)).astype(BF16)


def _dg(a, b, dims, passes):
    dg = lambda p, q: lax.dot_general(p, q, dims, preferred_element_type=F32)
    if passes == 1:
        return dg(a.astype(BF16), b.astype(BF16))
    bh, bl = _split(b)
    return dg(a.astype(BF16), bh) + dg(a.astype(BF16), bl)


_DOT_BWD = {_NN: (("g", "b", _NT), ("a", "g", _TN)),
            _NT: (("g", "b", _NN), ("g", "a", _TN)),
            _TN: (("b", "g", _NT), ("a", "g", _NN))}


@functools.partial(jax.custom_vjp, nondiff_argnums=(2, 3))
def _dot(a, b, dims, passes):
    return _dg(a, b, dims, passes)


def _dot_fwd(a, b, dims, passes):
    return _dg(a, b, dims, passes), (a, b)


def _dot_bwd(dims, passes, res, g):
    env = {"a": res[0], "b": res[1], "g": g}
    return tuple(_dg(env[p], env[q], d, passes) for p, q, d in _DOT_BWD[dims])


_dot.defvjp(_dot_fwd, _dot_bwd)


def _bdot(x, w):
    return _dot(x, w, _NN, 1)


@jax.custom_vjp
def _shift_down(z):
    return pltpu.roll(z, 1, 0)


def _shift_down_fwd(z):
    return pltpu.roll(z, 1, 0), None


def _shift_down_bwd(_, g):
    return (pltpu.roll(g, g.shape[0] - 1, 0),)


_shift_down.defvjp(_shift_down_fwd, _shift_down_bwd)


def _head_sum_impl(x):
    r = lax.broadcasted_iota(jnp.int32, (LANE, LANE), 0) // HEAD
    c = lax.broadcasted_iota(jnp.int32, (LANE, LANE), 1) // HEAD
    ones = (r == c).astype(BF16)
    hi, lo = _split(x)
    dg = lambda p: lax.dot_general(p, ones, _NN, preferred_element_type=F32)
    tiles = [slice(j, j + LANE) for j in range(0, x.shape[1], LANE)]
    return jnp.concatenate([dg(hi[:, s]) + dg(lo[:, s]) for s in tiles], axis=1)


@jax.custom_vjp
def _head_sum(x):
    return _head_sum_impl(x)


_head_sum.defvjp(lambda x: (_head_sum_impl(x), None), lambda _, g: (_head_sum_impl(g),))


def _mm(name, a, b, mode, out_dtype=F32, tm=1024, tn=1024, tk=1536, plan=None):
    if mode == 'nn':
        (m, k), (_, n) = a.shape, b.shape
    elif mode == 'nt':
        (m, k), (n, _) = a.shape, b.shape
    else:
        (k, m), (_, n) = a.shape, b.shape
    tm, tn, tk = _tile(m, tm), _tile(n, tn), _tile(k, tk)
    nm, nn, nk = m // tm, n // tn, k // tk
    dims = {'nn': _NN, 'nt': _NT, 'tn': _TN}[mode]
    plan = _NO_PLAN if plan is None else plan
    parts, plan_in_specs, plan_out_shape, plan_out_specs, plan_sems = _carry(plan, 2, 1)

    def body(*refs):
        a_ref, b_ref, o_ref = refs[0], refs[1], refs[2 + len(plan.ins)]
        acc_ref = refs[3 + len(plan.ins) + len(plan.out_shape)]
        i, j, kk = pl.program_id(0), pl.program_id(1), pl.program_id(2)

        if plan is not _NO_PLAN:
            pl.when((i == 0) & (j == 0) & (kk == 0))(lambda: plan.start(*parts(refs)))

        @pl.when(kk == 0)
        def _():
            acc_ref[...] = jnp.zeros_like(acc_ref)

        acc_ref[...] += _dg(a_ref[...], b_ref[...], dims, 1)

        @pl.when(kk == nk - 1)
        def _():
            o_ref[...] = acc_ref[...].astype(o_ref.dtype)

        if plan is not _NO_PLAN:
            pl.when((i == nm - 1) & (j == nn - 1) & (kk == nk - 1))(lambda: plan.wait(*parts(refs)))

    if mode == 'tn':
        a_spec = pl.BlockSpec((tk, tm), lambda i, j, l: (l, i))
    else:
        a_spec = pl.BlockSpec((tm, tk), lambda i, j, l: (i, l))
    if mode == 'nt':
        b_spec = pl.BlockSpec((tn, tk), lambda i, j, l: (j, l))
    else:
        b_spec = pl.BlockSpec((tk, tn), lambda i, j, l: (l, j))
    res = pl.pallas_call(
        body, name=name,
        out_shape=[jax.ShapeDtypeStruct((m, n), out_dtype)] + plan_out_shape,
        grid=(nm, nn, nk),
        in_specs=[a_spec, b_spec] + plan_in_specs,
        out_specs=[pl.BlockSpec((tm, tn), lambda i, j, l: (i, j))] + plan_out_specs,
        scratch_shapes=[pltpu.VMEM((tm, tn), F32)] + plan_sems,
        compiler_params=_params(("parallel", "parallel", "arbitrary") if plan is _NO_PLAN else ("arbitrary",) * 3),
    )(a, b, *plan.ins)
    return res[0] if plan is _NO_PLAN else (res[0], res[1:])


def _mm_tiles(name, a, b, mode, out_shape, grid, a_spec, b_spec, o_spec, add=None, out_dtype=F32):
    dims = {'nn': _NN, 'nt': _NT, 'tn': _TN}[mode]
    nk = grid[2]
    extra = [] if add is None else [add]

    def body(a_ref, b_ref, *refs):
        o_ref, acc_ref = refs[len(extra):]
        kk = pl.program_id(2)

        @pl.when(kk == 0)
        def _():
            acc_ref[...] = refs[0][...].astype(F32) if extra else jnp.zeros_like(acc_ref)

        acc_ref[...] += _dg(a_ref[...], b_ref[...], dims, 1)

        @pl.when(kk == nk - 1)
        def _():
            o_ref[...] = acc_ref[...].astype(o_ref.dtype)

    return pl.pallas_call(
        body, name=name,
        out_shape=jax.ShapeDtypeStruct(out_shape, out_dtype),
        grid=grid, in_specs=[a_spec, b_spec] + [o_spec] * len(extra), out_specs=o_spec,
        scratch_shapes=[pltpu.VMEM(o_spec.block_shape, F32)],
        compiler_params=_params(("parallel", "parallel", "arbitrary")),
    )(a, b, *extra)


def _s5_expand(name, u, blk, tm=2048):
    t = u.shape[0]
    tm = min(tm, t)
    ch, ln, nt = S5_TILE_CH, S5_TILE_LANES, S5_TILES
    return _mm_tiles(name, u, blk, 'nn', (t, 2 * S5_LANES), (t // tm, 2 * nt, 1),
                     pl.BlockSpec((tm, ch), lambda i, j, l: (i, j % nt)),
                     pl.BlockSpec((ch, ln), lambda i, j, l: (j % nt, j // nt)),
                     pl.BlockSpec((tm, ln), lambda i, j, l: (i, j)), out_dtype=BF16)


def _s5_contract(name, x, blk, tm=2048, add=None, out_dtype=F32):
    t = x.shape[0]
    tm = min(tm, t)
    ch, ln, nt = S5_TILE_CH, S5_TILE_LANES, S5_TILES
    return _mm_tiles(name, x, blk, 'nt', (t, S5_WIDTH), (t // tm, nt, 2),
                     pl.BlockSpec((tm, ln), lambda i, j, l: (i, j + nt * l)),
                     pl.BlockSpec((ch, ln), lambda i, j, l: (j, l)),
                     pl.BlockSpec((tm, ch), lambda i, j, l: (i, j)), add=add, out_dtype=out_dtype)


def _s5_block_grad(name, u, x, tk=2048):
    t = u.shape[0]
    tk = min(tk, t)
    ch, ln, nt = S5_TILE_CH, S5_TILE_LANES, S5_TILES
    return _mm_tiles(name, u, x, 'tn', (S5_WIDTH, 2 * ln), (nt, 2, t // tk),
                     pl.BlockSpec((tk, ch), lambda i, j, l: (l, i)),
                     pl.BlockSpec((tk, ln), lambda i, j, l: (l, i + nt * j)),
                     pl.BlockSpec((ch, ln), lambda i, j, l: (i, j)))


def _full_spec(p):
    nd = p.ndim
    return pl.BlockSpec(p.shape, lambda i, nd=nd: (0,) * nd)


def _tok_fwd(name, fn, toks, params, outs, tm):
    t = toks[0].shape[0]
    nt, npar = len(toks), len(params)

    def body(*refs):
        tv = [r[...].astype(F32) for r in refs[:nt]]
        pv = [r[...].astype(F32) for r in refs[nt:nt + npar]]
        res = fn(*tv, *pv)
        for r, v in zip(refs[nt + npar:], res):
            r[...] = v.astype(r.dtype)

    return pl.pallas_call(
        body, name=name,
        out_shape=[jax.ShapeDtypeStruct((t, w), d) for w, d in outs],
        grid=(t // tm,),
        in_specs=[pl.BlockSpec((tm, a.shape[1]), lambda i: (i, 0)) for a in toks] + [_full_spec(p) for p in params],
        out_specs=[pl.BlockSpec((tm, w), lambda i: (i, 0)) for w, _ in outs],
        compiler_params=_params(("parallel",)),
    )(*toks, *params)


def _tok_bwd(name, fn, toks, params, cots, dtok, dpar, tm, acc_out=None, add_to=None):
    t = toks[0].shape[0]
    nt, npar = len(toks), len(params)
    cot_arrays = [c for c in cots if c is not None]
    ncot = len(cot_arrays)
    extra = [] if add_to is None else [add_to[1]]
    dtok_idx = [i for i, d in enumerate(dtok) if d is not None]
    dpar_idx = [i for i, d in enumerate(dpar) if d]

    def body(*refs):
        pos = 0
        tin = refs[pos:pos + nt]; pos += nt
        pin = refs[pos:pos + npar]; pos += npar
        cin = refs[pos:pos + ncot]; pos += ncot
        ein = refs[pos:pos + len(extra)]; pos += len(extra)
        dto = refs[pos:pos + len(dtok_idx)]; pos += len(dtok_idx)
        dpo = refs[pos:pos + len(dpar_idx)]; pos += len(dpar_idx)
        acc = refs[pos] if acc_out is not None else None
        first = pl.program_id(0) == 0

        tv = [r[...].astype(F32) for r in tin]
        pv = [r[...].astype(F32) for r in pin]
        res, vjp = jax.vjp(fn, *tv, *pv)
        cit = iter(cin)
        cs = tuple(jnp.ones_like(o) if c is None else next(cit)[...].astype(F32) for c, o in zip(cots, res))
        g = vjp(cs)
        for r, i in zip(dto, dtok_idx):
            v = g[i]
            if add_to is not None and add_to[0] == i:
                v = v + ein[0][...].astype(F32)
            r[...] = v.astype(r.dtype)

        @pl.when(first)
        def _():
            for r in dpo:
                r[...] = jnp.zeros_like(r)
            if acc is not None:
                acc[...] = jnp.zeros_like(acc)

        for r, i in zip(dpo, dpar_idx):
            r[...] += g[nt + i]
        if acc is not None:
            acc[...] += res[acc_out]

    out_shape = [jax.ShapeDtypeStruct(toks[i].shape, dtok[i]) for i in dtok_idx]
    out_shape += [jax.ShapeDtypeStruct(params[i].shape, F32) for i in dpar_idx]
    out_specs = [pl.BlockSpec((tm, toks[i].shape[1]), lambda i_: (i_, 0)) for i in dtok_idx]
    out_specs += [_full_spec(params[i]) for i in dpar_idx]
    if acc_out is not None:
        out_shape.append(jax.ShapeDtypeStruct((1, 1), F32))
        out_specs.append(pl.BlockSpec((1, 1), lambda i_: (0, 0)))
    tok_spec = lambda a: pl.BlockSpec((tm, a.shape[1]), lambda i_: (i_, 0))
    return pl.pallas_call(
        body, name=name,
        out_shape=out_shape,
        grid=(t // tm,),
        in_specs=[tok_spec(a) for a in toks] + [_full_spec(p) for p in params]
        + [tok_spec(c) for c in cot_arrays] + [tok_spec(e) for e in extra],
        out_specs=out_specs,
        compiler_params=_params(("arbitrary",)),
    )(*toks, *params, *cot_arrays, *extra)


def _small_fwd(name, fn, ins, outs):
    n = len(ins)

    def body(*refs):
        res = fn(*[r[...] for r in refs[:n]])
        for r, v in zip(refs[n:], res):
            r[...] = v.astype(r.dtype)

    return pl.pallas_call(
        body, name=name,
        out_shape=[jax.ShapeDtypeStruct(s, d) for s, d in outs],
        compiler_params=_params(),
    )(*ins)


def _small_bwd(name, fn, ins, cots):
    n = len(ins)

    def body(*refs):
        _, vjp = jax.vjp(fn, *[r[...] for r in refs[:n]])
        g = vjp(tuple(r[...] for r in refs[n:n + len(cots)]))
        for r, v in zip(refs[n + len(cots):], g):
            r[...] = v

    return pl.pallas_call(
        body, name=name,
        out_shape=[jax.ShapeDtypeStruct(a.shape, F32) for a in ins],
        compiler_params=_params(),
    )(*ins, *cots)


def _rms(x, g):
    return x * lax.rsqrt(jnp.mean(x * x, axis=-1, keepdims=True) + RMS_EPS) * g


def _f_norm_in(x, g):
    return (_rms(x, g),)


def _f_mix_res(x, mixed, g):
    h1 = x + mixed
    return h1, _rms(h1, g)


def _f_ffn_act(a13):
    a1, a3 = a13[:, :FFN_HIDDEN], a13[:, FFN_HIDDEN:]
    return (jax.nn.silu(a1) * a3,)


def _f_ffn_res(h1, ffo, g):
    h2 = h1 + ffo
    return h2, _rms(h2, g)


def _f_loss(h2, gpre, pu, target, g):
    h3 = h2 + jax.nn.sigmoid(gpre) * pu
    y = _rms(h3, g)
    err = jnp.square(y - target)
    return (0.5 * jnp.sum(jnp.mean(err, axis=-1, keepdims=True), axis=0, keepdims=True),)


def _f_s5_post(ypre, u, d, glu_w, glu_b):
    z = jax.nn.gelu(ypre + u * d)
    return (z * jax.nn.sigmoid(_bdot(z, glu_w) + glu_b),)


def _softplus(x):
    return jnp.maximum(x, 0.0) + jnp.log(1.0 + jnp.exp(-jnp.abs(x)))


def _f_rw_pre(z, carry, shift_mu, w0, w2, a0, a2, g2, k_k, k_a):
    rw = RWKV_WIDTH
    first_row = lax.broadcasted_iota(jnp.int32, z.shape, 0) == 0
    prev = jnp.where(first_row, carry, _shift_down(z))
    zs = z + (prev - z) * shift_mu
    o1, o2 = 3 * rw + DECAY_LORA, 3 * rw + DECAY_LORA + AAA_LORA
    r, k, v = zs[:, :rw], zs[:, rw:2 * rw], zs[:, 2 * rw:3 * rw]
    wl, al, gl = zs[:, 3 * rw:o1], zs[:, o1:o2], zs[:, o2:]
    w = -_softplus(-(w0 + _bdot(jnp.tanh(wl), w2))) - 0.5
    log_decay = -jnp.exp(w)
    a = jax.nn.sigmoid(a0 + _bdot(al, a2))
    g = _bdot(jax.nn.sigmoid(gl), g2)
    kk = k * k_k
    norm = jnp.sqrt(_head_sum(kk * kk))
    kk = kk / jnp.maximum(norm, L2_EPS)
    kp = k * (1.0 + (a - 1.0) * k_a)
    return r, log_decay, kp, v, -kk, kk * a, g


def _f_rw_post(y, r, kp, v, g, ln_w, ln_b, r_k):
    yc = y - _head_sum(y) * (1.0 / HEAD)
    var = _head_sum(yc * yc) * (1.0 / HEAD)
    yn = yc * lax.rsqrt(var + GN_EPS) * ln_w + ln_b
    bonus = _head_sum(r * kp * r_k) * v
    return ((yn + bonus) * g,)


def _f_s5_lam(lam_re, lam_im, log_step):
    step = jnp.exp(log_step)
    dr, di = lam_re * step, lam_im * step
    e = jnp.exp(dr)
    lbr, lbi = e * jnp.cos(di), e * jnp.sin(di)
    nr, ni = lbr - 1.0, lbi
    den = lam_re * lam_re + lam_im * lam_im
    return lbr, lbi, (nr * lam_re + ni * lam_im) / den, (ni * lam_re - nr * lam_im) / den


def _f_s5_build(coef_r, coef_i, btr, bti, ctr, cti):
    bbr = coef_r * btr - coef_i * bti
    bbi = coef_r * bti + coef_i * btr
    shape = (S5_WIDTH, S5_TILE_LANES)
    rows = (lax.broadcasted_iota(jnp.int32, shape, 0) % S5_TILE_CH) // S5_GROUP
    cols = lax.broadcasted_iota(jnp.int32, shape, 1) // S5_STATE
    mask = (rows == cols).astype(F32)

    def blocks(m):
        per_tile = [m[:, S5_TILE_LANES * i:S5_TILE_LANES * (i + 1)] for i in range(S5_TILES)]
        return jnp.concatenate([t for t in per_tile for _ in range(S5_TILE_GROUPS)], axis=0) * mask

    return (jnp.concatenate([blocks(bbr), blocks(bbi)], axis=1),
            jnp.concatenate([blocks(ctr), -blocks(cti)], axis=1))


HALO = 8


def _rw_pre_specs(z, params, tm, order):
    halo_blocks = tm // HALO
    return ([pl.BlockSpec((tm, z.shape[1]), lambda i: (order(i), 0)),
             pl.BlockSpec((HALO, z.shape[1]), lambda i: (jnp.maximum(order(i) * halo_blocks - 1, 0), 0))]
            + [_full_spec(p) for p in params])


def _rw_pre_fwd(z, params, tm):
    t = z.shape[0]
    npar = len(params)

    def body(z_ref, halo_ref, *refs):
        carry = jnp.where(pl.program_id(0) == 0, 0.0, halo_ref[pl.ds(HALO - 1, 1), :])
        res = _f_rw_pre(z_ref[...], carry, *[r[...].astype(F32) for r in refs[:npar]])
        for r, v in zip(refs[npar:], res):
            r[...] = v

    return pl.pallas_call(
        body, name="rw_pre",
        out_shape=[jax.ShapeDtypeStruct((t, RWKV_WIDTH), F32)] * 7,
        grid=(t // tm,),
        in_specs=_rw_pre_specs(z, params, tm, lambda i: i),
        out_specs=[pl.BlockSpec((tm, RWKV_WIDTH), lambda i: (i, 0))] * 7,
        compiler_params=_params(("parallel",)),
    )(z, z, *params)


def _rw_pre_bwd(z, params, cots, tm):
    t = z.shape[0]
    nt = t // tm
    npar = len(params)
    order = lambda i: nt - 1 - i
    flat_cots = [a for group in cots for a in group]
    ncot = len(flat_cots)

    def body(z_ref, halo_ref, *refs):
        pin, cin = refs[:npar], list(refs[npar:npar + ncot])
        dz_ref = refs[npar + ncot]
        dpo = refs[npar + ncot + 1:npar + ncot + 1 + npar]
        dcarry_ref = refs[npar + ncot + 1 + npar]
        i = pl.program_id(0)

        @pl.when(i == 0)
        def _():
            dcarry_ref[...] = jnp.zeros_like(dcarry_ref)
            for r in dpo:
                r[...] = jnp.zeros_like(r)

        carry = jnp.where(i == nt - 1, 0.0, halo_ref[pl.ds(HALO - 1, 1), :])
        _, vjp = jax.vjp(_f_rw_pre, z_ref[...], carry, *[r[...].astype(F32) for r in pin])
        g = vjp(tuple(sum(cin.pop(0)[...] for _ in group) for group in cots))
        last_row = lax.broadcasted_iota(jnp.int32, z_ref.shape, 0) == tm - 1
        dz_ref[...] = (g[0] + jnp.where(last_row, dcarry_ref[...], 0.0)).astype(dz_ref.dtype)
        dcarry_ref[...] = g[1]
        for r, v in zip(dpo, g[2:]):
            r[...] += v

    tok = lambda w: pl.BlockSpec((tm, w), lambda i: (order(i), 0))
    return pl.pallas_call(
        body, name="rw_pre_bwd",
        out_shape=[jax.ShapeDtypeStruct(z.shape, BF16)] + [jax.ShapeDtypeStruct(p.shape, F32) for p in params],
        grid=(nt,),
        in_specs=_rw_pre_specs(z, params, tm, order) + [tok(RWKV_WIDTH)] * ncot,
        out_specs=[tok(z.shape[1])] + [_full_spec(p) for p in params],
        scratch_shapes=[pltpu.VMEM((1, z.shape[1]), F32)],
        compiler_params=_params(("arbitrary",)),
    )(z, z, *params, *flat_cots)


def _s5_scan(bu, lam, tm, plan):
    t, w = bu.shape
    h = w // 2
    nt = t // tm
    parts, plan_in_specs, plan_out_shape, plan_out_specs, plan_sems = _carry(plan, 2, 1)

    def body(*refs):
        bu_ref, lam_ref = refs[:2]
        xb_ref = refs[2 + len(plan.ins)]
        carry_ref = refs[3 + len(plan.ins) + len(plan.out_shape)]
        x_ref, refs = refs[-1], refs[:-1]

        @pl.when(pl.program_id(0) == 0)
        def _():
            carry_ref[...] = jnp.zeros_like(carry_ref)
            plan.start(*parts(refs))

        lr, li = lam_ref[:, :h], lam_ref[:, h:]
        x_ref[...] = bu_ref[...].astype(F32)

        def step(s, c):
            cr, ci = c
            row = pl.ds(s, 1)
            nr = lr * cr - li * ci + x_ref[row, :h]
            ni = lr * ci + li * cr + x_ref[row, h:]
            x_ref[row, :h] = nr
            x_ref[row, h:] = ni
            return nr, ni

        cr, ci = lax.fori_loop(0, tm, step, (carry_ref[:, :h], carry_ref[:, h:]), unroll=SCAN_UNROLL)
        carry_ref[:, :h] = cr
        carry_ref[:, h:] = ci
        xb_ref[...] = x_ref[...].astype(BF16)

        @pl.when(pl.program_id(0) == nt - 1)
        def _():
            plan.wait(*parts(refs))

    spec = pl.BlockSpec((tm, w), lambda i: (i, 0))
    res = pl.pallas_call(
        body, name="s5_scan",
        out_shape=[jax.ShapeDtypeStruct((t, w), BF16)] + plan_out_shape,
        grid=(nt,),
        in_specs=[spec, pl.BlockSpec((1, w), lambda i: (0, 0))] + plan_in_specs,
        out_specs=[spec] + plan_out_specs,
        scratch_shapes=[pltpu.VMEM((1, w), F32)] + plan_sems + [pltpu.VMEM((tm, w), F32)],
        compiler_params=_params(("arbitrary",)),
    )(bu, lam, *plan.ins)
    return res[0], res[1:]


def _s5_scan_bwd(dx, xb, lam, tm):
    t, w = dx.shape
    h = w // 2
    nt = t // tm
    halo = BF16_SUBLANES

    rows8 = 8

    def body(dx_ref, xb_ref, halo_ref, lam_ref, dbu_out_ref, dlam_ref, carry_ref, dbu_ref, xp_ref):
        @pl.when(pl.program_id(0) == 0)
        def _():
            carry_ref[...] = jnp.zeros_like(carry_ref)
            dlam_ref[...] = jnp.zeros_like(dlam_ref)

        lr, li = lam_ref[:, :h], lam_ref[:, h:]
        dbu_ref[...] = dx_ref[...].astype(F32)

        def step(s, c):
            cr, ci = c
            row = pl.ds(tm - 1 - s, 1)
            nr = lr * cr + li * ci + dbu_ref[row, :h]
            ni = lr * ci - li * cr + dbu_ref[row, h:]
            dbu_ref[row, :h] = nr
            dbu_ref[row, h:] = ni
            return nr, ni

        cr, ci = lax.fori_loop(0, tm, step, (carry_ref[:, :h], carry_ref[:, h:]), unroll=SCAN_UNROLL)
        carry_ref[:, :h] = cr
        carry_ref[:, h:] = ci
        halo_rows = lax.broadcasted_iota(jnp.int32, (halo, w), 0)
        before = jnp.sum(jnp.where(halo_rows == halo - 1, halo_ref[...].astype(F32), 0.0), axis=0, keepdims=True)
        before = jnp.where(pl.program_id(0) == nt - 1, 0.0, before)
        first_row = lax.broadcasted_iota(jnp.int32, (tm, w), 0) == 0
        xp_ref[...] = jnp.where(first_row, before, pltpu.roll(xb_ref[...].astype(F32), 1, 0))

        def accumulate(s, acc):
            ar, ai = acc
            rows = pl.ds(pl.multiple_of(s * rows8, rows8), rows8)
            gr, gi = dbu_ref[rows, :h], dbu_ref[rows, h:]
            pr, pi_ = xp_ref[rows, :h], xp_ref[rows, h:]
            return ar + (gr * pr + gi * pi_), ai + (gi * pr - gr * pi_)

        zero = jnp.zeros((rows8, h), F32)
        ar, ai = lax.fori_loop(0, tm // rows8, accumulate, (zero, zero))
        dlam_ref[:, :h] += jnp.sum(ar, axis=0, keepdims=True)
        dlam_ref[:, h:] += jnp.sum(ai, axis=0, keepdims=True)
        dbu_out_ref[...] = dbu_ref[...].astype(BF16)

    spec = pl.BlockSpec((tm, w), lambda i: (nt - 1 - i, 0))
    halo_spec = pl.BlockSpec((halo, w), lambda i: (jnp.maximum((nt - 1 - i) * (tm // halo) - 1, 0), 0))
    row_spec = pl.BlockSpec((1, w), lambda i: (0, 0))
    return pl.pallas_call(
        body, name="s5_scan_bwd",
        out_shape=[jax.ShapeDtypeStruct((t, w), BF16), jax.ShapeDtypeStruct((1, w), F32)],
        grid=(nt,),
        in_specs=[spec, spec, halo_spec, row_spec],
        out_specs=[spec, row_spec],
        scratch_shapes=[pltpu.VMEM((1, w), F32), pltpu.VMEM((tm, w), F32), pltpu.VMEM((tm, w), F32)],
        compiler_params=_params(("arbitrary",)),
    )(dx, xb, xb, lam)


def _unit_lower_inverses_impl(ns):
    c = ns[0].shape[0]
    eye = (lax.broadcasted_iota(jnp.int32, (c, c), 0) == lax.broadcasted_iota(jnp.int32, (c, c), 1)).astype(F32)
    inv = [eye + n for n in ns]
    pw = [_dg(n, n, _NN, WKV_PASSES) for n in ns]
    for _ in range(int(math.log2(c)) - 2):
        both = [_dg(jnp.concatenate([i, q], axis=0), q, _NN, WKV_PASSES) for i, q in zip(inv, pw)]
        inv = [i + q[:c] for i, q in zip(inv, both)]
        pw = [q[c:] for q in both]
    return tuple(i + _dg(i, q, _NN, WKV_PASSES) for i, q in zip(inv, pw))


@jax.custom_vjp
def _unit_lower_inverses(ns):
    return _unit_lower_inverses_impl(ns)


def _unit_lower_inverses_fwd(ns):
    inv = _unit_lower_inverses_impl(ns)
    return inv, inv


def _unit_lower_inverses_bwd(inv, g):
    left = [_dg(i, gi, _TN, WKV_PASSES) for i, gi in zip(inv, g)]
    return (tuple(_dg(q, i, _NT, WKV_PASSES) for q, i in zip(left, inv)),)


_unit_lower_inverses.defvjp(_unit_lower_inverses_fwd, _unit_lower_inverses_bwd)


def _wkv_chunks(s0, r, lw, k, v, a, b):
    c = r[0].shape[0]
    row = lax.broadcasted_iota(jnp.int32, (c, c), 0)
    col = lax.broadcasted_iota(jnp.int32, (c, c), 1)
    incl, strict = col <= row, col < row
    tri = incl.astype(F32)
    each = lambda f, *xs: [f(*t) for t in zip(*xs)]
    stack = lambda p, q: jnp.concatenate([p, q], axis=0)
    dot = lambda p, q, dims=_NN: _dot(p, q, dims, WKV_PASSES)
    lc = each(lambda l: _dot(tri, l, _NN, 2), lw)
    e_neg = each(lambda l: jnp.exp(-l), lc)
    ar = each(lambda x, z, l, w: stack(x * jnp.exp(l - w), z * jnp.exp(l)), a, r, lc, lw)
    bk = each(lambda x, z, e: stack(x * e, z * e), b, k, e_neg)
    m = each(lambda p, q: dot(p, q, _NT), ar, bk)
    mab = each(lambda q: jnp.where(strict, q[:c, :c], 0.0), m)
    mak_mrk = each(lambda q: stack(jnp.where(strict, q[:c, c:], 0.0), jnp.where(incl, q[c:, c:], 0.0)), m)
    mrb = each(lambda q: jnp.where(incl, q[c:, :c], 0.0), m)
    xy = each(lambda p, s, q, z: dot(p, s, _NT) + dot(q, z), ar, s0, mak_mrk, v)
    inv = _unit_lower_inverses(tuple(mab))
    u = each(lambda i, q: dot(i, q[:c]), inv, xy)
    y = each(lambda q, z, p: q[c:] + dot(z, p), xy, mrb, u)
    e_tot = each(lambda l: jnp.exp(jnp.sum(l, axis=0, keepdims=True)), lw)
    s1 = each(lambda s, p, z, q, e: (s + dot(stack(p, z), q, _TN)) * e, s0, u, v, bk, e_tot)
    return y, s1


def _carry(plan, n_args, n_outs):
    n_in, n_out = len(plan.ins), len(plan.out_shape)

    def parts(refs):
        base = n_args + n_in + n_outs
        return refs[n_args:n_args + n_in], refs[base:base + n_out], refs[base + n_out + 1:]

    return parts, [HBM_SPEC] * n_in, list(plan.out_shape), [HBM_SPEC] * n_out, list(plan.sems)


def _head_cols(ref):
    return tuple(ref[:, h * HEAD:(h + 1) * HEAD] for h in range(HEADS))


def _wkv_fwd(seqs, plan):
    t, w = seqs[0].shape
    c, n = WKV_CHUNK, HEAD
    nc = t // c
    parts, plan_in_specs, plan_out_shape, plan_out_specs, plan_sems = _carry(plan, 6, 2)

    def body(*refs):
        ins, (y_ref, ck_ref) = refs[:6], refs[6 + len(plan.ins):8 + len(plan.ins)]
        s_ref = refs[8 + len(plan.ins) + len(plan.out_shape)]

        @pl.when(pl.program_id(0) == 0)
        def _():
            s_ref[...] = jnp.zeros_like(s_ref)
            plan.start(*parts(refs))

        s0 = tuple(s_ref[h] for h in range(HEADS))
        ys, s1 = _wkv_chunks(s0, *[_head_cols(r) for r in ins])
        for h in range(HEADS):
            ck_ref[0, h] = s0[h]
            y_ref[:, h * n:(h + 1) * n] = ys[h]
            s_ref[h] = s1[h]

        @pl.when(pl.program_id(0) == nc - 1)
        def _():
            plan.wait(*parts(refs))

    spec = pl.BlockSpec((c, w), lambda i: (i, 0))
    res = pl.pallas_call(
        body, name="wkv_fwd",
        out_shape=[jax.ShapeDtypeStruct((t, w), F32), jax.ShapeDtypeStruct((nc, HEADS, n, n), F32)] + plan_out_shape,
        grid=(nc,),
        in_specs=[spec] * 6 + plan_in_specs,
        out_specs=[spec, pl.BlockSpec((1, HEADS, n, n), lambda i: (i, 0, 0, 0))] + plan_out_specs,
        scratch_shapes=[pltpu.VMEM((HEADS, n, n), F32)] + plan_sems,
        compiler_params=_params(("arbitrary",)),
    )(*seqs, *plan.ins)
    return res[0], res[1], res[2:]


def _wkv_bwd(seqs, ck, dy, plan):
    t, w = seqs[0].shape
    c, n = WKV_CHUNK, HEAD
    nc = t // c
    parts, plan_in_specs, plan_out_shape, plan_out_specs, plan_sems = _carry(plan, 8, 6)

    def body(*refs):
        ins, ck_ref, dy_ref = refs[:6], refs[6], refs[7]
        outs = refs[8 + len(plan.ins):14 + len(plan.ins)]
        ds_ref = refs[14 + len(plan.ins) + len(plan.out_shape)]

        @pl.when(pl.program_id(0) == 0)
        def _():
            ds_ref[...] = jnp.zeros_like(ds_ref)
            plan.start(*parts(refs))

        s0 = tuple(ck_ref[0, h] for h in range(HEADS))
        _, vjp = jax.vjp(_wkv_chunks, s0, *[_head_cols(r) for r in ins])
        g = vjp((list(_head_cols(dy_ref)), [ds_ref[h] for h in range(HEADS)]))
        for h in range(HEADS):
            ds_ref[h] = g[0][h]
            for o, d in zip(outs, g[1:]):
                o[:, h * n:(h + 1) * n] = d[h]

        @pl.when(pl.program_id(0) == nc - 1)
        def _():
            plan.wait(*parts(refs))

    spec = pl.BlockSpec((c, w), lambda i: (nc - 1 - i, 0))
    res = pl.pallas_call(
        body, name="wkv_bwd",
        out_shape=[jax.ShapeDtypeStruct((t, w), F32)] * 6 + plan_out_shape,
        grid=(nc,),
        in_specs=[spec] * 6 + [pl.BlockSpec((1, HEADS, n, n), lambda i: (nc - 1 - i, 0, 0, 0)), spec] + plan_in_specs,
        out_specs=[spec] * 6 + plan_out_specs,
        scratch_shapes=[pltpu.VMEM((HEADS, n, n), F32)] + plan_sems,
        compiler_params=_params(("arbitrary",)),
    )(*seqs, ck, dy, *plan.ins)
    return res[:6], res[6:]


def _coords():
    return lax.axis_index("x"), lax.axis_index("y"), lax.axis_index("c")


def _flip(v, f):
    return 1 - v if f else v


_CHIP_FLIPS = [(1, 0), (0, 1), (1, 1)]
_DEV_FLIPS = [(fx, fy, fc) for fx in (0, 1) for fy in (0, 1) for fc in (0, 1) if (fx, fy, fc) != (0, 0, 0)]
HBM_SPEC = pl.BlockSpec(memory_space=pl.ANY)


def _chip_peer(k, x, y):
    fx, fy = _CHIP_FLIPS[k]
    return _flip(x, fx), _flip(y, fy)


def _dev_peer(k, x, y, c):
    fx, fy, fc = _DEV_FLIPS[k]
    return _flip(x, fx), _flip(y, fy), _flip(c, fc)


def _rows_of_core(ref, core):
    h = ref.shape[-2] // 2
    rows = pl.ds(pl.multiple_of(core * h, 8), h)
    return ref.at[rows, :] if len(ref.shape) == 2 else ref.at[:, rows, :]


class _Plan(NamedTuple):
    ins: Sequence[Any]
    out_shape: Sequence[Any]
    sems: Sequence[Any]
    start: Callable
    wait: Callable


_NO_PLAN = _Plan([], [], [], lambda *_: None, lambda *_: None)


def _run_plan(name, plan):
    n_in, n_out = len(plan.ins), len(plan.out_shape)

    def body(*refs):
        parts = refs[:n_in], refs[n_in:n_in + n_out], refs[n_in + n_out:]
        plan.start(*parts)
        plan.wait(*parts)

    return pl.pallas_call(
        body, name=name, out_shape=list(plan.out_shape),
        in_specs=[HBM_SPEC] * n_in, out_specs=[HBM_SPEC] * n_out, scratch_shapes=list(plan.sems),
    )(*plan.ins)


def _gather_plan(shards):
    n = len(shards)

    def copies(srcs, outs, sems):
        send_sems, recv_sems, local_sems = sems
        x, y, c = _coords()
        me = 2 * x + y

        def remote(i, k, arriving):
            px, py = _chip_peer(k, x, y)
            return pltpu.make_async_remote_copy(
                src_ref=srcs[i], dst_ref=outs[i].at[2 * px + py if arriving else me],
                send_sem=send_sems.at[i, k], recv_sem=recv_sems.at[i, k],
                device_id=(px, py, c), device_id_type=MESH)

        own = [pltpu.make_async_copy(srcs[i], outs[i].at[me], local_sems.at[i]) for i in range(n)]
        pairs = [(i, k) for k in range(3) for i in range(n)]
        return own, [remote(i, k, False) for i, k in pairs], [remote(i, k, True) for i, k in pairs]

    return _Plan(
        ins=shards, out_shape=[jax.ShapeDtypeStruct((4,) + s.shape, s.dtype) for s in shards],
        sems=[pltpu.SemaphoreType.DMA((n, 3)), pltpu.SemaphoreType.DMA((n, 3)), pltpu.SemaphoreType.DMA((n,))],
        start=functools.partial(_start_copies, copies), wait=functools.partial(_wait_copies, copies))


def _start_copies(copies, ins, outs, sems):
    own, sends, _ = copies(ins, outs, sems)
    for cp in own + sends:
        cp.start()


def _wait_copies(copies, ins, outs, sems):
    own, sends, arrivals = copies(ins, outs, sems)
    for cp in arrivals:
        cp.wait_recv()
    for cp in sends:
        cp.wait_send()
    for cp in own:
        cp.wait()


def _exchange_plan(gs, small=None):
    n = len(gs)
    arrays = list(gs) + ([] if small is None else [small])

    def copies(srcs, outs, sems):
        send_sems, recv_sems, local_sems = sems
        x, y, c = _coords()
        me = 4 * x + 2 * y + c

        def piece(i, px, py, pc):
            if i == n:
                return srcs[i].at[4 * px + 2 * py + pc]
            return _rows_of_core(srcs[i].at[2 * px + py], pc)

        def remote(i, k, arriving):
            px, py, pc = _dev_peer(k, x, y, c)
            return pltpu.make_async_remote_copy(
                src_ref=piece(i, px, py, pc), dst_ref=outs[i].at[4 * px + 2 * py + pc if arriving else me],
                send_sem=send_sems.at[i, k], recv_sem=recv_sems.at[i, k],
                device_id=(px, py, pc), device_id_type=MESH)

        own = [pltpu.make_async_copy(piece(i, x, y, c), outs[i].at[me], local_sems.at[i]) for i in range(len(arrays))]
        pairs = [(i, k) for k in range(7) for i in range(len(arrays))]
        return own, [remote(i, k, False) for i, k in pairs], [remote(i, k, True) for i, k in pairs]

    out_shape = [jax.ShapeDtypeStruct((8, g.shape[1] // 2, g.shape[2]), g.dtype) for g in gs]
    if small is not None:
        out_shape.append(jax.ShapeDtypeStruct(small.shape, small.dtype))
    m = len(arrays)
    return _Plan(
        ins=arrays, out_shape=out_shape,
        sems=[pltpu.SemaphoreType.DMA((m, 7)), pltpu.SemaphoreType.DMA((m, 7)), pltpu.SemaphoreType.DMA((m,))],
        start=functools.partial(_start_copies, copies), wait=functools.partial(_wait_copies, copies))


REDUCE_ROWS = 32


def _share_cores(pieces, small):
    n = len(pieces)
    halves = [jax.ShapeDtypeStruct(p.shape[1:], F32) for p in pieces]

    def body(*refs):
        srcs, small_src, outs, small_out = refs[:n], refs[n], refs[n + 1:2 * n + 1], refs[2 * n + 1]
        parts = refs[2 * n + 2:3 * n + 2]
        mine, theirs = refs[3 * n + 2:4 * n + 2], refs[4 * n + 2:5 * n + 2]
        send_sems, recv_sems, ssend, srecv, local_sems = refs[5 * n + 2:]
        x, y, c = _coords()
        me = 4 * x + 2 * y + c

        def reduce(i):
            rows_total = parts[i].shape[1]
            step_rows = min(REDUCE_ROWS, rows_total)

            def step(s, carry):
                rows = pl.ds(pl.multiple_of(s * step_rows, step_rows), step_rows)
                total = parts[i][0, rows, :].astype(F32)
                for d in range(1, 8):
                    total = total + parts[i][d, rows, :].astype(F32)
                mine[i][rows, :] = total
                return carry

            lax.fori_loop(0, rows_total // step_rows, step, 0)

        def big(i):
            return pltpu.make_async_remote_copy(
                src_ref=mine[i], dst_ref=theirs[i], send_sem=send_sems.at[i], recv_sem=recv_sems.at[i],
                device_id=(x, y, 1 - c), device_id_type=MESH)

        def tiny(k, arriving):
            px, py, pc = _dev_peer(k, x, y, c)
            return pltpu.make_async_remote_copy(
                src_ref=small_src, dst_ref=small_out.at[4 * px + 2 * py + pc if arriving else me],
                send_sem=ssend.at[k], recv_sem=srecv.at[k], device_id=(px, py, pc), device_id_type=MESH)

        small_sends = [tiny(k, False) for k in range(7)]
        own_small = pltpu.make_async_copy(small_src, small_out.at[me], local_sems.at[2 * n])
        stage = [pltpu.make_async_copy(srcs[i], parts[i], local_sems.at[i]) for i in range(n)]
        for cp in small_sends + [own_small] + stage:
            cp.start()
        sends = []
        for i in range(n):
            stage[i].wait()
            reduce(i)
            sends.append(big(i))
            sends[-1].start()
        store = [pltpu.make_async_copy(mine[i], outs[i].at[c], local_sems.at[i]) for i in range(n)]
        for cp in store:
            cp.start()
        for i in range(n):
            big(i).wait_recv()
            store.append(pltpu.make_async_copy(theirs[i], outs[i].at[1 - c], local_sems.at[n + i]))
            store[-1].start()
        for k in range(7):
            tiny(k, True).wait_recv()
        for cp in sends + small_sends:
            cp.wait_send()
        for cp in store + [own_small]:
            cp.wait()

    staged = [pltpu.VMEM(s.shape, s.dtype) for s in halves]
    res = pl.pallas_call(
        body, name="share_cores",
        out_shape=[jax.ShapeDtypeStruct((2,) + s.shape, s.dtype) for s in halves]
        + [jax.ShapeDtypeStruct((8,) + small.shape, small.dtype)],
        in_specs=[HBM_SPEC] * (n + 1), out_specs=[HBM_SPEC] * (n + 1),
        scratch_shapes=[pltpu.VMEM(p.shape, p.dtype) for p in pieces] + staged + staged + [
            pltpu.SemaphoreType.DMA((n,)), pltpu.SemaphoreType.DMA((n,)),
            pltpu.SemaphoreType.DMA((7,)), pltpu.SemaphoreType.DMA((7,)),
            pltpu.SemaphoreType.DMA((2 * n + 1,))],
        compiler_params=pltpu.CompilerParams(vmem_limit_bytes=SHARE_VMEM_LIMIT_BYTES),
    )(*pieces, small)
    return res[:n], res[n]


BF16_SUBLANES = 16


def _row_tile(n, target, step=BF16_SUBLANES):
    return max([d for d in range(step, min(n, target) + 1, step) if n % d == 0] or [n])


def _ew(name, fn, ins, outs, block_bytes=2 << 20):
    rows, cols = ins[0].shape[-2:]
    lead = max(math.prod(a.shape[:-2]) for a in ins)
    tr = _row_tile(rows, max(8, block_bytes // (4 * cols * lead)))
    n = len(ins)

    def spec(shape):
        if len(shape) == 2:
            return pl.BlockSpec((tr, cols), lambda i: (i, 0))
        return pl.BlockSpec((shape[0], tr, cols), lambda i: (0, i, 0))

    def body(*refs):
        res = fn(*[r[...] for r in refs[:n]])
        for r, v in zip(refs[n:], res):
            r[...] = v

    return pl.pallas_call(
        body, name=name,
        out_shape=[jax.ShapeDtypeStruct(s, F32) for s in outs],
        grid=(rows // tr,),
        in_specs=[spec(a.shape) for a in ins],
        out_specs=[spec(s) for s in outs],
        compiler_params=_params(("parallel",)),
    )(*ins)


def _sum_slots(a):
    total = a[0].astype(F32)
    for s in range(1, a.shape[0]):
        total = total + a[s].astype(F32)
    return (total,)


def _adamw(g, w, m, v):
    bc1 = 1.0 - ADAM_B1 ** ADAM_STEP
    bc2 = 1.0 - ADAM_B2 ** ADAM_STEP
    m_new = ADAM_B1 * m + (1.0 - ADAM_B1) * g
    v_new = ADAM_B2 * v + (1.0 - ADAM_B2) * jnp.square(g)
    delta = -ADAM_LR * ((m_new / bc1) / (jnp.sqrt(v_new / bc2) + ADAM_EPS) + ADAM_WD * w)
    return delta, m_new, v_new


def _mat(a):
    return a.reshape(a.shape[-2:])


def _to_shard_major(full, axis):
    rows, cols = full.shape
    if axis == 0:
        return full.reshape(4, rows // 4, cols)
    return full.reshape(rows, 4, cols // 4).transpose(1, 0, 2)


def _from_shard_major(a, axis):
    _, r, cs = a.shape
    if axis == 0:
        return a.reshape(4 * r, cs)
    return a.transpose(1, 0, 2).reshape(r, 4 * cs)


def _pack_small(arrays, tail=None):
    flat = [arrays[n].reshape(-1) for n in SMALL_NAMES] + ([] if tail is None else [tail.reshape(1)])
    used = sum(a.shape[0] for a in flat)
    flat.append(jnp.zeros((8 * SMALL_ROWS * PACK_COLS - used,), F32))
    return jnp.concatenate(flat).reshape(8, SMALL_ROWS, PACK_COLS)


def _unpack_small(packed, shapes):
    flat = packed.reshape(-1)
    out, off = {}, 0
    for n in SMALL_NAMES:
        size = math.prod(shapes[n])
        out[n] = flat[off:off + size].reshape(shapes[n])
        off += size
    return out


def _row(a):
    return a.reshape(1, -1)


def _local_step(x, p, target, wf, ws, late_shards):
    wf = dict(wf)
    t = x.shape[0]
    tm = min(256, t)
    tw = min(512, t)
    tx = min(1024, t)
    g = {}

    lam_re, lam_im = ws['s5_lam_re'].reshape(S5_GROUPS, S5_STATE), ws['s5_lam_im'].reshape(S5_GROUPS, S5_STATE)
    log_step = ws['s5_log_step'].reshape(S5_GROUPS, 1)
    gp = (S5_GROUPS, S5_STATE)
    lam_ins = (lam_re, lam_im, log_step)
    lbr, lbi, cfr, cfi = _small_fwd("s5_lam", _f_s5_lam, lam_ins, [(gp, F32)] * 4)
    lam_row = jnp.concatenate([_row(lbr), _row(lbi)], axis=1)
    to_t = lambda a, perm: a.reshape((S5_GROUPS,) + a.shape[-2:]).transpose(perm).reshape(S5_GROUP, S5_LANES)
    build_ins = (_row(cfr), _row(cfi), to_t(ws['s5_b_re'], (2, 0, 1)), to_t(ws['s5_b_im'], (2, 0, 1)),
                 to_t(ws['s5_c_re'], (1, 0, 2)), to_t(ws['s5_c_im'], (1, 0, 2)))
    block_shape = (S5_WIDTH, 2 * S5_TILE_LANES)
    b_blk, c_blk = _small_fwd("s5_build", _f_s5_build, build_ins, [(block_shape, F32)] * 2)

    norm_mix, norm_ffn, norm_ple = _row(ws['norm_mix']), _row(ws['norm_ffn']), _row(ws['norm_ple'])
    final_norm = _row(ws['final_norm'])
    (xn,) = _tok_fwd("norm_in", _f_norm_in, [x], [norm_mix], [(x.shape[1], BF16)], tx)
    u = _mm("proj_s5", xn, wf['w_in'][:, :S5_WIDTH], 'nn')
    z = _mm("proj_rw", xn, wf['w_in'][:, S5_WIDTH:], 'nn')

    bu = _s5_expand("s5_bu", u, b_blk)
    def late_plan(carrier):
        return _gather_plan([late_shards[n] for n in LATE_GATHER[carrier]])

    def arrived(carrier, got):
        wf.update({n: _from_shard_major(a, SHARDED[n]) for n, a in zip(LATE_GATHER[carrier], got)})

    xs, got = _s5_scan(bu, lam_row, tm, late_plan('s5_scan'))
    arrived('s5_scan', got)
    ypre = _s5_contract("s5_y", xs, c_blk)
    s5_par = [_row(ws['s5_d']), wf['s5_glu_w'], _row(ws['s5_glu_b'])]
    (s5_out,) = _tok_fwd("s5_post", _f_s5_post, [ypre, u], s5_par, [(S5_WIDTH, BF16)], tx)

    pre_par = [_row(ws['rw_shift_mu']), _row(ws['rw_w0']), wf['rw_w2'], _row(ws['rw_a0']), wf['rw_a2'],
               wf['rw_g2'], _row(ws['rw_k_k']), _row(ws['rw_k_a'])]
    r, lw, kp, v, an, bn, gate = _rw_pre_fwd(z, pre_par, tw)
    seqs = [r, lw, kp, v, an, bn]
    y_wkv, ck, got = _wkv_fwd(seqs, late_plan('wkv_fwd'))
    arrived('wkv_fwd', got)
    post_par = [_row(ws['rw_ln_w']), _row(ws['rw_ln_b']), _row(ws['rw_r_k'])]
    post_toks = [y_wkv, r, kp, v, gate]
    (rw_out,) = _tok_fwd("rw_post", _f_rw_post, post_toks, post_par, [(RWKV_WIDTH, BF16)], tx)

    mixcat = jnp.concatenate([s5_out, rw_out], axis=1)
    mixed = _mm("mix_out", mixcat, wf['w_out'], 'nn')
    h1, hn = _tok_fwd("mix_res", _f_mix_res, [x, mixed], [norm_ffn], [(x.shape[1], F32), (x.shape[1], BF16)], tx)
    w13 = jnp.concatenate([wf['ffn_w1'], wf['ffn_w3']], axis=1)
    a13, got = _mm("ffn_up", hn, w13, 'nn', out_dtype=BF16, tn=FFN_TILE, plan=late_plan('ffn_up'))
    arrived('ffn_up', got)
    (f,) = _tok_fwd("ffn_act", _f_ffn_act, [a13], [], [(FFN_HIDDEN, BF16)], tw)
    ffo = _mm("ffn_down", f, wf['ffn_w2'], 'nn')
    h2, hp = _tok_fwd("ffn_res", _f_ffn_res, [h1, ffo], [norm_ple], [(x.shape[1], F32), (x.shape[1], BF16)], tx)
    gpre = _mm("ple_gate", hp, wf['ple_gate_w'], 'nn')
    pu = _mm("ple_up", p, wf['ple_up_w'], 'nn')

    dh2, dgpre, dpu, g['final_norm'], loss = _tok_bwd(
        "loss", _f_loss, [h2, gpre, pu, target], [final_norm], [None],
        [F32, BF16, BF16, None], [True], tw, acc_out=0)
    g['ple_gate_w'] = _mm("d_ple_gate_w", hp, dgpre, 'tn', out_dtype=WIRE)
    g['ple_up_w'] = _mm("d_ple_up_w", p, dpu, 'tn', out_dtype=WIRE)
    dhp = _mm("d_hp", dgpre, wf['ple_gate_w'], 'nt')
    dh1, dffo, g['norm_ple'] = _tok_bwd("ffn_res_bwd", _f_ffn_res, [h1, ffo], [norm_ple], [dh2, dhp],
                                        [F32, BF16], [True], tw)
    g['ffn_w2'] = _mm("d_ffn_w2", f, dffo, 'tn', out_dtype=WIRE, tm=FFN_TILE)
    df = _mm("d_f", dffo, wf['ffn_w2'], 'nt', out_dtype=BF16, tn=FFN_TILE)
    (da13,) = _tok_bwd("ffn_act_bwd", _f_ffn_act, [a13], [], [df], [BF16], [], tw)
    dw13 = _mm("d_ffn_w13", hn, da13, 'tn', out_dtype=WIRE, tn=FFN_TILE)
    dw13 = dw13.reshape(dw13.shape[0], 8, FFN_HIDDEN // 4).transpose(1, 0, 2)
    shard_major = {'ffn_w1': dw13[:4], 'ffn_w3': dw13[4:]}
    dhn = _mm("d_hn", da13, w13, 'nt')
    dx_a, dmixed, g['norm_ffn'] = _tok_bwd("mix_res_bwd", _f_mix_res, [x, mixed], [norm_ffn], [dh1, dhn],
                                           [F32, BF16], [True], tw)
    g['w_out'] = _mm("d_w_out", mixcat, dmixed, 'tn', out_dtype=WIRE)
    ds5_out = _mm("d_s5_out", dmixed, wf['w_out'][:S5_WIDTH], 'nt')
    drw_out = _mm("d_rw_out", dmixed, wf['w_out'][S5_WIDTH:], 'nt')

    dy_wkv, dr_b, dkp_b, dv_b, dgate, g['rw_ln_w'], g['rw_ln_b'], g['rw_r_k'] = _tok_bwd(
        "rw_post_bwd", _f_rw_post, post_toks, post_par, [drw_out], [F32] * 5, [True] * 3, tw)
    late_exchange = _exchange_plan([shard_major[n] if n in shard_major else
                                    _to_shard_major(g[n], SHARDED[n]).astype(WIRE) for n in LATE_NAMES])
    dseqs, late_pieces = _wkv_bwd(seqs, ck, dy_wkv, late_exchange)
    pre_cots = [(dseqs[0], dr_b), (dseqs[1],), (dseqs[2], dkp_b), (dseqs[3], dv_b), (dseqs[4],), (dseqs[5],),
                (dgate,)]
    dz, *dpre = _rw_pre_bwd(z, pre_par, pre_cots, tw)
    for n, d in zip(['rw_shift_mu', 'rw_w0', 'rw_w2', 'rw_a0', 'rw_a2', 'rw_g2', 'rw_k_k', 'rw_k_a'], dpre):
        g[n] = d

    dypre, du_a, g['s5_d'], g['s5_glu_w'], g['s5_glu_b'] = _tok_bwd(
        "s5_post_bwd", _f_s5_post, [ypre, u], s5_par, [ds5_out], [F32, F32], [True] * 3, tw)
    dxs = _s5_expand("d_s5_x", dypre, c_blk)
    dc_blk = _s5_block_grad("d_s5_c", dypre, xs)
    dbu, dlam_row = _s5_scan_bwd(dxs, xs, lam_row, tm)
    du = _s5_contract("d_s5_u", dbu, b_blk, add=du_a, out_dtype=BF16)
    db_blk = _s5_block_grad("d_s5_b", u, dbu)
    dbuild = _small_bwd("s5_build_bwd", _f_s5_build, build_ins, (db_blk, dc_blk))
    lam_cots = (dlam_row[:, :S5_LANES].reshape(gp), dlam_row[:, S5_LANES:].reshape(gp),
                dbuild[0].reshape(gp), dbuild[1].reshape(gp))
    g['s5_lam_re'], g['s5_lam_im'], g['s5_log_step'] = _small_bwd("s5_lam_bwd", _f_s5_lam, lam_ins, lam_cots)
    from_t = lambda a, perm: a.reshape(S5_GROUP, S5_GROUPS, S5_STATE).transpose(perm)
    g['s5_b_re'], g['s5_b_im'] = from_t(dbuild[2], (1, 2, 0)), from_t(dbuild[3], (1, 2, 0))
    g['s5_c_re'], g['s5_c_im'] = from_t(dbuild[4], (1, 0, 2)), from_t(dbuild[5], (1, 0, 2))

    dproj = jnp.concatenate([du, dz], axis=1)
    dw_in = _mm("d_w_in", xn, dproj, 'tn', out_dtype=WIRE)
    w_in_exchange = _exchange_plan([_to_shard_major(dw_in, SHARDED['w_in'])])
    dxn, (w_in_pieces,) = _mm("d_xn", dproj, wf['w_in'], 'nt', plan=w_in_exchange)
    pieces = dict(zip(LATE_NAMES, late_pieces), w_in=w_in_pieces)
    grad_x, g['norm_mix'] = _tok_bwd("norm_in_bwd", _f_norm_in, [x], [norm_mix], [dxn], [F32], [True], tw,
                                     add_to=(0, dx_a))
    return loss[0, 0], grad_x, g, pieces


def _step(x, p, target, w, m, v):
    shards = {n: _mat(w[n]).astype(BF16) for n in SHARDED_NAMES}
    early = _run_plan("gather_early", _gather_plan([shards[n] for n in EARLY_NAMES]))
    wf = {n: _from_shard_major(a, SHARDED[n]) for n, a in zip(EARLY_NAMES, early)}
    ws = {n: w[n] for n in SMALL_NAMES}

    loss, grad_x, g, pieces = _local_step(x[0], p[0, 0], target[0], wf, ws, shards)

    last_names = [n for n in SHARDED_NAMES if n not in pieces]
    last_plan = _exchange_plan([_to_shard_major(g[n], SHARDED[n]).astype(WIRE) for n in last_names],
                               _pack_small({n: g[n] for n in SMALL_NAMES}, tail=loss))
    *last_pieces, by_dev = _run_plan("exchange_last", last_plan)
    pieces.update(zip(last_names, last_pieces))
    (small_piece,) = _ew("add_devices_small", _sum_slots, [by_dev], [by_dev.shape[1:]])
    both, small_g = _share_cores([pieces[n] for n in SHARDED_NAMES], small_piece)

    kinds = [{}, {}, {}, {}]
    for n, gn in zip(SHARDED_NAMES, both):
        shard = _mat(w[n]).shape
        res = _ew("adamw_" + n, _adamw, [gn.reshape(shard), _mat(w[n]), _mat(m[n]), _mat(v[n])], [shard] * 3)
        for kind, a in zip(kinds, [gn] + list(res)):
            kind[n] = a.reshape(w[n].shape)
    flat = (8 * SMALL_ROWS, PACK_COLS)
    packed = [_pack_small({n: d[n] for n in SMALL_NAMES}).reshape(flat) for d in (w, m, v)]
    small_res = _ew("adamw_small", _adamw, [small_g.reshape(flat)] + packed, [flat] * 3)
    small_shapes = {n: w[n].shape for n in SMALL_NAMES}
    for kind, a in zip(kinds, [small_g] + list(small_res)):
        kind.update(_unpack_small(a, small_shapes))
    total = small_g.reshape(-1)[sum(math.prod(s) for s in small_shapes.values())]
    return (total, grad_x[None], *[kind[n] for kind in kinds for n in WEIGHT_NAMES])


def kernel(x, p, norm_mix, w_in, s5_lam_re, s5_lam_im, s5_log_step, s5_b_re, s5_b_im, s5_c_re, s5_c_im, s5_d, s5_glu_w, s5_glu_b, rw_shift_mu, rw_w0, rw_w2, rw_a0, rw_a2, rw_g2, rw_k_k, rw_k_a, rw_r_k, rw_ln_w, rw_ln_b, w_out, norm_ffn, ffn_w1, ffn_w3, ffn_w2, norm_ple, ple_gate_w, ple_up_w, final_norm, loss_target, m_norm_mix, m_w_in, m_s5_lam_re, m_s5_lam_im, m_s5_log_step, m_s5_b_re, m_s5_b_im, m_s5_c_re, m_s5_c_im, m_s5_d, m_s5_glu_w, m_s5_glu_b, m_rw_shift_mu, m_rw_w0, m_rw_w2, m_rw_a0, m_rw_a2, m_rw_g2, m_rw_k_k, m_rw_k_a, m_rw_r_k, m_rw_ln_w, m_rw_ln_b, m_w_out, m_norm_ffn, m_ffn_w1, m_ffn_w3, m_ffn_w2, m_norm_ple, m_ple_gate_w, m_ple_up_w, m_final_norm, v_norm_mix, v_w_in, v_s5_lam_re, v_s5_lam_im, v_s5_log_step, v_s5_b_re, v_s5_b_im, v_s5_c_re, v_s5_c_im, v_s5_d, v_s5_glu_w, v_s5_glu_b, v_rw_shift_mu, v_rw_w0, v_rw_w2, v_rw_a0, v_rw_a2, v_rw_g2, v_rw_k_k, v_rw_k_a, v_rw_r_k, v_rw_ln_w, v_rw_ln_b, v_w_out, v_norm_ffn, v_ffn_w1, v_ffn_w3, v_ffn_w2, v_norm_ple, v_ple_gate_w, v_ple_up_w, v_final_norm):
    args = dict(locals())
    w = {n: args[n] for n in WEIGHT_NAMES}
    m = {n: args["m_" + n] for n in WEIGHT_NAMES}
    v = {n: args["v_" + n] for n in WEIGHT_NAMES}
    return _step(x, p, loss_target, w, m, v)
```

```python
import functools
import math
from typing import Any, Callable, NamedTuple, Sequence

import jax
import jax.numpy as jnp
from jax import lax
from jax.experimental import pallas as pl
from jax.experimental.pallas import tpu as pltpu

F32 = jnp.float32
BF16 = jnp.bfloat16
MESH = pl.DeviceIdType.MESH

S5_WIDTH = 512
RWKV_WIDTH = 512
S5_GROUP = 16
S5_GROUPS = 32
S5_STATE = 64
S5_LANES = S5_GROUPS * S5_STATE
S5_TILE_GROUPS = 8
S5_TILES = S5_GROUPS // S5_TILE_GROUPS
S5_TILE_CH = S5_TILE_GROUPS * S5_GROUP
S5_TILE_LANES = S5_TILE_GROUPS * S5_STATE
HEAD = 64
HEADS = 8
DECAY_LORA = 64
AAA_LORA = 64
GATE_LORA = 128
FFN_HIDDEN = 2816
FFN_TILE = FFN_HIDDEN // 2
RMS_EPS = 1e-6
GN_EPS = 64e-5
L2_EPS = 1e-12
ADAM_LR = 0.001
ADAM_B1 = 0.9
ADAM_B2 = 0.999
ADAM_EPS = 1e-08
ADAM_WD = 0.01
ADAM_STEP = 10

WKV_CHUNK = 64
SCAN_UNROLL = 4
WIRE = jnp.bfloat16
WKV_PASSES = 1
VMEM_LIMIT_BYTES = 48 * 1024 * 1024
SHARE_VMEM_LIMIT_BYTES = 56 * 1024 * 1024
LANE = 128
PACK_COLS = 1024
SMALL_ROWS = 24

WEIGHT_NAMES = ['norm_mix', 'w_in', 's5_lam_re', 's5_lam_im', 's5_log_step', 's5_b_re', 's5_b_im', 's5_c_re',
                's5_c_im', 's5_d', 's5_glu_w', 's5_glu_b', 'rw_shift_mu', 'rw_w0', 'rw_w2', 'rw_a0', 'rw_a2',
                'rw_g2', 'rw_k_k', 'rw_k_a', 'rw_r_k', 'rw_ln_w', 'rw_ln_b', 'w_out', 'norm_ffn', 'ffn_w1',
                'ffn_w3', 'ffn_w2', 'norm_ple', 'ple_gate_w', 'ple_up_w', 'final_norm']
SHARDED = {'w_in': 1, 's5_glu_w': 0, 'rw_w2': 1, 'rw_a2': 1, 'rw_g2': 1, 'w_out': 0, 'ffn_w1': 1, 'ffn_w3': 1,
           'ffn_w2': 0, 'ple_gate_w': 0, 'ple_up_w': 1}
SHARDED_NAMES = [n for n in WEIGHT_NAMES if n in SHARDED]
LATE_NAMES = ['w_out', 'ffn_w1', 'ffn_w3', 'ffn_w2', 'ple_gate_w', 'ple_up_w']
EARLY_NAMES = [n for n in SHARDED_NAMES if n not in LATE_NAMES]
LATE_GATHER = {'s5_scan': ['w_out', 'ple_gate_w', 'ple_up_w'], 'wkv_fwd': ['ffn_w1', 'ffn_w3'], 'ffn_up': ['ffn_w2']}
SMALL_NAMES = [n for n in WEIGHT_NAMES if n not in SHARDED]


def _params(sem=None):
    return pltpu.CompilerParams(dimension_semantics=sem, vmem_limit_bytes=VMEM_LIMIT_BYTES)


def _tile(n, target):
    best = None
    for d in range(LANE, min(n, target) + 1, LANE):
        if n % d == 0:
            best = d
    return n if best is None else best


_NN = (((1,), (0,)), ((), ()))
_NT = (((1,), (1,)), ((), ()))
_TN = (((0,), (0,)), ((), ()))


def _split(a):
    a = a.astype(F32)
    hi = a.astype(BF16)
    return hi, (a - hi.astype(F32)).astype(BF16)


def _dg(a, b, dims, passes):
    dg = lambda p, q: lax.dot_general(p, q, dims, preferred_element_type=F32)
    if passes == 1:
        return dg(a.astype(BF16), b.astype(BF16))
    bh, bl = _split(b)
    return dg(a.astype(BF16), bh) + dg(a.astype(BF16), bl)


_DOT_BWD = {_NN: (("g", "b", _NT), ("a", "g", _TN)),
            _NT: (("g", "b", _NN), ("g", "a", _TN)),
            _TN: (("b", "g", _NT), ("a", "g", _NN))}


@functools.partial(jax.custom_vjp, nondiff_argnums=(2, 3))
def _dot(a, b, dims, passes):
    return _dg(a, b, dims, passes)


def _dot_fwd(a, b, dims, passes):
    return _dg(a, b, dims, passes), (a, b)


def _dot_bwd(dims, passes, res, g):
    env = {"a": res[0], "b": res[1], "g": g}
    return tuple(_dg(env[p], env[q], d, passes) for p, q, d in _DOT_BWD[dims])


_dot.defvjp(_dot_fwd, _dot_bwd)


def _bdot(x, w):
    return _dot(x, w, _NN, 1)


@jax.custom_vjp
def _shift_down(z):
    return pltpu.roll(z, 1, 0)


def _shift_down_fwd(z):
    return pltpu.roll(z, 1, 0), None


def _shift_down_bwd(_, g):
    return (pltpu.roll(g, g.shape[0] - 1, 0),)


_shift_down.defvjp(_shift_down_fwd, _shift_down_bwd)


def _head_sum_impl(x):
    r = lax.broadcasted_iota(jnp.int32, (LANE, LANE), 0) // HEAD
    c = lax.broadcasted_iota(jnp.int32, (LANE, LANE), 1) // HEAD
    ones = (r == c).astype(BF16)
    hi, lo = _split(x)
    dg = lambda p: lax.dot_general(p, ones, _NN, preferred_element_type=F32)
    tiles = [slice(j, j + LANE) for j in range(0, x.shape[1], LANE)]
    return jnp.concatenate([dg(hi[:, s]) + dg(lo[:, s]) for s in tiles], axis=1)


@jax.custom_vjp
def _head_sum(x):
    return _head_sum_impl(x)


_head_sum.defvjp(lambda x: (_head_sum_impl(x), None), lambda _, g: (_head_sum_impl(g),))


def _mm(name, a, b, mode, out_dtype=F32, tm=1024, tn=1024, tk=1536, plan=None):
    if mode == 'nn':
        (m, k), (_, n) = a.shape, b.shape
    elif mode == 'nt':
        (m, k), (n, _) = a.shape, b.shape
    else:
        (k, m), (_, n) = a.shape, b.shape
    tm, tn, tk = _tile(m, tm), _tile(n, tn), _tile(k, tk)
    nm, nn, nk = m // tm, n // tn, k // tk
    dims = {'nn': _NN, 'nt': _NT, 'tn': _TN}[mode]
    plan = _NO_PLAN if plan is None else plan
    parts, plan_in_specs, plan_out_shape, plan_out_specs, plan_sems = _carry(plan, 2, 1)

    def body(*refs):
        a_ref, b_ref, o_ref = refs[0], refs[1], refs[2 + len(plan.ins)]
        acc_ref = refs[3 + len(plan.ins) + len(plan.out_shape)]
        i, j, kk = pl.program_id(0), pl.program_id(1), pl.program_id(2)

        if plan is not _NO_PLAN:
            pl.when((i == 0) & (j == 0) & (kk == 0))(lambda: plan.start(*parts(refs)))

        @pl.when(kk == 0)
        def _():
            acc_ref[...] = jnp.zeros_like(acc_ref)

        acc_ref[...] += _dg(a_ref[...], b_ref[...], dims, 1)

        @pl.when(kk == nk - 1)
        def _():
            o_ref[...] = acc_ref[...].astype(o_ref.dtype)

        if plan is not _NO_PLAN:
            pl.when((i == nm - 1) & (j == nn - 1) & (kk == nk - 1))(lambda: plan.wait(*parts(refs)))

    if mode == 'tn':
        a_spec = pl.BlockSpec((tk, tm), lambda i, j, l: (l, i))
    else:
        a_spec = pl.BlockSpec((tm, tk), lambda i, j, l: (i, l))
    if mode == 'nt':
        b_spec = pl.BlockSpec((tn, tk), lambda i, j, l: (j, l))
    else:
        b_spec = pl.BlockSpec((tk, tn), lambda i, j, l: (l, j))
    res = pl.pallas_call(
        body, name=name,
        out_shape=[jax.ShapeDtypeStruct((m, n), out_dtype)] + plan_out_shape,
        grid=(nm, nn, nk),
        in_specs=[a_spec, b_spec] + plan_in_specs,
        out_specs=[pl.BlockSpec((tm, tn), lambda i, j, l: (i, j))] + plan_out_specs,
        scratch_shapes=[pltpu.VMEM((tm, tn), F32)] + plan_sems,
        compiler_params=_params(("parallel", "parallel", "arbitrary") if plan is _NO_PLAN else ("arbitrary",) * 3),
    )(a, b, *plan.ins)
    return res[0] if plan is _NO_PLAN else (res[0], res[1:])


def _mm_tiles(name, a, b, mode, out_shape, grid, a_spec, b_spec, o_spec, add=None, out_dtype=F32):
    dims = {'nn': _NN, 'nt': _NT, 'tn': _TN}[mode]
    nk = grid[2]
    extra = [] if add is None else [add]

    def body(a_ref, b_ref, *refs):
        o_ref, acc_ref = refs[len(extra):]
        kk = pl.program_id(2)

        @pl.when(kk == 0)
        def _():
            acc_ref[...] = refs[0][...].astype(F32) if extra else jnp.zeros_like(acc_ref)

        acc_ref[...] += _dg(a_ref[...], b_ref[...], dims, 1)

        @pl.when(kk == nk - 1)
        def _():
            o_ref[...] = acc_ref[...].astype(o_ref.dtype)

    return pl.pallas_call(
        body, name=name,
        out_shape=jax.ShapeDtypeStruct(out_shape, out_dtype),
        grid=grid, in_specs=[a_spec, b_spec] + [o_spec] * len(extra), out_specs=o_spec,
        scratch_shapes=[pltpu.VMEM(o_spec.block_shape, F32)],
        compiler_params=_params(("parallel", "parallel", "arbitrary")),
    )(a, b, *extra)


def _s5_expand(name, u, blk, tm=2048):
    t = u.shape[0]
    tm = min(tm, t)
    ch, ln, nt = S5_TILE_CH, S5_TILE_LANES, S5_TILES
    return _mm_tiles(name, u, blk, 'nn', (t, 2 * S5_LANES), (t // tm, 2 * nt, 1),
                     pl.BlockSpec((tm, ch), lambda i, j, l: (i, j % nt)),
                     pl.BlockSpec((ch, ln), lambda i, j, l: (j % nt, j // nt)),
                     pl.BlockSpec((tm, ln), lambda i, j, l: (i, j)), out_dtype=BF16)


def _s5_contract(name, x, blk, tm=2048, add=None, out_dtype=F32):
    t = x.shape[0]
    tm = min(tm, t)
    ch, ln, nt = S5_TILE_CH, S5_TILE_LANES, S5_TILES
    return _mm_tiles(name, x, blk, 'nt', (t, S5_WIDTH), (t // tm, nt, 2),
                     pl.BlockSpec((tm, ln), lambda i, j, l: (i, j + nt * l)),
                     pl.BlockSpec((ch, ln), lambda i, j, l: (j, l)),
                     pl.BlockSpec((tm, ch), lambda i, j, l: (i, j)), add=add, out_dtype=out_dtype)


def _s5_block_grad(name, u, x, tk=2048):
    t = u.shape[0]
    tk = min(tk, t)
    ch, ln, nt = S5_TILE_CH, S5_TILE_LANES, S5_TILES
    return _mm_tiles(name, u, x, 'tn', (S5_WIDTH, 2 * ln), (nt, 2, t // tk),
                     pl.BlockSpec((tk, ch), lambda i, j, l: (l, i)),
                     pl.BlockSpec((tk, ln), lambda i, j, l: (l, i + nt * j)),
                     pl.BlockSpec((ch, ln), lambda i, j, l: (i, j)))


def _full_spec(p):
    nd = p.ndim
    return pl.BlockSpec(p.shape, lambda i, nd=nd: (0,) * nd)


def _tok_fwd(name, fn, toks, params, outs, tm):
    t = toks[0].shape[0]
    nt, npar = len(toks), len(params)

    def body(*refs):
        tv = [r[...].astype(F32) for r in refs[:nt]]
        pv = [r[...].astype(F32) for r in refs[nt:nt + npar]]
        res = fn(*tv, *pv)
        for r, v in zip(refs[nt + npar:], res):
            r[...] = v.astype(r.dtype)

    return pl.pallas_call(
        body, name=name,
        out_shape=[jax.ShapeDtypeStruct((t, w), d) for w, d in outs],
        grid=(t // tm,),
        in_specs=[pl.BlockSpec((tm, a.shape[1]), lambda i: (i, 0)) for a in toks] + [_full_spec(p) for p in params],
        out_specs=[pl.BlockSpec((tm, w), lambda i: (i, 0)) for w, _ in outs],
        compiler_params=_params(("parallel",)),
    )(*toks, *params)


def _tok_bwd(name, fn, toks, params, cots, dtok, dpar, tm, acc_out=None, add_to=None):
    t = toks[0].shape[0]
    nt, npar = len(toks), len(params)
    cot_arrays = [c for c in cots if c is not None]
    ncot = len(cot_arrays)
    extra = [] if add_to is None else [add_to[1]]
    dtok_idx = [i for i, d in enumerate(dtok) if d is not None]
    dpar_idx = [i for i, d in enumerate(dpar) if d]

    def body(*refs):
        pos = 0
        tin = refs[pos:pos + nt]; pos += nt
        pin = refs[pos:pos + npar]; pos += npar
        cin = refs[pos:pos + ncot]; pos += ncot
        ein = refs[pos:pos + len(extra)]; pos += len(extra)
        dto = refs[pos:pos + len(dtok_idx)]; pos += len(dtok_idx)
        dpo = refs[pos:pos + len(dpar_idx)]; pos += len(dpar_idx)
        acc = refs[pos] if acc_out is not None else None
        first = pl.program_id(0) == 0

        tv = [r[...].astype(F32) for r in tin]
        pv = [r[...].astype(F32) for r in pin]
        res, vjp = jax.vjp(fn, *tv, *pv)
        cit = iter(cin)
        cs = tuple(jnp.ones_like(o) if c is None else next(cit)[...].astype(F32) for c, o in zip(cots, res))
        g = vjp(cs)
        for r, i in zip(dto, dtok_idx):
            v = g[i]
            if add_to is not None and add_to[0] == i:
                v = v + ein[0][...].astype(F32)
            r[...] = v.astype(r.dtype)

        @pl.when(first)
        def _():
            for r in dpo:
                r[...] = jnp.zeros_like(r)
            if acc is not None:
                acc[...] = jnp.zeros_like(acc)

        for r, i in zip(dpo, dpar_idx):
            r[...] += g[nt + i]
        if acc is not None:
            acc[...] += res[acc_out]

    out_shape = [jax.ShapeDtypeStruct(toks[i].shape, dtok[i]) for i in dtok_idx]
    out_shape += [jax.ShapeDtypeStruct(params[i].shape, F32) for i in dpar_idx]
    out_specs = [pl.BlockSpec((tm, toks[i].shape[1]), lambda i_: (i_, 0)) for i in dtok_idx]
    out_specs += [_full_spec(params[i]) for i in dpar_idx]
    if acc_out is not None:
        out_shape.append(jax.ShapeDtypeStruct((1, 1), F32))
        out_specs.append(pl.BlockSpec((1, 1), lambda i_: (0, 0)))
    tok_spec = lambda a: pl.BlockSpec((tm, a.shape[1]), lambda i_: (i_, 0))
    return pl.pallas_call(
        body, name=name,
        out_shape=out_shape,
        grid=(t // tm,),
        in_specs=[tok_spec(a) for a in toks] + [_full_spec(p) for p in params]
        + [tok_spec(c) for c in cot_arrays] + [tok_spec(e) for e in extra],
        out_specs=out_specs,
        compiler_params=_params(("arbitrary",)),
    )(*toks, *params, *cot_arrays, *extra)


def _small_fwd(name, fn, ins, outs):
    n = len(ins)

    def body(*refs):
        res = fn(*[r[...] for r in refs[:n]])
        for r, v in zip(refs[n:], res):
            r[...] = v.astype(r.dtype)

    return pl.pallas_call(
        body, name=name,
        out_shape=[jax.ShapeDtypeStruct(s, d) for s, d in outs],
        compiler_params=_params(),
    )(*ins)


def _small_bwd(name, fn, ins, cots):
    n = len(ins)

    def body(*refs):
        _, vjp = jax.vjp(fn, *[r[...] for r in refs[:n]])
        g = vjp(tuple(r[...] for r in refs[n:n + len(cots)]))
        for r, v in zip(refs[n + len(cots):], g):
            r[...] = v

    return pl.pallas_call(
        body, name=name,
        out_shape=[jax.ShapeDtypeStruct(a.shape, F32) for a in ins],
        compiler_params=_params(),
    )(*ins, *cots)


def _rms(x, g):
    return x * lax.rsqrt(jnp.mean(x * x, axis=-1, keepdims=True) + RMS_EPS) * g


def _f_norm_in(x, g):
    return (_rms(x, g),)


def _f_mix_res(x, mixed, g):
    h1 = x + mixed
    return h1, _rms(h1, g)


def _f_ffn_act(a13):
    a1, a3 = a13[:, :FFN_HIDDEN], a13[:, FFN_HIDDEN:]
    return (jax.nn.silu(a1) * a3,)


def _f_ffn_res(h1, ffo, g):
    h2 = h1 + ffo
    return h2, _rms(h2, g)


def _f_loss(h2, gpre, pu, target, g):
    h3 = h2 + jax.nn.sigmoid(gpre) * pu
    y = _rms(h3, g)
    err = jnp.square(y - target)
    return (0.5 * jnp.sum(jnp.mean(err, axis=-1, keepdims=True), axis=0, keepdims=True),)


def _f_s5_post(ypre, u, d, glu_w, glu_b):
    z = jax.nn.gelu(ypre + u * d)
    return (z * jax.nn.sigmoid(_bdot(z, glu_w) + glu_b),)


def _softplus(x):
    return jnp.maximum(x, 0.0) + jnp.log(1.0 + jnp.exp(-jnp.abs(x)))


def _f_rw_pre(z, carry, shift_mu, w0, w2, a0, a2, g2, k_k, k_a):
    rw = RWKV_WIDTH
    first_row = lax.broadcasted_iota(jnp.int32, z.shape, 0) == 0
    prev = jnp.where(first_row, carry, _shift_down(z))
    zs = z + (prev - z) * shift_mu
    o1, o2 = 3 * rw + DECAY_LORA, 3 * rw + DECAY_LORA + AAA_LORA
    r, k, v = zs[:, :rw], zs[:, rw:2 * rw], zs[:, 2 * rw:3 * rw]
    wl, al, gl = zs[:, 3 * rw:o1], zs[:, o1:o2], zs[:, o2:]
    w = -_softplus(-(w0 + _bdot(jnp.tanh(wl), w2))) - 0.5
    log_decay = -jnp.exp(w)
    a = jax.nn.sigmoid(a0 + _bdot(al, a2))
    g = _bdot(jax.nn.sigmoid(gl), g2)
    kk = k * k_k
    norm = jnp.sqrt(_head_sum(kk * kk))
    kk = kk / jnp.maximum(norm, L2_EPS)
    kp = k * (1.0 + (a - 1.0) * k_a)
    return r, log_decay, kp, v, -kk, kk * a, g


def _f_rw_post(y, r, kp, v, g, ln_w, ln_b, r_k):
    yc = y - _head_sum(y) * (1.0 / HEAD)
    var = _head_sum(yc * yc) * (1.0 / HEAD)
    yn = yc * lax.rsqrt(var + GN_EPS) * ln_w + ln_b
    bonus = _head_sum(r * kp * r_k) * v
    return ((yn + bonus) * g,)


def _f_s5_lam(lam_re, lam_im, log_step):
    step = jnp.exp(log_step)
    dr, di = lam_re * step, lam_im * step
    e = jnp.exp(dr)
    lbr, lbi = e * jnp.cos(di), e * jnp.sin(di)
    nr, ni = lbr - 1.0, lbi
    den = lam_re * lam_re + lam_im * lam_im
    return lbr, lbi, (nr * lam_re + ni * lam_im) / den, (ni * lam_re - nr * lam_im) / den


def _f_s5_build(coef_r, coef_i, btr, bti, ctr, cti):
    bbr = coef_r * btr - coef_i * bti
    bbi = coef_r * bti + coef_i * btr
    shape = (S5_WIDTH, S5_TILE_LANES)
    rows = (lax.broadcasted_iota(jnp.int32, shape, 0) % S5_TILE_CH) // S5_GROUP
    cols = lax.broadcasted_iota(jnp.int32, shape, 1) // S5_STATE
    mask = (rows == cols).astype(F32)

    def blocks(m):
        per_tile = [m[:, S5_TILE_LANES * i:S5_TILE_LANES * (i + 1)] for i in range(S5_TILES)]
        return jnp.concatenate([t for t in per_tile for _ in range(S5_TILE_GROUPS)], axis=0) * mask

    return (jnp.concatenate([blocks(bbr), blocks(bbi)], axis=1),
            jnp.concatenate([blocks(ctr), -blocks(cti)], axis=1))


HALO = 8


def _rw_pre_specs(z, params, tm, order):
    halo_blocks = tm // HALO
    return ([pl.BlockSpec((tm, z.shape[1]), lambda i: (order(i), 0)),
             pl.BlockSpec((HALO, z.shape[1]), lambda i: (jnp.maximum(order(i) * halo_blocks - 1, 0), 0))]
            + [_full_spec(p) for p in params])


def _rw_pre_fwd(z, params, tm):
    t = z.shape[0]
    npar = len(params)

    def body(z_ref, halo_ref, *refs):
        carry = jnp.where(pl.program_id(0) == 0, 0.0, halo_ref[pl.ds(HALO - 1, 1), :])
        res = _f_rw_pre(z_ref[...], carry, *[r[...].astype(F32) for r in refs[:npar]])
        for r, v in zip(refs[npar:], res):
            r[...] = v

    return pl.pallas_call(
        body, name="rw_pre",
        out_shape=[jax.ShapeDtypeStruct((t, RWKV_WIDTH), F32)] * 7,
        grid=(t // tm,),
        in_specs=_rw_pre_specs(z, params, tm, lambda i: i),
        out_specs=[pl.BlockSpec((tm, RWKV_WIDTH), lambda i: (i, 0))] * 7,
        compiler_params=_params(("parallel",)),
    )(z, z, *params)


def _rw_pre_bwd(z, params, cots, tm):
    t = z.shape[0]
    nt = t // tm
    npar = len(params)
    order = lambda i: nt - 1 - i
    flat_cots = [a for group in cots for a in group]
    ncot = len(flat_cots)

    def body(z_ref, halo_ref, *refs):
        pin, cin = refs[:npar], list(refs[npar:npar + ncot])
        dz_ref = refs[npar + ncot]
        dpo = refs[npar + ncot + 1:npar + ncot + 1 + npar]
        dcarry_ref = refs[npar + ncot + 1 + npar]
        i = pl.program_id(0)

        @pl.when(i == 0)
        def _():
            dcarry_ref[...] = jnp.zeros_like(dcarry_ref)
            for r in dpo:
                r[...] = jnp.zeros_like(r)

        carry = jnp.where(i == nt - 1, 0.0, halo_ref[pl.ds(HALO - 1, 1), :])
        _, vjp = jax.vjp(_f_rw_pre, z_ref[...], carry, *[r[...].astype(F32) for r in pin])
        g = vjp(tuple(sum(cin.pop(0)[...] for _ in group) for group in cots))
        last_row = lax.broadcasted_iota(jnp.int32, z_ref.shape, 0) == tm - 1
        dz_ref[...] = (g[0] + jnp.where(last_row, dcarry_ref[...], 0.0)).astype(dz_ref.dtype)
        dcarry_ref[...] = g[1]
        for r, v in zip(dpo, g[2:]):
            r[...] += v

    tok = lambda w: pl.BlockSpec((tm, w), lambda i: (order(i), 0))
    return pl.pallas_call(
        body, name="rw_pre_bwd",
        out_shape=[jax.ShapeDtypeStruct(z.shape, BF16)] + [jax.ShapeDtypeStruct(p.shape, F32) for p in params],
        grid=(nt,),
        in_specs=_rw_pre_specs(z, params, tm, order) + [tok(RWKV_WIDTH)] * ncot,
        out_specs=[tok(z.shape[1])] + [_full_spec(p) for p in params],
        scratch_shapes=[pltpu.VMEM((1, z.shape[1]), F32)],
        compiler_params=_params(("arbitrary",)),
    )(z, z, *params, *flat_cots)


def _s5_scan(bu, lam, tm, plan):
    t, w = bu.shape
    h = w // 2
    nt = t // tm
    parts, plan_in_specs, plan_out_shape, plan_out_specs, plan_sems = _carry(plan, 2, 1)

    def body(*refs):
        bu_ref, lam_ref = refs[:2]
        xb_ref = refs[2 + len(plan.ins)]
        carry_ref = refs[3 + len(plan.ins) + len(plan.out_shape)]
        x_ref, refs = refs[-1], refs[:-1]

        @pl.when(pl.program_id(0) == 0)
        def _():
            carry_ref[...] = jnp.zeros_like(carry_ref)
            plan.start(*parts(refs))

        lr, li = lam_ref[:, :h], lam_ref[:, h:]
        x_ref[...] = bu_ref[...].astype(F32)

        def step(s, c):
            cr, ci = c
            row = pl.ds(s, 1)
            nr = lr * cr - li * ci + x_ref[row, :h]
            ni = lr * ci + li * cr + x_ref[row, h:]
            x_ref[row, :h] = nr
            x_ref[row, h:] = ni
            return nr, ni

        cr, ci = lax.fori_loop(0, tm, step, (carry_ref[:, :h], carry_ref[:, h:]), unroll=SCAN_UNROLL)
        carry_ref[:, :h] = cr
        carry_ref[:, h:] = ci
        xb_ref[...] = x_ref[...].astype(BF16)

        @pl.when(pl.program_id(0) == nt - 1)
        def _():
            plan.wait(*parts(refs))

    spec = pl.BlockSpec((tm, w), lambda i: (i, 0))
    res = pl.pallas_call(
        body, name="s5_scan",
        out_shape=[jax.ShapeDtypeStruct((t, w), BF16)] + plan_out_shape,
        grid=(nt,),
        in_specs=[spec, pl.BlockSpec((1, w), lambda i: (0, 0))] + plan_in_specs,
        out_specs=[spec] + plan_out_specs,
        scratch_shapes=[pltpu.VMEM((1, w), F32)] + plan_sems + [pltpu.VMEM((tm, w), F32)],
        compiler_params=_params(("arbitrary",)),
    )(bu, lam, *plan.ins)
    return res[0], res[1:]


def _s5_scan_bwd(dx, xb, lam, tm):
    t, w = dx.shape
    h = w // 2
    nt = t // tm
    halo = BF16_SUBLANES

    rows8 = 8

    def body(dx_ref, xb_ref, halo_ref, lam_ref, dbu_out_ref, dlam_ref, carry_ref, dbu_ref, xp_ref):
        @pl.when(pl.program_id(0) == 0)
        def _():
            carry_ref[...] = jnp.zeros_like(carry_ref)
            dlam_ref[...] = jnp.zeros_like(dlam_ref)

        lr, li = lam_ref[:, :h], lam_ref[:, h:]
        dbu_ref[...] = dx_ref[...].astype(F32)

        def step(s, c):
            cr, ci = c
            row = pl.ds(tm - 1 - s, 1)
            nr = lr * cr + li * ci + dbu_ref[row, :h]
            ni = lr * ci - li * cr + dbu_ref[row, h:]
            dbu_ref[row, :h] = nr
            dbu_ref[row, h:] = ni
            return nr, ni

        cr, ci = lax.fori_loop(0, tm, step, (carry_ref[:, :h], carry_ref[:, h:]), unroll=SCAN_UNROLL)
        carry_ref[:, :h] = cr
        carry_ref[:, h:] = ci
        halo_rows = lax.broadcasted_iota(jnp.int32, (halo, w), 0)
        before = jnp.sum(jnp.where(halo_rows == halo - 1, halo_ref[...].astype(F32), 0.0), axis=0, keepdims=True)
        before = jnp.where(pl.program_id(0) == nt - 1, 0.0, before)
        first_row = lax.broadcasted_iota(jnp.int32, (tm, w), 0) == 0
        xp_ref[...] = jnp.where(first_row, before, pltpu.roll(xb_ref[...].astype(F32), 1, 0))

        def accumulate(s, acc):
            ar, ai = acc
            rows = pl.ds(pl.multiple_of(s * rows8, rows8), rows8)
            gr, gi = dbu_ref[rows, :h], dbu_ref[rows, h:]
            pr, pi_ = xp_ref[rows, :h], xp_ref[rows, h:]
            return ar + (gr * pr + gi * pi_), ai + (gi * pr - gr * pi_)

        zero = jnp.zeros((rows8, h), F32)
        ar, ai = lax.fori_loop(0, tm // rows8, accumulate, (zero, zero))
        dlam_ref[:, :h] += jnp.sum(ar, axis=0, keepdims=True)
        dlam_ref[:, h:] += jnp.sum(ai, axis=0, keepdims=True)
        dbu_out_ref[...] = dbu_ref[...].astype(BF16)

    spec = pl.BlockSpec((tm, w), lambda i: (nt - 1 - i, 0))
    halo_spec = pl.BlockSpec((halo, w), lambda i: (jnp.maximum((nt - 1 - i) * (tm // halo) - 1, 0), 0))
    row_spec = pl.BlockSpec((1, w), lambda i: (0, 0))
    return pl.pallas_call(
        body, name="s5_scan_bwd",
        out_shape=[jax.ShapeDtypeStruct((t, w), BF16), jax.ShapeDtypeStruct((1, w), F32)],
        grid=(nt,),
        in_specs=[spec, spec, halo_spec, row_spec],
        out_specs=[spec, row_spec],
        scratch_shapes=[pltpu.VMEM((1, w), F32), pltpu.VMEM((tm, w), F32), pltpu.VMEM((tm, w), F32)],
        compiler_params=_params(("arbitrary",)),
    )(dx, xb, xb, lam)


def _unit_lower_inverses_impl(ns):
    c = ns[0].shape[0]
    eye = (lax.broadcasted_iota(jnp.int32, (c, c), 0) == lax.broadcasted_iota(jnp.int32, (c, c), 1)).astype(F32)
    inv = [eye + n for n in ns]
    pw = [_dg(n, n, _NN, WKV_PASSES) for n in ns]
    for _ in range(int(math.log2(c)) - 2):
        both = [_dg(jnp.concatenate([i, q], axis=0), q, _NN, WKV_PASSES) for i, q in zip(inv, pw)]
        inv = [i + q[:c] for i, q in zip(inv, both)]
        pw = [q[c:] for q in both]
    return tuple(i + _dg(i, q, _NN, WKV_PASSES) for i, q in zip(inv, pw))


@jax.custom_vjp
def _unit_lower_inverses(ns):
    return _unit_lower_inverses_impl(ns)


def _unit_lower_inverses_fwd(ns):
    inv = _unit_lower_inverses_impl(ns)
    return inv, inv


def _unit_lower_inverses_bwd(inv, g):
    left = [_dg(i, gi, _TN, WKV_PASSES) for i, gi in zip(inv, g)]
    return (tuple(_dg(q, i, _NT, WKV_PASSES) for q, i in zip(left, inv)),)


_unit_lower_inverses.defvjp(_unit_lower_inverses_fwd, _unit_lower_inverses_bwd)


def _wkv_chunks(s0, r, lw, k, v, a, b):
    c = r[0].shape[0]
    row = lax.broadcasted_iota(jnp.int32, (c, c), 0)
    col = lax.broadcasted_iota(jnp.int32, (c, c), 1)
    incl, strict = col <= row, col < row
    tri = incl.astype(F32)
    each = lambda f, *xs: [f(*t) for t in zip(*xs)]
    stack = lambda p, q: jnp.concatenate([p, q], axis=0)
    dot = lambda p, q, dims=_NN: _dot(p, q, dims, WKV_PASSES)
    lc = each(lambda l: _dot(tri, l, _NN, 2), lw)
    e_neg = each(lambda l: jnp.exp(-l), lc)
    ar = each(lambda x, z, l, w: stack(x * jnp.exp(l - w), z * jnp.exp(l)), a, r, lc, lw)
    bk = each(lambda x, z, e: stack(x * e, z * e), b, k, e_neg)
    m = each(lambda p, q: dot(p, q, _NT), ar, bk)
    mab = each(lambda q: jnp.where(strict, q[:c, :c], 0.0), m)
    mak_mrk = each(lambda q: stack(jnp.where(strict, q[:c, c:], 0.0), jnp.where(incl, q[c:, c:], 0.0)), m)
    mrb = each(lambda q: jnp.where(incl, q[c:, :c], 0.0), m)
    xy = each(lambda p, s, q, z: dot(p, s, _NT) + dot(q, z), ar, s0, mak_mrk, v)
    inv = _unit_lower_inverses(tuple(mab))
    u = each(lambda i, q: dot(i, q[:c]), inv, xy)
    y = each(lambda q, z, p: q[c:] + dot(z, p), xy, mrb, u)
    e_tot = each(lambda l: jnp.exp(jnp.sum(l, axis=0, keepdims=True)), lw)
    s1 = each(lambda s, p, z, q, e: (s + dot(stack(p, z), q, _TN)) * e, s0, u, v, bk, e_tot)
    return y, s1


def _carry(plan, n_args, n_outs):
    n_in, n_out = len(plan.ins), len(plan.out_shape)

    def parts(refs):
        base = n_args + n_in + n_outs
        return refs[n_args:n_args + n_in], refs[base:base + n_out], refs[base + n_out + 1:]

    return parts, [HBM_SPEC] * n_in, list(plan.out_shape), [HBM_SPEC] * n_out, list(plan.sems)


def _head_cols(ref):
    return tuple(ref[:, h * HEAD:(h + 1) * HEAD] for h in range(HEADS))


def _wkv_fwd(seqs, plan):
    t, w = seqs[0].shape
    c, n = WKV_CHUNK, HEAD
    nc = t // c
    parts, plan_in_specs, plan_out_shape, plan_out_specs, plan_sems = _carry(plan, 6, 2)

    def body(*refs):
        ins, (y_ref, ck_ref) = refs[:6], refs[6 + len(plan.ins):8 + len(plan.ins)]
        s_ref = refs[8 + len(plan.ins) + len(plan.out_shape)]

        @pl.when(pl.program_id(0) == 0)
        def _():
            s_ref[...] = jnp.zeros_like(s_ref)
            plan.start(*parts(refs))

        s0 = tuple(s_ref[h] for h in range(HEADS))
        ys, s1 = _wkv_chunks(s0, *[_head_cols(r) for r in ins])
        for h in range(HEADS):
            ck_ref[0, h] = s0[h]
            y_ref[:, h * n:(h + 1) * n] = ys[h]
            s_ref[h] = s1[h]

        @pl.when(pl.program_id(0) == nc - 1)
        def _():
            plan.wait(*parts(refs))

    spec = pl.BlockSpec((c, w), lambda i: (i, 0))
    res = pl.pallas_call(
        body, name="wkv_fwd",
        out_shape=[jax.ShapeDtypeStruct((t, w), F32), jax.ShapeDtypeStruct((nc, HEADS, n, n), F32)] + plan_out_shape,
        grid=(nc,),
        in_specs=[spec] * 6 + plan_in_specs,
        out_specs=[spec, pl.BlockSpec((1, HEADS, n, n), lambda i: (i, 0, 0, 0))] + plan_out_specs,
        scratch_shapes=[pltpu.VMEM((HEADS, n, n), F32)] + plan_sems,
        compiler_params=_params(("arbitrary",)),
    )(*seqs, *plan.ins)
    return res[0], res[1], res[2:]


def _wkv_bwd(seqs, ck, dy, plan):
    t, w = seqs[0].shape
    c, n = WKV_CHUNK, HEAD
    nc = t // c
    parts, plan_in_specs, plan_out_shape, plan_out_specs, plan_sems = _carry(plan, 8, 6)

    def body(*refs):
        ins, ck_ref, dy_ref = refs[:6], refs[6], refs[7]
        outs = refs[8 + len(plan.ins):14 + len(plan.ins)]
        ds_ref = refs[14 + len(plan.ins) + len(plan.out_shape)]

        @pl.when(pl.program_id(0) == 0)
        def _():
            ds_ref[...] = jnp.zeros_like(ds_ref)
            plan.start(*parts(refs))

        s0 = tuple(ck_ref[0, h] for h in range(HEADS))
        _, vjp = jax.vjp(_wkv_chunks, s0, *[_head_cols(r) for r in ins])
        g = vjp((list(_head_cols(dy_ref)), [ds_ref[h] for h in range(HEADS)]))
        for h in range(HEADS):
            ds_ref[h] = g[0][h]
            for o, d in zip(outs, g[1:]):
                o[:, h * n:(h + 1) * n] = d[h]

        @pl.when(pl.program_id(0) == nc - 1)
        def _():
            plan.wait(*parts(refs))

    spec = pl.BlockSpec((c, w), lambda i: (nc - 1 - i, 0))
    res = pl.pallas_call(
        body, name="wkv_bwd",
        out_shape=[jax.ShapeDtypeStruct((t, w), F32)] * 6 + plan_out_shape,
        grid=(nc,),
        in_specs=[spec] * 6 + [pl.BlockSpec((1, HEADS, n, n), lambda i: (nc - 1 - i, 0, 0, 0)), spec] + plan_in_specs,
        out_specs=[spec] * 6 + plan_out_specs,
        scratch_shapes=[pltpu.VMEM((HEADS, n, n), F32)] + plan_sems,
        compiler_params=_params(("arbitrary",)),
    )(*seqs, ck, dy, *plan.ins)
    return res[:6], res[6:]


def _coords():
    return lax.axis_index("x"), lax.axis_index("y"), lax.axis_index("c")


def _flip(v, f):
    return 1 - v if f else v


_CHIP_FLIPS = [(1, 0), (0, 1), (1, 1)]
_DEV_FLIPS = [(fx, fy, fc) for fx in (0, 1) for fy in (0, 1) for fc in (0, 1) if (fx, fy, fc) != (0, 0, 0)]
HBM_SPEC = pl.BlockSpec(memory_space=pl.ANY)


def _chip_peer(k, x, y):
    fx, fy = _CHIP_FLIPS[k]
    return _flip(x, fx), _flip(y, fy)


def _dev_peer(k, x, y, c):
    fx, fy, fc = _DEV_FLIPS[k]
    return _flip(x, fx), _flip(y, fy), _flip(c, fc)


def _rows_of_core(ref, core):
    h = ref.shape[-2] // 2
    rows = pl.ds(pl.multiple_of(core * h, 8), h)
    return ref.at[rows, :] if len(ref.shape) == 2 else ref.at[:, rows, :]


class _Plan(NamedTuple):
    ins: Sequence[Any]
    out_shape: Sequence[Any]
    sems: Sequence[Any]
    start: Callable
    wait: Callable


_NO_PLAN = _Plan([], [], [], lambda *_: None, lambda *_: None)


def _run_plan(name, plan):
    n_in, n_out = len(plan.ins), len(plan.out_shape)

    def body(*refs):
        parts = refs[:n_in], refs[n_in:n_in + n_out], refs[n_in + n_out:]
        plan.start(*parts)
        plan.wait(*parts)

    return pl.pallas_call(
        body, name=name, out_shape=list(plan.out_shape),
        in_specs=[HBM_SPEC] * n_in, out_specs=[HBM_SPEC] * n_out, scratch_shapes=list(plan.sems),
    )(*plan.ins)


def _gather_plan(shards):
    n = len(shards)

    def copies(srcs, outs, sems):
        send_sems, recv_sems, local_sems = sems
        x, y, c = _coords()
        me = 2 * x + y

        def remote(i, k, arriving):
            px, py = _chip_peer(k, x, y)
            return pltpu.make_async_remote_copy(
                src_ref=srcs[i], dst_ref=outs[i].at[2 * px + py if arriving else me],
                send_sem=send_sems.at[i, k], recv_sem=recv_sems.at[i, k],
                device_id=(px, py, c), device_id_type=MESH)

        own = [pltpu.make_async_copy(srcs[i], outs[i].at[me], local_sems.at[i]) for i in range(n)]
        pairs = [(i, k) for k in range(3) for i in range(n)]
        return own, [remote(i, k, False) for i, k in pairs], [remote(i, k, True) for i, k in pairs]

    return _Plan(
        ins=shards, out_shape=[jax.ShapeDtypeStruct((4,) + s.shape, s.dtype) for s in shards],
        sems=[pltpu.SemaphoreType.DMA((n, 3)), pltpu.SemaphoreType.DMA((n, 3)), pltpu.SemaphoreType.DMA((n,))],
        start=functools.partial(_start_copies, copies), wait=functools.partial(_wait_copies, copies))


def _start_copies(copies, ins, outs, sems):
    own, sends, _ = copies(ins, outs, sems)
    for cp in own + sends:
        cp.start()


def _wait_copies(copies, ins, outs, sems):
    own, sends, arrivals = copies(ins, outs, sems)
    for cp in arrivals:
        cp.wait_recv()
    for cp in sends:
        cp.wait_send()
    for cp in own:
        cp.wait()


def _exchange_plan(gs, small=None):
    n = len(gs)
    arrays = list(gs) + ([] if small is None else [small])

    def copies(srcs, outs, sems):
        send_sems, recv_sems, local_sems = sems
        x, y, c = _coords()
        me = 4 * x + 2 * y + c

        def piece(i, px, py, pc):
            if i == n:
                return srcs[i].at[4 * px + 2 * py + pc]
            return _rows_of_core(srcs[i].at[2 * px + py], pc)

        def remote(i, k, arriving):
            px, py, pc = _dev_peer(k, x, y, c)
            return pltpu.make_async_remote_copy(
                src_ref=piece(i, px, py, pc), dst_ref=outs[i].at[4 * px + 2 * py + pc if arriving else me],
                send_sem=send_sems.at[i, k], recv_sem=recv_sems.at[i, k],
                device_id=(px, py, pc), device_id_type=MESH)

        own = [pltpu.make_async_copy(piece(i, x, y, c), outs[i].at[me], local_sems.at[i]) for i in range(len(arrays))]
        pairs = [(i, k) for k in range(7) for i in range(len(arrays))]
        return own, [remote(i, k, False) for i, k in pairs], [remote(i, k, True) for i, k in pairs]

    out_shape = [jax.ShapeDtypeStruct((8, g.shape[1] // 2, g.shape[2]), g.dtype) for g in gs]
    if small is not None:
        out_shape.append(jax.ShapeDtypeStruct(small.shape, small.dtype))
    m = len(arrays)
    return _Plan(
        ins=arrays, out_shape=out_shape,
        sems=[pltpu.SemaphoreType.DMA((m, 7)), pltpu.SemaphoreType.DMA((m, 7)), pltpu.SemaphoreType.DMA((m,))],
        start=functools.partial(_start_copies, copies), wait=functools.partial(_wait_copies, copies))


REDUCE_ROWS = 32


def _share_cores(pieces, small):
    n = len(pieces)
    halves = [jax.ShapeDtypeStruct(p.shape[1:], F32) for p in pieces]

    def body(*refs):
        srcs, small_parts, outs, small_out = refs[:n], refs[n], refs[n + 1:2 * n + 1], refs[2 * n + 1]
        parts = refs[2 * n + 2:3 * n + 2]
        mine, theirs = refs[3 * n + 2:4 * n + 2], refs[4 * n + 2:5 * n + 2]
        small_buf, small_src = refs[5 * n + 2], refs[5 * n + 3]
        send_sems, recv_sems, ssend, srecv, local_sems = refs[5 * n + 4:]
        x, y, c = _coords()
        me = 4 * x + 2 * y + c
        stage_small = pltpu.make_async_copy(small_parts, small_buf, local_sems.at[2 * n])
        stage_small.start()

        def reduce(i):
            rows_total = parts[i].shape[1]
            step_rows = min(REDUCE_ROWS, rows_total)

            def step(s, carry):
                rows = pl.ds(pl.multiple_of(s * step_rows, step_rows), step_rows)
                total = parts[i][0, rows, :].astype(F32)
                for d in range(1, 8):
                    total = total + parts[i][d, rows, :].astype(F32)
                mine[i][rows, :] = total
                return carry

            lax.fori_loop(0, rows_total // step_rows, step, 0)

        def big(i):
            return pltpu.make_async_remote_copy(
                src_ref=mine[i], dst_ref=theirs[i], send_sem=send_sems.at[i], recv_sem=recv_sems.at[i],
                device_id=(x, y, 1 - c), device_id_type=MESH)

        def tiny(k, arriving):
            px, py, pc = _dev_peer(k, x, y, c)
            return pltpu.make_async_remote_copy(
                src_ref=small_src, dst_ref=small_out.at[4 * px + 2 * py + pc if arriving else me],
                send_sem=ssend.at[k], recv_sem=srecv.at[k], device_id=(px, py, pc), device_id_type=MESH)

        stage_small.wait()
        small_src[...] = _sum_slots(small_buf[...])[0]
        small_sends = [tiny(k, False) for k in range(7)]
        own_small = pltpu.make_async_copy(small_src, small_out.at[me], local_sems.at[2 * n])
        stage = [pltpu.make_async_copy(srcs[i], parts[i], local_sems.at[i]) for i in range(n)]
        for cp in small_sends + [own_small] + stage:
            cp.start()
        sends = []
        for i in range(n):
            stage[i].wait()
            reduce(i)
            sends.append(big(i))
            sends[-1].start()
        store = [pltpu.make_async_copy(mine[i], outs[i].at[c], local_sems.at[i]) for i in range(n)]
        for cp in store:
            cp.start()
        for i in range(n):
            big(i).wait_recv()
            store.append(pltpu.make_async_copy(theirs[i], outs[i].at[1 - c], local_sems.at[n + i]))
            store[-1].start()
        for k in range(7):
            tiny(k, True).wait_recv()
        for cp in sends + small_sends:
            cp.wait_send()
        for cp in store + [own_small]:
            cp.wait()

    staged = [pltpu.VMEM(s.shape, s.dtype) for s in halves]
    res = pl.pallas_call(
        body, name="share_cores",
        out_shape=[jax.ShapeDtypeStruct((2,) + s.shape, s.dtype) for s in halves]
        + [jax.ShapeDtypeStruct(small.shape, small.dtype)],
        in_specs=[HBM_SPEC] * (n + 1), out_specs=[HBM_SPEC] * (n + 1),
        scratch_shapes=[pltpu.VMEM(p.shape, p.dtype) for p in pieces] + staged + staged + [
            pltpu.VMEM(small.shape, small.dtype), pltpu.VMEM(small.shape[1:], small.dtype),
            pltpu.SemaphoreType.DMA((n,)), pltpu.SemaphoreType.DMA((n,)),
            pltpu.SemaphoreType.DMA((7,)), pltpu.SemaphoreType.DMA((7,)),
            pltpu.SemaphoreType.DMA((2 * n + 1,))],
        compiler_params=pltpu.CompilerParams(vmem_limit_bytes=SHARE_VMEM_LIMIT_BYTES),
    )(*pieces, small)
    return res[:n], res[n]


BF16_SUBLANES = 16


def _row_tile(n, target, step=BF16_SUBLANES):
    return max([d for d in range(step, min(n, target) + 1, step) if n % d == 0] or [n])


def _ew(name, fn, ins, outs, block_bytes=2 << 20):
    rows, cols = ins[0].shape[-2:]
    lead = max(math.prod(a.shape[:-2]) for a in ins)
    tr = _row_tile(rows, max(8, block_bytes // (4 * cols * lead)))
    n = len(ins)

    def spec(shape):
        if len(shape) == 2:
            return pl.BlockSpec((tr, cols), lambda i: (i, 0))
        return pl.BlockSpec((shape[0], tr, cols), lambda i: (0, i, 0))

    def body(*refs):
        res = fn(*[r[...] for r in refs[:n]])
        for r, v in zip(refs[n:], res):
            r[...] = v

    return pl.pallas_call(
        body, name=name,
        out_shape=[jax.ShapeDtypeStruct(s, F32) for s in outs],
        grid=(rows // tr,),
        in_specs=[spec(a.shape) for a in ins],
        out_specs=[spec(s) for s in outs],
        compiler_params=_params(("parallel",)),
    )(*ins)


def _sum_slots(a):
    total = a[0].astype(F32)
    for s in range(1, a.shape[0]):
        total = total + a[s].astype(F32)
    return (total,)


def _adamw(g, w, m, v):
    bc1 = 1.0 - ADAM_B1 ** ADAM_STEP
    bc2 = 1.0 - ADAM_B2 ** ADAM_STEP
    m_new = ADAM_B1 * m + (1.0 - ADAM_B1) * g
    v_new = ADAM_B2 * v + (1.0 - ADAM_B2) * jnp.square(g)
    delta = -ADAM_LR * ((m_new / bc1) / (jnp.sqrt(v_new / bc2) + ADAM_EPS) + ADAM_WD * w)
    return delta, m_new, v_new


def _mat(a):
    return a.reshape(a.shape[-2:])


def _to_shard_major(full, axis):
    rows, cols = full.shape
    if axis == 0:
        return full.reshape(4, rows // 4, cols)
    return full.reshape(rows, 4, cols // 4).transpose(1, 0, 2)


def _from_shard_major(a, axis):
    _, r, cs = a.shape
    if axis == 0:
        return a.reshape(4 * r, cs)
    return a.transpose(1, 0, 2).reshape(r, 4 * cs)


def _pack_small(arrays, tail=None):
    flat = [arrays[n].reshape(-1) for n in SMALL_NAMES] + ([] if tail is None else [tail.reshape(1)])
    used = sum(a.shape[0] for a in flat)
    flat.append(jnp.zeros((8 * SMALL_ROWS * PACK_COLS - used,), F32))
    return jnp.concatenate(flat).reshape(8, SMALL_ROWS, PACK_COLS)


def _unpack_small(packed, shapes):
    flat = packed.reshape(-1)
    out, off = {}, 0
    for n in SMALL_NAMES:
        size = math.prod(shapes[n])
        out[n] = flat[off:off + size].reshape(shapes[n])
        off += size
    return out


def _row(a):
    return a.reshape(1, -1)


def _local_step(x, p, target, wf, ws, late_shards):
    wf = dict(wf)
    t = x.shape[0]
    tm = min(256, t)
    tw = min(512, t)
    tx = min(1024, t)
    g = {}

    lam_re, lam_im = ws['s5_lam_re'].reshape(S5_GROUPS, S5_STATE), ws['s5_lam_im'].reshape(S5_GROUPS, S5_STATE)
    log_step = ws['s5_log_step'].reshape(S5_GROUPS, 1)
    gp = (S5_GROUPS, S5_STATE)
    lam_ins = (lam_re, lam_im, log_step)
    lbr, lbi, cfr, cfi = _small_fwd("s5_lam", _f_s5_lam, lam_ins, [(gp, F32)] * 4)
    lam_row = jnp.concatenate([_row(lbr), _row(lbi)], axis=1)
    to_t = lambda a, perm: a.reshape((S5_GROUPS,) + a.shape[-2:]).transpose(perm).reshape(S5_GROUP, S5_LANES)
    build_ins = (_row(cfr), _row(cfi), to_t(ws['s5_b_re'], (2, 0, 1)), to_t(ws['s5_b_im'], (2, 0, 1)),
                 to_t(ws['s5_c_re'], (1, 0, 2)), to_t(ws['s5_c_im'], (1, 0, 2)))
    block_shape = (S5_WIDTH, 2 * S5_TILE_LANES)
    b_blk, c_blk = _small_fwd("s5_build", _f_s5_build, build_ins, [(block_shape, F32)] * 2)

    norm_mix, norm_ffn, norm_ple = _row(ws['norm_mix']), _row(ws['norm_ffn']), _row(ws['norm_ple'])
    final_norm = _row(ws['final_norm'])
    (xn,) = _tok_fwd("norm_in", _f_norm_in, [x], [norm_mix], [(x.shape[1], BF16)], tx)
    u = _mm("proj_s5", xn, wf['w_in'][:, :S5_WIDTH], 'nn')
    z = _mm("proj_rw", xn, wf['w_in'][:, S5_WIDTH:], 'nn')

    bu = _s5_expand("s5_bu", u, b_blk)
    def late_plan(carrier):
        return _gather_plan([late_shards[n] for n in LATE_GATHER[carrier]])

    def arrived(carrier, got):
        wf.update({n: _from_shard_major(a, SHARDED[n]) for n, a in zip(LATE_GATHER[carrier], got)})

    xs, got = _s5_scan(bu, lam_row, tm, late_plan('s5_scan'))
    arrived('s5_scan', got)
    ypre = _s5_contract("s5_y", xs, c_blk)
    s5_par = [_row(ws['s5_d']), wf['s5_glu_w'], _row(ws['s5_glu_b'])]
    (s5_out,) = _tok_fwd("s5_post", _f_s5_post, [ypre, u], s5_par, [(S5_WIDTH, BF16)], tx)

    pre_par = [_row(ws['rw_shift_mu']), _row(ws['rw_w0']), wf['rw_w2'], _row(ws['rw_a0']), wf['rw_a2'],
               wf['rw_g2'], _row(ws['rw_k_k']), _row(ws['rw_k_a'])]
    r, lw, kp, v, an, bn, gate = _rw_pre_fwd(z, pre_par, tw)
    seqs = [r, lw, kp, v, an, bn]
    y_wkv, ck, got = _wkv_fwd(seqs, late_plan('wkv_fwd'))
    arrived('wkv_fwd', got)
    post_par = [_row(ws['rw_ln_w']), _row(ws['rw_ln_b']), _row(ws['rw_r_k'])]
    post_toks = [y_wkv, r, kp, v, gate]
    (rw_out,) = _tok_fwd("rw_post", _f_rw_post, post_toks, post_par, [(RWKV_WIDTH, BF16)], tx)

    mixcat = jnp.concatenate([s5_out, rw_out], axis=1)
    mixed = _mm("mix_out", mixcat, wf['w_out'], 'nn')
    h1, hn = _tok_fwd("mix_res", _f_mix_res, [x, mixed], [norm_ffn], [(x.shape[1], F32), (x.shape[1], BF16)], tx)
    w13 = jnp.concatenate([wf['ffn_w1'], wf['ffn_w3']], axis=1)
    a13, got = _mm("ffn_up", hn, w13, 'nn', out_dtype=BF16, tn=FFN_TILE, plan=late_plan('ffn_up'))
    arrived('ffn_up', got)
    (f,) = _tok_fwd("ffn_act", _f_ffn_act, [a13], [], [(FFN_HIDDEN, BF16)], tw)
    ffo = _mm("ffn_down", f, wf['ffn_w2'], 'nn')
    h2, hp = _tok_fwd("ffn_res", _f_ffn_res, [h1, ffo], [norm_ple], [(x.shape[1], F32), (x.shape[1], BF16)], tx)
    gpre = _mm("ple_gate", hp, wf['ple_gate_w'], 'nn')
    pu = _mm("ple_up", p, wf['ple_up_w'], 'nn')

    dh2, dgpre, dpu, g['final_norm'], loss = _tok_bwd(
        "loss", _f_loss, [h2, gpre, pu, target], [final_norm], [None],
        [F32, BF16, BF16, None], [True], tw, acc_out=0)
    g['ple_gate_w'] = _mm("d_ple_gate_w", hp, dgpre, 'tn', out_dtype=WIRE)
    g['ple_up_w'] = _mm("d_ple_up_w", p, dpu, 'tn', out_dtype=WIRE)
    dhp = _mm("d_hp", dgpre, wf['ple_gate_w'], 'nt')
    dh1, dffo, g['norm_ple'] = _tok_bwd("ffn_res_bwd", _f_ffn_res, [h1, ffo], [norm_ple], [dh2, dhp],
                                        [F32, BF16], [True], tw)
    g['ffn_w2'] = _mm("d_ffn_w2", f, dffo, 'tn', out_dtype=WIRE, tm=FFN_TILE)
    df = _mm("d_f", dffo, wf['ffn_w2'], 'nt', out_dtype=BF16, tn=FFN_TILE)
    (da13,) = _tok_bwd("ffn_act_bwd", _f_ffn_act, [a13], [], [df], [BF16], [], tw)
    dw13 = _mm("d_ffn_w13", hn, da13, 'tn', out_dtype=WIRE, tn=FFN_TILE)
    dw13 = dw13.reshape(dw13.shape[0], 8, FFN_HIDDEN // 4).transpose(1, 0, 2)
    shard_major = {'ffn_w1': dw13[:4], 'ffn_w3': dw13[4:]}
    dhn = _mm("d_hn", da13, w13, 'nt')
    dx_a, dmixed, g['norm_ffn'] = _tok_bwd("mix_res_bwd", _f_mix_res, [x, mixed], [norm_ffn], [dh1, dhn],
                                           [F32, BF16], [True], tw)
    g['w_out'] = _mm("d_w_out", mixcat, dmixed, 'tn', out_dtype=WIRE)
    ds5_out = _mm("d_s5_out", dmixed, wf['w_out'][:S5_WIDTH], 'nt')
    drw_out = _mm("d_rw_out", dmixed, wf['w_out'][S5_WIDTH:], 'nt')

    dy_wkv, dr_b, dkp_b, dv_b, dgate, g['rw_ln_w'], g['rw_ln_b'], g['rw_r_k'] = _tok_bwd(
        "rw_post_bwd", _f_rw_post, post_toks, post_par, [drw_out], [F32] * 5, [True] * 3, tw)
    late_exchange = _exchange_plan([shard_major[n] if n in shard_major else
                                    _to_shard_major(g[n], SHARDED[n]).astype(WIRE) for n in LATE_NAMES])
    dseqs, late_pieces = _wkv_bwd(seqs, ck, dy_wkv, late_exchange)
    pre_cots = [(dseqs[0], dr_b), (dseqs[1],), (dseqs[2], dkp_b), (dseqs[3], dv_b), (dseqs[4],), (dseqs[5],),
                (dgate,)]
    dz, *dpre = _rw_pre_bwd(z, pre_par, pre_cots, tw)
    for n, d in zip(['rw_shift_mu', 'rw_w0', 'rw_w2', 'rw_a0', 'rw_a2', 'rw_g2', 'rw_k_k', 'rw_k_a'], dpre):
        g[n] = d

    dypre, du_a, g['s5_d'], g['s5_glu_w'], g['s5_glu_b'] = _tok_bwd(
        "s5_post_bwd", _f_s5_post, [ypre, u], s5_par, [ds5_out], [F32, F32], [True] * 3, tw)
    dxs = _s5_expand("d_s5_x", dypre, c_blk)
    dc_blk = _s5_block_grad("d_s5_c", dypre, xs)
    dbu, dlam_row = _s5_scan_bwd(dxs, xs, lam_row, tm)
    du = _s5_contract("d_s5_u", dbu, b_blk, add=du_a, out_dtype=BF16)
    db_blk = _s5_block_grad("d_s5_b", u, dbu)
    dbuild = _small_bwd("s5_build_bwd", _f_s5_build, build_ins, (db_blk, dc_blk))
    lam_cots = (dlam_row[:, :S5_LANES].reshape(gp), dlam_row[:, S5_LANES:].reshape(gp),
                dbuild[0].reshape(gp), dbuild[1].reshape(gp))
    g['s5_lam_re'], g['s5_lam_im'], g['s5_log_step'] = _small_bwd("s5_lam_bwd", _f_s5_lam, lam_ins, lam_cots)
    from_t = lambda a, perm: a.reshape(S5_GROUP, S5_GROUPS, S5_STATE).transpose(perm)
    g['s5_b_re'], g['s5_b_im'] = from_t(dbuild[2], (1, 2, 0)), from_t(dbuild[3], (1, 2, 0))
    g['s5_c_re'], g['s5_c_im'] = from_t(dbuild[4], (1, 0, 2)), from_t(dbuild[5], (1, 0, 2))

    dproj = jnp.concatenate([du, dz], axis=1)
    dw_in = _mm("d_w_in", xn, dproj, 'tn', out_dtype=WIRE)
    w_in_exchange = _exchange_plan([_to_shard_major(dw_in, SHARDED['w_in'])])
    dxn, (w_in_pieces,) = _mm("d_xn", dproj, wf['w_in'], 'nt', plan=w_in_exchange)
    pieces = dict(zip(LATE_NAMES, late_pieces), w_in=w_in_pieces)
    grad_x, g['norm_mix'] = _tok_bwd("norm_in_bwd", _f_norm_in, [x], [norm_mix], [dxn], [F32], [True], tw,
                                     add_to=(0, dx_a))
    return loss[0, 0], grad_x, g, pieces


def _step(x, p, target, w, m, v):
    shards = {n: _mat(w[n]).astype(BF16) for n in SHARDED_NAMES}
    early = _run_plan("gather_early", _gather_plan([shards[n] for n in EARLY_NAMES]))
    wf = {n: _from_shard_major(a, SHARDED[n]) for n, a in zip(EARLY_NAMES, early)}
    ws = {n: w[n] for n in SMALL_NAMES}

    loss, grad_x, g, pieces = _local_step(x[0], p[0, 0], target[0], wf, ws, shards)

    last_names = [n for n in SHARDED_NAMES if n not in pieces]
    last_plan = _exchange_plan([_to_shard_major(g[n], SHARDED[n]).astype(WIRE) for n in last_names],
                               _pack_small({n: g[n] for n in SMALL_NAMES}, tail=loss))
    *last_pieces, by_dev = _run_plan("exchange_last", last_plan)
    pieces.update(zip(last_names, last_pieces))
    both, small_g = _share_cores([pieces[n] for n in SHARDED_NAMES], by_dev)

    kinds = [{}, {}, {}, {}]
    for n, gn in zip(SHARDED_NAMES, both):
        shard = _mat(w[n]).shape
        res = _ew("adamw_" + n, _adamw, [gn.reshape(shard), _mat(w[n]), _mat(m[n]), _mat(v[n])], [shard] * 3)
        for kind, a in zip(kinds, [gn] + list(res)):
            kind[n] = a.reshape(w[n].shape)
    flat = (8 * SMALL_ROWS, PACK_COLS)
    packed = [_pack_small({n: d[n] for n in SMALL_NAMES}).reshape(flat) for d in (w, m, v)]
    small_res = _ew("adamw_small", _adamw, [small_g.reshape(flat)] + packed, [flat] * 3)
    small_shapes = {n: w[n].shape for n in SMALL_NAMES}
    for kind, a in zip(kinds, [small_g] + list(small_res)):
        kind.update(_unpack_small(a, small_shapes))
    total = small_g.reshape(-1)[sum(math.prod(s) for s in small_shapes.values())]
    return (total, grad_x[None], *[kind[n] for kind in kinds for n in WEIGHT_NAMES])


def kernel(x, p, norm_mix, w_in, s5_lam_re, s5_lam_im, s5_log_step, s5_b_re, s5_b_im, s5_c_re, s5_c_im, s5_d, s5_glu_w, s5_glu_b, rw_shift_mu, rw_w0, rw_w2, rw_a0, rw_a2, rw_g2, rw_k_k, rw_k_a, rw_r_k, rw_ln_w, rw_ln_b, w_out, norm_ffn, ffn_w1, ffn_w3, ffn_w2, norm_ple, ple_gate_w, ple_up_w, final_norm, loss_target, m_norm_mix, m_w_in, m_s5_lam_re, m_s5_lam_im, m_s5_log_step, m_s5_b_re, m_s5_b_im, m_s5_c_re, m_s5_c_im, m_s5_d, m_s5_glu_w, m_s5_glu_b, m_rw_shift_mu, m_rw_w0, m_rw_w2, m_rw_a0, m_rw_a2, m_rw_g2, m_rw_k_k, m_rw_k_a, m_rw_r_k, m_rw_ln_w, m_rw_ln_b, m_w_out, m_norm_ffn, m_ffn_w1, m_ffn_w3, m_ffn_w2, m_norm_ple, m_ple_gate_w, m_ple_up_w, m_final_norm, v_norm_mix, v_w_in, v_s5_lam_re, v_s5_lam_im, v_s5_log_step, v_s5_b_re, v_s5_b_im, v_s5_c_re, v_s5_c_im, v_s5_d, v_s5_glu_w, v_s5_glu_b, v_rw_shift_mu, v_rw_w0, v_rw_w2, v_rw_a0, v_rw_a2, v_rw_g2, v_rw_k_k, v_rw_k_a, v_rw_r_k, v_rw_ln_w, v_rw_ln_b, v_w_out, v_norm_ffn, v_ffn_w1, v_ffn_w3, v_ffn_w2, v_norm_ple, v_ple_gate_w, v_ple_up_w, v_final_norm):
    args = dict(locals())
    w = {n: args[n] for n in WEIGHT_NAMES}
    m = {n: args["m_" + n] for n in WEIGHT_NAMES}
    v = {n: args["v_" + n] for n in WEIGHT_NAMES}
    return _step(x, p, loss_target, w, m, v)
```
